```python
import jax, jax.numpy as jnp
from jax import lax
import numpy as np

D_MODEL = 1024
BATCH = 8
SEQ = 4096
DEPTH = 4

MEM_LEN = 256
HEAD_DIM = 64
EPS = 1e-6
GM_WIDTH = D_MODEL // 4
GM_GROUPS = GM_WIDTH // HEAD_DIM
CHUNK = 128
POOL_WIDTH = D_MODEL // 4
POOL_WINDOWS = (2, 4, 8, 16)
POOL_GROUPS = len(POOL_WINDOWS)
POOL_GROUP_DIM = POOL_WIDTH // POOL_GROUPS
ATT_WIDTH = D_MODEL // 2
ATT_Q_HEADS = ATT_WIDTH // HEAD_DIM
ATT_KV_HEADS = ATT_Q_HEADS // 4
ATT_GROUP = ATT_Q_HEADS // ATT_KV_HEADS
WINDOW = 128
ROPE_THETA = 10000.0
D_MIX = GM_WIDTH + POOL_WIDTH + ATT_WIDTH
IN_SIZES = (2 * GM_WIDTH, POOL_WIDTH, ATT_Q_HEADS * HEAD_DIM, ATT_KV_HEADS * HEAD_DIM, ATT_KV_HEADS * HEAD_DIM)
IN_SPLITS = tuple(int(s) for s in np.cumsum(IN_SIZES)[:-1])
D_IN = sum(IN_SIZES)
X_HEADS = 4
X_HEAD_DIM = D_MODEL // X_HEADS
D_FF = -(-8 * D_MODEL // (3 * 256)) * 256

kernel_name = "hybrid_gmlp_pool_swa_sink_trunk"


def rms_norm(x, g):
    xf = x.astype(jnp.float32)
    y = xf * lax.rsqrt(jnp.mean(xf * xf, axis=-1, keepdims=True) + EPS)
    return (y * g.astype(jnp.float32)).astype(x.dtype)


def spatial_gating(uv, v_gain, w_s, b_s):
    B, S, _ = uv.shape
    u, v = jnp.split(uv, 2, axis=-1)
    v = v.reshape(B, S // CHUNK, CHUNK, GM_GROUPS, HEAD_DIM)
    v = rms_norm(v, v_gain.reshape(GM_GROUPS, HEAD_DIM))
    causal = jnp.tril(jnp.ones((CHUNK, CHUNK), dtype=bool))
    w = jnp.where(causal[None], w_s, jnp.zeros_like(w_s))
    mixed = jnp.einsum('gts,bcsgd->bctgd', w, v) + b_s.T[None, None, :, :, None]
    return u * mixed.reshape(B, S, GM_WIDTH)


def multiscale_pool(p, pool_w, pool_scale):
    B, S, _ = p.shape
    pf = p.astype(jnp.float32)
    cs = jnp.pad(jnp.cumsum(pf, axis=1), ((0, 0), (1, 0), (0, 0)))
    t = jnp.arange(S)
    outs = []
    for g, w in enumerate(POOL_WINDOWS):
        sl = slice(g * POOL_GROUP_DIM, (g + 1) * POOL_GROUP_DIM)
        hi = cs[:, 1:, sl]
        lo = jnp.pad(cs[:, :S - w + 1, sl], ((0, 0), (w - 1, 0), (0, 0)))
        count = jnp.minimum(t + 1, w).astype(jnp.float32)[None, :, None]
        outs.append((hi - lo) / count - pf[:, :, sl])
    pooled = jnp.stack(outs, axis=2).astype(p.dtype)
    mapped = jnp.einsum('bsgc,gcd->bsgd', pooled, pool_w).reshape(B, S, POOL_WIDTH)
    return mapped * pool_scale


def rope(x, positions):
    half = HEAD_DIM // 2
    inv = ROPE_THETA ** (-jnp.arange(half, dtype=jnp.float32) / half)
    ang = positions.astype(jnp.float32)[..., None] * inv
    cos = jnp.cos(ang)[:, :, None, :]
    sin = jnp.sin(ang)[:, :, None, :]
    xf = x.astype(jnp.float32)
    x1, x2 = xf[..., :half], xf[..., half:]
    return jnp.concatenate([x1 * cos - x2 * sin, x2 * cos + x1 * sin], axis=-1).astype(x.dtype)


def sliding_window_attention(q, k, v, sinks):
    B, S, _, _ = q.shape
    NB = S // WINDOW
    qb = q.reshape(B, NB, WINDOW, ATT_KV_HEADS, ATT_GROUP, HEAD_DIM)

    def band(t_):
        tb = t_.reshape(B, NB, WINDOW, ATT_KV_HEADS, HEAD_DIM)
        prev = jnp.pad(tb[:, :-1], ((0, 0), (1, 0), (0, 0), (0, 0), (0, 0)))
        return jnp.concatenate([prev, tb], axis=2)

    kb, vb = band(k), band(v)
    scores = jnp.einsum('bnqhgd,bnkhd->bnhgqk', qb, kb,
                        preferred_element_type=jnp.float32) * (HEAD_DIM ** -0.5)
    qi = jnp.arange(WINDOW)[:, None] + WINDOW
    ki = jnp.arange(2 * WINDOW)[None, :]
    rel = qi - ki
    valid = (rel >= 0) & (rel < WINDOW)
    valid = valid[None] & ((jnp.arange(NB) > 0)[:, None, None] | (ki >= WINDOW)[None])
    scores = jnp.where(valid[None, :, None, None], scores, -jnp.inf)
    sink = jnp.broadcast_to(
        sinks.astype(jnp.float32).reshape(ATT_KV_HEADS, ATT_GROUP)[None, None, :, :, None, None],
        scores.shape[:-1] + (1,))
    probs = jax.nn.softmax(jnp.concatenate([scores, sink], axis=-1), axis=-1)[..., :-1]
    out = jnp.einsum('bnhgqk,bnkhd->bnqhgd', probs.astype(v.dtype), vb)
    return out.reshape(B, S, ATT_WIDTH)


def cross_attention(h, mem_n, w_xq, w_xkv, w_xo):
    B, S, _ = h.shape
    M = mem_n.shape[1]
    q = (h @ w_xq).reshape(B, S, X_HEADS, X_HEAD_DIM)
    k, v = jnp.split(mem_n @ w_xkv, 2, axis=-1)
    k = k.reshape(B, M, X_HEADS, X_HEAD_DIM)
    v = v.reshape(B, M, X_HEADS, X_HEAD_DIM)
    s = jnp.einsum('bshd,bmhd->bhsm', q, k, preferred_element_type=jnp.float32) * (X_HEAD_DIM ** -0.5)
    p = jax.nn.softmax(s, axis=-1).astype(v.dtype)
    o = jnp.einsum('bhsm,bmhd->bshd', p, v).reshape(B, S, D_MODEL)
    return o @ w_xo


def _fwd_setup_inputs(seed: int = 0) -> dict:
    key = jax.random.key(seed)
    ks = jax.random.split(key, 32)
    f32 = jnp.float32

    def nrm(k, shape, scale):
        return jax.random.normal(k, shape, f32) * scale

    def gain(k, shape):
        return 1.0 + 0.05 * jax.random.normal(k, shape, f32)

    offsets = jax.random.randint(ks[2], (BATCH, 1), 0, 1024, dtype=jnp.int32)
    positions = offsets + jnp.arange(SEQ, dtype=jnp.int32)[None, :]
    return {
        "x": nrm(ks[0], (BATCH, SEQ, D_MODEL), 1.0),
        "mem": nrm(ks[1], (BATCH, MEM_LEN, D_MODEL), 1.0),
        "positions": positions,
        "mem_norm_g": gain(ks[3], (D_MODEL,)),
        "mix_pre_g": gain(ks[4], (DEPTH, D_MODEL)),
        "mix_post_g": gain(ks[5], (DEPTH, D_MODEL)),
        "w_in": nrm(ks[6], (DEPTH, D_MODEL, D_IN), D_MODEL ** -0.5),
        "gm_v_g": gain(ks[7], (DEPTH, GM_WIDTH)),
        "gm_w_s": nrm(ks[8], (DEPTH, GM_GROUPS, CHUNK, CHUNK), 0.5 * CHUNK ** -0.5),
        "gm_b_s": 1.0 + 0.1 * jax.random.normal(ks[9], (DEPTH, GM_GROUPS, CHUNK), f32),
        "pool_w": nrm(ks[10], (DEPTH, POOL_GROUPS, POOL_GROUP_DIM, POOL_GROUP_DIM), POOL_GROUP_DIM ** -0.5),
        "pool_scale": 1.0 + 0.1 * jax.random.normal(ks[11], (DEPTH, POOL_WIDTH), f32),
        "attn_sinks": nrm(ks[12], (DEPTH, ATT_Q_HEADS), 1.0),
        "w_o": nrm(ks[13], (DEPTH, D_MIX, D_MODEL), D_MIX ** -0.5),
        "x_pre_g": gain(ks[14], (DEPTH, D_MODEL)),
        "x_post_g": gain(ks[15], (DEPTH, D_MODEL)),
        "w_xq": nrm(ks[16], (DEPTH, D_MODEL, D_MODEL), D_MODEL ** -0.5),
        "w_xkv": nrm(ks[17], (DEPTH, D_MODEL, 2 * D_MODEL), D_MODEL ** -0.5),
        "w_xo": nrm(ks[18], (DEPTH, D_MODEL, D_MODEL), D_MODEL ** -0.5),
        "ffn_pre_g": gain(ks[19], (DEPTH, D_MODEL)),
        "ffn_post_g": gain(ks[20], (DEPTH, D_MODEL)),
        "w_gate_up": nrm(ks[21], (DEPTH, D_MODEL, 2 * D_FF), D_MODEL ** -0.5),
        "w_down": nrm(ks[22], (DEPTH, D_FF, D_MODEL), D_FF ** -0.5),
    }


def _fwd_reference(x, mem, positions, mem_norm_g, mix_pre_g, mix_post_g, w_in, gm_v_g, gm_w_s, gm_b_s,
              pool_w, pool_scale, attn_sinks, w_o, x_pre_g, x_post_g, w_xq, w_xkv, w_xo,
              ffn_pre_g, ffn_post_g, w_gate_up, w_down):
    B, S, _ = x.shape
    mem_n = rms_norm(mem, mem_norm_g)
    for l in range(DEPTH):
        h = rms_norm(x, mix_pre_g[l])
        z = h @ w_in[l]
        z_gm, z_pool, z_q, z_k, z_v = jnp.split(z, IN_SPLITS, axis=-1)
        a = spatial_gating(jax.nn.gelu(z_gm), gm_v_g[l], gm_w_s[l], gm_b_s[l])
        b = multiscale_pool(z_pool, pool_w[l], pool_scale[l])
        q = rope(z_q.reshape(B, S, ATT_Q_HEADS, HEAD_DIM), positions)
        k = rope(z_k.reshape(B, S, ATT_KV_HEADS, HEAD_DIM), positions)
        v = z_v.reshape(B, S, ATT_KV_HEADS, HEAD_DIM)
        c = sliding_window_attention(q, k, v, attn_sinks[l])
        mix = jnp.concatenate([a, b, c], axis=-1) @ w_o[l]
        x = x + rms_norm(mix, mix_post_g[l])
        h = rms_norm(x, x_pre_g[l])
        x = x + rms_norm(cross_attention(h, mem_n, w_xq[l], w_xkv[l], w_xo[l]), x_post_g[l])
        h = rms_norm(x, ffn_pre_g[l])
        gate, up = jnp.split(h @ w_gate_up[l], 2, axis=-1)
        f = (jax.nn.silu(gate) * up) @ w_down[l]
        x = x + rms_norm(f, ffn_post_g[l])
    return x


import jax as _jax
import jax.numpy as _jnp

TWIN_FORMAT = 'train_step'
FWD_PARAMS = ['x', 'mem', 'positions', 'mem_norm_g', 'mix_pre_g', 'mix_post_g', 'w_in', 'gm_v_g', 'gm_w_s', 'gm_b_s', 'pool_w', 'pool_scale', 'attn_sinks', 'w_o', 'x_pre_g', 'x_post_g', 'w_xq', 'w_xkv', 'w_xo', 'ffn_pre_g', 'ffn_post_g', 'w_gate_up', 'w_down']
TWIN_WEIGHTS = ['mem_norm_g', 'mix_pre_g', 'mix_post_g', 'w_in', 'gm_v_g', 'gm_w_s', 'gm_b_s', 'pool_w', 'pool_scale', 'attn_sinks', 'w_o', 'x_pre_g', 'x_post_g', 'w_xq', 'w_xkv', 'w_xo', 'ffn_pre_g', 'ffn_post_g', 'w_gate_up', 'w_down']
TWIN_DIFF_INPUT = 'x'
TWIN_INPUTS = ['x', 'mem', 'positions', 'mem_norm_g', 'mix_pre_g', 'mix_post_g', 'w_in', 'gm_v_g', 'gm_w_s', 'gm_b_s', 'pool_w', 'pool_scale', 'attn_sinks', 'w_o', 'x_pre_g', 'x_post_g', 'w_xq', 'w_xkv', 'w_xo', 'ffn_pre_g', 'ffn_post_g', 'w_gate_up', 'w_down', 'loss_target', 'm_mem_norm_g', 'm_mix_pre_g', 'm_mix_post_g', 'm_w_in', 'm_gm_v_g', 'm_gm_w_s', 'm_gm_b_s', 'm_pool_w', 'm_pool_scale', 'm_attn_sinks', 'm_w_o', 'm_x_pre_g', 'm_x_post_g', 'm_w_xq', 'm_w_xkv', 'm_w_xo', 'm_ffn_pre_g', 'm_ffn_post_g', 'm_w_gate_up', 'm_w_down', 'v_mem_norm_g', 'v_mix_pre_g', 'v_mix_post_g', 'v_w_in', 'v_gm_v_g', 'v_gm_w_s', 'v_gm_b_s', 'v_pool_w', 'v_pool_scale', 'v_attn_sinks', 'v_w_o', 'v_x_pre_g', 'v_x_post_g', 'v_w_xq', 'v_w_xkv', 'v_w_xo', 'v_ffn_pre_g', 'v_ffn_post_g', 'v_w_gate_up', 'v_w_down']
TWIN_OUTPUTS = ['loss', 'grad_x', 'grad_mem_norm_g', 'grad_mix_pre_g', 'grad_mix_post_g', 'grad_w_in', 'grad_gm_v_g', 'grad_gm_w_s', 'grad_gm_b_s', 'grad_pool_w', 'grad_pool_scale', 'grad_attn_sinks', 'grad_w_o', 'grad_x_pre_g', 'grad_x_post_g', 'grad_w_xq', 'grad_w_xkv', 'grad_w_xo', 'grad_ffn_pre_g', 'grad_ffn_post_g', 'grad_w_gate_up', 'grad_w_down', 'delta_mem_norm_g', 'delta_mix_pre_g', 'delta_mix_post_g', 'delta_w_in', 'delta_gm_v_g', 'delta_gm_w_s', 'delta_gm_b_s', 'delta_pool_w', 'delta_pool_scale', 'delta_attn_sinks', 'delta_w_o', 'delta_x_pre_g', 'delta_x_post_g', 'delta_w_xq', 'delta_w_xkv', 'delta_w_xo', 'delta_ffn_pre_g', 'delta_ffn_post_g', 'delta_w_gate_up', 'delta_w_down', 'new_m_mem_norm_g', 'new_m_mix_pre_g', 'new_m_mix_post_g', 'new_m_w_in', 'new_m_gm_v_g', 'new_m_gm_w_s', 'new_m_gm_b_s', 'new_m_pool_w', 'new_m_pool_scale', 'new_m_attn_sinks', 'new_m_w_o', 'new_m_x_pre_g', 'new_m_x_post_g', 'new_m_w_xq', 'new_m_w_xkv', 'new_m_w_xo', 'new_m_ffn_pre_g', 'new_m_ffn_post_g', 'new_m_w_gate_up', 'new_m_w_down', 'new_v_mem_norm_g', 'new_v_mix_pre_g', 'new_v_mix_post_g', 'new_v_w_in', 'new_v_gm_v_g', 'new_v_gm_w_s', 'new_v_gm_b_s', 'new_v_pool_w', 'new_v_pool_scale', 'new_v_attn_sinks', 'new_v_w_o', 'new_v_x_pre_g', 'new_v_x_post_g', 'new_v_w_xq', 'new_v_w_xkv', 'new_v_w_xo', 'new_v_ffn_pre_g', 'new_v_ffn_post_g', 'new_v_w_gate_up', 'new_v_w_down']
TWIN_LEAF_KINDS = {'loss': 'loss', 'grad_x': 'grad_x', 'grad_mem_norm_g': 'grad_w', 'grad_mix_pre_g': 'grad_w', 'grad_mix_post_g': 'grad_w', 'grad_w_in': 'grad_w', 'grad_gm_v_g': 'grad_w', 'grad_gm_w_s': 'grad_w', 'grad_gm_b_s': 'grad_w', 'grad_pool_w': 'grad_w', 'grad_pool_scale': 'grad_w', 'grad_attn_sinks': 'grad_w', 'grad_w_o': 'grad_w', 'grad_x_pre_g': 'grad_w', 'grad_x_post_g': 'grad_w', 'grad_w_xq': 'grad_w', 'grad_w_xkv': 'grad_w', 'grad_w_xo': 'grad_w', 'grad_ffn_pre_g': 'grad_w', 'grad_ffn_post_g': 'grad_w', 'grad_w_gate_up': 'grad_w', 'grad_w_down': 'grad_w', 'delta_mem_norm_g': 'delta_w', 'delta_mix_pre_g': 'delta_w', 'delta_mix_post_g': 'delta_w', 'delta_w_in': 'delta_w', 'delta_gm_v_g': 'delta_w', 'delta_gm_w_s': 'delta_w', 'delta_gm_b_s': 'delta_w', 'delta_pool_w': 'delta_w', 'delta_pool_scale': 'delta_w', 'delta_attn_sinks': 'delta_w', 'delta_w_o': 'delta_w', 'delta_x_pre_g': 'delta_w', 'delta_x_post_g': 'delta_w', 'delta_w_xq': 'delta_w', 'delta_w_xkv': 'delta_w', 'delta_w_xo': 'delta_w', 'delta_ffn_pre_g': 'delta_w', 'delta_ffn_post_g': 'delta_w', 'delta_w_gate_up': 'delta_w', 'delta_w_down': 'delta_w', 'new_m_mem_norm_g': 'new_m', 'new_m_mix_pre_g': 'new_m', 'new_m_mix_post_g': 'new_m', 'new_m_w_in': 'new_m', 'new_m_gm_v_g': 'new_m', 'new_m_gm_w_s': 'new_m', 'new_m_gm_b_s': 'new_m', 'new_m_pool_w': 'new_m', 'new_m_pool_scale': 'new_m', 'new_m_attn_sinks': 'new_m', 'new_m_w_o': 'new_m', 'new_m_x_pre_g': 'new_m', 'new_m_x_post_g': 'new_m', 'new_m_w_xq': 'new_m', 'new_m_w_xkv': 'new_m', 'new_m_w_xo': 'new_m', 'new_m_ffn_pre_g': 'new_m', 'new_m_ffn_post_g': 'new_m', 'new_m_w_gate_up': 'new_m', 'new_m_w_down': 'new_m', 'new_v_mem_norm_g': 'new_v', 'new_v_mix_pre_g': 'new_v', 'new_v_mix_post_g': 'new_v', 'new_v_w_in': 'new_v', 'new_v_gm_v_g': 'new_v', 'new_v_gm_w_s': 'new_v', 'new_v_gm_b_s': 'new_v', 'new_v_pool_w': 'new_v', 'new_v_pool_scale': 'new_v', 'new_v_attn_sinks': 'new_v', 'new_v_w_o': 'new_v', 'new_v_x_pre_g': 'new_v', 'new_v_x_post_g': 'new_v', 'new_v_w_xq': 'new_v', 'new_v_w_xkv': 'new_v', 'new_v_w_xo': 'new_v', 'new_v_ffn_pre_g': 'new_v', 'new_v_ffn_post_g': 'new_v', 'new_v_w_gate_up': 'new_v', 'new_v_w_down': 'new_v'}


def _forward(args):
    return _fwd_reference(*[args[k] for k in FWD_PARAMS])


def _output_shape():
    out = _jax.eval_shape(lambda: _forward(_fwd_setup_inputs(0)))
    return out.shape, out.dtype

N_MICROBATCH = 1
ADAM_LR = 0.001
ADAM_B1 = 0.9
ADAM_B2 = 0.999
ADAM_EPS = 1e-08
ADAM_WD = 0.01
ADAM_STEP = 10
PER_EXAMPLE_BATCH_AXIS = {'x': 0, 'mem': 0, 'positions': 0, 'loss_target': 0}
SHARED_INPUTS = []
_WEIGHT_DTYPES = {'mem_norm_g': _jnp.float32, 'mix_pre_g': _jnp.float32, 'mix_post_g': _jnp.float32, 'w_in': _jnp.float32, 'gm_v_g': _jnp.float32, 'gm_w_s': _jnp.float32, 'gm_b_s': _jnp.float32, 'pool_w': _jnp.float32, 'pool_scale': _jnp.float32, 'attn_sinks': _jnp.float32, 'w_o': _jnp.float32, 'x_pre_g': _jnp.float32, 'x_post_g': _jnp.float32, 'w_xq': _jnp.float32, 'w_xkv': _jnp.float32, 'w_xo': _jnp.float32, 'ffn_pre_g': _jnp.float32, 'ffn_post_g': _jnp.float32, 'w_gate_up': _jnp.float32, 'w_down': _jnp.float32}
MOMENT_SCALE = {'mem_norm_g': 3.940520e+01, 'mix_pre_g': 9.424588e+00, 'mix_post_g': 3.483714e+01, 'w_in': 8.131345e+00, 'gm_v_g': 5.988946e-01, 'gm_w_s': 7.983212e-01, 'gm_b_s': 1.157927e+00, 'pool_w': 3.925209e+00, 'pool_scale': 4.741486e+00, 'attn_sinks': 8.681859e-01, 'w_o': 1.359225e+01, 'x_pre_g': 5.954501e+00, 'x_post_g': 3.769192e+01, 'w_xq': 5.845645e+00, 'w_xkv': 1.381565e+01, 'w_xo': 1.900777e+01, 'ffn_pre_g': 7.335269e+00, 'ffn_post_g': 3.224893e+01, 'w_gate_up': 2.967658e+00, 'w_down': 5.734798e+00}


def _to_microbatches(a, axis):
    t = _jnp.moveaxis(a, axis, 0)
    t = t.reshape((N_MICROBATCH, t.shape[0] // N_MICROBATCH) + t.shape[1:])
    return _jnp.moveaxis(t, 1, axis + 1)


def setup_inputs(seed: int = 0) -> dict:
    inp = _fwd_setup_inputs(seed)
    key = _jax.random.fold_in(_jax.random.key(seed), 7919)
    shape, _ = _output_shape()
    out = dict(inp)
    out["loss_target"] = _jax.random.normal(_jax.random.fold_in(key, 0), shape, _jnp.float32)
    for i, name in enumerate(TWIN_WEIGHTS):
        w = inp[name].astype(_jnp.float32)
        if MOMENT_SCALE is None:
            s = _jnp.sqrt(_jnp.mean(_jnp.square(w)) + 1e-30)
        else:
            s = MOMENT_SCALE[name]
        km, kv = _jax.random.split(_jax.random.fold_in(key, i + 1))
        out[name] = w
        out["m_" + name] = s * _jax.random.normal(km, w.shape, _jnp.float32)
        out["v_" + name] = (s * s) * _jax.random.uniform(kv, w.shape, _jnp.float32, 0.5, 1.5)
    if N_MICROBATCH > 1:
        for name, axis in PER_EXAMPLE_BATCH_AXIS.items():
            out[name] = _to_microbatches(out[name], axis)
    return {'x': out['x'], 'mem': out['mem'], 'positions': out['positions'], 'mem_norm_g': out['mem_norm_g'], 'mix_pre_g': out['mix_pre_g'], 'mix_post_g': out['mix_post_g'], 'w_in': out['w_in'], 'gm_v_g': out['gm_v_g'], 'gm_w_s': out['gm_w_s'], 'gm_b_s': out['gm_b_s'], 'pool_w': out['pool_w'], 'pool_scale': out['pool_scale'], 'attn_sinks': out['attn_sinks'], 'w_o': out['w_o'], 'x_pre_g': out['x_pre_g'], 'x_post_g': out['x_post_g'], 'w_xq': out['w_xq'], 'w_xkv': out['w_xkv'], 'w_xo': out['w_xo'], 'ffn_pre_g': out['ffn_pre_g'], 'ffn_post_g': out['ffn_post_g'], 'w_gate_up': out['w_gate_up'], 'w_down': out['w_down'], 'loss_target': out['loss_target'], 'm_mem_norm_g': out['m_mem_norm_g'], 'm_mix_pre_g': out['m_mix_pre_g'], 'm_mix_post_g': out['m_mix_post_g'], 'm_w_in': out['m_w_in'], 'm_gm_v_g': out['m_gm_v_g'], 'm_gm_w_s': out['m_gm_w_s'], 'm_gm_b_s': out['m_gm_b_s'], 'm_pool_w': out['m_pool_w'], 'm_pool_scale': out['m_pool_scale'], 'm_attn_sinks': out['m_attn_sinks'], 'm_w_o': out['m_w_o'], 'm_x_pre_g': out['m_x_pre_g'], 'm_x_post_g': out['m_x_post_g'], 'm_w_xq': out['m_w_xq'], 'm_w_xkv': out['m_w_xkv'], 'm_w_xo': out['m_w_xo'], 'm_ffn_pre_g': out['m_ffn_pre_g'], 'm_ffn_post_g': out['m_ffn_post_g'], 'm_w_gate_up': out['m_w_gate_up'], 'm_w_down': out['m_w_down'], 'v_mem_norm_g': out['v_mem_norm_g'], 'v_mix_pre_g': out['v_mix_pre_g'], 'v_mix_post_g': out['v_mix_post_g'], 'v_w_in': out['v_w_in'], 'v_gm_v_g': out['v_gm_v_g'], 'v_gm_w_s': out['v_gm_w_s'], 'v_gm_b_s': out['v_gm_b_s'], 'v_pool_w': out['v_pool_w'], 'v_pool_scale': out['v_pool_scale'], 'v_attn_sinks': out['v_attn_sinks'], 'v_w_o': out['v_w_o'], 'v_x_pre_g': out['v_x_pre_g'], 'v_x_post_g': out['v_x_post_g'], 'v_w_xq': out['v_w_xq'], 'v_w_xkv': out['v_w_xkv'], 'v_w_xo': out['v_w_xo'], 'v_ffn_pre_g': out['v_ffn_pre_g'], 'v_ffn_post_g': out['v_ffn_post_g'], 'v_w_gate_up': out['v_w_gate_up'], 'v_w_down': out['v_w_down']}


def _loss(weights, diff, rest, loss_target):
    with _jax.named_scope("forward"):
        args = {**rest, TWIN_DIFF_INPUT: diff, **{k: w.astype(_WEIGHT_DTYPES[k]) for k, w in weights.items()}}
        y = _forward(args)
    with _jax.named_scope("loss_head"):
        err = _jnp.square(y.astype(_jnp.float32) - loss_target)
        return 0.5 * _jnp.sum(_jnp.mean(err, axis=-1)) if err.ndim else 0.5 * err


def _adamw(w, g, m, v):
    m = ADAM_B1 * m + (1.0 - ADAM_B1) * g
    v = ADAM_B2 * v + (1.0 - ADAM_B2) * _jnp.square(g)
    m_hat = m / (1.0 - ADAM_B1 ** ADAM_STEP)
    v_hat = v / (1.0 - ADAM_B2 ** ADAM_STEP)
    delta = -ADAM_LR * (m_hat / (_jnp.sqrt(v_hat) + ADAM_EPS) + ADAM_WD * w)
    return delta, m, v


def reference(x, mem, positions, mem_norm_g, mix_pre_g, mix_post_g, w_in, gm_v_g, gm_w_s, gm_b_s, pool_w, pool_scale, attn_sinks, w_o, x_pre_g, x_post_g, w_xq, w_xkv, w_xo, ffn_pre_g, ffn_post_g, w_gate_up, w_down, loss_target, m_mem_norm_g, m_mix_pre_g, m_mix_post_g, m_w_in, m_gm_v_g, m_gm_w_s, m_gm_b_s, m_pool_w, m_pool_scale, m_attn_sinks, m_w_o, m_x_pre_g, m_x_post_g, m_w_xq, m_w_xkv, m_w_xo, m_ffn_pre_g, m_ffn_post_g, m_w_gate_up, m_w_down, v_mem_norm_g, v_mix_pre_g, v_mix_post_g, v_w_in, v_gm_v_g, v_gm_w_s, v_gm_b_s, v_pool_w, v_pool_scale, v_attn_sinks, v_w_o, v_x_pre_g, v_x_post_g, v_w_xq, v_w_xkv, v_w_xo, v_ffn_pre_g, v_ffn_post_g, v_w_gate_up, v_w_down):
    given = dict(x=x, mem=mem, positions=positions, mem_norm_g=mem_norm_g, mix_pre_g=mix_pre_g, mix_post_g=mix_post_g, w_in=w_in, gm_v_g=gm_v_g, gm_w_s=gm_w_s, gm_b_s=gm_b_s, pool_w=pool_w, pool_scale=pool_scale, attn_sinks=attn_sinks, w_o=w_o, x_pre_g=x_pre_g, x_post_g=x_post_g, w_xq=w_xq, w_xkv=w_xkv, w_xo=w_xo, ffn_pre_g=ffn_pre_g, ffn_post_g=ffn_post_g, w_gate_up=w_gate_up, w_down=w_down, loss_target=loss_target, m_mem_norm_g=m_mem_norm_g, m_mix_pre_g=m_mix_pre_g, m_mix_post_g=m_mix_post_g, m_w_in=m_w_in, m_gm_v_g=m_gm_v_g, m_gm_w_s=m_gm_w_s, m_gm_b_s=m_gm_b_s, m_pool_w=m_pool_w, m_pool_scale=m_pool_scale, m_attn_sinks=m_attn_sinks, m_w_o=m_w_o, m_x_pre_g=m_x_pre_g, m_x_post_g=m_x_post_g, m_w_xq=m_w_xq, m_w_xkv=m_w_xkv, m_w_xo=m_w_xo, m_ffn_pre_g=m_ffn_pre_g, m_ffn_post_g=m_ffn_post_g, m_w_gate_up=m_w_gate_up, m_w_down=m_w_down, v_mem_norm_g=v_mem_norm_g, v_mix_pre_g=v_mix_pre_g, v_mix_post_g=v_mix_post_g, v_w_in=v_w_in, v_gm_v_g=v_gm_v_g, v_gm_w_s=v_gm_w_s, v_gm_b_s=v_gm_b_s, v_pool_w=v_pool_w, v_pool_scale=v_pool_scale, v_attn_sinks=v_attn_sinks, v_w_o=v_w_o, v_x_pre_g=v_x_pre_g, v_x_post_g=v_x_post_g, v_w_xq=v_w_xq, v_w_xkv=v_w_xkv, v_w_xo=v_w_xo, v_ffn_pre_g=v_ffn_pre_g, v_ffn_post_g=v_ffn_post_g, v_w_gate_up=v_w_gate_up, v_w_down=v_w_down)
    weights = {n: given[n] for n in TWIN_WEIGHTS}
    shared = {n: given[n] for n in SHARED_INPUTS}
    per_example = {n: given[n] for n in ['x', 'mem', 'positions']}
    grad_fn = _jax.value_and_grad(_loss, argnums=(0, 1))

    def one_microbatch(ex, loss_target):
        ex = dict(ex)
        diff = ex.pop(TWIN_DIFF_INPUT)
        return grad_fn(weights, diff, {**shared, **ex}, loss_target)

    if N_MICROBATCH == 1:
        loss, (grad_w, grad_x) = one_microbatch(per_example, given["loss_target"])
    else:
        def body(carry, xs):
            loss_sum, grad_sum = carry
            l_k, (gw_k, gx_k) = one_microbatch(xs[0], xs[1])
            with _jax.named_scope("update"):
                return (loss_sum + l_k, _jax.tree.map(_jnp.add, grad_sum, gw_k)), gx_k

        init = (_jnp.zeros((), _jnp.float32), _jax.tree.map(_jnp.zeros_like, weights))
        (loss, grad_w), grad_x = _jax.lax.scan(body, init, (per_example, given["loss_target"]))
    with _jax.named_scope("update"):
        delta_w, new_m, new_v = {}, {}, {}
        for n in TWIN_WEIGHTS:
            delta_w[n], new_m[n], new_v[n] = _adamw(weights[n], grad_w[n], given["m_" + n], given["v_" + n])
    return (loss, grad_x, *[grad_w[n] for n in TWIN_WEIGHTS], *[delta_w[n] for n in TWIN_WEIGHTS],
            *[new_m[n] for n in TWIN_WEIGHTS], *[new_v[n] for n in TWIN_WEIGHTS])
```

```python
import functools

import jax
import jax.numpy as jnp
from jax import lax
from jax.experimental import pallas as pl
from jax.experimental.pallas import tpu as pltpu

F32 = jnp.float32
BF16 = jnp.bfloat16
EPS = 1e-6
CHUNK = 128
HEAD = 64
ROPE_THETA = 10000.0
POOL_WINDOWS = (2, 4, 8, 16)
LR, B1, B2, ADAM_EPS, WD, STEP = 0.001, 0.9, 0.999, 1e-08, 0.01, 10
MESH = pl.DeviceIdType.MESH
VMEM_LIMIT = 56 * 1024 * 1024

NAMES = ['x', 'mem', 'positions', 'mem_norm_g', 'mix_pre_g', 'mix_post_g', 'w_in', 'gm_v_g', 'gm_w_s', 'gm_b_s',
         'pool_w', 'pool_scale', 'attn_sinks', 'w_o', 'x_pre_g', 'x_post_g', 'w_xq', 'w_xkv', 'w_xo', 'ffn_pre_g',
         'ffn_post_g', 'w_gate_up', 'w_down']
WEIGHTS = NAMES[3:]
BIG = ['w_in', 'w_o', 'w_xq', 'w_xkv', 'w_xo', 'w_gate_up', 'w_down']
BIG_AXIS = {'w_in': 2, 'w_o': 1, 'w_xq': 1, 'w_xkv': 2, 'w_xo': 1, 'w_gate_up': 2, 'w_down': 1}
SMALL = [n for n in WEIGHTS if n not in BIG]

NN = (((1,), (0,)), ((), ()))
NT = (((1,), (1,)), ((), ()))
TN = (((0,), (0,)), ((), ()))


def _dot(a, b, dims=NN):
    return lax.dot_general(a, b, dims, preferred_element_type=F32)


def _params(sem):
    return pltpu.CompilerParams(dimension_semantics=sem, vmem_limit_bytes=VMEM_LIMIT)


def _rows_tile(rows, limit=256):
    return max(t for t in range(16, limit + 1, 16) if rows % t == 0)


def _mm(name, a, a_spec, b, b_spec, dims, grid, nk, out_shape, out_spec, add=None, add_spec=None, buf=None):
    acc_shape = out_spec.block_shape
    acc_shape = tuple(s for s in acc_shape if s is not None)

    def body(*refs):
        a_ref, b_ref = refs[0], refs[1]
        pos = 2
        add_ref = None
        if add is not None:
            add_ref = refs[pos]
            pos += 1
        if buf is not None:
            pos += 1
        o_ref = refs[pos]
        part = _dot(a_ref[...].astype(BF16), b_ref[...].astype(BF16), dims)
        if nk == 1:
            if add_ref is not None:
                part = part + add_ref[...]
            o_ref[...] = part.astype(o_ref.dtype)
        else:
            acc_ref = refs[pos + 1]
            k = pl.program_id(2)

            @pl.when(k == 0)
            def _():
                acc_ref[...] = part if add_ref is None else part + add_ref[...]

            @pl.when(k > 0)
            def _():
                acc_ref[...] += part

            @pl.when(k == nk - 1)
            def _():
                o_ref[...] = acc_ref[...].astype(o_ref.dtype)

    ops, specs = [a, b], [a_spec, b_spec]
    if add is not None:
        ops.append(add)
        specs.append(add_spec)
    aliases = {}
    if buf is not None:
        aliases = {len(ops): 0}
        ops.append(buf)
        specs.append(pl.BlockSpec(memory_space=pl.ANY))
    return pl.pallas_call(
        body, name=name, grid=grid, in_specs=specs, out_specs=out_spec, out_shape=out_shape,
        scratch_shapes=[pltpu.VMEM(acc_shape, F32)] if nk > 1 else [],
        input_output_aliases=aliases, compiler_params=_params(("parallel", "parallel", "arbitrary")),
    )(*ops)


def _wspec(block, layer, fn):
    return pl.BlockSpec((None,) + block, lambda i, j, k: (layer,) + fn(i, j, k))


def _mm_nn(name, a, w, layer, *, tm, tn, tk, out_dtype, n0=0, n=None, k0=0):
    m, kk = a.shape
    n = w.shape[2] if n is None else n
    tm = min(tm, m)
    nk = kk // tk
    return _mm(name, a, pl.BlockSpec((tm, tk), lambda i, j, k: (i, k)),
               w, _wspec((tk, tn), layer, lambda i, j, k: (k + k0 // tk, j + n0 // tn)), NN,
               (m // tm, n // tn, nk), nk, jax.ShapeDtypeStruct((m, n), out_dtype),
               pl.BlockSpec((tm, tn), lambda i, j, k: (i, j)))


def _mm_nt(name, a, w, layer, *, tm, tn, tk, out_dtype, k0=0, add=None):
    m, kk = a.shape
    n = w.shape[1]
    tm = min(tm, m)
    nk = kk // tk
    ospec = pl.BlockSpec((tm, tn), lambda i, j, k: (i, j))
    return _mm(name, a, pl.BlockSpec((tm, tk), lambda i, j, k: (i, k)),
               w, _wspec((tn, tk), layer, lambda i, j, k: (j, k + k0 // tk)), NT,
               (m // tm, n // tn, nk), nk, jax.ShapeDtypeStruct((m, n), out_dtype), ospec,
               add=add, add_spec=ospec if add is not None else None)


def _mm_tn(name, a, b, buf, layer, *, tm, tn, tk, n0=0):
    kk, m = a.shape
    n = b.shape[1]
    tk = min(tk, kk)
    nk = kk // tk
    return _mm(name, a, pl.BlockSpec((tk, tm), lambda i, j, k: (k, i)),
               b, pl.BlockSpec((tk, tn), lambda i, j, k: (k, j)), TN,
               (m // tm, n // tn, nk), nk, jax.ShapeDtypeStruct(buf.shape, buf.dtype),
               _wspec((tm, tn), layer, lambda i, j, k: (i, j + n0 // tn)), buf=buf)


def _rstd(x):
    return lax.rsqrt(jnp.mean(x * x, axis=-1, keepdims=True) + EPS)


def _row(d):
    return pl.BlockSpec((1, d), lambda i: (0, 0))


def _prenorm(name, x, g, tm):
    m, d = x.shape

    def body(x_ref, g_ref, o_ref):
        xv = x_ref[...]
        o_ref[...] = (xv * _rstd(xv) * g_ref[...]).astype(BF16)

    blk = pl.BlockSpec((tm, d), lambda i: (i, 0))
    return pl.pallas_call(body, name=name, grid=(m // tm,), in_specs=[blk, _row(d)], out_specs=blk,
                          out_shape=jax.ShapeDtypeStruct((m, d), BF16), compiler_params=_params(("parallel",)))(x, g)


def _post_pre(name, x, y, g_post, g_pre, tm):
    m, d = x.shape

    def body(x_ref, y_ref, gp_ref, gn_ref, xo_ref, h_ref):
        yv = y_ref[...]
        xn = x_ref[...] + yv * _rstd(yv) * gp_ref[...]
        xo_ref[...] = xn
        h_ref[...] = (xn * _rstd(xn) * gn_ref[...]).astype(BF16)

    blk = pl.BlockSpec((tm, d), lambda i: (i, 0))
    return pl.pallas_call(
        body, name=name, grid=(m // tm,), in_specs=[blk, blk, _row(d), _row(d)], out_specs=[blk, blk],
        out_shape=[jax.ShapeDtypeStruct((m, d), F32), jax.ShapeDtypeStruct((m, d), BF16)],
        compiler_params=_params(("parallel",)))(x, y, g_post, g_pre)


def _loss_head(name, x, y, g_post, tgt, tm):
    m, d = x.shape

    def body(x_ref, y_ref, g_ref, t_ref, dy_ref, loss_ref):
        yv = y_ref[...]
        err = x_ref[...] + yv * _rstd(yv) * g_ref[...] - t_ref[...]
        dy_ref[...] = err * (1.0 / d)

        @pl.when(pl.program_id(0) == 0)
        def _():
            loss_ref[...] = jnp.zeros_like(loss_ref)

        loss_ref[...] += 0.5 * jnp.sum(jnp.mean(err * err, axis=-1, keepdims=True), axis=0, keepdims=True)

    blk = pl.BlockSpec((tm, d), lambda i: (i, 0))
    return pl.pallas_call(
        body, name=name, grid=(m // tm,), in_specs=[blk, blk, _row(d), blk],
        out_specs=[blk, pl.BlockSpec((1, 1), lambda i: (0, 0))],
        out_shape=[jax.ShapeDtypeStruct((m, d), F32), jax.ShapeDtypeStruct((1, 1), F32)],
        compiler_params=_params(("arbitrary",)))(x, y, g_post, tgt)


def _norm_bwd(name, xin, g, dy, resid, out_dtype, tm):
    m, d = xin.shape

    def body(*refs):
        if resid is None:
            x_ref, g_ref, dy_ref, dx_ref, dg_ref = refs
        else:
            x_ref, g_ref, dy_ref, r_ref, dx_ref, dg_ref = refs
        xv = x_ref[...]
        r = _rstd(xv)
        xh = xv * r
        dyv = dy_ref[...].astype(F32)
        dyg = dyv * g_ref[...]
        dx = r * (dyg - xh * jnp.mean(dyg * xh, axis=-1, keepdims=True))
        if resid is not None:
            dx = dx + r_ref[...]
        dx_ref[...] = dx.astype(dx_ref.dtype)

        @pl.when(pl.program_id(0) == 0)
        def _():
            dg_ref[...] = jnp.zeros_like(dg_ref)

        dg_ref[...] += jnp.sum(dyv * xh, axis=0, keepdims=True)

    blk = pl.BlockSpec((tm, d), lambda i: (i, 0))
    ops = [xin, g, dy] + ([] if resid is None else [resid])
    specs = [blk, _row(d), blk] + ([] if resid is None else [blk])
    return pl.pallas_call(
        body, name=name, grid=(m // tm,), in_specs=specs, out_specs=[blk, _row(d)],
        out_shape=[jax.ShapeDtypeStruct((m, d), out_dtype), jax.ShapeDtypeStruct((1, d), F32)],
        compiler_params=_params(("arbitrary",)))(*ops)


def _adamw(name, g, w, m, v, tr):
    rows, cols = g.shape
    c1 = 1.0 - B1 ** STEP
    c2 = 1.0 - B2 ** STEP

    def body(g_ref, w_ref, m_ref, v_ref, d_ref, mo_ref, vo_ref):
        gv = g_ref[...]
        mn = B1 * m_ref[...] + (1.0 - B1) * gv
        vn = B2 * v_ref[...] + (1.0 - B2) * (gv * gv)
        mo_ref[...] = mn
        vo_ref[...] = vn
        d_ref[...] = -LR * ((mn / c1) / (jnp.sqrt(vn / c2) + ADAM_EPS) + WD * w_ref[...])

    blk = pl.BlockSpec((tr, cols), lambda i: (i, 0))
    sd = jax.ShapeDtypeStruct((rows, cols), F32)
    return pl.pallas_call(body, name=name, grid=(rows // tr,), in_specs=[blk] * 4, out_specs=[blk] * 3,
                          out_shape=[sd, sd, sd], compiler_params=_params(("parallel",)))(g, w, m, v)


def _gelu_parts(x):
    c = 0.7978845608028654
    t = jnp.tanh(c * (x + 0.044715 * (x * x * x)))
    return 0.5 * x * (1.0 + t), t


def _gelu_grad(x, t):
    c = 0.7978845608028654
    return 0.5 * (1.0 + t) + 0.5 * x * (1.0 - t * t) * (c * (1.0 + 3.0 * 0.044715 * x * x))


def _rot_half(x):
    ax = x.ndim - 1
    w = x.shape[ax]
    lane = lax.broadcasted_iota(jnp.int32, x.shape, ax)
    return jnp.where((lane & 63) < 32, pltpu.roll(x, w - 32, ax), pltpu.roll(x, 32, ax))


def _gm_group(gel, g, gv, ws_ref, bt):
    u = gel[:, HEAD * g:HEAD * (g + 1)]
    vg = gel[:, 256 + HEAD * g:256 + HEAD * (g + 1)]
    r = _rstd(vg)
    xh = vg * r
    vn = (xh * gv[:, HEAD * g:HEAD * (g + 1)]).astype(BF16)
    row = lax.broadcasted_iota(jnp.int32, (CHUNK, CHUNK), 0)
    col = lax.broadcasted_iota(jnp.int32, (CHUNK, CHUNK), 1)
    causal = col <= row
    wc = jnp.where(causal, ws_ref[g], 0.0).astype(BF16)
    mixed = _dot(wc, vn) + bt[:, g:g + 1]
    return u, r, xh, vn, wc, causal, mixed


def _lane_select(lane, vals):
    return jnp.where(lane < 64, vals[0], jnp.where(lane < 128, vals[1], jnp.where(lane < 192, vals[2], vals[3])))


def _pool_fwd(pc, pp, ci):
    ext = jnp.concatenate([pp, pc], axis=0)
    s2 = ext + pltpu.roll(ext, 1, 0)
    s4 = s2 + pltpu.roll(s2, 2, 0)
    s8 = s4 + pltpu.roll(s4, 4, 0)
    s16 = s8 + pltpu.roll(s8, 8, 0)
    t1 = ci * CHUNK + lax.broadcasted_iota(jnp.int32, (CHUNK, 1), 0) + 1
    lane = lax.broadcasted_iota(jnp.int32, (1, 256), 1)
    cnt = _lane_select(lane, [jnp.minimum(t1, w).astype(F32) for w in POOL_WINDOWS])
    ssel = _lane_select(lane, [s[CHUNK:] for s in (s2, s4, s8, s16)])
    return ssel / cnt - pc, cnt, lane


def _attn_prep(zc, zpkv, cq, sq, cp, sp, ci):
    q = zc[:, 768:1280]
    kc = zc[:, 1280:1408]
    vc = zc[:, 1408:1536]
    kp = zpkv[:, :128]
    vp = zpkv[:, 128:]
    qr = q * cq + _rot_half(q) * sq
    krc = kc * cq[:, :128] + _rot_half(kc) * sq[:, :128]
    krp = kp * cp + _rot_half(kp) * sp
    kband = jnp.concatenate([krp, krc], axis=0).astype(BF16)
    vband = jnp.concatenate([vp, vc], axis=0).astype(BF16)
    row = lax.broadcasted_iota(jnp.int32, (CHUNK, 2 * CHUNK), 0)
    col = lax.broadcasted_iota(jnp.int32, (CHUNK, 2 * CHUNK), 1)
    valid = ((col < CHUNK) & (col > row) & (ci > 0)) | ((col >= CHUNK) & (col - CHUNK <= row))
    return qr, kband, vband, valid


def _head_probs(qh, kh, valid, sink):
    s = _dot(qh, kh, NT) * (HEAD ** -0.5)
    s = jnp.where(valid, s, -1e30)
    mx = jnp.maximum(jnp.max(s, axis=-1, keepdims=True), sink)
    e = jnp.exp(s - mx)
    es = jnp.exp(sink - mx)
    den = jnp.sum(e, axis=-1, keepdims=True) + es
    return e / den, es / den


def _mixer_specs(nb, rev):
    def cur(i):
        return nb - 1 - i if rev else i

    def prev(i):
        return jnp.maximum(cur(i) - 1, 0)

    full = lambda shape: pl.BlockSpec(shape, lambda i: (0,) * len(shape))
    specs = [
        pl.BlockSpec((CHUNK, 1536), lambda i: (cur(i), 0)),
        pl.BlockSpec((CHUNK, 256), lambda i: (prev(i), 2)),
        pl.BlockSpec((CHUNK, 256), lambda i: (prev(i), 5)),
        pl.BlockSpec((CHUNK, 512), lambda i: (cur(i), 0)),
        pl.BlockSpec((CHUNK, 512), lambda i: (cur(i), 0)),
        pl.BlockSpec((CHUNK, 128), lambda i: (prev(i), 0)),
        pl.BlockSpec((CHUNK, 128), lambda i: (prev(i), 0)),
        full((1, 256)), full((4, CHUNK, CHUNK)), full((CHUNK, 4)), full((256, 256)), full((1, 256)), full((1, 8)),
    ]
    return specs, cur


def _mixer_fwd(name, z, cosq, sinq, gv, ws, bt, pw, psc, snk):
    s = z.shape[0]
    nb = s // CHUNK
    specs, _ = _mixer_specs(nb, False)

    def body(zc_ref, zpp_ref, zpkv_ref, cq_ref, sq_ref, cp_ref, sp_ref, gv_ref, ws_ref, bt_ref, pw_ref, psc_ref,
             snk_ref, o_ref):
        ci = pl.program_id(0)
        zc = zc_ref[...]
        gel, _ = _gelu_parts(zc[:, :512])
        gvv = gv_ref[...]
        btv = bt_ref[...]
        for g in range(4):
            u, _, _, _, _, _, mixed = _gm_group(gel, g, gvv, ws_ref, btv)
            o_ref[:, HEAD * g:HEAD * (g + 1)] = (u * mixed).astype(BF16)
        pp = jnp.where(ci > 0, zpp_ref[...], 0.0)
        pooled, _, _ = _pool_fwd(zc[:, 512:768], pp, ci)
        mp = _dot(pooled.astype(BF16), pw_ref[...].astype(BF16))
        o_ref[:, 256:512] = (mp * psc_ref[...]).astype(BF16)
        qr, kband, vband, valid = _attn_prep(zc, zpkv_ref[...], cq_ref[...], sq_ref[...], cp_ref[...], sp_ref[...], ci)
        snkv = snk_ref[...]
        for h in range(8):
            hk = h // 4
            p, _ = _head_probs(qr[:, HEAD * h:HEAD * (h + 1)].astype(BF16), kband[:, HEAD * hk:HEAD * (hk + 1)],
                               valid, snkv[:, h:h + 1])
            o = _dot(p.astype(BF16), vband[:, HEAD * hk:HEAD * (hk + 1)])
            o_ref[:, 512 + HEAD * h:512 + HEAD * (h + 1)] = o.astype(BF16)

    return pl.pallas_call(
        body, name=name, grid=(nb,), in_specs=specs, out_specs=pl.BlockSpec((CHUNK, 1024), lambda i: (i, 0)),
        out_shape=jax.ShapeDtypeStruct((s, 1024), BF16), compiler_params=_params(("parallel",)),
    )(z, z, z, cosq, sinq, cosq, sinq, gv, ws, bt, pw, psc, snk)


def _mixer_bwd(name, z, dabc, cosq, sinq, gv, ws, bt, pw, psc, snk):
    s = z.shape[0]
    nb = s // CHUNK
    specs, cur = _mixer_specs(nb, True)
    specs = specs + [pl.BlockSpec((CHUNK, 1024), lambda i: (cur(i), 0))]
    full = lambda shape: pl.BlockSpec(shape, lambda i: (0,) * len(shape))
    acc_shapes = [(1, 256), (4, CHUNK, CHUNK), (CHUNK, 4), (256, 256), (1, 256), (1, 8)]

    def body(zc_ref, zpp_ref, zpkv_ref, cq_ref, sq_ref, cp_ref, sp_ref, gv_ref, ws_ref, bt_ref, pw_ref, psc_ref,
             snk_ref, dabc_ref, dz_ref, dgv_ref, dws_ref, dbt_ref, dpw_ref, dpsc_ref, dsnk_ref,
             cpool, ck, cv, dq_s, dkv_s):
        step = pl.program_id(0)
        ci = nb - 1 - step

        @pl.when(step == 0)
        def _():
            for r in (dgv_ref, dws_ref, dbt_ref, dpw_ref, dpsc_ref, dsnk_ref, cpool, ck, cv):
                r[...] = jnp.zeros_like(r)

        zc = zc_ref[...]
        dabc = dabc_ref[...]
        zg = zc[:, :512]
        gel, th = _gelu_parts(zg)
        gp = _gelu_grad(zg, th)
        gvv = gv_ref[...]
        btv = bt_ref[...]
        lane4 = lax.broadcasted_iota(jnp.int32, (CHUNK, 4), 1)
        dbt = jnp.zeros((CHUNK, 4), F32)
        for g in range(4):
            lo, hi = HEAD * g, HEAD * (g + 1)
            u, r, xh, vn, wc, causal, mixed = _gm_group(gel, g, gvv, ws_ref, btv)
            da = dabc[:, lo:hi]
            dm = da * u
            dmb = dm.astype(BF16)
            dws_ref[g] += jnp.where(causal, _dot(dmb, vn, NT), 0.0)
            dbt = dbt + jnp.where(lane4 == g, jnp.sum(dm, axis=-1, keepdims=True), 0.0)
            dvn = _dot(wc, dmb, TN)
            dgv_ref[:, lo:hi] += jnp.sum(dvn * xh, axis=0, keepdims=True)
            dxh = dvn * gvv[:, lo:hi]
            dvg = r * (dxh - xh * jnp.mean(dxh * xh, axis=-1, keepdims=True))
            dz_ref[:, lo:hi] = (da * mixed * gp[:, lo:hi]).astype(BF16)
            dz_ref[:, 256 + lo:256 + hi] = (dvg * gp[:, 256 + lo:256 + hi]).astype(BF16)
        dbt_ref[...] += dbt
        pc = zc[:, 512:768]
        pp = jnp.where(ci > 0, zpp_ref[...], 0.0)
        pooled, cnt, lane = _pool_fwd(pc, pp, ci)
        pwb = pw_ref[...].astype(BF16)
        pooled_b = pooled.astype(BF16)
        mp = _dot(pooled_b, pwb)
        db = dabc[:, 256:512]
        dpsc_ref[...] += jnp.sum(db * mp, axis=0, keepdims=True)
        dmpb = (db * psc_ref[...]).astype(BF16)
        dpw_ref[...] += _dot(pooled_b, dmpb, TN)
        dpooled = _dot(dmpb, pwb, NT)
        davg = dpooled / cnt
        zero = jnp.zeros((CHUNK, 256), F32)
        d2, d4, d8, d16 = [jnp.concatenate([zero, jnp.where((lane >= 64 * k) & (lane < 64 * (k + 1)), davg, 0.0)],
                                           axis=0) for k in range(4)]
        g8 = d8 + d16 + pltpu.roll(d16, 2 * CHUNK - 8, 0)
        g4 = d4 + g8 + pltpu.roll(g8, 2 * CHUNK - 4, 0)
        g2 = d2 + g4 + pltpu.roll(g4, 2 * CHUNK - 2, 0)
        ge = g2 + pltpu.roll(g2, 2 * CHUNK - 1, 0)
        dz_ref[:, 512:768] = (ge[CHUNK:] - dpooled + cpool[...]).astype(BF16)
        cpool[...] = ge[:CHUNK]
        cq = cq_ref[...]
        sq = sq_ref[...]
        qr, kband, vband, valid = _attn_prep(zc, zpkv_ref[...], cq, sq, cp_ref[...], sp_ref[...], ci)
        snkv = snk_ref[...]
        lane8 = lax.broadcasted_iota(jnp.int32, (1, 8), 1)
        dsnk = jnp.zeros((1, 8), F32)
        for hk in range(2):
            kh = kband[:, HEAD * hk:HEAD * (hk + 1)]
            vh = vband[:, HEAD * hk:HEAD * (hk + 1)]
            dkh = jnp.zeros((2 * CHUNK, HEAD), F32)
            dvh = jnp.zeros((2 * CHUNK, HEAD), F32)
            for h in range(4 * hk, 4 * hk + 4):
                qh = qr[:, HEAD * h:HEAD * (h + 1)].astype(BF16)
                p, ps = _head_probs(qh, kh, valid, snkv[:, h:h + 1])
                dob = dabc[:, 512 + HEAD * h:512 + HEAD * (h + 1)].astype(BF16)
                dp = _dot(dob, vh, NT)
                dd = jnp.sum(p * dp, axis=-1, keepdims=True)
                dsnk = dsnk + jnp.where(lane8 == h, jnp.sum(-ps * dd, axis=0, keepdims=True), 0.0)
                dsb = (p * (dp - dd) * (HEAD ** -0.5)).astype(BF16)
                dq_s[:, HEAD * h:HEAD * (h + 1)] = _dot(dsb, kh)
                dkh = dkh + _dot(dsb, qh, TN)
                dvh = dvh + _dot(p.astype(BF16), dob, TN)
            dkv_s[:, HEAD * hk:HEAD * (hk + 1)] = dkh
            dkv_s[:, 128 + HEAD * hk:128 + HEAD * (hk + 1)] = dvh
        dsnk_ref[...] += dsnk
        dqr = dq_s[...]
        dz_ref[:, 768:1280] = (dqr * cq + _rot_half(dqr * sq)).astype(BF16)
        dkv = dkv_s[...]
        dkr = dkv[CHUNK:, :128] + ck[...]
        dz_ref[:, 1280:1408] = (dkr * cq[:, :128] + _rot_half(dkr * sq[:, :128])).astype(BF16)
        dz_ref[:, 1408:1536] = (dkv[CHUNK:, 128:] + cv[...]).astype(BF16)
        ck[...] = dkv[:CHUNK, :128]
        cv[...] = dkv[:CHUNK, 128:]

    return pl.pallas_call(
        body, name=name, grid=(nb,), in_specs=specs,
        out_specs=[pl.BlockSpec((CHUNK, 1536), lambda i: (cur(i), 0))] + [full(a) for a in acc_shapes],
        out_shape=[jax.ShapeDtypeStruct((s, 1536), BF16)] + [jax.ShapeDtypeStruct(a, F32) for a in acc_shapes],
        scratch_shapes=[pltpu.VMEM((CHUNK, 256), F32), pltpu.VMEM((CHUNK, 128), F32), pltpu.VMEM((CHUNK, 128), F32),
                        pltpu.VMEM((CHUNK, 512), F32), pltpu.VMEM((2 * CHUNK, 256), F32)],
        compiler_params=_params(("arbitrary",)),
    )(z, z, z, cosq, sinq, cosq, sinq, gv, ws, bt, pw, psc, snk, dabc)


def _xattn_probs(qh, kh):
    s = _dot(qh, kh, NT) * (256 ** -0.5)
    e = jnp.exp(s - jnp.max(s, axis=-1, keepdims=True))
    return e / jnp.sum(e, axis=-1, keepdims=True)


def _xattn_fwd(name, q, kv, tq):
    s, d = q.shape
    mlen = kv.shape[0]

    def body(q_ref, kv_ref, o_ref):
        for h in range(4):
            lo, hi = 256 * h, 256 * (h + 1)
            p = _xattn_probs(q_ref[:, lo:hi], kv_ref[:, lo:hi])
            o_ref[:, lo:hi] = _dot(p.astype(BF16), kv_ref[:, d + lo:d + hi]).astype(BF16)

    blk = pl.BlockSpec((tq, d), lambda i: (i, 0))
    return pl.pallas_call(body, name=name, grid=(s // tq,),
                          in_specs=[blk, pl.BlockSpec((mlen, 2 * d), lambda i: (0, 0))], out_specs=blk,
                          out_shape=jax.ShapeDtypeStruct((s, d), BF16), compiler_params=_params(("parallel",)))(q, kv)


def _xattn_bwd(name, q, kv, do, tq):
    s, d = q.shape
    mlen = kv.shape[0]

    def body(q_ref, kv_ref, do_ref, dq_ref, dkv_ref):
        @pl.when(pl.program_id(0) == 0)
        def _():
            dkv_ref[...] = jnp.zeros_like(dkv_ref)

        for h in range(4):
            lo, hi = 256 * h, 256 * (h + 1)
            qh = q_ref[:, lo:hi]
            kh = kv_ref[:, lo:hi]
            vh = kv_ref[:, d + lo:d + hi]
            doh = do_ref[:, lo:hi]
            p = _xattn_probs(qh, kh)
            dp = _dot(doh, vh, NT)
            dsb = (p * (dp - jnp.sum(p * dp, axis=-1, keepdims=True)) * (256 ** -0.5)).astype(BF16)
            dq_ref[:, lo:hi] = _dot(dsb, kh).astype(BF16)
            dkv_ref[:, lo:hi] += _dot(dsb, qh, TN)
            dkv_ref[:, d + lo:d + hi] += _dot(p.astype(BF16), doh, TN)

    blk = pl.BlockSpec((tq, d), lambda i: (i, 0))
    kvb = pl.BlockSpec((mlen, 2 * d), lambda i: (0, 0))
    return pl.pallas_call(
        body, name=name, grid=(s // tq,), in_specs=[blk, kvb, blk], out_specs=[blk, kvb],
        out_shape=[jax.ShapeDtypeStruct((s, d), BF16), jax.ShapeDtypeStruct((mlen, 2 * d), F32)],
        compiler_params=_params(("arbitrary",)))(q, kv, do)


def _ffn_up(name, h, wgu, layer, tm, tn):
    s, d = h.shape
    dff = wgu.shape[2] // 2
    nj = dff // tn

    def body(h_ref, wg_ref, wu_ref, g_ref, u_ref, a_ref):
        hv = h_ref[...]
        gate = _dot(hv, wg_ref[...])
        up = _dot(hv, wu_ref[...])
        g_ref[...] = gate.astype(BF16)
        u_ref[...] = up.astype(BF16)
        a_ref[...] = (gate / (1.0 + jnp.exp(-gate)) * up).astype(BF16)

    ob = pl.BlockSpec((tm, tn), lambda j, i: (i, j))
    sd = jax.ShapeDtypeStruct((s, dff), BF16)
    return pl.pallas_call(
        body, name=name, grid=(nj, s // tm),
        in_specs=[pl.BlockSpec((tm, d), lambda j, i: (i, 0)),
                  pl.BlockSpec((None, d, tn), lambda j, i: (layer, 0, j)),
                  pl.BlockSpec((None, d, tn), lambda j, i: (layer, 0, j + nj))],
        out_specs=[ob, ob, ob], out_shape=[sd, sd, sd], compiler_params=_params(("parallel", "parallel")),
    )(h, wgu, wgu)


def _ffn_act_bwd(name, dfn, wdown, layer, gate, up, tm, tn):
    s, d = dfn.shape
    dff = gate.shape[1]

    def body(df_ref, wd_ref, g_ref, u_ref, dg_ref, du_ref):
        dact = _dot(df_ref[...], wd_ref[...], NT)
        gate = g_ref[...].astype(F32)
        sig = 1.0 / (1.0 + jnp.exp(-gate))
        du_ref[...] = (dact * gate * sig).astype(BF16)
        dg_ref[...] = (dact * u_ref[...].astype(F32) * sig * (1.0 + gate * (1.0 - sig))).astype(BF16)

    ob = pl.BlockSpec((tm, tn), lambda j, i: (i, j))
    sd = jax.ShapeDtypeStruct((s, dff), BF16)
    return pl.pallas_call(
        body, name=name, grid=(dff // tn, s // tm),
        in_specs=[pl.BlockSpec((tm, d), lambda j, i: (i, 0)),
                  pl.BlockSpec((None, tn, d), lambda j, i: (layer, j, 0)), ob, ob],
        out_specs=[ob, ob], out_shape=[sd, sd], compiler_params=_params(("parallel", "parallel")),
    )(dfn, wdown, gate, up)


def _place():
    return lax.axis_index("x"), lax.axis_index("y"), lax.axis_index("c")


def _other_chips(x, y):
    return [(1 - x, y), (x, 1 - y), (1 - x, 1 - y)]


def _region(ref, axis, chip, size):
    start = pl.multiple_of(chip * size, size)
    if axis == 1:
        return ref.at[:, pl.ds(start, size), :]
    return ref.at[:, :, pl.ds(start, size)]


ANY = pl.BlockSpec(memory_space=pl.ANY)


def _allgather_weights(shards, axes):
    n = len(shards)
    nl = shards[0].shape[0]
    hl = nl // 2
    sizes = [s.shape[a] for s, a in zip(shards, axes)]
    fulls = [jax.ShapeDtypeStruct(tuple(4 * d if i == a else d for i, d in enumerate(s.shape)), s.dtype)
             for s, a in zip(shards, axes)]
    nrem = 3 * n

    def body(*refs):
        sh, full = refs[:n], refs[n:2 * n]
        send, recv, lsem = refs[2 * n:]
        x, y, c = _place()
        me = 2 * x + y
        chips = _other_chips(x, y)
        sib = (x, y, 1 - c)

        def half(t, chip, hc):
            return _region(full[t], axes[t], chip, sizes[t]).at[pl.ds(hc * hl, hl)]

        local = [pltpu.make_async_copy(sh[t], _region(full[t], axes[t], me, sizes[t]), lsem.at[t]) for t in range(n)]
        for cp in local:
            cp.start()
        first, passed = [], []
        for t in range(n):
            for j, (px, py) in enumerate(chips):
                k = 3 * t + j
                first.append(pltpu.make_async_remote_copy(
                    src_ref=sh[t].at[pl.ds(c * hl, hl)], dst_ref=half(t, me, c), send_sem=send.at[k],
                    recv_sem=recv.at[k], device_id=(px, py, c), device_id_type=MESH))
        for cp in first:
            cp.start()
        for t in range(n):
            for j, (px, py) in enumerate(chips):
                k = 3 * t + j
                got = half(t, 2 * px + py, c)
                pltpu.make_async_remote_copy(src_ref=got, dst_ref=got, send_sem=send.at[k], recv_sem=recv.at[k],
                                             device_id=(x, y, c), device_id_type=MESH).wait_recv()
                fw = pltpu.make_async_remote_copy(src_ref=got, dst_ref=got, send_sem=send.at[nrem + k],
                                                  recv_sem=recv.at[nrem + k], device_id=sib, device_id_type=MESH)
                fw.start()
                passed.append(fw)
        for t in range(n):
            for j, (px, py) in enumerate(chips):
                k = 3 * t + j
                got = half(t, 2 * px + py, 1 - c)
                pltpu.make_async_remote_copy(src_ref=got, dst_ref=got, send_sem=send.at[nrem + k],
                                             recv_sem=recv.at[nrem + k], device_id=(x, y, c),
                                             device_id_type=MESH).wait_recv()
        for cp in first + passed:
            cp.wait_send()
        for cp in local:
            cp.wait()

    return pl.pallas_call(
        body, name="allgather_weights", in_specs=[ANY] * n, out_specs=[ANY] * n, out_shape=fulls,
        scratch_shapes=[pltpu.SemaphoreType.DMA((2 * nrem,)), pltpu.SemaphoreType.DMA((2 * nrem,)),
                        pltpu.SemaphoreType.DMA((n,))],
    )(*shards)


def _pair_exchange(name, srcs, out_shapes, src_fn, dst_fn, keep_fn=None):
    n = len(srcs)

    def body(*refs):
        src, out = refs[:n], refs[n:2 * n]
        send, recv, lsem = refs[2 * n:]
        x, y, c = _place()
        cps = [pltpu.make_async_remote_copy(src_ref=src_fn(src[t], c), dst_ref=dst_fn(out[t], c), send_sem=send.at[t],
                                            recv_sem=recv.at[t], device_id=(x, y, 1 - c), device_id_type=MESH)
               for t in range(n)]
        loc = []
        if keep_fn is not None:
            loc = [pltpu.make_async_copy(*keep_fn(src[t], out[t], c), lsem.at[t]) for t in range(n)]
        for cp in cps + loc:
            cp.start()
        for t in range(n):
            pltpu.make_async_remote_copy(src_ref=src_fn(src[t], 1 - c), dst_ref=dst_fn(out[t], 1 - c),
                                         send_sem=send.at[t], recv_sem=recv.at[t], device_id=(x, y, c),
                                         device_id_type=MESH).wait_recv()
        for cp in cps:
            cp.wait_send()
        for cp in loc:
            cp.wait()

    return pl.pallas_call(
        body, name=name, in_specs=[ANY] * n, out_specs=[ANY] * n, out_shape=out_shapes,
        scratch_shapes=[pltpu.SemaphoreType.DMA((n,)), pltpu.SemaphoreType.DMA((n,)), pltpu.SemaphoreType.DMA((n,))],
    )(*srcs)


def _chip_scatter(reds, axes, sizes):
    n = len(reds)
    outs = []
    for r, a, sz in zip(reds, axes, sizes):
        shp = tuple(sz if i == a else d for i, d in enumerate(r.shape))
        outs.append(jax.ShapeDtypeStruct((4,) + shp, r.dtype))

    def body(*refs):
        red, out = refs[:n], refs[n:2 * n]
        send, recv, lsem = refs[2 * n:]
        x, y, c = _place()
        me = 2 * x + y
        chips = _other_chips(x, y)
        loc = [pltpu.make_async_copy(_region(red[t], axes[t], me, sizes[t]), out[t].at[me], lsem.at[t])
               for t in range(n)]
        rem = []
        for t in range(n):
            for j, (px, py) in enumerate(chips):
                k = 3 * t + j
                rem.append(pltpu.make_async_remote_copy(
                    src_ref=_region(red[t], axes[t], 2 * px + py, sizes[t]), dst_ref=out[t].at[me],
                    send_sem=send.at[k], recv_sem=recv.at[k], device_id=(px, py, c), device_id_type=MESH))
        for cp in loc + rem:
            cp.start()
        for t in range(n):
            for j, (px, py) in enumerate(chips):
                k = 3 * t + j
                slot = out[t].at[2 * px + py]
                pltpu.make_async_remote_copy(src_ref=slot, dst_ref=slot, send_sem=send.at[k], recv_sem=recv.at[k],
                                             device_id=(x, y, c), device_id_type=MESH).wait_recv()
        for cp in rem:
            cp.wait_send()
        for cp in loc:
            cp.wait()

    return pl.pallas_call(
        body, name="chip_scatter_grads", in_specs=[ANY] * n, out_specs=[ANY] * n, out_shape=outs,
        scratch_shapes=[pltpu.SemaphoreType.DMA((3 * n,)), pltpu.SemaphoreType.DMA((3 * n,)),
                        pltpu.SemaphoreType.DMA((n,))],
    )(*reds)


def _pair_sum(name, g, recv, cidx, tr):
    rows, cols = recv.shape
    nb = rows // tr

    def body(c_ref, g_ref, r_ref, o_ref):
        o_ref[...] = (g_ref[...] + r_ref[...]).astype(BF16)

    return pl.pallas_call(
        body, name=name,
        grid_spec=pltpu.PrefetchScalarGridSpec(
            num_scalar_prefetch=1, grid=(nb,),
            in_specs=[pl.BlockSpec((tr, cols), lambda i, c: (c[0] * nb + i, 0)),
                      pl.BlockSpec((tr, cols), lambda i, c: (i, 0))],
            out_specs=pl.BlockSpec((tr, cols), lambda i, c: (i, 0))),
        out_shape=jax.ShapeDtypeStruct((rows, cols), BF16), compiler_params=_params(("parallel",)),
    )(cidx, g, recv)


def _slot_sum(name, slots, tr):
    _, rows, cols = slots.shape

    def body(s_ref, o_ref):
        acc = s_ref[0].astype(F32)
        for k in range(1, 4):
            acc = acc + s_ref[k].astype(F32)
        o_ref[...] = acc

    return pl.pallas_call(
        body, name=name, grid=(rows // tr,), in_specs=[pl.BlockSpec((4, tr, cols), lambda i: (0, i, 0))],
        out_specs=pl.BlockSpec((tr, cols), lambda i: (i, 0)), out_shape=jax.ShapeDtypeStruct((rows, cols), F32),
        compiler_params=_params(("parallel",)))(slots)


def _allreduce_small(p):
    rows = p.shape[0]

    def body(p_ref, o_ref, gath, send, recv):
        x, y, c = _place()
        me = 4 * x + 2 * y + c

        def flip(v, bit):
            return 1 - v if bit else v

        gath[me] = p_ref[...]
        cps = []
        for k in range(1, 8):
            peer = (flip(x, k & 4), flip(y, k & 2), flip(c, k & 1))
            cps.append(pltpu.make_async_remote_copy(src_ref=p_ref, dst_ref=gath.at[me], send_sem=send.at[k - 1],
                                                    recv_sem=recv.at[k - 1], device_id=peer, device_id_type=MESH))
        for cp in cps:
            cp.start()
        for k in range(1, 8):
            slot = gath.at[4 * flip(x, k & 4) + 2 * flip(y, k & 2) + flip(c, k & 1)]
            pltpu.make_async_remote_copy(src_ref=slot, dst_ref=slot, send_sem=send.at[k - 1], recv_sem=recv.at[k - 1],
                                         device_id=(x, y, c), device_id_type=MESH).wait_recv()
        for cp in cps:
            cp.wait_send()
        acc = gath[0]
        for k in range(1, 8):
            acc = acc + gath[k]
        o_ref[...] = acc

    vm = pl.BlockSpec(memory_space=pltpu.VMEM)
    return pl.pallas_call(
        body, name="allreduce_small", in_specs=[vm], out_specs=vm, out_shape=jax.ShapeDtypeStruct(p.shape, F32),
        scratch_shapes=[pltpu.VMEM((8, rows, 128), F32), pltpu.SemaphoreType.DMA((7,)), pltpu.SemaphoreType.DMA((7,))],
        compiler_params=pltpu.CompilerParams(vmem_limit_bytes=VMEM_LIMIT),
    )(p)


def _pack(parts):
    flat = []
    for p in parts:
        v = p.reshape(-1).astype(F32)
        flat.append(jnp.pad(v, (0, (-v.shape[0]) % 128)))
    v = jnp.concatenate(flat)
    v = jnp.pad(v, (0, (-v.shape[0]) % (512 * 128)))
    return v.reshape(-1, 128)


def _unpack(buf, like):
    v = buf.reshape(-1)
    out, off = [], 0
    for p in like:
        nelem = 1
        for s in p.shape:
            nelem *= s
        out.append(v[off:off + nelem].reshape(p.shape))
        off += nelem + (-nelem) % 128
    return out


def kernel(x, mem, positions, mem_norm_g, mix_pre_g, mix_post_g, w_in, gm_v_g, gm_w_s, gm_b_s, pool_w, pool_scale, attn_sinks, w_o, x_pre_g, x_post_g, w_xq, w_xkv, w_xo, ffn_pre_g, ffn_post_g, w_gate_up, w_down, loss_target, m_mem_norm_g, m_mix_pre_g, m_mix_post_g, m_w_in, m_gm_v_g, m_gm_w_s, m_gm_b_s, m_pool_w, m_pool_scale, m_attn_sinks, m_w_o, m_x_pre_g, m_x_post_g, m_w_xq, m_w_xkv, m_w_xo, m_ffn_pre_g, m_ffn_post_g, m_w_gate_up, m_w_down, v_mem_norm_g, v_mix_pre_g, v_mix_post_g, v_w_in, v_gm_v_g, v_gm_w_s, v_gm_b_s, v_pool_w, v_pool_scale, v_attn_sinks, v_w_o, v_x_pre_g, v_x_post_g, v_w_xq, v_w_xkv, v_w_xo, v_ffn_pre_g, v_ffn_post_g, v_w_gate_up, v_w_down):
    args = (x, mem, positions, mem_norm_g, mix_pre_g, mix_post_g, w_in, gm_v_g, gm_w_s, gm_b_s, pool_w, pool_scale, attn_sinks, w_o, x_pre_g, x_post_g, w_xq, w_xkv, w_xo, ffn_pre_g, ffn_post_g, w_gate_up, w_down)
    moms_m = (m_mem_norm_g, m_mix_pre_g, m_mix_post_g, m_w_in, m_gm_v_g, m_gm_w_s, m_gm_b_s, m_pool_w, m_pool_scale, m_attn_sinks, m_w_o, m_x_pre_g, m_x_post_g, m_w_xq, m_w_xkv, m_w_xo, m_ffn_pre_g, m_ffn_post_g, m_w_gate_up, m_w_down)
    moms_v = (v_mem_norm_g, v_mix_pre_g, v_mix_post_g, v_w_in, v_gm_v_g, v_gm_w_s, v_gm_b_s, v_pool_w, v_pool_scale, v_attn_sinks, v_w_o, v_x_pre_g, v_x_post_g, v_w_xq, v_w_xkv, v_w_xo, v_ffn_pre_g, v_ffn_post_g, v_w_gate_up, v_w_down)
    P = dict(zip(NAMES, args))
    P['loss_target'] = loss_target
    M = dict(zip(WEIGHTS, moms_m))
    V = dict(zip(WEIGHTS, moms_v))
    depth = w_in.shape[0]
    cidx = lax.axis_index("c").astype(jnp.int32).reshape(1)

    axes = [BIG_AXIS[n] for n in BIG]
    W = dict(zip(BIG, _allgather_weights([P[n].astype(BF16) for n in BIG], axes)))

    loss_part, dx, G, small_g = _fwd_bwd(P, W)
    loss = lax.psum(loss_part[0, 0], ("x", "y", "c"))
    grad_x = dx.reshape(x.shape)

    hl = depth // 2
    gl = [G[n] for n in BIG]
    sizes = [P[n].shape[a] for n, a in zip(BIG, axes)]
    half_shapes = [jax.ShapeDtypeStruct((hl,) + g.shape[1:], F32) for g in gl]
    recv_a = _pair_exchange("pair_send_grads", gl, half_shapes,
                            lambda r, c: r.at[pl.ds((1 - c) * hl, hl)], lambda o, c: o)
    reds = []
    for n, g, r in zip(BIG, gl, recv_a):
        rr, cc = g.shape[1], g.shape[2]
        red = _pair_sum("pair_sum_" + n, g.reshape(depth * rr, cc), r.reshape(hl * rr, cc), cidx,
                        _rows_tile(hl * rr))
        reds.append(red.reshape(hl, rr, cc))
    slots = _chip_scatter(reds, axes, sizes)
    halves = []
    for n, sl in zip(BIG, slots):
        _, _, rr, cc = sl.shape
        halves.append(_slot_sum("slot_sum_" + n, sl.reshape(4, hl * rr, cc), _rows_tile(hl * rr)).reshape(hl, rr, cc))
    shard_shapes = [jax.ShapeDtypeStruct(P[n].shape, F32) for n in BIG]
    gfull = _pair_exchange("pair_swap_grads", halves, shard_shapes,
                           lambda r, c: r, lambda o, c: o.at[pl.ds(c * hl, hl)],
                           keep_fn=lambda r, o, c: (r, o.at[pl.ds(c * hl, hl)]))
    grads, deltas, new_m, new_v = {}, {}, {}, {}
    for n, g in zip(BIG, gfull):
        cc = g.shape[2]
        v2 = lambda a: a.reshape(-1, cc)
        dlt, mn, vn = _adamw("adamw_" + n, v2(g), v2(P[n]), v2(M[n]), v2(V[n]), _rows_tile(g.shape[0] * g.shape[1]))
        grads[n] = g
        deltas[n], new_m[n], new_v[n] = dlt.reshape(g.shape), mn.reshape(g.shape), vn.reshape(g.shape)

    small_like = [P[n] for n in SMALL]
    gsum = _allreduce_small(_pack(small_g))
    dlt, mn, vn = _adamw("adamw_small", gsum, _pack(small_like), _pack([M[n] for n in SMALL]),
                         _pack([V[n] for n in SMALL]), 512)
    for name_map, buf in ((grads, gsum), (deltas, dlt), (new_m, mn), (new_v, vn)):
        for n, a in zip(SMALL, _unpack(buf, small_like)):
            name_map[n] = a

    return (loss, grad_x, *[grads[n] for n in WEIGHTS], *[deltas[n] for n in WEIGHTS],
            *[new_m[n] for n in WEIGHTS], *[new_v[n] for n in WEIGHTS])


def _fwd_bwd(P, W):
    (x, mem, positions, mem_norm_g, mix_pre_g, mix_post_g, w_in, gm_v_g, gm_w_s, gm_b_s, pool_w, pool_scale, attn_sinks,
     w_o, x_pre_g, x_post_g, w_xq, w_xkv, w_xo, ffn_pre_g, ffn_post_g, w_gate_up, w_down) = [P[n] for n in NAMES]
    x0 = x[0]
    s, d = x0.shape
    depth = W['w_in'].shape[0]
    dff = W['w_down'].shape[1]
    tgt = P['loss_target'][0]
    tmn = 256

    half = HEAD // 2
    inv = ROPE_THETA ** (-jnp.arange(half, dtype=F32) / half)
    ang = positions[0].astype(F32)[:, None] * inv
    cos, sin = jnp.cos(ang), jnp.sin(ang)
    cosq = jnp.tile(jnp.concatenate([cos, cos], axis=-1), (1, 8))
    sinq = jnp.tile(jnp.concatenate([-sin, sin], axis=-1), (1, 8))

    row = lambda a, l: a[l].reshape(1, -1)
    memn = _prenorm("mem_norm", mem[0], mem_norm_g.reshape(1, d), tmn)
    pw_bd = []
    for l in range(depth):
        bd = jnp.zeros((256, 256), F32)
        for g in range(4):
            bd = lax.dynamic_update_slice(bd, pool_w[l, g], (64 * g, 64 * g))
        pw_bd.append(bd)

    saved = []
    xc = x0
    h = _prenorm("pre_norm0", x0, row(mix_pre_g, 0), tmn)
    for l in range(depth):
        sv = {'x0': xc, 'h1': h}
        z = _mm_nn("fwd_w_in", h, W['w_in'], l, tm=1024, tn=512, tk=d, out_dtype=F32)
        abc = _mixer_fwd("mixer_fwd", z, cosq, sinq, row(gm_v_g, l), gm_w_s[l], gm_b_s[l].T, pw_bd[l],
                         row(pool_scale, l), row(attn_sinks, l))
        mix = _mm_nn("fwd_w_o", abc, W['w_o'], l, tm=1024, tn=512, tk=d, out_dtype=F32)
        xc, h = _post_pre("post_mix", xc, mix, row(mix_post_g, l), row(x_pre_g, l), tmn)
        sv.update(z=z, abc=abc, mix=mix, x1=xc, h2=h)
        q = _mm_nn("fwd_w_xq", h, W['w_xq'], l, tm=1024, tn=512, tk=d, out_dtype=BF16)
        kv = _mm_nn("fwd_w_xkv", memn, W['w_xkv'], l, tm=256, tn=512, tk=d, out_dtype=BF16)
        o = _xattn_fwd("xattn_fwd", q, kv, 512)
        xo = _mm_nn("fwd_w_xo", o, W['w_xo'], l, tm=1024, tn=512, tk=d, out_dtype=F32)
        xc, h = _post_pre("post_xattn", xc, xo, row(x_post_g, l), row(ffn_pre_g, l), tmn)
        sv.update(q=q, kv=kv, o=o, xo=xo, x2=xc, h3=h)
        gate, up, act = _ffn_up("ffn_up", h, W['w_gate_up'], l, 512, dff // 2)
        f = _mm_nn("fwd_w_down", act, W['w_down'], l, tm=1024, tn=512, tk=dff // 2, out_dtype=F32)
        sv.update(gate=gate, up=up, act=act, f=f)
        if l + 1 < depth:
            xc, h = _post_pre("post_ffn", xc, f, row(ffn_post_g, l), row(mix_pre_g, l + 1), tmn)
        saved.append(sv)
    dx, loss_part = _loss_head("loss_head", xc, saved[-1]['f'], row(ffn_post_g, depth - 1), tgt, tmn)

    G = {n: jnp.zeros(W[n].shape, F32) for n in BIG}
    gs = {n: [None] * depth for n in SMALL if n != 'mem_norm_g'}
    dmemn = None
    for l in reversed(range(depth)):
        sv = saved[l]
        dfn, gs['ffn_post_g'][l] = _norm_bwd("bwd_post_ffn", sv['f'], row(ffn_post_g, l), dx, None, BF16, tmn)
        G['w_down'] = _mm_tn("dw_down", sv['act'], dfn, G['w_down'], l, tm=dff // 2, tn=d, tk=512)
        dgate, dup = _ffn_act_bwd("ffn_act_bwd", dfn, W['w_down'], l, sv['gate'], sv['up'], 512, dff // 2)
        G['w_gate_up'] = _mm_tn("dw_gate", sv['h3'], dgate, G['w_gate_up'], l, tm=d, tn=dff // 2, tk=512)
        G['w_gate_up'] = _mm_tn("dw_up", sv['h3'], dup, G['w_gate_up'], l, tm=d, tn=dff // 2, tk=512, n0=dff)
        dh = _mm_nt("bwd_w_gate", dgate, W['w_gate_up'], l, tm=1024, tn=512, tk=dff // 2, out_dtype=F32)
        dh = _mm_nt("bwd_w_up", dup, W['w_gate_up'], l, tm=1024, tn=512, tk=dff // 2, out_dtype=F32, k0=dff, add=dh)
        dx, gs['ffn_pre_g'][l] = _norm_bwd("bwd_pre_ffn", sv['x2'], row(ffn_pre_g, l), dh, dx, F32, tmn)
        dxo, gs['x_post_g'][l] = _norm_bwd("bwd_post_xattn", sv['xo'], row(x_post_g, l), dx, None, BF16, tmn)
        G['w_xo'] = _mm_tn("dw_xo", sv['o'], dxo, G['w_xo'], l, tm=d, tn=d, tk=512)
        do = _mm_nt("bwd_w_xo", dxo, W['w_xo'], l, tm=1024, tn=512, tk=d, out_dtype=BF16)
        dq, dkv = _xattn_bwd("xattn_bwd", sv['q'], sv['kv'], do, 512)
        dkv = dkv.astype(BF16)
        G['w_xkv'] = _mm_tn("dw_xkv", memn, dkv, G['w_xkv'], l, tm=d, tn=d, tk=mem.shape[1])
        dmemn = _mm_nt("bwd_w_xkv", dkv, W['w_xkv'], l, tm=mem.shape[1], tn=512, tk=2 * d, out_dtype=F32, add=dmemn)
        G['w_xq'] = _mm_tn("dw_xq", sv['h2'], dq, G['w_xq'], l, tm=d, tn=d, tk=512)
        dh = _mm_nt("bwd_w_xq", dq, W['w_xq'], l, tm=1024, tn=512, tk=d, out_dtype=F32)
        dx, gs['x_pre_g'][l] = _norm_bwd("bwd_pre_xattn", sv['x1'], row(x_pre_g, l), dh, dx, F32, tmn)
        dmix, gs['mix_post_g'][l] = _norm_bwd("bwd_post_mix", sv['mix'], row(mix_post_g, l), dx, None, BF16, tmn)
        G['w_o'] = _mm_tn("dw_o", sv['abc'], dmix, G['w_o'], l, tm=d, tn=d, tk=512)
        dabc = _mm_nt("bwd_w_o", dmix, W['w_o'], l, tm=1024, tn=512, tk=d, out_dtype=F32)
        dz, dgv, dws, dbt, dpw, dpsc, dsnk = _mixer_bwd(
            "mixer_bwd", sv['z'], dabc, cosq, sinq, row(gm_v_g, l), gm_w_s[l], gm_b_s[l].T, pw_bd[l],
            row(pool_scale, l), row(attn_sinks, l))
        gs['gm_v_g'][l] = dgv
        gs['gm_w_s'][l] = dws
        gs['gm_b_s'][l] = dbt.T
        gs['pool_w'][l] = jnp.stack([dpw[64 * g:64 * (g + 1), 64 * g:64 * (g + 1)] for g in range(4)])
        gs['pool_scale'][l] = dpsc
        gs['attn_sinks'][l] = dsnk
        G['w_in'] = _mm_tn("dw_in", sv['h1'], dz, G['w_in'], l, tm=d, tn=dz.shape[1], tk=512)
        dh = _mm_nt("bwd_w_in", dz, W['w_in'], l, tm=1024, tn=512, tk=dz.shape[1], out_dtype=F32)
        dx, gs['mix_pre_g'][l] = _norm_bwd("bwd_pre_mix", sv['x0'], row(mix_pre_g, l), dh, dx, F32, tmn)
    _, dg_mem = _norm_bwd("bwd_mem_norm", mem[0], mem_norm_g.reshape(1, d), dmemn, None, BF16, tmn)
    small_g = []
    for n in SMALL:
        if n == 'mem_norm_g':
            small_g.append(dg_mem.reshape(P[n].shape))
        else:
            small_g.append(jnp.stack([a.reshape(P[n].shape[1:]) for a in gs[n]]))
    return loss_part, dx, G, small_g
```

```python
import functools

import jax
import jax.numpy as jnp
from jax import lax
from jax.experimental import pallas as pl
from jax.experimental.pallas import tpu as pltpu

F32 = jnp.float32
BF16 = jnp.bfloat16
EPS = 1e-6
CHUNK = 128
HEAD = 64
ROPE_THETA = 10000.0
POOL_WINDOWS = (2, 4, 8, 16)
LR, B1, B2, ADAM_EPS, WD, STEP = 0.001, 0.9, 0.999, 1e-08, 0.01, 10
MESH = pl.DeviceIdType.MESH
VMEM_LIMIT = 56 * 1024 * 1024

NAMES = ['x', 'mem', 'positions', 'mem_norm_g', 'mix_pre_g', 'mix_post_g', 'w_in', 'gm_v_g', 'gm_w_s', 'gm_b_s',
         'pool_w', 'pool_scale', 'attn_sinks', 'w_o', 'x_pre_g', 'x_post_g', 'w_xq', 'w_xkv', 'w_xo', 'ffn_pre_g',
         'ffn_post_g', 'w_gate_up', 'w_down']
WEIGHTS = NAMES[3:]
BIG = ['w_in', 'w_o', 'w_xq', 'w_xkv', 'w_xo', 'w_gate_up', 'w_down']
BIG_AXIS = {'w_in': 2, 'w_o': 1, 'w_xq': 1, 'w_xkv': 2, 'w_xo': 1, 'w_gate_up': 2, 'w_down': 1}
SMALL = [n for n in WEIGHTS if n not in BIG]

NN = (((1,), (0,)), ((), ()))
NT = (((1,), (1,)), ((), ()))
TN = (((0,), (0,)), ((), ()))


def _dot(a, b, dims=NN):
    return lax.dot_general(a, b, dims, preferred_element_type=F32)


def _params(sem):
    return pltpu.CompilerParams(dimension_semantics=sem, vmem_limit_bytes=VMEM_LIMIT)


def _rows_tile(rows, limit=256):
    return max(t for t in range(16, limit + 1, 16) if rows % t == 0)


def _mm(name, a, a_spec, b, b_spec, dims, grid, nk, out_shape, out_spec, add=None, add_spec=None, buf=None):
    acc_shape = out_spec.block_shape
    acc_shape = tuple(s for s in acc_shape if s is not None)

    def body(*refs):
        a_ref, b_ref = refs[0], refs[1]
        pos = 2
        add_ref = None
        if add is not None:
            add_ref = refs[pos]
            pos += 1
        if buf is not None:
            pos += 1
        o_ref = refs[pos]
        part = _dot(a_ref[...].astype(BF16), b_ref[...].astype(BF16), dims)
        if nk == 1:
            if add_ref is not None:
                part = part + add_ref[...]
            o_ref[...] = part.astype(o_ref.dtype)
        else:
            acc_ref = refs[pos + 1]
            k = pl.program_id(2)

            @pl.when(k == 0)
            def _():
                acc_ref[...] = part if add_ref is None else part + add_ref[...]

            @pl.when(k > 0)
            def _():
                acc_ref[...] += part

            @pl.when(k == nk - 1)
            def _():
                o_ref[...] = acc_ref[...].astype(o_ref.dtype)

    ops, specs = [a, b], [a_spec, b_spec]
    if add is not None:
        ops.append(add)
        specs.append(add_spec)
    aliases = {}
    if buf is not None:
        aliases = {len(ops): 0}
        ops.append(buf)
        specs.append(pl.BlockSpec(memory_space=pl.ANY))
    return pl.pallas_call(
        body, name=name, grid=grid, in_specs=specs, out_specs=out_spec, out_shape=out_shape,
        scratch_shapes=[pltpu.VMEM(acc_shape, F32)] if nk > 1 else [],
        input_output_aliases=aliases, compiler_params=_params(("parallel", "parallel", "arbitrary")),
    )(*ops)


def _wspec(block, layer, fn):
    return pl.BlockSpec((None,) + block, lambda i, j, k: (layer,) + fn(i, j, k))


def _mm_nn(name, a, w, layer, *, tm, tn, tk, out_dtype, n0=0, n=None, k0=0):
    m, kk = a.shape
    n = w.shape[2] if n is None else n
    tm = min(tm, m)
    nk = kk // tk
    return _mm(name, a, pl.BlockSpec((tm, tk), lambda i, j, k: (i, k)),
               w, _wspec((tk, tn), layer, lambda i, j, k: (k + k0 // tk, j + n0 // tn)), NN,
               (m // tm, n // tn, nk), nk, jax.ShapeDtypeStruct((m, n), out_dtype),
               pl.BlockSpec((tm, tn), lambda i, j, k: (i, j)))


def _mm_nt(name, a, w, layer, *, tm, tn, tk, out_dtype, k0=0, add=None):
    m, kk = a.shape
    n = w.shape[1]
    tm = min(tm, m)
    nk = kk // tk
    ospec = pl.BlockSpec((tm, tn), lambda i, j, k: (i, j))
    return _mm(name, a, pl.BlockSpec((tm, tk), lambda i, j, k: (i, k)),
               w, _wspec((tn, tk), layer, lambda i, j, k: (j, k + k0 // tk)), NT,
               (m // tm, n // tn, nk), nk, jax.ShapeDtypeStruct((m, n), out_dtype), ospec,
               add=add, add_spec=ospec if add is not None else None)


def _mm_tn(name, a, b, *, tm, tn, tk, n_total=None, n0=0, buf=None):
    kk, m = a.shape
    n = b.shape[1]
    n_total = n if n_total is None else n_total
    tk = min(tk, kk)
    nk = kk // tk
    return _mm(name, a, pl.BlockSpec((tk, tm), lambda i, j, k: (k, i)),
               b, pl.BlockSpec((tk, tn), lambda i, j, k: (k, j)), TN,
               (m // tm, n // tn, nk), nk, jax.ShapeDtypeStruct((m, n_total), BF16),
               pl.BlockSpec((tm, tn), lambda i, j, k: (i, j + n0 // tn)), buf=buf)


def _rstd(x):
    return lax.rsqrt(jnp.mean(x * x, axis=-1, keepdims=True) + EPS)


def _row(d):
    return pl.BlockSpec((1, d), lambda i: (0, 0))


def _prenorm(name, x, g, tm):
    m, d = x.shape

    def body(x_ref, g_ref, o_ref):
        xv = x_ref[...]
        o_ref[...] = (xv * _rstd(xv) * g_ref[...]).astype(BF16)

    blk = pl.BlockSpec((tm, d), lambda i: (i, 0))
    return pl.pallas_call(body, name=name, grid=(m // tm,), in_specs=[blk, _row(d)], out_specs=blk,
                          out_shape=jax.ShapeDtypeStruct((m, d), BF16), compiler_params=_params(("parallel",)))(x, g)


def _post_pre(name, x, y, g_post, g_pre, tm):
    m, d = x.shape

    def body(x_ref, y_ref, gp_ref, gn_ref, xo_ref, h_ref):
        yv = y_ref[...]
        xn = x_ref[...] + yv * _rstd(yv) * gp_ref[...]
        xo_ref[...] = xn
        h_ref[...] = (xn * _rstd(xn) * gn_ref[...]).astype(BF16)

    blk = pl.BlockSpec((tm, d), lambda i: (i, 0))
    return pl.pallas_call(
        body, name=name, grid=(m // tm,), in_specs=[blk, blk, _row(d), _row(d)], out_specs=[blk, blk],
        out_shape=[jax.ShapeDtypeStruct((m, d), F32), jax.ShapeDtypeStruct((m, d), BF16)],
        compiler_params=_params(("parallel",)))(x, y, g_post, g_pre)


def _loss_head(name, x, y, g_post, tgt, tm):
    m, d = x.shape

    def body(x_ref, y_ref, g_ref, t_ref, dy_ref, loss_ref):
        yv = y_ref[...]
        err = x_ref[...] + yv * _rstd(yv) * g_ref[...] - t_ref[...]
        dy_ref[...] = err * (1.0 / d)

        @pl.when(pl.program_id(0) == 0)
        def _():
            loss_ref[...] = jnp.zeros_like(loss_ref)

        loss_ref[...] += 0.5 * jnp.sum(jnp.mean(err * err, axis=-1, keepdims=True), axis=0, keepdims=True)

    blk = pl.BlockSpec((tm, d), lambda i: (i, 0))
    return pl.pallas_call(
        body, name=name, grid=(m // tm,), in_specs=[blk, blk, _row(d), blk],
        out_specs=[blk, pl.BlockSpec((1, 1), lambda i: (0, 0))],
        out_shape=[jax.ShapeDtypeStruct((m, d), F32), jax.ShapeDtypeStruct((1, 1), F32)],
        compiler_params=_params(("arbitrary",)))(x, y, g_post, tgt)


def _norm_bwd(name, xin, g, dy, resid, out_dtype, tm):
    m, d = xin.shape

    def body(*refs):
        if resid is None:
            x_ref, g_ref, dy_ref, dx_ref, dg_ref = refs
        else:
            x_ref, g_ref, dy_ref, r_ref, dx_ref, dg_ref = refs
        xv = x_ref[...]
        r = _rstd(xv)
        xh = xv * r
        dyv = dy_ref[...].astype(F32)
        dyg = dyv * g_ref[...]
        dx = r * (dyg - xh * jnp.mean(dyg * xh, axis=-1, keepdims=True))
        if resid is not None:
            dx = dx + r_ref[...]
        dx_ref[...] = dx.astype(dx_ref.dtype)

        @pl.when(pl.program_id(0) == 0)
        def _():
            dg_ref[...] = jnp.zeros_like(dg_ref)

        dg_ref[...] += jnp.sum(dyv * xh, axis=0, keepdims=True)

    blk = pl.BlockSpec((tm, d), lambda i: (i, 0))
    ops = [xin, g, dy] + ([] if resid is None else [resid])
    specs = [blk, _row(d), blk] + ([] if resid is None else [blk])
    return pl.pallas_call(
        body, name=name, grid=(m // tm,), in_specs=specs, out_specs=[blk, _row(d)],
        out_shape=[jax.ShapeDtypeStruct((m, d), out_dtype), jax.ShapeDtypeStruct((1, d), F32)],
        compiler_params=_params(("arbitrary",)))(*ops)


def _adamw(name, g, w, m, v, tr):
    rows, cols = g.shape
    c1 = 1.0 - B1 ** STEP
    c2 = 1.0 - B2 ** STEP

    def body(g_ref, w_ref, m_ref, v_ref, d_ref, mo_ref, vo_ref):
        gv = g_ref[...]
        mn = B1 * m_ref[...] + (1.0 - B1) * gv
        vn = B2 * v_ref[...] + (1.0 - B2) * (gv * gv)
        mo_ref[...] = mn
        vo_ref[...] = vn
        d_ref[...] = -LR * ((mn / c1) / (jnp.sqrt(vn / c2) + ADAM_EPS) + WD * w_ref[...])

    blk = pl.BlockSpec((tr, cols), lambda i: (i, 0))
    sd = jax.ShapeDtypeStruct((rows, cols), F32)
    return pl.pallas_call(body, name=name, grid=(rows // tr,), in_specs=[blk] * 4, out_specs=[blk] * 3,
                          out_shape=[sd, sd, sd], compiler_params=_params(("parallel",)))(g, w, m, v)


def _gelu_parts(x):
    c = 0.7978845608028654
    t = jnp.tanh(c * (x + 0.044715 * (x * x * x)))
    return 0.5 * x * (1.0 + t), t


def _gelu_grad(x, t):
    c = 0.7978845608028654
    return 0.5 * (1.0 + t) + 0.5 * x * (1.0 - t * t) * (c * (1.0 + 3.0 * 0.044715 * x * x))


def _rot_half(x):
    ax = x.ndim - 1
    w = x.shape[ax]
    lane = lax.broadcasted_iota(jnp.int32, x.shape, ax)
    return jnp.where((lane & 63) < 32, pltpu.roll(x, w - 32, ax), pltpu.roll(x, 32, ax))


def _gm_group(gel, g, gv, ws_ref, bt):
    u = gel[:, HEAD * g:HEAD * (g + 1)]
    vg = gel[:, 256 + HEAD * g:256 + HEAD * (g + 1)]
    r = _rstd(vg)
    xh = vg * r
    vn = (xh * gv[:, HEAD * g:HEAD * (g + 1)]).astype(BF16)
    row = lax.broadcasted_iota(jnp.int32, (CHUNK, CHUNK), 0)
    col = lax.broadcasted_iota(jnp.int32, (CHUNK, CHUNK), 1)
    causal = col <= row
    wc = jnp.where(causal, ws_ref[g], 0.0).astype(BF16)
    mixed = _dot(wc, vn) + bt[:, g:g + 1]
    return u, r, xh, vn, wc, causal, mixed


def _lane_select(lane, vals):
    return jnp.where(lane < 64, vals[0], jnp.where(lane < 128, vals[1], jnp.where(lane < 192, vals[2], vals[3])))


def _pool_fwd(pc, pp, ci):
    ext = jnp.concatenate([pp, pc], axis=0)
    s2 = ext + pltpu.roll(ext, 1, 0)
    s4 = s2 + pltpu.roll(s2, 2, 0)
    s8 = s4 + pltpu.roll(s4, 4, 0)
    s16 = s8 + pltpu.roll(s8, 8, 0)
    t1 = ci * CHUNK + lax.broadcasted_iota(jnp.int32, (CHUNK, 1), 0) + 1
    lane = lax.broadcasted_iota(jnp.int32, (1, 256), 1)
    cnt = _lane_select(lane, [jnp.minimum(t1, w).astype(F32) for w in POOL_WINDOWS])
    ssel = _lane_select(lane, [s[CHUNK:] for s in (s2, s4, s8, s16)])
    return ssel / cnt - pc, cnt, lane


def _attn_prep(zc, zpkv, cq, sq, cp, sp, ci):
    q = zc[:, 768:1280]
    kc = zc[:, 1280:1408]
    vc = zc[:, 1408:1536]
    kp = zpkv[:, :128]
    vp = zpkv[:, 128:]
    qr = q * cq + _rot_half(q) * sq
    krc = kc * cq[:, :128] + _rot_half(kc) * sq[:, :128]
    krp = kp * cp + _rot_half(kp) * sp
    kband = jnp.concatenate([krp, krc], axis=0).astype(BF16)
    vband = jnp.concatenate([vp, vc], axis=0).astype(BF16)
    row = lax.broadcasted_iota(jnp.int32, (CHUNK, 2 * CHUNK), 0)
    col = lax.broadcasted_iota(jnp.int32, (CHUNK, 2 * CHUNK), 1)
    valid = ((col < CHUNK) & (col > row) & (ci > 0)) | ((col >= CHUNK) & (col - CHUNK <= row))
    return qr, kband, vband, valid


def _head_probs(qh, kh, valid, sink):
    s = _dot(qh, kh, NT) * (HEAD ** -0.5)
    s = jnp.where(valid, s, -1e30)
    mx = jnp.maximum(jnp.max(s, axis=-1, keepdims=True), sink)
    e = jnp.exp(s - mx)
    es = jnp.exp(sink - mx)
    den = jnp.sum(e, axis=-1, keepdims=True) + es
    return e / den, es / den


def _mixer_specs(nb, rev):
    def cur(i):
        return nb - 1 - i if rev else i

    def prev(i):
        return jnp.maximum(cur(i) - 1, 0)

    full = lambda shape: pl.BlockSpec(shape, lambda i: (0,) * len(shape))
    specs = [
        pl.BlockSpec((CHUNK, 1536), lambda i: (cur(i), 0)),
        pl.BlockSpec((CHUNK, 256), lambda i: (prev(i), 2)),
        pl.BlockSpec((CHUNK, 256), lambda i: (prev(i), 5)),
        pl.BlockSpec((CHUNK, 512), lambda i: (cur(i), 0)),
        pl.BlockSpec((CHUNK, 512), lambda i: (cur(i), 0)),
        pl.BlockSpec((CHUNK, 128), lambda i: (prev(i), 0)),
        pl.BlockSpec((CHUNK, 128), lambda i: (prev(i), 0)),
        full((1, 256)), full((4, CHUNK, CHUNK)), full((CHUNK, 4)), full((256, 256)), full((1, 256)), full((1, 8)),
    ]
    return specs, cur


def _mixer_fwd(name, z, cosq, sinq, gv, ws, bt, pw, psc, snk):
    s = z.shape[0]
    nb = s // CHUNK
    specs, _ = _mixer_specs(nb, False)

    def body(zc_ref, zpp_ref, zpkv_ref, cq_ref, sq_ref, cp_ref, sp_ref, gv_ref, ws_ref, bt_ref, pw_ref, psc_ref,
             snk_ref, o_ref):
        ci = pl.program_id(0)
        zc = zc_ref[...]
        gel, _ = _gelu_parts(zc[:, :512])
        gvv = gv_ref[...]
        btv = bt_ref[...]
        for g in range(4):
            u, _, _, _, _, _, mixed = _gm_group(gel, g, gvv, ws_ref, btv)
            o_ref[:, HEAD * g:HEAD * (g + 1)] = (u * mixed).astype(BF16)
        pp = jnp.where(ci > 0, zpp_ref[...], 0.0)
        pooled, _, _ = _pool_fwd(zc[:, 512:768], pp, ci)
        mp = _dot(pooled.astype(BF16), pw_ref[...].astype(BF16))
        o_ref[:, 256:512] = (mp * psc_ref[...]).astype(BF16)
        qr, kband, vband, valid = _attn_prep(zc, zpkv_ref[...], cq_ref[...], sq_ref[...], cp_ref[...], sp_ref[...], ci)
        snkv = snk_ref[...]
        for h in range(8):
            hk = h // 4
            p, _ = _head_probs(qr[:, HEAD * h:HEAD * (h + 1)].astype(BF16), kband[:, HEAD * hk:HEAD * (hk + 1)],
                               valid, snkv[:, h:h + 1])
            o = _dot(p.astype(BF16), vband[:, HEAD * hk:HEAD * (hk + 1)])
            o_ref[:, 512 + HEAD * h:512 + HEAD * (h + 1)] = o.astype(BF16)

    return pl.pallas_call(
        body, name=name, grid=(nb,), in_specs=specs, out_specs=pl.BlockSpec((CHUNK, 1024), lambda i: (i, 0)),
        out_shape=jax.ShapeDtypeStruct((s, 1024), BF16), compiler_params=_params(("parallel",)),
    )(z, z, z, cosq, sinq, cosq, sinq, gv, ws, bt, pw, psc, snk)


def _mixer_bwd(name, z, dabc, cosq, sinq, gv, ws, bt, pw, psc, snk):
    s = z.shape[0]
    nb = s // CHUNK
    specs, cur = _mixer_specs(nb, True)
    specs = specs + [pl.BlockSpec((CHUNK, 1024), lambda i: (cur(i), 0))]
    full = lambda shape: pl.BlockSpec(shape, lambda i: (0,) * len(shape))
    acc_shapes = [(1, 256), (4, CHUNK, CHUNK), (CHUNK, 4), (256, 256), (1, 256), (1, 8)]

    def body(zc_ref, zpp_ref, zpkv_ref, cq_ref, sq_ref, cp_ref, sp_ref, gv_ref, ws_ref, bt_ref, pw_ref, psc_ref,
             snk_ref, dabc_ref, dz_ref, dgv_ref, dws_ref, dbt_ref, dpw_ref, dpsc_ref, dsnk_ref,
             cpool, ck, cv, dq_s, dkv_s):
        step = pl.program_id(0)
        ci = nb - 1 - step

        @pl.when(step == 0)
        def _():
            for r in (dgv_ref, dws_ref, dbt_ref, dpw_ref, dpsc_ref, dsnk_ref, cpool, ck, cv):
                r[...] = jnp.zeros_like(r)

        zc = zc_ref[...]
        dabc = dabc_ref[...]
        zg = zc[:, :512]
        gel, th = _gelu_parts(zg)
        gp = _gelu_grad(zg, th)
        gvv = gv_ref[...]
        btv = bt_ref[...]
        lane4 = lax.broadcasted_iota(jnp.int32, (CHUNK, 4), 1)
        dbt = jnp.zeros((CHUNK, 4), F32)
        for g in range(4):
            lo, hi = HEAD * g, HEAD * (g + 1)
            u, r, xh, vn, wc, causal, mixed = _gm_group(gel, g, gvv, ws_ref, btv)
            da = dabc[:, lo:hi]
            dm = da * u
            dmb = dm.astype(BF16)
            dws_ref[g] += jnp.where(causal, _dot(dmb, vn, NT), 0.0)
            dbt = dbt + jnp.where(lane4 == g, jnp.sum(dm, axis=-1, keepdims=True), 0.0)
            dvn = _dot(wc, dmb, TN)
            dgv_ref[:, lo:hi] += jnp.sum(dvn * xh, axis=0, keepdims=True)
            dxh = dvn * gvv[:, lo:hi]
            dvg = r * (dxh - xh * jnp.mean(dxh * xh, axis=-1, keepdims=True))
            dz_ref[:, lo:hi] = (da * mixed * gp[:, lo:hi]).astype(BF16)
            dz_ref[:, 256 + lo:256 + hi] = (dvg * gp[:, 256 + lo:256 + hi]).astype(BF16)
        dbt_ref[...] += dbt
        pc = zc[:, 512:768]
        pp = jnp.where(ci > 0, zpp_ref[...], 0.0)
        pooled, cnt, lane = _pool_fwd(pc, pp, ci)
        pwb = pw_ref[...].astype(BF16)
        pooled_b = pooled.astype(BF16)
        mp = _dot(pooled_b, pwb)
        db = dabc[:, 256:512]
        dpsc_ref[...] += jnp.sum(db * mp, axis=0, keepdims=True)
        dmpb = (db * psc_ref[...]).astype(BF16)
        dpw_ref[...] += _dot(pooled_b, dmpb, TN)
        dpooled = _dot(dmpb, pwb, NT)
        davg = dpooled / cnt
        zero = jnp.zeros((CHUNK, 256), F32)
        d2, d4, d8, d16 = [jnp.concatenate([zero, jnp.where((lane >= 64 * k) & (lane < 64 * (k + 1)), davg, 0.0)],
                                           axis=0) for k in range(4)]
        g8 = d8 + d16 + pltpu.roll(d16, 2 * CHUNK - 8, 0)
        g4 = d4 + g8 + pltpu.roll(g8, 2 * CHUNK - 4, 0)
        g2 = d2 + g4 + pltpu.roll(g4, 2 * CHUNK - 2, 0)
        ge = g2 + pltpu.roll(g2, 2 * CHUNK - 1, 0)
        dz_ref[:, 512:768] = (ge[CHUNK:] - dpooled + cpool[...]).astype(BF16)
        cpool[...] = ge[:CHUNK]
        cq = cq_ref[...]
        sq = sq_ref[...]
        qr, kband, vband, valid = _attn_prep(zc, zpkv_ref[...], cq, sq, cp_ref[...], sp_ref[...], ci)
        snkv = snk_ref[...]
        lane8 = lax.broadcasted_iota(jnp.int32, (1, 8), 1)
        dsnk = jnp.zeros((1, 8), F32)
        for hk in range(2):
            kh = kband[:, HEAD * hk:HEAD * (hk + 1)]
            vh = vband[:, HEAD * hk:HEAD * (hk + 1)]
            dkh = jnp.zeros((2 * CHUNK, HEAD), F32)
            dvh = jnp.zeros((2 * CHUNK, HEAD), F32)
            for h in range(4 * hk, 4 * hk + 4):
                qh = qr[:, HEAD * h:HEAD * (h + 1)].astype(BF16)
                p, ps = _head_probs(qh, kh, valid, snkv[:, h:h + 1])
                dob = dabc[:, 512 + HEAD * h:512 + HEAD * (h + 1)].astype(BF16)
                dp = _dot(dob, vh, NT)
                dd = jnp.sum(p * dp, axis=-1, keepdims=True)
                dsnk = dsnk + jnp.where(lane8 == h, jnp.sum(-ps * dd, axis=0, keepdims=True), 0.0)
                dsb = (p * (dp - dd) * (HEAD ** -0.5)).astype(BF16)
                dq_s[:, HEAD * h:HEAD * (h + 1)] = _dot(dsb, kh)
                dkh = dkh + _dot(dsb, qh, TN)
                dvh = dvh + _dot(p.astype(BF16), dob, TN)
            dkv_s[:, HEAD * hk:HEAD * (hk + 1)] = dkh
            dkv_s[:, 128 + HEAD * hk:128 + HEAD * (hk + 1)] = dvh
        dsnk_ref[...] += dsnk
        dqr = dq_s[...]
        dz_ref[:, 768:1280] = (dqr * cq + _rot_half(dqr * sq)).astype(BF16)
        dkv = dkv_s[...]
        dkr = dkv[CHUNK:, :128] + ck[...]
        dz_ref[:, 1280:1408] = (dkr * cq[:, :128] + _rot_half(dkr * sq[:, :128])).astype(BF16)
        dz_ref[:, 1408:1536] = (dkv[CHUNK:, 128:] + cv[...]).astype(BF16)
        ck[...] = dkv[:CHUNK, :128]
        cv[...] = dkv[:CHUNK, 128:]

    return pl.pallas_call(
        body, name=name, grid=(nb,), in_specs=specs,
        out_specs=[pl.BlockSpec((CHUNK, 1536), lambda i: (cur(i), 0))] + [full(a) for a in acc_shapes],
        out_shape=[jax.ShapeDtypeStruct((s, 1536), BF16)] + [jax.ShapeDtypeStruct(a, F32) for a in acc_shapes],
        scratch_shapes=[pltpu.VMEM((CHUNK, 256), F32), pltpu.VMEM((CHUNK, 128), F32), pltpu.VMEM((CHUNK, 128), F32),
                        pltpu.VMEM((CHUNK, 512), F32), pltpu.VMEM((2 * CHUNK, 256), F32)],
        compiler_params=_params(("arbitrary",)),
    )(z, z, z, cosq, sinq, cosq, sinq, gv, ws, bt, pw, psc, snk, dabc)


def _xattn_probs(qh, kh):
    s = _dot(qh, kh, NT) * (256 ** -0.5)
    e = jnp.exp(s - jnp.max(s, axis=-1, keepdims=True))
    return e / jnp.sum(e, axis=-1, keepdims=True)


def _xattn_fwd(name, q, kv, tq):
    s, d = q.shape
    mlen = kv.shape[0]

    def body(q_ref, kv_ref, o_ref):
        for h in range(4):
            lo, hi = 256 * h, 256 * (h + 1)
            p = _xattn_probs(q_ref[:, lo:hi], kv_ref[:, lo:hi])
            o_ref[:, lo:hi] = _dot(p.astype(BF16), kv_ref[:, d + lo:d + hi]).astype(BF16)

    blk = pl.BlockSpec((tq, d), lambda i: (i, 0))
    return pl.pallas_call(body, name=name, grid=(s // tq,),
                          in_specs=[blk, pl.BlockSpec((mlen, 2 * d), lambda i: (0, 0))], out_specs=blk,
                          out_shape=jax.ShapeDtypeStruct((s, d), BF16), compiler_params=_params(("parallel",)))(q, kv)


def _xattn_bwd(name, q, kv, do, tq):
    s, d = q.shape
    mlen = kv.shape[0]

    def body(q_ref, kv_ref, do_ref, dq_ref, dkv_ref):
        @pl.when(pl.program_id(0) == 0)
        def _():
            dkv_ref[...] = jnp.zeros_like(dkv_ref)

        for h in range(4):
            lo, hi = 256 * h, 256 * (h + 1)
            qh = q_ref[:, lo:hi]
            kh = kv_ref[:, lo:hi]
            vh = kv_ref[:, d + lo:d + hi]
            doh = do_ref[:, lo:hi]
            p = _xattn_probs(qh, kh)
            dp = _dot(doh, vh, NT)
            dsb = (p * (dp - jnp.sum(p * dp, axis=-1, keepdims=True)) * (256 ** -0.5)).astype(BF16)
            dq_ref[:, lo:hi] = _dot(dsb, kh).astype(BF16)
            dkv_ref[:, lo:hi] += _dot(dsb, qh, TN)
            dkv_ref[:, d + lo:d + hi] += _dot(p.astype(BF16), doh, TN)

    blk = pl.BlockSpec((tq, d), lambda i: (i, 0))
    kvb = pl.BlockSpec((mlen, 2 * d), lambda i: (0, 0))
    return pl.pallas_call(
        body, name=name, grid=(s // tq,), in_specs=[blk, kvb, blk], out_specs=[blk, kvb],
        out_shape=[jax.ShapeDtypeStruct((s, d), BF16), jax.ShapeDtypeStruct((mlen, 2 * d), F32)],
        compiler_params=_params(("arbitrary",)))(q, kv, do)


def _ffn_up(name, h, wgu, layer, tm, tn):
    s, d = h.shape
    dff = wgu.shape[2] // 2
    nj = dff // tn

    def body(h_ref, wg_ref, wu_ref, g_ref, u_ref, a_ref):
        hv = h_ref[...]
        gate = _dot(hv, wg_ref[...])
        up = _dot(hv, wu_ref[...])
        g_ref[...] = gate.astype(BF16)
        u_ref[...] = up.astype(BF16)
        a_ref[...] = (gate / (1.0 + jnp.exp(-gate)) * up).astype(BF16)

    ob = pl.BlockSpec((tm, tn), lambda j, i: (i, j))
    sd = jax.ShapeDtypeStruct((s, dff), BF16)
    return pl.pallas_call(
        body, name=name, grid=(nj, s // tm),
        in_specs=[pl.BlockSpec((tm, d), lambda j, i: (i, 0)),
                  pl.BlockSpec((None, d, tn), lambda j, i: (layer, 0, j)),
                  pl.BlockSpec((None, d, tn), lambda j, i: (layer, 0, j + nj))],
        out_specs=[ob, ob, ob], out_shape=[sd, sd, sd], compiler_params=_params(("parallel", "parallel")),
    )(h, wgu, wgu)


def _ffn_act_bwd(name, dfn, wdown, layer, gate, up, tm, tn):
    s, d = dfn.shape
    dff = gate.shape[1]

    def body(df_ref, wd_ref, g_ref, u_ref, dg_ref, du_ref):
        dact = _dot(df_ref[...], wd_ref[...], NT)
        gate = g_ref[...].astype(F32)
        sig = 1.0 / (1.0 + jnp.exp(-gate))
        du_ref[...] = (dact * gate * sig).astype(BF16)
        dg_ref[...] = (dact * u_ref[...].astype(F32) * sig * (1.0 + gate * (1.0 - sig))).astype(BF16)

    ob = pl.BlockSpec((tm, tn), lambda j, i: (i, j))
    sd = jax.ShapeDtypeStruct((s, dff), BF16)
    return pl.pallas_call(
        body, name=name, grid=(dff // tn, s // tm),
        in_specs=[pl.BlockSpec((tm, d), lambda j, i: (i, 0)),
                  pl.BlockSpec((None, tn, d), lambda j, i: (layer, j, 0)), ob, ob],
        out_specs=[ob, ob], out_shape=[sd, sd], compiler_params=_params(("parallel", "parallel")),
    )(dfn, wdown, gate, up)


def _place():
    return lax.axis_index("x"), lax.axis_index("y"), lax.axis_index("c")


def _other_chips(x, y):
    return [(1 - x, y), (x, 1 - y), (1 - x, 1 - y)]


def _region(ref, axis, chip, size):
    start = pl.multiple_of(chip * size, size)
    if axis == 1:
        return ref.at[:, pl.ds(start, size), :]
    return ref.at[:, :, pl.ds(start, size)]


ANY = pl.BlockSpec(memory_space=pl.ANY)


HBM = pl.BlockSpec(memory_space=pltpu.HBM)
SEM = pl.BlockSpec(memory_space=pltpu.SEMAPHORE)
EFFECT = pltpu.SideEffectType.DATAFLOW_SIDE_EFFECTING


def _in_hbm(a):
    return pltpu.with_memory_space_constraint(a, pltpu.HBM)


def _split_start(name, srcs, lands, ncopies, plan):
    ns, nl = len(srcs), len(lands)

    def body(*refs):
        src, land = refs[:ns], refs[ns:ns + nl]
        send, recv = refs[ns + nl], refs[ns + nl + 1]
        token = refs[-1]
        x, y, c = _place()
        for k, (s_ref, d_ref, peer, _) in enumerate(plan(src, land, x, y, c)):
            pltpu.make_async_remote_copy(src_ref=s_ref, dst_ref=d_ref, send_sem=send.at[k], recv_sem=recv.at[k],
                                         device_id=peer, device_id_type=MESH).start()
        token[...] = jnp.zeros_like(token)

    ops = list(srcs) + list(lands)
    out = pl.pallas_call(
        body, name=name,
        out_shape=(pltpu.SemaphoreType.DMA((ncopies,)), pltpu.SemaphoreType.DMA((ncopies,)),
                   *[pltpu.HBM(a.shape, a.dtype) for a in ops], jax.ShapeDtypeStruct((8, 128), F32)),
        in_specs=(HBM,) * (ns + nl), out_specs=(SEM, SEM) + (HBM,) * (ns + nl) + (pl.BlockSpec(memory_space=pltpu.VMEM),),
        input_output_aliases={i: 2 + i for i in range(ns + nl)},
        compiler_params=pltpu.CompilerParams(has_side_effects=EFFECT),
    )(*[_in_hbm(a) for a in ops])
    return out[0], out[1], list(out[2:2 + ns]), list(out[2 + ns:2 + ns + nl]), out[-1]


def _split_wait(name, send, recv, srcs, lands, after, plan):
    ns, nl = len(srcs), len(lands)

    def body(*refs):
        src, land = refs[:ns], refs[ns:ns + nl]
        send_ref, recv_ref = refs[ns + nl], refs[ns + nl + 1]
        x, y, c = _place()
        for k, (s_ref, _, _, got) in enumerate(plan(src, land, x, y, c)):
            cp = pltpu.make_async_remote_copy(src_ref=s_ref, dst_ref=got, send_sem=send_ref.at[k],
                                              recv_sem=recv_ref.at[k], device_id=(x, y, c), device_id_type=MESH)
            cp.wait_send()
            cp.wait_recv()

    ops = list(srcs) + list(lands)
    out = pl.pallas_call(
        body, name=name, out_shape=tuple(pltpu.HBM(a.shape, a.dtype) for a in ops),
        in_specs=(HBM,) * (ns + nl) + (SEM, SEM, ANY), out_specs=(HBM,) * (ns + nl),
        input_output_aliases={i: i for i in range(ns + nl)},
        compiler_params=pltpu.CompilerParams(has_side_effects=EFFECT),
    )(*ops, send, recv, after)
    return list(out[:ns]), list(out[ns:])


def _gather_plan(axes, sizes, layer):
    def plan(src, land, x, y, c):
        me = 2 * x + y
        out = []
        for t in range(len(src)):
            for px, py in _other_chips(x, y):
                out.append((src[t].at[pl.ds(layer, 1)], _region(land[t], axes[t], me, sizes[t]), (px, py, c),
                            _region(land[t], axes[t], 2 * px + py, sizes[t])))
        return out
    return plan


def _scatter_plan(axes, sizes):
    def plan(src, land, x, y, c):
        out = []
        for t in range(len(src)):
            for k, (px, py) in enumerate(_other_chips(x, y)):
                out.append((_region(src[t], axes[t], 2 * px + py, sizes[t]).at[0], land[t].at[k], (px, py, c),
                            land[t].at[k]))
        return out
    return plan


def _pair_plan(src, land, x, y, c):
    return [(src[t], land[t], (x, y, 1 - c), land[t]) for t in range(len(src))]


def _place_own(shards, lands, axes, sizes):
    n = len(shards)
    nl = len(lands)

    def body(*refs):
        sh, land = refs[:n], refs[n:n + nl]
        lsem = refs[-1]
        x, y, _ = _place()
        cps = [pltpu.make_async_copy(sh[i % n].at[pl.ds(i // n, 1)],
                                     _region(land[i], axes[i % n], 2 * x + y, sizes[i % n]), lsem.at[i])
               for i in range(nl)]
        for cp in cps:
            cp.start()
        for cp in cps:
            cp.wait()

    return pl.pallas_call(
        body, name="place_own_shard", in_specs=[ANY] * (n + nl), out_specs=[ANY] * nl,
        out_shape=[jax.ShapeDtypeStruct(a.shape, a.dtype) for a in lands],
        input_output_aliases={n + i: i for i in range(nl)}, scratch_shapes=[pltpu.SemaphoreType.DMA((nl,))],
    )(*shards, *lands)


def _chip_sum(name, g, slots, axis, chip):
    _, r, cs = slots.shape
    tr = _rows_tile(r)
    nb = r // tr

    def body(m_ref, g_ref, s_ref, o_ref):
        acc = g_ref[...].astype(F32)
        for k in range(3):
            acc = acc + s_ref[k].astype(F32)
        o_ref[...] = acc

    if axis == 1:
        gspec = pl.BlockSpec((tr, cs), lambda i, m: (m[0] * nb + i, 0))
    else:
        gspec = pl.BlockSpec((tr, cs), lambda i, m: (i, m[0]))
    return pl.pallas_call(
        body, name=name,
        grid_spec=pltpu.PrefetchScalarGridSpec(
            num_scalar_prefetch=1, grid=(nb,),
            in_specs=[gspec, pl.BlockSpec((3, tr, cs), lambda i, m: (0, i, 0))],
            out_specs=pl.BlockSpec((tr, cs), lambda i, m: (i, 0))),
        out_shape=jax.ShapeDtypeStruct((r, cs), F32), compiler_params=_params(("parallel",)),
    )(chip, g, slots)


def _pair_adamw(name, mine, theirs, w, m, v, layer, bufs):
    r, cs = mine.shape
    tr = _rows_tile(r)
    c1 = 1.0 - B1 ** STEP
    c2 = 1.0 - B2 ** STEP

    def body(a_ref, b_ref, w_ref, m_ref, v_ref, _g, _d, _m, _v, g_ref, d_ref, mo_ref, vo_ref):
        gv = a_ref[...] + b_ref[...]
        mn = B1 * m_ref[...] + (1.0 - B1) * gv
        vn = B2 * v_ref[...] + (1.0 - B2) * (gv * gv)
        g_ref[...] = gv
        mo_ref[...] = mn
        vo_ref[...] = vn
        d_ref[...] = -LR * ((mn / c1) / (jnp.sqrt(vn / c2) + ADAM_EPS) + WD * w_ref[...])

    blk = pl.BlockSpec((tr, cs), lambda i: (i, 0))
    lay = pl.BlockSpec((None, tr, cs), lambda i: (layer, i, 0))
    return pl.pallas_call(
        body, name=name, grid=(r // tr,), in_specs=[blk, blk, lay, lay, lay] + [ANY] * 4, out_specs=[lay] * 4,
        out_shape=[jax.ShapeDtypeStruct(b.shape, b.dtype) for b in bufs],
        input_output_aliases={5 + i: i for i in range(4)}, compiler_params=_params(("parallel",)),
    )(mine, theirs, w, m, v, *bufs)


def _allreduce_small(p):
    rows = p.shape[0]

    def body(p_ref, o_ref, gath, send, recv):
        x, y, c = _place()
        me = 4 * x + 2 * y + c

        def flip(v, bit):
            return 1 - v if bit else v

        gath[me] = p_ref[...]
        cps = []
        for k in range(1, 8):
            peer = (flip(x, k & 4), flip(y, k & 2), flip(c, k & 1))
            cps.append(pltpu.make_async_remote_copy(src_ref=p_ref, dst_ref=gath.at[me], send_sem=send.at[k - 1],
                                                    recv_sem=recv.at[k - 1], device_id=peer, device_id_type=MESH))
        for cp in cps:
            cp.start()
        for k in range(1, 8):
            slot = gath.at[4 * flip(x, k & 4) + 2 * flip(y, k & 2) + flip(c, k & 1)]
            pltpu.make_async_remote_copy(src_ref=slot, dst_ref=slot, send_sem=send.at[k - 1], recv_sem=recv.at[k - 1],
                                         device_id=(x, y, c), device_id_type=MESH).wait_recv()
        for cp in cps:
            cp.wait_send()
        acc = gath[0]
        for k in range(1, 8):
            acc = acc + gath[k]
        o_ref[...] = acc

    vm = pl.BlockSpec(memory_space=pltpu.VMEM)
    return pl.pallas_call(
        body, name="allreduce_small", in_specs=[vm], out_specs=vm, out_shape=jax.ShapeDtypeStruct(p.shape, F32),
        scratch_shapes=[pltpu.VMEM((8, rows, 128), F32), pltpu.SemaphoreType.DMA((7,)), pltpu.SemaphoreType.DMA((7,))],
        compiler_params=pltpu.CompilerParams(vmem_limit_bytes=VMEM_LIMIT),
    )(p)


def _pack(parts):
    flat = []
    for p in parts:
        v = p.reshape(-1).astype(F32)
        flat.append(jnp.pad(v, (0, (-v.shape[0]) % 128)))
    v = jnp.concatenate(flat)
    v = jnp.pad(v, (0, (-v.shape[0]) % (512 * 128)))
    return v.reshape(-1, 128)


def _unpack(buf, like):
    v = buf.reshape(-1)
    out, off = [], 0
    for p in like:
        nelem = 1
        for s in p.shape:
            nelem *= s
        out.append(v[off:off + nelem].reshape(p.shape))
        off += nelem + (-nelem) % 128
    return out


def kernel(x, mem, positions, mem_norm_g, mix_pre_g, mix_post_g, w_in, gm_v_g, gm_w_s, gm_b_s, pool_w, pool_scale, attn_sinks, w_o, x_pre_g, x_post_g, w_xq, w_xkv, w_xo, ffn_pre_g, ffn_post_g, w_gate_up, w_down, loss_target, m_mem_norm_g, m_mix_pre_g, m_mix_post_g, m_w_in, m_gm_v_g, m_gm_w_s, m_gm_b_s, m_pool_w, m_pool_scale, m_attn_sinks, m_w_o, m_x_pre_g, m_x_post_g, m_w_xq, m_w_xkv, m_w_xo, m_ffn_pre_g, m_ffn_post_g, m_w_gate_up, m_w_down, v_mem_norm_g, v_mix_pre_g, v_mix_post_g, v_w_in, v_gm_v_g, v_gm_w_s, v_gm_b_s, v_pool_w, v_pool_scale, v_attn_sinks, v_w_o, v_x_pre_g, v_x_post_g, v_w_xq, v_w_xkv, v_w_xo, v_ffn_pre_g, v_ffn_post_g, v_w_gate_up, v_w_down):
    args = (x, mem, positions, mem_norm_g, mix_pre_g, mix_post_g, w_in, gm_v_g, gm_w_s, gm_b_s, pool_w, pool_scale, attn_sinks, w_o, x_pre_g, x_post_g, w_xq, w_xkv, w_xo, ffn_pre_g, ffn_post_g, w_gate_up, w_down)
    moms_m = (m_mem_norm_g, m_mix_pre_g, m_mix_post_g, m_w_in, m_gm_v_g, m_gm_w_s, m_gm_b_s, m_pool_w, m_pool_scale, m_attn_sinks, m_w_o, m_x_pre_g, m_x_post_g, m_w_xq, m_w_xkv, m_w_xo, m_ffn_pre_g, m_ffn_post_g, m_w_gate_up, m_w_down)
    moms_v = (v_mem_norm_g, v_mix_pre_g, v_mix_post_g, v_w_in, v_gm_v_g, v_gm_w_s, v_gm_b_s, v_pool_w, v_pool_scale, v_attn_sinks, v_w_o, v_x_pre_g, v_x_post_g, v_w_xq, v_w_xkv, v_w_xo, v_ffn_pre_g, v_ffn_post_g, v_w_gate_up, v_w_down)
    P = dict(zip(NAMES, args))
    P['loss_target'] = loss_target
    M = dict(zip(WEIGHTS, moms_m))
    V = dict(zip(WEIGHTS, moms_v))
    depth = w_in.shape[0]
    nbig = len(BIG)
    axes = [BIG_AXIS[n] for n in BIG]
    sizes = [P[n].shape[a] for n, a in zip(BIG, axes)]
    full_shape = lambda n: tuple(4 * d if i == BIG_AXIS[n] else d for i, d in enumerate(P[n].shape))[1:]
    chip = (2 * lax.axis_index("x") + lax.axis_index("y")).astype(jnp.int32).reshape(1)

    shards = [P[n].astype(BF16) for n in BIG]
    lands = _place_own(shards, [lax.empty((1,) + full_shape(n), BF16) for _ in range(depth) for n in BIG], axes, sizes)
    gathers = []
    for l in range(depth):
        send, recv, shards, land_l, _ = _split_start("gather_start%d" % l, shards, lands[l * nbig:(l + 1) * nbig],
                                                     3 * nbig, _gather_plan(axes, sizes, l))
        gathers.append((send, recv, land_l))
    chain = {'shards': shards}

    def weights_of(l, after):
        send, recv, land_l = gathers[l]
        chain['shards'], land_l = _split_wait("gather_wait%d" % l, send, recv, chain['shards'], land_l, after,
                                              _gather_plan(axes, sizes, l))
        return dict(zip(BIG, land_l))

    outs = {n: [lax.empty(P[n].shape, F32) for _ in range(4)] for n in BIG}
    scatters, pairs = {}, {}

    def finish_scatter(l, after):
        send, recv, g_l, slots = scatters.pop(l)
        g_l, slots = _split_wait("scatter_wait%d" % l, send, recv, g_l, slots, after, _scatter_plan(axes, sizes))
        mine = [_chip_sum("chip_sum_" + n, g.reshape(g.shape[1:]), sl, a, chip)
                for n, g, sl, a in zip(BIG, g_l, slots, axes)]
        send, recv, mine, theirs, tok = _split_start("pair_start%d" % l, mine, [lax.empty(a.shape, F32) for a in mine],
                                                     nbig, _pair_plan)
        pairs[l] = (send, recv, mine, theirs)
        return tok[:1, :1]

    def finish_pair(l, after):
        send, recv, mine, theirs = pairs.pop(l)
        mine, theirs = _split_wait("pair_wait%d" % l, send, recv, mine, theirs, after, _pair_plan)
        for n, a, b in zip(BIG, mine, theirs):
            outs[n] = _pair_adamw("adamw_" + n, a, b, P[n], M[n], V[n], l, outs[n])

    def grads_of(l, g_l, after):
        srcs = [g_l[n].reshape((1,) + g_l[n].shape) for n in BIG]
        send, recv, srcs, slots, tok = _split_start("scatter_start%d" % l, srcs,
                                                    [lax.empty((3,) + P[n].shape[1:], BF16) for n in BIG],
                                                    3 * nbig, _scatter_plan(axes, sizes))
        scatters[l] = (send, recv, srcs, slots)
        tok = tok[:1, :1]
        if l + 1 < depth:
            tok = tok + finish_scatter(l + 1, after)
        if l + 2 < depth:
            finish_pair(l + 2, after)
        return tok

    loss_part, dx, small_g = _fwd_bwd(P, weights_of, grads_of)
    loss = lax.psum(loss_part[0, 0], ("x", "y", "c"))
    grad_x = dx.reshape(x.shape)

    small_like = [P[n] for n in SMALL]
    gsum = _allreduce_small(_pack(small_g))
    dlt, mn, vn = _adamw("adamw_small", gsum, _pack(small_like), _pack([M[n] for n in SMALL]),
                         _pack([V[n] for n in SMALL]), 512)
    grads, deltas, new_m, new_v = {}, {}, {}, {}
    for name_map, buf in ((grads, gsum), (deltas, dlt), (new_m, mn), (new_v, vn)):
        for n, a in zip(SMALL, _unpack(buf, small_like)):
            name_map[n] = a

    finish_scatter(0, dlt)
    for l in range(min(depth, 2) - 1, -1, -1):
        finish_pair(l, dlt)
    for n in BIG:
        grads[n], deltas[n], new_m[n], new_v[n] = outs[n]

    return (loss, grad_x, *[grads[n] for n in WEIGHTS], *[deltas[n] for n in WEIGHTS],
            *[new_m[n] for n in WEIGHTS], *[new_v[n] for n in WEIGHTS])


def _fwd_bwd(P, weights_of, grads_of):
    (x, mem, positions, mem_norm_g, mix_pre_g, mix_post_g, w_in, gm_v_g, gm_w_s, gm_b_s, pool_w, pool_scale, attn_sinks,
     w_o, x_pre_g, x_post_g, w_xq, w_xkv, w_xo, ffn_pre_g, ffn_post_g, w_gate_up, w_down) = [P[n] for n in NAMES]
    x0 = x[0]
    s, d = x0.shape
    depth = w_in.shape[0]
    tgt = P['loss_target'][0]
    tmn = 256

    half = HEAD // 2
    inv = ROPE_THETA ** (-jnp.arange(half, dtype=F32) / half)
    ang = positions[0].astype(F32)[:, None] * inv
    cos, sin = jnp.cos(ang), jnp.sin(ang)
    cosq = jnp.tile(jnp.concatenate([cos, cos], axis=-1), (1, 8))
    sinq = jnp.tile(jnp.concatenate([-sin, sin], axis=-1), (1, 8))

    row = lambda a, l: a[l].reshape(1, -1)
    memn = _prenorm("mem_norm", mem[0], mem_norm_g.reshape(1, d), tmn)
    pw_bd = []
    for l in range(depth):
        bd = jnp.zeros((256, 256), F32)
        for g in range(4):
            bd = lax.dynamic_update_slice(bd, pool_w[l, g], (64 * g, 64 * g))
        pw_bd.append(bd)

    saved, weights = [], []
    xc = x0
    h = _prenorm("pre_norm0", x0, row(mix_pre_g, 0), tmn)
    for l in range(depth):
        W = weights_of(l, xc)
        weights.append(W)
        dff = W['w_down'].shape[1]
        sv = {'x0': xc, 'h1': h}
        z = _mm_nn("fwd_w_in", h, W['w_in'], 0, tm=1024, tn=512, tk=d, out_dtype=F32)
        abc = _mixer_fwd("mixer_fwd", z, cosq, sinq, row(gm_v_g, l), gm_w_s[l], gm_b_s[l].T, pw_bd[l],
                         row(pool_scale, l), row(attn_sinks, l))
        mix = _mm_nn("fwd_w_o", abc, W['w_o'], 0, tm=1024, tn=512, tk=d, out_dtype=F32)
        xc, h = _post_pre("post_mix", xc, mix, row(mix_post_g, l), row(x_pre_g, l), tmn)
        sv.update(z=z, abc=abc, mix=mix, x1=xc, h2=h)
        q = _mm_nn("fwd_w_xq", h, W['w_xq'], 0, tm=1024, tn=512, tk=d, out_dtype=BF16)
        kv = _mm_nn("fwd_w_xkv", memn, W['w_xkv'], 0, tm=256, tn=512, tk=d, out_dtype=BF16)
        o = _xattn_fwd("xattn_fwd", q, kv, 512)
        xo = _mm_nn("fwd_w_xo", o, W['w_xo'], 0, tm=1024, tn=512, tk=d, out_dtype=F32)
        xc, h = _post_pre("post_xattn", xc, xo, row(x_post_g, l), row(ffn_pre_g, l), tmn)
        sv.update(q=q, kv=kv, o=o, xo=xo, x2=xc, h3=h)
        gate, up, act = _ffn_up("ffn_up", h, W['w_gate_up'], 0, 512, dff // 2)
        f = _mm_nn("fwd_w_down", act, W['w_down'], 0, tm=1024, tn=512, tk=dff // 2, out_dtype=F32)
        sv.update(gate=gate, up=up, act=act, f=f)
        if l + 1 < depth:
            xc, h = _post_pre("post_ffn", xc, f, row(ffn_post_g, l), row(mix_pre_g, l + 1), tmn)
        saved.append(sv)
    dx, loss_part = _loss_head("loss_head", xc, saved[-1]['f'], row(ffn_post_g, depth - 1), tgt, tmn)

    gs = {n: [None] * depth for n in SMALL if n != 'mem_norm_g'}
    dmemn = None
    tok = jnp.zeros((1, 1), F32)
    for l in reversed(range(depth)):
        sv, W, G = saved[l], weights[l], {}
        dfn, gs['ffn_post_g'][l] = _norm_bwd("bwd_post_ffn", sv['f'], row(ffn_post_g, l) + tok, dx, None, BF16, tmn)
        G['w_down'] = _mm_tn("dw_down", sv['act'], dfn, tm=dff // 2, tn=d, tk=512)
        dgate, dup = _ffn_act_bwd("ffn_act_bwd", dfn, W['w_down'], 0, sv['gate'], sv['up'], 512, dff // 2)
        gu = _mm_tn("dw_gate", sv['h3'], dgate, tm=d, tn=dff // 2, tk=512, n_total=2 * dff)
        G['w_gate_up'] = _mm_tn("dw_up", sv['h3'], dup, tm=d, tn=dff // 2, tk=512, n_total=2 * dff, n0=dff, buf=gu)
        dh = _mm_nt("bwd_w_gate", dgate, W['w_gate_up'], 0, tm=1024, tn=512, tk=dff // 2, out_dtype=F32)
        dh = _mm_nt("bwd_w_up", dup, W['w_gate_up'], 0, tm=1024, tn=512, tk=dff // 2, out_dtype=F32, k0=dff, add=dh)
        dx, gs['ffn_pre_g'][l] = _norm_bwd("bwd_pre_ffn", sv['x2'], row(ffn_pre_g, l), dh, dx, F32, tmn)
        dxo, gs['x_post_g'][l] = _norm_bwd("bwd_post_xattn", sv['xo'], row(x_post_g, l), dx, None, BF16, tmn)
        G['w_xo'] = _mm_tn("dw_xo", sv['o'], dxo, tm=d, tn=d, tk=512)
        do = _mm_nt("bwd_w_xo", dxo, W['w_xo'], 0, tm=1024, tn=512, tk=d, out_dtype=BF16)
        dq, dkv = _xattn_bwd("xattn_bwd", sv['q'], sv['kv'], do, 512)
        dkv = dkv.astype(BF16)
        G['w_xkv'] = _mm_tn("dw_xkv", memn, dkv, tm=d, tn=d, tk=mem.shape[1])
        dmemn = _mm_nt("bwd_w_xkv", dkv, W['w_xkv'], 0, tm=mem.shape[1], tn=512, tk=2 * d, out_dtype=F32, add=dmemn)
        G['w_xq'] = _mm_tn("dw_xq", sv['h2'], dq, tm=d, tn=d, tk=512)
        dh = _mm_nt("bwd_w_xq", dq, W['w_xq'], 0, tm=1024, tn=512, tk=d, out_dtype=F32)
        dx, gs['x_pre_g'][l] = _norm_bwd("bwd_pre_xattn", sv['x1'], row(x_pre_g, l), dh, dx, F32, tmn)
        dmix, gs['mix_post_g'][l] = _norm_bwd("bwd_post_mix", sv['mix'], row(mix_post_g, l), dx, None, BF16, tmn)
        G['w_o'] = _mm_tn("dw_o", sv['abc'], dmix, tm=d, tn=d, tk=512)
        dabc = _mm_nt("bwd_w_o", dmix, W['w_o'], 0, tm=1024, tn=512, tk=d, out_dtype=F32)
        dz, dgv, dws, dbt, dpw, dpsc, dsnk = _mixer_bwd(
            "mixer_bwd", sv['z'], dabc, cosq, sinq, row(gm_v_g, l), gm_w_s[l], gm_b_s[l].T, pw_bd[l],
            row(pool_scale, l), row(attn_sinks, l))
        gs['gm_v_g'][l] = dgv
        gs['gm_w_s'][l] = dws
        gs['gm_b_s'][l] = dbt.T
        gs['pool_w'][l] = jnp.stack([dpw[64 * g:64 * (g + 1), 64 * g:64 * (g + 1)] for g in range(4)])
        gs['pool_scale'][l] = dpsc
        gs['attn_sinks'][l] = dsnk
        G['w_in'] = _mm_tn("dw_in", sv['h1'], dz, tm=d, tn=dz.shape[1], tk=512)
        dh = _mm_nt("bwd_w_in", dz, W['w_in'], 0, tm=1024, tn=512, tk=dz.shape[1], out_dtype=F32)
        dx, gs['mix_pre_g'][l] = _norm_bwd("bwd_pre_mix", sv['x0'], row(mix_pre_g, l), dh, dx, F32, tmn)
        tok = grads_of(l, G, dx)
    _, dg_mem = _norm_bwd("bwd_mem_norm", mem[0], mem_norm_g.reshape(1, d), dmemn, None, BF16, tmn)
    small_g = []
    for n in SMALL:
        if n == 'mem_norm_g':
            small_g.append(dg_mem.reshape(P[n].shape))
        else:
            small_g.append(jnp.stack([a.reshape(P[n].shape[1:]) for a in gs[n]]))
    return loss_part, dx, small_g
```

```python
import functools

import jax
import jax.numpy as jnp
from jax import lax
from jax.experimental import pallas as pl
from jax.experimental.pallas import tpu as pltpu

F32 = jnp.float32
BF16 = jnp.bfloat16
EPS = 1e-6
CHUNK = 128
HEAD = 64
ROPE_THETA = 10000.0
POOL_WINDOWS = (2, 4, 8, 16)
LR, B1, B2, ADAM_EPS, WD, STEP = 0.001, 0.9, 0.999, 1e-08, 0.01, 10
MESH = pl.DeviceIdType.MESH
VMEM_LIMIT = 56 * 1024 * 1024

NAMES = ['x', 'mem', 'positions', 'mem_norm_g', 'mix_pre_g', 'mix_post_g', 'w_in', 'gm_v_g', 'gm_w_s', 'gm_b_s',
         'pool_w', 'pool_scale', 'attn_sinks', 'w_o', 'x_pre_g', 'x_post_g', 'w_xq', 'w_xkv', 'w_xo', 'ffn_pre_g',
         'ffn_post_g', 'w_gate_up', 'w_down']
WEIGHTS = NAMES[3:]
BIG = ['w_in', 'w_o', 'w_xq', 'w_xkv', 'w_xo', 'w_gate_up', 'w_down']
BIG_AXIS = {'w_in': 2, 'w_o': 1, 'w_xq': 1, 'w_xkv': 2, 'w_xo': 1, 'w_gate_up': 2, 'w_down': 1}
SMALL = [n for n in WEIGHTS if n not in BIG]

NN = (((1,), (0,)), ((), ()))
NT = (((1,), (1,)), ((), ()))
TN = (((0,), (0,)), ((), ()))


def _dot(a, b, dims=NN):
    return lax.dot_general(a, b, dims, preferred_element_type=F32)


def _params(sem):
    return pltpu.CompilerParams(dimension_semantics=sem, vmem_limit_bytes=VMEM_LIMIT)


def _rows_tile(rows, limit=256):
    return max(t for t in range(16, limit + 1, 16) if rows % t == 0)


def _mm(name, a, a_spec, b, b_spec, dims, grid, nk, out_shape, out_spec, add=None, add_spec=None, buf=None):
    acc_shape = out_spec.block_shape
    acc_shape = tuple(s for s in acc_shape if s is not None)

    def body(*refs):
        a_ref, b_ref = refs[0], refs[1]
        pos = 2
        add_ref = None
        if add is not None:
            add_ref = refs[pos]
            pos += 1
        if buf is not None:
            pos += 1
        o_ref = refs[pos]
        part = _dot(a_ref[...].astype(BF16), b_ref[...].astype(BF16), dims)
        if nk == 1:
            if add_ref is not None:
                part = part + add_ref[...]
            o_ref[...] = part.astype(o_ref.dtype)
        else:
            acc_ref = refs[pos + 1]
            k = pl.program_id(2)

            @pl.when(k == 0)
            def _():
                acc_ref[...] = part if add_ref is None else part + add_ref[...]

            @pl.when(k > 0)
            def _():
                acc_ref[...] += part

            @pl.when(k == nk - 1)
            def _():
                o_ref[...] = acc_ref[...].astype(o_ref.dtype)

    ops, specs = [a, b], [a_spec, b_spec]
    if add is not None:
        ops.append(add)
        specs.append(add_spec)
    aliases = {}
    if buf is not None:
        aliases = {len(ops): 0}
        ops.append(buf)
        specs.append(pl.BlockSpec(memory_space=pl.ANY))
    return pl.pallas_call(
        body, name=name, grid=grid, in_specs=specs, out_specs=out_spec, out_shape=out_shape,
        scratch_shapes=[pltpu.VMEM(acc_shape, F32)] if nk > 1 else [],
        input_output_aliases=aliases, compiler_params=_params(("parallel", "parallel", "arbitrary")),
    )(*ops)


def _wspec(block, layer, fn):
    return pl.BlockSpec((None,) + block, lambda i, j, k: (layer,) + fn(i, j, k))


def _mm_nn(name, a, w, layer, *, tm, tn, tk, out_dtype, n0=0, n=None, k0=0):
    m, kk = a.shape
    n = w.shape[2] if n is None else n
    tm = min(tm, m)
    nk = kk // tk
    return _mm(name, a, pl.BlockSpec((tm, tk), lambda i, j, k: (i, k)),
               w, _wspec((tk, tn), layer, lambda i, j, k: (k + k0 // tk, j + n0 // tn)), NN,
               (m // tm, n // tn, nk), nk, jax.ShapeDtypeStruct((m, n), out_dtype),
               pl.BlockSpec((tm, tn), lambda i, j, k: (i, j)))


def _mm_nt(name, a, w, layer, *, tm, tn, tk, out_dtype, k0=0, add=None):
    m, kk = a.shape
    n = w.shape[1]
    tm = min(tm, m)
    nk = kk // tk
    ospec = pl.BlockSpec((tm, tn), lambda i, j, k: (i, j))
    return _mm(name, a, pl.BlockSpec((tm, tk), lambda i, j, k: (i, k)),
               w, _wspec((tn, tk), layer, lambda i, j, k: (j, k + k0 // tk)), NT,
               (m // tm, n // tn, nk), nk, jax.ShapeDtypeStruct((m, n), out_dtype), ospec,
               add=add, add_spec=ospec if add is not None else None)


def _mm_tn(name, a, b, *, tm, tn, tk, n_total=None, n0=0, buf=None):
    kk, m = a.shape
    n = b.shape[1]
    n_total = n if n_total is None else n_total
    tk = min(tk, kk)
    nk = kk // tk
    return _mm(name, a, pl.BlockSpec((tk, tm), lambda i, j, k: (k, i)),
               b, pl.BlockSpec((tk, tn), lambda i, j, k: (k, j)), TN,
               (m // tm, n // tn, nk), nk, jax.ShapeDtypeStruct((m, n_total), BF16),
               pl.BlockSpec((tm, tn), lambda i, j, k: (i, j + n0 // tn)), buf=buf)


def _rstd(x):
    return lax.rsqrt(jnp.mean(x * x, axis=-1, keepdims=True) + EPS)


def _row(d):
    return pl.BlockSpec((1, d), lambda i: (0, 0))


def _prenorm(name, x, g, tm):
    m, d = x.shape

    def body(x_ref, g_ref, o_ref):
        xv = x_ref[...]
        o_ref[...] = (xv * _rstd(xv) * g_ref[...]).astype(BF16)

    blk = pl.BlockSpec((tm, d), lambda i: (i, 0))
    return pl.pallas_call(body, name=name, grid=(m // tm,), in_specs=[blk, _row(d)], out_specs=blk,
                          out_shape=jax.ShapeDtypeStruct((m, d), BF16), compiler_params=_params(("parallel",)))(x, g)


def _post_pre(name, x, y, g_post, g_pre, tm):
    m, d = x.shape

    def body(x_ref, y_ref, gp_ref, gn_ref, xo_ref, h_ref):
        yv = y_ref[...]
        xn = x_ref[...] + yv * _rstd(yv) * gp_ref[...]
        xo_ref[...] = xn
        h_ref[...] = (xn * _rstd(xn) * gn_ref[...]).astype(BF16)

    blk = pl.BlockSpec((tm, d), lambda i: (i, 0))
    return pl.pallas_call(
        body, name=name, grid=(m // tm,), in_specs=[blk, blk, _row(d), _row(d)], out_specs=[blk, blk],
        out_shape=[jax.ShapeDtypeStruct((m, d), F32), jax.ShapeDtypeStruct((m, d), BF16)],
        compiler_params=_params(("parallel",)))(x, y, g_post, g_pre)


def _loss_head(name, x, y, g_post, tgt, tm):
    m, d = x.shape

    def body(x_ref, y_ref, g_ref, t_ref, dy_ref, loss_ref):
        yv = y_ref[...]
        err = x_ref[...] + yv * _rstd(yv) * g_ref[...] - t_ref[...]
        dy_ref[...] = err * (1.0 / d)

        @pl.when(pl.program_id(0) == 0)
        def _():
            loss_ref[...] = jnp.zeros_like(loss_ref)

        loss_ref[...] += 0.5 * jnp.sum(jnp.mean(err * err, axis=-1, keepdims=True), axis=0, keepdims=True)

    blk = pl.BlockSpec((tm, d), lambda i: (i, 0))
    return pl.pallas_call(
        body, name=name, grid=(m // tm,), in_specs=[blk, blk, _row(d), blk],
        out_specs=[blk, pl.BlockSpec((1, 1), lambda i: (0, 0))],
        out_shape=[jax.ShapeDtypeStruct((m, d), F32), jax.ShapeDtypeStruct((1, 1), F32)],
        compiler_params=_params(("arbitrary",)))(x, y, g_post, tgt)


def _norm_bwd(name, xin, g, dy, resid, out_dtype, tm):
    m, d = xin.shape

    def body(*refs):
        if resid is None:
            x_ref, g_ref, dy_ref, dx_ref, dg_ref = refs
        else:
            x_ref, g_ref, dy_ref, r_ref, dx_ref, dg_ref = refs
        xv = x_ref[...]
        r = _rstd(xv)
        xh = xv * r
        dyv = dy_ref[...].astype(F32)
        dyg = dyv * g_ref[...]
        dx = r * (dyg - xh * jnp.mean(dyg * xh, axis=-1, keepdims=True))
        if resid is not None:
            dx = dx + r_ref[...]
        dx_ref[...] = dx.astype(dx_ref.dtype)

        @pl.when(pl.program_id(0) == 0)
        def _():
            dg_ref[...] = jnp.zeros_like(dg_ref)

        dg_ref[...] += jnp.sum(dyv * xh, axis=0, keepdims=True)

    blk = pl.BlockSpec((tm, d), lambda i: (i, 0))
    ops = [xin, g, dy] + ([] if resid is None else [resid])
    specs = [blk, _row(d), blk] + ([] if resid is None else [blk])
    return pl.pallas_call(
        body, name=name, grid=(m // tm,), in_specs=specs, out_specs=[blk, _row(d)],
        out_shape=[jax.ShapeDtypeStruct((m, d), out_dtype), jax.ShapeDtypeStruct((1, d), F32)],
        compiler_params=_params(("arbitrary",)))(*ops)


def _adamw(name, g, w, m, v, tr):
    rows, cols = g.shape
    c1 = 1.0 - B1 ** STEP
    c2 = 1.0 - B2 ** STEP

    def body(g_ref, w_ref, m_ref, v_ref, d_ref, mo_ref, vo_ref):
        gv = g_ref[...]
        mn = B1 * m_ref[...] + (1.0 - B1) * gv
        vn = B2 * v_ref[...] + (1.0 - B2) * (gv * gv)
        mo_ref[...] = mn
        vo_ref[...] = vn
        d_ref[...] = -LR * ((mn / c1) / (jnp.sqrt(vn / c2) + ADAM_EPS) + WD * w_ref[...])

    blk = pl.BlockSpec((tr, cols), lambda i: (i, 0))
    sd = jax.ShapeDtypeStruct((rows, cols), F32)
    return pl.pallas_call(body, name=name, grid=(rows // tr,), in_specs=[blk] * 4, out_specs=[blk] * 3,
                          out_shape=[sd, sd, sd], compiler_params=_params(("parallel",)))(g, w, m, v)


def _gelu_parts(x):
    c = 0.7978845608028654
    t = jnp.tanh(c * (x + 0.044715 * (x * x * x)))
    return 0.5 * x * (1.0 + t), t


def _gelu_grad(x, t):
    c = 0.7978845608028654
    return 0.5 * (1.0 + t) + 0.5 * x * (1.0 - t * t) * (c * (1.0 + 3.0 * 0.044715 * x * x))


def _rot_half(x):
    ax = x.ndim - 1
    w = x.shape[ax]
    lane = lax.broadcasted_iota(jnp.int32, x.shape, ax)
    return jnp.where((lane & 63) < 32, pltpu.roll(x, w - 32, ax), pltpu.roll(x, 32, ax))


def _gm_group(gel, g, gv, ws_ref, bt):
    u = gel[:, HEAD * g:HEAD * (g + 1)]
    vg = gel[:, 256 + HEAD * g:256 + HEAD * (g + 1)]
    r = _rstd(vg)
    xh = vg * r
    vn = (xh * gv[:, HEAD * g:HEAD * (g + 1)]).astype(BF16)
    row = lax.broadcasted_iota(jnp.int32, (CHUNK, CHUNK), 0)
    col = lax.broadcasted_iota(jnp.int32, (CHUNK, CHUNK), 1)
    causal = col <= row
    wc = jnp.where(causal, ws_ref[g], 0.0).astype(BF16)
    mixed = _dot(wc, vn) + bt[:, g:g + 1]
    return u, r, xh, vn, wc, causal, mixed


def _lane_select(lane, vals):
    return jnp.where(lane < 64, vals[0], jnp.where(lane < 128, vals[1], jnp.where(lane < 192, vals[2], vals[3])))


def _pool_fwd(pc, pp, ci):
    ext = jnp.concatenate([pp, pc], axis=0)
    s2 = ext + pltpu.roll(ext, 1, 0)
    s4 = s2 + pltpu.roll(s2, 2, 0)
    s8 = s4 + pltpu.roll(s4, 4, 0)
    s16 = s8 + pltpu.roll(s8, 8, 0)
    t1 = ci * CHUNK + lax.broadcasted_iota(jnp.int32, (CHUNK, 1), 0) + 1
    lane = lax.broadcasted_iota(jnp.int32, (1, 256), 1)
    cnt = _lane_select(lane, [jnp.minimum(t1, w).astype(F32) for w in POOL_WINDOWS])
    ssel = _lane_select(lane, [s[CHUNK:] for s in (s2, s4, s8, s16)])
    return ssel / cnt - pc, cnt, lane


def _attn_prep(zc, zpkv, cq, sq, cp, sp, ci):
    q = zc[:, 768:1280]
    kc = zc[:, 1280:1408]
    vc = zc[:, 1408:1536]
    kp = zpkv[:, :128]
    vp = zpkv[:, 128:]
    qr = q * cq + _rot_half(q) * sq
    krc = kc * cq[:, :128] + _rot_half(kc) * sq[:, :128]
    krp = kp * cp + _rot_half(kp) * sp
    kband = jnp.concatenate([krp, krc], axis=0).astype(BF16)
    vband = jnp.concatenate([vp, vc], axis=0).astype(BF16)
    row = lax.broadcasted_iota(jnp.int32, (CHUNK, 2 * CHUNK), 0)
    col = lax.broadcasted_iota(jnp.int32, (CHUNK, 2 * CHUNK), 1)
    valid = ((col < CHUNK) & (col > row) & (ci > 0)) | ((col >= CHUNK) & (col - CHUNK <= row))
    return qr, kband, vband, valid


def _head_probs(qh, kh, valid, sink):
    s = _dot(qh, kh, NT) * (HEAD ** -0.5)
    s = jnp.where(valid, s, -1e30)
    mx = jnp.maximum(jnp.max(s, axis=-1, keepdims=True), sink)
    e = jnp.exp(s - mx)
    es = jnp.exp(sink - mx)
    den = jnp.sum(e, axis=-1, keepdims=True) + es
    return e / den, es / den


def _mixer_specs(nb, rev):
    def cur(i):
        return nb - 1 - i if rev else i

    def prev(i):
        return jnp.maximum(cur(i) - 1, 0)

    full = lambda shape: pl.BlockSpec(shape, lambda i: (0,) * len(shape))
    specs = [
        pl.BlockSpec((CHUNK, 1536), lambda i: (cur(i), 0)),
        pl.BlockSpec((CHUNK, 256), lambda i: (prev(i), 2)),
        pl.BlockSpec((CHUNK, 256), lambda i: (prev(i), 5)),
        pl.BlockSpec((CHUNK, 512), lambda i: (cur(i), 0)),
        pl.BlockSpec((CHUNK, 512), lambda i: (cur(i), 0)),
        pl.BlockSpec((CHUNK, 128), lambda i: (prev(i), 0)),
        pl.BlockSpec((CHUNK, 128), lambda i: (prev(i), 0)),
        full((1, 256)), full((4, CHUNK, CHUNK)), full((CHUNK, 4)), full((256, 256)), full((1, 256)), full((1, 8)),
    ]
    return specs, cur


def _mixer_fwd(name, z, cosq, sinq, gv, ws, bt, pw, psc, snk):
    s = z.shape[0]
    nb = s // CHUNK
    specs, _ = _mixer_specs(nb, False)

    def body(zc_ref, zpp_ref, zpkv_ref, cq_ref, sq_ref, cp_ref, sp_ref, gv_ref, ws_ref, bt_ref, pw_ref, psc_ref,
             snk_ref, o_ref):
        ci = pl.program_id(0)
        zc = zc_ref[...]
        gel, _ = _gelu_parts(zc[:, :512])
        gvv = gv_ref[...]
        btv = bt_ref[...]
        for g in range(4):
            u, _, _, _, _, _, mixed = _gm_group(gel, g, gvv, ws_ref, btv)
            o_ref[:, HEAD * g:HEAD * (g + 1)] = (u * mixed).astype(BF16)
        pp = jnp.where(ci > 0, zpp_ref[...], 0.0)
        pooled, _, _ = _pool_fwd(zc[:, 512:768], pp, ci)
        mp = _dot(pooled.astype(BF16), pw_ref[...].astype(BF16))
        o_ref[:, 256:512] = (mp * psc_ref[...]).astype(BF16)
        qr, kband, vband, valid = _attn_prep(zc, zpkv_ref[...], cq_ref[...], sq_ref[...], cp_ref[...], sp_ref[...], ci)
        snkv = snk_ref[...]
        for h in range(8):
            hk = h // 4
            p, _ = _head_probs(qr[:, HEAD * h:HEAD * (h + 1)].astype(BF16), kband[:, HEAD * hk:HEAD * (hk + 1)],
                               valid, snkv[:, h:h + 1])
            o = _dot(p.astype(BF16), vband[:, HEAD * hk:HEAD * (hk + 1)])
            o_ref[:, 512 + HEAD * h:512 + HEAD * (h + 1)] = o.astype(BF16)

    return pl.pallas_call(
        body, name=name, grid=(nb,), in_specs=specs, out_specs=pl.BlockSpec((CHUNK, 1024), lambda i: (i, 0)),
        out_shape=jax.ShapeDtypeStruct((s, 1024), BF16), compiler_params=_params(("parallel",)),
    )(z, z, z, cosq, sinq, cosq, sinq, gv, ws, bt, pw, psc, snk)


def _mixer_bwd(name, z, dabc, cosq, sinq, gv, ws, bt, pw, psc, snk):
    s = z.shape[0]
    nb = s // CHUNK
    specs, cur = _mixer_specs(nb, True)
    specs = specs + [pl.BlockSpec((CHUNK, 1024), lambda i: (cur(i), 0))]
    full = lambda shape: pl.BlockSpec(shape, lambda i: (0,) * len(shape))
    acc_shapes = [(1, 256), (4, CHUNK, CHUNK), (CHUNK, 4), (256, 256), (1, 256), (1, 8)]

    def body(zc_ref, zpp_ref, zpkv_ref, cq_ref, sq_ref, cp_ref, sp_ref, gv_ref, ws_ref, bt_ref, pw_ref, psc_ref,
             snk_ref, dabc_ref, dz_ref, dgv_ref, dws_ref, dbt_ref, dpw_ref, dpsc_ref, dsnk_ref,
             cpool, ck, cv, dq_s, dkv_s):
        step = pl.program_id(0)
        ci = nb - 1 - step

        @pl.when(step == 0)
        def _():
            for r in (dgv_ref, dws_ref, dbt_ref, dpw_ref, dpsc_ref, dsnk_ref, cpool, ck, cv):
                r[...] = jnp.zeros_like(r)

        zc = zc_ref[...]
        dabc = dabc_ref[...]
        zg = zc[:, :512]
        gel, th = _gelu_parts(zg)
        gp = _gelu_grad(zg, th)
        gvv = gv_ref[...]
        btv = bt_ref[...]
        lane4 = lax.broadcasted_iota(jnp.int32, (CHUNK, 4), 1)
        dbt = jnp.zeros((CHUNK, 4), F32)
        for g in range(4):
            lo, hi = HEAD * g, HEAD * (g + 1)
            u, r, xh, vn, wc, causal, mixed = _gm_group(gel, g, gvv, ws_ref, btv)
            da = dabc[:, lo:hi]
            dm = da * u
            dmb = dm.astype(BF16)
            dws_ref[g] += jnp.where(causal, _dot(dmb, vn, NT), 0.0)
            dbt = dbt + jnp.where(lane4 == g, jnp.sum(dm, axis=-1, keepdims=True), 0.0)
            dvn = _dot(wc, dmb, TN)
            dgv_ref[:, lo:hi] += jnp.sum(dvn * xh, axis=0, keepdims=True)
            dxh = dvn * gvv[:, lo:hi]
            dvg = r * (dxh - xh * jnp.mean(dxh * xh, axis=-1, keepdims=True))
            dz_ref[:, lo:hi] = (da * mixed * gp[:, lo:hi]).astype(BF16)
            dz_ref[:, 256 + lo:256 + hi] = (dvg * gp[:, 256 + lo:256 + hi]).astype(BF16)
        dbt_ref[...] += dbt
        pc = zc[:, 512:768]
        pp = jnp.where(ci > 0, zpp_ref[...], 0.0)
        pooled, cnt, lane = _pool_fwd(pc, pp, ci)
        pwb = pw_ref[...].astype(BF16)
        pooled_b = pooled.astype(BF16)
        mp = _dot(pooled_b, pwb)
        db = dabc[:, 256:512]
        dpsc_ref[...] += jnp.sum(db * mp, axis=0, keepdims=True)
        dmpb = (db * psc_ref[...]).astype(BF16)
        dpw_ref[...] += _dot(pooled_b, dmpb, TN)
        dpooled = _dot(dmpb, pwb, NT)
        davg = dpooled / cnt
        zero = jnp.zeros((CHUNK, 256), F32)
        d2, d4, d8, d16 = [jnp.concatenate([zero, jnp.where((lane >= 64 * k) & (lane < 64 * (k + 1)), davg, 0.0)],
                                           axis=0) for k in range(4)]
        g8 = d8 + d16 + pltpu.roll(d16, 2 * CHUNK - 8, 0)
        g4 = d4 + g8 + pltpu.roll(g8, 2 * CHUNK - 4, 0)
        g2 = d2 + g4 + pltpu.roll(g4, 2 * CHUNK - 2, 0)
        ge = g2 + pltpu.roll(g2, 2 * CHUNK - 1, 0)
        dz_ref[:, 512:768] = (ge[CHUNK:] - dpooled + cpool[...]).astype(BF16)
        cpool[...] = ge[:CHUNK]
        cq = cq_ref[...]
        sq = sq_ref[...]
        qr, kband, vband, valid = _attn_prep(zc, zpkv_ref[...], cq, sq, cp_ref[...], sp_ref[...], ci)
        snkv = snk_ref[...]
        lane8 = lax.broadcasted_iota(jnp.int32, (1, 8), 1)
        dsnk = jnp.zeros((1, 8), F32)
        for hk in range(2):
            kh = kband[:, HEAD * hk:HEAD * (hk + 1)]
            vh = vband[:, HEAD * hk:HEAD * (hk + 1)]
            dkh = jnp.zeros((2 * CHUNK, HEAD), F32)
            dvh = jnp.zeros((2 * CHUNK, HEAD), F32)
            for h in range(4 * hk, 4 * hk + 4):
                qh = qr[:, HEAD * h:HEAD * (h + 1)].astype(BF16)
                p, ps = _head_probs(qh, kh, valid, snkv[:, h:h + 1])
                dob = dabc[:, 512 + HEAD * h:512 + HEAD * (h + 1)].astype(BF16)
                dp = _dot(dob, vh, NT)
                dd = jnp.sum(p * dp, axis=-1, keepdims=True)
                dsnk = dsnk + jnp.where(lane8 == h, jnp.sum(-ps * dd, axis=0, keepdims=True), 0.0)
                dsb = (p * (dp - dd) * (HEAD ** -0.5)).astype(BF16)
                dq_s[:, HEAD * h:HEAD * (h + 1)] = _dot(dsb, kh)
                dkh = dkh + _dot(dsb, qh, TN)
                dvh = dvh + _dot(p.astype(BF16), dob, TN)
            dkv_s[:, HEAD * hk:HEAD * (hk + 1)] = dkh
            dkv_s[:, 128 + HEAD * hk:128 + HEAD * (hk + 1)] = dvh
        dsnk_ref[...] += dsnk
        dqr = dq_s[...]
        dz_ref[:, 768:1280] = (dqr * cq + _rot_half(dqr * sq)).astype(BF16)
        dkv = dkv_s[...]
        dkr = dkv[CHUNK:, :128] + ck[...]
        dz_ref[:, 1280:1408] = (dkr * cq[:, :128] + _rot_half(dkr * sq[:, :128])).astype(BF16)
        dz_ref[:, 1408:1536] = (dkv[CHUNK:, 128:] + cv[...]).astype(BF16)
        ck[...] = dkv[:CHUNK, :128]
        cv[...] = dkv[:CHUNK, 128:]

    return pl.pallas_call(
        body, name=name, grid=(nb,), in_specs=specs,
        out_specs=[pl.BlockSpec((CHUNK, 1536), lambda i: (cur(i), 0))] + [full(a) for a in acc_shapes],
        out_shape=[jax.ShapeDtypeStruct((s, 1536), BF16)] + [jax.ShapeDtypeStruct(a, F32) for a in acc_shapes],
        scratch_shapes=[pltpu.VMEM((CHUNK, 256), F32), pltpu.VMEM((CHUNK, 128), F32), pltpu.VMEM((CHUNK, 128), F32),
                        pltpu.VMEM((CHUNK, 512), F32), pltpu.VMEM((2 * CHUNK, 256), F32)],
        compiler_params=_params(("arbitrary",)),
    )(z, z, z, cosq, sinq, cosq, sinq, gv, ws, bt, pw, psc, snk, dabc)


def _xattn_probs(qh, kh):
    s = _dot(qh, kh, NT) * (256 ** -0.5)
    e = jnp.exp(s - jnp.max(s, axis=-1, keepdims=True))
    return e / jnp.sum(e, axis=-1, keepdims=True)


def _xattn_fwd(name, q, kv, tq):
    s, d = q.shape
    mlen = kv.shape[0]

    def body(q_ref, kv_ref, o_ref):
        for h in range(4):
            lo, hi = 256 * h, 256 * (h + 1)
            p = _xattn_probs(q_ref[:, lo:hi], kv_ref[:, lo:hi])
            o_ref[:, lo:hi] = _dot(p.astype(BF16), kv_ref[:, d + lo:d + hi]).astype(BF16)

    blk = pl.BlockSpec((tq, d), lambda i: (i, 0))
    return pl.pallas_call(body, name=name, grid=(s // tq,),
                          in_specs=[blk, pl.BlockSpec((mlen, 2 * d), lambda i: (0, 0))], out_specs=blk,
                          out_shape=jax.ShapeDtypeStruct((s, d), BF16), compiler_params=_params(("parallel",)))(q, kv)


def _xattn_bwd(name, q, kv, do, tq):
    s, d = q.shape
    mlen = kv.shape[0]

    def body(q_ref, kv_ref, do_ref, dq_ref, dkv_ref):
        @pl.when(pl.program_id(0) == 0)
        def _():
            dkv_ref[...] = jnp.zeros_like(dkv_ref)

        for h in range(4):
            lo, hi = 256 * h, 256 * (h + 1)
            qh = q_ref[:, lo:hi]
            kh = kv_ref[:, lo:hi]
            vh = kv_ref[:, d + lo:d + hi]
            doh = do_ref[:, lo:hi]
            p = _xattn_probs(qh, kh)
            dp = _dot(doh, vh, NT)
            dsb = (p * (dp - jnp.sum(p * dp, axis=-1, keepdims=True)) * (256 ** -0.5)).astype(BF16)
            dq_ref[:, lo:hi] = _dot(dsb, kh).astype(BF16)
            dkv_ref[:, lo:hi] += _dot(dsb, qh, TN)
            dkv_ref[:, d + lo:d + hi] += _dot(p.astype(BF16), doh, TN)

    blk = pl.BlockSpec((tq, d), lambda i: (i, 0))
    kvb = pl.BlockSpec((mlen, 2 * d), lambda i: (0, 0))
    return pl.pallas_call(
        body, name=name, grid=(s // tq,), in_specs=[blk, kvb, blk], out_specs=[blk, kvb],
        out_shape=[jax.ShapeDtypeStruct((s, d), BF16), jax.ShapeDtypeStruct((mlen, 2 * d), F32)],
        compiler_params=_params(("arbitrary",)))(q, kv, do)


def _ffn_up(name, h, wgu, layer, tm, tn):
    s, d = h.shape
    dff = wgu.shape[2] // 2
    nj = dff // tn

    def body(h_ref, wg_ref, wu_ref, g_ref, u_ref, a_ref):
        hv = h_ref[...]
        gate = _dot(hv, wg_ref[...])
        up = _dot(hv, wu_ref[...])
        g_ref[...] = gate.astype(BF16)
        u_ref[...] = up.astype(BF16)
        a_ref[...] = (gate / (1.0 + jnp.exp(-gate)) * up).astype(BF16)

    ob = pl.BlockSpec((tm, tn), lambda j, i: (i, j))
    sd = jax.ShapeDtypeStruct((s, dff), BF16)
    return pl.pallas_call(
        body, name=name, grid=(nj, s // tm),
        in_specs=[pl.BlockSpec((tm, d), lambda j, i: (i, 0)),
                  pl.BlockSpec((None, d, tn), lambda j, i: (layer, 0, j)),
                  pl.BlockSpec((None, d, tn), lambda j, i: (layer, 0, j + nj))],
        out_specs=[ob, ob, ob], out_shape=[sd, sd, sd], compiler_params=_params(("parallel", "parallel")),
    )(h, wgu, wgu)


def _ffn_act_bwd(name, dfn, wdown, layer, gate, up, tm, tn):
    s, d = dfn.shape
    dff = gate.shape[1]

    def body(df_ref, wd_ref, g_ref, u_ref, dg_ref, du_ref):
        dact = _dot(df_ref[...], wd_ref[...], NT)
        gate = g_ref[...].astype(F32)
        sig = 1.0 / (1.0 + jnp.exp(-gate))
        du_ref[...] = (dact * gate * sig).astype(BF16)
        dg_ref[...] = (dact * u_ref[...].astype(F32) * sig * (1.0 + gate * (1.0 - sig))).astype(BF16)

    ob = pl.BlockSpec((tm, tn), lambda j, i: (i, j))
    sd = jax.ShapeDtypeStruct((s, dff), BF16)
    return pl.pallas_call(
        body, name=name, grid=(dff // tn, s // tm),
        in_specs=[pl.BlockSpec((tm, d), lambda j, i: (i, 0)),
                  pl.BlockSpec((None, tn, d), lambda j, i: (layer, j, 0)), ob, ob],
        out_specs=[ob, ob], out_shape=[sd, sd], compiler_params=_params(("parallel", "parallel")),
    )(dfn, wdown, gate, up)


def _place():
    return lax.axis_index("x"), lax.axis_index("y"), lax.axis_index("c")


def _other_chips(x, y):
    return [(1 - x, y), (x, 1 - y), (1 - x, 1 - y)]


def _region(ref, axis, chip, size):
    start = pl.multiple_of(chip * size, size)
    if axis == 1:
        return ref.at[:, pl.ds(start, size), :]
    return ref.at[:, :, pl.ds(start, size)]


ANY = pl.BlockSpec(memory_space=pl.ANY)


HBM = pl.BlockSpec(memory_space=pltpu.HBM)
SEM = pl.BlockSpec(memory_space=pltpu.SEMAPHORE)
EFFECT = pltpu.SideEffectType.DATAFLOW_SIDE_EFFECTING


def _in_hbm(a):
    return pltpu.with_memory_space_constraint(a, pltpu.HBM)


def _split_start(name, srcs, lands, ncopies, plan):
    ns, nl = len(srcs), len(lands)

    def body(*refs):
        src, land = refs[:ns], refs[ns:ns + nl]
        send, recv = refs[ns + nl], refs[ns + nl + 1]
        token = refs[-1]
        x, y, c = _place()
        for k, (s_ref, d_ref, peer, _) in enumerate(plan(src, land, x, y, c)):
            pltpu.make_async_remote_copy(src_ref=s_ref, dst_ref=d_ref, send_sem=send.at[k], recv_sem=recv.at[k],
                                         device_id=peer, device_id_type=MESH).start()
        token[...] = jnp.zeros_like(token)

    ops = list(srcs) + list(lands)
    out = pl.pallas_call(
        body, name=name,
        out_shape=(pltpu.SemaphoreType.DMA((ncopies,)), pltpu.SemaphoreType.DMA((ncopies,)),
                   *[pltpu.HBM(a.shape, a.dtype) for a in ops], jax.ShapeDtypeStruct((8, 128), F32)),
        in_specs=(HBM,) * (ns + nl), out_specs=(SEM, SEM) + (HBM,) * (ns + nl) + (pl.BlockSpec(memory_space=pltpu.VMEM),),
        input_output_aliases={i: 2 + i for i in range(ns + nl)},
        compiler_params=pltpu.CompilerParams(has_side_effects=EFFECT),
    )(*[_in_hbm(a) for a in ops])
    return out[0], out[1], list(out[2:2 + ns]), list(out[2 + ns:2 + ns + nl]), out[-1]


def _split_wait(name, send, recv, srcs, lands, after, plan):
    ns, nl = len(srcs), len(lands)

    def body(*refs):
        src, land = refs[:ns], refs[ns:ns + nl]
        send_ref, recv_ref = refs[ns + nl], refs[ns + nl + 1]
        x, y, c = _place()
        for k, (s_ref, _, _, got) in enumerate(plan(src, land, x, y, c)):
            cp = pltpu.make_async_remote_copy(src_ref=s_ref, dst_ref=got, send_sem=send_ref.at[k],
                                              recv_sem=recv_ref.at[k], device_id=(x, y, c), device_id_type=MESH)
            cp.wait_send()
            cp.wait_recv()

    ops = list(srcs) + list(lands)
    out = pl.pallas_call(
        body, name=name, out_shape=tuple(pltpu.HBM(a.shape, a.dtype) for a in ops),
        in_specs=(HBM,) * (ns + nl) + (SEM, SEM, ANY), out_specs=(HBM,) * (ns + nl),
        input_output_aliases={i: i for i in range(ns + nl)},
        compiler_params=pltpu.CompilerParams(has_side_effects=EFFECT),
    )(*ops, send, recv, after)
    return list(out[:ns]), list(out[ns:])


def _gather_plan(axes, sizes, layer):
    def plan(src, land, x, y, c):
        me = 2 * x + y
        out = []
        for t in range(len(src)):
            for px, py in _other_chips(x, y):
                out.append((src[t].at[pl.ds(layer, 1)], _region(land[t], axes[t], me, sizes[t]), (px, py, c),
                            _region(land[t], axes[t], 2 * px + py, sizes[t])))
        return out
    return plan


def _scatter_plan(axes, sizes):
    def plan(src, land, x, y, c):
        out = []
        for t in range(len(src)):
            for k, (px, py) in enumerate(_other_chips(x, y)):
                out.append((_region(src[t], axes[t], 2 * px + py, sizes[t]).at[0], land[t].at[k], (px, py, c),
                            land[t].at[k]))
        return out
    return plan


def _pair_plan(src, land, x, y, c):
    return [(src[t], land[t], (x, y, 1 - c), land[t]) for t in range(len(src))]


def _chip_sum(name, g, slots, axis, chip):
    _, r, cs = slots.shape
    tr = _rows_tile(r)
    nb = r // tr

    def body(m_ref, g_ref, s_ref, o_ref):
        acc = g_ref[...].astype(F32)
        for k in range(3):
            acc = acc + s_ref[k].astype(F32)
        o_ref[...] = acc

    if axis == 1:
        gspec = pl.BlockSpec((tr, cs), lambda i, m: (m[0] * nb + i, 0))
    else:
        gspec = pl.BlockSpec((tr, cs), lambda i, m: (i, m[0]))
    return pl.pallas_call(
        body, name=name,
        grid_spec=pltpu.PrefetchScalarGridSpec(
            num_scalar_prefetch=1, grid=(nb,),
            in_specs=[gspec, pl.BlockSpec((3, tr, cs), lambda i, m: (0, i, 0))],
            out_specs=pl.BlockSpec((tr, cs), lambda i, m: (i, 0))),
        out_shape=jax.ShapeDtypeStruct((r, cs), F32), compiler_params=_params(("parallel",)),
    )(chip, g, slots)


def _pair_adamw(name, mine, theirs, w, m, v, layer, bufs):
    r, cs = mine.shape
    tr = _rows_tile(r)
    c1 = 1.0 - B1 ** STEP
    c2 = 1.0 - B2 ** STEP

    def body(a_ref, b_ref, w_ref, m_ref, v_ref, _g, _d, _m, _v, g_ref, d_ref, mo_ref, vo_ref):
        gv = a_ref[...] + b_ref[...]
        mn = B1 * m_ref[...] + (1.0 - B1) * gv
        vn = B2 * v_ref[...] + (1.0 - B2) * (gv * gv)
        g_ref[...] = gv
        mo_ref[...] = mn
        vo_ref[...] = vn
        d_ref[...] = -LR * ((mn / c1) / (jnp.sqrt(vn / c2) + ADAM_EPS) + WD * w_ref[...])

    blk = pl.BlockSpec((tr, cs), lambda i: (i, 0))
    lay = pl.BlockSpec((None, tr, cs), lambda i: (layer, i, 0))
    return pl.pallas_call(
        body, name=name, grid=(r // tr,), in_specs=[blk, blk, lay, lay, lay] + [ANY] * 4, out_specs=[lay] * 4,
        out_shape=[jax.ShapeDtypeStruct(b.shape, b.dtype) for b in bufs],
        input_output_aliases={5 + i: i for i in range(4)}, compiler_params=_params(("parallel",)),
    )(mine, theirs, w, m, v, *bufs)


def _allreduce_small(p):
    rows = p.shape[0]

    def body(p_ref, o_ref, gath, send, recv):
        x, y, c = _place()
        me = 4 * x + 2 * y + c

        def flip(v, bit):
            return 1 - v if bit else v

        gath[me] = p_ref[...]
        cps = []
        for k in range(1, 8):
            peer = (flip(x, k & 4), flip(y, k & 2), flip(c, k & 1))
            cps.append(pltpu.make_async_remote_copy(src_ref=p_ref, dst_ref=gath.at[me], send_sem=send.at[k - 1],
                                                    recv_sem=recv.at[k - 1], device_id=peer, device_id_type=MESH))
        for cp in cps:
            cp.start()
        for k in range(1, 8):
            slot = gath.at[4 * flip(x, k & 4) + 2 * flip(y, k & 2) + flip(c, k & 1)]
            pltpu.make_async_remote_copy(src_ref=slot, dst_ref=slot, send_sem=send.at[k - 1], recv_sem=recv.at[k - 1],
                                         device_id=(x, y, c), device_id_type=MESH).wait_recv()
        for cp in cps:
            cp.wait_send()
        acc = gath[0]
        for k in range(1, 8):
            acc = acc + gath[k]
        o_ref[...] = acc

    vm = pl.BlockSpec(memory_space=pltpu.VMEM)
    return pl.pallas_call(
        body, name="allreduce_small", in_specs=[vm], out_specs=vm, out_shape=jax.ShapeDtypeStruct(p.shape, F32),
        scratch_shapes=[pltpu.VMEM((8, rows, 128), F32), pltpu.SemaphoreType.DMA((7,)), pltpu.SemaphoreType.DMA((7,))],
        compiler_params=pltpu.CompilerParams(vmem_limit_bytes=VMEM_LIMIT),
    )(p)


def _pack(parts):
    flat = []
    for p in parts:
        v = p.reshape(-1).astype(F32)
        flat.append(jnp.pad(v, (0, (-v.shape[0]) % 128)))
    v = jnp.concatenate(flat)
    v = jnp.pad(v, (0, (-v.shape[0]) % (512 * 128)))
    return v.reshape(-1, 128)


def _unpack(buf, like):
    v = buf.reshape(-1)
    out, off = [], 0
    for p in like:
        nelem = 1
        for s in p.shape:
            nelem *= s
        out.append(v[off:off + nelem].reshape(p.shape))
        off += nelem + (-nelem) % 128
    return out


def kernel(x, mem, positions, mem_norm_g, mix_pre_g, mix_post_g, w_in, gm_v_g, gm_w_s, gm_b_s, pool_w, pool_scale, attn_sinks, w_o, x_pre_g, x_post_g, w_xq, w_xkv, w_xo, ffn_pre_g, ffn_post_g, w_gate_up, w_down, loss_target, m_mem_norm_g, m_mix_pre_g, m_mix_post_g, m_w_in, m_gm_v_g, m_gm_w_s, m_gm_b_s, m_pool_w, m_pool_scale, m_attn_sinks, m_w_o, m_x_pre_g, m_x_post_g, m_w_xq, m_w_xkv, m_w_xo, m_ffn_pre_g, m_ffn_post_g, m_w_gate_up, m_w_down, v_mem_norm_g, v_mix_pre_g, v_mix_post_g, v_w_in, v_gm_v_g, v_gm_w_s, v_gm_b_s, v_pool_w, v_pool_scale, v_attn_sinks, v_w_o, v_x_pre_g, v_x_post_g, v_w_xq, v_w_xkv, v_w_xo, v_ffn_pre_g, v_ffn_post_g, v_w_gate_up, v_w_down):
    args = (x, mem, positions, mem_norm_g, mix_pre_g, mix_post_g, w_in, gm_v_g, gm_w_s, gm_b_s, pool_w, pool_scale, attn_sinks, w_o, x_pre_g, x_post_g, w_xq, w_xkv, w_xo, ffn_pre_g, ffn_post_g, w_gate_up, w_down)
    moms_m = (m_mem_norm_g, m_mix_pre_g, m_mix_post_g, m_w_in, m_gm_v_g, m_gm_w_s, m_gm_b_s, m_pool_w, m_pool_scale, m_attn_sinks, m_w_o, m_x_pre_g, m_x_post_g, m_w_xq, m_w_xkv, m_w_xo, m_ffn_pre_g, m_ffn_post_g, m_w_gate_up, m_w_down)
    moms_v = (v_mem_norm_g, v_mix_pre_g, v_mix_post_g, v_w_in, v_gm_v_g, v_gm_w_s, v_gm_b_s, v_pool_w, v_pool_scale, v_attn_sinks, v_w_o, v_x_pre_g, v_x_post_g, v_w_xq, v_w_xkv, v_w_xo, v_ffn_pre_g, v_ffn_post_g, v_w_gate_up, v_w_down)
    P = dict(zip(NAMES, args))
    P['loss_target'] = loss_target
    M = dict(zip(WEIGHTS, moms_m))
    V = dict(zip(WEIGHTS, moms_v))
    depth = w_in.shape[0]
    nbig = len(BIG)
    axes = [BIG_AXIS[n] for n in BIG]
    sizes = [P[n].shape[a] for n, a in zip(BIG, axes)]
    full_shape = lambda n: tuple(4 * d if i == BIG_AXIS[n] else d for i, d in enumerate(P[n].shape))[1:]
    chip = (2 * lax.axis_index("x") + lax.axis_index("y")).astype(jnp.int32).reshape(1)

    groups = [['w_in'], ['w_o', 'w_xq', 'w_xkv', 'w_xo'], ['w_gate_up', 'w_down']]
    units = [(0, g) for g in groups] + [(l, BIG) for l in range(1, depth)]
    unit_of = {(l, n): i for i, (l, names) in enumerate(units) for n in names}
    ax = lambda names: [BIG_AXIS[n] for n in names]
    sz = lambda names: [P[n].shape[BIG_AXIS[n]] for n in names]

    shards = {n: P[n].astype(BF16) for n in BIG}

    def own_placed(n, l):
        start = [jnp.int32(0)] * 3
        start[BIG_AXIS[n]] = chip[0] * P[n].shape[BIG_AXIS[n]]
        return lax.dynamic_update_slice(lax.empty((1,) + full_shape(n), BF16), shards[n][l:l + 1], start)

    placed = [[own_placed(n, l) for n in names] for l, names in units]
    gathers = []
    for i, (l, names) in enumerate(units):
        srcs = [shards[n] for n in names]
        send, recv, srcs, land, _ = _split_start("gather_start%d" % i, srcs, placed[i],
                                                 3 * len(names), _gather_plan(ax(names), sz(names), l))
        shards.update(zip(names, srcs))
        gathers.append([send, recv, land])
    gathered = {}

    def weights_of(l, names, after):
        i = unit_of[(l, names[0])]
        if i not in gathered:
            _, unames = units[i]
            send, recv, land = gathers[i]
            srcs, land = _split_wait("gather_wait%d" % i, send, recv, [shards[n] for n in unames], land, after,
                                     _gather_plan(ax(unames), sz(unames), l))
            shards.update(zip(unames, srcs))
            gathered[i] = dict(zip(unames, land))
        return {n: gathered[i][n] for n in names}

    outs = {n: [lax.empty(P[n].shape, F32) for _ in range(4)] for n in BIG}
    gunits = [(l, BIG) for l in range(depth - 1, 0, -1)] + [
        (0, g) for g in (['w_gate_up', 'w_down'], ['w_xq', 'w_xkv', 'w_xo'], ['w_o'], ['w_in'])]
    collected, scatters, pairs = {}, {}, {}

    def finish_scatter(i, after):
        _, names = gunits[i]
        send, recv, g_l, slots = scatters.pop(i)
        g_l, slots = _split_wait("scatter_wait%d" % i, send, recv, g_l, slots, after,
                                 _scatter_plan(ax(names), sz(names)))
        mine = [_chip_sum("chip_sum_" + n, g.reshape(g.shape[1:]), sl, BIG_AXIS[n], chip)
                for n, g, sl in zip(names, g_l, slots)]
        send, recv, mine, theirs, tok = _split_start("pair_start%d" % i, mine, [lax.empty(a.shape, F32) for a in mine],
                                                     len(names), _pair_plan)
        pairs[i] = (send, recv, mine, theirs)
        return tok[:1, :1]

    def finish_pair(i, after):
        l, names = gunits[i]
        send, recv, mine, theirs = pairs.pop(i)
        mine, theirs = _split_wait("pair_wait%d" % i, send, recv, mine, theirs, after, _pair_plan)
        for n, a, b in zip(names, mine, theirs):
            outs[n] = _pair_adamw("adamw_" + n, a, b, P[n], M[n], V[n], l, outs[n])

    def grads_of(l, g_part, after):
        collected.update({(l, n): g for n, g in g_part.items()})
        tok = jnp.zeros((1, 1), F32)
        for i, (ul, names) in enumerate(gunits):
            if ul != l or ('started', i) in collected or any((l, n) not in collected for n in names):
                continue
            collected[('started', i)] = True
            srcs = [collected[(l, n)].reshape((1,) + collected[(l, n)].shape) for n in names]
            send, recv, srcs, slots, t = _split_start("scatter_start%d" % i, srcs,
                                                      [lax.empty((3,) + P[n].shape[1:], BF16) for n in names],
                                                      3 * len(names), _scatter_plan(ax(names), sz(names)))
            scatters[i] = (send, recv, srcs, slots)
            tok = tok + t[:1, :1]
            if i >= 1:
                tok = tok + finish_scatter(i - 1, after)
            if i >= 2:
                finish_pair(i - 2, after)
        return tok

    loss_part, dx, small_g = _fwd_bwd(P, weights_of, grads_of)
    loss = lax.psum(loss_part[0, 0], ("x", "y", "c"))
    grad_x = dx.reshape(x.shape)

    small_like = [P[n] for n in SMALL]
    gsum = _allreduce_small(_pack(small_g))
    dlt, mn, vn = _adamw("adamw_small", gsum, _pack(small_like), _pack([M[n] for n in SMALL]),
                         _pack([V[n] for n in SMALL]), 512)
    grads, deltas, new_m, new_v = {}, {}, {}, {}
    for name_map, buf in ((grads, gsum), (deltas, dlt), (new_m, mn), (new_v, vn)):
        for n, a in zip(SMALL, _unpack(buf, small_like)):
            name_map[n] = a

    last = len(gunits) - 1
    finish_scatter(last, dlt)
    finish_pair(last - 1, dlt)
    finish_pair(last, dlt)
    for n in BIG:
        grads[n], deltas[n], new_m[n], new_v[n] = outs[n]

    return (loss, grad_x, *[grads[n] for n in WEIGHTS], *[deltas[n] for n in WEIGHTS],
            *[new_m[n] for n in WEIGHTS], *[new_v[n] for n in WEIGHTS])


def _fwd_bwd(P, weights_of, grads_of):
    (x, mem, positions, mem_norm_g, mix_pre_g, mix_post_g, w_in, gm_v_g, gm_w_s, gm_b_s, pool_w, pool_scale, attn_sinks,
     w_o, x_pre_g, x_post_g, w_xq, w_xkv, w_xo, ffn_pre_g, ffn_post_g, w_gate_up, w_down) = [P[n] for n in NAMES]
    x0 = x[0]
    s, d = x0.shape
    depth = w_in.shape[0]
    tgt = P['loss_target'][0]
    tmn = 256

    half = HEAD // 2
    inv = ROPE_THETA ** (-jnp.arange(half, dtype=F32) / half)
    ang = positions[0].astype(F32)[:, None] * inv
    cos, sin = jnp.cos(ang), jnp.sin(ang)
    cosq = jnp.tile(jnp.concatenate([cos, cos], axis=-1), (1, 8))
    sinq = jnp.tile(jnp.concatenate([-sin, sin], axis=-1), (1, 8))

    row = lambda a, l: a[l].reshape(1, -1)
    memn = _prenorm("mem_norm", mem[0], mem_norm_g.reshape(1, d), tmn)
    pw_bd = []
    for l in range(depth):
        bd = jnp.zeros((256, 256), F32)
        for g in range(4):
            bd = lax.dynamic_update_slice(bd, pool_w[l, g], (64 * g, 64 * g))
        pw_bd.append(bd)

    saved, weights = [], []
    xc = x0
    h = _prenorm("pre_norm0", x0, row(mix_pre_g, 0), tmn)
    for l in range(depth):
        W = weights_of(l, ['w_in'], xc)
        weights.append(W)
        sv = {'x0': xc, 'h1': h}
        z = _mm_nn("fwd_w_in", h, W['w_in'], 0, tm=1024, tn=512, tk=d, out_dtype=F32)
        abc = _mixer_fwd("mixer_fwd", z, cosq, sinq, row(gm_v_g, l), gm_w_s[l], gm_b_s[l].T, pw_bd[l],
                         row(pool_scale, l), row(attn_sinks, l))
        W.update(weights_of(l, ['w_o'], z))
        mix = _mm_nn("fwd_w_o", abc, W['w_o'], 0, tm=1024, tn=512, tk=d, out_dtype=F32)
        xc, h = _post_pre("post_mix", xc, mix, row(mix_post_g, l), row(x_pre_g, l), tmn)
        sv.update(z=z, abc=abc, mix=mix, x1=xc, h2=h)
        W.update(weights_of(l, ['w_xq', 'w_xkv', 'w_xo'], xc))
        q = _mm_nn("fwd_w_xq", h, W['w_xq'], 0, tm=1024, tn=512, tk=d, out_dtype=BF16)
        kv = _mm_nn("fwd_w_xkv", memn, W['w_xkv'], 0, tm=256, tn=512, tk=d, out_dtype=BF16)
        o = _xattn_fwd("xattn_fwd", q, kv, 512)
        xo = _mm_nn("fwd_w_xo", o, W['w_xo'], 0, tm=1024, tn=512, tk=d, out_dtype=F32)
        xc, h = _post_pre("post_xattn", xc, xo, row(x_post_g, l), row(ffn_pre_g, l), tmn)
        sv.update(q=q, kv=kv, o=o, xo=xo, x2=xc, h3=h)
        W.update(weights_of(l, ['w_gate_up', 'w_down'], xc))
        dff = W['w_down'].shape[1]
        gate, up, act = _ffn_up("ffn_up", h, W['w_gate_up'], 0, 512, dff // 2)
        f = _mm_nn("fwd_w_down", act, W['w_down'], 0, tm=1024, tn=512, tk=dff // 2, out_dtype=F32)
        sv.update(gate=gate, up=up, act=act, f=f)
        if l + 1 < depth:
            xc, h = _post_pre("post_ffn", xc, f, row(ffn_post_g, l), row(mix_pre_g, l + 1), tmn)
        saved.append(sv)
    dx, loss_part = _loss_head("loss_head", xc, saved[-1]['f'], row(ffn_post_g, depth - 1), tgt, tmn)

    gs = {n: [None] * depth for n in SMALL if n != 'mem_norm_g'}
    dmemn = None
    tok = jnp.zeros((1, 1), F32)
    for l in reversed(range(depth)):
        sv, W, G = saved[l], weights[l], {}
        dfn, gs['ffn_post_g'][l] = _norm_bwd("bwd_post_ffn", sv['f'], row(ffn_post_g, l) + tok, dx, None, BF16, tmn)
        G['w_down'] = _mm_tn("dw_down", sv['act'], dfn, tm=dff // 2, tn=d, tk=512)
        dgate, dup = _ffn_act_bwd("ffn_act_bwd", dfn, W['w_down'], 0, sv['gate'], sv['up'], 512, dff // 2)
        gu = _mm_tn("dw_gate", sv['h3'], dgate, tm=d, tn=dff // 2, tk=512, n_total=2 * dff)
        G['w_gate_up'] = _mm_tn("dw_up", sv['h3'], dup, tm=d, tn=dff // 2, tk=512, n_total=2 * dff, n0=dff, buf=gu)
        dh = _mm_nt("bwd_w_gate", dgate, W['w_gate_up'], 0, tm=1024, tn=512, tk=dff // 2, out_dtype=F32)
        dh = _mm_nt("bwd_w_up", dup, W['w_gate_up'], 0, tm=1024, tn=512, tk=dff // 2, out_dtype=F32, k0=dff, add=dh)
        dx, gs['ffn_pre_g'][l] = _norm_bwd("bwd_pre_ffn", sv['x2'], row(ffn_pre_g, l), dh, dx, F32, tmn)
        tok = grads_of(l, {n: G[n] for n in ('w_gate_up', 'w_down')}, dx)
        dxo, gs['x_post_g'][l] = _norm_bwd("bwd_post_xattn", sv['xo'], row(x_post_g, l) + tok, dx, None, BF16, tmn)
        G['w_xo'] = _mm_tn("dw_xo", sv['o'], dxo, tm=d, tn=d, tk=512)
        do = _mm_nt("bwd_w_xo", dxo, W['w_xo'], 0, tm=1024, tn=512, tk=d, out_dtype=BF16)
        dq, dkv = _xattn_bwd("xattn_bwd", sv['q'], sv['kv'], do, 512)
        dkv = dkv.astype(BF16)
        G['w_xkv'] = _mm_tn("dw_xkv", memn, dkv, tm=d, tn=d, tk=mem.shape[1])
        dmemn = _mm_nt("bwd_w_xkv", dkv, W['w_xkv'], 0, tm=mem.shape[1], tn=512, tk=2 * d, out_dtype=F32, add=dmemn)
        G['w_xq'] = _mm_tn("dw_xq", sv['h2'], dq, tm=d, tn=d, tk=512)
        dh = _mm_nt("bwd_w_xq", dq, W['w_xq'], 0, tm=1024, tn=512, tk=d, out_dtype=F32)
        dx, gs['x_pre_g'][l] = _norm_bwd("bwd_pre_xattn", sv['x1'], row(x_pre_g, l), dh, dx, F32, tmn)
        tok = grads_of(l, {n: G[n] for n in ('w_xq', 'w_xkv', 'w_xo')}, dx)
        dmix, gs['mix_post_g'][l] = _norm_bwd("bwd_post_mix", sv['mix'], row(mix_post_g, l) + tok, dx, None, BF16, tmn)
        G['w_o'] = _mm_tn("dw_o", sv['abc'], dmix, tm=d, tn=d, tk=512)
        dabc = _mm_nt("bwd_w_o", dmix, W['w_o'], 0, tm=1024, tn=512, tk=d, out_dtype=F32)
        tok = grads_of(l, {'w_o': G['w_o']}, dabc)
        dz, dgv, dws, dbt, dpw, dpsc, dsnk = _mixer_bwd(
            "mixer_bwd", sv['z'], dabc, cosq, sinq, row(gm_v_g, l) + tok, gm_w_s[l], gm_b_s[l].T, pw_bd[l],
            row(pool_scale, l), row(attn_sinks, l))
        gs['gm_v_g'][l] = dgv
        gs['gm_w_s'][l] = dws
        gs['gm_b_s'][l] = dbt.T
        gs['pool_w'][l] = jnp.stack([dpw[64 * g:64 * (g + 1), 64 * g:64 * (g + 1)] for g in range(4)])
        gs['pool_scale'][l] = dpsc
        gs['attn_sinks'][l] = dsnk
        G['w_in'] = _mm_tn("dw_in", sv['h1'], dz, tm=d, tn=dz.shape[1], tk=512)
        dh = _mm_nt("bwd_w_in", dz, W['w_in'], 0, tm=1024, tn=512, tk=dz.shape[1], out_dtype=F32)
        dx, gs['mix_pre_g'][l] = _norm_bwd("bwd_pre_mix", sv['x0'], row(mix_pre_g, l), dh, dx, F32, tmn)
        tok = grads_of(l, {'w_in': G['w_in']}, dx)
    _, dg_mem = _norm_bwd("bwd_mem_norm", mem[0], mem_norm_g.reshape(1, d), dmemn, None, BF16, tmn)
    small_g = []
    for n in SMALL:
        if n == 'mem_norm_g':
            small_g.append(dg_mem.reshape(P[n].shape))
        else:
            small_g.append(jnp.stack([a.reshape(P[n].shape[1:]) for a in gs[n]]))
    return loss_part, dx, small_g
```

```python
import functools

import jax
import jax.numpy as jnp
from jax import lax
from jax.experimental import pallas as pl
from jax.experimental.pallas import tpu as pltpu

F32 = jnp.float32
BF16 = jnp.bfloat16
EPS = 1e-6
CHUNK = 128
HEAD = 64
ROPE_THETA = 10000.0
POOL_WINDOWS = (2, 4, 8, 16)
LR, B1, B2, ADAM_EPS, WD, STEP = 0.001, 0.9, 0.999, 1e-08, 0.01, 10
MESH = pl.DeviceIdType.MESH
VMEM_LIMIT = 56 * 1024 * 1024

NAMES = ['x', 'mem', 'positions', 'mem_norm_g', 'mix_pre_g', 'mix_post_g', 'w_in', 'gm_v_g', 'gm_w_s', 'gm_b_s',
         'pool_w', 'pool_scale', 'attn_sinks', 'w_o', 'x_pre_g', 'x_post_g', 'w_xq', 'w_xkv', 'w_xo', 'ffn_pre_g',
         'ffn_post_g', 'w_gate_up', 'w_down']
WEIGHTS = NAMES[3:]
BIG = ['w_in', 'w_o', 'w_xq', 'w_xkv', 'w_xo', 'w_gate_up', 'w_down']
BIG_AXIS = {'w_in': 2, 'w_o': 1, 'w_xq': 1, 'w_xkv': 2, 'w_xo': 1, 'w_gate_up': 2, 'w_down': 1}
SMALL = [n for n in WEIGHTS if n not in BIG]

NN = (((1,), (0,)), ((), ()))
NT = (((1,), (1,)), ((), ()))
TN = (((0,), (0,)), ((), ()))


def _dot(a, b, dims=NN):
    return lax.dot_general(a, b, dims, preferred_element_type=F32)


def _params(sem):
    return pltpu.CompilerParams(dimension_semantics=sem, vmem_limit_bytes=VMEM_LIMIT)


def _rows_tile(rows, limit=256):
    return max(t for t in range(16, limit + 1, 16) if rows % t == 0)


def _mm(name, a, a_spec, b, b_spec, dims, grid, nk, out_shape, out_spec, add=None, add_spec=None, buf=None):
    acc_shape = out_spec.block_shape
    acc_shape = tuple(s for s in acc_shape if s is not None)

    def body(*refs):
        a_ref, b_ref = refs[0], refs[1]
        pos = 2
        add_ref = None
        if add is not None:
            add_ref = refs[pos]
            pos += 1
        if buf is not None:
            pos += 1
        o_ref = refs[pos]
        part = _dot(a_ref[...].astype(BF16), b_ref[...].astype(BF16), dims)
        if nk == 1:
            if add_ref is not None:
                part = part + add_ref[...]
            o_ref[...] = part.astype(o_ref.dtype)
        else:
            acc_ref = refs[pos + 1]
            k = pl.program_id(2)

            @pl.when(k == 0)
            def _():
                acc_ref[...] = part if add_ref is None else part + add_ref[...]

            @pl.when(k > 0)
            def _():
                acc_ref[...] += part

            @pl.when(k == nk - 1)
            def _():
                o_ref[...] = acc_ref[...].astype(o_ref.dtype)

    ops, specs = [a, b], [a_spec, b_spec]
    if add is not None:
        ops.append(add)
        specs.append(add_spec)
    aliases = {}
    if buf is not None:
        aliases = {len(ops): 0}
        ops.append(buf)
        specs.append(pl.BlockSpec(memory_space=pl.ANY))
    return pl.pallas_call(
        body, name=name, grid=grid, in_specs=specs, out_specs=out_spec, out_shape=out_shape,
        scratch_shapes=[pltpu.VMEM(acc_shape, F32)] if nk > 1 else [],
        input_output_aliases=aliases, compiler_params=_params(("parallel", "parallel", "arbitrary")),
    )(*ops)


def _wspec(block, layer, fn):
    return pl.BlockSpec((None,) + block, lambda i, j, k: (layer,) + fn(i, j, k))


def _mm_nn(name, a, w, layer, *, tm, tn, tk, out_dtype, n0=0, n=None, k0=0):
    m, kk = a.shape
    n = w.shape[2] if n is None else n
    tm = min(tm, m)
    nk = kk // tk
    return _mm(name, a, pl.BlockSpec((tm, tk), lambda i, j, k: (i, k)),
               w, _wspec((tk, tn), layer, lambda i, j, k: (k + k0 // tk, j + n0 // tn)), NN,
               (m // tm, n // tn, nk), nk, jax.ShapeDtypeStruct((m, n), out_dtype),
               pl.BlockSpec((tm, tn), lambda i, j, k: (i, j)))


def _mm_nt(name, a, w, layer, *, tm, tn, tk, out_dtype, k0=0, add=None):
    m, kk = a.shape
    n = w.shape[1]
    tm = min(tm, m)
    nk = kk // tk
    ospec = pl.BlockSpec((tm, tn), lambda i, j, k: (i, j))
    return _mm(name, a, pl.BlockSpec((tm, tk), lambda i, j, k: (i, k)),
               w, _wspec((tn, tk), layer, lambda i, j, k: (j, k + k0 // tk)), NT,
               (m // tm, n // tn, nk), nk, jax.ShapeDtypeStruct((m, n), out_dtype), ospec,
               add=add, add_spec=ospec if add is not None else None)


def _mm_tn(name, a, b, *, tm, tn, tk, n_total=None, n0=0, buf=None):
    kk, m = a.shape
    n = b.shape[1]
    n_total = n if n_total is None else n_total
    tk = min(tk, kk)
    nk = kk // tk
    return _mm(name, a, pl.BlockSpec((tk, tm), lambda i, j, k: (k, i)),
               b, pl.BlockSpec((tk, tn), lambda i, j, k: (k, j)), TN,
               (m // tm, n // tn, nk), nk, jax.ShapeDtypeStruct((m, n_total), BF16),
               pl.BlockSpec((tm, tn), lambda i, j, k: (i, j + n0 // tn)), buf=buf)


def _rstd(x):
    return lax.rsqrt(jnp.mean(x * x, axis=-1, keepdims=True) + EPS)


def _row(d):
    return pl.BlockSpec((1, d), lambda i: (0, 0))


def _prenorm(name, x, g, tm):
    m, d = x.shape

    def body(x_ref, g_ref, o_ref):
        xv = x_ref[...]
        o_ref[...] = (xv * _rstd(xv) * g_ref[...]).astype(BF16)

    blk = pl.BlockSpec((tm, d), lambda i: (i, 0))
    return pl.pallas_call(body, name=name, grid=(m // tm,), in_specs=[blk, _row(d)], out_specs=blk,
                          out_shape=jax.ShapeDtypeStruct((m, d), BF16), compiler_params=_params(("parallel",)))(x, g)


def _post_pre(name, x, y, g_post, g_pre, tm):
    m, d = x.shape

    def body(x_ref, y_ref, gp_ref, gn_ref, xo_ref, h_ref):
        yv = y_ref[...]
        xn = x_ref[...] + yv * _rstd(yv) * gp_ref[...]
        xo_ref[...] = xn
        h_ref[...] = (xn * _rstd(xn) * gn_ref[...]).astype(BF16)

    blk = pl.BlockSpec((tm, d), lambda i: (i, 0))
    return pl.pallas_call(
        body, name=name, grid=(m // tm,), in_specs=[blk, blk, _row(d), _row(d)], out_specs=[blk, blk],
        out_shape=[jax.ShapeDtypeStruct((m, d), F32), jax.ShapeDtypeStruct((m, d), BF16)],
        compiler_params=_params(("parallel",)))(x, y, g_post, g_pre)


def _loss_head(name, x, y, g_post, tgt, tm):
    m, d = x.shape

    def body(x_ref, y_ref, g_ref, t_ref, dy_ref, loss_ref):
        yv = y_ref[...]
        err = x_ref[...] + yv * _rstd(yv) * g_ref[...] - t_ref[...]
        dy_ref[...] = err * (1.0 / d)

        @pl.when(pl.program_id(0) == 0)
        def _():
            loss_ref[...] = jnp.zeros_like(loss_ref)

        loss_ref[...] += 0.5 * jnp.sum(jnp.mean(err * err, axis=-1, keepdims=True), axis=0, keepdims=True)

    blk = pl.BlockSpec((tm, d), lambda i: (i, 0))
    return pl.pallas_call(
        body, name=name, grid=(m // tm,), in_specs=[blk, blk, _row(d), blk],
        out_specs=[blk, pl.BlockSpec((1, 1), lambda i: (0, 0))],
        out_shape=[jax.ShapeDtypeStruct((m, d), F32), jax.ShapeDtypeStruct((1, 1), F32)],
        compiler_params=_params(("arbitrary",)))(x, y, g_post, tgt)


def _norm_bwd(name, xin, g, dy, resid, out_dtype, tm):
    m, d = xin.shape

    def body(*refs):
        if resid is None:
            x_ref, g_ref, dy_ref, dx_ref, dg_ref = refs
        else:
            x_ref, g_ref, dy_ref, r_ref, dx_ref, dg_ref = refs
        xv = x_ref[...]
        r = _rstd(xv)
        xh = xv * r
        dyv = dy_ref[...].astype(F32)
        dyg = dyv * g_ref[...]
        dx = r * (dyg - xh * jnp.mean(dyg * xh, axis=-1, keepdims=True))
        if resid is not None:
            dx = dx + r_ref[...]
        dx_ref[...] = dx.astype(dx_ref.dtype)

        @pl.when(pl.program_id(0) == 0)
        def _():
            dg_ref[...] = jnp.zeros_like(dg_ref)

        dg_ref[...] += jnp.sum(dyv * xh, axis=0, keepdims=True)

    blk = pl.BlockSpec((tm, d), lambda i: (i, 0))
    ops = [xin, g, dy] + ([] if resid is None else [resid])
    specs = [blk, _row(d), blk] + ([] if resid is None else [blk])
    return pl.pallas_call(
        body, name=name, grid=(m // tm,), in_specs=specs, out_specs=[blk, _row(d)],
        out_shape=[jax.ShapeDtypeStruct((m, d), out_dtype), jax.ShapeDtypeStruct((1, d), F32)],
        compiler_params=_params(("arbitrary",)))(*ops)


def _adamw(name, g, w, m, v, tr):
    rows, cols = g.shape
    c1 = 1.0 - B1 ** STEP
    c2 = 1.0 - B2 ** STEP

    def body(g_ref, w_ref, m_ref, v_ref, d_ref, mo_ref, vo_ref):
        gv = g_ref[...]
        mn = B1 * m_ref[...] + (1.0 - B1) * gv
        vn = B2 * v_ref[...] + (1.0 - B2) * (gv * gv)
        mo_ref[...] = mn
        vo_ref[...] = vn
        d_ref[...] = -LR * ((mn / c1) / (jnp.sqrt(vn / c2) + ADAM_EPS) + WD * w_ref[...])

    blk = pl.BlockSpec((tr, cols), lambda i: (i, 0))
    sd = jax.ShapeDtypeStruct((rows, cols), F32)
    return pl.pallas_call(body, name=name, grid=(rows // tr,), in_specs=[blk] * 4, out_specs=[blk] * 3,
                          out_shape=[sd, sd, sd], compiler_params=_params(("parallel",)))(g, w, m, v)


def _gelu_parts(x):
    c = 0.7978845608028654
    t = jnp.tanh(c * (x + 0.044715 * (x * x * x)))
    return 0.5 * x * (1.0 + t), t


def _gelu_grad(x, t):
    c = 0.7978845608028654
    return 0.5 * (1.0 + t) + 0.5 * x * (1.0 - t * t) * (c * (1.0 + 3.0 * 0.044715 * x * x))


def _rot_half(x):
    ax = x.ndim - 1
    w = x.shape[ax]
    lane = lax.broadcasted_iota(jnp.int32, x.shape, ax)
    return jnp.where((lane & 63) < 32, pltpu.roll(x, w - 32, ax), pltpu.roll(x, 32, ax))


def _gm_group(gel, g, gv, ws_ref, bt):
    u = gel[:, HEAD * g:HEAD * (g + 1)]
    vg = gel[:, 256 + HEAD * g:256 + HEAD * (g + 1)]
    r = _rstd(vg)
    xh = vg * r
    vn = (xh * gv[:, HEAD * g:HEAD * (g + 1)]).astype(BF16)
    row = lax.broadcasted_iota(jnp.int32, (CHUNK, CHUNK), 0)
    col = lax.broadcasted_iota(jnp.int32, (CHUNK, CHUNK), 1)
    causal = col <= row
    wc = jnp.where(causal, ws_ref[g], 0.0).astype(BF16)
    mixed = _dot(wc, vn) + bt[:, g:g + 1]
    return u, r, xh, vn, wc, causal, mixed


def _lane_select(lane, vals):
    return jnp.where(lane < 64, vals[0], jnp.where(lane < 128, vals[1], jnp.where(lane < 192, vals[2], vals[3])))


def _pool_fwd(pc, pp, ci):
    ext = jnp.concatenate([pp, pc], axis=0)
    s2 = ext + pltpu.roll(ext, 1, 0)
    s4 = s2 + pltpu.roll(s2, 2, 0)
    s8 = s4 + pltpu.roll(s4, 4, 0)
    s16 = s8 + pltpu.roll(s8, 8, 0)
    t1 = ci * CHUNK + lax.broadcasted_iota(jnp.int32, (CHUNK, 1), 0) + 1
    lane = lax.broadcasted_iota(jnp.int32, (1, 256), 1)
    cnt = _lane_select(lane, [jnp.minimum(t1, w).astype(F32) for w in POOL_WINDOWS])
    ssel = _lane_select(lane, [s[CHUNK:] for s in (s2, s4, s8, s16)])
    return ssel / cnt - pc, cnt, lane


def _attn_prep(zc, zpkv, cq, sq, cp, sp, ci):
    q = zc[:, 768:1280]
    kc = zc[:, 1280:1408]
    vc = zc[:, 1408:1536]
    kp = zpkv[:, :128]
    vp = zpkv[:, 128:]
    qr = q * cq + _rot_half(q) * sq
    krc = kc * cq[:, :128] + _rot_half(kc) * sq[:, :128]
    krp = kp * cp + _rot_half(kp) * sp
    kband = jnp.concatenate([krp, krc], axis=0).astype(BF16)
    vband = jnp.concatenate([vp, vc], axis=0).astype(BF16)
    row = lax.broadcasted_iota(jnp.int32, (CHUNK, 2 * CHUNK), 0)
    col = lax.broadcasted_iota(jnp.int32, (CHUNK, 2 * CHUNK), 1)
    valid = ((col < CHUNK) & (col > row) & (ci > 0)) | ((col >= CHUNK) & (col - CHUNK <= row))
    return qr, kband, vband, valid


def _head_probs(qh, kh, valid, sink):
    s = _dot(qh, kh, NT) * (HEAD ** -0.5)
    s = jnp.where(valid, s, -1e30)
    mx = jnp.maximum(jnp.max(s, axis=-1, keepdims=True), sink)
    e = jnp.exp(s - mx)
    es = jnp.exp(sink - mx)
    den = jnp.sum(e, axis=-1, keepdims=True) + es
    return e / den, es / den


def _mixer_specs(nb, rev):
    def cur(i):
        return nb - 1 - i if rev else i

    def prev(i):
        return jnp.maximum(cur(i) - 1, 0)

    full = lambda shape: pl.BlockSpec(shape, lambda i: (0,) * len(shape))
    specs = [
        pl.BlockSpec((CHUNK, 1536), lambda i: (cur(i), 0)),
        pl.BlockSpec((CHUNK, 256), lambda i: (prev(i), 2)),
        pl.BlockSpec((CHUNK, 256), lambda i: (prev(i), 5)),
        pl.BlockSpec((CHUNK, 512), lambda i: (cur(i), 0)),
        pl.BlockSpec((CHUNK, 512), lambda i: (cur(i), 0)),
        pl.BlockSpec((CHUNK, 128), lambda i: (prev(i), 0)),
        pl.BlockSpec((CHUNK, 128), lambda i: (prev(i), 0)),
        full((1, 256)), full((4, CHUNK, CHUNK)), full((CHUNK, 4)), full((256, 256)), full((1, 256)), full((1, 8)),
    ]
    return specs, cur


def _mixer_fwd(name, z, cosq, sinq, gv, ws, bt, pw, psc, snk):
    s = z.shape[0]
    nb = s // CHUNK
    specs, _ = _mixer_specs(nb, False)

    def body(zc_ref, zpp_ref, zpkv_ref, cq_ref, sq_ref, cp_ref, sp_ref, gv_ref, ws_ref, bt_ref, pw_ref, psc_ref,
             snk_ref, o_ref):
        ci = pl.program_id(0)
        zc = zc_ref[...]
        gel, _ = _gelu_parts(zc[:, :512])
        gvv = gv_ref[...]
        btv = bt_ref[...]
        for g in range(4):
            u, _, _, _, _, _, mixed = _gm_group(gel, g, gvv, ws_ref, btv)
            o_ref[:, HEAD * g:HEAD * (g + 1)] = (u * mixed).astype(BF16)
        pp = jnp.where(ci > 0, zpp_ref[...], 0.0)
        pooled, _, _ = _pool_fwd(zc[:, 512:768], pp, ci)
        mp = _dot(pooled.astype(BF16), pw_ref[...].astype(BF16))
        o_ref[:, 256:512] = (mp * psc_ref[...]).astype(BF16)
        qr, kband, vband, valid = _attn_prep(zc, zpkv_ref[...], cq_ref[...], sq_ref[...], cp_ref[...], sp_ref[...], ci)
        snkv = snk_ref[...]
        for h in range(8):
            hk = h // 4
            p, _ = _head_probs(qr[:, HEAD * h:HEAD * (h + 1)].astype(BF16), kband[:, HEAD * hk:HEAD * (hk + 1)],
                               valid, snkv[:, h:h + 1])
            o = _dot(p.astype(BF16), vband[:, HEAD * hk:HEAD * (hk + 1)])
            o_ref[:, 512 + HEAD * h:512 + HEAD * (h + 1)] = o.astype(BF16)

    return pl.pallas_call(
        body, name=name, grid=(nb,), in_specs=specs, out_specs=pl.BlockSpec((CHUNK, 1024), lambda i: (i, 0)),
        out_shape=jax.ShapeDtypeStruct((s, 1024), BF16), compiler_params=_params(("parallel",)),
    )(z, z, z, cosq, sinq, cosq, sinq, gv, ws, bt, pw, psc, snk)


def _mixer_bwd(name, z, dabc, cosq, sinq, gv, ws, bt, pw, psc, snk):
    s = z.shape[0]
    nb = s // CHUNK
    specs, cur = _mixer_specs(nb, True)
    specs = specs + [pl.BlockSpec((CHUNK, 1024), lambda i: (cur(i), 0))]
    full = lambda shape: pl.BlockSpec(shape, lambda i: (0,) * len(shape))
    acc_shapes = [(1, 256), (4, CHUNK, CHUNK), (CHUNK, 4), (256, 256), (1, 256), (1, 8)]

    def body(zc_ref, zpp_ref, zpkv_ref, cq_ref, sq_ref, cp_ref, sp_ref, gv_ref, ws_ref, bt_ref, pw_ref, psc_ref,
             snk_ref, dabc_ref, dz_ref, dgv_ref, dws_ref, dbt_ref, dpw_ref, dpsc_ref, dsnk_ref,
             cpool, ck, cv, dq_s, dkv_s):
        step = pl.program_id(0)
        ci = nb - 1 - step

        @pl.when(step == 0)
        def _():
            for r in (dgv_ref, dws_ref, dbt_ref, dpw_ref, dpsc_ref, dsnk_ref, cpool, ck, cv):
                r[...] = jnp.zeros_like(r)

        zc = zc_ref[...]
        dabc = dabc_ref[...]
        zg = zc[:, :512]
        gel, th = _gelu_parts(zg)
        gp = _gelu_grad(zg, th)
        gvv = gv_ref[...]
        btv = bt_ref[...]
        lane4 = lax.broadcasted_iota(jnp.int32, (CHUNK, 4), 1)
        dbt = jnp.zeros((CHUNK, 4), F32)
        for g in range(4):
            lo, hi = HEAD * g, HEAD * (g + 1)
            u, r, xh, vn, wc, causal, mixed = _gm_group(gel, g, gvv, ws_ref, btv)
            da = dabc[:, lo:hi]
            dm = da * u
            dmb = dm.astype(BF16)
            dws_ref[g] += jnp.where(causal, _dot(dmb, vn, NT), 0.0)
            dbt = dbt + jnp.where(lane4 == g, jnp.sum(dm, axis=-1, keepdims=True), 0.0)
            dvn = _dot(wc, dmb, TN)
            dgv_ref[:, lo:hi] += jnp.sum(dvn * xh, axis=0, keepdims=True)
            dxh = dvn * gvv[:, lo:hi]
            dvg = r * (dxh - xh * jnp.mean(dxh * xh, axis=-1, keepdims=True))
            dz_ref[:, lo:hi] = (da * mixed * gp[:, lo:hi]).astype(BF16)
            dz_ref[:, 256 + lo:256 + hi] = (dvg * gp[:, 256 + lo:256 + hi]).astype(BF16)
        dbt_ref[...] += dbt
        pc = zc[:, 512:768]
        pp = jnp.where(ci > 0, zpp_ref[...], 0.0)
        pooled, cnt, lane = _pool_fwd(pc, pp, ci)
        pwb = pw_ref[...].astype(BF16)
        pooled_b = pooled.astype(BF16)
        mp = _dot(pooled_b, pwb)
        db = dabc[:, 256:512]
        dpsc_ref[...] += jnp.sum(db * mp, axis=0, keepdims=True)
        dmpb = (db * psc_ref[...]).astype(BF16)
        dpw_ref[...] += _dot(pooled_b, dmpb, TN)
        dpooled = _dot(dmpb, pwb, NT)
        davg = dpooled / cnt
        zero = jnp.zeros((CHUNK, 256), F32)
        d2, d4, d8, d16 = [jnp.concatenate([zero, jnp.where((lane >= 64 * k) & (lane < 64 * (k + 1)), davg, 0.0)],
                                           axis=0) for k in range(4)]
        g8 = d8 + d16 + pltpu.roll(d16, 2 * CHUNK - 8, 0)
        g4 = d4 + g8 + pltpu.roll(g8, 2 * CHUNK - 4, 0)
        g2 = d2 + g4 + pltpu.roll(g4, 2 * CHUNK - 2, 0)
        ge = g2 + pltpu.roll(g2, 2 * CHUNK - 1, 0)
        dz_ref[:, 512:768] = (ge[CHUNK:] - dpooled + cpool[...]).astype(BF16)
        cpool[...] = ge[:CHUNK]
        cq = cq_ref[...]
        sq = sq_ref[...]
        qr, kband, vband, valid = _attn_prep(zc, zpkv_ref[...], cq, sq, cp_ref[...], sp_ref[...], ci)
        snkv = snk_ref[...]
        lane8 = lax.broadcasted_iota(jnp.int32, (1, 8), 1)
        dsnk = jnp.zeros((1, 8), F32)
        for hk in range(2):
            kh = kband[:, HEAD * hk:HEAD * (hk + 1)]
            vh = vband[:, HEAD * hk:HEAD * (hk + 1)]
            dkh = jnp.zeros((2 * CHUNK, HEAD), F32)
            dvh = jnp.zeros((2 * CHUNK, HEAD), F32)
            for h in range(4 * hk, 4 * hk + 4):
                qh = qr[:, HEAD * h:HEAD * (h + 1)].astype(BF16)
                p, ps = _head_probs(qh, kh, valid, snkv[:, h:h + 1])
                dob = dabc[:, 512 + HEAD * h:512 + HEAD * (h + 1)].astype(BF16)
                dp = _dot(dob, vh, NT)
                dd = jnp.sum(p * dp, axis=-1, keepdims=True)
                dsnk = dsnk + jnp.where(lane8 == h, jnp.sum(-ps * dd, axis=0, keepdims=True), 0.0)
                dsb = (p * (dp - dd) * (HEAD ** -0.5)).astype(BF16)
                dq_s[:, HEAD * h:HEAD * (h + 1)] = _dot(dsb, kh)
                dkh = dkh + _dot(dsb, qh, TN)
                dvh = dvh + _dot(p.astype(BF16), dob, TN)
            dkv_s[:, HEAD * hk:HEAD * (hk + 1)] = dkh
            dkv_s[:, 128 + HEAD * hk:128 + HEAD * (hk + 1)] = dvh
        dsnk_ref[...] += dsnk
        dqr = dq_s[...]
        dz_ref[:, 768:1280] = (dqr * cq + _rot_half(dqr * sq)).astype(BF16)
        dkv = dkv_s[...]
        dkr = dkv[CHUNK:, :128] + ck[...]
        dz_ref[:, 1280:1408] = (dkr * cq[:, :128] + _rot_half(dkr * sq[:, :128])).astype(BF16)
        dz_ref[:, 1408:1536] = (dkv[CHUNK:, 128:] + cv[...]).astype(BF16)
        ck[...] = dkv[:CHUNK, :128]
        cv[...] = dkv[:CHUNK, 128:]

    return pl.pallas_call(
        body, name=name, grid=(nb,), in_specs=specs,
        out_specs=[pl.BlockSpec((CHUNK, 1536), lambda i: (cur(i), 0))] + [full(a) for a in acc_shapes],
        out_shape=[jax.ShapeDtypeStruct((s, 1536), BF16)] + [jax.ShapeDtypeStruct(a, F32) for a in acc_shapes],
        scratch_shapes=[pltpu.VMEM((CHUNK, 256), F32), pltpu.VMEM((CHUNK, 128), F32), pltpu.VMEM((CHUNK, 128), F32),
                        pltpu.VMEM((CHUNK, 512), F32), pltpu.VMEM((2 * CHUNK, 256), F32)],
        compiler_params=_params(("arbitrary",)),
    )(z, z, z, cosq, sinq, cosq, sinq, gv, ws, bt, pw, psc, snk, dabc)


def _xattn_probs(qh, kh):
    s = _dot(qh, kh, NT) * (256 ** -0.5)
    e = jnp.exp(s - jnp.max(s, axis=-1, keepdims=True))
    return e / jnp.sum(e, axis=-1, keepdims=True)


def _xattn_fwd(name, q, kv, tq):
    s, d = q.shape
    mlen = kv.shape[0]

    def body(q_ref, kv_ref, o_ref):
        for h in range(4):
            lo, hi = 256 * h, 256 * (h + 1)
            p = _xattn_probs(q_ref[:, lo:hi], kv_ref[:, lo:hi])
            o_ref[:, lo:hi] = _dot(p.astype(BF16), kv_ref[:, d + lo:d + hi]).astype(BF16)

    blk = pl.BlockSpec((tq, d), lambda i: (i, 0))
    return pl.pallas_call(body, name=name, grid=(s // tq,),
                          in_specs=[blk, pl.BlockSpec((mlen, 2 * d), lambda i: (0, 0))], out_specs=blk,
                          out_shape=jax.ShapeDtypeStruct((s, d), BF16), compiler_params=_params(("parallel",)))(q, kv)


def _xattn_bwd(name, q, kv, do, tq):
    s, d = q.shape
    mlen = kv.shape[0]

    def body(q_ref, kv_ref, do_ref, dq_ref, dkv_ref):
        @pl.when(pl.program_id(0) == 0)
        def _():
            dkv_ref[...] = jnp.zeros_like(dkv_ref)

        for h in range(4):
            lo, hi = 256 * h, 256 * (h + 1)
            qh = q_ref[:, lo:hi]
            kh = kv_ref[:, lo:hi]
            vh = kv_ref[:, d + lo:d + hi]
            doh = do_ref[:, lo:hi]
            p = _xattn_probs(qh, kh)
            dp = _dot(doh, vh, NT)
            dsb = (p * (dp - jnp.sum(p * dp, axis=-1, keepdims=True)) * (256 ** -0.5)).astype(BF16)
            dq_ref[:, lo:hi] = _dot(dsb, kh).astype(BF16)
            dkv_ref[:, lo:hi] += _dot(dsb, qh, TN)
            dkv_ref[:, d + lo:d + hi] += _dot(p.astype(BF16), doh, TN)

    blk = pl.BlockSpec((tq, d), lambda i: (i, 0))
    kvb = pl.BlockSpec((mlen, 2 * d), lambda i: (0, 0))
    return pl.pallas_call(
        body, name=name, grid=(s // tq,), in_specs=[blk, kvb, blk], out_specs=[blk, kvb],
        out_shape=[jax.ShapeDtypeStruct((s, d), BF16), jax.ShapeDtypeStruct((mlen, 2 * d), F32)],
        compiler_params=_params(("arbitrary",)))(q, kv, do)


def _ffn_up(name, h, wgu, layer, tm, tn):
    s, d = h.shape
    dff = wgu.shape[2] // 2
    nj = dff // tn

    def body(h_ref, wg_ref, wu_ref, g_ref, u_ref, a_ref):
        hv = h_ref[...]
        gate = _dot(hv, wg_ref[...])
        up = _dot(hv, wu_ref[...])
        g_ref[...] = gate.astype(BF16)
        u_ref[...] = up.astype(BF16)
        a_ref[...] = (gate / (1.0 + jnp.exp(-gate)) * up).astype(BF16)

    ob = pl.BlockSpec((tm, tn), lambda j, i: (i, j))
    sd = jax.ShapeDtypeStruct((s, dff), BF16)
    return pl.pallas_call(
        body, name=name, grid=(nj, s // tm),
        in_specs=[pl.BlockSpec((tm, d), lambda j, i: (i, 0)),
                  pl.BlockSpec((None, d, tn), lambda j, i: (layer, 0, j)),
                  pl.BlockSpec((None, d, tn), lambda j, i: (layer, 0, j + nj))],
        out_specs=[ob, ob, ob], out_shape=[sd, sd, sd], compiler_params=_params(("parallel", "parallel")),
    )(h, wgu, wgu)


def _ffn_act_bwd(name, dfn, wdown, layer, gate, up, tm, tn):
    s, d = dfn.shape
    dff = gate.shape[1]

    def body(df_ref, wd_ref, g_ref, u_ref, dg_ref, du_ref):
        dact = _dot(df_ref[...], wd_ref[...], NT)
        gate = g_ref[...].astype(F32)
        sig = 1.0 / (1.0 + jnp.exp(-gate))
        du_ref[...] = (dact * gate * sig).astype(BF16)
        dg_ref[...] = (dact * u_ref[...].astype(F32) * sig * (1.0 + gate * (1.0 - sig))).astype(BF16)

    ob = pl.BlockSpec((tm, tn), lambda j, i: (i, j))
    sd = jax.ShapeDtypeStruct((s, dff), BF16)
    return pl.pallas_call(
        body, name=name, grid=(dff // tn, s // tm),
        in_specs=[pl.BlockSpec((tm, d), lambda j, i: (i, 0)),
                  pl.BlockSpec((None, tn, d), lambda j, i: (layer, j, 0)), ob, ob],
        out_specs=[ob, ob], out_shape=[sd, sd], compiler_params=_params(("parallel", "parallel")),
    )(dfn, wdown, gate, up)


def _place():
    return lax.axis_index("x"), lax.axis_index("y"), lax.axis_index("c")


def _other_chips(x, y):
    return [(1 - x, y), (x, 1 - y), (1 - x, 1 - y)]


def _region(ref, axis, chip, size):
    start = pl.multiple_of(chip * size, size)
    if axis == 1:
        return ref.at[:, pl.ds(start, size), :]
    return ref.at[:, :, pl.ds(start, size)]


ANY = pl.BlockSpec(memory_space=pl.ANY)


HBM = pl.BlockSpec(memory_space=pltpu.HBM)
SEM = pl.BlockSpec(memory_space=pltpu.SEMAPHORE)
EFFECT = pltpu.SideEffectType.DATAFLOW_SIDE_EFFECTING


def _in_hbm(a):
    return pltpu.with_memory_space_constraint(a, pltpu.HBM)


def _split_start(name, srcs, lands, ncopies, plan):
    ns, nl = len(srcs), len(lands)

    def body(*refs):
        src, land = refs[:ns], refs[ns:ns + nl]
        send, recv = refs[ns + nl], refs[ns + nl + 1]
        token = refs[-1]
        x, y, c = _place()
        for k, (s_ref, d_ref, peer, _) in enumerate(plan(src, land, x, y, c)):
            pltpu.make_async_remote_copy(src_ref=s_ref, dst_ref=d_ref, send_sem=send.at[k], recv_sem=recv.at[k],
                                         device_id=peer, device_id_type=MESH).start()
        token[...] = jnp.zeros_like(token)

    ops = list(srcs) + list(lands)
    out = pl.pallas_call(
        body, name=name,
        out_shape=(pltpu.SemaphoreType.DMA((ncopies,)), pltpu.SemaphoreType.DMA((ncopies,)),
                   *[pltpu.HBM(a.shape, a.dtype) for a in ops], jax.ShapeDtypeStruct((8, 128), F32)),
        in_specs=(HBM,) * (ns + nl), out_specs=(SEM, SEM) + (HBM,) * (ns + nl) + (pl.BlockSpec(memory_space=pltpu.VMEM),),
        input_output_aliases={i: 2 + i for i in range(ns + nl)},
        compiler_params=pltpu.CompilerParams(has_side_effects=EFFECT),
    )(*[_in_hbm(a) for a in ops])
    return out[0], out[1], list(out[2:2 + ns]), list(out[2 + ns:2 + ns + nl]), out[-1]


def _split_wait(name, send, recv, srcs, lands, after, plan):
    ns, nl = len(srcs), len(lands)

    def body(*refs):
        src, land = refs[:ns], refs[ns:ns + nl]
        send_ref, recv_ref = refs[ns + nl], refs[ns + nl + 1]
        x, y, c = _place()
        for k, (s_ref, _, _, got) in enumerate(plan(src, land, x, y, c)):
            cp = pltpu.make_async_remote_copy(src_ref=s_ref, dst_ref=got, send_sem=send_ref.at[k],
                                              recv_sem=recv_ref.at[k], device_id=(x, y, c), device_id_type=MESH)
            cp.wait_send()
            cp.wait_recv()

    ops = list(srcs) + list(lands)
    out = pl.pallas_call(
        body, name=name, out_shape=tuple(pltpu.HBM(a.shape, a.dtype) for a in ops),
        in_specs=(HBM,) * (ns + nl) + (SEM, SEM, ANY), out_specs=(HBM,) * (ns + nl),
        input_output_aliases={i: i for i in range(ns + nl)},
        compiler_params=pltpu.CompilerParams(has_side_effects=EFFECT),
    )(*ops, send, recv, after)
    return list(out[:ns]), list(out[ns:])


def _gather_plan(axes, sizes, layer):
    def plan(src, land, x, y, c):
        me = 2 * x + y
        out = []
        for t in range(len(land)):
            mine = _region(land[t], axes[t], me, sizes[t]).at[pl.ds(layer, 1)]
            for px, py in _other_chips(x, y):
                out.append((mine, mine, (px, py, c),
                            _region(land[t], axes[t], 2 * px + py, sizes[t]).at[pl.ds(layer, 1)]))
        return out
    return plan


def _place_own(name, w, axis, chip):
    nl, r, cs = w.shape
    tr = _rows_tile(r)
    nb = r // tr
    full = (nl, 4 * r, cs) if axis == 1 else (nl, r, 4 * cs)

    def body(m_ref, w_ref, o_ref):
        o_ref[...] = w_ref[...].astype(BF16)

    if axis == 1:
        ospec = pl.BlockSpec((None, tr, cs), lambda l, i, m: (l, m[0] * nb + i, 0))
    else:
        ospec = pl.BlockSpec((None, tr, cs), lambda l, i, m: (l, i, m[0]))
    return pl.pallas_call(
        body, name=name,
        grid_spec=pltpu.PrefetchScalarGridSpec(
            num_scalar_prefetch=1, grid=(nl, nb),
            in_specs=[pl.BlockSpec((None, tr, cs), lambda l, i, m: (l, i, 0))], out_specs=ospec),
        out_shape=jax.ShapeDtypeStruct(full, BF16), compiler_params=_params(("parallel", "parallel")),
    )(chip, w)


def _scatter_plan(axes, sizes):
    def plan(src, land, x, y, c):
        out = []
        for t in range(len(src)):
            for k, (px, py) in enumerate(_other_chips(x, y)):
                out.append((_region(src[t], axes[t], 2 * px + py, sizes[t]).at[0], land[t].at[k], (px, py, c),
                            land[t].at[k]))
        return out
    return plan


def _pair_plan(src, land, x, y, c):
    return [(src[t], land[t], (x, y, 1 - c), land[t]) for t in range(len(src))]


def _chip_sum(name, g, slots, axis, chip):
    _, r, cs = slots.shape
    tr = _rows_tile(r)
    nb = r // tr

    def body(m_ref, g_ref, s_ref, o_ref):
        acc = g_ref[...].astype(F32)
        for k in range(3):
            acc = acc + s_ref[k].astype(F32)
        o_ref[...] = acc

    if axis == 1:
        gspec = pl.BlockSpec((tr, cs), lambda i, m: (m[0] * nb + i, 0))
    else:
        gspec = pl.BlockSpec((tr, cs), lambda i, m: (i, m[0]))
    return pl.pallas_call(
        body, name=name,
        grid_spec=pltpu.PrefetchScalarGridSpec(
            num_scalar_prefetch=1, grid=(nb,),
            in_specs=[gspec, pl.BlockSpec((3, tr, cs), lambda i, m: (0, i, 0))],
            out_specs=pl.BlockSpec((tr, cs), lambda i, m: (i, 0))),
        out_shape=jax.ShapeDtypeStruct((r, cs), F32), compiler_params=_params(("parallel",)),
    )(chip, g, slots)


def _pair_adamw(name, mine, theirs, w, m, v, layer, bufs):
    r, cs = mine.shape
    tr = _rows_tile(r)
    c1 = 1.0 - B1 ** STEP
    c2 = 1.0 - B2 ** STEP

    def body(a_ref, b_ref, w_ref, m_ref, v_ref, _g, _d, _m, _v, g_ref, d_ref, mo_ref, vo_ref):
        gv = a_ref[...] + b_ref[...]
        mn = B1 * m_ref[...] + (1.0 - B1) * gv
        vn = B2 * v_ref[...] + (1.0 - B2) * (gv * gv)
        g_ref[...] = gv
        mo_ref[...] = mn
        vo_ref[...] = vn
        d_ref[...] = -LR * ((mn / c1) / (jnp.sqrt(vn / c2) + ADAM_EPS) + WD * w_ref[...])

    blk = pl.BlockSpec((tr, cs), lambda i: (i, 0))
    lay = pl.BlockSpec((None, tr, cs), lambda i: (layer, i, 0))
    return pl.pallas_call(
        body, name=name, grid=(r // tr,), in_specs=[blk, blk, lay, lay, lay] + [ANY] * 4, out_specs=[lay] * 4,
        out_shape=[jax.ShapeDtypeStruct(b.shape, b.dtype) for b in bufs],
        input_output_aliases={5 + i: i for i in range(4)}, compiler_params=_params(("parallel",)),
    )(mine, theirs, w, m, v, *bufs)


def _allreduce_small(p):
    rows = p.shape[0]

    def body(p_ref, o_ref, gath, send, recv):
        x, y, c = _place()
        me = 4 * x + 2 * y + c

        def flip(v, bit):
            return 1 - v if bit else v

        gath[me] = p_ref[...]
        cps = []
        for k in range(1, 8):
            peer = (flip(x, k & 4), flip(y, k & 2), flip(c, k & 1))
            cps.append(pltpu.make_async_remote_copy(src_ref=p_ref, dst_ref=gath.at[me], send_sem=send.at[k - 1],
                                                    recv_sem=recv.at[k - 1], device_id=peer, device_id_type=MESH))
        for cp in cps:
            cp.start()
        for k in range(1, 8):
            slot = gath.at[4 * flip(x, k & 4) + 2 * flip(y, k & 2) + flip(c, k & 1)]
            pltpu.make_async_remote_copy(src_ref=slot, dst_ref=slot, send_sem=send.at[k - 1], recv_sem=recv.at[k - 1],
                                         device_id=(x, y, c), device_id_type=MESH).wait_recv()
        for cp in cps:
            cp.wait_send()
        acc = gath[0]
        for k in range(1, 8):
            acc = acc + gath[k]
        o_ref[...] = acc

    vm = pl.BlockSpec(memory_space=pltpu.VMEM)
    return pl.pallas_call(
        body, name="allreduce_small", in_specs=[vm], out_specs=vm, out_shape=jax.ShapeDtypeStruct(p.shape, F32),
        scratch_shapes=[pltpu.VMEM((8, rows, 128), F32), pltpu.SemaphoreType.DMA((7,)), pltpu.SemaphoreType.DMA((7,))],
        compiler_params=pltpu.CompilerParams(vmem_limit_bytes=VMEM_LIMIT),
    )(p)


def _pack(parts):
    flat = []
    for p in parts:
        v = p.reshape(-1).astype(F32)
        flat.append(jnp.pad(v, (0, (-v.shape[0]) % 128)))
    v = jnp.concatenate(flat)
    v = jnp.pad(v, (0, (-v.shape[0]) % (512 * 128)))
    return v.reshape(-1, 128)


def _unpack(buf, like):
    v = buf.reshape(-1)
    out, off = [], 0
    for p in like:
        nelem = 1
        for s in p.shape:
            nelem *= s
        out.append(v[off:off + nelem].reshape(p.shape))
        off += nelem + (-nelem) % 128
    return out


def kernel(x, mem, positions, mem_norm_g, mix_pre_g, mix_post_g, w_in, gm_v_g, gm_w_s, gm_b_s, pool_w, pool_scale, attn_sinks, w_o, x_pre_g, x_post_g, w_xq, w_xkv, w_xo, ffn_pre_g, ffn_post_g, w_gate_up, w_down, loss_target, m_mem_norm_g, m_mix_pre_g, m_mix_post_g, m_w_in, m_gm_v_g, m_gm_w_s, m_gm_b_s, m_pool_w, m_pool_scale, m_attn_sinks, m_w_o, m_x_pre_g, m_x_post_g, m_w_xq, m_w_xkv, m_w_xo, m_ffn_pre_g, m_ffn_post_g, m_w_gate_up, m_w_down, v_mem_norm_g, v_mix_pre_g, v_mix_post_g, v_w_in, v_gm_v_g, v_gm_w_s, v_gm_b_s, v_pool_w, v_pool_scale, v_attn_sinks, v_w_o, v_x_pre_g, v_x_post_g, v_w_xq, v_w_xkv, v_w_xo, v_ffn_pre_g, v_ffn_post_g, v_w_gate_up, v_w_down):
    args = (x, mem, positions, mem_norm_g, mix_pre_g, mix_post_g, w_in, gm_v_g, gm_w_s, gm_b_s, pool_w, pool_scale, attn_sinks, w_o, x_pre_g, x_post_g, w_xq, w_xkv, w_xo, ffn_pre_g, ffn_post_g, w_gate_up, w_down)
    moms_m = (m_mem_norm_g, m_mix_pre_g, m_mix_post_g, m_w_in, m_gm_v_g, m_gm_w_s, m_gm_b_s, m_pool_w, m_pool_scale, m_attn_sinks, m_w_o, m_x_pre_g, m_x_post_g, m_w_xq, m_w_xkv, m_w_xo, m_ffn_pre_g, m_ffn_post_g, m_w_gate_up, m_w_down)
    moms_v = (v_mem_norm_g, v_mix_pre_g, v_mix_post_g, v_w_in, v_gm_v_g, v_gm_w_s, v_gm_b_s, v_pool_w, v_pool_scale, v_attn_sinks, v_w_o, v_x_pre_g, v_x_post_g, v_w_xq, v_w_xkv, v_w_xo, v_ffn_pre_g, v_ffn_post_g, v_w_gate_up, v_w_down)
    P = dict(zip(NAMES, args))
    P['loss_target'] = loss_target
    M = dict(zip(WEIGHTS, moms_m))
    V = dict(zip(WEIGHTS, moms_v))
    depth = w_in.shape[0]
    nbig = len(BIG)
    axes = [BIG_AXIS[n] for n in BIG]
    sizes = [P[n].shape[a] for n, a in zip(BIG, axes)]
    chip = (2 * lax.axis_index("x") + lax.axis_index("y")).astype(jnp.int32).reshape(1)

    groups = [['w_in'], ['w_o', 'w_xq', 'w_xkv', 'w_xo'], ['w_gate_up', 'w_down']]
    units = [(0, g) for g in groups] + [(l, BIG) for l in range(1, depth)]
    unit_of = {(l, n): i for i, (l, names) in enumerate(units) for n in names}
    ax = lambda names: [BIG_AXIS[n] for n in names]
    sz = lambda names: [P[n].shape[BIG_AXIS[n]] for n in names]

    full = {n: _place_own("place_" + n, P[n], BIG_AXIS[n], chip) for n in BIG}
    gathers = []
    for i, (l, names) in enumerate(units):
        send, recv, _, land, _ = _split_start("gather_start%d" % i, [], [full[n] for n in names],
                                              3 * len(names), _gather_plan(ax(names), sz(names), l))
        full.update(zip(names, land))
        gathers.append((send, recv))
    gathered = set()

    def weights_of(l, names, after):
        i = unit_of[(l, names[0])]
        if i not in gathered:
            _, unames = units[i]
            send, recv = gathers[i]
            _, land = _split_wait("gather_wait%d" % i, send, recv, [], [full[n] for n in unames], after,
                                  _gather_plan(ax(unames), sz(unames), l))
            full.update(zip(unames, land))
            gathered.add(i)
        return {n: (full[n], l) for n in names}

    outs = {n: [lax.empty(P[n].shape, F32) for _ in range(4)] for n in BIG}
    gunits = [(l, BIG) for l in range(depth - 1, 0, -1)] + [
        (0, g) for g in (['w_gate_up', 'w_down'], ['w_xq', 'w_xkv', 'w_xo'], ['w_o'], ['w_in'])]
    collected, scatters, pairs = {}, {}, {}

    def finish_scatter(i, after):
        _, names = gunits[i]
        send, recv, g_l, slots = scatters.pop(i)
        g_l, slots = _split_wait("scatter_wait%d" % i, send, recv, g_l, slots, after,
                                 _scatter_plan(ax(names), sz(names)))
        mine = [_chip_sum("chip_sum_" + n, g.reshape(g.shape[1:]), sl, BIG_AXIS[n], chip)
                for n, g, sl in zip(names, g_l, slots)]
        send, recv, mine, theirs, tok = _split_start("pair_start%d" % i, mine, [lax.empty(a.shape, F32) for a in mine],
                                                     len(names), _pair_plan)
        pairs[i] = (send, recv, mine, theirs)
        return tok[:1, :1]

    def finish_pair(i, after):
        l, names = gunits[i]
        send, recv, mine, theirs = pairs.pop(i)
        mine, theirs = _split_wait("pair_wait%d" % i, send, recv, mine, theirs, after, _pair_plan)
        for n, a, b in zip(names, mine, theirs):
            outs[n] = _pair_adamw("adamw_" + n, a, b, P[n], M[n], V[n], l, outs[n])

    def grads_of(l, g_part, after):
        collected.update({(l, n): g for n, g in g_part.items()})
        tok = jnp.zeros((1, 1), F32)
        for i, (ul, names) in enumerate(gunits):
            if ul != l or ('started', i) in collected or any((l, n) not in collected for n in names):
                continue
            collected[('started', i)] = True
            srcs = [collected[(l, n)].reshape((1,) + collected[(l, n)].shape) for n in names]
            send, recv, srcs, slots, t = _split_start("scatter_start%d" % i, srcs,
                                                      [lax.empty((3,) + P[n].shape[1:], BF16) for n in names],
                                                      3 * len(names), _scatter_plan(ax(names), sz(names)))
            scatters[i] = (send, recv, srcs, slots)
            tok = tok + t[:1, :1]
            if i >= 1:
                tok = tok + finish_scatter(i - 1, after)
            if i >= 2:
                finish_pair(i - 2, after)
        return tok

    loss_part, dx, small_g = _fwd_bwd(P, weights_of, grads_of)
    loss = lax.psum(loss_part[0, 0], ("x", "y", "c"))
    grad_x = dx.reshape(x.shape)

    small_like = [P[n] for n in SMALL]
    gsum = _allreduce_small(_pack(small_g))
    dlt, mn, vn = _adamw("adamw_small", gsum, _pack(small_like), _pack([M[n] for n in SMALL]),
                         _pack([V[n] for n in SMALL]), 512)
    grads, deltas, new_m, new_v = {}, {}, {}, {}
    for name_map, buf in ((grads, gsum), (deltas, dlt), (new_m, mn), (new_v, vn)):
        for n, a in zip(SMALL, _unpack(buf, small_like)):
            name_map[n] = a

    last = len(gunits) - 1
    finish_scatter(last, dlt)
    finish_pair(last - 1, dlt)
    finish_pair(last, dlt)
    for n in BIG:
        grads[n], deltas[n], new_m[n], new_v[n] = outs[n]

    return (loss, grad_x, *[grads[n] for n in WEIGHTS], *[deltas[n] for n in WEIGHTS],
            *[new_m[n] for n in WEIGHTS], *[new_v[n] for n in WEIGHTS])


def _fwd_bwd(P, weights_of, grads_of):
    (x, mem, positions, mem_norm_g, mix_pre_g, mix_post_g, w_in, gm_v_g, gm_w_s, gm_b_s, pool_w, pool_scale, attn_sinks,
     w_o, x_pre_g, x_post_g, w_xq, w_xkv, w_xo, ffn_pre_g, ffn_post_g, w_gate_up, w_down) = [P[n] for n in NAMES]
    x0 = x[0]
    s, d = x0.shape
    depth = w_in.shape[0]
    tgt = P['loss_target'][0]
    tmn = 256

    half = HEAD // 2
    inv = ROPE_THETA ** (-jnp.arange(half, dtype=F32) / half)
    ang = positions[0].astype(F32)[:, None] * inv
    cos, sin = jnp.cos(ang), jnp.sin(ang)
    cosq = jnp.tile(jnp.concatenate([cos, cos], axis=-1), (1, 8))
    sinq = jnp.tile(jnp.concatenate([-sin, sin], axis=-1), (1, 8))

    row = lambda a, l: a[l].reshape(1, -1)
    memn = _prenorm("mem_norm", mem[0], mem_norm_g.reshape(1, d), tmn)
    pw_bd = []
    for l in range(depth):
        bd = jnp.zeros((256, 256), F32)
        for g in range(4):
            bd = lax.dynamic_update_slice(bd, pool_w[l, g], (64 * g, 64 * g))
        pw_bd.append(bd)

    saved = []
    xc = x0
    h = _prenorm("pre_norm0", x0, row(mix_pre_g, 0), tmn)
    for l in range(depth):
        W = weights_of(l, ['w_in'], xc)
        sv = {'x0': xc, 'h1': h}
        z = _mm_nn("fwd_w_in", h, *W['w_in'], tm=1024, tn=512, tk=d, out_dtype=F32)
        abc = _mixer_fwd("mixer_fwd", z, cosq, sinq, row(gm_v_g, l), gm_w_s[l], gm_b_s[l].T, pw_bd[l],
                         row(pool_scale, l), row(attn_sinks, l))
        W.update(weights_of(l, ['w_o'], z))
        mix = _mm_nn("fwd_w_o", abc, *W['w_o'], tm=1024, tn=512, tk=d, out_dtype=F32)
        xc, h = _post_pre("post_mix", xc, mix, row(mix_post_g, l), row(x_pre_g, l), tmn)
        sv.update(z=z, abc=abc, mix=mix, x1=xc, h2=h)
        W.update(weights_of(l, ['w_xq', 'w_xkv', 'w_xo'], xc))
        q = _mm_nn("fwd_w_xq", h, *W['w_xq'], tm=1024, tn=512, tk=d, out_dtype=BF16)
        kv = _mm_nn("fwd_w_xkv", memn, *W['w_xkv'], tm=256, tn=512, tk=d, out_dtype=BF16)
        o = _xattn_fwd("xattn_fwd", q, kv, 512)
        xo = _mm_nn("fwd_w_xo", o, *W['w_xo'], tm=1024, tn=512, tk=d, out_dtype=F32)
        xc, h = _post_pre("post_xattn", xc, xo, row(x_post_g, l), row(ffn_pre_g, l), tmn)
        sv.update(q=q, kv=kv, o=o, xo=xo, x2=xc, h3=h)
        W.update(weights_of(l, ['w_gate_up', 'w_down'], xc))
        dff = W['w_down'][0].shape[1]
        gate, up, act = _ffn_up("ffn_up", h, *W['w_gate_up'], 512, dff // 2)
        f = _mm_nn("fwd_w_down", act, *W['w_down'], tm=1024, tn=512, tk=dff // 2, out_dtype=F32)
        sv.update(gate=gate, up=up, act=act, f=f)
        if l + 1 < depth:
            xc, h = _post_pre("post_ffn", xc, f, row(ffn_post_g, l), row(mix_pre_g, l + 1), tmn)
        saved.append(sv)
    dx, loss_part = _loss_head("loss_head", xc, saved[-1]['f'], row(ffn_post_g, depth - 1), tgt, tmn)

    gs = {n: [None] * depth for n in SMALL if n != 'mem_norm_g'}
    dmemn = None
    tok = jnp.zeros((1, 1), F32)
    for l in reversed(range(depth)):
        sv, W, G = saved[l], weights_of(l, BIG, dx), {}
        dfn, gs['ffn_post_g'][l] = _norm_bwd("bwd_post_ffn", sv['f'], row(ffn_post_g, l) + tok, dx, None, BF16, tmn)
        G['w_down'] = _mm_tn("dw_down", sv['act'], dfn, tm=dff // 2, tn=d, tk=512)
        dgate, dup = _ffn_act_bwd("ffn_act_bwd", dfn, *W['w_down'], sv['gate'], sv['up'], 512, dff // 2)
        gu = _mm_tn("dw_gate", sv['h3'], dgate, tm=d, tn=dff // 2, tk=512, n_total=2 * dff)
        G['w_gate_up'] = _mm_tn("dw_up", sv['h3'], dup, tm=d, tn=dff // 2, tk=512, n_total=2 * dff, n0=dff, buf=gu)
        dh = _mm_nt("bwd_w_gate", dgate, *W['w_gate_up'], tm=1024, tn=512, tk=dff // 2, out_dtype=F32)
        dh = _mm_nt("bwd_w_up", dup, *W['w_gate_up'], tm=1024, tn=512, tk=dff // 2, out_dtype=F32, k0=dff, add=dh)
        dx, gs['ffn_pre_g'][l] = _norm_bwd("bwd_pre_ffn", sv['x2'], row(ffn_pre_g, l), dh, dx, F32, tmn)
        tok = grads_of(l, {n: G[n] for n in ('w_gate_up', 'w_down')}, dx)
        dxo, gs['x_post_g'][l] = _norm_bwd("bwd_post_xattn", sv['xo'], row(x_post_g, l) + tok, dx, None, BF16, tmn)
        G['w_xo'] = _mm_tn("dw_xo", sv['o'], dxo, tm=d, tn=d, tk=512)
        do = _mm_nt("bwd_w_xo", dxo, *W['w_xo'], tm=1024, tn=512, tk=d, out_dtype=BF16)
        dq, dkv = _xattn_bwd("xattn_bwd", sv['q'], sv['kv'], do, 512)
        dkv = dkv.astype(BF16)
        G['w_xkv'] = _mm_tn("dw_xkv", memn, dkv, tm=d, tn=d, tk=mem.shape[1])
        dmemn = _mm_nt("bwd_w_xkv", dkv, *W['w_xkv'], tm=mem.shape[1], tn=512, tk=2 * d, out_dtype=F32, add=dmemn)
        G['w_xq'] = _mm_tn("dw_xq", sv['h2'], dq, tm=d, tn=d, tk=512)
        dh = _mm_nt("bwd_w_xq", dq, *W['w_xq'], tm=1024, tn=512, tk=d, out_dtype=F32)
        dx, gs['x_pre_g'][l] = _norm_bwd("bwd_pre_xattn", sv['x1'], row(x_pre_g, l), dh, dx, F32, tmn)
        tok = grads_of(l, {n: G[n] for n in ('w_xq', 'w_xkv', 'w_xo')}, dx)
        dmix, gs['mix_post_g'][l] = _norm_bwd("bwd_post_mix", sv['mix'], row(mix_post_g, l) + tok, dx, None, BF16, tmn)
        G['w_o'] = _mm_tn("dw_o", sv['abc'], dmix, tm=d, tn=d, tk=512)
        dabc = _mm_nt("bwd_w_o", dmix, *W['w_o'], tm=1024, tn=512, tk=d, out_dtype=F32)
        tok = grads_of(l, {'w_o': G['w_o']}, dabc)
        dz, dgv, dws, dbt, dpw, dpsc, dsnk = _mixer_bwd(
            "mixer_bwd", sv['z'], dabc, cosq, sinq, row(gm_v_g, l) + tok, gm_w_s[l], gm_b_s[l].T, pw_bd[l],
            row(pool_scale, l), row(attn_sinks, l))
        gs['gm_v_g'][l] = dgv
        gs['gm_w_s'][l] = dws
        gs['gm_b_s'][l] = dbt.T
        gs['pool_w'][l] = jnp.stack([dpw[64 * g:64 * (g + 1), 64 * g:64 * (g + 1)] for g in range(4)])
        gs['pool_scale'][l] = dpsc
        gs['attn_sinks'][l] = dsnk
        G['w_in'] = _mm_tn("dw_in", sv['h1'], dz, tm=d, tn=dz.shape[1], tk=512)
        dh = _mm_nt("bwd_w_in", dz, *W['w_in'], tm=1024, tn=512, tk=dz.shape[1], out_dtype=F32)
        dx, gs['mix_pre_g'][l] = _norm_bwd("bwd_pre_mix", sv['x0'], row(mix_pre_g, l), dh, dx, F32, tmn)
        tok = grads_of(l, {'w_in': G['w_in']}, dx)
    _, dg_mem = _norm_bwd("bwd_mem_norm", mem[0], mem_norm_g.reshape(1, d), dmemn, None, BF16, tmn)
    small_g = []
    for n in SMALL:
        if n == 'mem_norm_g':
            small_g.append(dg_mem.reshape(P[n].shape))
        else:
            small_g.append(jnp.stack([a.reshape(P[n].shape[1:]) for a in gs[n]]))
    return loss_part, dx, small_g
```

```python
import functools

import jax
import jax.numpy as jnp
from jax import lax
from jax.experimental import pallas as pl
from jax.experimental.pallas import tpu as pltpu

F32 = jnp.float32
BF16 = jnp.bfloat16
EPS = 1e-6
CHUNK = 128
HEAD = 64
ROPE_THETA = 10000.0
POOL_WINDOWS = (2, 4, 8, 16)
LR, B1, B2, ADAM_EPS, WD, STEP = 0.001, 0.9, 0.999, 1e-08, 0.01, 10
MESH = pl.DeviceIdType.MESH
VMEM_LIMIT = 56 * 1024 * 1024

NAMES = ['x', 'mem', 'positions', 'mem_norm_g', 'mix_pre_g', 'mix_post_g', 'w_in', 'gm_v_g', 'gm_w_s', 'gm_b_s',
         'pool_w', 'pool_scale', 'attn_sinks', 'w_o', 'x_pre_g', 'x_post_g', 'w_xq', 'w_xkv', 'w_xo', 'ffn_pre_g',
         'ffn_post_g', 'w_gate_up', 'w_down']
WEIGHTS = NAMES[3:]
BIG = ['w_in', 'w_o', 'w_xq', 'w_xkv', 'w_xo', 'w_gate_up', 'w_down']
BIG_AXIS = {'w_in': 2, 'w_o': 1, 'w_xq': 1, 'w_xkv': 2, 'w_xo': 1, 'w_gate_up': 2, 'w_down': 1}
SMALL = [n for n in WEIGHTS if n not in BIG]

NN = (((1,), (0,)), ((), ()))
NT = (((1,), (1,)), ((), ()))
TN = (((0,), (0,)), ((), ()))


def _dot(a, b, dims=NN):
    return lax.dot_general(a, b, dims, preferred_element_type=F32)


def _params(sem):
    return pltpu.CompilerParams(dimension_semantics=sem, vmem_limit_bytes=VMEM_LIMIT)


def _rows_tile(rows, limit=256):
    return max(t for t in range(16, limit + 1, 16) if rows % t == 0)


def _mm(name, a, a_spec, b, b_spec, dims, grid, nk, out_shape, out_spec, add=None, add_spec=None, dep=None):
    acc_shape = out_spec.block_shape

    def body(*refs):
        a_ref, b_ref = refs[0], refs[1]
        pos = 2
        add_ref = None
        if add is not None:
            add_ref = refs[pos]
            pos += 1
        if dep is not None:
            pos += 1
        o_ref = refs[pos]
        part = _dot(a_ref[...].astype(BF16), b_ref[...].astype(BF16), dims)
        if nk == 1:
            if add_ref is not None:
                part = part + add_ref[...]
            o_ref[...] = part.astype(o_ref.dtype)
        else:
            acc_ref = refs[pos + 1]
            k = pl.program_id(2)

            @pl.when(k == 0)
            def _():
                acc_ref[...] = part if add_ref is None else part + add_ref[...]

            @pl.when(k > 0)
            def _():
                acc_ref[...] += part

            @pl.when(k == nk - 1)
            def _():
                o_ref[...] = acc_ref[...].astype(o_ref.dtype)

    ops, specs = [a, b], [a_spec, b_spec]
    if add is not None:
        ops.append(add)
        specs.append(add_spec)
    if dep is not None:
        ops.append(dep)
        specs.append(pl.BlockSpec((1, 1), lambda i, j, k: (0, 0)))
    return pl.pallas_call(
        body, name=name, grid=grid, in_specs=specs, out_specs=out_spec, out_shape=out_shape,
        scratch_shapes=[pltpu.VMEM(acc_shape, F32)] if nk > 1 else [],
        compiler_params=_params(("parallel", "parallel", "arbitrary")),
    )(*ops)


def _wspec(block, layer, fn):
    return pl.BlockSpec((None,) + block, lambda i, j, k: (layer,) + fn(i, j, k))


def _mm_nn(name, a, w, layer, *, tm, tn, tk, out_dtype, n0=0, n=None, k0=0):
    m, kk = a.shape
    n = w.shape[2] if n is None else n
    tm = min(tm, m)
    nk = kk // tk
    return _mm(name, a, pl.BlockSpec((tm, tk), lambda i, j, k: (i, k)),
               w, _wspec((tk, tn), layer, lambda i, j, k: (k + k0 // tk, j + n0 // tn)), NN,
               (m // tm, n // tn, nk), nk, jax.ShapeDtypeStruct((m, n), out_dtype),
               pl.BlockSpec((tm, tn), lambda i, j, k: (i, j)))


def _mm_nt(name, a, w, layer, *, tm, tn, tk, out_dtype, k0=0, add=None):
    m, kk = a.shape
    n = w.shape[1]
    tm = min(tm, m)
    nk = kk // tk
    ospec = pl.BlockSpec((tm, tn), lambda i, j, k: (i, j))
    return _mm(name, a, pl.BlockSpec((tm, tk), lambda i, j, k: (i, k)),
               w, _wspec((tn, tk), layer, lambda i, j, k: (j, k + k0 // tk)), NT,
               (m // tm, n // tn, nk), nk, jax.ShapeDtypeStruct((m, n), out_dtype), ospec,
               add=add, add_spec=ospec if add is not None else None)


def _mm_tn(name, a, b, *, tm, tn, tk, dep=None):
    kk, m = a.shape
    n = b.shape[1]
    tk = min(tk, kk)
    nk = kk // tk
    return _mm(name, a, pl.BlockSpec((tk, tm), lambda i, j, k: (k, i)),
               b, pl.BlockSpec((tk, tn), lambda i, j, k: (k, j)), TN,
               (m // tm, n // tn, nk), nk, jax.ShapeDtypeStruct((m, n), BF16),
               pl.BlockSpec((tm, tn), lambda i, j, k: (i, j)), dep=dep)


def _mm_rows(name, a, w, layer, mode, *, tm, tk, k0=0, rows_in=(), params=(), rows_out=(), n_sums=0, epilogue):
    m, kk = a.shape
    n = w.shape[2] if mode == 'nn' else w.shape[1]
    nk = kk // tk
    nr, npar, no = len(rows_in), len(params), len(rows_out)

    def body(*refs):
        a_ref, w_ref = refs[0], refs[1]
        rin = refs[2:2 + nr]
        par = refs[2 + nr:2 + nr + npar]
        outs = refs[2 + nr + npar:2 + nr + npar + no]
        sums = refs[2 + nr + npar + no:2 + nr + npar + no + n_sums]
        i, k = pl.program_id(0), pl.program_id(1)
        part = _dot(a_ref[...], w_ref[...], NN if mode == 'nn' else NT)

        def finish(acc):
            res, sm = epilogue(acc, [r[...] for r in rin], [p[...] for p in par])
            for r, v in zip(outs, res):
                r[...] = v.astype(r.dtype)

            @pl.when(i == 0)
            def _():
                for r in sums:
                    r[...] = jnp.zeros_like(r)

            for r, v in zip(sums, sm):
                r[...] += v

        if nk == 1:
            finish(part)
        else:
            acc_ref = refs[-1]

            @pl.when(k == 0)
            def _():
                acc_ref[...] = part

            @pl.when(k > 0)
            def _():
                acc_ref[...] += part

            @pl.when(k == nk - 1)
            def _():
                finish(acc_ref[...])

    if mode == 'nn':
        wspec = pl.BlockSpec((None, tk, n), lambda i, k: (layer, k + k0 // tk, 0))
    else:
        wspec = pl.BlockSpec((None, n, tk), lambda i, k: (layer, 0, k + k0 // tk))
    rowblk = pl.BlockSpec((tm, n), lambda i, k: (i, 0))
    one = pl.BlockSpec((1, n), lambda i, k: (0, 0))
    return pl.pallas_call(
        body, name=name, grid=(m // tm, nk),
        in_specs=[pl.BlockSpec((tm, tk), lambda i, k: (i, k)), wspec] + [rowblk] * nr + [one] * npar,
        out_specs=[rowblk] * no + [one] * n_sums,
        out_shape=[jax.ShapeDtypeStruct((m, n), dt) for dt in rows_out] +
                  [jax.ShapeDtypeStruct((1, n), F32)] * n_sums,
        scratch_shapes=[pltpu.VMEM((tm, n), F32)] if nk > 1 else [],
        compiler_params=_params(("arbitrary", "arbitrary")),
    )(a, w, *rows_in, *params)


def _rstd(x):
    return lax.rsqrt(jnp.mean(x * x, axis=-1, keepdims=True) + EPS)


def _norm_back(xin, g, dy):
    r = _rstd(xin)
    xh = xin * r
    dyg = dy * g
    return r * (dyg - xh * jnp.mean(dyg * xh, axis=-1, keepdims=True)), jnp.sum(dy * xh, axis=0, keepdims=True)


def _post_pre_rows(y, rows, pars):
    xn = rows[0] + y * _rstd(y) * pars[0]
    return [y, xn, xn * _rstd(xn) * pars[1]], []


def _make_loss_rows(d):
    def fn(y, rows, pars):
        x, tgt = rows
        err = x + y * _rstd(y) * pars[0] - tgt
        dout = err * (1.0 / d)
        dy, dg = _norm_back(y, pars[0], dout)
        lsum = 0.5 * jnp.sum(jnp.mean(err * err, axis=-1, keepdims=True), axis=0, keepdims=True)
        return [dout, dy], [dg, jnp.broadcast_to(lsum, dg.shape)]
    return fn


def _bwd_rows(dh, rows, pars):
    xin, resid, yprev = rows
    dxa, dg_pre = _norm_back(xin, pars[0], dh)
    dx = resid + dxa
    dyp, dg_post = _norm_back(yprev.astype(F32), pars[1], dx)
    return [dx, dyp], [dg_pre, dg_post]


def _bwd_rows_first(dh, rows, pars):
    xin, resid = rows
    dxa, dg_pre = _norm_back(xin, pars[0], dh)
    return [resid + dxa], [dg_pre]


def _row(d):
    return pl.BlockSpec((1, d), lambda i: (0, 0))


def _prenorm(name, x, g, tm):
    m, d = x.shape

    def body(x_ref, g_ref, o_ref):
        xv = x_ref[...]
        o_ref[...] = (xv * _rstd(xv) * g_ref[...]).astype(BF16)

    blk = pl.BlockSpec((tm, d), lambda i: (i, 0))
    return pl.pallas_call(body, name=name, grid=(m // tm,), in_specs=[blk, _row(d)], out_specs=blk,
                          out_shape=jax.ShapeDtypeStruct((m, d), BF16), compiler_params=_params(("parallel",)))(x, g)


def _norm_bwd(name, xin, g, dy, resid, out_dtype, tm):
    m, d = xin.shape

    def body(*refs):
        if resid is None:
            x_ref, g_ref, dy_ref, dx_ref, dg_ref = refs
        else:
            x_ref, g_ref, dy_ref, r_ref, dx_ref, dg_ref = refs
        xv = x_ref[...]
        r = _rstd(xv)
        xh = xv * r
        dyv = dy_ref[...].astype(F32)
        dyg = dyv * g_ref[...]
        dx = r * (dyg - xh * jnp.mean(dyg * xh, axis=-1, keepdims=True))
        if resid is not None:
            dx = dx + r_ref[...]
        dx_ref[...] = dx.astype(dx_ref.dtype)

        @pl.when(pl.program_id(0) == 0)
        def _():
            dg_ref[...] = jnp.zeros_like(dg_ref)

        dg_ref[...] += jnp.sum(dyv * xh, axis=0, keepdims=True)

    blk = pl.BlockSpec((tm, d), lambda i: (i, 0))
    ops = [xin, g, dy] + ([] if resid is None else [resid])
    specs = [blk, _row(d), blk] + ([] if resid is None else [blk])
    return pl.pallas_call(
        body, name=name, grid=(m // tm,), in_specs=specs, out_specs=[blk, _row(d)],
        out_shape=[jax.ShapeDtypeStruct((m, d), out_dtype), jax.ShapeDtypeStruct((1, d), F32)],
        compiler_params=_params(("arbitrary",)))(*ops)


def _adamw(name, g, w, m, v, tr):
    rows, cols = g.shape
    c1 = 1.0 - B1 ** STEP
    c2 = 1.0 - B2 ** STEP

    def body(g_ref, w_ref, m_ref, v_ref, d_ref, mo_ref, vo_ref):
        gv = g_ref[...]
        mn = B1 * m_ref[...] + (1.0 - B1) * gv
        vn = B2 * v_ref[...] + (1.0 - B2) * (gv * gv)
        mo_ref[...] = mn
        vo_ref[...] = vn
        d_ref[...] = -LR * ((mn / c1) / (jnp.sqrt(vn / c2) + ADAM_EPS) + WD * w_ref[...])

    blk = pl.BlockSpec((tr, cols), lambda i: (i, 0))
    sd = jax.ShapeDtypeStruct((rows, cols), F32)
    return pl.pallas_call(body, name=name, grid=(rows // tr,), in_specs=[blk] * 4, out_specs=[blk] * 3,
                          out_shape=[sd, sd, sd], compiler_params=_params(("parallel",)))(g, w, m, v)


def _gelu_parts(x):
    c = 0.7978845608028654
    t = jnp.tanh(c * (x + 0.044715 * (x * x * x)))
    return 0.5 * x * (1.0 + t), t


def _gelu_grad(x, t):
    c = 0.7978845608028654
    return 0.5 * (1.0 + t) + 0.5 * x * (1.0 - t * t) * (c * (1.0 + 3.0 * 0.044715 * x * x))


def _rot_half(x):
    ax = x.ndim - 1
    w = x.shape[ax]
    lane = lax.broadcasted_iota(jnp.int32, x.shape, ax)
    return jnp.where((lane & 63) < 32, pltpu.roll(x, w - 32, ax), pltpu.roll(x, 32, ax))


def _gm_group(gel, g, gv, ws_ref, bt):
    u = gel[:, HEAD * g:HEAD * (g + 1)]
    vg = gel[:, 256 + HEAD * g:256 + HEAD * (g + 1)]
    r = _rstd(vg)
    xh = vg * r
    vn = (xh * gv[:, HEAD * g:HEAD * (g + 1)]).astype(BF16)
    row = lax.broadcasted_iota(jnp.int32, (CHUNK, CHUNK), 0)
    col = lax.broadcasted_iota(jnp.int32, (CHUNK, CHUNK), 1)
    causal = col <= row
    wc = jnp.where(causal, ws_ref[g], 0.0).astype(BF16)
    mixed = _dot(wc, vn) + bt[:, g:g + 1]
    return u, r, xh, vn, wc, causal, mixed


def _lane_select(lane, vals):
    return jnp.where(lane < 64, vals[0], jnp.where(lane < 128, vals[1], jnp.where(lane < 192, vals[2], vals[3])))


def _pool_fwd(pc, pp, ci):
    ext = jnp.concatenate([pp, pc], axis=0)
    s2 = ext + pltpu.roll(ext, 1, 0)
    s4 = s2 + pltpu.roll(s2, 2, 0)
    s8 = s4 + pltpu.roll(s4, 4, 0)
    s16 = s8 + pltpu.roll(s8, 8, 0)
    t1 = ci * CHUNK + lax.broadcasted_iota(jnp.int32, (CHUNK, 1), 0) + 1
    lane = lax.broadcasted_iota(jnp.int32, (1, 256), 1)
    cnt = _lane_select(lane, [jnp.minimum(t1, w).astype(F32) for w in POOL_WINDOWS])
    ssel = _lane_select(lane, [s[CHUNK:] for s in (s2, s4, s8, s16)])
    return ssel / cnt - pc, cnt, lane


def _attn_prep(zc, zpkv, cq, sq, cp, sp, ci):
    q = zc[:, 768:1280]
    kc = zc[:, 1280:1408]
    vc = zc[:, 1408:1536]
    kp = zpkv[:, :128]
    vp = zpkv[:, 128:]
    qr = q * cq + _rot_half(q) * sq
    krc = kc * cq[:, :128] + _rot_half(kc) * sq[:, :128]
    krp = kp * cp + _rot_half(kp) * sp
    kband = jnp.concatenate([krp, krc], axis=0).astype(BF16)
    vband = jnp.concatenate([vp, vc], axis=0).astype(BF16)
    row = lax.broadcasted_iota(jnp.int32, (CHUNK, 2 * CHUNK), 0)
    col = lax.broadcasted_iota(jnp.int32, (CHUNK, 2 * CHUNK), 1)
    valid = ((col < CHUNK) & (col > row) & (ci > 0)) | ((col >= CHUNK) & (col - CHUNK <= row))
    return qr, kband, vband, valid


def _head_probs(qh, kh, valid, sink):
    s = _dot(qh, kh, NT) * (HEAD ** -0.5)
    s = jnp.where(valid, s, -1e30)
    mx = jnp.maximum(jnp.max(s, axis=-1, keepdims=True), sink)
    e = jnp.exp(s - mx)
    es = jnp.exp(sink - mx)
    den = jnp.sum(e, axis=-1, keepdims=True) + es
    return e / den, es / den


def _mixer_specs(nb, rev):
    def cur(i):
        return nb - 1 - i if rev else i

    def prev(i):
        return jnp.maximum(cur(i) - 1, 0)

    full = lambda shape: pl.BlockSpec(shape, lambda i: (0,) * len(shape))
    specs = [
        pl.BlockSpec((CHUNK, 1536), lambda i: (cur(i), 0)),
        pl.BlockSpec((CHUNK, 256), lambda i: (prev(i), 2)),
        pl.BlockSpec((CHUNK, 256), lambda i: (prev(i), 5)),
        pl.BlockSpec((CHUNK, 512), lambda i: (cur(i), 0)),
        pl.BlockSpec((CHUNK, 512), lambda i: (cur(i), 0)),
        pl.BlockSpec((CHUNK, 128), lambda i: (prev(i), 0)),
        pl.BlockSpec((CHUNK, 128), lambda i: (prev(i), 0)),
        full((1, 256)), full((4, CHUNK, CHUNK)), full((CHUNK, 4)), full((256, 256)), full((1, 256)), full((1, 8)),
    ]
    return specs, cur


def _mixer_fwd(name, z, cosq, sinq, gv, ws, bt, pw, psc, snk):
    s = z.shape[0]
    nb = s // CHUNK
    specs, _ = _mixer_specs(nb, False)

    def body(zc_ref, zpp_ref, zpkv_ref, cq_ref, sq_ref, cp_ref, sp_ref, gv_ref, ws_ref, bt_ref, pw_ref, psc_ref,
             snk_ref, o_ref):
        ci = pl.program_id(0)
        zc = zc_ref[...]
        gel, _ = _gelu_parts(zc[:, :512])
        gvv = gv_ref[...]
        btv = bt_ref[...]
        for g in range(4):
            u, _, _, _, _, _, mixed = _gm_group(gel, g, gvv, ws_ref, btv)
            o_ref[:, HEAD * g:HEAD * (g + 1)] = (u * mixed).astype(BF16)
        pp = jnp.where(ci > 0, zpp_ref[...], 0.0)
        pooled, _, _ = _pool_fwd(zc[:, 512:768], pp, ci)
        mp = _dot(pooled.astype(BF16), pw_ref[...].astype(BF16))
        o_ref[:, 256:512] = (mp * psc_ref[...]).astype(BF16)
        qr, kband, vband, valid = _attn_prep(zc, zpkv_ref[...], cq_ref[...], sq_ref[...], cp_ref[...], sp_ref[...], ci)
        snkv = snk_ref[...]
        for h in range(8):
            hk = h // 4
            p, _ = _head_probs(qr[:, HEAD * h:HEAD * (h + 1)].astype(BF16), kband[:, HEAD * hk:HEAD * (hk + 1)],
                               valid, snkv[:, h:h + 1])
            o = _dot(p.astype(BF16), vband[:, HEAD * hk:HEAD * (hk + 1)])
            o_ref[:, 512 + HEAD * h:512 + HEAD * (h + 1)] = o.astype(BF16)

    return pl.pallas_call(
        body, name=name, grid=(nb,), in_specs=specs, out_specs=pl.BlockSpec((CHUNK, 1024), lambda i: (i, 0)),
        out_shape=jax.ShapeDtypeStruct((s, 1024), BF16), compiler_params=_params(("parallel",)),
    )(z, z, z, cosq, sinq, cosq, sinq, gv, ws, bt, pw, psc, snk)


def _mixer_bwd(name, z, dabc, cosq, sinq, gv, ws, bt, pw, psc, snk):
    s = z.shape[0]
    nb = s // CHUNK
    specs, cur = _mixer_specs(nb, True)
    specs = specs + [pl.BlockSpec((CHUNK, 1024), lambda i: (cur(i), 0))]
    full = lambda shape: pl.BlockSpec(shape, lambda i: (0,) * len(shape))
    acc_shapes = [(1, 256), (4, CHUNK, CHUNK), (CHUNK, 4), (256, 256), (1, 256), (1, 8)]

    def body(zc_ref, zpp_ref, zpkv_ref, cq_ref, sq_ref, cp_ref, sp_ref, gv_ref, ws_ref, bt_ref, pw_ref, psc_ref,
             snk_ref, dabc_ref, dz_ref, dgv_ref, dws_ref, dbt_ref, dpw_ref, dpsc_ref, dsnk_ref,
             cpool, ck, cv, dq_s, dkv_s):
        step = pl.program_id(0)
        ci = nb - 1 - step

        @pl.when(step == 0)
        def _():
            for r in (dgv_ref, dws_ref, dbt_ref, dpw_ref, dpsc_ref, dsnk_ref, cpool, ck, cv):
                r[...] = jnp.zeros_like(r)

        zc = zc_ref[...]
        dabc = dabc_ref[...]
        zg = zc[:, :512]
        gel, th = _gelu_parts(zg)
        gp = _gelu_grad(zg, th)
        gvv = gv_ref[...]
        btv = bt_ref[...]
        lane4 = lax.broadcasted_iota(jnp.int32, (CHUNK, 4), 1)
        dbt = jnp.zeros((CHUNK, 4), F32)
        for g in range(4):
            lo, hi = HEAD * g, HEAD * (g + 1)
            u, r, xh, vn, wc, causal, mixed = _gm_group(gel, g, gvv, ws_ref, btv)
            da = dabc[:, lo:hi]
            dm = da * u
            dmb = dm.astype(BF16)
            dws_ref[g] += jnp.where(causal, _dot(dmb, vn, NT), 0.0)
            dbt = dbt + jnp.where(lane4 == g, jnp.sum(dm, axis=-1, keepdims=True), 0.0)
            dvn = _dot(wc, dmb, TN)
            dgv_ref[:, lo:hi] += jnp.sum(dvn * xh, axis=0, keepdims=True)
            dxh = dvn * gvv[:, lo:hi]
            dvg = r * (dxh - xh * jnp.mean(dxh * xh, axis=-1, keepdims=True))
            dz_ref[:, lo:hi] = (da * mixed * gp[:, lo:hi]).astype(BF16)
            dz_ref[:, 256 + lo:256 + hi] = (dvg * gp[:, 256 + lo:256 + hi]).astype(BF16)
        dbt_ref[...] += dbt
        pc = zc[:, 512:768]
        pp = jnp.where(ci > 0, zpp_ref[...], 0.0)
        pooled, cnt, lane = _pool_fwd(pc, pp, ci)
        pwb = pw_ref[...].astype(BF16)
        pooled_b = pooled.astype(BF16)
        mp = _dot(pooled_b, pwb)
        db = dabc[:, 256:512]
        dpsc_ref[...] += jnp.sum(db * mp, axis=0, keepdims=True)
        dmpb = (db * psc_ref[...]).astype(BF16)
        dpw_ref[...] += _dot(pooled_b, dmpb, TN)
        dpooled = _dot(dmpb, pwb, NT)
        davg = dpooled / cnt
        zero = jnp.zeros((CHUNK, 256), F32)
        d2, d4, d8, d16 = [jnp.concatenate([zero, jnp.where((lane >= 64 * k) & (lane < 64 * (k + 1)), davg, 0.0)],
                                           axis=0) for k in range(4)]
        g8 = d8 + d16 + pltpu.roll(d16, 2 * CHUNK - 8, 0)
        g4 = d4 + g8 + pltpu.roll(g8, 2 * CHUNK - 4, 0)
        g2 = d2 + g4 + pltpu.roll(g4, 2 * CHUNK - 2, 0)
        ge = g2 + pltpu.roll(g2, 2 * CHUNK - 1, 0)
        dz_ref[:, 512:768] = (ge[CHUNK:] - dpooled + cpool[...]).astype(BF16)
        cpool[...] = ge[:CHUNK]
        cq = cq_ref[...]
        sq = sq_ref[...]
        qr, kband, vband, valid = _attn_prep(zc, zpkv_ref[...], cq, sq, cp_ref[...], sp_ref[...], ci)
        snkv = snk_ref[...]
        lane8 = lax.broadcasted_iota(jnp.int32, (1, 8), 1)
        dsnk = jnp.zeros((1, 8), F32)
        for hk in range(2):
            kh = kband[:, HEAD * hk:HEAD * (hk + 1)]
            vh = vband[:, HEAD * hk:HEAD * (hk + 1)]
            dkh = jnp.zeros((2 * CHUNK, HEAD), F32)
            dvh = jnp.zeros((2 * CHUNK, HEAD), F32)
            for h in range(4 * hk, 4 * hk + 4):
                qh = qr[:, HEAD * h:HEAD * (h + 1)].astype(BF16)
                p, ps = _head_probs(qh, kh, valid, snkv[:, h:h + 1])
                dob = dabc[:, 512 + HEAD * h:512 + HEAD * (h + 1)].astype(BF16)
                dp = _dot(dob, vh, NT)
                dd = jnp.sum(p * dp, axis=-1, keepdims=True)
                dsnk = dsnk + jnp.where(lane8 == h, jnp.sum(-ps * dd, axis=0, keepdims=True), 0.0)
                dsb = (p * (dp - dd) * (HEAD ** -0.5)).astype(BF16)
                dq_s[:, HEAD * h:HEAD * (h + 1)] = _dot(dsb, kh)
                dkh = dkh + _dot(dsb, qh, TN)
                dvh = dvh + _dot(p.astype(BF16), dob, TN)
            dkv_s[:, HEAD * hk:HEAD * (hk + 1)] = dkh
            dkv_s[:, 128 + HEAD * hk:128 + HEAD * (hk + 1)] = dvh
        dsnk_ref[...] += dsnk
        dqr = dq_s[...]
        dz_ref[:, 768:1280] = (dqr * cq + _rot_half(dqr * sq)).astype(BF16)
        dkv = dkv_s[...]
        dkr = dkv[CHUNK:, :128] + ck[...]
        dz_ref[:, 1280:1408] = (dkr * cq[:, :128] + _rot_half(dkr * sq[:, :128])).astype(BF16)
        dz_ref[:, 1408:1536] = (dkv[CHUNK:, 128:] + cv[...]).astype(BF16)
        ck[...] = dkv[:CHUNK, :128]
        cv[...] = dkv[:CHUNK, 128:]

    return pl.pallas_call(
        body, name=name, grid=(nb,), in_specs=specs,
        out_specs=[pl.BlockSpec((CHUNK, 1536), lambda i: (cur(i), 0))] + [full(a) for a in acc_shapes],
        out_shape=[jax.ShapeDtypeStruct((s, 1536), BF16)] + [jax.ShapeDtypeStruct(a, F32) for a in acc_shapes],
        scratch_shapes=[pltpu.VMEM((CHUNK, 256), F32), pltpu.VMEM((CHUNK, 128), F32), pltpu.VMEM((CHUNK, 128), F32),
                        pltpu.VMEM((CHUNK, 512), F32), pltpu.VMEM((2 * CHUNK, 256), F32)],
        compiler_params=_params(("arbitrary",)),
    )(z, z, z, cosq, sinq, cosq, sinq, gv, ws, bt, pw, psc, snk, dabc)


def _xattn_probs(qh, kh):
    s = _dot(qh, kh, NT) * (256 ** -0.5)
    e = jnp.exp(s - jnp.max(s, axis=-1, keepdims=True))
    return e / jnp.sum(e, axis=-1, keepdims=True)


def _xattn_fwd(name, q, kv, tq):
    s, d = q.shape
    mlen = kv.shape[0]

    def body(q_ref, kv_ref, o_ref):
        for h in range(4):
            lo, hi = 256 * h, 256 * (h + 1)
            p = _xattn_probs(q_ref[:, lo:hi], kv_ref[:, lo:hi])
            o_ref[:, lo:hi] = _dot(p.astype(BF16), kv_ref[:, d + lo:d + hi]).astype(BF16)

    blk = pl.BlockSpec((tq, d), lambda i: (i, 0))
    return pl.pallas_call(body, name=name, grid=(s // tq,),
                          in_specs=[blk, pl.BlockSpec((mlen, 2 * d), lambda i: (0, 0))], out_specs=blk,
                          out_shape=jax.ShapeDtypeStruct((s, d), BF16), compiler_params=_params(("parallel",)))(q, kv)


def _xattn_bwd(name, q, kv, do, tq):
    s, d = q.shape
    mlen = kv.shape[0]

    def body(q_ref, kv_ref, do_ref, dq_ref, dkv_ref):
        @pl.when(pl.program_id(0) == 0)
        def _():
            dkv_ref[...] = jnp.zeros_like(dkv_ref)

        for h in range(4):
            lo, hi = 256 * h, 256 * (h + 1)
            qh = q_ref[:, lo:hi]
            kh = kv_ref[:, lo:hi]
            vh = kv_ref[:, d + lo:d + hi]
            doh = do_ref[:, lo:hi]
            p = _xattn_probs(qh, kh)
            dp = _dot(doh, vh, NT)
            dsb = (p * (dp - jnp.sum(p * dp, axis=-1, keepdims=True)) * (256 ** -0.5)).astype(BF16)
            dq_ref[:, lo:hi] = _dot(dsb, kh).astype(BF16)
            dkv_ref[:, lo:hi] += _dot(dsb, qh, TN)
            dkv_ref[:, d + lo:d + hi] += _dot(p.astype(BF16), doh, TN)

    blk = pl.BlockSpec((tq, d), lambda i: (i, 0))
    kvb = pl.BlockSpec((mlen, 2 * d), lambda i: (0, 0))
    return pl.pallas_call(
        body, name=name, grid=(s // tq,), in_specs=[blk, kvb, blk], out_specs=[blk, kvb],
        out_shape=[jax.ShapeDtypeStruct((s, d), BF16), jax.ShapeDtypeStruct((mlen, 2 * d), F32)],
        compiler_params=_params(("arbitrary",)))(q, kv, do)


def _ffn_up(name, h, wgu, layer, tm, tn):
    s, d = h.shape
    dff = wgu.shape[2] // 2
    nj = dff // tn

    def body(h_ref, wg_ref, wu_ref, g_ref, u_ref, a_ref):
        hv = h_ref[...]
        gate = _dot(hv, wg_ref[...])
        up = _dot(hv, wu_ref[...])
        g_ref[...] = gate.astype(BF16)
        u_ref[...] = up.astype(BF16)
        a_ref[...] = (gate / (1.0 + jnp.exp(-gate)) * up).astype(BF16)

    ob = pl.BlockSpec((tm, tn), lambda j, i: (i, j))
    sd = jax.ShapeDtypeStruct((s, dff), BF16)
    return pl.pallas_call(
        body, name=name, grid=(nj, s // tm),
        in_specs=[pl.BlockSpec((tm, d), lambda j, i: (i, 0)),
                  pl.BlockSpec((None, d, tn), lambda j, i: (layer, 0, j)),
                  pl.BlockSpec((None, d, tn), lambda j, i: (layer, 0, j + nj))],
        out_specs=[ob, ob, ob], out_shape=[sd, sd, sd], compiler_params=_params(("parallel", "parallel")),
    )(h, wgu, wgu)


def _ffn_act_bwd(name, dfn, wdown, layer, gate, up, tm):
    s, d = dfn.shape
    dff = gate.shape[1]

    def body(df_ref, wd_ref, g_ref, u_ref, o_ref):
        dact = _dot(df_ref[...], wd_ref[...], NT)
        gate = g_ref[...].astype(F32)
        sig = 1.0 / (1.0 + jnp.exp(-gate))
        o_ref[:, :dff] = (dact * u_ref[...].astype(F32) * sig * (1.0 + gate * (1.0 - sig))).astype(BF16)
        o_ref[:, dff:] = (dact * gate * sig).astype(BF16)

    gb = pl.BlockSpec((tm, dff), lambda i: (i, 0))
    return pl.pallas_call(
        body, name=name, grid=(s // tm,),
        in_specs=[pl.BlockSpec((tm, d), lambda i: (i, 0)),
                  pl.BlockSpec((None, dff, d), lambda i: (layer, 0, 0)), gb, gb],
        out_specs=pl.BlockSpec((tm, 2 * dff), lambda i: (i, 0)),
        out_shape=jax.ShapeDtypeStruct((s, 2 * dff), BF16), compiler_params=_params(("parallel",)),
    )(dfn, wdown, gate, up)


def _place():
    return lax.axis_index("x"), lax.axis_index("y"), lax.axis_index("c")


def _other_chips(x, y):
    return [(1 - x, y), (x, 1 - y), (1 - x, 1 - y)]


def _region(ref, axis, chip, size):
    start = pl.multiple_of(chip * size, size)
    if axis == 1:
        return ref.at[:, pl.ds(start, size), :]
    return ref.at[:, :, pl.ds(start, size)]


ANY = pl.BlockSpec(memory_space=pl.ANY)


HBM = pl.BlockSpec(memory_space=pltpu.HBM)
SEM = pl.BlockSpec(memory_space=pltpu.SEMAPHORE)
EFFECT = pltpu.SideEffectType.DATAFLOW_SIDE_EFFECTING


def _in_hbm(a):
    return pltpu.with_memory_space_constraint(a, pltpu.HBM)


def _split_start(name, srcs, lands, ncopies, plan):
    ns, nl = len(srcs), len(lands)

    def body(*refs):
        src, land = refs[:ns], refs[ns:ns + nl]
        send, recv = refs[ns + nl], refs[ns + nl + 1]
        token = refs[-1]
        x, y, c = _place()
        for k, (s_ref, d_ref, peer, _) in enumerate(plan(src, land, x, y, c)):
            pltpu.make_async_remote_copy(src_ref=s_ref, dst_ref=d_ref, send_sem=send.at[k], recv_sem=recv.at[k],
                                         device_id=peer, device_id_type=MESH).start()
        token[...] = jnp.zeros_like(token)

    ops = list(srcs) + list(lands)
    out = pl.pallas_call(
        body, name=name,
        out_shape=(pltpu.SemaphoreType.DMA((ncopies,)), pltpu.SemaphoreType.DMA((ncopies,)),
                   *[pltpu.HBM(a.shape, a.dtype) for a in ops], jax.ShapeDtypeStruct((8, 128), F32)),
        in_specs=(HBM,) * (ns + nl), out_specs=(SEM, SEM) + (HBM,) * (ns + nl) + (pl.BlockSpec(memory_space=pltpu.VMEM),),
        input_output_aliases={i: 2 + i for i in range(ns + nl)},
        compiler_params=pltpu.CompilerParams(has_side_effects=EFFECT),
    )(*[_in_hbm(a) for a in ops])
    return out[0], out[1], list(out[2:2 + ns]), list(out[2 + ns:2 + ns + nl]), out[-1]


def _split_wait(name, send, recv, srcs, lands, after, plan):
    ns, nl = len(srcs), len(lands)

    def body(*refs):
        src, land = refs[:ns], refs[ns:ns + nl]
        send_ref, recv_ref = refs[ns + nl], refs[ns + nl + 1]
        x, y, c = _place()
        for k, (s_ref, _, _, got) in enumerate(plan(src, land, x, y, c)):
            cp = pltpu.make_async_remote_copy(src_ref=s_ref, dst_ref=got, send_sem=send_ref.at[k],
                                              recv_sem=recv_ref.at[k], device_id=(x, y, c), device_id_type=MESH)
            cp.wait_send()
            cp.wait_recv()

    ops = list(srcs) + list(lands)
    out = pl.pallas_call(
        body, name=name, out_shape=tuple(pltpu.HBM(a.shape, a.dtype) for a in ops),
        in_specs=(HBM,) * (ns + nl) + (SEM, SEM, ANY), out_specs=(HBM,) * (ns + nl),
        input_output_aliases={i: i for i in range(ns + nl)},
        compiler_params=pltpu.CompilerParams(has_side_effects=EFFECT),
    )(*ops, send, recv, after)
    return list(out[:ns]), list(out[ns:])


def _gather_plan(axes, sizes, layer):
    def plan(src, land, x, y, c):
        me = 2 * x + y
        out = []
        for t in range(len(land)):
            mine = _region(land[t], axes[t], me, sizes[t]).at[pl.ds(layer, 1)]
            for px, py in _other_chips(x, y):
                out.append((mine, mine, (px, py, c),
                            _region(land[t], axes[t], 2 * px + py, sizes[t]).at[pl.ds(layer, 1)]))
        return out
    return plan


def _place_own(name, w, axis, chip):
    nl, r, cs = w.shape
    tr = _rows_tile(r)
    nb = r // tr
    full = (nl, 4 * r, cs) if axis == 1 else (nl, r, 4 * cs)

    def body(m_ref, w_ref, o_ref):
        o_ref[...] = w_ref[...].astype(BF16)

    if axis == 1:
        ospec = pl.BlockSpec((None, tr, cs), lambda l, i, m: (l, m[0] * nb + i, 0))
    else:
        ospec = pl.BlockSpec((None, tr, cs), lambda l, i, m: (l, i, m[0]))
    return pl.pallas_call(
        body, name=name,
        grid_spec=pltpu.PrefetchScalarGridSpec(
            num_scalar_prefetch=1, grid=(nl, nb),
            in_specs=[pl.BlockSpec((None, tr, cs), lambda l, i, m: (l, i, 0))], out_specs=ospec),
        out_shape=jax.ShapeDtypeStruct(full, BF16), compiler_params=_params(("parallel", "parallel")),
    )(chip, w)


def _scatter_plan(axes, sizes):
    def plan(src, land, x, y, c):
        out = []
        for t in range(len(src)):
            for k, (px, py) in enumerate(_other_chips(x, y)):
                out.append((_region(src[t], axes[t], 2 * px + py, sizes[t]).at[0], land[t].at[k], (px, py, c),
                            land[t].at[k]))
        return out
    return plan


def _pair_plan(src, land, x, y, c):
    return [(src[t], land[t], (x, y, 1 - c), land[t]) for t in range(len(src))]


def _chip_sum(name, g, slots, axis, chip):
    _, r, cs = slots.shape
    tr = _rows_tile(r)
    nb = r // tr

    def body(m_ref, g_ref, s_ref, o_ref):
        acc = g_ref[...].astype(F32)
        for k in range(3):
            acc = acc + s_ref[k].astype(F32)
        o_ref[...] = acc

    if axis == 1:
        gspec = pl.BlockSpec((tr, cs), lambda i, m: (m[0] * nb + i, 0))
    else:
        gspec = pl.BlockSpec((tr, cs), lambda i, m: (i, m[0]))
    return pl.pallas_call(
        body, name=name,
        grid_spec=pltpu.PrefetchScalarGridSpec(
            num_scalar_prefetch=1, grid=(nb,),
            in_specs=[gspec, pl.BlockSpec((3, tr, cs), lambda i, m: (0, i, 0))],
            out_specs=pl.BlockSpec((tr, cs), lambda i, m: (i, 0))),
        out_shape=jax.ShapeDtypeStruct((r, cs), F32), compiler_params=_params(("parallel",)),
    )(chip, g, slots)


def _pair_adamw(name, mine, theirs, w, m, v, layer, bufs):
    r, cs = mine.shape
    tr = _rows_tile(r)
    c1 = 1.0 - B1 ** STEP
    c2 = 1.0 - B2 ** STEP

    def body(a_ref, b_ref, w_ref, m_ref, v_ref, _g, _d, _m, _v, g_ref, d_ref, mo_ref, vo_ref):
        gv = a_ref[...] + b_ref[...]
        mn = B1 * m_ref[...] + (1.0 - B1) * gv
        vn = B2 * v_ref[...] + (1.0 - B2) * (gv * gv)
        g_ref[...] = gv
        mo_ref[...] = mn
        vo_ref[...] = vn
        d_ref[...] = -LR * ((mn / c1) / (jnp.sqrt(vn / c2) + ADAM_EPS) + WD * w_ref[...])

    blk = pl.BlockSpec((tr, cs), lambda i: (i, 0))
    lay = pl.BlockSpec((None, tr, cs), lambda i: (layer, i, 0))
    return pl.pallas_call(
        body, name=name, grid=(r // tr,), in_specs=[blk, blk, lay, lay, lay] + [ANY] * 4, out_specs=[lay] * 4,
        out_shape=[jax.ShapeDtypeStruct(b.shape, b.dtype) for b in bufs],
        input_output_aliases={5 + i: i for i in range(4)}, compiler_params=_params(("parallel",)),
    )(mine, theirs, w, m, v, *bufs)


def _allreduce_small(p):
    rows = p.shape[0]

    def body(p_ref, o_ref, gath, send, recv):
        x, y, c = _place()
        me = 4 * x + 2 * y + c

        def flip(v, bit):
            return 1 - v if bit else v

        gath[me] = p_ref[...]
        cps = []
        for k in range(1, 8):
            peer = (flip(x, k & 4), flip(y, k & 2), flip(c, k & 1))
            cps.append(pltpu.make_async_remote_copy(src_ref=p_ref, dst_ref=gath.at[me], send_sem=send.at[k - 1],
                                                    recv_sem=recv.at[k - 1], device_id=peer, device_id_type=MESH))
        for cp in cps:
            cp.start()
        for k in range(1, 8):
            slot = gath.at[4 * flip(x, k & 4) + 2 * flip(y, k & 2) + flip(c, k & 1)]
            pltpu.make_async_remote_copy(src_ref=slot, dst_ref=slot, send_sem=send.at[k - 1], recv_sem=recv.at[k - 1],
                                         device_id=(x, y, c), device_id_type=MESH).wait_recv()
        for cp in cps:
            cp.wait_send()
        acc = gath[0]
        for k in range(1, 8):
            acc = acc + gath[k]
        o_ref[...] = acc

    vm = pl.BlockSpec(memory_space=pltpu.VMEM)
    return pl.pallas_call(
        body, name="allreduce_small", in_specs=[vm], out_specs=vm, out_shape=jax.ShapeDtypeStruct(p.shape, F32),
        scratch_shapes=[pltpu.VMEM((8, rows, 128), F32), pltpu.SemaphoreType.DMA((7,)), pltpu.SemaphoreType.DMA((7,))],
        compiler_params=pltpu.CompilerParams(vmem_limit_bytes=VMEM_LIMIT),
    )(p)


def _pack(parts):
    flat = []
    for p in parts:
        v = p.reshape(-1).astype(F32)
        flat.append(jnp.pad(v, (0, (-v.shape[0]) % 128)))
    v = jnp.concatenate(flat)
    v = jnp.pad(v, (0, (-v.shape[0]) % (512 * 128)))
    return v.reshape(-1, 128)


def _unpack(buf, like):
    v = buf.reshape(-1)
    out, off = [], 0
    for p in like:
        nelem = 1
        for s in p.shape:
            nelem *= s
        out.append(v[off:off + nelem].reshape(p.shape))
        off += nelem + (-nelem) % 128
    return out


def kernel(x, mem, positions, mem_norm_g, mix_pre_g, mix_post_g, w_in, gm_v_g, gm_w_s, gm_b_s, pool_w, pool_scale, attn_sinks, w_o, x_pre_g, x_post_g, w_xq, w_xkv, w_xo, ffn_pre_g, ffn_post_g, w_gate_up, w_down, loss_target, m_mem_norm_g, m_mix_pre_g, m_mix_post_g, m_w_in, m_gm_v_g, m_gm_w_s, m_gm_b_s, m_pool_w, m_pool_scale, m_attn_sinks, m_w_o, m_x_pre_g, m_x_post_g, m_w_xq, m_w_xkv, m_w_xo, m_ffn_pre_g, m_ffn_post_g, m_w_gate_up, m_w_down, v_mem_norm_g, v_mix_pre_g, v_mix_post_g, v_w_in, v_gm_v_g, v_gm_w_s, v_gm_b_s, v_pool_w, v_pool_scale, v_attn_sinks, v_w_o, v_x_pre_g, v_x_post_g, v_w_xq, v_w_xkv, v_w_xo, v_ffn_pre_g, v_ffn_post_g, v_w_gate_up, v_w_down):
    args = (x, mem, positions, mem_norm_g, mix_pre_g, mix_post_g, w_in, gm_v_g, gm_w_s, gm_b_s, pool_w, pool_scale, attn_sinks, w_o, x_pre_g, x_post_g, w_xq, w_xkv, w_xo, ffn_pre_g, ffn_post_g, w_gate_up, w_down)
    moms_m = (m_mem_norm_g, m_mix_pre_g, m_mix_post_g, m_w_in, m_gm_v_g, m_gm_w_s, m_gm_b_s, m_pool_w, m_pool_scale, m_attn_sinks, m_w_o, m_x_pre_g, m_x_post_g, m_w_xq, m_w_xkv, m_w_xo, m_ffn_pre_g, m_ffn_post_g, m_w_gate_up, m_w_down)
    moms_v = (v_mem_norm_g, v_mix_pre_g, v_mix_post_g, v_w_in, v_gm_v_g, v_gm_w_s, v_gm_b_s, v_pool_w, v_pool_scale, v_attn_sinks, v_w_o, v_x_pre_g, v_x_post_g, v_w_xq, v_w_xkv, v_w_xo, v_ffn_pre_g, v_ffn_post_g, v_w_gate_up, v_w_down)
    P = dict(zip(NAMES, args))
    P['loss_target'] = loss_target
    M = dict(zip(WEIGHTS, moms_m))
    V = dict(zip(WEIGHTS, moms_v))
    depth = w_in.shape[0]
    nbig = len(BIG)
    axes = [BIG_AXIS[n] for n in BIG]
    sizes = [P[n].shape[a] for n, a in zip(BIG, axes)]
    chip = (2 * lax.axis_index("x") + lax.axis_index("y")).astype(jnp.int32).reshape(1)

    groups = [['w_in'], ['w_o', 'w_xq', 'w_xkv', 'w_xo'], ['w_gate_up', 'w_down']]
    units = [(l, g) for l in range(depth) for g in groups]
    unit_of = {(l, n): i for i, (l, names) in enumerate(units) for n in names}
    ax = lambda names: [BIG_AXIS[n] for n in names]
    sz = lambda names: [P[n].shape[BIG_AXIS[n]] for n in names]

    full = {n: _place_own("place_" + n, P[n], BIG_AXIS[n], chip) for n in BIG}
    gathers = []
    for i, (l, names) in enumerate(units):
        send, recv, _, land, _ = _split_start("gather_start%d" % i, [], [full[n] for n in names],
                                              3 * len(names), _gather_plan(ax(names), sz(names), l))
        full.update(zip(names, land))
        gathers.append((send, recv))
    gathered = set()

    def weights_of(l, names, after):
        i = unit_of[(l, names[0])]
        if i not in gathered:
            _, unames = units[i]
            send, recv = gathers[i]
            _, land = _split_wait("gather_wait%d" % i, send, recv, [], [full[n] for n in unames], after,
                                  _gather_plan(ax(unames), sz(unames), l))
            full.update(zip(unames, land))
            gathered.add(i)
        return {n: (full[n], l) for n in names}

    outs = {n: [lax.empty(P[n].shape, F32) for _ in range(4)] for n in BIG}
    gunits = [(l, BIG) for l in range(depth - 1, 0, -1)] + [
        (0, g) for g in (['w_gate_up', 'w_down'], ['w_xq', 'w_xkv', 'w_xo'], ['w_o'], ['w_in'])]
    collected, scatters, pairs = {}, {}, {}

    def finish_scatter(i, after):
        _, names = gunits[i]
        send, recv, g_l, slots = scatters.pop(i)
        g_l, slots = _split_wait("scatter_wait%d" % i, send, recv, g_l, slots, after,
                                 _scatter_plan(ax(names), sz(names)))
        mine = [_chip_sum("chip_sum_" + n, g.reshape(g.shape[1:]), sl, BIG_AXIS[n], chip)
                for n, g, sl in zip(names, g_l, slots)]
        send, recv, mine, theirs, tok = _split_start("pair_start%d" % i, mine, [lax.empty(a.shape, F32) for a in mine],
                                                     len(names), _pair_plan)
        pairs[i] = (send, recv, mine, theirs)
        return tok[:1, :1]

    def finish_pair(i, after):
        l, names = gunits[i]
        send, recv, mine, theirs = pairs.pop(i)
        mine, theirs = _split_wait("pair_wait%d" % i, send, recv, mine, theirs, after, _pair_plan)
        for n, a, b in zip(names, mine, theirs):
            outs[n] = _pair_adamw("adamw_" + n, a, b, P[n], M[n], V[n], l, outs[n])

    def grads_of(l, g_part, after):
        collected.update({(l, n): g for n, g in g_part.items()})
        tok = jnp.zeros((1, 1), F32)
        for i, (ul, names) in enumerate(gunits):
            if ul != l or ('started', i) in collected or any((l, n) not in collected for n in names):
                continue
            collected[('started', i)] = True
            srcs = [collected[(l, n)].reshape((1,) + collected[(l, n)].shape) for n in names]
            send, recv, srcs, slots, t = _split_start("scatter_start%d" % i, srcs,
                                                      [lax.empty((3,) + P[n].shape[1:], BF16) for n in names],
                                                      3 * len(names), _scatter_plan(ax(names), sz(names)))
            scatters[i] = (send, recv, srcs, slots)
            tok = tok + t[:1, :1]
            if i >= 1:
                tok = tok + finish_scatter(i - 1, after)
            if i >= 2:
                finish_pair(i - 2, after)
        return tok

    loss_part, dx, small_g = _fwd_bwd(P, weights_of, grads_of)
    loss = lax.psum(loss_part[0, 0], ("x", "y", "c"))
    grad_x = dx.reshape(x.shape)

    small_like = [P[n] for n in SMALL]
    gsum = _allreduce_small(_pack(small_g))
    dlt, mn, vn = _adamw("adamw_small", gsum, _pack(small_like), _pack([M[n] for n in SMALL]),
                         _pack([V[n] for n in SMALL]), 512)
    grads, deltas, new_m, new_v = {}, {}, {}, {}
    for name_map, buf in ((grads, gsum), (deltas, dlt), (new_m, mn), (new_v, vn)):
        for n, a in zip(SMALL, _unpack(buf, small_like)):
            name_map[n] = a

    last = len(gunits) - 1
    finish_scatter(last, dlt)
    finish_pair(last - 1, dlt)
    finish_pair(last, dlt)
    for n in BIG:
        grads[n], deltas[n], new_m[n], new_v[n] = outs[n]

    return (loss, grad_x, *[grads[n] for n in WEIGHTS], *[deltas[n] for n in WEIGHTS],
            *[new_m[n] for n in WEIGHTS], *[new_v[n] for n in WEIGHTS])


def _fwd_bwd(P, weights_of, grads_of):
    (x, mem, positions, mem_norm_g, mix_pre_g, mix_post_g, w_in, gm_v_g, gm_w_s, gm_b_s, pool_w, pool_scale, attn_sinks,
     w_o, x_pre_g, x_post_g, w_xq, w_xkv, w_xo, ffn_pre_g, ffn_post_g, w_gate_up, w_down) = [P[n] for n in NAMES]
    x0 = x[0]
    s, d = x0.shape
    depth = w_in.shape[0]
    tgt = P['loss_target'][0]
    tmn = 256
    tmr = min(512, s)

    half = HEAD // 2
    inv = ROPE_THETA ** (-jnp.arange(half, dtype=F32) / half)
    ang = positions[0].astype(F32)[:, None] * inv
    cos, sin = jnp.cos(ang), jnp.sin(ang)
    cosq = jnp.tile(jnp.concatenate([cos, cos], axis=-1), (1, 8))
    sinq = jnp.tile(jnp.concatenate([-sin, sin], axis=-1), (1, 8))

    row = lambda a, l: a[l].reshape(1, -1)
    memn = _prenorm("mem_norm", mem[0], mem_norm_g.reshape(1, d), tmn)
    pw_bd = []
    for l in range(depth):
        bd = jnp.zeros((256, 256), F32)
        for g in range(4):
            bd = lax.dynamic_update_slice(bd, pool_w[l, g], (64 * g, 64 * g))
        pw_bd.append(bd)

    saved = []
    xc = x0
    h = _prenorm("pre_norm0", x0, row(mix_pre_g, 0), tmn)
    for l in range(depth):
        W = weights_of(l, ['w_in'], xc)
        sv = {'x0': xc, 'h1': h}
        z = _mm_nn("fwd_w_in", h, *W['w_in'], tm=1024, tn=512, tk=d, out_dtype=F32)
        abc = _mixer_fwd("mixer_fwd", z, cosq, sinq, row(gm_v_g, l), gm_w_s[l], gm_b_s[l].T, pw_bd[l],
                         row(pool_scale, l), row(attn_sinks, l))
        W.update(weights_of(l, ['w_o'], z))
        mix, xc, h = _mm_rows("fwd_w_o", abc, *W['w_o'], 'nn', tm=tmr, tk=d, rows_in=[xc],
                              params=[row(mix_post_g, l), row(x_pre_g, l)], rows_out=[BF16, F32, BF16],
                              epilogue=_post_pre_rows)
        sv.update(z=z, abc=abc, mix=mix, x1=xc, h2=h)
        W.update(weights_of(l, ['w_xq', 'w_xkv', 'w_xo'], xc))
        q = _mm_nn("fwd_w_xq", h, *W['w_xq'], tm=1024, tn=512, tk=d, out_dtype=BF16)
        kv = _mm_nn("fwd_w_xkv", memn, *W['w_xkv'], tm=256, tn=512, tk=d, out_dtype=BF16)
        o = _xattn_fwd("xattn_fwd", q, kv, 512)
        xo, xc, h = _mm_rows("fwd_w_xo", o, *W['w_xo'], 'nn', tm=tmr, tk=d, rows_in=[xc],
                             params=[row(x_post_g, l), row(ffn_pre_g, l)], rows_out=[BF16, F32, BF16],
                             epilogue=_post_pre_rows)
        sv.update(q=q, kv=kv, o=o, xo=xo, x2=xc, h3=h)
        W.update(weights_of(l, ['w_gate_up', 'w_down'], xc))
        dff = W['w_down'][0].shape[1]
        gate, up, act = _ffn_up("ffn_up", h, *W['w_gate_up'], 512, dff // 2)
        sv.update(gate=gate, up=up, act=act)
        if l + 1 < depth:
            f, xc, h = _mm_rows("fwd_w_down", act, *W['w_down'], 'nn', tm=tmr, tk=dff // 2, rows_in=[xc],
                                params=[row(ffn_post_g, l), row(mix_pre_g, l + 1)], rows_out=[BF16, F32, BF16],
                                epilogue=_post_pre_rows)
            sv.update(f=f)
        saved.append(sv)
    gs = {n: [None] * depth for n in SMALL if n != 'mem_norm_g'}
    dx, dfn, gs['ffn_post_g'][depth - 1], loss_part = _mm_rows(
        "fwd_w_down_loss", saved[-1]['act'], *W['w_down'], 'nn', tm=tmr, tk=dff // 2, rows_in=[xc, tgt],
        params=[row(ffn_post_g, depth - 1)], rows_out=[F32, BF16], n_sums=2, epilogue=_make_loss_rows(d))

    dmemn = None
    tok = jnp.zeros((1, 1), F32)
    for l in reversed(range(depth)):
        sv, W, G = saved[l], weights_of(l, BIG, dx), {}
        G['w_down'] = _mm_tn("dw_down", sv['act'], dfn, tm=dff // 2, tn=d, tk=512, dep=tok)
        dgu = _ffn_act_bwd("ffn_act_bwd", dfn, *W['w_down'], sv['gate'], sv['up'], 256)
        G['w_gate_up'] = _mm_tn("dw_gate_up", sv['h3'], dgu, tm=d, tn=dff // 2, tk=512)
        dx, dxo, gs['ffn_pre_g'][l], gs['x_post_g'][l] = _mm_rows(
            "bwd_w_gate_up", dgu, *W['w_gate_up'], 'nt', tm=tmr, tk=dff // 2, rows_in=[sv['x2'], dx, sv['xo']],
            params=[row(ffn_pre_g, l), row(x_post_g, l)], rows_out=[F32, BF16], n_sums=2, epilogue=_bwd_rows)
        tok = grads_of(l, {n: G[n] for n in ('w_gate_up', 'w_down')}, dx)
        G['w_xo'] = _mm_tn("dw_xo", sv['o'], dxo, tm=d, tn=d, tk=512, dep=tok)
        do = _mm_nt("bwd_w_xo", dxo, *W['w_xo'], tm=1024, tn=512, tk=d, out_dtype=BF16)
        dq, dkv = _xattn_bwd("xattn_bwd", sv['q'], sv['kv'], do, 512)
        dkv = dkv.astype(BF16)
        G['w_xkv'] = _mm_tn("dw_xkv", memn, dkv, tm=d, tn=d, tk=mem.shape[1])
        dmemn = _mm_nt("bwd_w_xkv", dkv, *W['w_xkv'], tm=mem.shape[1], tn=512, tk=2 * d, out_dtype=F32, add=dmemn)
        G['w_xq'] = _mm_tn("dw_xq", sv['h2'], dq, tm=d, tn=d, tk=512)
        dx, dmix, gs['x_pre_g'][l], gs['mix_post_g'][l] = _mm_rows(
            "bwd_w_xq", dq, *W['w_xq'], 'nt', tm=tmr, tk=d, rows_in=[sv['x1'], dx, sv['mix']],
            params=[row(x_pre_g, l), row(mix_post_g, l)], rows_out=[F32, BF16], n_sums=2, epilogue=_bwd_rows)
        tok = grads_of(l, {n: G[n] for n in ('w_xq', 'w_xkv', 'w_xo')}, dx)
        G['w_o'] = _mm_tn("dw_o", sv['abc'], dmix, tm=d, tn=d, tk=512, dep=tok)
        dabc = _mm_nt("bwd_w_o", dmix, *W['w_o'], tm=1024, tn=512, tk=d, out_dtype=F32)
        tok = grads_of(l, {'w_o': G['w_o']}, dabc)
        dz, dgv, dws, dbt, dpw, dpsc, dsnk = _mixer_bwd(
            "mixer_bwd", sv['z'], dabc, cosq, sinq, row(gm_v_g, l) + tok, gm_w_s[l], gm_b_s[l].T, pw_bd[l],
            row(pool_scale, l), row(attn_sinks, l))
        gs['gm_v_g'][l] = dgv
        gs['gm_w_s'][l] = dws
        gs['gm_b_s'][l] = dbt.T
        gs['pool_w'][l] = jnp.stack([dpw[64 * g:64 * (g + 1), 64 * g:64 * (g + 1)] for g in range(4)])
        gs['pool_scale'][l] = dpsc
        gs['attn_sinks'][l] = dsnk
        G['w_in'] = _mm_tn("dw_in", sv['h1'], dz, tm=d, tn=dz.shape[1], tk=512)
        if l > 0:
            dx, dfn, gs['mix_pre_g'][l], gs['ffn_post_g'][l - 1] = _mm_rows(
                "bwd_w_in", dz, *W['w_in'], 'nt', tm=tmr, tk=dz.shape[1], rows_in=[sv['x0'], dx, saved[l - 1]['f']],
                params=[row(mix_pre_g, l), row(ffn_post_g, l - 1)], rows_out=[F32, BF16], n_sums=2,
                epilogue=_bwd_rows)
        else:
            dx, gs['mix_pre_g'][l] = _mm_rows(
                "bwd_w_in_first", dz, *W['w_in'], 'nt', tm=tmr, tk=dz.shape[1], rows_in=[sv['x0'], dx],
                params=[row(mix_pre_g, l)], rows_out=[F32], n_sums=1, epilogue=_bwd_rows_first)
        tok = grads_of(l, {'w_in': G['w_in']}, dx)
    _, dg_mem = _norm_bwd("bwd_mem_norm", mem[0], mem_norm_g.reshape(1, d), dmemn, None, BF16, tmn)
    small_g = []
    for n in SMALL:
        if n == 'mem_norm_g':
            small_g.append(dg_mem.reshape(P[n].shape))
        else:
            small_g.append(jnp.stack([a.reshape(P[n].shape[1:]) for a in gs[n]]))
    return loss_part, dx, small_g
```

```python
import functools

import jax
import jax.numpy as jnp
from jax import lax
from jax.experimental import pallas as pl
from jax.experimental.pallas import tpu as pltpu

F32 = jnp.float32
BF16 = jnp.bfloat16
EPS = 1e-6
CHUNK = 128
HEAD = 64
ROPE_THETA = 10000.0
POOL_WINDOWS = (2, 4, 8, 16)
LR, B1, B2, ADAM_EPS, WD, STEP = 0.001, 0.9, 0.999, 1e-08, 0.01, 10
MESH = pl.DeviceIdType.MESH
VMEM_LIMIT = 56 * 1024 * 1024

NAMES = ['x', 'mem', 'positions', 'mem_norm_g', 'mix_pre_g', 'mix_post_g', 'w_in', 'gm_v_g', 'gm_w_s', 'gm_b_s',
         'pool_w', 'pool_scale', 'attn_sinks', 'w_o', 'x_pre_g', 'x_post_g', 'w_xq', 'w_xkv', 'w_xo', 'ffn_pre_g',
         'ffn_post_g', 'w_gate_up', 'w_down']
WEIGHTS = NAMES[3:]
BIG = ['w_in', 'w_o', 'w_xq', 'w_xkv', 'w_xo', 'w_gate_up', 'w_down']
BIG_AXIS = {'w_in': 2, 'w_o': 1, 'w_xq': 1, 'w_xkv': 2, 'w_xo': 1, 'w_gate_up': 2, 'w_down': 1}
SMALL = [n for n in WEIGHTS if n not in BIG]

NN = (((1,), (0,)), ((), ()))
NT = (((1,), (1,)), ((), ()))
TN = (((0,), (0,)), ((), ()))


def _dot(a, b, dims=NN):
    return lax.dot_general(a, b, dims, preferred_element_type=F32)


def _params(sem):
    return pltpu.CompilerParams(dimension_semantics=sem, vmem_limit_bytes=VMEM_LIMIT)


def _rows_tile(rows, limit=256):
    return max(t for t in range(16, limit + 1, 16) if rows % t == 0)


def _mm(name, a, a_spec, b, b_spec, dims, grid, nk, out_shape, out_spec, add=None, add_spec=None, dep=None):
    acc_shape = out_spec.block_shape

    def body(*refs):
        a_ref, b_ref = refs[0], refs[1]
        pos = 2
        add_ref = None
        if add is not None:
            add_ref = refs[pos]
            pos += 1
        if dep is not None:
            pos += 1
        o_ref = refs[pos]
        part = _dot(a_ref[...].astype(BF16), b_ref[...].astype(BF16), dims)
        if nk == 1:
            if add_ref is not None:
                part = part + add_ref[...]
            o_ref[...] = part.astype(o_ref.dtype)
        else:
            acc_ref = refs[pos + 1]
            k = pl.program_id(2)

            @pl.when(k == 0)
            def _():
                acc_ref[...] = part if add_ref is None else part + add_ref[...]

            @pl.when(k > 0)
            def _():
                acc_ref[...] += part

            @pl.when(k == nk - 1)
            def _():
                o_ref[...] = acc_ref[...].astype(o_ref.dtype)

    ops, specs = [a, b], [a_spec, b_spec]
    if add is not None:
        ops.append(add)
        specs.append(add_spec)
    if dep is not None:
        ops.append(dep)
        specs.append(pl.BlockSpec((1, 1), lambda i, j, k: (0, 0)))
    return pl.pallas_call(
        body, name=name, grid=grid, in_specs=specs, out_specs=out_spec, out_shape=out_shape,
        scratch_shapes=[pltpu.VMEM(acc_shape, F32)] if nk > 1 else [],
        compiler_params=_params(("parallel", "parallel", "arbitrary")),
    )(*ops)


def _wspec(block, layer, fn):
    return pl.BlockSpec((None,) + block, lambda i, j, k: (layer,) + fn(i, j, k))


def _mm_nn(name, a, w, layer, *, tm, tn, tk, out_dtype, n0=0, n=None, k0=0):
    m, kk = a.shape
    n = w.shape[2] if n is None else n
    tm = min(tm, m)
    nk = kk // tk
    return _mm(name, a, pl.BlockSpec((tm, tk), lambda i, j, k: (i, k)),
               w, _wspec((tk, tn), layer, lambda i, j, k: (k + k0 // tk, j + n0 // tn)), NN,
               (m // tm, n // tn, nk), nk, jax.ShapeDtypeStruct((m, n), out_dtype),
               pl.BlockSpec((tm, tn), lambda i, j, k: (i, j)))


def _mm_nt(name, a, w, layer, *, tm, tn, tk, out_dtype, k0=0, add=None, dep=None):
    m, kk = a.shape
    n = w.shape[1]
    tm = min(tm, m)
    nk = kk // tk
    ospec = pl.BlockSpec((tm, tn), lambda i, j, k: (i, j))
    return _mm(name, a, pl.BlockSpec((tm, tk), lambda i, j, k: (i, k)),
               w, _wspec((tn, tk), layer, lambda i, j, k: (j, k + k0 // tk)), NT,
               (m // tm, n // tn, nk), nk, jax.ShapeDtypeStruct((m, n), out_dtype), ospec,
               add=add, add_spec=ospec if add is not None else None, dep=dep)


def _mm_tn(name, a, b, *, tm, tn, tk):
    kk, m = a.shape
    n = b.shape[1]
    tk = min(tk, kk)
    nk = kk // tk
    return _mm(name, a, pl.BlockSpec((tk, tm), lambda i, j, k: (k, i)),
               b, pl.BlockSpec((tk, tn), lambda i, j, k: (k, j)), TN,
               (m // tm, n // tn, nk), nk, jax.ShapeDtypeStruct((m, n), BF16),
               pl.BlockSpec((tm, tn), lambda i, j, k: (i, j)))


def _mm_rows(name, a, w, layer, mode, *, tm, tk, k0=0, rows_in=(), params=(), rows_out=(), n_sums=0, epilogue):
    m, kk = a.shape
    n = w.shape[2] if mode == 'nn' else w.shape[1]
    nk = kk // tk
    nr, npar, no = len(rows_in), len(params), len(rows_out)

    def body(*refs):
        a_ref, w_ref = refs[0], refs[1]
        rin = refs[2:2 + nr]
        par = refs[2 + nr:2 + nr + npar]
        outs = refs[2 + nr + npar:2 + nr + npar + no]
        sums = refs[2 + nr + npar + no:2 + nr + npar + no + n_sums]
        i, k = pl.program_id(0), pl.program_id(1)
        part = _dot(a_ref[...], w_ref[...], NN if mode == 'nn' else NT)

        def finish(acc):
            res, sm = epilogue(acc, [r[...] for r in rin], [p[...] for p in par])
            for r, v in zip(outs, res):
                r[...] = v.astype(r.dtype)

            @pl.when(i == 0)
            def _():
                for r in sums:
                    r[...] = jnp.zeros_like(r)

            for r, v in zip(sums, sm):
                r[...] += v

        if nk == 1:
            finish(part)
        else:
            acc_ref = refs[-1]

            @pl.when(k == 0)
            def _():
                acc_ref[...] = part

            @pl.when(k > 0)
            def _():
                acc_ref[...] += part

            @pl.when(k == nk - 1)
            def _():
                finish(acc_ref[...])

    if mode == 'nn':
        wspec = pl.BlockSpec((None, tk, n), lambda i, k: (layer, k + k0 // tk, 0))
    else:
        wspec = pl.BlockSpec((None, n, tk), lambda i, k: (layer, 0, k + k0 // tk))
    rowblk = pl.BlockSpec((tm, n), lambda i, k: (i, 0))
    one = pl.BlockSpec((1, n), lambda i, k: (0, 0))
    return pl.pallas_call(
        body, name=name, grid=(m // tm, nk),
        in_specs=[pl.BlockSpec((tm, tk), lambda i, k: (i, k)), wspec] + [rowblk] * nr + [one] * npar,
        out_specs=[rowblk] * no + [one] * n_sums,
        out_shape=[jax.ShapeDtypeStruct((m, n), dt) for dt in rows_out] +
                  [jax.ShapeDtypeStruct((1, n), F32)] * n_sums,
        scratch_shapes=[pltpu.VMEM((tm, n), F32)] if nk > 1 else [],
        compiler_params=_params(("arbitrary", "arbitrary")),
    )(a, w, *rows_in, *params)


def _rstd(x):
    return lax.rsqrt(jnp.mean(x * x, axis=-1, keepdims=True) + EPS)


def _norm_back(xin, g, dy):
    r = _rstd(xin)
    xh = xin * r
    dyg = dy * g
    return r * (dyg - xh * jnp.mean(dyg * xh, axis=-1, keepdims=True)), jnp.sum(dy * xh, axis=0, keepdims=True)


def _post_pre_rows(y, rows, pars):
    xn = rows[0] + y * _rstd(y) * pars[0]
    return [y, xn, xn * _rstd(xn) * pars[1]], []


def _make_loss_rows(d):
    def fn(y, rows, pars):
        x, tgt = rows
        err = x + y * _rstd(y) * pars[0] - tgt
        dout = err * (1.0 / d)
        dy, dg = _norm_back(y, pars[0], dout)
        lsum = 0.5 * jnp.sum(jnp.mean(err * err, axis=-1, keepdims=True), axis=0, keepdims=True)
        return [dout, dy], [dg, jnp.broadcast_to(lsum, dg.shape)]
    return fn


def _bwd_rows(dh, rows, pars):
    xin, resid, yprev = rows
    dxa, dg_pre = _norm_back(xin, pars[0], dh)
    dx = resid + dxa
    dyp, dg_post = _norm_back(yprev.astype(F32), pars[1], dx)
    return [dx, dyp], [dg_pre, dg_post]


def _bwd_rows_first(dh, rows, pars):
    xin, resid = rows
    dxa, dg_pre = _norm_back(xin, pars[0], dh)
    return [resid + dxa], [dg_pre]


def _row(d):
    return pl.BlockSpec((1, d), lambda i: (0, 0))


def _prenorm(name, x, g, tm):
    m, d = x.shape

    def body(x_ref, g_ref, o_ref):
        xv = x_ref[...]
        o_ref[...] = (xv * _rstd(xv) * g_ref[...]).astype(BF16)

    blk = pl.BlockSpec((tm, d), lambda i: (i, 0))
    return pl.pallas_call(body, name=name, grid=(m // tm,), in_specs=[blk, _row(d)], out_specs=blk,
                          out_shape=jax.ShapeDtypeStruct((m, d), BF16), compiler_params=_params(("parallel",)))(x, g)


def _norm_bwd(name, xin, g, dy, resid, out_dtype, tm):
    m, d = xin.shape

    def body(*refs):
        if resid is None:
            x_ref, g_ref, dy_ref, dx_ref, dg_ref = refs
        else:
            x_ref, g_ref, dy_ref, r_ref, dx_ref, dg_ref = refs
        xv = x_ref[...]
        r = _rstd(xv)
        xh = xv * r
        dyv = dy_ref[...].astype(F32)
        dyg = dyv * g_ref[...]
        dx = r * (dyg - xh * jnp.mean(dyg * xh, axis=-1, keepdims=True))
        if resid is not None:
            dx = dx + r_ref[...]
        dx_ref[...] = dx.astype(dx_ref.dtype)

        @pl.when(pl.program_id(0) == 0)
        def _():
            dg_ref[...] = jnp.zeros_like(dg_ref)

        dg_ref[...] += jnp.sum(dyv * xh, axis=0, keepdims=True)

    blk = pl.BlockSpec((tm, d), lambda i: (i, 0))
    ops = [xin, g, dy] + ([] if resid is None else [resid])
    specs = [blk, _row(d), blk] + ([] if resid is None else [blk])
    return pl.pallas_call(
        body, name=name, grid=(m // tm,), in_specs=specs, out_specs=[blk, _row(d)],
        out_shape=[jax.ShapeDtypeStruct((m, d), out_dtype), jax.ShapeDtypeStruct((1, d), F32)],
        compiler_params=_params(("arbitrary",)))(*ops)


def _adamw(name, g, w, m, v, tr):
    rows, cols = g.shape
    c1 = 1.0 - B1 ** STEP
    c2 = 1.0 - B2 ** STEP

    def body(g_ref, w_ref, m_ref, v_ref, d_ref, mo_ref, vo_ref):
        gv = g_ref[...]
        mn = B1 * m_ref[...] + (1.0 - B1) * gv
        vn = B2 * v_ref[...] + (1.0 - B2) * (gv * gv)
        mo_ref[...] = mn
        vo_ref[...] = vn
        d_ref[...] = -LR * ((mn / c1) / (jnp.sqrt(vn / c2) + ADAM_EPS) + WD * w_ref[...])

    blk = pl.BlockSpec((tr, cols), lambda i: (i, 0))
    sd = jax.ShapeDtypeStruct((rows, cols), F32)
    return pl.pallas_call(body, name=name, grid=(rows // tr,), in_specs=[blk] * 4, out_specs=[blk] * 3,
                          out_shape=[sd, sd, sd], compiler_params=_params(("parallel",)))(g, w, m, v)


def _gelu_parts(x):
    c = 0.7978845608028654
    t = jnp.tanh(c * (x + 0.044715 * (x * x * x)))
    return 0.5 * x * (1.0 + t), t


def _gelu_grad(x, t):
    c = 0.7978845608028654
    return 0.5 * (1.0 + t) + 0.5 * x * (1.0 - t * t) * (c * (1.0 + 3.0 * 0.044715 * x * x))


def _rot_half(x):
    ax = x.ndim - 1
    w = x.shape[ax]
    lane = lax.broadcasted_iota(jnp.int32, x.shape, ax)
    return jnp.where((lane & 63) < 32, pltpu.roll(x, w - 32, ax), pltpu.roll(x, 32, ax))


def _gm_group(gel, g, gv, ws_ref, bt):
    u = gel[:, HEAD * g:HEAD * (g + 1)]
    vg = gel[:, 256 + HEAD * g:256 + HEAD * (g + 1)]
    r = _rstd(vg)
    xh = vg * r
    vn = (xh * gv[:, HEAD * g:HEAD * (g + 1)]).astype(BF16)
    row = lax.broadcasted_iota(jnp.int32, (CHUNK, CHUNK), 0)
    col = lax.broadcasted_iota(jnp.int32, (CHUNK, CHUNK), 1)
    causal = col <= row
    wc = jnp.where(causal, ws_ref[g], 0.0).astype(BF16)
    mixed = _dot(wc, vn) + bt[:, g:g + 1]
    return u, r, xh, vn, wc, causal, mixed


def _lane_select(lane, vals):
    return jnp.where(lane < 64, vals[0], jnp.where(lane < 128, vals[1], jnp.where(lane < 192, vals[2], vals[3])))


def _pool_fwd(pc, pp, ci):
    ext = jnp.concatenate([pp, pc], axis=0)
    s2 = ext + pltpu.roll(ext, 1, 0)
    s4 = s2 + pltpu.roll(s2, 2, 0)
    s8 = s4 + pltpu.roll(s4, 4, 0)
    s16 = s8 + pltpu.roll(s8, 8, 0)
    t1 = ci * CHUNK + lax.broadcasted_iota(jnp.int32, (CHUNK, 1), 0) + 1
    lane = lax.broadcasted_iota(jnp.int32, (1, 256), 1)
    cnt = _lane_select(lane, [jnp.minimum(t1, w).astype(F32) for w in POOL_WINDOWS])
    ssel = _lane_select(lane, [s[CHUNK:] for s in (s2, s4, s8, s16)])
    return ssel / cnt - pc, cnt, lane


def _attn_prep(zc, zpkv, cq, sq, cp, sp, ci):
    q = zc[:, 768:1280]
    kc = zc[:, 1280:1408]
    vc = zc[:, 1408:1536]
    kp = zpkv[:, :128]
    vp = zpkv[:, 128:]
    qr = q * cq + _rot_half(q) * sq
    krc = kc * cq[:, :128] + _rot_half(kc) * sq[:, :128]
    krp = kp * cp + _rot_half(kp) * sp
    kband = jnp.concatenate([krp, krc], axis=0).astype(BF16)
    vband = jnp.concatenate([vp, vc], axis=0).astype(BF16)
    row = lax.broadcasted_iota(jnp.int32, (CHUNK, 2 * CHUNK), 0)
    col = lax.broadcasted_iota(jnp.int32, (CHUNK, 2 * CHUNK), 1)
    valid = ((col < CHUNK) & (col > row) & (ci > 0)) | ((col >= CHUNK) & (col - CHUNK <= row))
    return qr, kband, vband, valid


def _head_probs(qh, kh, valid, sink):
    s = _dot(qh, kh, NT) * (HEAD ** -0.5)
    s = jnp.where(valid, s, -1e30)
    mx = jnp.maximum(jnp.max(s, axis=-1, keepdims=True), sink)
    e = jnp.exp(s - mx)
    es = jnp.exp(sink - mx)
    den = jnp.sum(e, axis=-1, keepdims=True) + es
    return e / den, es / den


def _mixer_specs(nb, rev):
    def cur(i):
        return nb - 1 - i if rev else i

    def prev(i):
        return jnp.maximum(cur(i) - 1, 0)

    full = lambda shape: pl.BlockSpec(shape, lambda i: (0,) * len(shape))
    specs = [
        pl.BlockSpec((CHUNK, 1536), lambda i: (cur(i), 0)),
        pl.BlockSpec((CHUNK, 256), lambda i: (prev(i), 2)),
        pl.BlockSpec((CHUNK, 256), lambda i: (prev(i), 5)),
        pl.BlockSpec((CHUNK, 512), lambda i: (cur(i), 0)),
        pl.BlockSpec((CHUNK, 512), lambda i: (cur(i), 0)),
        pl.BlockSpec((CHUNK, 128), lambda i: (prev(i), 0)),
        pl.BlockSpec((CHUNK, 128), lambda i: (prev(i), 0)),
        full((1, 256)), full((4, CHUNK, CHUNK)), full((CHUNK, 4)), full((256, 256)), full((1, 256)), full((1, 8)),
    ]
    return specs, cur


def _mixer_fwd(name, z, cosq, sinq, gv, ws, bt, pw, psc, snk):
    s = z.shape[0]
    nb = s // CHUNK
    specs, _ = _mixer_specs(nb, False)

    def body(zc_ref, zpp_ref, zpkv_ref, cq_ref, sq_ref, cp_ref, sp_ref, gv_ref, ws_ref, bt_ref, pw_ref, psc_ref,
             snk_ref, o_ref):
        ci = pl.program_id(0)
        zc = zc_ref[...]
        gel, _ = _gelu_parts(zc[:, :512])
        gvv = gv_ref[...]
        btv = bt_ref[...]
        for g in range(4):
            u, _, _, _, _, _, mixed = _gm_group(gel, g, gvv, ws_ref, btv)
            o_ref[:, HEAD * g:HEAD * (g + 1)] = (u * mixed).astype(BF16)
        pp = jnp.where(ci > 0, zpp_ref[...], 0.0)
        pooled, _, _ = _pool_fwd(zc[:, 512:768], pp, ci)
        mp = _dot(pooled.astype(BF16), pw_ref[...].astype(BF16))
        o_ref[:, 256:512] = (mp * psc_ref[...]).astype(BF16)
        qr, kband, vband, valid = _attn_prep(zc, zpkv_ref[...], cq_ref[...], sq_ref[...], cp_ref[...], sp_ref[...], ci)
        snkv = snk_ref[...]
        for h in range(8):
            hk = h // 4
            p, _ = _head_probs(qr[:, HEAD * h:HEAD * (h + 1)].astype(BF16), kband[:, HEAD * hk:HEAD * (hk + 1)],
                               valid, snkv[:, h:h + 1])
            o = _dot(p.astype(BF16), vband[:, HEAD * hk:HEAD * (hk + 1)])
            o_ref[:, 512 + HEAD * h:512 + HEAD * (h + 1)] = o.astype(BF16)

    return pl.pallas_call(
        body, name=name, grid=(nb,), in_specs=specs, out_specs=pl.BlockSpec((CHUNK, 1024), lambda i: (i, 0)),
        out_shape=jax.ShapeDtypeStruct((s, 1024), BF16), compiler_params=_params(("parallel",)),
    )(z, z, z, cosq, sinq, cosq, sinq, gv, ws, bt, pw, psc, snk)


def _mixer_bwd(name, z, dabc, cosq, sinq, gv, ws, bt, pw, psc, snk):
    s = z.shape[0]
    nb = s // CHUNK
    specs, cur = _mixer_specs(nb, True)
    specs = specs + [pl.BlockSpec((CHUNK, 1024), lambda i: (cur(i), 0))]
    full = lambda shape: pl.BlockSpec(shape, lambda i: (0,) * len(shape))
    acc_shapes = [(1, 256), (4, CHUNK, CHUNK), (CHUNK, 4), (256, 256), (1, 256), (1, 8)]

    def body(zc_ref, zpp_ref, zpkv_ref, cq_ref, sq_ref, cp_ref, sp_ref, gv_ref, ws_ref, bt_ref, pw_ref, psc_ref,
             snk_ref, dabc_ref, dz_ref, dgv_ref, dws_ref, dbt_ref, dpw_ref, dpsc_ref, dsnk_ref,
             cpool, ck, cv, dq_s, dkv_s):
        step = pl.program_id(0)
        ci = nb - 1 - step

        @pl.when(step == 0)
        def _():
            for r in (dgv_ref, dws_ref, dbt_ref, dpw_ref, dpsc_ref, dsnk_ref, cpool, ck, cv):
                r[...] = jnp.zeros_like(r)

        zc = zc_ref[...]
        dabc = dabc_ref[...]
        zg = zc[:, :512]
        gel, th = _gelu_parts(zg)
        gp = _gelu_grad(zg, th)
        gvv = gv_ref[...]
        btv = bt_ref[...]
        lane4 = lax.broadcasted_iota(jnp.int32, (CHUNK, 4), 1)
        dbt = jnp.zeros((CHUNK, 4), F32)
        for g in range(4):
            lo, hi = HEAD * g, HEAD * (g + 1)
            u, r, xh, vn, wc, causal, mixed = _gm_group(gel, g, gvv, ws_ref, btv)
            da = dabc[:, lo:hi]
            dm = da * u
            dmb = dm.astype(BF16)
            dws_ref[g] += jnp.where(causal, _dot(dmb, vn, NT), 0.0)
            dbt = dbt + jnp.where(lane4 == g, jnp.sum(dm, axis=-1, keepdims=True), 0.0)
            dvn = _dot(wc, dmb, TN)
            dgv_ref[:, lo:hi] += jnp.sum(dvn * xh, axis=0, keepdims=True)
            dxh = dvn * gvv[:, lo:hi]
            dvg = r * (dxh - xh * jnp.mean(dxh * xh, axis=-1, keepdims=True))
            dz_ref[:, lo:hi] = (da * mixed * gp[:, lo:hi]).astype(BF16)
            dz_ref[:, 256 + lo:256 + hi] = (dvg * gp[:, 256 + lo:256 + hi]).astype(BF16)
        dbt_ref[...] += dbt
        pc = zc[:, 512:768]
        pp = jnp.where(ci > 0, zpp_ref[...], 0.0)
        pooled, cnt, lane = _pool_fwd(pc, pp, ci)
        pwb = pw_ref[...].astype(BF16)
        pooled_b = pooled.astype(BF16)
        mp = _dot(pooled_b, pwb)
        db = dabc[:, 256:512]
        dpsc_ref[...] += jnp.sum(db * mp, axis=0, keepdims=True)
        dmpb = (db * psc_ref[...]).astype(BF16)
        dpw_ref[...] += _dot(pooled_b, dmpb, TN)
        dpooled = _dot(dmpb, pwb, NT)
        davg = dpooled / cnt
        zero = jnp.zeros((CHUNK, 256), F32)
        d2, d4, d8, d16 = [jnp.concatenate([zero, jnp.where((lane >= 64 * k) & (lane < 64 * (k + 1)), davg, 0.0)],
                                           axis=0) for k in range(4)]
        g8 = d8 + d16 + pltpu.roll(d16, 2 * CHUNK - 8, 0)
        g4 = d4 + g8 + pltpu.roll(g8, 2 * CHUNK - 4, 0)
        g2 = d2 + g4 + pltpu.roll(g4, 2 * CHUNK - 2, 0)
        ge = g2 + pltpu.roll(g2, 2 * CHUNK - 1, 0)
        dz_ref[:, 512:768] = (ge[CHUNK:] - dpooled + cpool[...]).astype(BF16)
        cpool[...] = ge[:CHUNK]
        cq = cq_ref[...]
        sq = sq_ref[...]
        qr, kband, vband, valid = _attn_prep(zc, zpkv_ref[...], cq, sq, cp_ref[...], sp_ref[...], ci)
        snkv = snk_ref[...]
        lane8 = lax.broadcasted_iota(jnp.int32, (1, 8), 1)
        dsnk = jnp.zeros((1, 8), F32)
        for hk in range(2):
            kh = kband[:, HEAD * hk:HEAD * (hk + 1)]
            vh = vband[:, HEAD * hk:HEAD * (hk + 1)]
            dkh = jnp.zeros((2 * CHUNK, HEAD), F32)
            dvh = jnp.zeros((2 * CHUNK, HEAD), F32)
            for h in range(4 * hk, 4 * hk + 4):
                qh = qr[:, HEAD * h:HEAD * (h + 1)].astype(BF16)
                p, ps = _head_probs(qh, kh, valid, snkv[:, h:h + 1])
                dob = dabc[:, 512 + HEAD * h:512 + HEAD * (h + 1)].astype(BF16)
                dp = _dot(dob, vh, NT)
                dd = jnp.sum(p * dp, axis=-1, keepdims=True)
                dsnk = dsnk + jnp.where(lane8 == h, jnp.sum(-ps * dd, axis=0, keepdims=True), 0.0)
                dsb = (p * (dp - dd) * (HEAD ** -0.5)).astype(BF16)
                dq_s[:, HEAD * h:HEAD * (h + 1)] = _dot(dsb, kh)
                dkh = dkh + _dot(dsb, qh, TN)
                dvh = dvh + _dot(p.astype(BF16), dob, TN)
            dkv_s[:, HEAD * hk:HEAD * (hk + 1)] = dkh
            dkv_s[:, 128 + HEAD * hk:128 + HEAD * (hk + 1)] = dvh
        dsnk_ref[...] += dsnk
        dqr = dq_s[...]
        dz_ref[:, 768:1280] = (dqr * cq + _rot_half(dqr * sq)).astype(BF16)
        dkv = dkv_s[...]
        dkr = dkv[CHUNK:, :128] + ck[...]
        dz_ref[:, 1280:1408] = (dkr * cq[:, :128] + _rot_half(dkr * sq[:, :128])).astype(BF16)
        dz_ref[:, 1408:1536] = (dkv[CHUNK:, 128:] + cv[...]).astype(BF16)
        ck[...] = dkv[:CHUNK, :128]
        cv[...] = dkv[:CHUNK, 128:]

    return pl.pallas_call(
        body, name=name, grid=(nb,), in_specs=specs,
        out_specs=[pl.BlockSpec((CHUNK, 1536), lambda i: (cur(i), 0))] + [full(a) for a in acc_shapes],
        out_shape=[jax.ShapeDtypeStruct((s, 1536), BF16)] + [jax.ShapeDtypeStruct(a, F32) for a in acc_shapes],
        scratch_shapes=[pltpu.VMEM((CHUNK, 256), F32), pltpu.VMEM((CHUNK, 128), F32), pltpu.VMEM((CHUNK, 128), F32),
                        pltpu.VMEM((CHUNK, 512), F32), pltpu.VMEM((2 * CHUNK, 256), F32)],
        compiler_params=_params(("arbitrary",)),
    )(z, z, z, cosq, sinq, cosq, sinq, gv, ws, bt, pw, psc, snk, dabc)


def _xattn_probs(qh, kh):
    s = _dot(qh, kh, NT) * (256 ** -0.5)
    e = jnp.exp(s - jnp.max(s, axis=-1, keepdims=True))
    return e / jnp.sum(e, axis=-1, keepdims=True)


def _xattn_fwd(name, q, kv, tq):
    s, d = q.shape
    mlen = kv.shape[0]

    def body(q_ref, kv_ref, o_ref):
        for h in range(4):
            lo, hi = 256 * h, 256 * (h + 1)
            p = _xattn_probs(q_ref[:, lo:hi], kv_ref[:, lo:hi])
            o_ref[:, lo:hi] = _dot(p.astype(BF16), kv_ref[:, d + lo:d + hi]).astype(BF16)

    blk = pl.BlockSpec((tq, d), lambda i: (i, 0))
    return pl.pallas_call(body, name=name, grid=(s // tq,),
                          in_specs=[blk, pl.BlockSpec((mlen, 2 * d), lambda i: (0, 0))], out_specs=blk,
                          out_shape=jax.ShapeDtypeStruct((s, d), BF16), compiler_params=_params(("parallel",)))(q, kv)


def _xattn_bwd(name, q, kv, do, tq):
    s, d = q.shape
    mlen = kv.shape[0]

    def body(q_ref, kv_ref, do_ref, dq_ref, dkv_ref):
        @pl.when(pl.program_id(0) == 0)
        def _():
            dkv_ref[...] = jnp.zeros_like(dkv_ref)

        for h in range(4):
            lo, hi = 256 * h, 256 * (h + 1)
            qh = q_ref[:, lo:hi]
            kh = kv_ref[:, lo:hi]
            vh = kv_ref[:, d + lo:d + hi]
            doh = do_ref[:, lo:hi]
            p = _xattn_probs(qh, kh)
            dp = _dot(doh, vh, NT)
            dsb = (p * (dp - jnp.sum(p * dp, axis=-1, keepdims=True)) * (256 ** -0.5)).astype(BF16)
            dq_ref[:, lo:hi] = _dot(dsb, kh).astype(BF16)
            dkv_ref[:, lo:hi] += _dot(dsb, qh, TN)
            dkv_ref[:, d + lo:d + hi] += _dot(p.astype(BF16), doh, TN)

    blk = pl.BlockSpec((tq, d), lambda i: (i, 0))
    kvb = pl.BlockSpec((mlen, 2 * d), lambda i: (0, 0))
    return pl.pallas_call(
        body, name=name, grid=(s // tq,), in_specs=[blk, kvb, blk], out_specs=[blk, kvb],
        out_shape=[jax.ShapeDtypeStruct((s, d), BF16), jax.ShapeDtypeStruct((mlen, 2 * d), F32)],
        compiler_params=_params(("arbitrary",)))(q, kv, do)


def _ffn_up(name, h, wgu, layer, tm, tn):
    s, d = h.shape
    dff = wgu.shape[2] // 2
    nj = dff // tn

    def body(h_ref, wg_ref, wu_ref, g_ref, u_ref, a_ref):
        hv = h_ref[...]
        gate = _dot(hv, wg_ref[...])
        up = _dot(hv, wu_ref[...])
        g_ref[...] = gate.astype(BF16)
        u_ref[...] = up.astype(BF16)
        a_ref[...] = (gate / (1.0 + jnp.exp(-gate)) * up).astype(BF16)

    ob = pl.BlockSpec((tm, tn), lambda j, i: (i, j))
    sd = jax.ShapeDtypeStruct((s, dff), BF16)
    return pl.pallas_call(
        body, name=name, grid=(nj, s // tm),
        in_specs=[pl.BlockSpec((tm, d), lambda j, i: (i, 0)),
                  pl.BlockSpec((None, d, tn), lambda j, i: (layer, 0, j)),
                  pl.BlockSpec((None, d, tn), lambda j, i: (layer, 0, j + nj))],
        out_specs=[ob, ob, ob], out_shape=[sd, sd, sd], compiler_params=_params(("parallel", "parallel")),
    )(h, wgu, wgu)


def _ffn_act_bwd(name, dfn, wdown, layer, gate, up, tm):
    s, d = dfn.shape
    dff = gate.shape[1]

    def body(df_ref, wd_ref, g_ref, u_ref, o_ref):
        dact = _dot(df_ref[...], wd_ref[...], NT)
        gate = g_ref[...].astype(F32)
        sig = 1.0 / (1.0 + jnp.exp(-gate))
        o_ref[:, :dff] = (dact * u_ref[...].astype(F32) * sig * (1.0 + gate * (1.0 - sig))).astype(BF16)
        o_ref[:, dff:] = (dact * gate * sig).astype(BF16)

    gb = pl.BlockSpec((tm, dff), lambda i: (i, 0))
    return pl.pallas_call(
        body, name=name, grid=(s // tm,),
        in_specs=[pl.BlockSpec((tm, d), lambda i: (i, 0)),
                  pl.BlockSpec((None, dff, d), lambda i: (layer, 0, 0)), gb, gb],
        out_specs=pl.BlockSpec((tm, 2 * dff), lambda i: (i, 0)),
        out_shape=jax.ShapeDtypeStruct((s, 2 * dff), BF16), compiler_params=_params(("parallel",)),
    )(dfn, wdown, gate, up)


def _place():
    return lax.axis_index("x"), lax.axis_index("y"), lax.axis_index("c")


def _other_chips(x, y):
    return [(1 - x, y), (x, 1 - y), (1 - x, 1 - y)]


def _region(ref, axis, chip, size):
    start = pl.multiple_of(chip * size, size)
    if axis == 1:
        return ref.at[:, pl.ds(start, size), :]
    return ref.at[:, :, pl.ds(start, size)]


ANY = pl.BlockSpec(memory_space=pl.ANY)


HBM = pl.BlockSpec(memory_space=pltpu.HBM)
SEM = pl.BlockSpec(memory_space=pltpu.SEMAPHORE)
EFFECT = pltpu.SideEffectType.DATAFLOW_SIDE_EFFECTING


def _in_hbm(a):
    return pltpu.with_memory_space_constraint(a, pltpu.HBM)


def _split_start(name, srcs, lands, ncopies, plan, after=None):
    ns, nl = len(srcs), len(lands)
    extra = [] if after is None else [after]

    def body(*refs):
        src, land = refs[:ns], refs[ns:ns + nl]
        send, recv = refs[ns + nl + len(extra)], refs[ns + nl + len(extra) + 1]
        token = refs[-1]
        x, y, c = _place()
        for k, (s_ref, d_ref, peer, _) in enumerate(plan(src, land, x, y, c)):
            pltpu.make_async_remote_copy(src_ref=s_ref, dst_ref=d_ref, send_sem=send.at[k], recv_sem=recv.at[k],
                                         device_id=peer, device_id_type=MESH).start()
        token[...] = jnp.zeros_like(token)

    ops = list(srcs) + list(lands)
    out = pl.pallas_call(
        body, name=name,
        out_shape=(pltpu.SemaphoreType.DMA((ncopies,)), pltpu.SemaphoreType.DMA((ncopies,)),
                   *[pltpu.HBM(a.shape, a.dtype) for a in ops], jax.ShapeDtypeStruct((8, 128), F32)),
        in_specs=(HBM,) * (ns + nl) + (ANY,) * len(extra),
        out_specs=(SEM, SEM) + (HBM,) * (ns + nl) + (pl.BlockSpec(memory_space=pltpu.VMEM),),
        input_output_aliases={i: 2 + i for i in range(ns + nl)},
        compiler_params=pltpu.CompilerParams(has_side_effects=EFFECT),
    )(*[_in_hbm(a) for a in ops], *extra)
    return out[0], out[1], list(out[2:2 + ns]), list(out[2 + ns:2 + ns + nl]), out[-1]


def _split_wait(name, send, recv, srcs, lands, after, plan):
    ns, nl = len(srcs), len(lands)

    def body(*refs):
        src, land = refs[:ns], refs[ns:ns + nl]
        send_ref, recv_ref = refs[ns + nl], refs[ns + nl + 1]
        x, y, c = _place()
        for k, (s_ref, _, _, got) in enumerate(plan(src, land, x, y, c)):
            cp = pltpu.make_async_remote_copy(src_ref=s_ref, dst_ref=got, send_sem=send_ref.at[k],
                                              recv_sem=recv_ref.at[k], device_id=(x, y, c), device_id_type=MESH)
            cp.wait_send()
            cp.wait_recv()

    ops = list(srcs) + list(lands)
    out = pl.pallas_call(
        body, name=name, out_shape=tuple(pltpu.HBM(a.shape, a.dtype) for a in ops),
        in_specs=(HBM,) * (ns + nl) + (SEM, SEM, ANY), out_specs=(HBM,) * (ns + nl),
        input_output_aliases={i: i for i in range(ns + nl)},
        compiler_params=pltpu.CompilerParams(has_side_effects=EFFECT),
    )(*ops, send, recv, after)
    return list(out[:ns]), list(out[ns:])


def _gather_plan(axes, sizes, layer):
    def plan(src, land, x, y, c):
        me = 2 * x + y
        out = []
        for t in range(len(land)):
            mine = _region(land[t], axes[t], me, sizes[t]).at[pl.ds(layer, 1)]
            for px, py in _other_chips(x, y):
                out.append((mine, mine, (px, py, c),
                            _region(land[t], axes[t], 2 * px + py, sizes[t]).at[pl.ds(layer, 1)]))
        return out
    return plan


def _place_own(name, w, axis, chip):
    nl, r, cs = w.shape
    tr = _rows_tile(r)
    nb = r // tr
    full = (nl, 4 * r, cs) if axis == 1 else (nl, r, 4 * cs)

    def body(m_ref, w_ref, o_ref):
        o_ref[...] = w_ref[...].astype(BF16)

    if axis == 1:
        ospec = pl.BlockSpec((None, tr, cs), lambda l, i, m: (l, m[0] * nb + i, 0))
    else:
        ospec = pl.BlockSpec((None, tr, cs), lambda l, i, m: (l, i, m[0]))
    return pl.pallas_call(
        body, name=name,
        grid_spec=pltpu.PrefetchScalarGridSpec(
            num_scalar_prefetch=1, grid=(nl, nb),
            in_specs=[pl.BlockSpec((None, tr, cs), lambda l, i, m: (l, i, 0))], out_specs=ospec),
        out_shape=jax.ShapeDtypeStruct(full, BF16), compiler_params=_params(("parallel", "parallel")),
    )(chip, w)


def _scatter_plan(axes, sizes):
    def plan(src, land, x, y, c):
        out = []
        for t in range(len(src)):
            for k, (px, py) in enumerate(_other_chips(x, y)):
                out.append((_region(src[t], axes[t], 2 * px + py, sizes[t]).at[0], land[t].at[k], (px, py, c),
                            land[t].at[k]))
        return out
    return plan


def _pair_plan(src, land, x, y, c):
    return [(src[t], land[t], (x, y, 1 - c), land[t]) for t in range(len(src))]


def _chip_sum(name, g, slots, axis, chip):
    _, r, cs = slots.shape
    tr = _rows_tile(r)
    nb = r // tr

    def body(m_ref, g_ref, s_ref, o_ref):
        acc = g_ref[...].astype(F32)
        for k in range(3):
            acc = acc + s_ref[k].astype(F32)
        o_ref[...] = acc

    if axis == 1:
        gspec = pl.BlockSpec((tr, cs), lambda i, m: (m[0] * nb + i, 0))
    else:
        gspec = pl.BlockSpec((tr, cs), lambda i, m: (i, m[0]))
    return pl.pallas_call(
        body, name=name,
        grid_spec=pltpu.PrefetchScalarGridSpec(
            num_scalar_prefetch=1, grid=(nb,),
            in_specs=[gspec, pl.BlockSpec((3, tr, cs), lambda i, m: (0, i, 0))],
            out_specs=pl.BlockSpec((tr, cs), lambda i, m: (i, 0))),
        out_shape=jax.ShapeDtypeStruct((r, cs), F32), compiler_params=_params(("parallel",)),
    )(chip, g, slots)


def _pair_adamw(name, mine, theirs, w, m, v, layer, bufs):
    r, cs = mine.shape
    tr = _rows_tile(r)
    c1 = 1.0 - B1 ** STEP
    c2 = 1.0 - B2 ** STEP

    def body(a_ref, b_ref, w_ref, m_ref, v_ref, _g, _d, _m, _v, g_ref, d_ref, mo_ref, vo_ref):
        gv = a_ref[...] + b_ref[...]
        mn = B1 * m_ref[...] + (1.0 - B1) * gv
        vn = B2 * v_ref[...] + (1.0 - B2) * (gv * gv)
        g_ref[...] = gv
        mo_ref[...] = mn
        vo_ref[...] = vn
        d_ref[...] = -LR * ((mn / c1) / (jnp.sqrt(vn / c2) + ADAM_EPS) + WD * w_ref[...])

    blk = pl.BlockSpec((tr, cs), lambda i: (i, 0))
    lay = pl.BlockSpec((None, tr, cs), lambda i: (layer, i, 0))
    return pl.pallas_call(
        body, name=name, grid=(r // tr,), in_specs=[blk, blk, lay, lay, lay] + [ANY] * 4, out_specs=[lay] * 4,
        out_shape=[jax.ShapeDtypeStruct(b.shape, b.dtype) for b in bufs],
        input_output_aliases={5 + i: i for i in range(4)}, compiler_params=_params(("parallel",)),
    )(mine, theirs, w, m, v, *bufs)


def _allreduce_small(p):
    rows = p.shape[0]

    def body(p_ref, o_ref, gath, send, recv):
        x, y, c = _place()
        me = 4 * x + 2 * y + c

        def flip(v, bit):
            return 1 - v if bit else v

        gath[me] = p_ref[...]
        cps = []
        for k in range(1, 8):
            peer = (flip(x, k & 4), flip(y, k & 2), flip(c, k & 1))
            cps.append(pltpu.make_async_remote_copy(src_ref=p_ref, dst_ref=gath.at[me], send_sem=send.at[k - 1],
                                                    recv_sem=recv.at[k - 1], device_id=peer, device_id_type=MESH))
        for cp in cps:
            cp.start()
        for k in range(1, 8):
            slot = gath.at[4 * flip(x, k & 4) + 2 * flip(y, k & 2) + flip(c, k & 1)]
            pltpu.make_async_remote_copy(src_ref=slot, dst_ref=slot, send_sem=send.at[k - 1], recv_sem=recv.at[k - 1],
                                         device_id=(x, y, c), device_id_type=MESH).wait_recv()
        for cp in cps:
            cp.wait_send()
        acc = gath[0]
        for k in range(1, 8):
            acc = acc + gath[k]
        o_ref[...] = acc

    vm = pl.BlockSpec(memory_space=pltpu.VMEM)
    return pl.pallas_call(
        body, name="allreduce_small", in_specs=[vm], out_specs=vm, out_shape=jax.ShapeDtypeStruct(p.shape, F32),
        scratch_shapes=[pltpu.VMEM((8, rows, 128), F32), pltpu.SemaphoreType.DMA((7,)), pltpu.SemaphoreType.DMA((7,))],
        compiler_params=pltpu.CompilerParams(vmem_limit_bytes=VMEM_LIMIT),
    )(p)


def _pack(parts):
    flat = []
    for p in parts:
        v = p.reshape(-1).astype(F32)
        flat.append(jnp.pad(v, (0, (-v.shape[0]) % 128)))
    v = jnp.concatenate(flat)
    v = jnp.pad(v, (0, (-v.shape[0]) % (512 * 128)))
    return v.reshape(-1, 128)


def _unpack(buf, like):
    v = buf.reshape(-1)
    out, off = [], 0
    for p in like:
        nelem = 1
        for s in p.shape:
            nelem *= s
        out.append(v[off:off + nelem].reshape(p.shape))
        off += nelem + (-nelem) % 128
    return out


def kernel(x, mem, positions, mem_norm_g, mix_pre_g, mix_post_g, w_in, gm_v_g, gm_w_s, gm_b_s, pool_w, pool_scale, attn_sinks, w_o, x_pre_g, x_post_g, w_xq, w_xkv, w_xo, ffn_pre_g, ffn_post_g, w_gate_up, w_down, loss_target, m_mem_norm_g, m_mix_pre_g, m_mix_post_g, m_w_in, m_gm_v_g, m_gm_w_s, m_gm_b_s, m_pool_w, m_pool_scale, m_attn_sinks, m_w_o, m_x_pre_g, m_x_post_g, m_w_xq, m_w_xkv, m_w_xo, m_ffn_pre_g, m_ffn_post_g, m_w_gate_up, m_w_down, v_mem_norm_g, v_mix_pre_g, v_mix_post_g, v_w_in, v_gm_v_g, v_gm_w_s, v_gm_b_s, v_pool_w, v_pool_scale, v_attn_sinks, v_w_o, v_x_pre_g, v_x_post_g, v_w_xq, v_w_xkv, v_w_xo, v_ffn_pre_g, v_ffn_post_g, v_w_gate_up, v_w_down):
    args = (x, mem, positions, mem_norm_g, mix_pre_g, mix_post_g, w_in, gm_v_g, gm_w_s, gm_b_s, pool_w, pool_scale, attn_sinks, w_o, x_pre_g, x_post_g, w_xq, w_xkv, w_xo, ffn_pre_g, ffn_post_g, w_gate_up, w_down)
    moms_m = (m_mem_norm_g, m_mix_pre_g, m_mix_post_g, m_w_in, m_gm_v_g, m_gm_w_s, m_gm_b_s, m_pool_w, m_pool_scale, m_attn_sinks, m_w_o, m_x_pre_g, m_x_post_g, m_w_xq, m_w_xkv, m_w_xo, m_ffn_pre_g, m_ffn_post_g, m_w_gate_up, m_w_down)
    moms_v = (v_mem_norm_g, v_mix_pre_g, v_mix_post_g, v_w_in, v_gm_v_g, v_gm_w_s, v_gm_b_s, v_pool_w, v_pool_scale, v_attn_sinks, v_w_o, v_x_pre_g, v_x_post_g, v_w_xq, v_w_xkv, v_w_xo, v_ffn_pre_g, v_ffn_post_g, v_w_gate_up, v_w_down)
    P = dict(zip(NAMES, args))
    P['loss_target'] = loss_target
    M = dict(zip(WEIGHTS, moms_m))
    V = dict(zip(WEIGHTS, moms_v))
    depth = w_in.shape[0]
    nbig = len(BIG)
    axes = [BIG_AXIS[n] for n in BIG]
    sizes = [P[n].shape[a] for n, a in zip(BIG, axes)]
    chip = (2 * lax.axis_index("x") + lax.axis_index("y")).astype(jnp.int32).reshape(1)

    groups = [['w_in'], ['w_o', 'w_xq', 'w_xkv', 'w_xo'], ['w_gate_up', 'w_down']]
    units = [(l, g) for l in range(depth) for g in groups]
    unit_of = {(l, n): i for i, (l, names) in enumerate(units) for n in names}
    ax = lambda names: [BIG_AXIS[n] for n in names]
    sz = lambda names: [P[n].shape[BIG_AXIS[n]] for n in names]

    full = {n: _place_own("place_" + n, P[n], BIG_AXIS[n], chip) for n in BIG}
    gathers = []
    prev, tok = None, None
    for i, (l, names) in enumerate(units):
        send, recv, _, land, tok = _split_start("gather_start%d" % i, [], [full[n] for n in names],
                                                3 * len(names), _gather_plan(ax(names), sz(names), l), after=prev)
        full.update(zip(names, land))
        gathers.append((send, recv))
        prev = land[0]
    P['first_dep'] = tok[:1, :1]
    gathered = set()

    def weights_of(l, names, after):
        i = unit_of[(l, names[0])]
        if i not in gathered:
            _, unames = units[i]
            send, recv = gathers[i]
            _, land = _split_wait("gather_wait%d" % i, send, recv, [], [full[n] for n in unames], after,
                                  _gather_plan(ax(unames), sz(unames), l))
            full.update(zip(unames, land))
            gathered.add(i)
        return {n: (full[n], l) for n in names}

    outs = {n: [lax.empty(P[n].shape, F32) for _ in range(4)] for n in BIG}
    gunits = [(l, BIG) for l in range(depth - 1, 0, -1)] + [
        (0, g) for g in (['w_gate_up', 'w_down'], ['w_xq', 'w_xkv', 'w_xo'], ['w_o'], ['w_in'])]
    collected, scatters, pairs = {}, {}, {}

    def finish_scatter(i, after):
        _, names = gunits[i]
        send, recv, g_l, slots = scatters.pop(i)
        g_l, slots = _split_wait("scatter_wait%d" % i, send, recv, g_l, slots, after,
                                 _scatter_plan(ax(names), sz(names)))
        mine = [_chip_sum("chip_sum_" + n, g.reshape(g.shape[1:]), sl, BIG_AXIS[n], chip)
                for n, g, sl in zip(names, g_l, slots)]
        send, recv, mine, theirs, tok = _split_start("pair_start%d" % i, mine, [lax.empty(a.shape, F32) for a in mine],
                                                     len(names), _pair_plan)
        pairs[i] = (send, recv, mine, theirs)
        return tok[:1, :1]

    def finish_pair(i, after):
        l, names = gunits[i]
        send, recv, mine, theirs = pairs.pop(i)
        mine, theirs = _split_wait("pair_wait%d" % i, send, recv, mine, theirs, after, _pair_plan)
        for n, a, b in zip(names, mine, theirs):
            outs[n] = _pair_adamw("adamw_" + n, a, b, P[n], M[n], V[n], l, outs[n])

    def grads_of(l, g_part, after):
        collected.update({(l, n): g for n, g in g_part.items()})
        tok = jnp.zeros((1, 1), F32)
        for i, (ul, names) in enumerate(gunits):
            if ul != l or ('started', i) in collected or any((l, n) not in collected for n in names):
                continue
            collected[('started', i)] = True
            srcs = [collected[(l, n)].reshape((1,) + collected[(l, n)].shape) for n in names]
            send, recv, srcs, slots, t = _split_start("scatter_start%d" % i, srcs,
                                                      [lax.empty((3,) + P[n].shape[1:], BF16) for n in names],
                                                      3 * len(names), _scatter_plan(ax(names), sz(names)))
            scatters[i] = (send, recv, srcs, slots)
            tok = tok + t[:1, :1]
            if i >= 1:
                tok = tok + finish_scatter(i - 1, after)
            if i >= 2:
                finish_pair(i - 2, after)
        return tok

    loss_part, dx, small_g = _fwd_bwd(P, weights_of, grads_of)
    loss = lax.psum(loss_part[0, 0], ("x", "y", "c"))
    grad_x = dx.reshape(x.shape)

    small_like = [P[n] for n in SMALL]
    gsum = _allreduce_small(_pack(small_g))
    dlt, mn, vn = _adamw("adamw_small", gsum, _pack(small_like), _pack([M[n] for n in SMALL]),
                         _pack([V[n] for n in SMALL]), 512)
    grads, deltas, new_m, new_v = {}, {}, {}, {}
    for name_map, buf in ((grads, gsum), (deltas, dlt), (new_m, mn), (new_v, vn)):
        for n, a in zip(SMALL, _unpack(buf, small_like)):
            name_map[n] = a

    last = len(gunits) - 1
    finish_scatter(last, dlt)
    finish_pair(last - 1, dlt)
    finish_pair(last, dlt)
    for n in BIG:
        grads[n], deltas[n], new_m[n], new_v[n] = outs[n]

    return (loss, grad_x, *[grads[n] for n in WEIGHTS], *[deltas[n] for n in WEIGHTS],
            *[new_m[n] for n in WEIGHTS], *[new_v[n] for n in WEIGHTS])


def _fwd_bwd(P, weights_of, grads_of):
    (x, mem, positions, mem_norm_g, mix_pre_g, mix_post_g, w_in, gm_v_g, gm_w_s, gm_b_s, pool_w, pool_scale, attn_sinks,
     w_o, x_pre_g, x_post_g, w_xq, w_xkv, w_xo, ffn_pre_g, ffn_post_g, w_gate_up, w_down) = [P[n] for n in NAMES]
    x0 = x[0]
    s, d = x0.shape
    depth = w_in.shape[0]
    tgt = P['loss_target'][0]
    tmn = 256
    tmr = min(512, s)

    half = HEAD // 2
    inv = ROPE_THETA ** (-jnp.arange(half, dtype=F32) / half)
    ang = positions[0].astype(F32)[:, None] * inv
    cos, sin = jnp.cos(ang), jnp.sin(ang)
    cosq = jnp.tile(jnp.concatenate([cos, cos], axis=-1), (1, 8))
    sinq = jnp.tile(jnp.concatenate([-sin, sin], axis=-1), (1, 8))

    row = lambda a, l: a[l].reshape(1, -1)
    memn = _prenorm("mem_norm", mem[0], mem_norm_g.reshape(1, d), tmn)
    pw_bd = []
    for l in range(depth):
        bd = jnp.zeros((256, 256), F32)
        for g in range(4):
            bd = lax.dynamic_update_slice(bd, pool_w[l, g], (64 * g, 64 * g))
        pw_bd.append(bd)

    saved = []
    xc = x0
    h = _prenorm("pre_norm0", x0, row(mix_pre_g, 0) + P['first_dep'], tmn)
    for l in range(depth):
        W = weights_of(l, ['w_in'], xc)
        sv = {'x0': xc, 'h1': h}
        z = _mm_nn("fwd_w_in", h, *W['w_in'], tm=1024, tn=512, tk=d, out_dtype=F32)
        abc = _mixer_fwd("mixer_fwd", z, cosq, sinq, row(gm_v_g, l), gm_w_s[l], gm_b_s[l].T, pw_bd[l],
                         row(pool_scale, l), row(attn_sinks, l))
        W.update(weights_of(l, ['w_o'], z))
        mix, xc, h = _mm_rows("fwd_w_o", abc, *W['w_o'], 'nn', tm=tmr, tk=d, rows_in=[xc],
                              params=[row(mix_post_g, l), row(x_pre_g, l)], rows_out=[BF16, F32, BF16],
                              epilogue=_post_pre_rows)
        sv.update(z=z, abc=abc, mix=mix, x1=xc, h2=h)
        W.update(weights_of(l, ['w_xq', 'w_xkv', 'w_xo'], xc))
        q = _mm_nn("fwd_w_xq", h, *W['w_xq'], tm=1024, tn=512, tk=d, out_dtype=BF16)
        kv = _mm_nn("fwd_w_xkv", memn, *W['w_xkv'], tm=256, tn=512, tk=d, out_dtype=BF16)
        o = _xattn_fwd("xattn_fwd", q, kv, 512)
        xo, xc, h = _mm_rows("fwd_w_xo", o, *W['w_xo'], 'nn', tm=tmr, tk=d, rows_in=[xc],
                             params=[row(x_post_g, l), row(ffn_pre_g, l)], rows_out=[BF16, F32, BF16],
                             epilogue=_post_pre_rows)
        sv.update(q=q, kv=kv, o=o, xo=xo, x2=xc, h3=h)
        W.update(weights_of(l, ['w_gate_up', 'w_down'], xc))
        dff = W['w_down'][0].shape[1]
        gate, up, act = _ffn_up("ffn_up", h, *W['w_gate_up'], 512, dff // 2)
        sv.update(gate=gate, up=up, act=act)
        if l + 1 < depth:
            f, xc, h = _mm_rows("fwd_w_down", act, *W['w_down'], 'nn', tm=tmr, tk=dff // 2, rows_in=[xc],
                                params=[row(ffn_post_g, l), row(mix_pre_g, l + 1)], rows_out=[BF16, F32, BF16],
                                epilogue=_post_pre_rows)
            sv.update(f=f)
        saved.append(sv)
    gs = {n: [None] * depth for n in SMALL if n != 'mem_norm_g'}
    dx, dfn, gs['ffn_post_g'][depth - 1], loss_part = _mm_rows(
        "fwd_w_down_loss", saved[-1]['act'], *W['w_down'], 'nn', tm=tmr, tk=dff // 2, rows_in=[xc, tgt],
        params=[row(ffn_post_g, depth - 1)], rows_out=[F32, BF16], n_sums=2, epilogue=_make_loss_rows(d))

    dmemn = None
    tok = jnp.zeros((1, 1), F32)
    for l in reversed(range(depth)):
        sv, W, G = saved[l], weights_of(l, BIG, dx), {}
        G['w_down'] = _mm_tn("dw_down", sv['act'], dfn, tm=dff // 2, tn=d, tk=512)
        dgu = _ffn_act_bwd("ffn_act_bwd", dfn, *W['w_down'], sv['gate'], sv['up'], 256)
        G['w_gate_up'] = _mm_tn("dw_gate_up", sv['h3'], dgu, tm=d, tn=dff // 2, tk=512)
        dx, dxo, gs['ffn_pre_g'][l], gs['x_post_g'][l] = _mm_rows(
            "bwd_w_gate_up", dgu, *W['w_gate_up'], 'nt', tm=tmr, tk=dff // 2, rows_in=[sv['x2'], dx, sv['xo']],
            params=[row(ffn_pre_g, l) + tok, row(x_post_g, l)], rows_out=[F32, BF16], n_sums=2, epilogue=_bwd_rows)
        tok = grads_of(l, {n: G[n] for n in ('w_gate_up', 'w_down')}, dx)
        G['w_xo'] = _mm_tn("dw_xo", sv['o'], dxo, tm=d, tn=d, tk=512)
        do = _mm_nt("bwd_w_xo", dxo, *W['w_xo'], tm=1024, tn=512, tk=d, out_dtype=BF16, dep=tok)
        dq, dkv = _xattn_bwd("xattn_bwd", sv['q'], sv['kv'], do, 512)
        dkv = dkv.astype(BF16)
        G['w_xkv'] = _mm_tn("dw_xkv", memn, dkv, tm=d, tn=d, tk=mem.shape[1])
        dmemn = _mm_nt("bwd_w_xkv", dkv, *W['w_xkv'], tm=mem.shape[1], tn=512, tk=2 * d, out_dtype=F32, add=dmemn)
        G['w_xq'] = _mm_tn("dw_xq", sv['h2'], dq, tm=d, tn=d, tk=512)
        dx, dmix, gs['x_pre_g'][l], gs['mix_post_g'][l] = _mm_rows(
            "bwd_w_xq", dq, *W['w_xq'], 'nt', tm=tmr, tk=d, rows_in=[sv['x1'], dx, sv['mix']],
            params=[row(x_pre_g, l), row(mix_post_g, l)], rows_out=[F32, BF16], n_sums=2, epilogue=_bwd_rows)
        tok = grads_of(l, {n: G[n] for n in ('w_xq', 'w_xkv', 'w_xo')}, dx)
        G['w_o'] = _mm_tn("dw_o", sv['abc'], dmix, tm=d, tn=d, tk=512)
        dabc = _mm_nt("bwd_w_o", dmix, *W['w_o'], tm=1024, tn=512, tk=d, out_dtype=F32, dep=tok)
        tok = grads_of(l, {'w_o': G['w_o']}, dabc)
        dz, dgv, dws, dbt, dpw, dpsc, dsnk = _mixer_bwd(
            "mixer_bwd", sv['z'], dabc, cosq, sinq, row(gm_v_g, l) + tok, gm_w_s[l], gm_b_s[l].T, pw_bd[l],
            row(pool_scale, l), row(attn_sinks, l))
        gs['gm_v_g'][l] = dgv
        gs['gm_w_s'][l] = dws
        gs['gm_b_s'][l] = dbt.T
        gs['pool_w'][l] = jnp.stack([dpw[64 * g:64 * (g + 1), 64 * g:64 * (g + 1)] for g in range(4)])
        gs['pool_scale'][l] = dpsc
        gs['attn_sinks'][l] = dsnk
        G['w_in'] = _mm_tn("dw_in", sv['h1'], dz, tm=d, tn=dz.shape[1], tk=512)
        if l > 0:
            dx, dfn, gs['mix_pre_g'][l], gs['ffn_post_g'][l - 1] = _mm_rows(
                "bwd_w_in", dz, *W['w_in'], 'nt', tm=tmr, tk=dz.shape[1], rows_in=[sv['x0'], dx, saved[l - 1]['f']],
                params=[row(mix_pre_g, l), row(ffn_post_g, l - 1)], rows_out=[F32, BF16], n_sums=2,
                epilogue=_bwd_rows)
        else:
            dx, gs['mix_pre_g'][l] = _mm_rows(
                "bwd_w_in_first", dz, *W['w_in'], 'nt', tm=tmr, tk=dz.shape[1], rows_in=[sv['x0'], dx],
                params=[row(mix_pre_g, l)], rows_out=[F32], n_sums=1, epilogue=_bwd_rows_first)
        tok = grads_of(l, {'w_in': G['w_in']}, dx)
    _, dg_mem = _norm_bwd("bwd_mem_norm", mem[0], mem_norm_g.reshape(1, d), dmemn, None, BF16, tmn)
    small_g = []
    for n in SMALL:
        if n == 'mem_norm_g':
            small_g.append(dg_mem.reshape(P[n].shape))
        else:
            small_g.append(jnp.stack([a.reshape(P[n].shape[1:]) for a in gs[n]]))
    return loss_part, dx, small_g
```

```python
import functools

import jax
import jax.numpy as jnp
from jax import lax
from jax.experimental import pallas as pl
from jax.experimental.pallas import tpu as pltpu

F32 = jnp.float32
BF16 = jnp.bfloat16
EPS = 1e-6
CHUNK = 128
HEAD = 64
ROPE_THETA = 10000.0
POOL_WINDOWS = (2, 4, 8, 16)
LR, B1, B2, ADAM_EPS, WD, STEP = 0.001, 0.9, 0.999, 1e-08, 0.01, 10
MESH = pl.DeviceIdType.MESH
VMEM_LIMIT = 56 * 1024 * 1024

NAMES = ['x', 'mem', 'positions', 'mem_norm_g', 'mix_pre_g', 'mix_post_g', 'w_in', 'gm_v_g', 'gm_w_s', 'gm_b_s',
         'pool_w', 'pool_scale', 'attn_sinks', 'w_o', 'x_pre_g', 'x_post_g', 'w_xq', 'w_xkv', 'w_xo', 'ffn_pre_g',
         'ffn_post_g', 'w_gate_up', 'w_down']
WEIGHTS = NAMES[3:]
BIG = ['w_in', 'w_o', 'w_xq', 'w_xkv', 'w_xo', 'w_gate_up', 'w_down']
BIG_AXIS = {'w_in': 2, 'w_o': 1, 'w_xq': 1, 'w_xkv': 2, 'w_xo': 1, 'w_gate_up': 2, 'w_down': 1}
SMALL = [n for n in WEIGHTS if n not in BIG]

NN = (((1,), (0,)), ((), ()))
NT = (((1,), (1,)), ((), ()))
TN = (((0,), (0,)), ((), ()))


def _dot(a, b, dims=NN):
    return lax.dot_general(a, b, dims, preferred_element_type=F32)


def _params(sem):
    return pltpu.CompilerParams(dimension_semantics=sem, vmem_limit_bytes=VMEM_LIMIT)


def _rows_tile(rows, limit=256):
    return max(t for t in range(16, limit + 1, 16) if rows % t == 0)


def _mm(name, a, a_spec, b, b_spec, dims, grid, nk, out_shape, out_spec, add=None, add_spec=None, dep=None):
    acc_shape = out_spec.block_shape

    def body(*refs):
        a_ref, b_ref = refs[0], refs[1]
        pos = 2
        add_ref = None
        if add is not None:
            add_ref = refs[pos]
            pos += 1
        if dep is not None:
            pos += 1
        o_ref = refs[pos]
        part = _dot(a_ref[...].astype(BF16), b_ref[...].astype(BF16), dims)
        if nk == 1:
            if add_ref is not None:
                part = part + add_ref[...]
            o_ref[...] = part.astype(o_ref.dtype)
        else:
            acc_ref = refs[pos + 1]
            k = pl.program_id(2)

            @pl.when(k == 0)
            def _():
                acc_ref[...] = part if add_ref is None else part + add_ref[...]

            @pl.when(k > 0)
            def _():
                acc_ref[...] += part

            @pl.when(k == nk - 1)
            def _():
                o_ref[...] = acc_ref[...].astype(o_ref.dtype)

    ops, specs = [a, b], [a_spec, b_spec]
    if add is not None:
        ops.append(add)
        specs.append(add_spec)
    if dep is not None:
        ops.append(dep)
        specs.append(pl.BlockSpec((1, 1), lambda i, j, k: (0, 0)))
    return pl.pallas_call(
        body, name=name, grid=grid, in_specs=specs, out_specs=out_spec, out_shape=out_shape,
        scratch_shapes=[pltpu.VMEM(acc_shape, F32)] if nk > 1 else [],
        compiler_params=_params(("parallel", "parallel", "arbitrary")),
    )(*ops)


def _wspec(block, layer, fn):
    return pl.BlockSpec((None,) + block, lambda i, j, k: (layer,) + fn(i, j, k))


def _mm_nn(name, a, w, layer, *, tm, tn, tk, out_dtype, n0=0, n=None, k0=0):
    m, kk = a.shape
    n = w.shape[2] if n is None else n
    tm = min(tm, m)
    nk = kk // tk
    return _mm(name, a, pl.BlockSpec((tm, tk), lambda i, j, k: (i, k)),
               w, _wspec((tk, tn), layer, lambda i, j, k: (k + k0 // tk, j + n0 // tn)), NN,
               (m // tm, n // tn, nk), nk, jax.ShapeDtypeStruct((m, n), out_dtype),
               pl.BlockSpec((tm, tn), lambda i, j, k: (i, j)))


def _mm_nt(name, a, w, layer, *, tm, tn, tk, out_dtype, k0=0, add=None, dep=None):
    m, kk = a.shape
    n = w.shape[1]
    tm = min(tm, m)
    nk = kk // tk
    ospec = pl.BlockSpec((tm, tn), lambda i, j, k: (i, j))
    return _mm(name, a, pl.BlockSpec((tm, tk), lambda i, j, k: (i, k)),
               w, _wspec((tn, tk), layer, lambda i, j, k: (j, k + k0 // tk)), NT,
               (m // tm, n // tn, nk), nk, jax.ShapeDtypeStruct((m, n), out_dtype), ospec,
               add=add, add_spec=ospec if add is not None else None, dep=dep)


def _mm_tn(name, a, b, *, tm, tn, tk):
    kk, m = a.shape
    n = b.shape[1]
    tk = min(tk, kk)
    nk = kk // tk
    return _mm(name, a, pl.BlockSpec((tk, tm), lambda i, j, k: (k, i)),
               b, pl.BlockSpec((tk, tn), lambda i, j, k: (k, j)), TN,
               (m // tm, n // tn, nk), nk, jax.ShapeDtypeStruct((m, n), BF16),
               pl.BlockSpec((tm, tn), lambda i, j, k: (i, j)))


def _mm_rows(name, a, w, layer, mode, *, tm, tk, k0=0, rows_in=(), params=(), rows_out=(), n_sums=0, epilogue):
    m, kk = a.shape
    n = w.shape[2] if mode == 'nn' else w.shape[1]
    nk = kk // tk
    nr, npar, no = len(rows_in), len(params), len(rows_out)

    def body(*refs):
        a_ref, w_ref = refs[0], refs[1]
        rin = refs[2:2 + nr]
        par = refs[2 + nr:2 + nr + npar]
        outs = refs[2 + nr + npar:2 + nr + npar + no]
        sums = refs[2 + nr + npar + no:2 + nr + npar + no + n_sums]
        i, k = pl.program_id(0), pl.program_id(1)
        part = _dot(a_ref[...], w_ref[...], NN if mode == 'nn' else NT)

        def finish(acc):
            res, sm = epilogue(acc, [r[...] for r in rin], [p[...] for p in par])
            for r, v in zip(outs, res):
                r[...] = v.astype(r.dtype)

            @pl.when(i == 0)
            def _():
                for r in sums:
                    r[...] = jnp.zeros_like(r)

            for r, v in zip(sums, sm):
                r[...] += v

        if nk == 1:
            finish(part)
        else:
            acc_ref = refs[-1]

            @pl.when(k == 0)
            def _():
                acc_ref[...] = part

            @pl.when(k > 0)
            def _():
                acc_ref[...] += part

            @pl.when(k == nk - 1)
            def _():
                finish(acc_ref[...])

    if mode == 'nn':
        wspec = pl.BlockSpec((None, tk, n), lambda i, k: (layer, k + k0 // tk, 0))
    else:
        wspec = pl.BlockSpec((None, n, tk), lambda i, k: (layer, 0, k + k0 // tk))
    rowblk = pl.BlockSpec((tm, n), lambda i, k: (i, 0))
    one = pl.BlockSpec((1, n), lambda i, k: (0, 0))
    return pl.pallas_call(
        body, name=name, grid=(m // tm, nk),
        in_specs=[pl.BlockSpec((tm, tk), lambda i, k: (i, k)), wspec] + [rowblk] * nr + [one] * npar,
        out_specs=[rowblk] * no + [one] * n_sums,
        out_shape=[jax.ShapeDtypeStruct((m, n), dt) for dt in rows_out] +
                  [jax.ShapeDtypeStruct((1, n), F32)] * n_sums,
        scratch_shapes=[pltpu.VMEM((tm, n), F32)] if nk > 1 else [],
        compiler_params=_params(("arbitrary", "arbitrary")),
    )(a, w, *rows_in, *params)


def _rstd(x):
    return lax.rsqrt(jnp.mean(x * x, axis=-1, keepdims=True) + EPS)


def _norm_back(xin, g, dy):
    r = _rstd(xin)
    xh = xin * r
    dyg = dy * g
    return r * (dyg - xh * jnp.mean(dyg * xh, axis=-1, keepdims=True)), jnp.sum(dy * xh, axis=0, keepdims=True)


def _post_pre_rows(y, rows, pars):
    xn = rows[0] + y * _rstd(y) * pars[0]
    return [y, xn, xn * _rstd(xn) * pars[1]], []


def _make_loss_rows(d):
    def fn(y, rows, pars):
        x, tgt = rows
        err = x + y * _rstd(y) * pars[0] - tgt
        dout = err * (1.0 / d)
        dy, dg = _norm_back(y, pars[0], dout)
        lsum = 0.5 * jnp.sum(jnp.mean(err * err, axis=-1, keepdims=True), axis=0, keepdims=True)
        return [dout, dy], [dg, jnp.broadcast_to(lsum, dg.shape)]
    return fn


def _bwd_rows(dh, rows, pars):
    xin, resid, yprev = rows
    dxa, dg_pre = _norm_back(xin, pars[0], dh)
    dx = resid + dxa
    dyp, dg_post = _norm_back(yprev.astype(F32), pars[1], dx)
    return [dx, dyp], [dg_pre, dg_post]


def _bwd_rows_first(dh, rows, pars):
    xin, resid = rows
    dxa, dg_pre = _norm_back(xin, pars[0], dh)
    return [resid + dxa], [dg_pre]


def _row(d):
    return pl.BlockSpec((1, d), lambda i: (0, 0))


def _prenorm(name, x, g, tm):
    m, d = x.shape

    def body(x_ref, g_ref, o_ref):
        xv = x_ref[...]
        o_ref[...] = (xv * _rstd(xv) * g_ref[...]).astype(BF16)

    blk = pl.BlockSpec((tm, d), lambda i: (i, 0))
    return pl.pallas_call(body, name=name, grid=(m // tm,), in_specs=[blk, _row(d)], out_specs=blk,
                          out_shape=jax.ShapeDtypeStruct((m, d), BF16), compiler_params=_params(("parallel",)))(x, g)


def _norm_bwd(name, xin, g, dy, resid, out_dtype, tm):
    m, d = xin.shape

    def body(*refs):
        if resid is None:
            x_ref, g_ref, dy_ref, dx_ref, dg_ref = refs
        else:
            x_ref, g_ref, dy_ref, r_ref, dx_ref, dg_ref = refs
        xv = x_ref[...]
        r = _rstd(xv)
        xh = xv * r
        dyv = dy_ref[...].astype(F32)
        dyg = dyv * g_ref[...]
        dx = r * (dyg - xh * jnp.mean(dyg * xh, axis=-1, keepdims=True))
        if resid is not None:
            dx = dx + r_ref[...]
        dx_ref[...] = dx.astype(dx_ref.dtype)

        @pl.when(pl.program_id(0) == 0)
        def _():
            dg_ref[...] = jnp.zeros_like(dg_ref)

        dg_ref[...] += jnp.sum(dyv * xh, axis=0, keepdims=True)

    blk = pl.BlockSpec((tm, d), lambda i: (i, 0))
    ops = [xin, g, dy] + ([] if resid is None else [resid])
    specs = [blk, _row(d), blk] + ([] if resid is None else [blk])
    return pl.pallas_call(
        body, name=name, grid=(m // tm,), in_specs=specs, out_specs=[blk, _row(d)],
        out_shape=[jax.ShapeDtypeStruct((m, d), out_dtype), jax.ShapeDtypeStruct((1, d), F32)],
        compiler_params=_params(("arbitrary",)))(*ops)


def _adamw(name, g, w, m, v, tr):
    rows, cols = g.shape
    c1 = 1.0 - B1 ** STEP
    c2 = 1.0 - B2 ** STEP

    def body(g_ref, w_ref, m_ref, v_ref, d_ref, mo_ref, vo_ref):
        gv = g_ref[...]
        mn = B1 * m_ref[...] + (1.0 - B1) * gv
        vn = B2 * v_ref[...] + (1.0 - B2) * (gv * gv)
        mo_ref[...] = mn
        vo_ref[...] = vn
        d_ref[...] = -LR * ((mn / c1) / (jnp.sqrt(vn / c2) + ADAM_EPS) + WD * w_ref[...])

    blk = pl.BlockSpec((tr, cols), lambda i: (i, 0))
    sd = jax.ShapeDtypeStruct((rows, cols), F32)
    return pl.pallas_call(body, name=name, grid=(rows // tr,), in_specs=[blk] * 4, out_specs=[blk] * 3,
                          out_shape=[sd, sd, sd], compiler_params=_params(("parallel",)))(g, w, m, v)


def _gelu_parts(x):
    c = 0.7978845608028654
    t = jnp.tanh(c * (x + 0.044715 * (x * x * x)))
    return 0.5 * x * (1.0 + t), t


def _gelu_grad(x, t):
    c = 0.7978845608028654
    return 0.5 * (1.0 + t) + 0.5 * x * (1.0 - t * t) * (c * (1.0 + 3.0 * 0.044715 * x * x))


def _rot_half(x):
    ax = x.ndim - 1
    w = x.shape[ax]
    lane = lax.broadcasted_iota(jnp.int32, x.shape, ax)
    return jnp.where((lane & 63) < 32, pltpu.roll(x, w - 32, ax), pltpu.roll(x, 32, ax))


def _gm_group(gel, g, gv, ws_ref, bt):
    u = gel[:, HEAD * g:HEAD * (g + 1)]
    vg = gel[:, 256 + HEAD * g:256 + HEAD * (g + 1)]
    r = _rstd(vg)
    xh = vg * r
    vn = (xh * gv[:, HEAD * g:HEAD * (g + 1)]).astype(BF16)
    row = lax.broadcasted_iota(jnp.int32, (CHUNK, CHUNK), 0)
    col = lax.broadcasted_iota(jnp.int32, (CHUNK, CHUNK), 1)
    causal = col <= row
    wc = jnp.where(causal, ws_ref[g], 0.0).astype(BF16)
    mixed = _dot(wc, vn) + bt[:, g:g + 1]
    return u, r, xh, vn, wc, causal, mixed


def _lane_select(lane, vals):
    return jnp.where(lane < 64, vals[0], jnp.where(lane < 128, vals[1], jnp.where(lane < 192, vals[2], vals[3])))


def _pool_fwd(pc, pp, ci):
    ext = jnp.concatenate([pp, pc], axis=0)
    s2 = ext + pltpu.roll(ext, 1, 0)
    s4 = s2 + pltpu.roll(s2, 2, 0)
    s8 = s4 + pltpu.roll(s4, 4, 0)
    s16 = s8 + pltpu.roll(s8, 8, 0)
    t1 = ci * CHUNK + lax.broadcasted_iota(jnp.int32, (CHUNK, 1), 0) + 1
    lane = lax.broadcasted_iota(jnp.int32, (1, 256), 1)
    cnt = _lane_select(lane, [jnp.minimum(t1, w).astype(F32) for w in POOL_WINDOWS])
    ssel = _lane_select(lane, [s[CHUNK:] for s in (s2, s4, s8, s16)])
    return ssel / cnt - pc, cnt, lane


def _attn_prep(zc, zpkv, cq, sq, cp, sp, ci):
    q = zc[:, 768:1280]
    kc = zc[:, 1280:1408]
    vc = zc[:, 1408:1536]
    kp = zpkv[:, :128]
    vp = zpkv[:, 128:]
    qr = q * cq + _rot_half(q) * sq
    krc = kc * cq[:, :128] + _rot_half(kc) * sq[:, :128]
    krp = kp * cp + _rot_half(kp) * sp
    kband = jnp.concatenate([krp, krc], axis=0)
    vband = jnp.concatenate([vp, vc], axis=0)
    key = lax.broadcasted_iota(jnp.int32, (2 * CHUNK, 4 * CHUNK), 0)
    t = lax.broadcasted_iota(jnp.int32, (2 * CHUNK, 4 * CHUNK), 1) & (CHUNK - 1)
    valid = ((key < CHUNK) & (key > t) & (ci > 0)) | ((key >= CHUNK) & (key - CHUNK <= t))
    return qr, kband, vband, valid


def _stack_heads(x, base, hk):
    return jnp.concatenate([x[:, base + HEAD * (4 * hk + i):base + HEAD * (4 * hk + i + 1)] for i in range(4)], axis=0)


def _sink_row(snk, hk):
    lane = lax.broadcasted_iota(jnp.int32, (1, 4 * CHUNK), 1)
    s = [snk[:, 4 * hk + i:4 * hk + i + 1] for i in range(4)]
    return jnp.where(lane < CHUNK, s[0], jnp.where(lane < 2 * CHUNK, s[1], jnp.where(lane < 3 * CHUNK, s[2], s[3])))


def _group_probs(kh, q4, valid, sink4):
    s = _dot(kh, q4, NT) * (HEAD ** -0.5)
    s = jnp.where(valid, s, -1e30)
    mx = jnp.maximum(jnp.max(s, axis=0, keepdims=True), sink4)
    e = jnp.exp(s - mx)
    es = jnp.exp(sink4 - mx)
    inv = 1.0 / (jnp.sum(e, axis=0, keepdims=True) + es)
    return e * inv, es * inv


def _mixer_specs(nb, rev):
    def cur(i):
        return nb - 1 - i if rev else i

    def prev(i):
        return jnp.maximum(cur(i) - 1, 0)

    full = lambda shape: pl.BlockSpec(shape, lambda i: (0,) * len(shape))
    specs = [
        pl.BlockSpec((CHUNK, 1536), lambda i: (cur(i), 0)),
        pl.BlockSpec((CHUNK, 256), lambda i: (prev(i), 2)),
        pl.BlockSpec((CHUNK, 256), lambda i: (prev(i), 5)),
        pl.BlockSpec((CHUNK, 512), lambda i: (cur(i), 0)),
        pl.BlockSpec((CHUNK, 512), lambda i: (cur(i), 0)),
        pl.BlockSpec((CHUNK, 128), lambda i: (prev(i), 0)),
        pl.BlockSpec((CHUNK, 128), lambda i: (prev(i), 0)),
        full((1, 256)), full((4, CHUNK, CHUNK)), full((CHUNK, 4)), full((256, 256)), full((1, 256)), full((1, 8)),
    ]
    return specs, cur


def _mixer_fwd(name, z, cosq, sinq, gv, ws, bt, pw, psc, snk):
    s = z.shape[0]
    nb = s // CHUNK
    specs, _ = _mixer_specs(nb, False)

    def body(zc_ref, zpp_ref, zpkv_ref, cq_ref, sq_ref, cp_ref, sp_ref, gv_ref, ws_ref, bt_ref, pw_ref, psc_ref,
             snk_ref, o_ref):
        ci = pl.program_id(0)
        zc = zc_ref[...]
        gel, _ = _gelu_parts(zc[:, :512])
        gvv = gv_ref[...]
        btv = bt_ref[...]
        for g in range(4):
            u, _, _, _, _, _, mixed = _gm_group(gel, g, gvv, ws_ref, btv)
            o_ref[:, HEAD * g:HEAD * (g + 1)] = (u * mixed).astype(BF16)
        pp = jnp.where(ci > 0, zpp_ref[...], 0.0)
        pooled, _, _ = _pool_fwd(zc[:, 512:768], pp, ci)
        mp = _dot(pooled.astype(BF16), pw_ref[...].astype(BF16))
        o_ref[:, 256:512] = (mp * psc_ref[...]).astype(BF16)
        qr, kband, vband, valid = _attn_prep(zc, zpkv_ref[...], cq_ref[...], sq_ref[...], cp_ref[...], sp_ref[...], ci)
        snkv = snk_ref[...]
        kb = kband.astype(BF16)
        vt = vband.T
        ots = []
        for hk in range(2):
            q4 = _stack_heads(qr, 0, hk).astype(BF16)
            p, _ = _group_probs(kb[:, HEAD * hk:HEAD * (hk + 1)], q4, valid, _sink_row(snkv, hk))
            ots.append(_dot(vt[HEAD * hk:HEAD * (hk + 1), :].astype(BF16), p.astype(BF16)))
        o = jnp.concatenate(ots, axis=0).T
        for hk in range(2):
            for i in range(4):
                h = 4 * hk + i
                o_ref[:, 512 + HEAD * h:512 + HEAD * (h + 1)] = o[CHUNK * i:CHUNK * (i + 1),
                                                                  HEAD * hk:HEAD * (hk + 1)].astype(BF16)

    return pl.pallas_call(
        body, name=name, grid=(nb,), in_specs=specs, out_specs=pl.BlockSpec((CHUNK, 1024), lambda i: (i, 0)),
        out_shape=jax.ShapeDtypeStruct((s, 1024), BF16), compiler_params=_params(("parallel",)),
    )(z, z, z, cosq, sinq, cosq, sinq, gv, ws, bt, pw, psc, snk)


def _mixer_bwd(name, z, dabc, cosq, sinq, gv, ws, bt, pw, psc, snk):
    s = z.shape[0]
    nb = s // CHUNK
    specs, cur = _mixer_specs(nb, True)
    specs = specs + [pl.BlockSpec((CHUNK, 1024), lambda i: (cur(i), 0))]
    full = lambda shape: pl.BlockSpec(shape, lambda i: (0,) * len(shape))
    acc_shapes = [(1, 256), (4, CHUNK, CHUNK), (CHUNK, 4), (256, 256), (1, 256), (1, 8)]

    def body(zc_ref, zpp_ref, zpkv_ref, cq_ref, sq_ref, cp_ref, sp_ref, gv_ref, ws_ref, bt_ref, pw_ref, psc_ref,
             snk_ref, dabc_ref, dz_ref, dgv_ref, dws_ref, dbt_ref, dpw_ref, dpsc_ref, dsnk_ref,
             cpool, ck, cv, dq_s, dkv_s):
        step = pl.program_id(0)
        ci = nb - 1 - step

        @pl.when(step == 0)
        def _():
            for r in (dgv_ref, dws_ref, dbt_ref, dpw_ref, dpsc_ref, dsnk_ref, cpool, ck, cv):
                r[...] = jnp.zeros_like(r)

        zc = zc_ref[...]
        dabc = dabc_ref[...]
        zg = zc[:, :512]
        gel, th = _gelu_parts(zg)
        gp = _gelu_grad(zg, th)
        gvv = gv_ref[...]
        btv = bt_ref[...]
        lane4 = lax.broadcasted_iota(jnp.int32, (CHUNK, 4), 1)
        dbt = jnp.zeros((CHUNK, 4), F32)
        for g in range(4):
            lo, hi = HEAD * g, HEAD * (g + 1)
            u, r, xh, vn, wc, causal, mixed = _gm_group(gel, g, gvv, ws_ref, btv)
            da = dabc[:, lo:hi]
            dm = da * u
            dmb = dm.astype(BF16)
            dws_ref[g] += jnp.where(causal, _dot(dmb, vn, NT), 0.0)
            dbt = dbt + jnp.where(lane4 == g, jnp.sum(dm, axis=-1, keepdims=True), 0.0)
            dvn = _dot(wc, dmb, TN)
            dgv_ref[:, lo:hi] += jnp.sum(dvn * xh, axis=0, keepdims=True)
            dxh = dvn * gvv[:, lo:hi]
            dvg = r * (dxh - xh * jnp.mean(dxh * xh, axis=-1, keepdims=True))
            dz_ref[:, lo:hi] = (da * mixed * gp[:, lo:hi]).astype(BF16)
            dz_ref[:, 256 + lo:256 + hi] = (dvg * gp[:, 256 + lo:256 + hi]).astype(BF16)
        dbt_ref[...] += dbt
        pc = zc[:, 512:768]
        pp = jnp.where(ci > 0, zpp_ref[...], 0.0)
        pooled, cnt, lane = _pool_fwd(pc, pp, ci)
        pwb = pw_ref[...].astype(BF16)
        pooled_b = pooled.astype(BF16)
        mp = _dot(pooled_b, pwb)
        db = dabc[:, 256:512]
        dpsc_ref[...] += jnp.sum(db * mp, axis=0, keepdims=True)
        dmpb = (db * psc_ref[...]).astype(BF16)
        dpw_ref[...] += _dot(pooled_b, dmpb, TN)
        dpooled = _dot(dmpb, pwb, NT)
        davg = dpooled / cnt
        zero = jnp.zeros((CHUNK, 256), F32)
        d2, d4, d8, d16 = [jnp.concatenate([zero, jnp.where((lane >= 64 * k) & (lane < 64 * (k + 1)), davg, 0.0)],
                                           axis=0) for k in range(4)]
        g8 = d8 + d16 + pltpu.roll(d16, 2 * CHUNK - 8, 0)
        g4 = d4 + g8 + pltpu.roll(g8, 2 * CHUNK - 4, 0)
        g2 = d2 + g4 + pltpu.roll(g4, 2 * CHUNK - 2, 0)
        ge = g2 + pltpu.roll(g2, 2 * CHUNK - 1, 0)
        dz_ref[:, 512:768] = (ge[CHUNK:] - dpooled + cpool[...]).astype(BF16)
        cpool[...] = ge[:CHUNK]
        cq = cq_ref[...]
        sq = sq_ref[...]
        qr, kband, vband, valid = _attn_prep(zc, zpkv_ref[...], cq, sq, cp_ref[...], sp_ref[...], ci)
        snkv = snk_ref[...]
        lane8 = lax.broadcasted_iota(jnp.int32, (1, 8), 1)
        qlane = lax.broadcasted_iota(jnp.int32, (1, 4 * CHUNK), 1)
        dsnk = jnp.zeros((1, 8), F32)
        kb = kband.astype(BF16)
        vb = vband.astype(BF16)
        kt = kband.T
        dqts = []
        for hk in range(2):
            kh = kb[:, HEAD * hk:HEAD * (hk + 1)]
            q4 = _stack_heads(qr, 0, hk).astype(BF16)
            do4 = _stack_heads(dabc, 512, hk).astype(BF16)
            p, ps = _group_probs(kh, q4, valid, _sink_row(snkv, hk))
            dp = _dot(vb[:, HEAD * hk:HEAD * (hk + 1)], do4, NT)
            dd = jnp.sum(p * dp, axis=0, keepdims=True)
            dsink = -ps * dd
            for i in range(4):
                part = jnp.sum(jnp.where((qlane >= CHUNK * i) & (qlane < CHUNK * (i + 1)), dsink, 0.0),
                               axis=1, keepdims=True)
                dsnk = dsnk + jnp.where(lane8 == 4 * hk + i, part, 0.0)
            dsb = (p * (dp - dd) * (HEAD ** -0.5)).astype(BF16)
            dqts.append(_dot(kt[HEAD * hk:HEAD * (hk + 1), :].astype(BF16), dsb))
            dkv_s[:, HEAD * hk:HEAD * (hk + 1)] = _dot(dsb, q4)
            dkv_s[:, 128 + HEAD * hk:128 + HEAD * (hk + 1)] = _dot(p.astype(BF16), do4)
        dq4 = jnp.concatenate(dqts, axis=0).T
        for hk in range(2):
            for i in range(4):
                h = 4 * hk + i
                dq_s[:, HEAD * h:HEAD * (h + 1)] = dq4[CHUNK * i:CHUNK * (i + 1), HEAD * hk:HEAD * (hk + 1)]
        dsnk_ref[...] += dsnk
        dqr = dq_s[...]
        dz_ref[:, 768:1280] = (dqr * cq + _rot_half(dqr * sq)).astype(BF16)
        dkv = dkv_s[...]
        dkr = dkv[CHUNK:, :128] + ck[...]
        dz_ref[:, 1280:1408] = (dkr * cq[:, :128] + _rot_half(dkr * sq[:, :128])).astype(BF16)
        dz_ref[:, 1408:1536] = (dkv[CHUNK:, 128:] + cv[...]).astype(BF16)
        ck[...] = dkv[:CHUNK, :128]
        cv[...] = dkv[:CHUNK, 128:]

    return pl.pallas_call(
        body, name=name, grid=(nb,), in_specs=specs,
        out_specs=[pl.BlockSpec((CHUNK, 1536), lambda i: (cur(i), 0))] + [full(a) for a in acc_shapes],
        out_shape=[jax.ShapeDtypeStruct((s, 1536), BF16)] + [jax.ShapeDtypeStruct(a, F32) for a in acc_shapes],
        scratch_shapes=[pltpu.VMEM((CHUNK, 256), F32), pltpu.VMEM((CHUNK, 128), F32), pltpu.VMEM((CHUNK, 128), F32),
                        pltpu.VMEM((CHUNK, 512), F32), pltpu.VMEM((2 * CHUNK, 256), F32)],
        compiler_params=_params(("arbitrary",)),
    )(z, z, z, cosq, sinq, cosq, sinq, gv, ws, bt, pw, psc, snk, dabc)


def _xattn_probs(qh, kh):
    s = _dot(qh, kh, NT) * (256 ** -0.5)
    e = jnp.exp(s - jnp.max(s, axis=-1, keepdims=True))
    return e / jnp.sum(e, axis=-1, keepdims=True)


def _xattn_fwd(name, q, kv, tq):
    s, d = q.shape
    mlen = kv.shape[0]

    def body(q_ref, kv_ref, o_ref):
        for h in range(4):
            lo, hi = 256 * h, 256 * (h + 1)
            p = _xattn_probs(q_ref[:, lo:hi], kv_ref[:, lo:hi])
            o_ref[:, lo:hi] = _dot(p.astype(BF16), kv_ref[:, d + lo:d + hi]).astype(BF16)

    blk = pl.BlockSpec((tq, d), lambda i: (i, 0))
    return pl.pallas_call(body, name=name, grid=(s // tq,),
                          in_specs=[blk, pl.BlockSpec((mlen, 2 * d), lambda i: (0, 0))], out_specs=blk,
                          out_shape=jax.ShapeDtypeStruct((s, d), BF16), compiler_params=_params(("parallel",)))(q, kv)


def _xattn_bwd(name, q, kv, do, tq):
    s, d = q.shape
    mlen = kv.shape[0]

    def body(q_ref, kv_ref, do_ref, dq_ref, dkv_ref):
        @pl.when(pl.program_id(0) == 0)
        def _():
            dkv_ref[...] = jnp.zeros_like(dkv_ref)

        for h in range(4):
            lo, hi = 256 * h, 256 * (h + 1)
            qh = q_ref[:, lo:hi]
            kh = kv_ref[:, lo:hi]
            vh = kv_ref[:, d + lo:d + hi]
            doh = do_ref[:, lo:hi]
            p = _xattn_probs(qh, kh)
            dp = _dot(doh, vh, NT)
            dsb = (p * (dp - jnp.sum(p * dp, axis=-1, keepdims=True)) * (256 ** -0.5)).astype(BF16)
            dq_ref[:, lo:hi] = _dot(dsb, kh).astype(BF16)
            dkv_ref[:, lo:hi] += _dot(dsb, qh, TN)
            dkv_ref[:, d + lo:d + hi] += _dot(p.astype(BF16), doh, TN)

    blk = pl.BlockSpec((tq, d), lambda i: (i, 0))
    kvb = pl.BlockSpec((mlen, 2 * d), lambda i: (0, 0))
    return pl.pallas_call(
        body, name=name, grid=(s // tq,), in_specs=[blk, kvb, blk], out_specs=[blk, kvb],
        out_shape=[jax.ShapeDtypeStruct((s, d), BF16), jax.ShapeDtypeStruct((mlen, 2 * d), F32)],
        compiler_params=_params(("arbitrary",)))(q, kv, do)


def _ffn_up(name, h, wgu, layer, tm, tn):
    s, d = h.shape
    dff = wgu.shape[2] // 2
    nj = dff // tn

    def body(h_ref, wg_ref, wu_ref, g_ref, u_ref, a_ref):
        hv = h_ref[...]
        gate = _dot(hv, wg_ref[...])
        up = _dot(hv, wu_ref[...])
        g_ref[...] = gate.astype(BF16)
        u_ref[...] = up.astype(BF16)
        a_ref[...] = (gate / (1.0 + jnp.exp(-gate)) * up).astype(BF16)

    ob = pl.BlockSpec((tm, tn), lambda j, i: (i, j))
    sd = jax.ShapeDtypeStruct((s, dff), BF16)
    return pl.pallas_call(
        body, name=name, grid=(nj, s // tm),
        in_specs=[pl.BlockSpec((tm, d), lambda j, i: (i, 0)),
                  pl.BlockSpec((None, d, tn), lambda j, i: (layer, 0, j)),
                  pl.BlockSpec((None, d, tn), lambda j, i: (layer, 0, j + nj))],
        out_specs=[ob, ob, ob], out_shape=[sd, sd, sd], compiler_params=_params(("parallel", "parallel")),
    )(h, wgu, wgu)


def _ffn_act_bwd(name, dfn, wdown, layer, gate, up, tm):
    s, d = dfn.shape
    dff = gate.shape[1]

    def body(df_ref, wd_ref, g_ref, u_ref, o_ref):
        dact = _dot(df_ref[...], wd_ref[...], NT)
        gate = g_ref[...].astype(F32)
        sig = 1.0 / (1.0 + jnp.exp(-gate))
        o_ref[:, :dff] = (dact * u_ref[...].astype(F32) * sig * (1.0 + gate * (1.0 - sig))).astype(BF16)
        o_ref[:, dff:] = (dact * gate * sig).astype(BF16)

    gb = pl.BlockSpec((tm, dff), lambda i: (i, 0))
    return pl.pallas_call(
        body, name=name, grid=(s // tm,),
        in_specs=[pl.BlockSpec((tm, d), lambda i: (i, 0)),
                  pl.BlockSpec((None, dff, d), lambda i: (layer, 0, 0)), gb, gb],
        out_specs=pl.BlockSpec((tm, 2 * dff), lambda i: (i, 0)),
        out_shape=jax.ShapeDtypeStruct((s, 2 * dff), BF16), compiler_params=_params(("parallel",)),
    )(dfn, wdown, gate, up)


def _place():
    return lax.axis_index("x"), lax.axis_index("y"), lax.axis_index("c")


def _other_chips(x, y):
    return [(1 - x, y), (x, 1 - y), (1 - x, 1 - y)]


def _region(ref, axis, chip, size):
    start = pl.multiple_of(chip * size, size)
    if axis == 1:
        return ref.at[:, pl.ds(start, size), :]
    return ref.at[:, :, pl.ds(start, size)]


ANY = pl.BlockSpec(memory_space=pl.ANY)


HBM = pl.BlockSpec(memory_space=pltpu.HBM)
SEM = pl.BlockSpec(memory_space=pltpu.SEMAPHORE)
EFFECT = pltpu.SideEffectType.DATAFLOW_SIDE_EFFECTING


def _in_hbm(a):
    return pltpu.with_memory_space_constraint(a, pltpu.HBM)


def _split_start(name, srcs, lands, ncopies, plan, after=None):
    ns, nl = len(srcs), len(lands)
    extra = [] if after is None else [after]

    def body(*refs):
        src, land = refs[:ns], refs[ns:ns + nl]
        send, recv = refs[ns + nl + len(extra)], refs[ns + nl + len(extra) + 1]
        token = refs[-1]
        x, y, c = _place()
        for k, (s_ref, d_ref, peer, _) in enumerate(plan(src, land, x, y, c)):
            pltpu.make_async_remote_copy(src_ref=s_ref, dst_ref=d_ref, send_sem=send.at[k], recv_sem=recv.at[k],
                                         device_id=peer, device_id_type=MESH).start()
        token[...] = jnp.zeros_like(token)

    ops = list(srcs) + list(lands)
    out = pl.pallas_call(
        body, name=name,
        out_shape=(pltpu.SemaphoreType.DMA((ncopies,)), pltpu.SemaphoreType.DMA((ncopies,)),
                   *[pltpu.HBM(a.shape, a.dtype) for a in ops], jax.ShapeDtypeStruct((8, 128), F32)),
        in_specs=(HBM,) * (ns + nl) + (ANY,) * len(extra),
        out_specs=(SEM, SEM) + (HBM,) * (ns + nl) + (pl.BlockSpec(memory_space=pltpu.VMEM),),
        input_output_aliases={i: 2 + i for i in range(ns + nl)},
        compiler_params=pltpu.CompilerParams(has_side_effects=EFFECT),
    )(*[_in_hbm(a) for a in ops], *extra)
    return out[0], out[1], list(out[2:2 + ns]), list(out[2 + ns:2 + ns + nl]), out[-1]


def _split_wait(name, send, recv, srcs, lands, after, plan):
    ns, nl = len(srcs), len(lands)

    def body(*refs):
        src, land = refs[:ns], refs[ns:ns + nl]
        send_ref, recv_ref = refs[ns + nl], refs[ns + nl + 1]
        x, y, c = _place()
        for k, (s_ref, _, _, got) in enumerate(plan(src, land, x, y, c)):
            cp = pltpu.make_async_remote_copy(src_ref=s_ref, dst_ref=got, send_sem=send_ref.at[k],
                                              recv_sem=recv_ref.at[k], device_id=(x, y, c), device_id_type=MESH)
            cp.wait_send()
            cp.wait_recv()

    ops = list(srcs) + list(lands)
    out = pl.pallas_call(
        body, name=name, out_shape=tuple(pltpu.HBM(a.shape, a.dtype) for a in ops),
        in_specs=(HBM,) * (ns + nl) + (SEM, SEM, ANY), out_specs=(HBM,) * (ns + nl),
        input_output_aliases={i: i for i in range(ns + nl)},
        compiler_params=pltpu.CompilerParams(has_side_effects=EFFECT),
    )(*ops, send, recv, after)
    return list(out[:ns]), list(out[ns:])


def _gather_plan(axes, sizes, layer):
    def plan(src, land, x, y, c):
        me = 2 * x + y
        out = []
        for t in range(len(land)):
            mine = _region(land[t], axes[t], me, sizes[t]).at[pl.ds(layer, 1)]
            for px, py in _other_chips(x, y):
                out.append((mine, mine, (px, py, c),
                            _region(land[t], axes[t], 2 * px + py, sizes[t]).at[pl.ds(layer, 1)]))
        return out
    return plan


def _place_own(name, w, axis, chip):
    nl, r, cs = w.shape
    tr = _rows_tile(r)
    nb = r // tr
    full = (nl, 4 * r, cs) if axis == 1 else (nl, r, 4 * cs)

    def body(m_ref, w_ref, o_ref):
        o_ref[...] = w_ref[...].astype(BF16)

    if axis == 1:
        ospec = pl.BlockSpec((None, tr, cs), lambda l, i, m: (l, m[0] * nb + i, 0))
    else:
        ospec = pl.BlockSpec((None, tr, cs), lambda l, i, m: (l, i, m[0]))
    return pl.pallas_call(
        body, name=name,
        grid_spec=pltpu.PrefetchScalarGridSpec(
            num_scalar_prefetch=1, grid=(nl, nb),
            in_specs=[pl.BlockSpec((None, tr, cs), lambda l, i, m: (l, i, 0))], out_specs=ospec),
        out_shape=jax.ShapeDtypeStruct(full, BF16), compiler_params=_params(("parallel", "parallel")),
    )(chip, w)


def _scatter_plan(axes, sizes):
    def plan(src, land, x, y, c):
        out = []
        for t in range(len(src)):
            for k, (px, py) in enumerate(_other_chips(x, y)):
                out.append((_region(src[t], axes[t], 2 * px + py, sizes[t]).at[0], land[t].at[k], (px, py, c),
                            land[t].at[k]))
        return out
    return plan


def _pair_plan(src, land, x, y, c):
    return [(src[t], land[t], (x, y, 1 - c), land[t]) for t in range(len(src))]


def _chip_sum(name, g, slots, axis, chip):
    _, r, cs = slots.shape
    tr = _rows_tile(r)
    nb = r // tr

    def body(m_ref, g_ref, s_ref, o_ref):
        acc = g_ref[...].astype(F32)
        for k in range(3):
            acc = acc + s_ref[k].astype(F32)
        o_ref[...] = acc

    if axis == 1:
        gspec = pl.BlockSpec((tr, cs), lambda i, m: (m[0] * nb + i, 0))
    else:
        gspec = pl.BlockSpec((tr, cs), lambda i, m: (i, m[0]))
    return pl.pallas_call(
        body, name=name,
        grid_spec=pltpu.PrefetchScalarGridSpec(
            num_scalar_prefetch=1, grid=(nb,),
            in_specs=[gspec, pl.BlockSpec((3, tr, cs), lambda i, m: (0, i, 0))],
            out_specs=pl.BlockSpec((tr, cs), lambda i, m: (i, 0))),
        out_shape=jax.ShapeDtypeStruct((r, cs), F32), compiler_params=_params(("parallel",)),
    )(chip, g, slots)


def _pair_adamw(name, mine, theirs, w, m, v, layer, bufs):
    r, cs = mine.shape
    tr = _rows_tile(r)
    c1 = 1.0 - B1 ** STEP
    c2 = 1.0 - B2 ** STEP

    def body(a_ref, b_ref, w_ref, m_ref, v_ref, _g, _d, _m, _v, g_ref, d_ref, mo_ref, vo_ref):
        gv = a_ref[...] + b_ref[...]
        mn = B1 * m_ref[...] + (1.0 - B1) * gv
        vn = B2 * v_ref[...] + (1.0 - B2) * (gv * gv)
        g_ref[...] = gv
        mo_ref[...] = mn
        vo_ref[...] = vn
        d_ref[...] = -LR * ((mn / c1) / (jnp.sqrt(vn / c2) + ADAM_EPS) + WD * w_ref[...])

    blk = pl.BlockSpec((tr, cs), lambda i: (i, 0))
    lay = pl.BlockSpec((None, tr, cs), lambda i: (layer, i, 0))
    return pl.pallas_call(
        body, name=name, grid=(r // tr,), in_specs=[blk, blk, lay, lay, lay] + [ANY] * 4, out_specs=[lay] * 4,
        out_shape=[jax.ShapeDtypeStruct(b.shape, b.dtype) for b in bufs],
        input_output_aliases={5 + i: i for i in range(4)}, compiler_params=_params(("parallel",)),
    )(mine, theirs, w, m, v, *bufs)


def _allreduce_small(p):
    rows = p.shape[0]

    def body(p_ref, o_ref, gath, send, recv):
        x, y, c = _place()
        me = 4 * x + 2 * y + c

        def flip(v, bit):
            return 1 - v if bit else v

        gath[me] = p_ref[...]
        cps = []
        for k in range(1, 8):
            peer = (flip(x, k & 4), flip(y, k & 2), flip(c, k & 1))
            cps.append(pltpu.make_async_remote_copy(src_ref=p_ref, dst_ref=gath.at[me], send_sem=send.at[k - 1],
                                                    recv_sem=recv.at[k - 1], device_id=peer, device_id_type=MESH))
        for cp in cps:
            cp.start()
        for k in range(1, 8):
            slot = gath.at[4 * flip(x, k & 4) + 2 * flip(y, k & 2) + flip(c, k & 1)]
            pltpu.make_async_remote_copy(src_ref=slot, dst_ref=slot, send_sem=send.at[k - 1], recv_sem=recv.at[k - 1],
                                         device_id=(x, y, c), device_id_type=MESH).wait_recv()
        for cp in cps:
            cp.wait_send()
        acc = gath[0]
        for k in range(1, 8):
            acc = acc + gath[k]
        o_ref[...] = acc

    vm = pl.BlockSpec(memory_space=pltpu.VMEM)
    return pl.pallas_call(
        body, name="allreduce_small", in_specs=[vm], out_specs=vm, out_shape=jax.ShapeDtypeStruct(p.shape, F32),
        scratch_shapes=[pltpu.VMEM((8, rows, 128), F32), pltpu.SemaphoreType.DMA((7,)), pltpu.SemaphoreType.DMA((7,))],
        compiler_params=pltpu.CompilerParams(vmem_limit_bytes=VMEM_LIMIT),
    )(p)


def _pack(parts):
    flat = []
    for p in parts:
        v = p.reshape(-1).astype(F32)
        flat.append(jnp.pad(v, (0, (-v.shape[0]) % 128)))
    v = jnp.concatenate(flat)
    v = jnp.pad(v, (0, (-v.shape[0]) % (512 * 128)))
    return v.reshape(-1, 128)


def _unpack(buf, like):
    v = buf.reshape(-1)
    out, off = [], 0
    for p in like:
        nelem = 1
        for s in p.shape:
            nelem *= s
        out.append(v[off:off + nelem].reshape(p.shape))
        off += nelem + (-nelem) % 128
    return out


def kernel(x, mem, positions, mem_norm_g, mix_pre_g, mix_post_g, w_in, gm_v_g, gm_w_s, gm_b_s, pool_w, pool_scale, attn_sinks, w_o, x_pre_g, x_post_g, w_xq, w_xkv, w_xo, ffn_pre_g, ffn_post_g, w_gate_up, w_down, loss_target, m_mem_norm_g, m_mix_pre_g, m_mix_post_g, m_w_in, m_gm_v_g, m_gm_w_s, m_gm_b_s, m_pool_w, m_pool_scale, m_attn_sinks, m_w_o, m_x_pre_g, m_x_post_g, m_w_xq, m_w_xkv, m_w_xo, m_ffn_pre_g, m_ffn_post_g, m_w_gate_up, m_w_down, v_mem_norm_g, v_mix_pre_g, v_mix_post_g, v_w_in, v_gm_v_g, v_gm_w_s, v_gm_b_s, v_pool_w, v_pool_scale, v_attn_sinks, v_w_o, v_x_pre_g, v_x_post_g, v_w_xq, v_w_xkv, v_w_xo, v_ffn_pre_g, v_ffn_post_g, v_w_gate_up, v_w_down):
    args = (x, mem, positions, mem_norm_g, mix_pre_g, mix_post_g, w_in, gm_v_g, gm_w_s, gm_b_s, pool_w, pool_scale, attn_sinks, w_o, x_pre_g, x_post_g, w_xq, w_xkv, w_xo, ffn_pre_g, ffn_post_g, w_gate_up, w_down)
    moms_m = (m_mem_norm_g, m_mix_pre_g, m_mix_post_g, m_w_in, m_gm_v_g, m_gm_w_s, m_gm_b_s, m_pool_w, m_pool_scale, m_attn_sinks, m_w_o, m_x_pre_g, m_x_post_g, m_w_xq, m_w_xkv, m_w_xo, m_ffn_pre_g, m_ffn_post_g, m_w_gate_up, m_w_down)
    moms_v = (v_mem_norm_g, v_mix_pre_g, v_mix_post_g, v_w_in, v_gm_v_g, v_gm_w_s, v_gm_b_s, v_pool_w, v_pool_scale, v_attn_sinks, v_w_o, v_x_pre_g, v_x_post_g, v_w_xq, v_w_xkv, v_w_xo, v_ffn_pre_g, v_ffn_post_g, v_w_gate_up, v_w_down)
    P = dict(zip(NAMES, args))
    P['loss_target'] = loss_target
    M = dict(zip(WEIGHTS, moms_m))
    V = dict(zip(WEIGHTS, moms_v))
    depth = w_in.shape[0]
    nbig = len(BIG)
    axes = [BIG_AXIS[n] for n in BIG]
    sizes = [P[n].shape[a] for n, a in zip(BIG, axes)]
    chip = (2 * lax.axis_index("x") + lax.axis_index("y")).astype(jnp.int32).reshape(1)

    groups = [['w_in'], ['w_o', 'w_xq', 'w_xkv', 'w_xo'], ['w_gate_up', 'w_down']]
    units = [(l, g) for l in range(depth) for g in groups]
    unit_of = {(l, n): i for i, (l, names) in enumerate(units) for n in names}
    ax = lambda names: [BIG_AXIS[n] for n in names]
    sz = lambda names: [P[n].shape[BIG_AXIS[n]] for n in names]

    full = {n: _place_own("place_" + n, P[n], BIG_AXIS[n], chip) for n in BIG}
    gathers = []
    prev, tok = None, None
    for i, (l, names) in enumerate(units):
        send, recv, _, land, tok = _split_start("gather_start%d" % i, [], [full[n] for n in names],
                                                3 * len(names), _gather_plan(ax(names), sz(names), l), after=prev)
        full.update(zip(names, land))
        gathers.append((send, recv))
        prev = land[0]
    P['first_dep'] = tok[:1, :1]
    gathered = set()

    def weights_of(l, names, after):
        i = unit_of[(l, names[0])]
        if i not in gathered:
            _, unames = units[i]
            send, recv = gathers[i]
            _, land = _split_wait("gather_wait%d" % i, send, recv, [], [full[n] for n in unames], after,
                                  _gather_plan(ax(unames), sz(unames), l))
            full.update(zip(unames, land))
            gathered.add(i)
        return {n: (full[n], l) for n in names}

    outs = {n: [lax.empty(P[n].shape, F32) for _ in range(4)] for n in BIG}
    gunits = [(l, BIG) for l in range(depth - 1, 0, -1)] + [
        (0, g) for g in (['w_gate_up', 'w_down'], ['w_xq', 'w_xkv', 'w_xo'], ['w_o'], ['w_in'])]
    collected, scatters, pairs = {}, {}, {}

    def finish_scatter(i, after):
        _, names = gunits[i]
        send, recv, g_l, slots = scatters.pop(i)
        g_l, slots = _split_wait("scatter_wait%d" % i, send, recv, g_l, slots, after,
                                 _scatter_plan(ax(names), sz(names)))
        mine = [_chip_sum("chip_sum_" + n, g.reshape(g.shape[1:]), sl, BIG_AXIS[n], chip)
                for n, g, sl in zip(names, g_l, slots)]
        send, recv, mine, theirs, tok = _split_start("pair_start%d" % i, mine, [lax.empty(a.shape, F32) for a in mine],
                                                     len(names), _pair_plan)
        pairs[i] = (send, recv, mine, theirs)
        return tok[:1, :1]

    def finish_pair(i, after):
        l, names = gunits[i]
        send, recv, mine, theirs = pairs.pop(i)
        mine, theirs = _split_wait("pair_wait%d" % i, send, recv, mine, theirs, after, _pair_plan)
        for n, a, b in zip(names, mine, theirs):
            outs[n] = _pair_adamw("adamw_" + n, a, b, P[n], M[n], V[n], l, outs[n])

    calls = {'n': 0}
    lag = 4

    def grads_of(l, g_part, after):
        collected.update({(l, n): g for n, g in g_part.items()})
        calls['n'] += 1
        now = calls['n']
        tok = jnp.zeros((1, 1), F32)
        for i, (ul, names) in enumerate(gunits):
            if ul != l or ('started', i) in collected or any((l, n) not in collected for n in names):
                continue
            collected[('started', i)] = now
            srcs = [collected[(l, n)].reshape((1,) + collected[(l, n)].shape) for n in names]
            send, recv, srcs, slots, t = _split_start("scatter_start%d" % i, srcs,
                                                      [lax.empty((3,) + P[n].shape[1:], BF16) for n in names],
                                                      3 * len(names), _scatter_plan(ax(names), sz(names)))
            scatters[i] = (send, recv, srcs, slots)
            tok = tok + t[:1, :1]
        for i in sorted(pairs):
            if collected[('summed', i)] + lag <= now:
                finish_pair(i, after)
        for i in sorted(scatters):
            if collected[('started', i)] + lag <= now:
                tok = tok + finish_scatter(i, after)
                collected[('summed', i)] = now
        return tok

    loss_part, dx, small_g = _fwd_bwd(P, weights_of, grads_of)
    loss = lax.psum(loss_part[0, 0], ("x", "y", "c"))
    grad_x = dx.reshape(x.shape)

    small_like = [P[n] for n in SMALL]
    gsum = _allreduce_small(_pack(small_g))
    dlt, mn, vn = _adamw("adamw_small", gsum, _pack(small_like), _pack([M[n] for n in SMALL]),
                         _pack([V[n] for n in SMALL]), 512)
    grads, deltas, new_m, new_v = {}, {}, {}, {}
    for name_map, buf in ((grads, gsum), (deltas, dlt), (new_m, mn), (new_v, vn)):
        for n, a in zip(SMALL, _unpack(buf, small_like)):
            name_map[n] = a

    for i in sorted(pairs):
        finish_pair(i, dlt)
    for i in sorted(scatters):
        finish_scatter(i, dlt)
    for i in sorted(pairs):
        finish_pair(i, dlt)
    for n in BIG:
        grads[n], deltas[n], new_m[n], new_v[n] = outs[n]

    return (loss, grad_x, *[grads[n] for n in WEIGHTS], *[deltas[n] for n in WEIGHTS],
            *[new_m[n] for n in WEIGHTS], *[new_v[n] for n in WEIGHTS])


def _fwd_bwd(P, weights_of, grads_of):
    (x, mem, positions, mem_norm_g, mix_pre_g, mix_post_g, w_in, gm_v_g, gm_w_s, gm_b_s, pool_w, pool_scale, attn_sinks,
     w_o, x_pre_g, x_post_g, w_xq, w_xkv, w_xo, ffn_pre_g, ffn_post_g, w_gate_up, w_down) = [P[n] for n in NAMES]
    x0 = x[0]
    s, d = x0.shape
    depth = w_in.shape[0]
    tgt = P['loss_target'][0]
    tmn = 256
    tmr = min(512, s)

    half = HEAD // 2
    inv = ROPE_THETA ** (-jnp.arange(half, dtype=F32) / half)
    ang = positions[0].astype(F32)[:, None] * inv
    cos, sin = jnp.cos(ang), jnp.sin(ang)
    cosq = jnp.tile(jnp.concatenate([cos, cos], axis=-1), (1, 8))
    sinq = jnp.tile(jnp.concatenate([-sin, sin], axis=-1), (1, 8))

    row = lambda a, l: a[l].reshape(1, -1)
    memn = _prenorm("mem_norm", mem[0], mem_norm_g.reshape(1, d), tmn)
    pw_bd = []
    for l in range(depth):
        bd = jnp.zeros((256, 256), F32)
        for g in range(4):
            bd = lax.dynamic_update_slice(bd, pool_w[l, g], (64 * g, 64 * g))
        pw_bd.append(bd)

    saved = []
    xc = x0
    h = _prenorm("pre_norm0", x0, row(mix_pre_g, 0) + P['first_dep'], tmn)
    for l in range(depth):
        W = weights_of(l, ['w_in'], xc)
        sv = {'x0': xc, 'h1': h}
        z = _mm_nn("fwd_w_in", h, *W['w_in'], tm=1024, tn=512, tk=d, out_dtype=F32)
        abc = _mixer_fwd("mixer_fwd", z, cosq, sinq, row(gm_v_g, l), gm_w_s[l], gm_b_s[l].T, pw_bd[l],
                         row(pool_scale, l), row(attn_sinks, l))
        W.update(weights_of(l, ['w_o'], z))
        mix, xc, h = _mm_rows("fwd_w_o", abc, *W['w_o'], 'nn', tm=tmr, tk=d, rows_in=[xc],
                              params=[row(mix_post_g, l), row(x_pre_g, l)], rows_out=[BF16, F32, BF16],
                              epilogue=_post_pre_rows)
        sv.update(z=z, abc=abc, mix=mix, x1=xc, h2=h)
        W.update(weights_of(l, ['w_xq', 'w_xkv', 'w_xo'], xc))
        q = _mm_nn("fwd_w_xq", h, *W['w_xq'], tm=1024, tn=512, tk=d, out_dtype=BF16)
        kv = _mm_nn("fwd_w_xkv", memn, *W['w_xkv'], tm=256, tn=512, tk=d, out_dtype=BF16)
        o = _xattn_fwd("xattn_fwd", q, kv, 512)
        xo, xc, h = _mm_rows("fwd_w_xo", o, *W['w_xo'], 'nn', tm=tmr, tk=d, rows_in=[xc],
                             params=[row(x_post_g, l), row(ffn_pre_g, l)], rows_out=[BF16, F32, BF16],
                             epilogue=_post_pre_rows)
        sv.update(q=q, kv=kv, o=o, xo=xo, x2=xc, h3=h)
        W.update(weights_of(l, ['w_gate_up', 'w_down'], xc))
        dff = W['w_down'][0].shape[1]
        gate, up, act = _ffn_up("ffn_up", h, *W['w_gate_up'], 512, dff // 2)
        sv.update(gate=gate, up=up, act=act)
        if l + 1 < depth:
            f, xc, h = _mm_rows("fwd_w_down", act, *W['w_down'], 'nn', tm=tmr, tk=dff // 2, rows_in=[xc],
                                params=[row(ffn_post_g, l), row(mix_pre_g, l + 1)], rows_out=[BF16, F32, BF16],
                                epilogue=_post_pre_rows)
            sv.update(f=f)
        saved.append(sv)
    gs = {n: [None] * depth for n in SMALL if n != 'mem_norm_g'}
    dx, dfn, gs['ffn_post_g'][depth - 1], loss_part = _mm_rows(
        "fwd_w_down_loss", saved[-1]['act'], *W['w_down'], 'nn', tm=tmr, tk=dff // 2, rows_in=[xc, tgt],
        params=[row(ffn_post_g, depth - 1)], rows_out=[F32, BF16], n_sums=2, epilogue=_make_loss_rows(d))

    dmemn = None
    tok = jnp.zeros((1, 1), F32)
    for l in reversed(range(depth)):
        sv, W, G = saved[l], weights_of(l, BIG, dx), {}
        G['w_down'] = _mm_tn("dw_down", sv['act'], dfn, tm=dff // 2, tn=d, tk=512)
        dgu = _ffn_act_bwd("ffn_act_bwd", dfn, *W['w_down'], sv['gate'], sv['up'], 256)
        G['w_gate_up'] = _mm_tn("dw_gate_up", sv['h3'], dgu, tm=d, tn=dff // 2, tk=512)
        dx, dxo, gs['ffn_pre_g'][l], gs['x_post_g'][l] = _mm_rows(
            "bwd_w_gate_up", dgu, *W['w_gate_up'], 'nt', tm=tmr, tk=dff // 2, rows_in=[sv['x2'], dx, sv['xo']],
            params=[row(ffn_pre_g, l) + tok, row(x_post_g, l)], rows_out=[F32, BF16], n_sums=2, epilogue=_bwd_rows)
        tok = grads_of(l, {n: G[n] for n in ('w_gate_up', 'w_down')}, dx)
        G['w_xo'] = _mm_tn("dw_xo", sv['o'], dxo, tm=d, tn=d, tk=512)
        do = _mm_nt("bwd_w_xo", dxo, *W['w_xo'], tm=1024, tn=512, tk=d, out_dtype=BF16, dep=tok)
        dq, dkv = _xattn_bwd("xattn_bwd", sv['q'], sv['kv'], do, 512)
        dkv = dkv.astype(BF16)
        G['w_xkv'] = _mm_tn("dw_xkv", memn, dkv, tm=d, tn=d, tk=mem.shape[1])
        dmemn = _mm_nt("bwd_w_xkv", dkv, *W['w_xkv'], tm=mem.shape[1], tn=512, tk=2 * d, out_dtype=F32, add=dmemn)
        G['w_xq'] = _mm_tn("dw_xq", sv['h2'], dq, tm=d, tn=d, tk=512)
        dx, dmix, gs['x_pre_g'][l], gs['mix_post_g'][l] = _mm_rows(
            "bwd_w_xq", dq, *W['w_xq'], 'nt', tm=tmr, tk=d, rows_in=[sv['x1'], dx, sv['mix']],
            params=[row(x_pre_g, l), row(mix_post_g, l)], rows_out=[F32, BF16], n_sums=2, epilogue=_bwd_rows)
        tok = grads_of(l, {n: G[n] for n in ('w_xq', 'w_xkv', 'w_xo')}, dx)
        G['w_o'] = _mm_tn("dw_o", sv['abc'], dmix, tm=d, tn=d, tk=512)
        dabc = _mm_nt("bwd_w_o", dmix, *W['w_o'], tm=1024, tn=512, tk=d, out_dtype=F32, dep=tok)
        tok = grads_of(l, {'w_o': G['w_o']}, dabc)
        dz, dgv, dws, dbt, dpw, dpsc, dsnk = _mixer_bwd(
            "mixer_bwd", sv['z'], dabc, cosq, sinq, row(gm_v_g, l) + tok, gm_w_s[l], gm_b_s[l].T, pw_bd[l],
            row(pool_scale, l), row(attn_sinks, l))
        gs['gm_v_g'][l] = dgv
        gs['gm_w_s'][l] = dws
        gs['gm_b_s'][l] = dbt.T
        gs['pool_w'][l] = jnp.stack([dpw[64 * g:64 * (g + 1), 64 * g:64 * (g + 1)] for g in range(4)])
        gs['pool_scale'][l] = dpsc
        gs['attn_sinks'][l] = dsnk
        G['w_in'] = _mm_tn("dw_in", sv['h1'], dz, tm=d, tn=dz.shape[1], tk=512)
        if l > 0:
            dx, dfn, gs['mix_pre_g'][l], gs['ffn_post_g'][l - 1] = _mm_rows(
                "bwd_w_in", dz, *W['w_in'], 'nt', tm=tmr, tk=dz.shape[1], rows_in=[sv['x0'], dx, saved[l - 1]['f']],
                params=[row(mix_pre_g, l), row(ffn_post_g, l - 1)], rows_out=[F32, BF16], n_sums=2,
                epilogue=_bwd_rows)
        else:
            dx, gs['mix_pre_g'][l] = _mm_rows(
                "bwd_w_in_first", dz, *W['w_in'], 'nt', tm=tmr, tk=dz.shape[1], rows_in=[sv['x0'], dx],
                params=[row(mix_pre_g, l)], rows_out=[F32], n_sums=1, epilogue=_bwd_rows_first)
        tok = grads_of(l, {'w_in': G['w_in']}, dx)
    _, dg_mem = _norm_bwd("bwd_mem_norm", mem[0], mem_norm_g.reshape(1, d), dmemn, None, BF16, tmn)
    small_g = []
    for n in SMALL:
        if n == 'mem_norm_g':
            small_g.append(dg_mem.reshape(P[n].shape))
        else:
            small_g.append(jnp.stack([a.reshape(P[n].shape[1:]) for a in gs[n]]))
    return loss_part, dx, small_g
```

```python
import functools

import jax
import jax.numpy as jnp
from jax import lax
from jax.experimental import pallas as pl
from jax.experimental.pallas import tpu as pltpu

F32 = jnp.float32
BF16 = jnp.bfloat16
EPS = 1e-6
CHUNK = 128
HEAD = 64
ROPE_THETA = 10000.0
POOL_WINDOWS = (2, 4, 8, 16)
LR, B1, B2, ADAM_EPS, WD, STEP = 0.001, 0.9, 0.999, 1e-08, 0.01, 10
MESH = pl.DeviceIdType.MESH
VMEM_LIMIT = 56 * 1024 * 1024

NAMES = ['x', 'mem', 'positions', 'mem_norm_g', 'mix_pre_g', 'mix_post_g', 'w_in', 'gm_v_g', 'gm_w_s', 'gm_b_s',
         'pool_w', 'pool_scale', 'attn_sinks', 'w_o', 'x_pre_g', 'x_post_g', 'w_xq', 'w_xkv', 'w_xo', 'ffn_pre_g',
         'ffn_post_g', 'w_gate_up', 'w_down']
WEIGHTS = NAMES[3:]
BIG = ['w_in', 'w_o', 'w_xq', 'w_xkv', 'w_xo', 'w_gate_up', 'w_down']
BIG_AXIS = {'w_in': 2, 'w_o': 1, 'w_xq': 1, 'w_xkv': 2, 'w_xo': 1, 'w_gate_up': 2, 'w_down': 1}
SMALL = [n for n in WEIGHTS if n not in BIG]

NN = (((1,), (0,)), ((), ()))
NT = (((1,), (1,)), ((), ()))
TN = (((0,), (0,)), ((), ()))


def _dot(a, b, dims=NN):
    return lax.dot_general(a, b, dims, preferred_element_type=F32)


def _params(sem):
    return pltpu.CompilerParams(dimension_semantics=sem, vmem_limit_bytes=VMEM_LIMIT)


def _rows_tile(rows, limit=256):
    return max(t for t in range(16, limit + 1, 16) if rows % t == 0)


def _mm(name, a, a_spec, b, b_spec, dims, grid, nk, out_shape, out_spec, add=None, add_spec=None, dep=None):
    acc_shape = out_spec.block_shape

    def body(*refs):
        a_ref, b_ref = refs[0], refs[1]
        pos = 2
        add_ref = None
        if add is not None:
            add_ref = refs[pos]
            pos += 1
        if dep is not None:
            pos += 1
        o_ref = refs[pos]
        part = _dot(a_ref[...].astype(BF16), b_ref[...].astype(BF16), dims)
        if nk == 1:
            if add_ref is not None:
                part = part + add_ref[...]
            o_ref[...] = part.astype(o_ref.dtype)
        else:
            acc_ref = refs[pos + 1]
            k = pl.program_id(2)

            @pl.when(k == 0)
            def _():
                acc_ref[...] = part if add_ref is None else part + add_ref[...]

            @pl.when(k > 0)
            def _():
                acc_ref[...] += part

            @pl.when(k == nk - 1)
            def _():
                o_ref[...] = acc_ref[...].astype(o_ref.dtype)

    ops, specs = [a, b], [a_spec, b_spec]
    if add is not None:
        ops.append(add)
        specs.append(add_spec)
    if dep is not None:
        ops.append(dep)
        specs.append(pl.BlockSpec((1, 1), lambda i, j, k: (0, 0)))
    return pl.pallas_call(
        body, name=name, grid=grid, in_specs=specs, out_specs=out_spec, out_shape=out_shape,
        scratch_shapes=[pltpu.VMEM(acc_shape, F32)] if nk > 1 else [],
        compiler_params=_params(("parallel", "parallel", "arbitrary")),
    )(*ops)


def _wspec(block, layer, fn):
    return pl.BlockSpec((None,) + block, lambda i, j, k: (layer,) + fn(i, j, k))


def _mm_nn(name, a, w, layer, *, tm, tn, tk, out_dtype, n0=0, n=None, k0=0):
    m, kk = a.shape
    n = w.shape[2] if n is None else n
    tm = min(tm, m)
    nk = kk // tk
    return _mm(name, a, pl.BlockSpec((tm, tk), lambda i, j, k: (i, k)),
               w, _wspec((tk, tn), layer, lambda i, j, k: (k + k0 // tk, j + n0 // tn)), NN,
               (m // tm, n // tn, nk), nk, jax.ShapeDtypeStruct((m, n), out_dtype),
               pl.BlockSpec((tm, tn), lambda i, j, k: (i, j)))


def _mm_nt(name, a, w, layer, *, tm, tn, tk, out_dtype, k0=0, add=None, dep=None):
    m, kk = a.shape
    n = w.shape[1]
    tm = min(tm, m)
    nk = kk // tk
    ospec = pl.BlockSpec((tm, tn), lambda i, j, k: (i, j))
    return _mm(name, a, pl.BlockSpec((tm, tk), lambda i, j, k: (i, k)),
               w, _wspec((tn, tk), layer, lambda i, j, k: (j, k + k0 // tk)), NT,
               (m // tm, n // tn, nk), nk, jax.ShapeDtypeStruct((m, n), out_dtype), ospec,
               add=add, add_spec=ospec if add is not None else None, dep=dep)


def _mm_tn(name, a, b, *, tm, tn, tk):
    kk, m = a.shape
    n = b.shape[1]
    tk = min(tk, kk)
    nk = kk // tk
    return _mm(name, a, pl.BlockSpec((tk, tm), lambda i, j, k: (k, i)),
               b, pl.BlockSpec((tk, tn), lambda i, j, k: (k, j)), TN,
               (m // tm, n // tn, nk), nk, jax.ShapeDtypeStruct((m, n), BF16),
               pl.BlockSpec((tm, tn), lambda i, j, k: (i, j)))


def _mm_rows(name, a, w, layer, mode, *, tm, tk, k0=0, rows_in=(), params=(), rows_out=(), n_sums=0, epilogue):
    m, kk = a.shape
    n = w.shape[2] if mode == 'nn' else w.shape[1]
    nk = kk // tk
    nr, npar, no = len(rows_in), len(params), len(rows_out)

    def body(*refs):
        a_ref, w_ref = refs[0], refs[1]
        rin = refs[2:2 + nr]
        par = refs[2 + nr:2 + nr + npar]
        outs = refs[2 + nr + npar:2 + nr + npar + no]
        sums = refs[2 + nr + npar + no:2 + nr + npar + no + n_sums]
        i, k = pl.program_id(0), pl.program_id(1)
        part = _dot(a_ref[...], w_ref[...], NN if mode == 'nn' else NT)

        def finish(acc):
            res, sm = epilogue(acc, [r[...] for r in rin], [p[...] for p in par])
            for r, v in zip(outs, res):
                r[...] = v.astype(r.dtype)

            @pl.when(i == 0)
            def _():
                for r in sums:
                    r[...] = jnp.zeros_like(r)

            for r, v in zip(sums, sm):
                r[...] += v

        if nk == 1:
            finish(part)
        else:
            acc_ref = refs[-1]

            @pl.when(k == 0)
            def _():
                acc_ref[...] = part

            @pl.when(k > 0)
            def _():
                acc_ref[...] += part

            @pl.when(k == nk - 1)
            def _():
                finish(acc_ref[...])

    if mode == 'nn':
        wspec = pl.BlockSpec((None, tk, n), lambda i, k: (layer, k + k0 // tk, 0))
    else:
        wspec = pl.BlockSpec((None, n, tk), lambda i, k: (layer, 0, k + k0 // tk))
    rowblk = pl.BlockSpec((tm, n), lambda i, k: (i, 0))
    one = pl.BlockSpec((1, n), lambda i, k: (0, 0))
    return pl.pallas_call(
        body, name=name, grid=(m // tm, nk),
        in_specs=[pl.BlockSpec((tm, tk), lambda i, k: (i, k)), wspec] + [rowblk] * nr + [one] * npar,
        out_specs=[rowblk] * no + [one] * n_sums,
        out_shape=[jax.ShapeDtypeStruct((m, n), dt) for dt in rows_out] +
                  [jax.ShapeDtypeStruct((1, n), F32)] * n_sums,
        scratch_shapes=[pltpu.VMEM((tm, n), F32)] if nk > 1 else [],
        compiler_params=_params(("arbitrary", "arbitrary")),
    )(a, w, *rows_in, *params)


def _rstd(x):
    return lax.rsqrt(jnp.mean(x * x, axis=-1, keepdims=True) + EPS)


def _norm_back(xin, g, dy):
    r = _rstd(xin)
    xh = xin * r
    dyg = dy * g
    return r * (dyg - xh * jnp.mean(dyg * xh, axis=-1, keepdims=True)), jnp.sum(dy * xh, axis=0, keepdims=True)


def _post_pre_rows(y, rows, pars):
    xn = rows[0] + y * _rstd(y) * pars[0]
    return [y, xn, xn * _rstd(xn) * pars[1]], []


def _make_loss_rows(d):
    def fn(y, rows, pars):
        x, tgt = rows
        err = x + y * _rstd(y) * pars[0] - tgt
        dout = err * (1.0 / d)
        dy, dg = _norm_back(y, pars[0], dout)
        lsum = 0.5 * jnp.sum(jnp.mean(err * err, axis=-1, keepdims=True), axis=0, keepdims=True)
        return [dout, dy], [dg, jnp.broadcast_to(lsum, dg.shape)]
    return fn


def _bwd_rows(dh, rows, pars):
    xin, resid, yprev = rows
    dxa, dg_pre = _norm_back(xin, pars[0], dh)
    dx = resid + dxa
    dyp, dg_post = _norm_back(yprev.astype(F32), pars[1], dx)
    return [dx, dyp], [dg_pre, dg_post]


def _bwd_rows_first(dh, rows, pars):
    xin, resid = rows
    dxa, dg_pre = _norm_back(xin, pars[0], dh)
    return [resid + dxa], [dg_pre]


def _row(d):
    return pl.BlockSpec((1, d), lambda i: (0, 0))


def _prenorm(name, x, g, tm):
    m, d = x.shape

    def body(x_ref, g_ref, o_ref):
        xv = x_ref[...]
        o_ref[...] = (xv * _rstd(xv) * g_ref[...]).astype(BF16)

    blk = pl.BlockSpec((tm, d), lambda i: (i, 0))
    return pl.pallas_call(body, name=name, grid=(m // tm,), in_specs=[blk, _row(d)], out_specs=blk,
                          out_shape=jax.ShapeDtypeStruct((m, d), BF16), compiler_params=_params(("parallel",)))(x, g)


def _norm_bwd(name, xin, g, dy, resid, out_dtype, tm):
    m, d = xin.shape

    def body(*refs):
        if resid is None:
            x_ref, g_ref, dy_ref, dx_ref, dg_ref = refs
        else:
            x_ref, g_ref, dy_ref, r_ref, dx_ref, dg_ref = refs
        xv = x_ref[...]
        r = _rstd(xv)
        xh = xv * r
        dyv = dy_ref[...].astype(F32)
        dyg = dyv * g_ref[...]
        dx = r * (dyg - xh * jnp.mean(dyg * xh, axis=-1, keepdims=True))
        if resid is not None:
            dx = dx + r_ref[...]
        dx_ref[...] = dx.astype(dx_ref.dtype)

        @pl.when(pl.program_id(0) == 0)
        def _():
            dg_ref[...] = jnp.zeros_like(dg_ref)

        dg_ref[...] += jnp.sum(dyv * xh, axis=0, keepdims=True)

    blk = pl.BlockSpec((tm, d), lambda i: (i, 0))
    ops = [xin, g, dy] + ([] if resid is None else [resid])
    specs = [blk, _row(d), blk] + ([] if resid is None else [blk])
    return pl.pallas_call(
        body, name=name, grid=(m // tm,), in_specs=specs, out_specs=[blk, _row(d)],
        out_shape=[jax.ShapeDtypeStruct((m, d), out_dtype), jax.ShapeDtypeStruct((1, d), F32)],
        compiler_params=_params(("arbitrary",)))(*ops)


def _adamw(name, g, w, m, v, tr):
    rows, cols = g.shape
    c1 = 1.0 - B1 ** STEP
    c2 = 1.0 - B2 ** STEP

    def body(g_ref, w_ref, m_ref, v_ref, d_ref, mo_ref, vo_ref):
        gv = g_ref[...]
        mn = B1 * m_ref[...] + (1.0 - B1) * gv
        vn = B2 * v_ref[...] + (1.0 - B2) * (gv * gv)
        mo_ref[...] = mn
        vo_ref[...] = vn
        d_ref[...] = -LR * ((mn / c1) / (jnp.sqrt(vn / c2) + ADAM_EPS) + WD * w_ref[...])

    blk = pl.BlockSpec((tr, cols), lambda i: (i, 0))
    sd = jax.ShapeDtypeStruct((rows, cols), F32)
    return pl.pallas_call(body, name=name, grid=(rows // tr,), in_specs=[blk] * 4, out_specs=[blk] * 3,
                          out_shape=[sd, sd, sd], compiler_params=_params(("parallel",)))(g, w, m, v)


def _gelu_parts(x):
    c = 0.7978845608028654
    t = jnp.tanh(c * (x + 0.044715 * (x * x * x)))
    return 0.5 * x * (1.0 + t), t


def _gelu_grad(x, t):
    c = 0.7978845608028654
    return 0.5 * (1.0 + t) + 0.5 * x * (1.0 - t * t) * (c * (1.0 + 3.0 * 0.044715 * x * x))


def _rot_half(x):
    ax = x.ndim - 1
    w = x.shape[ax]
    lane = lax.broadcasted_iota(jnp.int32, x.shape, ax)
    return jnp.where((lane & 63) < 32, pltpu.roll(x, w - 32, ax), pltpu.roll(x, 32, ax))


def _gm_group(gel, g, gv, ws_ref, bt):
    u = gel[:, HEAD * g:HEAD * (g + 1)]
    vg = gel[:, 256 + HEAD * g:256 + HEAD * (g + 1)]
    r = _rstd(vg)
    xh = vg * r
    vn = (xh * gv[:, HEAD * g:HEAD * (g + 1)]).astype(BF16)
    row = lax.broadcasted_iota(jnp.int32, (CHUNK, CHUNK), 0)
    col = lax.broadcasted_iota(jnp.int32, (CHUNK, CHUNK), 1)
    causal = col <= row
    wc = jnp.where(causal, ws_ref[g], 0.0).astype(BF16)
    mixed = _dot(wc, vn) + bt[:, g:g + 1]
    return u, r, xh, vn, wc, causal, mixed


def _lane_select(lane, vals):
    return jnp.where(lane < 64, vals[0], jnp.where(lane < 128, vals[1], jnp.where(lane < 192, vals[2], vals[3])))


def _pool_fwd(pc, pp, ci):
    ext = jnp.concatenate([pp, pc], axis=0)
    s2 = ext + pltpu.roll(ext, 1, 0)
    s4 = s2 + pltpu.roll(s2, 2, 0)
    s8 = s4 + pltpu.roll(s4, 4, 0)
    s16 = s8 + pltpu.roll(s8, 8, 0)
    t1 = ci * CHUNK + lax.broadcasted_iota(jnp.int32, (CHUNK, 1), 0) + 1
    lane = lax.broadcasted_iota(jnp.int32, (1, 256), 1)
    cnt = _lane_select(lane, [jnp.minimum(t1, w).astype(F32) for w in POOL_WINDOWS])
    ssel = _lane_select(lane, [s[CHUNK:] for s in (s2, s4, s8, s16)])
    return ssel / cnt - pc, cnt, lane


def _attn_prep(zc, zpkv, cq, sq, cp, sp, ci):
    q = zc[:, 768:1280]
    kc = zc[:, 1280:1408]
    vc = zc[:, 1408:1536]
    kp = zpkv[:, :128]
    vp = zpkv[:, 128:]
    qr = q * cq + _rot_half(q) * sq
    krc = kc * cq[:, :128] + _rot_half(kc) * sq[:, :128]
    krp = kp * cp + _rot_half(kp) * sp
    kband = jnp.concatenate([krp, krc], axis=0)
    vband = jnp.concatenate([vp, vc], axis=0)
    key = lax.broadcasted_iota(jnp.int32, (2 * CHUNK, 4 * CHUNK), 0)
    t = lax.broadcasted_iota(jnp.int32, (2 * CHUNK, 4 * CHUNK), 1) & (CHUNK - 1)
    valid = ((key < CHUNK) & (key > t) & (ci > 0)) | ((key >= CHUNK) & (key - CHUNK <= t))
    return qr, kband, vband, valid


def _stack_heads(x, base, hk):
    return jnp.concatenate([x[:, base + HEAD * (4 * hk + i):base + HEAD * (4 * hk + i + 1)] for i in range(4)], axis=0)


def _sink_row(snk, hk):
    lane = lax.broadcasted_iota(jnp.int32, (1, 4 * CHUNK), 1)
    s = [snk[:, 4 * hk + i:4 * hk + i + 1] for i in range(4)]
    return jnp.where(lane < CHUNK, s[0], jnp.where(lane < 2 * CHUNK, s[1], jnp.where(lane < 3 * CHUNK, s[2], s[3])))


def _group_probs(kh, q4, valid, sink4):
    s = _dot(kh, q4, NT) * (HEAD ** -0.5)
    s = jnp.where(valid, s, -1e30)
    mx = jnp.maximum(jnp.max(s, axis=0, keepdims=True), sink4)
    e = jnp.exp(s - mx)
    es = jnp.exp(sink4 - mx)
    inv = 1.0 / (jnp.sum(e, axis=0, keepdims=True) + es)
    return e * inv, es * inv


def _mixer_specs(nb, rev):
    def cur(i):
        return nb - 1 - i if rev else i

    def prev(i):
        return jnp.maximum(cur(i) - 1, 0)

    full = lambda shape: pl.BlockSpec(shape, lambda i: (0,) * len(shape))
    specs = [
        pl.BlockSpec((CHUNK, 1536), lambda i: (cur(i), 0)),
        pl.BlockSpec((CHUNK, 256), lambda i: (prev(i), 2)),
        pl.BlockSpec((CHUNK, 256), lambda i: (prev(i), 5)),
        pl.BlockSpec((CHUNK, 512), lambda i: (cur(i), 0)),
        pl.BlockSpec((CHUNK, 512), lambda i: (cur(i), 0)),
        pl.BlockSpec((CHUNK, 128), lambda i: (prev(i), 0)),
        pl.BlockSpec((CHUNK, 128), lambda i: (prev(i), 0)),
        full((1, 256)), full((4, CHUNK, CHUNK)), full((CHUNK, 4)), full((256, 256)), full((1, 256)), full((1, 8)),
    ]
    return specs, cur


def _mixer_fwd(name, z, cosq, sinq, gv, ws, bt, pw, psc, snk):
    s = z.shape[0]
    nb = s // CHUNK
    specs, _ = _mixer_specs(nb, False)

    def body(zc_ref, zpp_ref, zpkv_ref, cq_ref, sq_ref, cp_ref, sp_ref, gv_ref, ws_ref, bt_ref, pw_ref, psc_ref,
             snk_ref, o_ref):
        ci = pl.program_id(0)
        zc = zc_ref[...]
        gel, _ = _gelu_parts(zc[:, :512])
        gvv = gv_ref[...]
        btv = bt_ref[...]
        for g in range(4):
            u, _, _, _, _, _, mixed = _gm_group(gel, g, gvv, ws_ref, btv)
            o_ref[:, HEAD * g:HEAD * (g + 1)] = (u * mixed).astype(BF16)
        pp = jnp.where(ci > 0, zpp_ref[...], 0.0)
        pooled, _, _ = _pool_fwd(zc[:, 512:768], pp, ci)
        mp = _dot(pooled.astype(BF16), pw_ref[...].astype(BF16))
        o_ref[:, 256:512] = (mp * psc_ref[...]).astype(BF16)
        qr, kband, vband, valid = _attn_prep(zc, zpkv_ref[...], cq_ref[...], sq_ref[...], cp_ref[...], sp_ref[...], ci)
        snkv = snk_ref[...]
        kb = kband.astype(BF16)
        vt = vband.T
        ots = []
        for hk in range(2):
            q4 = _stack_heads(qr, 0, hk).astype(BF16)
            p, _ = _group_probs(kb[:, HEAD * hk:HEAD * (hk + 1)], q4, valid, _sink_row(snkv, hk))
            ots.append(_dot(vt[HEAD * hk:HEAD * (hk + 1), :].astype(BF16), p.astype(BF16)))
        o = jnp.concatenate(ots, axis=0).T
        for hk in range(2):
            for i in range(4):
                h = 4 * hk + i
                o_ref[:, 512 + HEAD * h:512 + HEAD * (h + 1)] = o[CHUNK * i:CHUNK * (i + 1),
                                                                  HEAD * hk:HEAD * (hk + 1)].astype(BF16)

    return pl.pallas_call(
        body, name=name, grid=(nb,), in_specs=specs, out_specs=pl.BlockSpec((CHUNK, 1024), lambda i: (i, 0)),
        out_shape=jax.ShapeDtypeStruct((s, 1024), BF16), compiler_params=_params(("parallel",)),
    )(z, z, z, cosq, sinq, cosq, sinq, gv, ws, bt, pw, psc, snk)


def _mixer_bwd(name, z, dabc, cosq, sinq, gv, ws, bt, pw, psc, snk):
    s = z.shape[0]
    nb = s // CHUNK
    specs, cur = _mixer_specs(nb, True)
    specs = specs + [pl.BlockSpec((CHUNK, 1024), lambda i: (cur(i), 0))]
    full = lambda shape: pl.BlockSpec(shape, lambda i: (0,) * len(shape))
    acc_shapes = [(1, 256), (4, CHUNK, CHUNK), (CHUNK, 4), (256, 256), (1, 256), (1, 8)]

    def body(zc_ref, zpp_ref, zpkv_ref, cq_ref, sq_ref, cp_ref, sp_ref, gv_ref, ws_ref, bt_ref, pw_ref, psc_ref,
             snk_ref, dabc_ref, dz_ref, dgv_ref, dws_ref, dbt_ref, dpw_ref, dpsc_ref, dsnk_ref,
             cpool, ck, cv, dq_s, dkv_s):
        step = pl.program_id(0)
        ci = nb - 1 - step

        @pl.when(step == 0)
        def _():
            for r in (dgv_ref, dws_ref, dbt_ref, dpw_ref, dpsc_ref, dsnk_ref, cpool, ck, cv):
                r[...] = jnp.zeros_like(r)

        zc = zc_ref[...]
        dabc = dabc_ref[...]
        zg = zc[:, :512]
        gel, th = _gelu_parts(zg)
        gp = _gelu_grad(zg, th)
        gvv = gv_ref[...]
        btv = bt_ref[...]
        lane4 = lax.broadcasted_iota(jnp.int32, (CHUNK, 4), 1)
        dbt = jnp.zeros((CHUNK, 4), F32)
        for g in range(4):
            lo, hi = HEAD * g, HEAD * (g + 1)
            u, r, xh, vn, wc, causal, mixed = _gm_group(gel, g, gvv, ws_ref, btv)
            da = dabc[:, lo:hi]
            dm = da * u
            dmb = dm.astype(BF16)
            dws_ref[g] += jnp.where(causal, _dot(dmb, vn, NT), 0.0)
            dbt = dbt + jnp.where(lane4 == g, jnp.sum(dm, axis=-1, keepdims=True), 0.0)
            dvn = _dot(wc, dmb, TN)
            dgv_ref[:, lo:hi] += jnp.sum(dvn * xh, axis=0, keepdims=True)
            dxh = dvn * gvv[:, lo:hi]
            dvg = r * (dxh - xh * jnp.mean(dxh * xh, axis=-1, keepdims=True))
            dz_ref[:, lo:hi] = (da * mixed * gp[:, lo:hi]).astype(BF16)
            dz_ref[:, 256 + lo:256 + hi] = (dvg * gp[:, 256 + lo:256 + hi]).astype(BF16)
        dbt_ref[...] += dbt
        pc = zc[:, 512:768]
        pp = jnp.where(ci > 0, zpp_ref[...], 0.0)
        pooled, cnt, lane = _pool_fwd(pc, pp, ci)
        pwb = pw_ref[...].astype(BF16)
        pooled_b = pooled.astype(BF16)
        mp = _dot(pooled_b, pwb)
        db = dabc[:, 256:512]
        dpsc_ref[...] += jnp.sum(db * mp, axis=0, keepdims=True)
        dmpb = (db * psc_ref[...]).astype(BF16)
        dpw_ref[...] += _dot(pooled_b, dmpb, TN)
        dpooled = _dot(dmpb, pwb, NT)
        davg = dpooled / cnt
        zero = jnp.zeros((CHUNK, 256), F32)
        d2, d4, d8, d16 = [jnp.concatenate([zero, jnp.where((lane >= 64 * k) & (lane < 64 * (k + 1)), davg, 0.0)],
                                           axis=0) for k in range(4)]
        g8 = d8 + d16 + pltpu.roll(d16, 2 * CHUNK - 8, 0)
        g4 = d4 + g8 + pltpu.roll(g8, 2 * CHUNK - 4, 0)
        g2 = d2 + g4 + pltpu.roll(g4, 2 * CHUNK - 2, 0)
        ge = g2 + pltpu.roll(g2, 2 * CHUNK - 1, 0)
        dz_ref[:, 512:768] = (ge[CHUNK:] - dpooled + cpool[...]).astype(BF16)
        cpool[...] = ge[:CHUNK]
        cq = cq_ref[...]
        sq = sq_ref[...]
        qr, kband, vband, valid = _attn_prep(zc, zpkv_ref[...], cq, sq, cp_ref[...], sp_ref[...], ci)
        snkv = snk_ref[...]
        lane8 = lax.broadcasted_iota(jnp.int32, (1, 8), 1)
        qlane = lax.broadcasted_iota(jnp.int32, (1, 4 * CHUNK), 1)
        dsnk = jnp.zeros((1, 8), F32)
        kb = kband.astype(BF16)
        vb = vband.astype(BF16)
        kt = kband.T
        dqts = []
        for hk in range(2):
            kh = kb[:, HEAD * hk:HEAD * (hk + 1)]
            q4 = _stack_heads(qr, 0, hk).astype(BF16)
            do4 = _stack_heads(dabc, 512, hk).astype(BF16)
            p, ps = _group_probs(kh, q4, valid, _sink_row(snkv, hk))
            dp = _dot(vb[:, HEAD * hk:HEAD * (hk + 1)], do4, NT)
            dd = jnp.sum(p * dp, axis=0, keepdims=True)
            dsink = -ps * dd
            for i in range(4):
                part = jnp.sum(jnp.where((qlane >= CHUNK * i) & (qlane < CHUNK * (i + 1)), dsink, 0.0),
                               axis=1, keepdims=True)
                dsnk = dsnk + jnp.where(lane8 == 4 * hk + i, part, 0.0)
            dsb = (p * (dp - dd) * (HEAD ** -0.5)).astype(BF16)
            dqts.append(_dot(kt[HEAD * hk:HEAD * (hk + 1), :].astype(BF16), dsb))
            dkv_s[:, HEAD * hk:HEAD * (hk + 1)] = _dot(dsb, q4)
            dkv_s[:, 128 + HEAD * hk:128 + HEAD * (hk + 1)] = _dot(p.astype(BF16), do4)
        dq4 = jnp.concatenate(dqts, axis=0).T
        for hk in range(2):
            for i in range(4):
                h = 4 * hk + i
                dq_s[:, HEAD * h:HEAD * (h + 1)] = dq4[CHUNK * i:CHUNK * (i + 1), HEAD * hk:HEAD * (hk + 1)]
        dsnk_ref[...] += dsnk
        dqr = dq_s[...]
        dz_ref[:, 768:1280] = (dqr * cq + _rot_half(dqr * sq)).astype(BF16)
        dkv = dkv_s[...]
        dkr = dkv[CHUNK:, :128] + ck[...]
        dz_ref[:, 1280:1408] = (dkr * cq[:, :128] + _rot_half(dkr * sq[:, :128])).astype(BF16)
        dz_ref[:, 1408:1536] = (dkv[CHUNK:, 128:] + cv[...]).astype(BF16)
        ck[...] = dkv[:CHUNK, :128]
        cv[...] = dkv[:CHUNK, 128:]

    return pl.pallas_call(
        body, name=name, grid=(nb,), in_specs=specs,
        out_specs=[pl.BlockSpec((CHUNK, 1536), lambda i: (cur(i), 0))] + [full(a) for a in acc_shapes],
        out_shape=[jax.ShapeDtypeStruct((s, 1536), BF16)] + [jax.ShapeDtypeStruct(a, F32) for a in acc_shapes],
        scratch_shapes=[pltpu.VMEM((CHUNK, 256), F32), pltpu.VMEM((CHUNK, 128), F32), pltpu.VMEM((CHUNK, 128), F32),
                        pltpu.VMEM((CHUNK, 512), F32), pltpu.VMEM((2 * CHUNK, 256), F32)],
        compiler_params=_params(("arbitrary",)),
    )(z, z, z, cosq, sinq, cosq, sinq, gv, ws, bt, pw, psc, snk, dabc)


def _xattn_probs(qh, kh):
    s = _dot(qh, kh, NT) * (256 ** -0.5)
    e = jnp.exp(s - jnp.max(s, axis=-1, keepdims=True))
    return e * (1.0 / jnp.sum(e, axis=-1, keepdims=True))


def _xattn_fwd(name, q, kv, tq):
    s, d = q.shape
    mlen = kv.shape[0]

    def body(q_ref, kv_ref, o_ref):
        for h in range(4):
            lo, hi = 256 * h, 256 * (h + 1)
            p = _xattn_probs(q_ref[:, lo:hi], kv_ref[:, lo:hi])
            o_ref[:, lo:hi] = _dot(p.astype(BF16), kv_ref[:, d + lo:d + hi]).astype(BF16)

    blk = pl.BlockSpec((tq, d), lambda i: (i, 0))
    return pl.pallas_call(body, name=name, grid=(s // tq,),
                          in_specs=[blk, pl.BlockSpec((mlen, 2 * d), lambda i: (0, 0))], out_specs=blk,
                          out_shape=jax.ShapeDtypeStruct((s, d), BF16), compiler_params=_params(("parallel",)))(q, kv)


def _xattn_bwd(name, q, kv, do, tq):
    s, d = q.shape
    mlen = kv.shape[0]

    def body(q_ref, kv_ref, do_ref, dq_ref, dkv_ref):
        @pl.when(pl.program_id(0) == 0)
        def _():
            dkv_ref[...] = jnp.zeros_like(dkv_ref)

        for h in range(4):
            lo, hi = 256 * h, 256 * (h + 1)
            qh = q_ref[:, lo:hi]
            kh = kv_ref[:, lo:hi]
            vh = kv_ref[:, d + lo:d + hi]
            doh = do_ref[:, lo:hi]
            p = _xattn_probs(qh, kh)
            dp = _dot(doh, vh, NT)
            dsb = (p * (dp - jnp.sum(p * dp, axis=-1, keepdims=True)) * (256 ** -0.5)).astype(BF16)
            dq_ref[:, lo:hi] = _dot(dsb, kh).astype(BF16)
            dkv_ref[:, lo:hi] += _dot(dsb, qh, TN)
            dkv_ref[:, d + lo:d + hi] += _dot(p.astype(BF16), doh, TN)

    blk = pl.BlockSpec((tq, d), lambda i: (i, 0))
    kvb = pl.BlockSpec((mlen, 2 * d), lambda i: (0, 0))
    return pl.pallas_call(
        body, name=name, grid=(s // tq,), in_specs=[blk, kvb, blk], out_specs=[blk, kvb],
        out_shape=[jax.ShapeDtypeStruct((s, d), BF16), jax.ShapeDtypeStruct((mlen, 2 * d), F32)],
        compiler_params=_params(("arbitrary",)))(q, kv, do)


def _sigmoid(x):
    return 0.5 * (1.0 + jnp.tanh(0.5 * x))


def _ffn_up(name, h, wgu, layer, tm, tn):
    s, d = h.shape
    dff = wgu.shape[2] // 2
    nj = dff // tn

    def body(h_ref, wg_ref, wu_ref, g_ref, u_ref, a_ref):
        hv = h_ref[...]
        gate = _dot(hv, wg_ref[...])
        up = _dot(hv, wu_ref[...])
        g_ref[...] = gate.astype(BF16)
        u_ref[...] = up.astype(BF16)
        a_ref[...] = (gate * _sigmoid(gate) * up).astype(BF16)

    ob = pl.BlockSpec((tm, tn), lambda j, i: (i, j))
    sd = jax.ShapeDtypeStruct((s, dff), BF16)
    return pl.pallas_call(
        body, name=name, grid=(nj, s // tm),
        in_specs=[pl.BlockSpec((tm, d), lambda j, i: (i, 0)),
                  pl.BlockSpec((None, d, tn), lambda j, i: (layer, 0, j)),
                  pl.BlockSpec((None, d, tn), lambda j, i: (layer, 0, j + nj))],
        out_specs=[ob, ob, ob], out_shape=[sd, sd, sd], compiler_params=_params(("parallel", "parallel")),
    )(h, wgu, wgu)


def _ffn_act_bwd(name, dfn, wdown, layer, gate, up, tm):
    s, d = dfn.shape
    dff = gate.shape[1]

    def body(df_ref, wd_ref, g_ref, u_ref, o_ref):
        dact = _dot(df_ref[...], wd_ref[...], NT)
        gate = g_ref[...].astype(F32)
        sig = _sigmoid(gate)
        o_ref[:, :dff] = (dact * u_ref[...].astype(F32) * sig * (1.0 + gate * (1.0 - sig))).astype(BF16)
        o_ref[:, dff:] = (dact * gate * sig).astype(BF16)

    gb = pl.BlockSpec((tm, dff), lambda i: (i, 0))
    return pl.pallas_call(
        body, name=name, grid=(s // tm,),
        in_specs=[pl.BlockSpec((tm, d), lambda i: (i, 0)),
                  pl.BlockSpec((None, dff, d), lambda i: (layer, 0, 0)), gb, gb],
        out_specs=pl.BlockSpec((tm, 2 * dff), lambda i: (i, 0)),
        out_shape=jax.ShapeDtypeStruct((s, 2 * dff), BF16), compiler_params=_params(("parallel",)),
    )(dfn, wdown, gate, up)


def _place():
    return lax.axis_index("x"), lax.axis_index("y"), lax.axis_index("c")


def _other_chips(x, y):
    return [(1 - x, y), (x, 1 - y), (1 - x, 1 - y)]


def _region(ref, axis, chip, size):
    start = pl.multiple_of(chip * size, size)
    if axis == 1:
        return ref.at[:, pl.ds(start, size), :]
    return ref.at[:, :, pl.ds(start, size)]


ANY = pl.BlockSpec(memory_space=pl.ANY)


HBM = pl.BlockSpec(memory_space=pltpu.HBM)
SEM = pl.BlockSpec(memory_space=pltpu.SEMAPHORE)
EFFECT = pltpu.SideEffectType.DATAFLOW_SIDE_EFFECTING


def _in_hbm(a):
    return pltpu.with_memory_space_constraint(a, pltpu.HBM)


def _split_start(name, srcs, lands, ncopies, plan, after=None):
    ns, nl = len(srcs), len(lands)
    extra = [] if after is None else [after]

    def body(*refs):
        src, land = refs[:ns], refs[ns:ns + nl]
        send, recv = refs[ns + nl + len(extra)], refs[ns + nl + len(extra) + 1]
        token = refs[-1]
        x, y, c = _place()
        for k, (s_ref, d_ref, peer, _) in enumerate(plan(src, land, x, y, c)):
            pltpu.make_async_remote_copy(src_ref=s_ref, dst_ref=d_ref, send_sem=send.at[k], recv_sem=recv.at[k],
                                         device_id=peer, device_id_type=MESH).start()
        token[...] = jnp.zeros_like(token)

    ops = list(srcs) + list(lands)
    out = pl.pallas_call(
        body, name=name,
        out_shape=(pltpu.SemaphoreType.DMA((ncopies,)), pltpu.SemaphoreType.DMA((ncopies,)),
                   *[pltpu.HBM(a.shape, a.dtype) for a in ops], jax.ShapeDtypeStruct((8, 128), F32)),
        in_specs=(HBM,) * (ns + nl) + (ANY,) * len(extra),
        out_specs=(SEM, SEM) + (HBM,) * (ns + nl) + (pl.BlockSpec(memory_space=pltpu.VMEM),),
        input_output_aliases={i: 2 + i for i in range(ns + nl)},
        compiler_params=pltpu.CompilerParams(has_side_effects=EFFECT),
    )(*[_in_hbm(a) for a in ops], *extra)
    return out[0], out[1], list(out[2:2 + ns]), list(out[2 + ns:2 + ns + nl]), out[-1]


def _split_wait(name, send, recv, srcs, lands, after, plan):
    ns, nl = len(srcs), len(lands)

    def body(*refs):
        src, land = refs[:ns], refs[ns:ns + nl]
        send_ref, recv_ref = refs[ns + nl], refs[ns + nl + 1]
        x, y, c = _place()
        for k, (s_ref, _, _, got) in enumerate(plan(src, land, x, y, c)):
            cp = pltpu.make_async_remote_copy(src_ref=s_ref, dst_ref=got, send_sem=send_ref.at[k],
                                              recv_sem=recv_ref.at[k], device_id=(x, y, c), device_id_type=MESH)
            cp.wait_send()
            cp.wait_recv()

    ops = list(srcs) + list(lands)
    out = pl.pallas_call(
        body, name=name, out_shape=tuple(pltpu.HBM(a.shape, a.dtype) for a in ops),
        in_specs=(HBM,) * (ns + nl) + (SEM, SEM, ANY), out_specs=(HBM,) * (ns + nl),
        input_output_aliases={i: i for i in range(ns + nl)},
        compiler_params=pltpu.CompilerParams(has_side_effects=EFFECT),
    )(*ops, send, recv, after)
    return list(out[:ns]), list(out[ns:])


def _gather_plan(axes, sizes, layer):
    def plan(src, land, x, y, c):
        me = 2 * x + y
        out = []
        for t in range(len(land)):
            mine = _region(land[t], axes[t], me, sizes[t]).at[pl.ds(layer, 1)]
            for px, py in _other_chips(x, y):
                out.append((mine, mine, (px, py, c),
                            _region(land[t], axes[t], 2 * px + py, sizes[t]).at[pl.ds(layer, 1)]))
        return out
    return plan


def _place_own(name, w, axis, chip):
    nl, r, cs = w.shape
    tr = _rows_tile(r)
    nb = r // tr
    full = (nl, 4 * r, cs) if axis == 1 else (nl, r, 4 * cs)

    def body(m_ref, w_ref, o_ref):
        o_ref[...] = w_ref[...].astype(BF16)

    if axis == 1:
        ospec = pl.BlockSpec((None, tr, cs), lambda l, i, m: (l, m[0] * nb + i, 0))
    else:
        ospec = pl.BlockSpec((None, tr, cs), lambda l, i, m: (l, i, m[0]))
    return pl.pallas_call(
        body, name=name,
        grid_spec=pltpu.PrefetchScalarGridSpec(
            num_scalar_prefetch=1, grid=(nl, nb),
            in_specs=[pl.BlockSpec((None, tr, cs), lambda l, i, m: (l, i, 0))], out_specs=ospec),
        out_shape=jax.ShapeDtypeStruct(full, BF16), compiler_params=_params(("parallel", "parallel")),
    )(chip, w)


def _scatter_plan(axes, sizes):
    def plan(src, land, x, y, c):
        out = []
        for t in range(len(src)):
            for k, (px, py) in enumerate(_other_chips(x, y)):
                out.append((_region(src[t], axes[t], 2 * px + py, sizes[t]).at[0], land[t].at[k], (px, py, c),
                            land[t].at[k]))
        return out
    return plan


def _pair_plan(src, land, x, y, c):
    return [(src[t], land[t], (x, y, 1 - c), land[t]) for t in range(len(src))]


def _chip_sum(name, g, slots, axis, chip):
    _, r, cs = slots.shape
    tr = _rows_tile(r)
    nb = r // tr

    def body(m_ref, g_ref, s_ref, o_ref):
        acc = g_ref[...].astype(F32)
        for k in range(3):
            acc = acc + s_ref[k].astype(F32)
        o_ref[...] = acc

    if axis == 1:
        gspec = pl.BlockSpec((tr, cs), lambda i, m: (m[0] * nb + i, 0))
    else:
        gspec = pl.BlockSpec((tr, cs), lambda i, m: (i, m[0]))
    return pl.pallas_call(
        body, name=name,
        grid_spec=pltpu.PrefetchScalarGridSpec(
            num_scalar_prefetch=1, grid=(nb,),
            in_specs=[gspec, pl.BlockSpec((3, tr, cs), lambda i, m: (0, i, 0))],
            out_specs=pl.BlockSpec((tr, cs), lambda i, m: (i, 0))),
        out_shape=jax.ShapeDtypeStruct((r, cs), F32), compiler_params=_params(("parallel",)),
    )(chip, g, slots)


def _pair_adamw(name, mine, theirs, w, m, v, layer, bufs):
    r, cs = mine.shape
    tr = _rows_tile(r)
    c1 = 1.0 - B1 ** STEP
    c2 = 1.0 - B2 ** STEP

    def body(a_ref, b_ref, w_ref, m_ref, v_ref, _g, _d, _m, _v, g_ref, d_ref, mo_ref, vo_ref):
        gv = a_ref[...] + b_ref[...]
        mn = B1 * m_ref[...] + (1.0 - B1) * gv
        vn = B2 * v_ref[...] + (1.0 - B2) * (gv * gv)
        g_ref[...] = gv
        mo_ref[...] = mn
        vo_ref[...] = vn
        d_ref[...] = -LR * ((mn / c1) / (jnp.sqrt(vn / c2) + ADAM_EPS) + WD * w_ref[...])

    blk = pl.BlockSpec((tr, cs), lambda i: (i, 0))
    lay = pl.BlockSpec((None, tr, cs), lambda i: (layer, i, 0))
    return pl.pallas_call(
        body, name=name, grid=(r // tr,), in_specs=[blk, blk, lay, lay, lay] + [ANY] * 4, out_specs=[lay] * 4,
        out_shape=[jax.ShapeDtypeStruct(b.shape, b.dtype) for b in bufs],
        input_output_aliases={5 + i: i for i in range(4)}, compiler_params=_params(("parallel",)),
    )(mine, theirs, w, m, v, *bufs)


def _allreduce_small(p):
    rows = p.shape[0]

    def body(p_ref, o_ref, sib, sums, send, recv):
        x, y, c = _place()
        me = 2 * x + y
        pair = pltpu.make_async_remote_copy(src_ref=p_ref, dst_ref=sib, send_sem=send.at[3], recv_sem=recv.at[3],
                                            device_id=(x, y, 1 - c), device_id_type=MESH)
        pair.start()
        pair.wait()
        sums[me] = (p_ref[...] + sib[...]).astype(BF16)
        cps = [pltpu.make_async_remote_copy(src_ref=sums.at[me], dst_ref=sums.at[me], send_sem=send.at[k],
                                            recv_sem=recv.at[k], device_id=(px, py, c), device_id_type=MESH)
               for k, (px, py) in enumerate(_other_chips(x, y))]
        for cp in cps:
            cp.start()
        for k, (px, py) in enumerate(_other_chips(x, y)):
            slot = sums.at[2 * px + py]
            pltpu.make_async_remote_copy(src_ref=slot, dst_ref=slot, send_sem=send.at[k], recv_sem=recv.at[k],
                                         device_id=(x, y, c), device_id_type=MESH).wait_recv()
        for cp in cps:
            cp.wait_send()
        acc = sums[0].astype(F32)
        for k in range(1, 4):
            acc = acc + sums[k].astype(F32)
        o_ref[...] = acc

    vm = pl.BlockSpec(memory_space=pltpu.VMEM)
    return pl.pallas_call(
        body, name="allreduce_small", in_specs=[vm], out_specs=vm, out_shape=jax.ShapeDtypeStruct(p.shape, F32),
        scratch_shapes=[pltpu.VMEM((rows, 128), F32), pltpu.VMEM((4, rows, 128), BF16),
                        pltpu.SemaphoreType.DMA((4,)), pltpu.SemaphoreType.DMA((4,))],
        compiler_params=pltpu.CompilerParams(vmem_limit_bytes=VMEM_LIMIT),
    )(p)


def _pack(parts):
    flat = []
    for p in parts:
        v = p.reshape(-1).astype(F32)
        flat.append(jnp.pad(v, (0, (-v.shape[0]) % 128)))
    v = jnp.concatenate(flat)
    v = jnp.pad(v, (0, (-v.shape[0]) % (512 * 128)))
    return v.reshape(-1, 128)


def _unpack(buf, like):
    v = buf.reshape(-1)
    out, off = [], 0
    for p in like:
        nelem = 1
        for s in p.shape:
            nelem *= s
        out.append(v[off:off + nelem].reshape(p.shape))
        off += nelem + (-nelem) % 128
    return out


def kernel(x, mem, positions, mem_norm_g, mix_pre_g, mix_post_g, w_in, gm_v_g, gm_w_s, gm_b_s, pool_w, pool_scale, attn_sinks, w_o, x_pre_g, x_post_g, w_xq, w_xkv, w_xo, ffn_pre_g, ffn_post_g, w_gate_up, w_down, loss_target, m_mem_norm_g, m_mix_pre_g, m_mix_post_g, m_w_in, m_gm_v_g, m_gm_w_s, m_gm_b_s, m_pool_w, m_pool_scale, m_attn_sinks, m_w_o, m_x_pre_g, m_x_post_g, m_w_xq, m_w_xkv, m_w_xo, m_ffn_pre_g, m_ffn_post_g, m_w_gate_up, m_w_down, v_mem_norm_g, v_mix_pre_g, v_mix_post_g, v_w_in, v_gm_v_g, v_gm_w_s, v_gm_b_s, v_pool_w, v_pool_scale, v_attn_sinks, v_w_o, v_x_pre_g, v_x_post_g, v_w_xq, v_w_xkv, v_w_xo, v_ffn_pre_g, v_ffn_post_g, v_w_gate_up, v_w_down):
    args = (x, mem, positions, mem_norm_g, mix_pre_g, mix_post_g, w_in, gm_v_g, gm_w_s, gm_b_s, pool_w, pool_scale, attn_sinks, w_o, x_pre_g, x_post_g, w_xq, w_xkv, w_xo, ffn_pre_g, ffn_post_g, w_gate_up, w_down)
    moms_m = (m_mem_norm_g, m_mix_pre_g, m_mix_post_g, m_w_in, m_gm_v_g, m_gm_w_s, m_gm_b_s, m_pool_w, m_pool_scale, m_attn_sinks, m_w_o, m_x_pre_g, m_x_post_g, m_w_xq, m_w_xkv, m_w_xo, m_ffn_pre_g, m_ffn_post_g, m_w_gate_up, m_w_down)
    moms_v = (v_mem_norm_g, v_mix_pre_g, v_mix_post_g, v_w_in, v_gm_v_g, v_gm_w_s, v_gm_b_s, v_pool_w, v_pool_scale, v_attn_sinks, v_w_o, v_x_pre_g, v_x_post_g, v_w_xq, v_w_xkv, v_w_xo, v_ffn_pre_g, v_ffn_post_g, v_w_gate_up, v_w_down)
    P = dict(zip(NAMES, args))
    P['loss_target'] = loss_target
    M = dict(zip(WEIGHTS, moms_m))
    V = dict(zip(WEIGHTS, moms_v))
    depth = w_in.shape[0]
    nbig = len(BIG)
    axes = [BIG_AXIS[n] for n in BIG]
    sizes = [P[n].shape[a] for n, a in zip(BIG, axes)]
    chip = (2 * lax.axis_index("x") + lax.axis_index("y")).astype(jnp.int32).reshape(1)

    groups = [['w_in'], ['w_o', 'w_xq', 'w_xkv', 'w_xo'], ['w_gate_up', 'w_down']]
    units = [(l, g) for l in range(depth) for g in groups]
    unit_of = {(l, n): i for i, (l, names) in enumerate(units) for n in names}
    ax = lambda names: [BIG_AXIS[n] for n in names]
    sz = lambda names: [P[n].shape[BIG_AXIS[n]] for n in names]

    full = {n: _place_own("place_" + n, P[n], BIG_AXIS[n], chip) for n in BIG}
    gathers = []
    prev, tok = None, None
    for i, (l, names) in enumerate(units):
        send, recv, _, land, tok = _split_start("gather_start%d" % i, [], [full[n] for n in names],
                                                3 * len(names), _gather_plan(ax(names), sz(names), l), after=prev)
        full.update(zip(names, land))
        gathers.append((send, recv))
        prev = land[0]
    P['first_dep'] = tok[:1, :1]
    gathered = set()

    def weights_of(l, names, after):
        i = unit_of[(l, names[0])]
        if i not in gathered:
            _, unames = units[i]
            send, recv = gathers[i]
            _, land = _split_wait("gather_wait%d" % i, send, recv, [], [full[n] for n in unames], after,
                                  _gather_plan(ax(unames), sz(unames), l))
            full.update(zip(unames, land))
            gathered.add(i)
        return {n: (full[n], l) for n in names}

    outs = {n: [lax.empty(P[n].shape, F32) for _ in range(4)] for n in BIG}
    gunits = [(l, BIG) for l in range(depth - 1, 0, -1)] + [
        (0, g) for g in (['w_gate_up', 'w_down'], ['w_xq', 'w_xkv', 'w_xo'], ['w_o'], ['w_in'])]
    collected, scatters, pairs = {}, {}, {}

    def finish_scatter(i, after):
        _, names = gunits[i]
        send, recv, g_l, slots = scatters.pop(i)
        g_l, slots = _split_wait("scatter_wait%d" % i, send, recv, g_l, slots, after,
                                 _scatter_plan(ax(names), sz(names)))
        mine = [_chip_sum("chip_sum_" + n, g.reshape(g.shape[1:]), sl, BIG_AXIS[n], chip)
                for n, g, sl in zip(names, g_l, slots)]
        send, recv, mine, theirs, tok = _split_start("pair_start%d" % i, mine, [lax.empty(a.shape, F32) for a in mine],
                                                     len(names), _pair_plan)
        pairs[i] = (send, recv, mine, theirs)
        return tok[:1, :1]

    def finish_pair(i, after):
        l, names = gunits[i]
        send, recv, mine, theirs = pairs.pop(i)
        mine, theirs = _split_wait("pair_wait%d" % i, send, recv, mine, theirs, after, _pair_plan)
        for n, a, b in zip(names, mine, theirs):
            outs[n] = _pair_adamw("adamw_" + n, a, b, P[n], M[n], V[n], l, outs[n])

    calls = {'n': 0}
    lag = 4

    def grads_of(l, g_part, after):
        collected.update({(l, n): g for n, g in g_part.items()})
        calls['n'] += 1
        now = calls['n']
        tok = jnp.zeros((1, 1), F32)
        for i, (ul, names) in enumerate(gunits):
            if ul != l or ('started', i) in collected or any((l, n) not in collected for n in names):
                continue
            collected[('started', i)] = now
            srcs = [collected[(l, n)].reshape((1,) + collected[(l, n)].shape) for n in names]
            send, recv, srcs, slots, t = _split_start("scatter_start%d" % i, srcs,
                                                      [lax.empty((3,) + P[n].shape[1:], BF16) for n in names],
                                                      3 * len(names), _scatter_plan(ax(names), sz(names)))
            scatters[i] = (send, recv, srcs, slots)
            tok = tok + t[:1, :1]
        for i in sorted(pairs):
            if collected[('summed', i)] + lag <= now:
                finish_pair(i, after)
        for i in sorted(scatters):
            if collected[('started', i)] + lag <= now:
                tok = tok + finish_scatter(i, after)
                collected[('summed', i)] = now
        return tok

    loss_part, dx, small_g = _fwd_bwd(P, weights_of, grads_of)
    loss = lax.psum(loss_part[0, 0], ("x", "y", "c"))
    grad_x = dx.reshape(x.shape)

    small_like = [P[n] for n in SMALL]
    gsum = _allreduce_small(_pack(small_g))
    dlt, mn, vn = _adamw("adamw_small", gsum, _pack(small_like), _pack([M[n] for n in SMALL]),
                         _pack([V[n] for n in SMALL]), 512)
    grads, deltas, new_m, new_v = {}, {}, {}, {}
    for name_map, buf in ((grads, gsum), (deltas, dlt), (new_m, mn), (new_v, vn)):
        for n, a in zip(SMALL, _unpack(buf, small_like)):
            name_map[n] = a

    for i in sorted(pairs):
        finish_pair(i, dlt)
    for i in sorted(scatters):
        finish_scatter(i, dlt)
    for i in sorted(pairs):
        finish_pair(i, dlt)
    for n in BIG:
        grads[n], deltas[n], new_m[n], new_v[n] = outs[n]

    return (loss, grad_x, *[grads[n] for n in WEIGHTS], *[deltas[n] for n in WEIGHTS],
            *[new_m[n] for n in WEIGHTS], *[new_v[n] for n in WEIGHTS])


def _fwd_bwd(P, weights_of, grads_of):
    (x, mem, positions, mem_norm_g, mix_pre_g, mix_post_g, w_in, gm_v_g, gm_w_s, gm_b_s, pool_w, pool_scale, attn_sinks,
     w_o, x_pre_g, x_post_g, w_xq, w_xkv, w_xo, ffn_pre_g, ffn_post_g, w_gate_up, w_down) = [P[n] for n in NAMES]
    x0 = x[0]
    s, d = x0.shape
    depth = w_in.shape[0]
    tgt = P['loss_target'][0]
    tmn = 256
    tmr = min(512, s)
    tkw = min(2048, s)

    half = HEAD // 2
    inv = ROPE_THETA ** (-jnp.arange(half, dtype=F32) / half)
    ang = positions[0].astype(F32)[:, None] * inv
    cos, sin = jnp.cos(ang), jnp.sin(ang)
    cosq = jnp.tile(jnp.concatenate([cos, cos], axis=-1), (1, 8))
    sinq = jnp.tile(jnp.concatenate([-sin, sin], axis=-1), (1, 8))

    row = lambda a, l: a[l].reshape(1, -1)
    memn = _prenorm("mem_norm", mem[0], mem_norm_g.reshape(1, d), tmn)
    pw_bd = []
    for l in range(depth):
        bd = jnp.zeros((256, 256), F32)
        for g in range(4):
            bd = lax.dynamic_update_slice(bd, pool_w[l, g], (64 * g, 64 * g))
        pw_bd.append(bd)

    saved = []
    xc = x0
    h = _prenorm("pre_norm0", x0, row(mix_pre_g, 0) + P['first_dep'], tmn)
    for l in range(depth):
        W = weights_of(l, ['w_in'], xc)
        sv = {'x0': xc, 'h1': h}
        z = _mm_nn("fwd_w_in", h, *W['w_in'], tm=1024, tn=512, tk=d, out_dtype=F32)
        abc = _mixer_fwd("mixer_fwd", z, cosq, sinq, row(gm_v_g, l), gm_w_s[l], gm_b_s[l].T, pw_bd[l],
                         row(pool_scale, l), row(attn_sinks, l))
        W.update(weights_of(l, ['w_o'], z))
        mix, xc, h = _mm_rows("fwd_w_o", abc, *W['w_o'], 'nn', tm=tmr, tk=d, rows_in=[xc],
                              params=[row(mix_post_g, l), row(x_pre_g, l)], rows_out=[BF16, F32, BF16],
                              epilogue=_post_pre_rows)
        sv.update(z=z, abc=abc, mix=mix, x1=xc, h2=h)
        W.update(weights_of(l, ['w_xq', 'w_xkv', 'w_xo'], xc))
        q = _mm_nn("fwd_w_xq", h, *W['w_xq'], tm=1024, tn=512, tk=d, out_dtype=BF16)
        kv = _mm_nn("fwd_w_xkv", memn, *W['w_xkv'], tm=256, tn=512, tk=d, out_dtype=BF16)
        o = _xattn_fwd("xattn_fwd", q, kv, 512)
        xo, xc, h = _mm_rows("fwd_w_xo", o, *W['w_xo'], 'nn', tm=tmr, tk=d, rows_in=[xc],
                             params=[row(x_post_g, l), row(ffn_pre_g, l)], rows_out=[BF16, F32, BF16],
                             epilogue=_post_pre_rows)
        sv.update(q=q, kv=kv, o=o, xo=xo, x2=xc, h3=h)
        W.update(weights_of(l, ['w_gate_up', 'w_down'], xc))
        dff = W['w_down'][0].shape[1]
        gate, up, act = _ffn_up("ffn_up", h, *W['w_gate_up'], 512, dff // 2)
        sv.update(gate=gate, up=up, act=act)
        if l + 1 < depth:
            f, xc, h = _mm_rows("fwd_w_down", act, *W['w_down'], 'nn', tm=tmr, tk=dff, rows_in=[xc],
                                params=[row(ffn_post_g, l), row(mix_pre_g, l + 1)], rows_out=[BF16, F32, BF16],
                                epilogue=_post_pre_rows)
            sv.update(f=f)
        saved.append(sv)
    gs = {n: [None] * depth for n in SMALL if n != 'mem_norm_g'}
    dx, dfn, gs['ffn_post_g'][depth - 1], loss_part = _mm_rows(
        "fwd_w_down_loss", saved[-1]['act'], *W['w_down'], 'nn', tm=tmr, tk=dff, rows_in=[xc, tgt],
        params=[row(ffn_post_g, depth - 1)], rows_out=[F32, BF16], n_sums=2, epilogue=_make_loss_rows(d))

    dmemn = None
    tok = jnp.zeros((1, 1), F32)
    for l in reversed(range(depth)):
        sv, W, G = saved[l], weights_of(l, BIG, dx), {}
        G['w_down'] = _mm_tn("dw_down", sv['act'], dfn, tm=dff // 2, tn=d, tk=tkw)
        dgu = _ffn_act_bwd("ffn_act_bwd", dfn, *W['w_down'], sv['gate'], sv['up'], 256)
        G['w_gate_up'] = _mm_tn("dw_gate_up", sv['h3'], dgu, tm=d, tn=dff // 2, tk=tkw)
        dx, dxo, gs['ffn_pre_g'][l], gs['x_post_g'][l] = _mm_rows(
            "bwd_w_gate_up", dgu, *W['w_gate_up'], 'nt', tm=tmr, tk=dff, rows_in=[sv['x2'], dx, sv['xo']],
            params=[row(ffn_pre_g, l) + tok, row(x_post_g, l)], rows_out=[F32, BF16], n_sums=2, epilogue=_bwd_rows)
        tok = grads_of(l, {n: G[n] for n in ('w_gate_up', 'w_down')}, dx)
        G['w_xo'] = _mm_tn("dw_xo", sv['o'], dxo, tm=d, tn=d, tk=tkw)
        do = _mm_nt("bwd_w_xo", dxo, *W['w_xo'], tm=1024, tn=512, tk=d, out_dtype=BF16, dep=tok)
        dq, dkv = _xattn_bwd("xattn_bwd", sv['q'], sv['kv'], do, 512)
        dkv = dkv.astype(BF16)
        G['w_xkv'] = _mm_tn("dw_xkv", memn, dkv, tm=d, tn=d, tk=mem.shape[1])
        dmemn = _mm_nt("bwd_w_xkv", dkv, *W['w_xkv'], tm=mem.shape[1], tn=512, tk=2 * d, out_dtype=F32, add=dmemn)
        G['w_xq'] = _mm_tn("dw_xq", sv['h2'], dq, tm=d, tn=d, tk=tkw)
        dx, dmix, gs['x_pre_g'][l], gs['mix_post_g'][l] = _mm_rows(
            "bwd_w_xq", dq, *W['w_xq'], 'nt', tm=tmr, tk=d, rows_in=[sv['x1'], dx, sv['mix']],
            params=[row(x_pre_g, l), row(mix_post_g, l)], rows_out=[F32, BF16], n_sums=2, epilogue=_bwd_rows)
        tok = grads_of(l, {n: G[n] for n in ('w_xq', 'w_xkv', 'w_xo')}, dx)
        G['w_o'] = _mm_tn("dw_o", sv['abc'], dmix, tm=d, tn=d, tk=tkw)
        dabc = _mm_nt("bwd_w_o", dmix, *W['w_o'], tm=1024, tn=512, tk=d, out_dtype=F32, dep=tok)
        tok = grads_of(l, {'w_o': G['w_o']}, dabc)
        dz, dgv, dws, dbt, dpw, dpsc, dsnk = _mixer_bwd(
            "mixer_bwd", sv['z'], dabc, cosq, sinq, row(gm_v_g, l) + tok, gm_w_s[l], gm_b_s[l].T, pw_bd[l],
            row(pool_scale, l), row(attn_sinks, l))
        gs['gm_v_g'][l] = dgv
        gs['gm_w_s'][l] = dws
        gs['gm_b_s'][l] = dbt.T
        gs['pool_w'][l] = jnp.stack([dpw[64 * g:64 * (g + 1), 64 * g:64 * (g + 1)] for g in range(4)])
        gs['pool_scale'][l] = dpsc
        gs['attn_sinks'][l] = dsnk
        G['w_in'] = _mm_tn("dw_in", sv['h1'], dz, tm=d, tn=dz.shape[1], tk=tkw)
        if l > 0:
            dx, dfn, gs['mix_pre_g'][l], gs['ffn_post_g'][l - 1] = _mm_rows(
                "bwd_w_in", dz, *W['w_in'], 'nt', tm=tmr, tk=dz.shape[1], rows_in=[sv['x0'], dx, saved[l - 1]['f']],
                params=[row(mix_pre_g, l), row(ffn_post_g, l - 1)], rows_out=[F32, BF16], n_sums=2,
                epilogue=_bwd_rows)
        else:
            dx, gs['mix_pre_g'][l] = _mm_rows(
                "bwd_w_in_first", dz, *W['w_in'], 'nt', tm=tmr, tk=dz.shape[1], rows_in=[sv['x0'], dx],
                params=[row(mix_pre_g, l)], rows_out=[F32], n_sums=1, epilogue=_bwd_rows_first)
        tok = grads_of(l, {'w_in': G['w_in']}, dx)
    _, dg_mem = _norm_bwd("bwd_mem_norm", mem[0], mem_norm_g.reshape(1, d), dmemn, None, BF16, tmn)
    small_g = []
    for n in SMALL:
        if n == 'mem_norm_g':
            small_g.append(dg_mem.reshape(P[n].shape))
        else:
            small_g.append(jnp.stack([a.reshape(P[n].shape[1:]) for a in gs[n]]))
    return loss_part, dx, small_g
```

```python
import functools

import jax
import jax.numpy as jnp
from jax import lax
from jax.experimental import pallas as pl
from jax.experimental.pallas import tpu as pltpu

F32 = jnp.float32
BF16 = jnp.bfloat16
EPS = 1e-6
CHUNK = 128
HEAD = 64
ROPE_THETA = 10000.0
POOL_WINDOWS = (2, 4, 8, 16)
LR, B1, B2, ADAM_EPS, WD, STEP = 0.001, 0.9, 0.999, 1e-08, 0.01, 10
MESH = pl.DeviceIdType.MESH
VMEM_LIMIT = 56 * 1024 * 1024

NAMES = ['x', 'mem', 'positions', 'mem_norm_g', 'mix_pre_g', 'mix_post_g', 'w_in', 'gm_v_g', 'gm_w_s', 'gm_b_s',
         'pool_w', 'pool_scale', 'attn_sinks', 'w_o', 'x_pre_g', 'x_post_g', 'w_xq', 'w_xkv', 'w_xo', 'ffn_pre_g',
         'ffn_post_g', 'w_gate_up', 'w_down']
WEIGHTS = NAMES[3:]
BIG = ['w_in', 'w_o', 'w_xq', 'w_xkv', 'w_xo', 'w_gate_up', 'w_down']
BIG_AXIS = {'w_in': 2, 'w_o': 1, 'w_xq': 1, 'w_xkv': 2, 'w_xo': 1, 'w_gate_up': 2, 'w_down': 1}
SMALL = [n for n in WEIGHTS if n not in BIG]

NN = (((1,), (0,)), ((), ()))
NT = (((1,), (1,)), ((), ()))
TN = (((0,), (0,)), ((), ()))


def _dot(a, b, dims=NN):
    return lax.dot_general(a, b, dims, preferred_element_type=F32)


def _params(sem):
    return pltpu.CompilerParams(dimension_semantics=sem, vmem_limit_bytes=VMEM_LIMIT)


def _rows_tile(rows, limit=256):
    return max(t for t in range(16, limit + 1, 16) if rows % t == 0)


def _mm(name, a, a_spec, b, b_spec, dims, grid, nk, out_shape, out_spec, add=None, add_spec=None, dep=None):
    acc_shape = out_spec.block_shape

    def body(*refs):
        a_ref, b_ref = refs[0], refs[1]
        pos = 2
        add_ref = None
        if add is not None:
            add_ref = refs[pos]
            pos += 1
        if dep is not None:
            pos += 1
        o_ref = refs[pos]
        part = _dot(a_ref[...].astype(BF16), b_ref[...].astype(BF16), dims)
        if nk == 1:
            if add_ref is not None:
                part = part + add_ref[...]
            o_ref[...] = part.astype(o_ref.dtype)
        else:
            acc_ref = refs[pos + 1]
            k = pl.program_id(2)

            @pl.when(k == 0)
            def _():
                acc_ref[...] = part if add_ref is None else part + add_ref[...]

            @pl.when(k > 0)
            def _():
                acc_ref[...] += part

            @pl.when(k == nk - 1)
            def _():
                o_ref[...] = acc_ref[...].astype(o_ref.dtype)

    ops, specs = [a, b], [a_spec, b_spec]
    if add is not None:
        ops.append(add)
        specs.append(add_spec)
    if dep is not None:
        ops.append(dep)
        specs.append(pl.BlockSpec((1, 1), lambda i, j, k: (0, 0)))
    return pl.pallas_call(
        body, name=name, grid=grid, in_specs=specs, out_specs=out_spec, out_shape=out_shape,
        scratch_shapes=[pltpu.VMEM(acc_shape, F32)] if nk > 1 else [],
        compiler_params=_params(("parallel", "parallel", "arbitrary")),
    )(*ops)


def _wspec(block, layer, fn):
    return pl.BlockSpec((None,) + block, lambda i, j, k: (layer,) + fn(i, j, k))


def _mm_nn(name, a, w, layer, *, tm, tn, tk, out_dtype, n0=0, n=None, k0=0):
    m, kk = a.shape
    n = w.shape[2] if n is None else n
    tm = min(tm, m)
    nk = kk // tk
    return _mm(name, a, pl.BlockSpec((tm, tk), lambda i, j, k: (i, k)),
               w, _wspec((tk, tn), layer, lambda i, j, k: (k + k0 // tk, j + n0 // tn)), NN,
               (m // tm, n // tn, nk), nk, jax.ShapeDtypeStruct((m, n), out_dtype),
               pl.BlockSpec((tm, tn), lambda i, j, k: (i, j)))


def _mm_nt(name, a, w, layer, *, tm, tn, tk, out_dtype, k0=0, add=None, dep=None):
    m, kk = a.shape
    n = w.shape[1]
    tm = min(tm, m)
    nk = kk // tk
    ospec = pl.BlockSpec((tm, tn), lambda i, j, k: (i, j))
    return _mm(name, a, pl.BlockSpec((tm, tk), lambda i, j, k: (i, k)),
               w, _wspec((tn, tk), layer, lambda i, j, k: (j, k + k0 // tk)), NT,
               (m // tm, n // tn, nk), nk, jax.ShapeDtypeStruct((m, n), out_dtype), ospec,
               add=add, add_spec=ospec if add is not None else None, dep=dep)


def _mm_tn(name, a, b, *, tm, tn, tk):
    kk, m = a.shape
    n = b.shape[1]
    tk = min(tk, kk)
    nk = kk // tk
    return _mm(name, a, pl.BlockSpec((tk, tm), lambda i, j, k: (k, i)),
               b, pl.BlockSpec((tk, tn), lambda i, j, k: (k, j)), TN,
               (m // tm, n // tn, nk), nk, jax.ShapeDtypeStruct((m, n), BF16),
               pl.BlockSpec((tm, tn), lambda i, j, k: (i, j)))


def _mm_rows(name, a, w, layer, mode, *, tm, tk, k0=0, rows_in=(), params=(), rows_out=(), n_sums=0, epilogue):
    m, kk = a.shape
    n = w.shape[2] if mode == 'nn' else w.shape[1]
    nk = kk // tk
    nr, npar, no = len(rows_in), len(params), len(rows_out)

    def body(*refs):
        a_ref, w_ref = refs[0], refs[1]
        rin = refs[2:2 + nr]
        par = refs[2 + nr:2 + nr + npar]
        outs = refs[2 + nr + npar:2 + nr + npar + no]
        sums = refs[2 + nr + npar + no:2 + nr + npar + no + n_sums]
        i, k = pl.program_id(0), pl.program_id(1)
        part = _dot(a_ref[...], w_ref[...], NN if mode == 'nn' else NT)

        def finish(acc):
            res, sm = epilogue(acc, [r[...] for r in rin], [p[...] for p in par])
            for r, v in zip(outs, res):
                r[...] = v.astype(r.dtype)

            @pl.when(i == 0)
            def _():
                for r in sums:
                    r[...] = jnp.zeros_like(r)

            for r, v in zip(sums, sm):
                r[...] += v

        if nk == 1:
            finish(part)
        else:
            acc_ref = refs[-1]

            @pl.when(k == 0)
            def _():
                acc_ref[...] = part

            @pl.when(k > 0)
            def _():
                acc_ref[...] += part

            @pl.when(k == nk - 1)
            def _():
                finish(acc_ref[...])

    if mode == 'nn':
        wspec = pl.BlockSpec((None, tk, n), lambda i, k: (layer, k + k0 // tk, 0))
    else:
        wspec = pl.BlockSpec((None, n, tk), lambda i, k: (layer, 0, k + k0 // tk))
    rowblk = pl.BlockSpec((tm, n), lambda i, k: (i, 0))
    one = pl.BlockSpec((1, n), lambda i, k: (0, 0))
    return pl.pallas_call(
        body, name=name, grid=(m // tm, nk),
        in_specs=[pl.BlockSpec((tm, tk), lambda i, k: (i, k)), wspec] + [rowblk] * nr + [one] * npar,
        out_specs=[rowblk] * no + [one] * n_sums,
        out_shape=[jax.ShapeDtypeStruct((m, n), dt) for dt in rows_out] +
                  [jax.ShapeDtypeStruct((1, n), F32)] * n_sums,
        scratch_shapes=[pltpu.VMEM((tm, n), F32)] if nk > 1 else [],
        compiler_params=_params(("arbitrary", "arbitrary")),
    )(a, w, *rows_in, *params)


def _rstd(x):
    return lax.rsqrt(jnp.mean(x * x, axis=-1, keepdims=True) + EPS)


def _norm_back(xin, g, dy):
    r = _rstd(xin)
    xh = xin * r
    dyg = dy * g
    return r * (dyg - xh * jnp.mean(dyg * xh, axis=-1, keepdims=True)), jnp.sum(dy * xh, axis=0, keepdims=True)


def _post_pre_rows(y, rows, pars):
    xn = rows[0] + y * _rstd(y) * pars[0]
    return [y, xn, xn * _rstd(xn) * pars[1]], []


def _make_loss_rows(d):
    def fn(y, rows, pars):
        x, tgt = rows
        err = x + y * _rstd(y) * pars[0] - tgt
        dout = err * (1.0 / d)
        dy, dg = _norm_back(y, pars[0], dout)
        lsum = 0.5 * jnp.sum(jnp.mean(err * err, axis=-1, keepdims=True), axis=0, keepdims=True)
        return [dout, dy], [dg, jnp.broadcast_to(lsum, dg.shape)]
    return fn


def _bwd_rows(dh, rows, pars):
    xin, resid, yprev = rows
    dxa, dg_pre = _norm_back(xin, pars[0], dh)
    dx = resid + dxa
    dyp, dg_post = _norm_back(yprev.astype(F32), pars[1], dx)
    return [dx, dyp], [dg_pre, dg_post]


def _bwd_rows_first(dh, rows, pars):
    xin, resid = rows
    dxa, dg_pre = _norm_back(xin, pars[0], dh)
    return [resid + dxa], [dg_pre]


def _row(d):
    return pl.BlockSpec((1, d), lambda i: (0, 0))


def _prenorm(name, x, g, tm):
    m, d = x.shape

    def body(x_ref, g_ref, o_ref):
        xv = x_ref[...]
        o_ref[...] = (xv * _rstd(xv) * g_ref[...]).astype(BF16)

    blk = pl.BlockSpec((tm, d), lambda i: (i, 0))
    return pl.pallas_call(body, name=name, grid=(m // tm,), in_specs=[blk, _row(d)], out_specs=blk,
                          out_shape=jax.ShapeDtypeStruct((m, d), BF16), compiler_params=_params(("parallel",)))(x, g)


def _norm_bwd(name, xin, g, dy, resid, out_dtype, tm):
    m, d = xin.shape

    def body(*refs):
        if resid is None:
            x_ref, g_ref, dy_ref, dx_ref, dg_ref = refs
        else:
            x_ref, g_ref, dy_ref, r_ref, dx_ref, dg_ref = refs
        xv = x_ref[...]
        r = _rstd(xv)
        xh = xv * r
        dyv = dy_ref[...].astype(F32)
        dyg = dyv * g_ref[...]
        dx = r * (dyg - xh * jnp.mean(dyg * xh, axis=-1, keepdims=True))
        if resid is not None:
            dx = dx + r_ref[...]
        dx_ref[...] = dx.astype(dx_ref.dtype)

        @pl.when(pl.program_id(0) == 0)
        def _():
            dg_ref[...] = jnp.zeros_like(dg_ref)

        dg_ref[...] += jnp.sum(dyv * xh, axis=0, keepdims=True)

    blk = pl.BlockSpec((tm, d), lambda i: (i, 0))
    ops = [xin, g, dy] + ([] if resid is None else [resid])
    specs = [blk, _row(d), blk] + ([] if resid is None else [blk])
    return pl.pallas_call(
        body, name=name, grid=(m // tm,), in_specs=specs, out_specs=[blk, _row(d)],
        out_shape=[jax.ShapeDtypeStruct((m, d), out_dtype), jax.ShapeDtypeStruct((1, d), F32)],
        compiler_params=_params(("arbitrary",)))(*ops)


def _adamw(name, g, w, m, v, tr):
    rows, cols = g.shape
    c1 = 1.0 - B1 ** STEP
    c2 = 1.0 - B2 ** STEP

    def body(g_ref, w_ref, m_ref, v_ref, d_ref, mo_ref, vo_ref):
        gv = g_ref[...]
        mn = B1 * m_ref[...] + (1.0 - B1) * gv
        vn = B2 * v_ref[...] + (1.0 - B2) * (gv * gv)
        mo_ref[...] = mn
        vo_ref[...] = vn
        d_ref[...] = -LR * ((mn / c1) / (jnp.sqrt(vn / c2) + ADAM_EPS) + WD * w_ref[...])

    blk = pl.BlockSpec((tr, cols), lambda i: (i, 0))
    sd = jax.ShapeDtypeStruct((rows, cols), F32)
    return pl.pallas_call(body, name=name, grid=(rows // tr,), in_specs=[blk] * 4, out_specs=[blk] * 3,
                          out_shape=[sd, sd, sd], compiler_params=_params(("parallel",)))(g, w, m, v)


def _gelu_parts(x):
    c = 0.7978845608028654
    t = jnp.tanh(c * (x + 0.044715 * (x * x * x)))
    return 0.5 * x * (1.0 + t), t


def _gelu_grad(x, t):
    c = 0.7978845608028654
    return 0.5 * (1.0 + t) + 0.5 * x * (1.0 - t * t) * (c * (1.0 + 3.0 * 0.044715 * x * x))


def _rot_half(x):
    ax = x.ndim - 1
    w = x.shape[ax]
    lane = lax.broadcasted_iota(jnp.int32, x.shape, ax)
    return jnp.where((lane & 63) < 32, pltpu.roll(x, w - 32, ax), pltpu.roll(x, 32, ax))


def _group_mean(x, ones_bd):
    hi = x.astype(BF16)
    lo = (x - hi.astype(F32)).astype(BF16)
    return (_dot(hi, ones_bd) + _dot(lo, ones_bd)) * (1.0 / HEAD)


def _gating(gel, gv, ws_ref, bt, ones_bd, mix_s):
    u = gel[:, :256]
    v = gel[:, 256:]
    r = lax.rsqrt(_group_mean(v * v, ones_bd) + EPS)
    xh = v * r
    vn = (xh * gv).astype(BF16)
    row = lax.broadcasted_iota(jnp.int32, (CHUNK, CHUNK), 0)
    col = lax.broadcasted_iota(jnp.int32, (CHUNK, CHUNK), 1)
    causal = col <= row
    wcs = []
    for g in range(4):
        wc = jnp.where(causal, ws_ref[g], 0.0).astype(BF16)
        wcs.append(wc)
        mix_s[:, HEAD * g:HEAD * (g + 1)] = _dot(wc, vn[:, HEAD * g:HEAD * (g + 1)]) + bt[:, g:g + 1]
    return u, r, xh, vn, wcs, causal, mix_s[...]


def _lane_select(lane, vals):
    return jnp.where(lane < 64, vals[0], jnp.where(lane < 128, vals[1], jnp.where(lane < 192, vals[2], vals[3])))


def _pool_fwd(pc, pp, ci):
    ext = jnp.concatenate([pp, pc], axis=0)
    s2 = ext + pltpu.roll(ext, 1, 0)
    s4 = s2 + pltpu.roll(s2, 2, 0)
    s8 = s4 + pltpu.roll(s4, 4, 0)
    s16 = s8 + pltpu.roll(s8, 8, 0)
    t1 = ci * CHUNK + lax.broadcasted_iota(jnp.int32, (CHUNK, 1), 0) + 1
    lane = lax.broadcasted_iota(jnp.int32, (1, 256), 1)
    cnt = _lane_select(lane, [jnp.minimum(t1, w).astype(F32) for w in POOL_WINDOWS])
    ssel = _lane_select(lane, [s[CHUNK:] for s in (s2, s4, s8, s16)])
    return ssel / cnt - pc, cnt, lane


def _attn_prep(zc, zpkv, cc, sc, cp, sp, ci):
    q = zc[:, 768:1280]
    kc = zc[:, 1280:1408]
    vc = zc[:, 1408:1536]
    kp = zpkv[:, :128]
    vp = zpkv[:, 128:]
    qr = q * jnp.concatenate([cc] * 4, axis=1) + _rot_half(q) * jnp.concatenate([sc] * 4, axis=1)
    krc = kc * cc + _rot_half(kc) * sc
    krp = kp * cp + _rot_half(kp) * sp
    kband = jnp.concatenate([krp, krc], axis=0)
    vband = jnp.concatenate([vp, vc], axis=0)
    key = lax.broadcasted_iota(jnp.int32, (2 * CHUNK, 4 * CHUNK), 0)
    t = lax.broadcasted_iota(jnp.int32, (2 * CHUNK, 4 * CHUNK), 1) & (CHUNK - 1)
    valid = ((key < CHUNK) & (key > t) & (ci > 0)) | ((key >= CHUNK) & (key - CHUNK <= t))
    return qr, kband, vband, valid


SCALE = HEAD ** -0.5


def _stack_heads(x, base, hk):
    return jnp.concatenate([x[:, base + HEAD * (4 * hk + i):base + HEAD * (4 * hk + i + 1)] for i in range(4)], axis=0)


def _sink_row(snk, hk):
    lane = lax.broadcasted_iota(jnp.int32, (1, 4 * CHUNK), 1)
    s = [snk[:, 4 * hk + i:4 * hk + i + 1] for i in range(4)]
    return jnp.where(lane < CHUNK, s[0], jnp.where(lane < 2 * CHUNK, s[1], jnp.where(lane < 3 * CHUNK, s[2], s[3])))


def _group_probs(kh, q4, valid, sink4):
    s = jnp.where(valid, _dot(kh, q4, NT), -1e30)
    mx = jnp.maximum(jnp.max(s, axis=0, keepdims=True), sink4)
    e = jnp.exp(s - mx)
    es = jnp.exp(sink4 - mx)
    inv = 1.0 / (jnp.sum(e, axis=0, keepdims=True) + es)
    return e * inv, es * inv


def _mixer_specs(nb, rev):
    def cur(i):
        return nb - 1 - i if rev else i

    def prev(i):
        return jnp.maximum(cur(i) - 1, 0)

    full = lambda shape: pl.BlockSpec(shape, lambda i: (0,) * len(shape))
    specs = [
        pl.BlockSpec((CHUNK, 1536), lambda i: (cur(i), 0)),
        pl.BlockSpec((CHUNK, 256), lambda i: (prev(i), 2)),
        pl.BlockSpec((CHUNK, 256), lambda i: (prev(i), 5)),
        pl.BlockSpec((CHUNK, 128), lambda i: (cur(i), 0)),
        pl.BlockSpec((CHUNK, 128), lambda i: (cur(i), 0)),
        pl.BlockSpec((CHUNK, 128), lambda i: (prev(i), 0)),
        pl.BlockSpec((CHUNK, 128), lambda i: (prev(i), 0)),
        full((1, 256)), full((4, CHUNK, CHUNK)), full((CHUNK, 4)), full((256, 256)), full((1, 256)), full((1, 8)),
        full((256, 256)),
    ]
    return specs, cur


def _ones_bd():
    g = lax.broadcasted_iota(jnp.int32, (256, 256), 0) // HEAD == lax.broadcasted_iota(jnp.int32, (256, 256), 1) // HEAD
    return g.astype(BF16)


def _mixer_fwd(name, z, cosq, sinq, gv, ws, bt, pw, psc, snk):
    s = z.shape[0]
    nb = s // CHUNK
    specs, _ = _mixer_specs(nb, False)

    def body(zc_ref, zpp_ref, zpkv_ref, cq_ref, sq_ref, cp_ref, sp_ref, gv_ref, ws_ref, bt_ref, pw_ref, psc_ref,
             snk_ref, bd_ref, o_ref, mix_s):
        ci = pl.program_id(0)
        zc = zc_ref[...]
        gel, _ = _gelu_parts(zc[:, :512])
        u, _, _, _, _, _, mixed = _gating(gel, gv_ref[...], ws_ref, bt_ref[...], bd_ref[...], mix_s)
        o_ref[:, :256] = (u * mixed).astype(BF16)
        pp = jnp.where(ci > 0, zpp_ref[...], 0.0)
        pooled, _, _ = _pool_fwd(zc[:, 512:768], pp, ci)
        mp = _dot(pooled.astype(BF16), pw_ref[...].astype(BF16))
        o_ref[:, 256:512] = (mp * psc_ref[...]).astype(BF16)
        qr, kband, vband, valid = _attn_prep(zc, zpkv_ref[...], cq_ref[...], sq_ref[...], cp_ref[...], sp_ref[...], ci)
        snkv = snk_ref[...]
        kb = kband.astype(BF16)
        vt = vband.T
        ots = []
        for hk in range(2):
            q4 = (_stack_heads(qr, 0, hk) * SCALE).astype(BF16)
            p, _ = _group_probs(kb[:, HEAD * hk:HEAD * (hk + 1)], q4, valid, _sink_row(snkv, hk))
            ots.append(_dot(vt[HEAD * hk:HEAD * (hk + 1), :].astype(BF16), p.astype(BF16)))
        o = jnp.concatenate(ots, axis=0).T
        for hk in range(2):
            for i in range(4):
                h = 4 * hk + i
                o_ref[:, 512 + HEAD * h:512 + HEAD * (h + 1)] = o[CHUNK * i:CHUNK * (i + 1),
                                                                  HEAD * hk:HEAD * (hk + 1)].astype(BF16)

    return pl.pallas_call(
        body, name=name, grid=(nb,), in_specs=specs, out_specs=pl.BlockSpec((CHUNK, 1024), lambda i: (i, 0)),
        out_shape=jax.ShapeDtypeStruct((s, 1024), BF16), scratch_shapes=[pltpu.VMEM((CHUNK, 256), F32)],
        compiler_params=_params(("parallel",)),
    )(z, z, z, cosq, sinq, cosq, sinq, gv, ws, bt, pw, psc, snk, _ones_bd())


def _mixer_bwd(name, z, dabc, cosq, sinq, gv, ws, bt, pw, psc, snk):
    s = z.shape[0]
    nb = s // CHUNK
    specs, cur = _mixer_specs(nb, True)
    specs = specs + [pl.BlockSpec((CHUNK, 1024), lambda i: (cur(i), 0))]
    full = lambda shape: pl.BlockSpec(shape, lambda i: (0,) * len(shape))
    acc_shapes = [(1, 256), (4, CHUNK, CHUNK), (CHUNK, 4), (256, 256), (1, 256), (1, 8)]

    def body(zc_ref, zpp_ref, zpkv_ref, cq_ref, sq_ref, cp_ref, sp_ref, gv_ref, ws_ref, bt_ref, pw_ref, psc_ref,
             snk_ref, bd_ref, dabc_ref, dz_ref, dgv_ref, dws_ref, dbt_ref, dpw_ref, dpsc_ref, dsnk_ref,
             cpool, ck, cv, dq_s, dkv_s, mix_s, dvn_s):
        step = pl.program_id(0)
        ci = nb - 1 - step

        @pl.when(step == 0)
        def _():
            for r in (dgv_ref, dws_ref, dbt_ref, dpw_ref, dpsc_ref, dsnk_ref, cpool, ck, cv):
                r[...] = jnp.zeros_like(r)

        zc = zc_ref[...]
        dabc = dabc_ref[...]
        zg = zc[:, :512]
        gel, th = _gelu_parts(zg)
        gp = _gelu_grad(zg, th)
        gvv = gv_ref[...]
        bd = bd_ref[...]
        u, r, xh, vn, wcs, causal, mixed = _gating(gel, gvv, ws_ref, bt_ref[...], bd, mix_s)
        da = dabc[:, :256]
        dm = da * u
        dmb = dm.astype(BF16)
        lane4 = lax.broadcasted_iota(jnp.int32, (CHUNK, 4), 1)
        dbt = jnp.zeros((CHUNK, 4), F32)
        for g in range(4):
            lo, hi = HEAD * g, HEAD * (g + 1)
            dws_ref[g] += jnp.where(causal, _dot(dmb[:, lo:hi], vn[:, lo:hi], NT), 0.0)
            dbt = dbt + jnp.where(lane4 == g, jnp.sum(dm[:, lo:hi], axis=-1, keepdims=True), 0.0)
            dvn_s[:, lo:hi] = _dot(wcs[g], dmb[:, lo:hi], TN)
        dbt_ref[...] += dbt
        dvn = dvn_s[...]
        dgv_ref[...] += jnp.sum(dvn * xh, axis=0, keepdims=True)
        dxh = dvn * gvv
        dvg = r * (dxh - xh * _group_mean(dxh * xh, bd))
        dz_ref[:, :256] = (da * mixed * gp[:, :256]).astype(BF16)
        dz_ref[:, 256:512] = (dvg * gp[:, 256:]).astype(BF16)
        pc = zc[:, 512:768]
        pp = jnp.where(ci > 0, zpp_ref[...], 0.0)
        pooled, cnt, lane = _pool_fwd(pc, pp, ci)
        pwb = pw_ref[...].astype(BF16)
        pooled_b = pooled.astype(BF16)
        mp = _dot(pooled_b, pwb)
        db = dabc[:, 256:512]
        dpsc_ref[...] += jnp.sum(db * mp, axis=0, keepdims=True)
        dmpb = (db * psc_ref[...]).astype(BF16)
        dpw_ref[...] += _dot(pooled_b, dmpb, TN)
        dpooled = _dot(dmpb, pwb, NT)
        davg = dpooled / cnt
        zero = jnp.zeros((CHUNK, 256), F32)
        d2, d4, d8, d16 = [jnp.concatenate([zero, jnp.where((lane >= 64 * k) & (lane < 64 * (k + 1)), davg, 0.0)],
                                           axis=0) for k in range(4)]
        g8 = d8 + d16 + pltpu.roll(d16, 2 * CHUNK - 8, 0)
        g4 = d4 + g8 + pltpu.roll(g8, 2 * CHUNK - 4, 0)
        g2 = d2 + g4 + pltpu.roll(g4, 2 * CHUNK - 2, 0)
        ge = g2 + pltpu.roll(g2, 2 * CHUNK - 1, 0)
        dz_ref[:, 512:768] = (ge[CHUNK:] - dpooled + cpool[...]).astype(BF16)
        cpool[...] = ge[:CHUNK]
        cc = cq_ref[...]
        sc = sq_ref[...]
        qr, kband, vband, valid = _attn_prep(zc, zpkv_ref[...], cc, sc, cp_ref[...], sp_ref[...], ci)
        snkv = snk_ref[...]
        lane8 = lax.broadcasted_iota(jnp.int32, (1, 8), 1)
        qlane = lax.broadcasted_iota(jnp.int32, (1, 4 * CHUNK), 1)
        dsnk = jnp.zeros((1, 8), F32)
        kb = kband.astype(BF16)
        vb = vband.astype(BF16)
        kt = kband.T * SCALE
        dqts = []
        for hk in range(2):
            kh = kb[:, HEAD * hk:HEAD * (hk + 1)]
            q4 = (_stack_heads(qr, 0, hk) * SCALE).astype(BF16)
            do4 = _stack_heads(dabc, 512, hk).astype(BF16)
            p, ps = _group_probs(kh, q4, valid, _sink_row(snkv, hk))
            dp = _dot(vb[:, HEAD * hk:HEAD * (hk + 1)], do4, NT)
            dd = jnp.sum(p * dp, axis=0, keepdims=True)
            dsink = -ps * dd
            for i in range(4):
                part = jnp.sum(jnp.where((qlane >= CHUNK * i) & (qlane < CHUNK * (i + 1)), dsink, 0.0),
                               axis=1, keepdims=True)
                dsnk = dsnk + jnp.where(lane8 == 4 * hk + i, part, 0.0)
            dsb = (p * (dp - dd)).astype(BF16)
            dqts.append(_dot(kt[HEAD * hk:HEAD * (hk + 1), :].astype(BF16), dsb))
            dkv_s[:, HEAD * hk:HEAD * (hk + 1)] = _dot(dsb, q4)
            dkv_s[:, 128 + HEAD * hk:128 + HEAD * (hk + 1)] = _dot(p.astype(BF16), do4)
        dq4 = jnp.concatenate(dqts, axis=0).T
        for hk in range(2):
            for i in range(4):
                h = 4 * hk + i
                dq_s[:, HEAD * h:HEAD * (h + 1)] = dq4[CHUNK * i:CHUNK * (i + 1), HEAD * hk:HEAD * (hk + 1)]
        dsnk_ref[...] += dsnk
        dqr = dq_s[...]
        dz_ref[:, 768:1280] = (dqr * jnp.concatenate([cc] * 4, axis=1)
                               + _rot_half(dqr * jnp.concatenate([sc] * 4, axis=1))).astype(BF16)
        dkv = dkv_s[...]
        dkr = dkv[CHUNK:, :128] + ck[...]
        dz_ref[:, 1280:1408] = (dkr * cc + _rot_half(dkr * sc)).astype(BF16)
        dz_ref[:, 1408:1536] = (dkv[CHUNK:, 128:] + cv[...]).astype(BF16)
        ck[...] = dkv[:CHUNK, :128]
        cv[...] = dkv[:CHUNK, 128:]

    return pl.pallas_call(
        body, name=name, grid=(nb,), in_specs=specs,
        out_specs=[pl.BlockSpec((CHUNK, 1536), lambda i: (cur(i), 0))] + [full(a) for a in acc_shapes],
        out_shape=[jax.ShapeDtypeStruct((s, 1536), BF16)] + [jax.ShapeDtypeStruct(a, F32) for a in acc_shapes],
        scratch_shapes=[pltpu.VMEM((CHUNK, 256), F32), pltpu.VMEM((CHUNK, 128), F32), pltpu.VMEM((CHUNK, 128), F32),
                        pltpu.VMEM((CHUNK, 512), F32), pltpu.VMEM((2 * CHUNK, 256), F32),
                        pltpu.VMEM((CHUNK, 256), F32), pltpu.VMEM((CHUNK, 256), F32)],
        compiler_params=_params(("arbitrary",)),
    )(z, z, z, cosq, sinq, cosq, sinq, gv, ws, bt, pw, psc, snk, _ones_bd(), dabc)


def _xattn_probs(qh, kh):
    s = _dot(qh, kh, NT) * (256 ** -0.5)
    e = jnp.exp(s - jnp.max(s, axis=-1, keepdims=True))
    return e * (1.0 / jnp.sum(e, axis=-1, keepdims=True))


def _xattn_fwd(name, q, kv, tq):
    s, d = q.shape
    mlen = kv.shape[0]

    def body(q_ref, kv_ref, o_ref):
        for h in range(4):
            lo, hi = 256 * h, 256 * (h + 1)
            p = _xattn_probs(q_ref[:, lo:hi], kv_ref[:, lo:hi])
            o_ref[:, lo:hi] = _dot(p.astype(BF16), kv_ref[:, d + lo:d + hi]).astype(BF16)

    blk = pl.BlockSpec((tq, d), lambda i: (i, 0))
    return pl.pallas_call(body, name=name, grid=(s // tq,),
                          in_specs=[blk, pl.BlockSpec((mlen, 2 * d), lambda i: (0, 0))], out_specs=blk,
                          out_shape=jax.ShapeDtypeStruct((s, d), BF16), compiler_params=_params(("parallel",)))(q, kv)


def _xattn_bwd(name, q, kv, do, tq):
    s, d = q.shape
    mlen = kv.shape[0]

    def body(q_ref, kv_ref, do_ref, dq_ref, dkv_ref):
        @pl.when(pl.program_id(0) == 0)
        def _():
            dkv_ref[...] = jnp.zeros_like(dkv_ref)

        for h in range(4):
            lo, hi = 256 * h, 256 * (h + 1)
            qh = q_ref[:, lo:hi]
            kh = kv_ref[:, lo:hi]
            vh = kv_ref[:, d + lo:d + hi]
            doh = do_ref[:, lo:hi]
            p = _xattn_probs(qh, kh)
            dp = _dot(doh, vh, NT)
            dsb = (p * (dp - jnp.sum(p * dp, axis=-1, keepdims=True)) * (256 ** -0.5)).astype(BF16)
            dq_ref[:, lo:hi] = _dot(dsb, kh).astype(BF16)
            dkv_ref[:, lo:hi] += _dot(dsb, qh, TN)
            dkv_ref[:, d + lo:d + hi] += _dot(p.astype(BF16), doh, TN)

    blk = pl.BlockSpec((tq, d), lambda i: (i, 0))
    kvb = pl.BlockSpec((mlen, 2 * d), lambda i: (0, 0))
    return pl.pallas_call(
        body, name=name, grid=(s // tq,), in_specs=[blk, kvb, blk], out_specs=[blk, kvb],
        out_shape=[jax.ShapeDtypeStruct((s, d), BF16), jax.ShapeDtypeStruct((mlen, 2 * d), F32)],
        compiler_params=_params(("arbitrary",)))(q, kv, do)


def _sigmoid(x):
    return 0.5 * (1.0 + jnp.tanh(0.5 * x))


def _ffn_up(name, h, wgu, layer, tm, tn):
    s, d = h.shape
    dff = wgu.shape[2] // 2
    nj = dff // tn

    def body(h_ref, wg_ref, wu_ref, g_ref, u_ref, a_ref):
        hv = h_ref[...]
        gate = _dot(hv, wg_ref[...])
        up = _dot(hv, wu_ref[...])
        g_ref[...] = gate.astype(BF16)
        u_ref[...] = up.astype(BF16)
        a_ref[...] = (gate * _sigmoid(gate) * up).astype(BF16)

    ob = pl.BlockSpec((tm, tn), lambda j, i: (i, j))
    sd = jax.ShapeDtypeStruct((s, dff), BF16)
    return pl.pallas_call(
        body, name=name, grid=(nj, s // tm),
        in_specs=[pl.BlockSpec((tm, d), lambda j, i: (i, 0)),
                  pl.BlockSpec((None, d, tn), lambda j, i: (layer, 0, j)),
                  pl.BlockSpec((None, d, tn), lambda j, i: (layer, 0, j + nj))],
        out_specs=[ob, ob, ob], out_shape=[sd, sd, sd], compiler_params=_params(("parallel", "parallel")),
    )(h, wgu, wgu)


def _ffn_act_bwd(name, dfn, wdown, layer, gate, up, tm):
    s, d = dfn.shape
    dff = gate.shape[1]

    def body(df_ref, wd_ref, g_ref, u_ref, o_ref):
        dact = _dot(df_ref[...], wd_ref[...], NT)
        gate = g_ref[...].astype(F32)
        sig = _sigmoid(gate)
        o_ref[:, :dff] = (dact * u_ref[...].astype(F32) * sig * (1.0 + gate * (1.0 - sig))).astype(BF16)
        o_ref[:, dff:] = (dact * gate * sig).astype(BF16)

    gb = pl.BlockSpec((tm, dff), lambda i: (i, 0))
    return pl.pallas_call(
        body, name=name, grid=(s // tm,),
        in_specs=[pl.BlockSpec((tm, d), lambda i: (i, 0)),
                  pl.BlockSpec((None, dff, d), lambda i: (layer, 0, 0)), gb, gb],
        out_specs=pl.BlockSpec((tm, 2 * dff), lambda i: (i, 0)),
        out_shape=jax.ShapeDtypeStruct((s, 2 * dff), BF16), compiler_params=_params(("parallel",)),
    )(dfn, wdown, gate, up)


def _place():
    return lax.axis_index("x"), lax.axis_index("y"), lax.axis_index("c")


def _other_chips(x, y):
    return [(1 - x, y), (x, 1 - y), (1 - x, 1 - y)]


def _region(ref, axis, chip, size):
    start = pl.multiple_of(chip * size, size)
    if axis == 1:
        return ref.at[:, pl.ds(start, size), :]
    return ref.at[:, :, pl.ds(start, size)]


ANY = pl.BlockSpec(memory_space=pl.ANY)


HBM = pl.BlockSpec(memory_space=pltpu.HBM)
SEM = pl.BlockSpec(memory_space=pltpu.SEMAPHORE)
EFFECT = pltpu.SideEffectType.DATAFLOW_SIDE_EFFECTING


def _in_hbm(a):
    return pltpu.with_memory_space_constraint(a, pltpu.HBM)


def _split_start(name, srcs, lands, ncopies, plan, after=None):
    ns, nl = len(srcs), len(lands)
    extra = [] if after is None else [after]

    def body(*refs):
        src, land = refs[:ns], refs[ns:ns + nl]
        send, recv = refs[ns + nl + len(extra)], refs[ns + nl + len(extra) + 1]
        token = refs[-1]
        x, y, c = _place()
        for k, (s_ref, d_ref, peer, _) in enumerate(plan(src, land, x, y, c)):
            pltpu.make_async_remote_copy(src_ref=s_ref, dst_ref=d_ref, send_sem=send.at[k], recv_sem=recv.at[k],
                                         device_id=peer, device_id_type=MESH).start()
        token[...] = jnp.zeros_like(token)

    ops = list(srcs) + list(lands)
    out = pl.pallas_call(
        body, name=name,
        out_shape=(pltpu.SemaphoreType.DMA((ncopies,)), pltpu.SemaphoreType.DMA((ncopies,)),
                   *[pltpu.HBM(a.shape, a.dtype) for a in ops], jax.ShapeDtypeStruct((8, 128), F32)),
        in_specs=(HBM,) * (ns + nl) + (ANY,) * len(extra),
        out_specs=(SEM, SEM) + (HBM,) * (ns + nl) + (pl.BlockSpec(memory_space=pltpu.VMEM),),
        input_output_aliases={i: 2 + i for i in range(ns + nl)},
        compiler_params=pltpu.CompilerParams(has_side_effects=EFFECT),
    )(*[_in_hbm(a) for a in ops], *extra)
    return out[0], out[1], list(out[2:2 + ns]), list(out[2 + ns:2 + ns + nl]), out[-1]


def _split_wait(name, send, recv, srcs, lands, after, plan):
    ns, nl = len(srcs), len(lands)

    def body(*refs):
        src, land = refs[:ns], refs[ns:ns + nl]
        send_ref, recv_ref = refs[ns + nl], refs[ns + nl + 1]
        x, y, c = _place()
        for k, (s_ref, _, _, got) in enumerate(plan(src, land, x, y, c)):
            cp = pltpu.make_async_remote_copy(src_ref=s_ref, dst_ref=got, send_sem=send_ref.at[k],
                                              recv_sem=recv_ref.at[k], device_id=(x, y, c), device_id_type=MESH)
            cp.wait_send()
            cp.wait_recv()

    ops = list(srcs) + list(lands)
    out = pl.pallas_call(
        body, name=name, out_shape=tuple(pltpu.HBM(a.shape, a.dtype) for a in ops),
        in_specs=(HBM,) * (ns + nl) + (SEM, SEM, ANY), out_specs=(HBM,) * (ns + nl),
        input_output_aliases={i: i for i in range(ns + nl)},
        compiler_params=pltpu.CompilerParams(has_side_effects=EFFECT),
    )(*ops, send, recv, after)
    return list(out[:ns]), list(out[ns:])


def _gather_plan(axes, sizes, layer):
    def plan(src, land, x, y, c):
        me = 2 * x + y
        out = []
        for t in range(len(land)):
            mine = _region(land[t], axes[t], me, sizes[t]).at[pl.ds(layer, 1)]
            for px, py in _other_chips(x, y):
                out.append((mine, mine, (px, py, c),
                            _region(land[t], axes[t], 2 * px + py, sizes[t]).at[pl.ds(layer, 1)]))
        return out
    return plan


def _place_own(name, w, axis, chip):
    nl, r, cs = w.shape
    tr = _rows_tile(r)
    nb = r // tr
    full = (nl, 4 * r, cs) if axis == 1 else (nl, r, 4 * cs)

    def body(m_ref, w_ref, o_ref):
        o_ref[...] = w_ref[...].astype(BF16)

    if axis == 1:
        ospec = pl.BlockSpec((None, tr, cs), lambda l, i, m: (l, m[0] * nb + i, 0))
    else:
        ospec = pl.BlockSpec((None, tr, cs), lambda l, i, m: (l, i, m[0]))
    return pl.pallas_call(
        body, name=name,
        grid_spec=pltpu.PrefetchScalarGridSpec(
            num_scalar_prefetch=1, grid=(nl, nb),
            in_specs=[pl.BlockSpec((None, tr, cs), lambda l, i, m: (l, i, 0))], out_specs=ospec),
        out_shape=jax.ShapeDtypeStruct(full, BF16), compiler_params=_params(("parallel", "parallel")),
    )(chip, w)


def _scatter_plan(axes, sizes):
    def plan(src, land, x, y, c):
        out = []
        for t in range(len(src)):
            for k, (px, py) in enumerate(_other_chips(x, y)):
                out.append((_region(src[t], axes[t], 2 * px + py, sizes[t]).at[0], land[t].at[k], (px, py, c),
                            land[t].at[k]))
        return out
    return plan


def _pair_plan(src, land, x, y, c):
    return [(src[t], land[t], (x, y, 1 - c), land[t]) for t in range(len(src))]


def _chip_sum(name, g, slots, axis, chip):
    _, r, cs = slots.shape
    tr = _rows_tile(r)
    nb = r // tr

    def body(m_ref, g_ref, s_ref, o_ref):
        acc = g_ref[...].astype(F32)
        for k in range(3):
            acc = acc + s_ref[k].astype(F32)
        o_ref[...] = acc

    if axis == 1:
        gspec = pl.BlockSpec((tr, cs), lambda i, m: (m[0] * nb + i, 0))
    else:
        gspec = pl.BlockSpec((tr, cs), lambda i, m: (i, m[0]))
    return pl.pallas_call(
        body, name=name,
        grid_spec=pltpu.PrefetchScalarGridSpec(
            num_scalar_prefetch=1, grid=(nb,),
            in_specs=[gspec, pl.BlockSpec((3, tr, cs), lambda i, m: (0, i, 0))],
            out_specs=pl.BlockSpec((tr, cs), lambda i, m: (i, 0))),
        out_shape=jax.ShapeDtypeStruct((r, cs), F32), compiler_params=_params(("parallel",)),
    )(chip, g, slots)


def _pair_adamw(name, mine, theirs, w, m, v, layer, bufs):
    r, cs = mine.shape
    tr = _rows_tile(r)
    c1 = 1.0 - B1 ** STEP
    c2 = 1.0 - B2 ** STEP

    def body(a_ref, b_ref, w_ref, m_ref, v_ref, _g, _d, _m, _v, g_ref, d_ref, mo_ref, vo_ref):
        gv = a_ref[...] + b_ref[...]
        mn = B1 * m_ref[...] + (1.0 - B1) * gv
        vn = B2 * v_ref[...] + (1.0 - B2) * (gv * gv)
        g_ref[...] = gv
        mo_ref[...] = mn
        vo_ref[...] = vn
        d_ref[...] = -LR * ((mn / c1) / (jnp.sqrt(vn / c2) + ADAM_EPS) + WD * w_ref[...])

    blk = pl.BlockSpec((tr, cs), lambda i: (i, 0))
    lay = pl.BlockSpec((None, tr, cs), lambda i: (layer, i, 0))
    return pl.pallas_call(
        body, name=name, grid=(r // tr,), in_specs=[blk, blk, lay, lay, lay] + [ANY] * 4, out_specs=[lay] * 4,
        out_shape=[jax.ShapeDtypeStruct(b.shape, b.dtype) for b in bufs],
        input_output_aliases={5 + i: i for i in range(4)}, compiler_params=_params(("parallel",)),
    )(mine, theirs, w, m, v, *bufs)


def _allreduce_small(p):
    rows = p.shape[0]

    def body(p_ref, o_ref, sib, sums, send, recv):
        x, y, c = _place()
        me = 2 * x + y
        pair = pltpu.make_async_remote_copy(src_ref=p_ref, dst_ref=sib, send_sem=send.at[3], recv_sem=recv.at[3],
                                            device_id=(x, y, 1 - c), device_id_type=MESH)
        pair.start()
        pair.wait()
        sums[me] = (p_ref[...] + sib[...]).astype(BF16)
        cps = [pltpu.make_async_remote_copy(src_ref=sums.at[me], dst_ref=sums.at[me], send_sem=send.at[k],
                                            recv_sem=recv.at[k], device_id=(px, py, c), device_id_type=MESH)
               for k, (px, py) in enumerate(_other_chips(x, y))]
        for cp in cps:
            cp.start()
        for k, (px, py) in enumerate(_other_chips(x, y)):
            slot = sums.at[2 * px + py]
            pltpu.make_async_remote_copy(src_ref=slot, dst_ref=slot, send_sem=send.at[k], recv_sem=recv.at[k],
                                         device_id=(x, y, c), device_id_type=MESH).wait_recv()
        for cp in cps:
            cp.wait_send()
        acc = sums[0].astype(F32)
        for k in range(1, 4):
            acc = acc + sums[k].astype(F32)
        o_ref[...] = acc

    vm = pl.BlockSpec(memory_space=pltpu.VMEM)
    return pl.pallas_call(
        body, name="allreduce_small", in_specs=[vm], out_specs=vm, out_shape=jax.ShapeDtypeStruct(p.shape, F32),
        scratch_shapes=[pltpu.VMEM((rows, 128), F32), pltpu.VMEM((4, rows, 128), BF16),
                        pltpu.SemaphoreType.DMA((4,)), pltpu.SemaphoreType.DMA((4,))],
        compiler_params=pltpu.CompilerParams(vmem_limit_bytes=VMEM_LIMIT),
    )(p)


def _pack(parts):
    flat = []
    for p in parts:
        v = p.reshape(-1).astype(F32)
        flat.append(jnp.pad(v, (0, (-v.shape[0]) % 128)))
    v = jnp.concatenate(flat)
    v = jnp.pad(v, (0, (-v.shape[0]) % (512 * 128)))
    return v.reshape(-1, 128)


def _unpack(buf, like):
    out, r0 = [], 0
    for p in like:
        nelem = 1
        for s in p.shape:
            nelem *= s
        rows = -(-nelem // 128)
        blk = buf[r0:r0 + rows]
        if nelem % 128:
            blk = blk.reshape(-1)[:nelem]
        out.append(blk.reshape(p.shape))
        r0 += rows
    return out


def kernel(x, mem, positions, mem_norm_g, mix_pre_g, mix_post_g, w_in, gm_v_g, gm_w_s, gm_b_s, pool_w, pool_scale, attn_sinks, w_o, x_pre_g, x_post_g, w_xq, w_xkv, w_xo, ffn_pre_g, ffn_post_g, w_gate_up, w_down, loss_target, m_mem_norm_g, m_mix_pre_g, m_mix_post_g, m_w_in, m_gm_v_g, m_gm_w_s, m_gm_b_s, m_pool_w, m_pool_scale, m_attn_sinks, m_w_o, m_x_pre_g, m_x_post_g, m_w_xq, m_w_xkv, m_w_xo, m_ffn_pre_g, m_ffn_post_g, m_w_gate_up, m_w_down, v_mem_norm_g, v_mix_pre_g, v_mix_post_g, v_w_in, v_gm_v_g, v_gm_w_s, v_gm_b_s, v_pool_w, v_pool_scale, v_attn_sinks, v_w_o, v_x_pre_g, v_x_post_g, v_w_xq, v_w_xkv, v_w_xo, v_ffn_pre_g, v_ffn_post_g, v_w_gate_up, v_w_down):
    args = (x, mem, positions, mem_norm_g, mix_pre_g, mix_post_g, w_in, gm_v_g, gm_w_s, gm_b_s, pool_w, pool_scale, attn_sinks, w_o, x_pre_g, x_post_g, w_xq, w_xkv, w_xo, ffn_pre_g, ffn_post_g, w_gate_up, w_down)
    moms_m = (m_mem_norm_g, m_mix_pre_g, m_mix_post_g, m_w_in, m_gm_v_g, m_gm_w_s, m_gm_b_s, m_pool_w, m_pool_scale, m_attn_sinks, m_w_o, m_x_pre_g, m_x_post_g, m_w_xq, m_w_xkv, m_w_xo, m_ffn_pre_g, m_ffn_post_g, m_w_gate_up, m_w_down)
    moms_v = (v_mem_norm_g, v_mix_pre_g, v_mix_post_g, v_w_in, v_gm_v_g, v_gm_w_s, v_gm_b_s, v_pool_w, v_pool_scale, v_attn_sinks, v_w_o, v_x_pre_g, v_x_post_g, v_w_xq, v_w_xkv, v_w_xo, v_ffn_pre_g, v_ffn_post_g, v_w_gate_up, v_w_down)
    P = dict(zip(NAMES, args))
    P['loss_target'] = loss_target
    M = dict(zip(WEIGHTS, moms_m))
    V = dict(zip(WEIGHTS, moms_v))
    depth = w_in.shape[0]
    nbig = len(BIG)
    axes = [BIG_AXIS[n] for n in BIG]
    sizes = [P[n].shape[a] for n, a in zip(BIG, axes)]
    chip = (2 * lax.axis_index("x") + lax.axis_index("y")).astype(jnp.int32).reshape(1)

    groups = [['w_in'], ['w_o', 'w_xq', 'w_xkv', 'w_xo'], ['w_gate_up', 'w_down']]
    units = [(l, g) for l in range(depth) for g in groups]
    unit_of = {(l, n): i for i, (l, names) in enumerate(units) for n in names}
    ax = lambda names: [BIG_AXIS[n] for n in names]
    sz = lambda names: [P[n].shape[BIG_AXIS[n]] for n in names]

    full = {n: _place_own("place_" + n, P[n], BIG_AXIS[n], chip) for n in BIG}
    gathers = []
    prev, tok = None, None
    for i, (l, names) in enumerate(units):
        send, recv, _, land, tok = _split_start("gather_start%d" % i, [], [full[n] for n in names],
                                                3 * len(names), _gather_plan(ax(names), sz(names), l), after=prev)
        full.update(zip(names, land))
        gathers.append((send, recv))
        prev = land[0]
    P['first_dep'] = tok[:1, :1]
    gathered = set()

    def weights_of(l, names, after):
        i = unit_of[(l, names[0])]
        if i not in gathered:
            _, unames = units[i]
            send, recv = gathers[i]
            _, land = _split_wait("gather_wait%d" % i, send, recv, [], [full[n] for n in unames], after,
                                  _gather_plan(ax(unames), sz(unames), l))
            full.update(zip(unames, land))
            gathered.add(i)
        return {n: (full[n], l) for n in names}

    outs = {n: [lax.empty(P[n].shape, F32) for _ in range(4)] for n in BIG}
    gunits = [(l, BIG) for l in range(depth - 1, 0, -1)] + [
        (0, g) for g in (['w_gate_up', 'w_down'], ['w_xq', 'w_xkv', 'w_xo'], ['w_o'], ['w_in'])]
    collected, scatters, pairs = {}, {}, {}

    def finish_scatter(i, after):
        _, names = gunits[i]
        send, recv, g_l, slots = scatters.pop(i)
        g_l, slots = _split_wait("scatter_wait%d" % i, send, recv, g_l, slots, after,
                                 _scatter_plan(ax(names), sz(names)))
        mine = [_chip_sum("chip_sum_" + n, g.reshape(g.shape[1:]), sl, BIG_AXIS[n], chip)
                for n, g, sl in zip(names, g_l, slots)]
        send, recv, mine, theirs, tok = _split_start("pair_start%d" % i, mine, [lax.empty(a.shape, F32) for a in mine],
                                                     len(names), _pair_plan)
        pairs[i] = (send, recv, mine, theirs)
        return tok[:1, :1]

    def finish_pair(i, after):
        l, names = gunits[i]
        send, recv, mine, theirs = pairs.pop(i)
        mine, theirs = _split_wait("pair_wait%d" % i, send, recv, mine, theirs, after, _pair_plan)
        for n, a, b in zip(names, mine, theirs):
            outs[n] = _pair_adamw("adamw_" + n, a, b, P[n], M[n], V[n], l, outs[n])

    calls = {'n': 0}
    lag = 4

    def grads_of(l, g_part, after):
        collected.update({(l, n): g for n, g in g_part.items()})
        calls['n'] += 1
        now = calls['n']
        tok = jnp.zeros((1, 1), F32)
        for i, (ul, names) in enumerate(gunits):
            if ul != l or ('started', i) in collected or any((l, n) not in collected for n in names):
                continue
            collected[('started', i)] = now
            srcs = [collected[(l, n)].reshape((1,) + collected[(l, n)].shape) for n in names]
            send, recv, srcs, slots, t = _split_start("scatter_start%d" % i, srcs,
                                                      [lax.empty((3,) + P[n].shape[1:], BF16) for n in names],
                                                      3 * len(names), _scatter_plan(ax(names), sz(names)))
            scatters[i] = (send, recv, srcs, slots)
            tok = tok + t[:1, :1]
        for i in sorted(pairs):
            if collected[('summed', i)] + lag <= now:
                finish_pair(i, after)
        for i in sorted(scatters):
            if collected[('started', i)] + lag <= now:
                tok = tok + finish_scatter(i, after)
                collected[('summed', i)] = now
        return tok

    loss_part, dx, small_g = _fwd_bwd(P, weights_of, grads_of)
    loss = lax.psum(loss_part[0, 0], ("x", "y", "c"))
    grad_x = dx.reshape(x.shape)

    small_like = [P[n] for n in SMALL]
    gsum = _allreduce_small(_pack(small_g))
    dlt, mn, vn = _adamw("adamw_small", gsum, _pack(small_like), _pack([M[n] for n in SMALL]),
                         _pack([V[n] for n in SMALL]), 512)
    grads, deltas, new_m, new_v = {}, {}, {}, {}
    for name_map, buf in ((grads, gsum), (deltas, dlt), (new_m, mn), (new_v, vn)):
        for n, a in zip(SMALL, _unpack(buf, small_like)):
            name_map[n] = a

    for i in sorted(pairs):
        finish_pair(i, dlt)
    for i in sorted(scatters):
        finish_scatter(i, dlt)
    for i in sorted(pairs):
        finish_pair(i, dlt)
    for n in BIG:
        grads[n], deltas[n], new_m[n], new_v[n] = outs[n]

    return (loss, grad_x, *[grads[n] for n in WEIGHTS], *[deltas[n] for n in WEIGHTS],
            *[new_m[n] for n in WEIGHTS], *[new_v[n] for n in WEIGHTS])


def _fwd_bwd(P, weights_of, grads_of):
    (x, mem, positions, mem_norm_g, mix_pre_g, mix_post_g, w_in, gm_v_g, gm_w_s, gm_b_s, pool_w, pool_scale, attn_sinks,
     w_o, x_pre_g, x_post_g, w_xq, w_xkv, w_xo, ffn_pre_g, ffn_post_g, w_gate_up, w_down) = [P[n] for n in NAMES]
    x0 = x[0]
    s, d = x0.shape
    depth = w_in.shape[0]
    tgt = P['loss_target'][0]
    tmn = 256
    tmr = min(512, s)
    tkw = min(2048, s)

    half = HEAD // 2
    inv = ROPE_THETA ** (-jnp.arange(half, dtype=F32) / half)
    ang = positions[0].astype(F32)[:, None] * inv
    cos, sin = jnp.cos(ang), jnp.sin(ang)
    cosq = jnp.tile(jnp.concatenate([cos, cos], axis=-1), (1, 2))
    sinq = jnp.tile(jnp.concatenate([-sin, sin], axis=-1), (1, 2))

    row = lambda a, l: a[l].reshape(1, -1)
    memn = _prenorm("mem_norm", mem[0], mem_norm_g.reshape(1, d), tmn)
    pw_bd = []
    for l in range(depth):
        bd = jnp.zeros((256, 256), F32)
        for g in range(4):
            bd = lax.dynamic_update_slice(bd, pool_w[l, g], (64 * g, 64 * g))
        pw_bd.append(bd)

    saved = []
    xc = x0
    h = _prenorm("pre_norm0", x0, row(mix_pre_g, 0) + P['first_dep'], tmn)
    for l in range(depth):
        W = weights_of(l, ['w_in'], xc)
        sv = {'x0': xc, 'h1': h}
        z = _mm_nn("fwd_w_in", h, *W['w_in'], tm=1024, tn=512, tk=d, out_dtype=F32)
        abc = _mixer_fwd("mixer_fwd", z, cosq, sinq, row(gm_v_g, l), gm_w_s[l], gm_b_s[l].T, pw_bd[l],
                         row(pool_scale, l), row(attn_sinks, l))
        W.update(weights_of(l, ['w_o'], z))
        mix, xc, h = _mm_rows("fwd_w_o", abc, *W['w_o'], 'nn', tm=tmr, tk=d, rows_in=[xc],
                              params=[row(mix_post_g, l), row(x_pre_g, l)], rows_out=[BF16, F32, BF16],
                              epilogue=_post_pre_rows)
        sv.update(z=z, abc=abc, mix=mix, x1=xc, h2=h)
        W.update(weights_of(l, ['w_xq', 'w_xkv', 'w_xo'], xc))
        q = _mm_nn("fwd_w_xq", h, *W['w_xq'], tm=1024, tn=512, tk=d, out_dtype=BF16)
        kv = _mm_nn("fwd_w_xkv", memn, *W['w_xkv'], tm=256, tn=512, tk=d, out_dtype=BF16)
        o = _xattn_fwd("xattn_fwd", q, kv, 512)
        xo, xc, h = _mm_rows("fwd_w_xo", o, *W['w_xo'], 'nn', tm=tmr, tk=d, rows_in=[xc],
                             params=[row(x_post_g, l), row(ffn_pre_g, l)], rows_out=[BF16, F32, BF16],
                             epilogue=_post_pre_rows)
        sv.update(q=q, kv=kv, o=o, xo=xo, x2=xc, h3=h)
        W.update(weights_of(l, ['w_gate_up', 'w_down'], xc))
        dff = W['w_down'][0].shape[1]
        gate, up, act = _ffn_up("ffn_up", h, *W['w_gate_up'], 512, dff // 2)
        sv.update(gate=gate, up=up, act=act)
        if l + 1 < depth:
            f, xc, h = _mm_rows("fwd_w_down", act, *W['w_down'], 'nn', tm=tmr, tk=dff, rows_in=[xc],
                                params=[row(ffn_post_g, l), row(mix_pre_g, l + 1)], rows_out=[BF16, F32, BF16],
                                epilogue=_post_pre_rows)
            sv.update(f=f)
        saved.append(sv)
    gs = {n: [None] * depth for n in SMALL if n != 'mem_norm_g'}
    dx, dfn, gs['ffn_post_g'][depth - 1], loss_part = _mm_rows(
        "fwd_w_down_loss", saved[-1]['act'], *W['w_down'], 'nn', tm=tmr, tk=dff, rows_in=[xc, tgt],
        params=[row(ffn_post_g, depth - 1)], rows_out=[F32, BF16], n_sums=2, epilogue=_make_loss_rows(d))

    dmemn = None
    tok = jnp.zeros((1, 1), F32)
    for l in reversed(range(depth)):
        sv, W, G = saved[l], weights_of(l, BIG, dx), {}
        G['w_down'] = _mm_tn("dw_down", sv['act'], dfn, tm=dff // 2, tn=d, tk=tkw)
        dgu = _ffn_act_bwd("ffn_act_bwd", dfn, *W['w_down'], sv['gate'], sv['up'], 256)
        G['w_gate_up'] = _mm_tn("dw_gate_up", sv['h3'], dgu, tm=d, tn=dff // 2, tk=tkw)
        dx, dxo, gs['ffn_pre_g'][l], gs['x_post_g'][l] = _mm_rows(
            "bwd_w_gate_up", dgu, *W['w_gate_up'], 'nt', tm=tmr, tk=dff, rows_in=[sv['x2'], dx, sv['xo']],
            params=[row(ffn_pre_g, l) + tok, row(x_post_g, l)], rows_out=[F32, BF16], n_sums=2, epilogue=_bwd_rows)
        tok = grads_of(l, {n: G[n] for n in ('w_gate_up', 'w_down')}, dx)
        G['w_xo'] = _mm_tn("dw_xo", sv['o'], dxo, tm=d, tn=d, tk=tkw)
        do = _mm_nt("bwd_w_xo", dxo, *W['w_xo'], tm=1024, tn=512, tk=d, out_dtype=BF16, dep=tok)
        dq, dkv = _xattn_bwd("xattn_bwd", sv['q'], sv['kv'], do, 512)
        dkv = dkv.astype(BF16)
        G['w_xkv'] = _mm_tn("dw_xkv", memn, dkv, tm=d, tn=d, tk=mem.shape[1])
        dmemn = _mm_nt("bwd_w_xkv", dkv, *W['w_xkv'], tm=mem.shape[1], tn=512, tk=2 * d, out_dtype=F32, add=dmemn)
        G['w_xq'] = _mm_tn("dw_xq", sv['h2'], dq, tm=d, tn=d, tk=tkw)
        dx, dmix, gs['x_pre_g'][l], gs['mix_post_g'][l] = _mm_rows(
            "bwd_w_xq", dq, *W['w_xq'], 'nt', tm=tmr, tk=d, rows_in=[sv['x1'], dx, sv['mix']],
            params=[row(x_pre_g, l), row(mix_post_g, l)], rows_out=[F32, BF16], n_sums=2, epilogue=_bwd_rows)
        tok = grads_of(l, {n: G[n] for n in ('w_xq', 'w_xkv', 'w_xo')}, dx)
        G['w_o'] = _mm_tn("dw_o", sv['abc'], dmix, tm=d, tn=d, tk=tkw)
        dabc = _mm_nt("bwd_w_o", dmix, *W['w_o'], tm=1024, tn=512, tk=d, out_dtype=F32, dep=tok)
        tok = grads_of(l, {'w_o': G['w_o']}, dabc)
        dz, dgv, dws, dbt, dpw, dpsc, dsnk = _mixer_bwd(
            "mixer_bwd", sv['z'], dabc, cosq, sinq, row(gm_v_g, l) + tok, gm_w_s[l], gm_b_s[l].T, pw_bd[l],
            row(pool_scale, l), row(attn_sinks, l))
        gs['gm_v_g'][l] = dgv
        gs['gm_w_s'][l] = dws
        gs['gm_b_s'][l] = dbt.T
        gs['pool_w'][l] = jnp.stack([dpw[64 * g:64 * (g + 1), 64 * g:64 * (g + 1)] for g in range(4)])
        gs['pool_scale'][l] = dpsc
        gs['attn_sinks'][l] = dsnk
        G['w_in'] = _mm_tn("dw_in", sv['h1'], dz, tm=d, tn=dz.shape[1], tk=tkw)
        if l > 0:
            dx, dfn, gs['mix_pre_g'][l], gs['ffn_post_g'][l - 1] = _mm_rows(
                "bwd_w_in", dz, *W['w_in'], 'nt', tm=tmr, tk=dz.shape[1], rows_in=[sv['x0'], dx, saved[l - 1]['f']],
                params=[row(mix_pre_g, l), row(ffn_post_g, l - 1)], rows_out=[F32, BF16], n_sums=2,
                epilogue=_bwd_rows)
        else:
            dx, gs['mix_pre_g'][l] = _mm_rows(
                "bwd_w_in_first", dz, *W['w_in'], 'nt', tm=tmr, tk=dz.shape[1], rows_in=[sv['x0'], dx],
                params=[row(mix_pre_g, l)], rows_out=[F32], n_sums=1, epilogue=_bwd_rows_first)
        tok = grads_of(l, {'w_in': G['w_in']}, dx)
    _, dg_mem = _norm_bwd("bwd_mem_norm", mem[0], mem_norm_g.reshape(1, d), dmemn, None, BF16, tmn)
    small_g = []
    for n in SMALL:
        if n == 'mem_norm_g':
            small_g.append(dg_mem.reshape(P[n].shape))
        else:
            small_g.append(jnp.stack([a.reshape(P[n].shape[1:]) for a in gs[n]]))
    return loss_part, dx, small_g
```

```python
import functools

import jax
import jax.numpy as jnp
from jax import lax
from jax.experimental import pallas as pl
from jax.experimental.pallas import tpu as pltpu

F32 = jnp.float32
BF16 = jnp.bfloat16
EPS = 1e-6
CHUNK = 128
HEAD = 64
ROPE_THETA = 10000.0
POOL_WINDOWS = (2, 4, 8, 16)
LR, B1, B2, ADAM_EPS, WD, STEP = 0.001, 0.9, 0.999, 1e-08, 0.01, 10
MESH = pl.DeviceIdType.MESH
VMEM_LIMIT = 56 * 1024 * 1024

NAMES = ['x', 'mem', 'positions', 'mem_norm_g', 'mix_pre_g', 'mix_post_g', 'w_in', 'gm_v_g', 'gm_w_s', 'gm_b_s',
         'pool_w', 'pool_scale', 'attn_sinks', 'w_o', 'x_pre_g', 'x_post_g', 'w_xq', 'w_xkv', 'w_xo', 'ffn_pre_g',
         'ffn_post_g', 'w_gate_up', 'w_down']
WEIGHTS = NAMES[3:]
BIG = ['w_in', 'w_o', 'w_xq', 'w_xkv', 'w_xo', 'w_gate_up', 'w_down']
BIG_AXIS = {'w_in': 2, 'w_o': 1, 'w_xq': 1, 'w_xkv': 2, 'w_xo': 1, 'w_gate_up': 2, 'w_down': 1}
SMALL = [n for n in WEIGHTS if n not in BIG]

NN = (((1,), (0,)), ((), ()))
NT = (((1,), (1,)), ((), ()))
TN = (((0,), (0,)), ((), ()))


def _dot(a, b, dims=NN):
    return lax.dot_general(a, b, dims, preferred_element_type=F32)


def _params(sem):
    return pltpu.CompilerParams(dimension_semantics=sem, vmem_limit_bytes=VMEM_LIMIT)


def _rows_tile(rows, limit=256):
    return max(t for t in range(16, limit + 1, 16) if rows % t == 0)


def _mm(name, a, a_spec, b, b_spec, dims, grid, nk, out_shape, out_spec, add=None, add_spec=None, dep=None):
    acc_shape = out_spec.block_shape

    def body(*refs):
        a_ref, b_ref = refs[0], refs[1]
        pos = 2
        add_ref = None
        if add is not None:
            add_ref = refs[pos]
            pos += 1
        if dep is not None:
            pos += 1
        o_ref = refs[pos]
        part = _dot(a_ref[...].astype(BF16), b_ref[...].astype(BF16), dims)
        if nk == 1:
            if add_ref is not None:
                part = part + add_ref[...]
            o_ref[...] = part.astype(o_ref.dtype)
        else:
            acc_ref = refs[pos + 1]
            k = pl.program_id(2)

            @pl.when(k == 0)
            def _():
                acc_ref[...] = part if add_ref is None else part + add_ref[...]

            @pl.when(k > 0)
            def _():
                acc_ref[...] += part

            @pl.when(k == nk - 1)
            def _():
                o_ref[...] = acc_ref[...].astype(o_ref.dtype)

    ops, specs = [a, b], [a_spec, b_spec]
    if add is not None:
        ops.append(add)
        specs.append(add_spec)
    if dep is not None:
        ops.append(dep)
        specs.append(pl.BlockSpec((1, 1), lambda i, j, k: (0, 0)))
    return pl.pallas_call(
        body, name=name, grid=grid, in_specs=specs, out_specs=out_spec, out_shape=out_shape,
        scratch_shapes=[pltpu.VMEM(acc_shape, F32)] if nk > 1 else [],
        compiler_params=_params(("parallel", "parallel", "arbitrary")),
    )(*ops)


def _wspec(block, layer, fn):
    return pl.BlockSpec((None,) + block, lambda i, j, k: (layer,) + fn(i, j, k))


def _mm_nn(name, a, w, layer, *, tm, tn, tk, out_dtype, n0=0, n=None, k0=0):
    m, kk = a.shape
    n = w.shape[2] if n is None else n
    tm = min(tm, m)
    nk = kk // tk
    return _mm(name, a, pl.BlockSpec((tm, tk), lambda i, j, k: (i, k)),
               w, _wspec((tk, tn), layer, lambda i, j, k: (k + k0 // tk, j + n0 // tn)), NN,
               (m // tm, n // tn, nk), nk, jax.ShapeDtypeStruct((m, n), out_dtype),
               pl.BlockSpec((tm, tn), lambda i, j, k: (i, j)))


def _mm_nt(name, a, w, layer, *, tm, tn, tk, out_dtype, k0=0, add=None, dep=None):
    m, kk = a.shape
    n = w.shape[1]
    tm = min(tm, m)
    nk = kk // tk
    ospec = pl.BlockSpec((tm, tn), lambda i, j, k: (i, j))
    return _mm(name, a, pl.BlockSpec((tm, tk), lambda i, j, k: (i, k)),
               w, _wspec((tn, tk), layer, lambda i, j, k: (j, k + k0 // tk)), NT,
               (m // tm, n // tn, nk), nk, jax.ShapeDtypeStruct((m, n), out_dtype), ospec,
               add=add, add_spec=ospec if add is not None else None, dep=dep)


def _mm_tn(name, a, b, *, tm, tn, tk):
    kk, m = a.shape
    n = b.shape[1]
    tk = min(tk, kk)
    nk = kk // tk
    return _mm(name, a, pl.BlockSpec((tk, tm), lambda i, j, k: (k, i)),
               b, pl.BlockSpec((tk, tn), lambda i, j, k: (k, j)), TN,
               (m // tm, n // tn, nk), nk, jax.ShapeDtypeStruct((m, n), BF16),
               pl.BlockSpec((tm, tn), lambda i, j, k: (i, j)))


def _mm_rows(name, a, w, layer, mode, *, tm, tk, k0=0, rows_in=(), params=(), rows_out=(), n_sums=0, epilogue):
    m, kk = a.shape
    n = w.shape[2] if mode == 'nn' else w.shape[1]
    nk = kk // tk
    nr, npar, no = len(rows_in), len(params), len(rows_out)

    def body(*refs):
        a_ref, w_ref = refs[0], refs[1]
        rin = refs[2:2 + nr]
        par = refs[2 + nr:2 + nr + npar]
        outs = refs[2 + nr + npar:2 + nr + npar + no]
        sums = refs[2 + nr + npar + no:2 + nr + npar + no + n_sums]
        i, k = pl.program_id(0), pl.program_id(1)
        part = _dot(a_ref[...], w_ref[...], NN if mode == 'nn' else NT)

        def finish(acc):
            res, sm = epilogue(acc, [r[...] for r in rin], [p[...] for p in par])
            for r, v in zip(outs, res):
                r[...] = v.astype(r.dtype)

            @pl.when(i == 0)
            def _():
                for r in sums:
                    r[...] = jnp.zeros_like(r)

            for r, v in zip(sums, sm):
                r[...] += v

        if nk == 1:
            finish(part)
        else:
            acc_ref = refs[-1]

            @pl.when(k == 0)
            def _():
                acc_ref[...] = part

            @pl.when(k > 0)
            def _():
                acc_ref[...] += part

            @pl.when(k == nk - 1)
            def _():
                finish(acc_ref[...])

    if mode == 'nn':
        wspec = pl.BlockSpec((None, tk, n), lambda i, k: (layer, k + k0 // tk, 0))
    else:
        wspec = pl.BlockSpec((None, n, tk), lambda i, k: (layer, 0, k + k0 // tk))
    rowblk = pl.BlockSpec((tm, n), lambda i, k: (i, 0))
    one = pl.BlockSpec((1, n), lambda i, k: (0, 0))
    return pl.pallas_call(
        body, name=name, grid=(m // tm, nk),
        in_specs=[pl.BlockSpec((tm, tk), lambda i, k: (i, k)), wspec] + [rowblk] * nr + [one] * npar,
        out_specs=[rowblk] * no + [one] * n_sums,
        out_shape=[jax.ShapeDtypeStruct((m, n), dt) for dt in rows_out] +
                  [jax.ShapeDtypeStruct((1, n), F32)] * n_sums,
        scratch_shapes=[pltpu.VMEM((tm, n), F32)] if nk > 1 else [],
        compiler_params=_params(("arbitrary", "arbitrary")),
    )(a, w, *rows_in, *params)


def _rstd(x):
    return lax.rsqrt(jnp.mean(x * x, axis=-1, keepdims=True) + EPS)


def _norm_back(xin, g, dy):
    r = _rstd(xin)
    xh = xin * r
    dyg = dy * g
    return r * (dyg - xh * jnp.mean(dyg * xh, axis=-1, keepdims=True)), jnp.sum(dy * xh, axis=0, keepdims=True)


def _post_pre_rows(y, rows, pars):
    xn = rows[0] + y * _rstd(y) * pars[0]
    return [y, xn, xn * _rstd(xn) * pars[1]], []


def _make_loss_rows(d):
    def fn(y, rows, pars):
        x, tgt = rows
        err = x + y * _rstd(y) * pars[0] - tgt
        dout = err * (1.0 / d)
        dy, dg = _norm_back(y, pars[0], dout)
        lsum = 0.5 * jnp.sum(jnp.mean(err * err, axis=-1, keepdims=True), axis=0, keepdims=True)
        return [dout, dy], [dg, jnp.broadcast_to(lsum, dg.shape)]
    return fn


def _bwd_rows(dh, rows, pars):
    xin, resid, yprev = rows
    dxa, dg_pre = _norm_back(xin, pars[0], dh)
    dx = resid + dxa
    dyp, dg_post = _norm_back(yprev.astype(F32), pars[1], dx)
    return [dx, dyp], [dg_pre, dg_post]


def _bwd_rows_first(dh, rows, pars):
    xin, resid = rows
    dxa, dg_pre = _norm_back(xin, pars[0], dh)
    return [resid + dxa], [dg_pre]


def _row(d):
    return pl.BlockSpec((1, d), lambda i: (0, 0))


def _prenorm(name, x, g, tm):
    m, d = x.shape

    def body(x_ref, g_ref, o_ref):
        xv = x_ref[...]
        o_ref[...] = (xv * _rstd(xv) * g_ref[...]).astype(BF16)

    blk = pl.BlockSpec((tm, d), lambda i: (i, 0))
    return pl.pallas_call(body, name=name, grid=(m // tm,), in_specs=[blk, _row(d)], out_specs=blk,
                          out_shape=jax.ShapeDtypeStruct((m, d), BF16), compiler_params=_params(("parallel",)))(x, g)


def _norm_bwd(name, xin, g, dy, resid, out_dtype, tm):
    m, d = xin.shape

    def body(*refs):
        if resid is None:
            x_ref, g_ref, dy_ref, dx_ref, dg_ref = refs
        else:
            x_ref, g_ref, dy_ref, r_ref, dx_ref, dg_ref = refs
        xv = x_ref[...]
        r = _rstd(xv)
        xh = xv * r
        dyv = dy_ref[...].astype(F32)
        dyg = dyv * g_ref[...]
        dx = r * (dyg - xh * jnp.mean(dyg * xh, axis=-1, keepdims=True))
        if resid is not None:
            dx = dx + r_ref[...]
        dx_ref[...] = dx.astype(dx_ref.dtype)

        @pl.when(pl.program_id(0) == 0)
        def _():
            dg_ref[...] = jnp.zeros_like(dg_ref)

        dg_ref[...] += jnp.sum(dyv * xh, axis=0, keepdims=True)

    blk = pl.BlockSpec((tm, d), lambda i: (i, 0))
    ops = [xin, g, dy] + ([] if resid is None else [resid])
    specs = [blk, _row(d), blk] + ([] if resid is None else [blk])
    return pl.pallas_call(
        body, name=name, grid=(m // tm,), in_specs=specs, out_specs=[blk, _row(d)],
        out_shape=[jax.ShapeDtypeStruct((m, d), out_dtype), jax.ShapeDtypeStruct((1, d), F32)],
        compiler_params=_params(("arbitrary",)))(*ops)


def _adamw(name, g, w, m, v, tr):
    rows, cols = g.shape
    c1 = 1.0 - B1 ** STEP
    c2 = 1.0 - B2 ** STEP

    def body(g_ref, w_ref, m_ref, v_ref, d_ref, mo_ref, vo_ref):
        gv = g_ref[...]
        mn = B1 * m_ref[...] + (1.0 - B1) * gv
        vn = B2 * v_ref[...] + (1.0 - B2) * (gv * gv)
        mo_ref[...] = mn
        vo_ref[...] = vn
        d_ref[...] = -LR * ((mn / c1) / (jnp.sqrt(vn / c2) + ADAM_EPS) + WD * w_ref[...])

    blk = pl.BlockSpec((tr, cols), lambda i: (i, 0))
    sd = jax.ShapeDtypeStruct((rows, cols), F32)
    return pl.pallas_call(body, name=name, grid=(rows // tr,), in_specs=[blk] * 4, out_specs=[blk] * 3,
                          out_shape=[sd, sd, sd], compiler_params=_params(("parallel",)))(g, w, m, v)


def _gelu_parts(x):
    c = 0.7978845608028654
    t = jnp.tanh(c * (x + 0.044715 * (x * x * x)))
    return 0.5 * x * (1.0 + t), t


def _gelu_grad(x, t):
    c = 0.7978845608028654
    return 0.5 * (1.0 + t) + 0.5 * x * (1.0 - t * t) * (c * (1.0 + 3.0 * 0.044715 * x * x))


def _rot_half(x):
    ax = x.ndim - 1
    w = x.shape[ax]
    lane = lax.broadcasted_iota(jnp.int32, x.shape, ax)
    return jnp.where((lane & 63) < 32, pltpu.roll(x, w - 32, ax), pltpu.roll(x, 32, ax))


def _group_mean(x, ones_bd):
    hi = x.astype(BF16)
    lo = (x - hi.astype(F32)).astype(BF16)
    return (_dot(hi, ones_bd) + _dot(lo, ones_bd)) * (1.0 / HEAD)


def _gating(gel, gv, ws_ref, bt, ones_bd, mix_s):
    u = gel[:, :256]
    v = gel[:, 256:]
    r = lax.rsqrt(_group_mean(v * v, ones_bd) + EPS)
    xh = v * r
    vn = (xh * gv).astype(BF16)
    row = lax.broadcasted_iota(jnp.int32, (CHUNK, CHUNK), 0)
    col = lax.broadcasted_iota(jnp.int32, (CHUNK, CHUNK), 1)
    causal = col <= row
    wcs = []
    for g in range(4):
        wc = jnp.where(causal, ws_ref[g], 0.0).astype(BF16)
        wcs.append(wc)
        mix_s[:, HEAD * g:HEAD * (g + 1)] = _dot(wc, vn[:, HEAD * g:HEAD * (g + 1)]) + bt[:, g:g + 1]
    return u, r, xh, vn, wcs, causal, mix_s[...]


def _lane_select(lane, vals):
    return jnp.where(lane < 64, vals[0], jnp.where(lane < 128, vals[1], jnp.where(lane < 192, vals[2], vals[3])))


def _pool_fwd(pc, pp, ci):
    ext = jnp.concatenate([pp, pc], axis=0)
    s2 = ext + pltpu.roll(ext, 1, 0)
    s4 = s2 + pltpu.roll(s2, 2, 0)
    s8 = s4 + pltpu.roll(s4, 4, 0)
    s16 = s8 + pltpu.roll(s8, 8, 0)
    t1 = ci * CHUNK + lax.broadcasted_iota(jnp.int32, (CHUNK, 1), 0) + 1
    lane = lax.broadcasted_iota(jnp.int32, (1, 256), 1)
    cnt = _lane_select(lane, [jnp.minimum(t1, w).astype(F32) for w in POOL_WINDOWS])
    ssel = _lane_select(lane, [s[CHUNK:] for s in (s2, s4, s8, s16)])
    return ssel / cnt - pc, cnt, lane


def _attn_prep(zc, zpkv, cc, sc, cp, sp, ci):
    q = zc[:, 768:1280]
    kc = zc[:, 1280:1408]
    vc = zc[:, 1408:1536]
    kp = zpkv[:, :128]
    vp = zpkv[:, 128:]
    qr = q * jnp.concatenate([cc] * 4, axis=1) + _rot_half(q) * jnp.concatenate([sc] * 4, axis=1)
    krc = kc * cc + _rot_half(kc) * sc
    krp = kp * cp + _rot_half(kp) * sp
    kband = jnp.concatenate([krp, krc], axis=0)
    vband = jnp.concatenate([vp, vc], axis=0)
    key = lax.broadcasted_iota(jnp.int32, (2 * CHUNK, 4 * CHUNK), 0)
    t = lax.broadcasted_iota(jnp.int32, (2 * CHUNK, 4 * CHUNK), 1) & (CHUNK - 1)
    valid = ((key < CHUNK) & (key > t) & (ci > 0)) | ((key >= CHUNK) & (key - CHUNK <= t))
    return qr, kband, vband, valid


SCALE = HEAD ** -0.5


def _stack_heads(x, base, hk):
    return jnp.concatenate([x[:, base + HEAD * (4 * hk + i):base + HEAD * (4 * hk + i + 1)] for i in range(4)], axis=0)


def _sink_row(snk, hk):
    lane = lax.broadcasted_iota(jnp.int32, (1, 4 * CHUNK), 1)
    s = [snk[:, 4 * hk + i:4 * hk + i + 1] for i in range(4)]
    return jnp.where(lane < CHUNK, s[0], jnp.where(lane < 2 * CHUNK, s[1], jnp.where(lane < 3 * CHUNK, s[2], s[3])))


def _group_probs(kh, q4, valid, sink4):
    s = jnp.where(valid, _dot(kh, q4, NT), -1e30)
    mx = jnp.maximum(jnp.max(s, axis=0, keepdims=True), sink4)
    e = jnp.exp(s - mx)
    es = jnp.exp(sink4 - mx)
    inv = 1.0 / (jnp.sum(e, axis=0, keepdims=True) + es)
    return e * inv, es * inv


def _mixer_specs(nb, rev):
    def cur(i):
        return nb - 1 - i if rev else i

    def prev(i):
        return jnp.maximum(cur(i) - 1, 0)

    full = lambda shape: pl.BlockSpec(shape, lambda i: (0,) * len(shape))
    specs = [
        pl.BlockSpec((CHUNK, 1536), lambda i: (cur(i), 0)),
        pl.BlockSpec((CHUNK, 256), lambda i: (prev(i), 2)),
        pl.BlockSpec((CHUNK, 256), lambda i: (prev(i), 5)),
        pl.BlockSpec((CHUNK, 128), lambda i: (cur(i), 0)),
        pl.BlockSpec((CHUNK, 128), lambda i: (cur(i), 0)),
        pl.BlockSpec((CHUNK, 128), lambda i: (prev(i), 0)),
        pl.BlockSpec((CHUNK, 128), lambda i: (prev(i), 0)),
        full((1, 256)), full((4, CHUNK, CHUNK)), full((CHUNK, 4)), full((256, 256)), full((1, 256)), full((1, 8)),
        full((256, 256)),
    ]
    return specs, cur


def _ones_bd():
    g = lax.broadcasted_iota(jnp.int32, (256, 256), 0) // HEAD == lax.broadcasted_iota(jnp.int32, (256, 256), 1) // HEAD
    return g.astype(BF16)


def _mixer_fwd(name, z, cosq, sinq, gv, ws, bt, pw, psc, snk):
    s = z.shape[0]
    nb = s // CHUNK
    specs, _ = _mixer_specs(nb, False)

    def body(zc_ref, zpp_ref, zpkv_ref, cq_ref, sq_ref, cp_ref, sp_ref, gv_ref, ws_ref, bt_ref, pw_ref, psc_ref,
             snk_ref, bd_ref, o_ref, mix_s):
        ci = pl.program_id(0)
        zc = zc_ref[...]
        gel, _ = _gelu_parts(zc[:, :512])
        u, _, _, _, _, _, mixed = _gating(gel, gv_ref[...], ws_ref, bt_ref[...], bd_ref[...], mix_s)
        o_ref[:, :256] = (u * mixed).astype(BF16)
        pp = jnp.where(ci > 0, zpp_ref[...], 0.0)
        pooled, _, _ = _pool_fwd(zc[:, 512:768], pp, ci)
        mp = _dot(pooled.astype(BF16), pw_ref[...].astype(BF16))
        o_ref[:, 256:512] = (mp * psc_ref[...]).astype(BF16)
        qr, kband, vband, valid = _attn_prep(zc, zpkv_ref[...], cq_ref[...], sq_ref[...], cp_ref[...], sp_ref[...], ci)
        snkv = snk_ref[...]
        kb = kband.astype(BF16)
        vt = vband.T
        ots = []
        for hk in range(2):
            q4 = (_stack_heads(qr, 0, hk) * SCALE).astype(BF16)
            p, _ = _group_probs(kb[:, HEAD * hk:HEAD * (hk + 1)], q4, valid, _sink_row(snkv, hk))
            ots.append(_dot(vt[HEAD * hk:HEAD * (hk + 1), :].astype(BF16), p.astype(BF16)))
        o = jnp.concatenate(ots, axis=0).T
        for hk in range(2):
            for i in range(4):
                h = 4 * hk + i
                o_ref[:, 512 + HEAD * h:512 + HEAD * (h + 1)] = o[CHUNK * i:CHUNK * (i + 1),
                                                                  HEAD * hk:HEAD * (hk + 1)].astype(BF16)

    return pl.pallas_call(
        body, name=name, grid=(nb,), in_specs=specs, out_specs=pl.BlockSpec((CHUNK, 1024), lambda i: (i, 0)),
        out_shape=jax.ShapeDtypeStruct((s, 1024), BF16), scratch_shapes=[pltpu.VMEM((CHUNK, 256), F32)],
        compiler_params=_params(("parallel",)),
    )(z, z, z, cosq, sinq, cosq, sinq, gv, ws, bt, pw, psc, snk, _ones_bd())


def _mixer_bwd(name, z, dabc, cosq, sinq, gv, ws, bt, pw, psc, snk):
    s = z.shape[0]
    nb = s // CHUNK
    specs, cur = _mixer_specs(nb, True)
    specs = specs + [pl.BlockSpec((CHUNK, 1024), lambda i: (cur(i), 0))]
    full = lambda shape: pl.BlockSpec(shape, lambda i: (0,) * len(shape))
    acc_shapes = [(1, 256), (4, CHUNK, CHUNK), (CHUNK, 4), (256, 256), (1, 256), (1, 8)]

    def body(zc_ref, zpp_ref, zpkv_ref, cq_ref, sq_ref, cp_ref, sp_ref, gv_ref, ws_ref, bt_ref, pw_ref, psc_ref,
             snk_ref, bd_ref, dabc_ref, dz_ref, dgv_ref, dws_ref, dbt_ref, dpw_ref, dpsc_ref, dsnk_ref,
             cpool, ck, cv, dq_s, dkv_s, mix_s, dvn_s):
        step = pl.program_id(0)
        ci = nb - 1 - step

        @pl.when(step == 0)
        def _():
            for r in (dgv_ref, dws_ref, dbt_ref, dpw_ref, dpsc_ref, dsnk_ref, cpool, ck, cv):
                r[...] = jnp.zeros_like(r)

        zc = zc_ref[...]
        dabc = dabc_ref[...]
        zg = zc[:, :512]
        gel, th = _gelu_parts(zg)
        gp = _gelu_grad(zg, th)
        gvv = gv_ref[...]
        bd = bd_ref[...]
        u, r, xh, vn, wcs, causal, mixed = _gating(gel, gvv, ws_ref, bt_ref[...], bd, mix_s)
        da = dabc[:, :256]
        dm = da * u
        dmb = dm.astype(BF16)
        lane4 = lax.broadcasted_iota(jnp.int32, (CHUNK, 4), 1)
        dbt = jnp.zeros((CHUNK, 4), F32)
        for g in range(4):
            lo, hi = HEAD * g, HEAD * (g + 1)
            dws_ref[g] += jnp.where(causal, _dot(dmb[:, lo:hi], vn[:, lo:hi], NT), 0.0)
            dbt = dbt + jnp.where(lane4 == g, jnp.sum(dm[:, lo:hi], axis=-1, keepdims=True), 0.0)
            dvn_s[:, lo:hi] = _dot(wcs[g], dmb[:, lo:hi], TN)
        dbt_ref[...] += dbt
        dvn = dvn_s[...]
        dgv_ref[...] += jnp.sum(dvn * xh, axis=0, keepdims=True)
        dxh = dvn * gvv
        dvg = r * (dxh - xh * _group_mean(dxh * xh, bd))
        dz_ref[:, :256] = (da * mixed * gp[:, :256]).astype(BF16)
        dz_ref[:, 256:512] = (dvg * gp[:, 256:]).astype(BF16)
        pc = zc[:, 512:768]
        pp = jnp.where(ci > 0, zpp_ref[...], 0.0)
        pooled, cnt, lane = _pool_fwd(pc, pp, ci)
        pwb = pw_ref[...].astype(BF16)
        pooled_b = pooled.astype(BF16)
        mp = _dot(pooled_b, pwb)
        db = dabc[:, 256:512]
        dpsc_ref[...] += jnp.sum(db * mp, axis=0, keepdims=True)
        dmpb = (db * psc_ref[...]).astype(BF16)
        dpw_ref[...] += _dot(pooled_b, dmpb, TN)
        dpooled = _dot(dmpb, pwb, NT)
        davg = dpooled / cnt
        zero = jnp.zeros((CHUNK, 256), F32)
        d2, d4, d8, d16 = [jnp.concatenate([zero, jnp.where((lane >= 64 * k) & (lane < 64 * (k + 1)), davg, 0.0)],
                                           axis=0) for k in range(4)]
        g8 = d8 + d16 + pltpu.roll(d16, 2 * CHUNK - 8, 0)
        g4 = d4 + g8 + pltpu.roll(g8, 2 * CHUNK - 4, 0)
        g2 = d2 + g4 + pltpu.roll(g4, 2 * CHUNK - 2, 0)
        ge = g2 + pltpu.roll(g2, 2 * CHUNK - 1, 0)
        dz_ref[:, 512:768] = (ge[CHUNK:] - dpooled + cpool[...]).astype(BF16)
        cpool[...] = ge[:CHUNK]
        cc = cq_ref[...]
        sc = sq_ref[...]
        qr, kband, vband, valid = _attn_prep(zc, zpkv_ref[...], cc, sc, cp_ref[...], sp_ref[...], ci)
        snkv = snk_ref[...]
        lane8 = lax.broadcasted_iota(jnp.int32, (1, 8), 1)
        qlane = lax.broadcasted_iota(jnp.int32, (1, 4 * CHUNK), 1)
        dsnk = jnp.zeros((1, 8), F32)
        kb = kband.astype(BF16)
        vb = vband.astype(BF16)
        kt = kband.T * SCALE
        dqts = []
        for hk in range(2):
            kh = kb[:, HEAD * hk:HEAD * (hk + 1)]
            q4 = (_stack_heads(qr, 0, hk) * SCALE).astype(BF16)
            do4 = _stack_heads(dabc, 512, hk).astype(BF16)
            p, ps = _group_probs(kh, q4, valid, _sink_row(snkv, hk))
            dp = _dot(vb[:, HEAD * hk:HEAD * (hk + 1)], do4, NT)
            dd = jnp.sum(p * dp, axis=0, keepdims=True)
            dsink = -ps * dd
            for i in range(4):
                part = jnp.sum(jnp.where((qlane >= CHUNK * i) & (qlane < CHUNK * (i + 1)), dsink, 0.0),
                               axis=1, keepdims=True)
                dsnk = dsnk + jnp.where(lane8 == 4 * hk + i, part, 0.0)
            dsb = (p * (dp - dd)).astype(BF16)
            dqts.append(_dot(kt[HEAD * hk:HEAD * (hk + 1), :].astype(BF16), dsb))
            dkv_s[:, HEAD * hk:HEAD * (hk + 1)] = _dot(dsb, q4)
            dkv_s[:, 128 + HEAD * hk:128 + HEAD * (hk + 1)] = _dot(p.astype(BF16), do4)
        dq4 = jnp.concatenate(dqts, axis=0).T
        for hk in range(2):
            for i in range(4):
                h = 4 * hk + i
                dq_s[:, HEAD * h:HEAD * (h + 1)] = dq4[CHUNK * i:CHUNK * (i + 1), HEAD * hk:HEAD * (hk + 1)]
        dsnk_ref[...] += dsnk
        dqr = dq_s[...]
        dz_ref[:, 768:1280] = (dqr * jnp.concatenate([cc] * 4, axis=1)
                               + _rot_half(dqr * jnp.concatenate([sc] * 4, axis=1))).astype(BF16)
        dkv = dkv_s[...]
        dkr = dkv[CHUNK:, :128] + ck[...]
        dz_ref[:, 1280:1408] = (dkr * cc + _rot_half(dkr * sc)).astype(BF16)
        dz_ref[:, 1408:1536] = (dkv[CHUNK:, 128:] + cv[...]).astype(BF16)
        ck[...] = dkv[:CHUNK, :128]
        cv[...] = dkv[:CHUNK, 128:]

    return pl.pallas_call(
        body, name=name, grid=(nb,), in_specs=specs,
        out_specs=[pl.BlockSpec((CHUNK, 1536), lambda i: (cur(i), 0))] + [full(a) for a in acc_shapes],
        out_shape=[jax.ShapeDtypeStruct((s, 1536), BF16)] + [jax.ShapeDtypeStruct(a, F32) for a in acc_shapes],
        scratch_shapes=[pltpu.VMEM((CHUNK, 256), F32), pltpu.VMEM((CHUNK, 128), F32), pltpu.VMEM((CHUNK, 128), F32),
                        pltpu.VMEM((CHUNK, 512), F32), pltpu.VMEM((2 * CHUNK, 256), F32),
                        pltpu.VMEM((CHUNK, 256), F32), pltpu.VMEM((CHUNK, 256), F32)],
        compiler_params=_params(("arbitrary",)),
    )(z, z, z, cosq, sinq, cosq, sinq, gv, ws, bt, pw, psc, snk, _ones_bd(), dabc)


def _xattn_probs(qh, kh):
    s = _dot(qh, kh, NT) * (256 ** -0.5)
    e = jnp.exp(s - jnp.max(s, axis=-1, keepdims=True))
    return e * (1.0 / jnp.sum(e, axis=-1, keepdims=True))


def _xattn_fwd(name, q, kv, tq):
    s, d = q.shape
    mlen = kv.shape[0]

    def body(q_ref, kv_ref, o_ref):
        for h in range(4):
            lo, hi = 256 * h, 256 * (h + 1)
            p = _xattn_probs(q_ref[:, lo:hi], kv_ref[:, lo:hi])
            o_ref[:, lo:hi] = _dot(p.astype(BF16), kv_ref[:, d + lo:d + hi]).astype(BF16)

    blk = pl.BlockSpec((tq, d), lambda i: (i, 0))
    return pl.pallas_call(body, name=name, grid=(s // tq,),
                          in_specs=[blk, pl.BlockSpec((mlen, 2 * d), lambda i: (0, 0))], out_specs=blk,
                          out_shape=jax.ShapeDtypeStruct((s, d), BF16), compiler_params=_params(("parallel",)))(q, kv)


def _xattn_bwd(name, q, kv, do, tq):
    s, d = q.shape
    mlen = kv.shape[0]

    def body(q_ref, kv_ref, do_ref, dq_ref, dkv_ref):
        @pl.when(pl.program_id(0) == 0)
        def _():
            dkv_ref[...] = jnp.zeros_like(dkv_ref)

        for h in range(4):
            lo, hi = 256 * h, 256 * (h + 1)
            qh = q_ref[:, lo:hi]
            kh = kv_ref[:, lo:hi]
            vh = kv_ref[:, d + lo:d + hi]
            doh = do_ref[:, lo:hi]
            p = _xattn_probs(qh, kh)
            dp = _dot(doh, vh, NT)
            dsb = (p * (dp - jnp.sum(p * dp, axis=-1, keepdims=True)) * (256 ** -0.5)).astype(BF16)
            dq_ref[:, lo:hi] = _dot(dsb, kh).astype(BF16)
            dkv_ref[:, lo:hi] += _dot(dsb, qh, TN)
            dkv_ref[:, d + lo:d + hi] += _dot(p.astype(BF16), doh, TN)

    blk = pl.BlockSpec((tq, d), lambda i: (i, 0))
    kvb = pl.BlockSpec((mlen, 2 * d), lambda i: (0, 0))
    return pl.pallas_call(
        body, name=name, grid=(s // tq,), in_specs=[blk, kvb, blk], out_specs=[blk, kvb],
        out_shape=[jax.ShapeDtypeStruct((s, d), BF16), jax.ShapeDtypeStruct((mlen, 2 * d), F32)],
        compiler_params=_params(("arbitrary",)))(q, kv, do)


def _sigmoid(x):
    return 0.5 * (1.0 + jnp.tanh(0.5 * x))


def _ffn_up(name, h, wgu, layer, tm, tn):
    s, d = h.shape
    dff = wgu.shape[2] // 2
    nj = dff // tn

    def body(h_ref, wg_ref, wu_ref, g_ref, u_ref, a_ref):
        hv = h_ref[...]
        gate = _dot(hv, wg_ref[...])
        up = _dot(hv, wu_ref[...])
        g_ref[...] = gate.astype(BF16)
        u_ref[...] = up.astype(BF16)
        a_ref[...] = (gate * _sigmoid(gate) * up).astype(BF16)

    ob = pl.BlockSpec((tm, tn), lambda j, i: (i, j))
    sd = jax.ShapeDtypeStruct((s, dff), BF16)
    return pl.pallas_call(
        body, name=name, grid=(nj, s // tm),
        in_specs=[pl.BlockSpec((tm, d), lambda j, i: (i, 0)),
                  pl.BlockSpec((None, d, tn), lambda j, i: (layer, 0, j)),
                  pl.BlockSpec((None, d, tn), lambda j, i: (layer, 0, j + nj))],
        out_specs=[ob, ob, ob], out_shape=[sd, sd, sd], compiler_params=_params(("parallel", "parallel")),
    )(h, wgu, wgu)


def _ffn_act_bwd(name, dfn, wdown, layer, gate, up, tm):
    s, d = dfn.shape
    dff = gate.shape[1]

    def body(df_ref, wd_ref, g_ref, u_ref, o_ref):
        dact = _dot(df_ref[...], wd_ref[...], NT)
        gate = g_ref[...].astype(F32)
        sig = _sigmoid(gate)
        o_ref[:, :dff] = (dact * u_ref[...].astype(F32) * sig * (1.0 + gate * (1.0 - sig))).astype(BF16)
        o_ref[:, dff:] = (dact * gate * sig).astype(BF16)

    gb = pl.BlockSpec((tm, dff), lambda i: (i, 0))
    return pl.pallas_call(
        body, name=name, grid=(s // tm,),
        in_specs=[pl.BlockSpec((tm, d), lambda i: (i, 0)),
                  pl.BlockSpec((None, dff, d), lambda i: (layer, 0, 0)), gb, gb],
        out_specs=pl.BlockSpec((tm, 2 * dff), lambda i: (i, 0)),
        out_shape=jax.ShapeDtypeStruct((s, 2 * dff), BF16), compiler_params=_params(("parallel",)),
    )(dfn, wdown, gate, up)


def _place():
    return lax.axis_index("x"), lax.axis_index("y"), lax.axis_index("c")


def _other_chips(x, y):
    return [(1 - x, y), (x, 1 - y), (1 - x, 1 - y)]


def _region(ref, axis, chip, size):
    start = pl.multiple_of(chip * size, size)
    if axis == 1:
        return ref.at[:, pl.ds(start, size), :]
    return ref.at[:, :, pl.ds(start, size)]


ANY = pl.BlockSpec(memory_space=pl.ANY)


HBM = pl.BlockSpec(memory_space=pltpu.HBM)
SEM = pl.BlockSpec(memory_space=pltpu.SEMAPHORE)
EFFECT = pltpu.SideEffectType.DATAFLOW_SIDE_EFFECTING


def _in_hbm(a):
    return pltpu.with_memory_space_constraint(a, pltpu.HBM)


def _split_start(name, srcs, lands, ncopies, plan, after=None):
    ns, nl = len(srcs), len(lands)
    extra = [] if after is None else [after]

    def body(*refs):
        src, land = refs[:ns], refs[ns:ns + nl]
        send, recv = refs[ns + nl + len(extra)], refs[ns + nl + len(extra) + 1]
        token = refs[-1]
        x, y, c = _place()
        for k, (s_ref, d_ref, peer, _) in enumerate(plan(src, land, x, y, c)):
            pltpu.make_async_remote_copy(src_ref=s_ref, dst_ref=d_ref, send_sem=send.at[k], recv_sem=recv.at[k],
                                         device_id=peer, device_id_type=MESH).start()
        token[...] = jnp.zeros_like(token)

    ops = list(srcs) + list(lands)
    out = pl.pallas_call(
        body, name=name,
        out_shape=(pltpu.SemaphoreType.DMA((ncopies,)), pltpu.SemaphoreType.DMA((ncopies,)),
                   *[pltpu.HBM(a.shape, a.dtype) for a in ops], jax.ShapeDtypeStruct((8, 128), F32)),
        in_specs=(HBM,) * (ns + nl) + (ANY,) * len(extra),
        out_specs=(SEM, SEM) + (HBM,) * (ns + nl) + (pl.BlockSpec(memory_space=pltpu.VMEM),),
        input_output_aliases={i: 2 + i for i in range(ns + nl)},
        compiler_params=pltpu.CompilerParams(has_side_effects=EFFECT),
    )(*[_in_hbm(a) for a in ops], *extra)
    return out[0], out[1], list(out[2:2 + ns]), list(out[2 + ns:2 + ns + nl]), out[-1]


def _split_wait(name, send, recv, srcs, lands, after, plan):
    ns, nl = len(srcs), len(lands)

    def body(*refs):
        src, land = refs[:ns], refs[ns:ns + nl]
        send_ref, recv_ref = refs[ns + nl], refs[ns + nl + 1]
        x, y, c = _place()
        for k, (s_ref, _, _, got) in enumerate(plan(src, land, x, y, c)):
            cp = pltpu.make_async_remote_copy(src_ref=s_ref, dst_ref=got, send_sem=send_ref.at[k],
                                              recv_sem=recv_ref.at[k], device_id=(x, y, c), device_id_type=MESH)
            cp.wait_send()
            cp.wait_recv()

    ops = list(srcs) + list(lands)
    out = pl.pallas_call(
        body, name=name, out_shape=tuple(pltpu.HBM(a.shape, a.dtype) for a in ops),
        in_specs=(HBM,) * (ns + nl) + (SEM, SEM, ANY), out_specs=(HBM,) * (ns + nl),
        input_output_aliases={i: i for i in range(ns + nl)},
        compiler_params=pltpu.CompilerParams(has_side_effects=EFFECT),
    )(*ops, send, recv, after)
    return list(out[:ns]), list(out[ns:])


def _split_wait_start(name, send, recv, lands, after, wait_plan, ncopies, start_plan, carried=()):
    nl, nc = len(lands), len(carried)
    lands = list(lands) + list(carried)

    def body(*refs):
        land = refs[:nl]
        send_in, recv_in = refs[nl + nc], refs[nl + nc + 1]
        send_out, recv_out = refs[nl + nc + 3], refs[nl + nc + 4]
        x, y, c = _place()
        for k, (s_ref, _, _, got) in enumerate(wait_plan((), land, x, y, c)):
            cp = pltpu.make_async_remote_copy(src_ref=s_ref, dst_ref=got, send_sem=send_in.at[k],
                                              recv_sem=recv_in.at[k], device_id=(x, y, c), device_id_type=MESH)
            cp.wait_send()
            cp.wait_recv()
        for k, (s_ref, d_ref, peer, _) in enumerate(start_plan((), land, x, y, c)):
            pltpu.make_async_remote_copy(src_ref=s_ref, dst_ref=d_ref, send_sem=send_out.at[k],
                                         recv_sem=recv_out.at[k], device_id=peer, device_id_type=MESH).start()

    out = pl.pallas_call(
        body, name=name,
        out_shape=(pltpu.SemaphoreType.DMA((ncopies,)), pltpu.SemaphoreType.DMA((ncopies,)),
                   *[pltpu.HBM(a.shape, a.dtype) for a in lands]),
        in_specs=(HBM,) * (nl + nc) + (SEM, SEM, ANY), out_specs=(SEM, SEM) + (HBM,) * (nl + nc),
        input_output_aliases={i: 2 + i for i in range(nl + nc)},
        compiler_params=pltpu.CompilerParams(has_side_effects=EFFECT),
    )(*lands, send, recv, after)
    return out[0], out[1], list(out[2:2 + nl]), list(out[2 + nl:])


def _half(ref, axis, chip, size, layer, h):
    reg = _region(ref, axis, chip, size).at[pl.ds(layer, 1)]
    rows = reg.shape[1] // 2
    return reg.at[:, pl.ds(pl.multiple_of(h * rows, rows), rows), :]


def _gather_plan(axes, sizes, layer):
    def plan(src, land, x, y, c):
        me = 2 * x + y
        out = []
        for t in range(len(land)):
            mine = _half(land[t], axes[t], me, sizes[t], layer, c)
            for px, py in _other_chips(x, y):
                out.append((mine, mine, (px, py, c), _half(land[t], axes[t], 2 * px + py, sizes[t], layer, c)))
        return out
    return plan


def _forward_plan(axes, sizes, layer):
    def plan(src, land, x, y, c):
        out = []
        for t in range(len(land)):
            for px, py in _other_chips(x, y):
                got = _half(land[t], axes[t], 2 * px + py, sizes[t], layer, c)
                out.append((got, got, (x, y, 1 - c), _half(land[t], axes[t], 2 * px + py, sizes[t], layer, 1 - c)))
        return out
    return plan


def _place_own(name, w, axis, chip):
    nl, r, cs = w.shape
    tr = _rows_tile(r)
    nb = r // tr
    full = (nl, 4 * r, cs) if axis == 1 else (nl, r, 4 * cs)

    def body(m_ref, w_ref, o_ref):
        o_ref[...] = w_ref[...].astype(BF16)

    if axis == 1:
        ospec = pl.BlockSpec((None, tr, cs), lambda l, i, m: (l, m[0] * nb + i, 0))
    else:
        ospec = pl.BlockSpec((None, tr, cs), lambda l, i, m: (l, i, m[0]))
    return pl.pallas_call(
        body, name=name,
        grid_spec=pltpu.PrefetchScalarGridSpec(
            num_scalar_prefetch=1, grid=(nl, nb),
            in_specs=[pl.BlockSpec((None, tr, cs), lambda l, i, m: (l, i, 0))], out_specs=ospec),
        out_shape=jax.ShapeDtypeStruct(full, BF16), compiler_params=_params(("parallel", "parallel")),
    )(chip, w)


def _scatter_plan(axes, sizes):
    def plan(src, land, x, y, c):
        out = []
        for t in range(len(src)):
            for k, (px, py) in enumerate(_other_chips(x, y)):
                out.append((_region(src[t], axes[t], 2 * px + py, sizes[t]).at[0], land[t].at[k], (px, py, c),
                            land[t].at[k]))
        return out
    return plan


def _pair_plan(src, land, x, y, c):
    return [(src[t], land[t], (x, y, 1 - c), land[t]) for t in range(len(src))]


def _chip_sum(name, g, slots, axis, chip):
    _, r, cs = slots.shape
    tr = _rows_tile(r)
    nb = r // tr

    def body(m_ref, g_ref, s_ref, o_ref):
        acc = g_ref[...].astype(F32)
        for k in range(3):
            acc = acc + s_ref[k].astype(F32)
        o_ref[...] = acc

    if axis == 1:
        gspec = pl.BlockSpec((tr, cs), lambda i, m: (m[0] * nb + i, 0))
    else:
        gspec = pl.BlockSpec((tr, cs), lambda i, m: (i, m[0]))
    return pl.pallas_call(
        body, name=name,
        grid_spec=pltpu.PrefetchScalarGridSpec(
            num_scalar_prefetch=1, grid=(nb,),
            in_specs=[gspec, pl.BlockSpec((3, tr, cs), lambda i, m: (0, i, 0))],
            out_specs=pl.BlockSpec((tr, cs), lambda i, m: (i, 0))),
        out_shape=jax.ShapeDtypeStruct((r, cs), F32), compiler_params=_params(("parallel",)),
    )(chip, g, slots)


def _pair_adamw(name, mine, theirs, w, m, v, layer, bufs):
    r, cs = mine.shape
    tr = _rows_tile(r)
    c1 = 1.0 - B1 ** STEP
    c2 = 1.0 - B2 ** STEP

    def body(a_ref, b_ref, w_ref, m_ref, v_ref, _g, _d, _m, _v, g_ref, d_ref, mo_ref, vo_ref):
        gv = a_ref[...] + b_ref[...]
        mn = B1 * m_ref[...] + (1.0 - B1) * gv
        vn = B2 * v_ref[...] + (1.0 - B2) * (gv * gv)
        g_ref[...] = gv
        mo_ref[...] = mn
        vo_ref[...] = vn
        d_ref[...] = -LR * ((mn / c1) / (jnp.sqrt(vn / c2) + ADAM_EPS) + WD * w_ref[...])

    blk = pl.BlockSpec((tr, cs), lambda i: (i, 0))
    lay = pl.BlockSpec((None, tr, cs), lambda i: (layer, i, 0))
    return pl.pallas_call(
        body, name=name, grid=(r // tr,), in_specs=[blk, blk, lay, lay, lay] + [ANY] * 4, out_specs=[lay] * 4,
        out_shape=[jax.ShapeDtypeStruct(b.shape, b.dtype) for b in bufs],
        input_output_aliases={5 + i: i for i in range(4)}, compiler_params=_params(("parallel",)),
    )(mine, theirs, w, m, v, *bufs)


def _allreduce_small(p):
    rows = p.shape[0]

    def body(p_ref, o_ref, sib, sums, send, recv):
        x, y, c = _place()
        me = 2 * x + y
        pair = pltpu.make_async_remote_copy(src_ref=p_ref, dst_ref=sib, send_sem=send.at[3], recv_sem=recv.at[3],
                                            device_id=(x, y, 1 - c), device_id_type=MESH)
        pair.start()
        pair.wait()
        sums[me] = (p_ref[...] + sib[...]).astype(BF16)
        cps = [pltpu.make_async_remote_copy(src_ref=sums.at[me], dst_ref=sums.at[me], send_sem=send.at[k],
                                            recv_sem=recv.at[k], device_id=(px, py, c), device_id_type=MESH)
               for k, (px, py) in enumerate(_other_chips(x, y))]
        for cp in cps:
            cp.start()
        for k, (px, py) in enumerate(_other_chips(x, y)):
            slot = sums.at[2 * px + py]
            pltpu.make_async_remote_copy(src_ref=slot, dst_ref=slot, send_sem=send.at[k], recv_sem=recv.at[k],
                                         device_id=(x, y, c), device_id_type=MESH).wait_recv()
        for cp in cps:
            cp.wait_send()
        acc = sums[0].astype(F32)
        for k in range(1, 4):
            acc = acc + sums[k].astype(F32)
        o_ref[...] = acc

    vm = pl.BlockSpec(memory_space=pltpu.VMEM)
    return pl.pallas_call(
        body, name="allreduce_small", in_specs=[vm], out_specs=vm, out_shape=jax.ShapeDtypeStruct(p.shape, F32),
        scratch_shapes=[pltpu.VMEM((rows, 128), F32), pltpu.VMEM((4, rows, 128), BF16),
                        pltpu.SemaphoreType.DMA((4,)), pltpu.SemaphoreType.DMA((4,))],
        compiler_params=pltpu.CompilerParams(vmem_limit_bytes=VMEM_LIMIT),
    )(p)


def _pack(parts):
    flat = []
    for p in parts:
        v = p.reshape(-1).astype(F32)
        flat.append(jnp.pad(v, (0, (-v.shape[0]) % 128)))
    v = jnp.concatenate(flat)
    v = jnp.pad(v, (0, (-v.shape[0]) % (512 * 128)))
    return v.reshape(-1, 128)


def _unpack(buf, like):
    out, r0 = [], 0
    for p in like:
        nelem = 1
        for s in p.shape:
            nelem *= s
        rows = -(-nelem // 128)
        blk = buf[r0:r0 + rows]
        if nelem % 128:
            blk = blk.reshape(-1)[:nelem]
        out.append(blk.reshape(p.shape))
        r0 += rows
    return out


def kernel(x, mem, positions, mem_norm_g, mix_pre_g, mix_post_g, w_in, gm_v_g, gm_w_s, gm_b_s, pool_w, pool_scale, attn_sinks, w_o, x_pre_g, x_post_g, w_xq, w_xkv, w_xo, ffn_pre_g, ffn_post_g, w_gate_up, w_down, loss_target, m_mem_norm_g, m_mix_pre_g, m_mix_post_g, m_w_in, m_gm_v_g, m_gm_w_s, m_gm_b_s, m_pool_w, m_pool_scale, m_attn_sinks, m_w_o, m_x_pre_g, m_x_post_g, m_w_xq, m_w_xkv, m_w_xo, m_ffn_pre_g, m_ffn_post_g, m_w_gate_up, m_w_down, v_mem_norm_g, v_mix_pre_g, v_mix_post_g, v_w_in, v_gm_v_g, v_gm_w_s, v_gm_b_s, v_pool_w, v_pool_scale, v_attn_sinks, v_w_o, v_x_pre_g, v_x_post_g, v_w_xq, v_w_xkv, v_w_xo, v_ffn_pre_g, v_ffn_post_g, v_w_gate_up, v_w_down):
    args = (x, mem, positions, mem_norm_g, mix_pre_g, mix_post_g, w_in, gm_v_g, gm_w_s, gm_b_s, pool_w, pool_scale, attn_sinks, w_o, x_pre_g, x_post_g, w_xq, w_xkv, w_xo, ffn_pre_g, ffn_post_g, w_gate_up, w_down)
    moms_m = (m_mem_norm_g, m_mix_pre_g, m_mix_post_g, m_w_in, m_gm_v_g, m_gm_w_s, m_gm_b_s, m_pool_w, m_pool_scale, m_attn_sinks, m_w_o, m_x_pre_g, m_x_post_g, m_w_xq, m_w_xkv, m_w_xo, m_ffn_pre_g, m_ffn_post_g, m_w_gate_up, m_w_down)
    moms_v = (v_mem_norm_g, v_mix_pre_g, v_mix_post_g, v_w_in, v_gm_v_g, v_gm_w_s, v_gm_b_s, v_pool_w, v_pool_scale, v_attn_sinks, v_w_o, v_x_pre_g, v_x_post_g, v_w_xq, v_w_xkv, v_w_xo, v_ffn_pre_g, v_ffn_post_g, v_w_gate_up, v_w_down)
    P = dict(zip(NAMES, args))
    P['loss_target'] = loss_target
    M = dict(zip(WEIGHTS, moms_m))
    V = dict(zip(WEIGHTS, moms_v))
    depth = w_in.shape[0]
    nbig = len(BIG)
    axes = [BIG_AXIS[n] for n in BIG]
    sizes = [P[n].shape[a] for n, a in zip(BIG, axes)]
    chip = (2 * lax.axis_index("x") + lax.axis_index("y")).astype(jnp.int32).reshape(1)

    groups = [['w_in'], ['w_o', 'w_xq', 'w_xkv', 'w_xo'], ['w_gate_up', 'w_down']]
    units = [(l, g) for l in range(depth) for g in groups]
    unit_of = {(l, n): i for i, (l, names) in enumerate(units) for n in names}
    ax = lambda names: [BIG_AXIS[n] for n in names]
    sz = lambda names: [P[n].shape[BIG_AXIS[n]] for n in names]

    full = {n: _place_own("place_" + n, P[n], BIG_AXIS[n], chip) for n in BIG}
    gathers = []
    prev, tok = None, None
    for i, (l, names) in enumerate(units):
        send, recv, _, land, tok = _split_start("gather_start%d" % i, [], [full[n] for n in names],
                                                3 * len(names), _gather_plan(ax(names), sz(names), l), after=prev)
        full.update(zip(names, land))
        gathers.append((send, recv))
        prev = land[0]
    P['first_dep'] = tok[:1, :1]
    forwards, gathered = {}, set()

    def forward_unit(i, after, carried=()):
        ul, unames = units[i]
        send, recv = gathers[i]
        send, recv, land, thru = _split_wait_start(
            "gather_pass%d" % i, send, recv, [full[n] for n in unames], after,
            _gather_plan(ax(unames), sz(unames), ul), 3 * len(unames), _forward_plan(ax(unames), sz(unames), ul),
            carried=[full[n] for n in carried])
        full.update(zip(unames, land))
        full.update(zip(carried, thru))
        forwards[i] = (send, recv)

    def weights_of(l, names, after):
        i = unit_of[(l, names[0])]
        if i not in gathered:
            _, unames = units[i]
            if i not in forwards:
                forward_unit(i, after)
            send, recv = forwards.pop(i)
            _, land = _split_wait("gather_wait%d" % i, send, recv, [], [full[n] for n in unames], after,
                                  _forward_plan(ax(unames), sz(unames), l))
            full.update(zip(unames, land))
            gathered.add(i)
            if len(groups) <= i + 1 < len(units):
                forward_unit(i + 1, after, carried=[n for n in unames if n not in units[i + 1][1]])
        return {n: (full[n], l) for n in names}

    outs = {n: [lax.empty(P[n].shape, F32) for _ in range(4)] for n in BIG}
    gunits = [(l, BIG) for l in range(depth - 1, 0, -1)] + [
        (0, g) for g in (['w_gate_up', 'w_down'], ['w_xq', 'w_xkv', 'w_xo'], ['w_o'], ['w_in'])]
    collected, scatters, pairs = {}, {}, {}

    def finish_scatter(i, after):
        _, names = gunits[i]
        send, recv, g_l, slots = scatters.pop(i)
        g_l, slots = _split_wait("scatter_wait%d" % i, send, recv, g_l, slots, after,
                                 _scatter_plan(ax(names), sz(names)))
        mine = [_chip_sum("chip_sum_" + n, g.reshape(g.shape[1:]), sl, BIG_AXIS[n], chip)
                for n, g, sl in zip(names, g_l, slots)]
        send, recv, mine, theirs, tok = _split_start("pair_start%d" % i, mine, [lax.empty(a.shape, F32) for a in mine],
                                                     len(names), _pair_plan)
        pairs[i] = (send, recv, mine, theirs)
        return tok[:1, :1]

    def finish_pair(i, after):
        l, names = gunits[i]
        send, recv, mine, theirs = pairs.pop(i)
        mine, theirs = _split_wait("pair_wait%d" % i, send, recv, mine, theirs, after, _pair_plan)
        for n, a, b in zip(names, mine, theirs):
            outs[n] = _pair_adamw("adamw_" + n, a, b, P[n], M[n], V[n], l, outs[n])

    calls = {'n': 0}
    lag = 4

    def grads_of(l, g_part, after):
        collected.update({(l, n): g for n, g in g_part.items()})
        calls['n'] += 1
        now = calls['n']
        tok = jnp.zeros((1, 1), F32)
        for i, (ul, names) in enumerate(gunits):
            if ul != l or ('started', i) in collected or any((l, n) not in collected for n in names):
                continue
            collected[('started', i)] = now
            srcs = [collected[(l, n)].reshape((1,) + collected[(l, n)].shape) for n in names]
            send, recv, srcs, slots, t = _split_start("scatter_start%d" % i, srcs,
                                                      [lax.empty((3,) + P[n].shape[1:], BF16) for n in names],
                                                      3 * len(names), _scatter_plan(ax(names), sz(names)))
            scatters[i] = (send, recv, srcs, slots)
            tok = tok + t[:1, :1]
        for i in sorted(pairs):
            if collected[('summed', i)] + lag <= now:
                finish_pair(i, after)
        for i in sorted(scatters):
            if collected[('started', i)] + lag <= now:
                tok = tok + finish_scatter(i, after)
                collected[('summed', i)] = now
        return tok

    loss_part, dx, small_g = _fwd_bwd(P, weights_of, grads_of)
    loss = lax.psum(loss_part[0, 0], ("x", "y", "c"))
    grad_x = dx.reshape(x.shape)

    small_like = [P[n] for n in SMALL]
    gsum = _allreduce_small(_pack(small_g))
    dlt, mn, vn = _adamw("adamw_small", gsum, _pack(small_like), _pack([M[n] for n in SMALL]),
                         _pack([V[n] for n in SMALL]), 512)
    grads, deltas, new_m, new_v = {}, {}, {}, {}
    for name_map, buf in ((grads, gsum), (deltas, dlt), (new_m, mn), (new_v, vn)):
        for n, a in zip(SMALL, _unpack(buf, small_like)):
            name_map[n] = a

    for i in sorted(pairs):
        finish_pair(i, dlt)
    for i in sorted(scatters):
        finish_scatter(i, dlt)
    for i in sorted(pairs):
        finish_pair(i, dlt)
    for n in BIG:
        grads[n], deltas[n], new_m[n], new_v[n] = outs[n]

    return (loss, grad_x, *[grads[n] for n in WEIGHTS], *[deltas[n] for n in WEIGHTS],
            *[new_m[n] for n in WEIGHTS], *[new_v[n] for n in WEIGHTS])


def _fwd_bwd(P, weights_of, grads_of):
    (x, mem, positions, mem_norm_g, mix_pre_g, mix_post_g, w_in, gm_v_g, gm_w_s, gm_b_s, pool_w, pool_scale, attn_sinks,
     w_o, x_pre_g, x_post_g, w_xq, w_xkv, w_xo, ffn_pre_g, ffn_post_g, w_gate_up, w_down) = [P[n] for n in NAMES]
    x0 = x[0]
    s, d = x0.shape
    depth = w_in.shape[0]
    tgt = P['loss_target'][0]
    tmn = 256
    tmr = min(512, s)
    tkw = min(2048, s)

    half = HEAD // 2
    inv = ROPE_THETA ** (-jnp.arange(half, dtype=F32) / half)
    ang = positions[0].astype(F32)[:, None] * inv
    cos, sin = jnp.cos(ang), jnp.sin(ang)
    cosq = jnp.tile(jnp.concatenate([cos, cos], axis=-1), (1, 2))
    sinq = jnp.tile(jnp.concatenate([-sin, sin], axis=-1), (1, 2))

    row = lambda a, l: a[l].reshape(1, -1)
    memn = _prenorm("mem_norm", mem[0], mem_norm_g.reshape(1, d), tmn)
    pw_bd = []
    for l in range(depth):
        bd = jnp.zeros((256, 256), F32)
        for g in range(4):
            bd = lax.dynamic_update_slice(bd, pool_w[l, g], (64 * g, 64 * g))
        pw_bd.append(bd)

    saved = []
    xc = x0
    h = _prenorm("pre_norm0", x0, row(mix_pre_g, 0) + P['first_dep'], tmn)
    for l in range(depth):
        W = weights_of(l, ['w_in'], xc)
        sv = {'x0': xc, 'h1': h}
        z = _mm_nn("fwd_w_in", h, *W['w_in'], tm=1024, tn=512, tk=d, out_dtype=F32)
        abc = _mixer_fwd("mixer_fwd", z, cosq, sinq, row(gm_v_g, l), gm_w_s[l], gm_b_s[l].T, pw_bd[l],
                         row(pool_scale, l), row(attn_sinks, l))
        W.update(weights_of(l, ['w_o'], z))
        mix, xc, h = _mm_rows("fwd_w_o", abc, *W['w_o'], 'nn', tm=tmr, tk=d, rows_in=[xc],
                              params=[row(mix_post_g, l), row(x_pre_g, l)], rows_out=[BF16, F32, BF16],
                              epilogue=_post_pre_rows)
        sv.update(z=z, abc=abc, mix=mix, x1=xc, h2=h)
        W.update(weights_of(l, ['w_xq', 'w_xkv', 'w_xo'], xc))
        q = _mm_nn("fwd_w_xq", h, *W['w_xq'], tm=1024, tn=512, tk=d, out_dtype=BF16)
        kv = _mm_nn("fwd_w_xkv", memn, *W['w_xkv'], tm=256, tn=512, tk=d, out_dtype=BF16)
        o = _xattn_fwd("xattn_fwd", q, kv, 512)
        xo, xc, h = _mm_rows("fwd_w_xo", o, *W['w_xo'], 'nn', tm=tmr, tk=d, rows_in=[xc],
                             params=[row(x_post_g, l), row(ffn_pre_g, l)], rows_out=[BF16, F32, BF16],
                             epilogue=_post_pre_rows)
        sv.update(q=q, kv=kv, o=o, xo=xo, x2=xc, h3=h)
        W.update(weights_of(l, ['w_gate_up', 'w_down'], xc))
        dff = W['w_down'][0].shape[1]
        gate, up, act = _ffn_up("ffn_up", h, *W['w_gate_up'], 512, dff // 2)
        sv.update(gate=gate, up=up, act=act)
        if l + 1 < depth:
            f, xc, h = _mm_rows("fwd_w_down", act, *W['w_down'], 'nn', tm=tmr, tk=dff, rows_in=[xc],
                                params=[row(ffn_post_g, l), row(mix_pre_g, l + 1)], rows_out=[BF16, F32, BF16],
                                epilogue=_post_pre_rows)
            sv.update(f=f)
        saved.append(sv)
    gs = {n: [None] * depth for n in SMALL if n != 'mem_norm_g'}
    dx, dfn, gs['ffn_post_g'][depth - 1], loss_part = _mm_rows(
        "fwd_w_down_loss", saved[-1]['act'], *W['w_down'], 'nn', tm=tmr, tk=dff, rows_in=[xc, tgt],
        params=[row(ffn_post_g, depth - 1)], rows_out=[F32, BF16], n_sums=2, epilogue=_make_loss_rows(d))

    dmemn = None
    tok = jnp.zeros((1, 1), F32)
    for l in reversed(range(depth)):
        sv, W, G = saved[l], weights_of(l, BIG, dx), {}
        G['w_down'] = _mm_tn("dw_down", sv['act'], dfn, tm=dff // 2, tn=d, tk=tkw)
        dgu = _ffn_act_bwd("ffn_act_bwd", dfn, *W['w_down'], sv['gate'], sv['up'], 256)
        G['w_gate_up'] = _mm_tn("dw_gate_up", sv['h3'], dgu, tm=d, tn=dff // 2, tk=tkw)
        dx, dxo, gs['ffn_pre_g'][l], gs['x_post_g'][l] = _mm_rows(
            "bwd_w_gate_up", dgu, *W['w_gate_up'], 'nt', tm=tmr, tk=dff, rows_in=[sv['x2'], dx, sv['xo']],
            params=[row(ffn_pre_g, l) + tok, row(x_post_g, l)], rows_out=[F32, BF16], n_sums=2, epilogue=_bwd_rows)
        tok = grads_of(l, {n: G[n] for n in ('w_gate_up', 'w_down')}, dx)
        G['w_xo'] = _mm_tn("dw_xo", sv['o'], dxo, tm=d, tn=d, tk=tkw)
        do = _mm_nt("bwd_w_xo", dxo, *W['w_xo'], tm=1024, tn=512, tk=d, out_dtype=BF16, dep=tok)
        dq, dkv = _xattn_bwd("xattn_bwd", sv['q'], sv['kv'], do, 512)
        dkv = dkv.astype(BF16)
        G['w_xkv'] = _mm_tn("dw_xkv", memn, dkv, tm=d, tn=d, tk=mem.shape[1])
        dmemn = _mm_nt("bwd_w_xkv", dkv, *W['w_xkv'], tm=mem.shape[1], tn=512, tk=2 * d, out_dtype=F32, add=dmemn)
        G['w_xq'] = _mm_tn("dw_xq", sv['h2'], dq, tm=d, tn=d, tk=tkw)
        dx, dmix, gs['x_pre_g'][l], gs['mix_post_g'][l] = _mm_rows(
            "bwd_w_xq", dq, *W['w_xq'], 'nt', tm=tmr, tk=d, rows_in=[sv['x1'], dx, sv['mix']],
            params=[row(x_pre_g, l), row(mix_post_g, l)], rows_out=[F32, BF16], n_sums=2, epilogue=_bwd_rows)
        tok = grads_of(l, {n: G[n] for n in ('w_xq', 'w_xkv', 'w_xo')}, dx)
        G['w_o'] = _mm_tn("dw_o", sv['abc'], dmix, tm=d, tn=d, tk=tkw)
        dabc = _mm_nt("bwd_w_o", dmix, *W['w_o'], tm=1024, tn=512, tk=d, out_dtype=F32, dep=tok)
        tok = grads_of(l, {'w_o': G['w_o']}, dabc)
        dz, dgv, dws, dbt, dpw, dpsc, dsnk = _mixer_bwd(
            "mixer_bwd", sv['z'], dabc, cosq, sinq, row(gm_v_g, l) + tok, gm_w_s[l], gm_b_s[l].T, pw_bd[l],
            row(pool_scale, l), row(attn_sinks, l))
        gs['gm_v_g'][l] = dgv
        gs['gm_w_s'][l] = dws
        gs['gm_b_s'][l] = dbt.T
        gs['pool_w'][l] = jnp.stack([dpw[64 * g:64 * (g + 1), 64 * g:64 * (g + 1)] for g in range(4)])
        gs['pool_scale'][l] = dpsc
        gs['attn_sinks'][l] = dsnk
        G['w_in'] = _mm_tn("dw_in", sv['h1'], dz, tm=d, tn=dz.shape[1], tk=tkw)
        if l > 0:
            dx, dfn, gs['mix_pre_g'][l], gs['ffn_post_g'][l - 1] = _mm_rows(
                "bwd_w_in", dz, *W['w_in'], 'nt', tm=tmr, tk=dz.shape[1], rows_in=[sv['x0'], dx, saved[l - 1]['f']],
                params=[row(mix_pre_g, l), row(ffn_post_g, l - 1)], rows_out=[F32, BF16], n_sums=2,
                epilogue=_bwd_rows)
        else:
            dx, gs['mix_pre_g'][l] = _mm_rows(
                "bwd_w_in_first", dz, *W['w_in'], 'nt', tm=tmr, tk=dz.shape[1], rows_in=[sv['x0'], dx],
                params=[row(mix_pre_g, l)], rows_out=[F32], n_sums=1, epilogue=_bwd_rows_first)
        tok = grads_of(l, {'w_in': G['w_in']}, dx)
    _, dg_mem = _norm_bwd("bwd_mem_norm", mem[0], mem_norm_g.reshape(1, d), dmemn, None, BF16, tmn)
    small_g = []
    for n in SMALL:
        if n == 'mem_norm_g':
            small_g.append(dg_mem.reshape(P[n].shape))
        else:
            small_g.append(jnp.stack([a.reshape(P[n].shape[1:]) for a in gs[n]]))
    return loss_part, dx, small_g
```

```python
import functools

import jax
import jax.numpy as jnp
from jax import lax
from jax.experimental import pallas as pl
from jax.experimental.pallas import tpu as pltpu

F32 = jnp.float32
BF16 = jnp.bfloat16
EPS = 1e-6
CHUNK = 128
HEAD = 64
ROPE_THETA = 10000.0
POOL_WINDOWS = (2, 4, 8, 16)
LR, B1, B2, ADAM_EPS, WD, STEP = 0.001, 0.9, 0.999, 1e-08, 0.01, 10
MESH = pl.DeviceIdType.MESH
VMEM_LIMIT = 56 * 1024 * 1024

NAMES = ['x', 'mem', 'positions', 'mem_norm_g', 'mix_pre_g', 'mix_post_g', 'w_in', 'gm_v_g', 'gm_w_s', 'gm_b_s',
         'pool_w', 'pool_scale', 'attn_sinks', 'w_o', 'x_pre_g', 'x_post_g', 'w_xq', 'w_xkv', 'w_xo', 'ffn_pre_g',
         'ffn_post_g', 'w_gate_up', 'w_down']
WEIGHTS = NAMES[3:]
BIG = ['w_in', 'w_o', 'w_xq', 'w_xkv', 'w_xo', 'w_gate_up', 'w_down']
BIG_AXIS = {'w_in': 2, 'w_o': 1, 'w_xq': 1, 'w_xkv': 2, 'w_xo': 1, 'w_gate_up': 2, 'w_down': 1}
SMALL = [n for n in WEIGHTS if n not in BIG]

NN = (((1,), (0,)), ((), ()))
NT = (((1,), (1,)), ((), ()))
TN = (((0,), (0,)), ((), ()))


def _dot(a, b, dims=NN):
    return lax.dot_general(a, b, dims, preferred_element_type=F32)


def _params(sem):
    return pltpu.CompilerParams(dimension_semantics=sem, vmem_limit_bytes=VMEM_LIMIT)


def _rows_tile(rows, limit=256):
    return max(t for t in range(16, limit + 1, 16) if rows % t == 0)


def _mm(name, a, a_spec, b, b_spec, dims, grid, nk, out_shape, out_spec, add=None, add_spec=None, dep=None):
    acc_shape = out_spec.block_shape

    def body(*refs):
        a_ref, b_ref = refs[0], refs[1]
        pos = 2
        add_ref = None
        if add is not None:
            add_ref = refs[pos]
            pos += 1
        if dep is not None:
            pos += 1
        o_ref = refs[pos]
        part = _dot(a_ref[...].astype(BF16), b_ref[...].astype(BF16), dims)
        if nk == 1:
            if add_ref is not None:
                part = part + add_ref[...]
            o_ref[...] = part.astype(o_ref.dtype)
        else:
            acc_ref = refs[pos + 1]
            k = pl.program_id(2)

            @pl.when(k == 0)
            def _():
                acc_ref[...] = part if add_ref is None else part + add_ref[...]

            @pl.when(k > 0)
            def _():
                acc_ref[...] += part

            @pl.when(k == nk - 1)
            def _():
                o_ref[...] = acc_ref[...].astype(o_ref.dtype)

    ops, specs = [a, b], [a_spec, b_spec]
    if add is not None:
        ops.append(add)
        specs.append(add_spec)
    if dep is not None:
        ops.append(dep)
        specs.append(pl.BlockSpec((1, 1), lambda i, j, k: (0, 0)))
    return pl.pallas_call(
        body, name=name, grid=grid, in_specs=specs, out_specs=out_spec, out_shape=out_shape,
        scratch_shapes=[pltpu.VMEM(acc_shape, F32)] if nk > 1 else [],
        compiler_params=_params(("parallel", "parallel", "arbitrary")),
    )(*ops)


def _wspec(block, layer, fn):
    return pl.BlockSpec((None,) + block, lambda i, j, k: (layer,) + fn(i, j, k))


def _mm_nn(name, a, w, layer, *, tm, tn, tk, out_dtype, n0=0, n=None, k0=0):
    m, kk = a.shape
    n = w.shape[2] if n is None else n
    tm = min(tm, m)
    nk = kk // tk
    return _mm(name, a, pl.BlockSpec((tm, tk), lambda i, j, k: (i, k)),
               w, _wspec((tk, tn), layer, lambda i, j, k: (k + k0 // tk, j + n0 // tn)), NN,
               (m // tm, n // tn, nk), nk, jax.ShapeDtypeStruct((m, n), out_dtype),
               pl.BlockSpec((tm, tn), lambda i, j, k: (i, j)))


def _mm_nt(name, a, w, layer, *, tm, tn, tk, out_dtype, k0=0, add=None, dep=None):
    m, kk = a.shape
    n = w.shape[1]
    tm = min(tm, m)
    nk = kk // tk
    ospec = pl.BlockSpec((tm, tn), lambda i, j, k: (i, j))
    return _mm(name, a, pl.BlockSpec((tm, tk), lambda i, j, k: (i, k)),
               w, _wspec((tn, tk), layer, lambda i, j, k: (j, k + k0 // tk)), NT,
               (m // tm, n // tn, nk), nk, jax.ShapeDtypeStruct((m, n), out_dtype), ospec,
               add=add, add_spec=ospec if add is not None else None, dep=dep)


def _mm_tn(name, a, b, *, tm, tn, tk):
    kk, m = a.shape
    n = b.shape[1]
    tk = min(tk, kk)
    nk = kk // tk
    return _mm(name, a, pl.BlockSpec((tk, tm), lambda i, j, k: (k, i)),
               b, pl.BlockSpec((tk, tn), lambda i, j, k: (k, j)), TN,
               (m // tm, n // tn, nk), nk, jax.ShapeDtypeStruct((m, n), BF16),
               pl.BlockSpec((tm, tn), lambda i, j, k: (i, j)))


def _mm_rows(name, a, w, layer, mode, *, tm, tk, k0=0, rows_in=(), params=(), rows_out=(), n_sums=0, epilogue):
    m, kk = a.shape
    n = w.shape[2] if mode == 'nn' else w.shape[1]
    nk = kk // tk
    nr, npar, no = len(rows_in), len(params), len(rows_out)

    def body(*refs):
        a_ref, w_ref = refs[0], refs[1]
        rin = refs[2:2 + nr]
        par = refs[2 + nr:2 + nr + npar]
        outs = refs[2 + nr + npar:2 + nr + npar + no]
        sums = refs[2 + nr + npar + no:2 + nr + npar + no + n_sums]
        i, k = pl.program_id(0), pl.program_id(1)
        part = _dot(a_ref[...], w_ref[...], NN if mode == 'nn' else NT)

        def finish(acc):
            res, sm = epilogue(acc, [r[...] for r in rin], [p[...] for p in par])
            for r, v in zip(outs, res):
                r[...] = v.astype(r.dtype)

            @pl.when(i == 0)
            def _():
                for r in sums:
                    r[...] = jnp.zeros_like(r)

            for r, v in zip(sums, sm):
                r[...] += v

        if nk == 1:
            finish(part)
        else:
            acc_ref = refs[-1]

            @pl.when(k == 0)
            def _():
                acc_ref[...] = part

            @pl.when(k > 0)
            def _():
                acc_ref[...] += part

            @pl.when(k == nk - 1)
            def _():
                finish(acc_ref[...])

    once = pl.Buffered(1) if nk == 1 else None
    if mode == 'nn':
        wspec = pl.BlockSpec((None, tk, n), lambda i, k: (layer, k + k0 // tk, 0), pipeline_mode=once)
    else:
        wspec = pl.BlockSpec((None, n, tk), lambda i, k: (layer, 0, k + k0 // tk), pipeline_mode=once)
    rowblk = pl.BlockSpec((tm, n), lambda i, k: (i, 0))
    one = pl.BlockSpec((1, n), lambda i, k: (0, 0))
    return pl.pallas_call(
        body, name=name, grid=(m // tm, nk),
        in_specs=[pl.BlockSpec((tm, tk), lambda i, k: (i, k)), wspec] + [rowblk] * nr + [one] * npar,
        out_specs=[rowblk] * no + [one] * n_sums,
        out_shape=[jax.ShapeDtypeStruct((m, n), dt) for dt in rows_out] +
                  [jax.ShapeDtypeStruct((1, n), F32)] * n_sums,
        scratch_shapes=[pltpu.VMEM((tm, n), F32)] if nk > 1 else [],
        compiler_params=_params(("arbitrary", "arbitrary")),
    )(a, w, *rows_in, *params)


def _rstd(x):
    return lax.rsqrt(jnp.mean(x * x, axis=-1, keepdims=True) + EPS)


def _norm_back(xin, g, dy):
    r = _rstd(xin)
    xh = xin * r
    dyg = dy * g
    return r * (dyg - xh * jnp.mean(dyg * xh, axis=-1, keepdims=True)), jnp.sum(dy * xh, axis=0, keepdims=True)


def _post_pre_rows(y, rows, pars):
    xn = rows[0] + y * _rstd(y) * pars[0]
    return [y, xn, xn * _rstd(xn) * pars[1]], []


def _make_loss_rows(d):
    def fn(y, rows, pars):
        x, tgt = rows
        err = x + y * _rstd(y) * pars[0] - tgt
        dout = err * (1.0 / d)
        dy, dg = _norm_back(y, pars[0], dout)
        lsum = 0.5 * jnp.sum(jnp.mean(err * err, axis=-1, keepdims=True), axis=0, keepdims=True)
        return [dout, dy], [dg, jnp.broadcast_to(lsum, dg.shape)]
    return fn


def _bwd_rows(dh, rows, pars):
    xin, resid, yprev = rows
    dxa, dg_pre = _norm_back(xin, pars[0], dh)
    dx = resid + dxa
    dyp, dg_post = _norm_back(yprev.astype(F32), pars[1], dx)
    return [dx, dyp], [dg_pre, dg_post]


def _bwd_rows_first(dh, rows, pars):
    xin, resid = rows
    dxa, dg_pre = _norm_back(xin, pars[0], dh)
    return [resid + dxa], [dg_pre]


def _row(d):
    return pl.BlockSpec((1, d), lambda i: (0, 0))


def _prenorm(name, x, g, tm):
    m, d = x.shape

    def body(x_ref, g_ref, o_ref):
        xv = x_ref[...]
        o_ref[...] = (xv * _rstd(xv) * g_ref[...]).astype(BF16)

    blk = pl.BlockSpec((tm, d), lambda i: (i, 0))
    return pl.pallas_call(body, name=name, grid=(m // tm,), in_specs=[blk, _row(d)], out_specs=blk,
                          out_shape=jax.ShapeDtypeStruct((m, d), BF16), compiler_params=_params(("parallel",)))(x, g)


def _norm_bwd(name, xin, g, dy, resid, out_dtype, tm):
    m, d = xin.shape

    def body(*refs):
        if resid is None:
            x_ref, g_ref, dy_ref, dx_ref, dg_ref = refs
        else:
            x_ref, g_ref, dy_ref, r_ref, dx_ref, dg_ref = refs
        xv = x_ref[...]
        r = _rstd(xv)
        xh = xv * r
        dyv = dy_ref[...].astype(F32)
        dyg = dyv * g_ref[...]
        dx = r * (dyg - xh * jnp.mean(dyg * xh, axis=-1, keepdims=True))
        if resid is not None:
            dx = dx + r_ref[...]
        dx_ref[...] = dx.astype(dx_ref.dtype)

        @pl.when(pl.program_id(0) == 0)
        def _():
            dg_ref[...] = jnp.zeros_like(dg_ref)

        dg_ref[...] += jnp.sum(dyv * xh, axis=0, keepdims=True)

    blk = pl.BlockSpec((tm, d), lambda i: (i, 0))
    ops = [xin, g, dy] + ([] if resid is None else [resid])
    specs = [blk, _row(d), blk] + ([] if resid is None else [blk])
    return pl.pallas_call(
        body, name=name, grid=(m // tm,), in_specs=specs, out_specs=[blk, _row(d)],
        out_shape=[jax.ShapeDtypeStruct((m, d), out_dtype), jax.ShapeDtypeStruct((1, d), F32)],
        compiler_params=_params(("arbitrary",)))(*ops)


def _adamw(name, g, w, m, v, tr):
    rows, cols = g.shape
    c1 = 1.0 - B1 ** STEP
    c2 = 1.0 - B2 ** STEP

    def body(g_ref, w_ref, m_ref, v_ref, d_ref, mo_ref, vo_ref):
        gv = g_ref[...]
        mn = B1 * m_ref[...] + (1.0 - B1) * gv
        vn = B2 * v_ref[...] + (1.0 - B2) * (gv * gv)
        mo_ref[...] = mn
        vo_ref[...] = vn
        d_ref[...] = -LR * ((mn / c1) / (jnp.sqrt(vn / c2) + ADAM_EPS) + WD * w_ref[...])

    blk = pl.BlockSpec((tr, cols), lambda i: (i, 0))
    sd = jax.ShapeDtypeStruct((rows, cols), F32)
    return pl.pallas_call(body, name=name, grid=(rows // tr,), in_specs=[blk] * 4, out_specs=[blk] * 3,
                          out_shape=[sd, sd, sd], compiler_params=_params(("parallel",)))(g, w, m, v)


def _gelu_parts(x):
    c = 0.7978845608028654
    t = jnp.tanh(c * (x + 0.044715 * (x * x * x)))
    return 0.5 * x * (1.0 + t), t


def _gelu_grad(x, t):
    c = 0.7978845608028654
    return 0.5 * (1.0 + t) + 0.5 * x * (1.0 - t * t) * (c * (1.0 + 3.0 * 0.044715 * x * x))


def _rot_half(x):
    ax = x.ndim - 1
    w = x.shape[ax]
    lane = lax.broadcasted_iota(jnp.int32, x.shape, ax)
    return jnp.where((lane & 63) < 32, pltpu.roll(x, w - 32, ax), pltpu.roll(x, 32, ax))


def _group_mean(x, ones_bd):
    hi = x.astype(BF16)
    lo = (x - hi.astype(F32)).astype(BF16)
    return (_dot(hi, ones_bd) + _dot(lo, ones_bd)) * (1.0 / HEAD)


def _gating(gel, gv, ws_ref, bt, ones_bd, mix_s):
    u = gel[:, :256]
    v = gel[:, 256:]
    r = lax.rsqrt(_group_mean(v * v, ones_bd) + EPS)
    xh = v * r
    vn = (xh * gv).astype(BF16)
    row = lax.broadcasted_iota(jnp.int32, (CHUNK, CHUNK), 0)
    col = lax.broadcasted_iota(jnp.int32, (CHUNK, CHUNK), 1)
    causal = col <= row
    wcs = []
    for g in range(4):
        wc = jnp.where(causal, ws_ref[g], 0.0).astype(BF16)
        wcs.append(wc)
        mix_s[:, HEAD * g:HEAD * (g + 1)] = _dot(wc, vn[:, HEAD * g:HEAD * (g + 1)]) + bt[:, g:g + 1]
    return u, r, xh, vn, wcs, causal, mix_s[...]


def _lane_select(lane, vals):
    return jnp.where(lane < 64, vals[0], jnp.where(lane < 128, vals[1], jnp.where(lane < 192, vals[2], vals[3])))


def _pool_fwd(pc, pp, ci):
    ext = jnp.concatenate([pp, pc], axis=0)
    s2 = ext + pltpu.roll(ext, 1, 0)
    s4 = s2 + pltpu.roll(s2, 2, 0)
    s8 = s4 + pltpu.roll(s4, 4, 0)
    s16 = s8 + pltpu.roll(s8, 8, 0)
    t1 = ci * CHUNK + lax.broadcasted_iota(jnp.int32, (CHUNK, 1), 0) + 1
    lane = lax.broadcasted_iota(jnp.int32, (1, 256), 1)
    cnt = _lane_select(lane, [jnp.minimum(t1, w).astype(F32) for w in POOL_WINDOWS])
    ssel = _lane_select(lane, [s[CHUNK:] for s in (s2, s4, s8, s16)])
    return ssel / cnt - pc, cnt, lane


def _attn_prep(zc, zpkv, cc, sc, cp, sp, ci):
    q = zc[:, 768:1280]
    kc = zc[:, 1280:1408]
    vc = zc[:, 1408:1536]
    kp = zpkv[:, :128]
    vp = zpkv[:, 128:]
    qr = q * jnp.concatenate([cc] * 4, axis=1) + _rot_half(q) * jnp.concatenate([sc] * 4, axis=1)
    krc = kc * cc + _rot_half(kc) * sc
    krp = kp * cp + _rot_half(kp) * sp
    kband = jnp.concatenate([krp, krc], axis=0)
    vband = jnp.concatenate([vp, vc], axis=0)
    key = lax.broadcasted_iota(jnp.int32, (2 * CHUNK, 4 * CHUNK), 0)
    t = lax.broadcasted_iota(jnp.int32, (2 * CHUNK, 4 * CHUNK), 1) & (CHUNK - 1)
    valid = ((key < CHUNK) & (key > t) & (ci > 0)) | ((key >= CHUNK) & (key - CHUNK <= t))
    return qr, kband, vband, valid


SCALE = HEAD ** -0.5


def _stack_heads(x, base, hk):
    return jnp.concatenate([x[:, base + HEAD * (4 * hk + i):base + HEAD * (4 * hk + i + 1)] for i in range(4)], axis=0)


def _sink_row(snk, hk):
    lane = lax.broadcasted_iota(jnp.int32, (1, 4 * CHUNK), 1)
    s = [snk[:, 4 * hk + i:4 * hk + i + 1] for i in range(4)]
    return jnp.where(lane < CHUNK, s[0], jnp.where(lane < 2 * CHUNK, s[1], jnp.where(lane < 3 * CHUNK, s[2], s[3])))


def _group_probs(kh, q4, valid, sink4):
    s = jnp.where(valid, _dot(kh, q4, NT), -1e30)
    mx = jnp.maximum(jnp.max(s, axis=0, keepdims=True), sink4)
    e = jnp.exp(s - mx)
    es = jnp.exp(sink4 - mx)
    inv = 1.0 / (jnp.sum(e, axis=0, keepdims=True) + es)
    return e * inv, es * inv


def _mixer_specs(nb, rev):
    def cur(i):
        return nb - 1 - i if rev else i

    def prev(i):
        return jnp.maximum(cur(i) - 1, 0)

    full = lambda shape: pl.BlockSpec(shape, lambda i: (0,) * len(shape))
    specs = [
        pl.BlockSpec((CHUNK, 1536), lambda i: (cur(i), 0)),
        pl.BlockSpec((CHUNK, 256), lambda i: (prev(i), 2)),
        pl.BlockSpec((CHUNK, 256), lambda i: (prev(i), 5)),
        pl.BlockSpec((CHUNK, 128), lambda i: (cur(i), 0)),
        pl.BlockSpec((CHUNK, 128), lambda i: (cur(i), 0)),
        pl.BlockSpec((CHUNK, 128), lambda i: (prev(i), 0)),
        pl.BlockSpec((CHUNK, 128), lambda i: (prev(i), 0)),
        full((1, 256)), full((4, CHUNK, CHUNK)), full((CHUNK, 4)), full((256, 256)), full((1, 256)), full((1, 8)),
        full((256, 256)),
    ]
    return specs, cur


def _ones_bd():
    g = lax.broadcasted_iota(jnp.int32, (256, 256), 0) // HEAD == lax.broadcasted_iota(jnp.int32, (256, 256), 1) // HEAD
    return g.astype(BF16)


def _mixer_fwd(name, z, cosq, sinq, gv, ws, bt, pw, psc, snk):
    s = z.shape[0]
    nb = s // CHUNK
    specs, _ = _mixer_specs(nb, False)

    def body(zc_ref, zpp_ref, zpkv_ref, cq_ref, sq_ref, cp_ref, sp_ref, gv_ref, ws_ref, bt_ref, pw_ref, psc_ref,
             snk_ref, bd_ref, o_ref, mix_s):
        ci = pl.program_id(0)
        zc = zc_ref[...]
        gel, _ = _gelu_parts(zc[:, :512])
        u, _, _, _, _, _, mixed = _gating(gel, gv_ref[...], ws_ref, bt_ref[...], bd_ref[...], mix_s)
        o_ref[:, :256] = (u * mixed).astype(BF16)
        pp = jnp.where(ci > 0, zpp_ref[...], 0.0)
        pooled, _, _ = _pool_fwd(zc[:, 512:768], pp, ci)
        mp = _dot(pooled.astype(BF16), pw_ref[...].astype(BF16))
        o_ref[:, 256:512] = (mp * psc_ref[...]).astype(BF16)
        qr, kband, vband, valid = _attn_prep(zc, zpkv_ref[...], cq_ref[...], sq_ref[...], cp_ref[...], sp_ref[...], ci)
        snkv = snk_ref[...]
        kb = kband.astype(BF16)
        vt = vband.T
        ots = []
        for hk in range(2):
            q4 = (_stack_heads(qr, 0, hk) * SCALE).astype(BF16)
            p, _ = _group_probs(kb[:, HEAD * hk:HEAD * (hk + 1)], q4, valid, _sink_row(snkv, hk))
            ots.append(_dot(vt[HEAD * hk:HEAD * (hk + 1), :].astype(BF16), p.astype(BF16)))
        o = jnp.concatenate(ots, axis=0).T
        for hk in range(2):
            for i in range(4):
                h = 4 * hk + i
                o_ref[:, 512 + HEAD * h:512 + HEAD * (h + 1)] = o[CHUNK * i:CHUNK * (i + 1),
                                                                  HEAD * hk:HEAD * (hk + 1)].astype(BF16)

    return pl.pallas_call(
        body, name=name, grid=(nb,), in_specs=specs, out_specs=pl.BlockSpec((CHUNK, 1024), lambda i: (i, 0)),
        out_shape=jax.ShapeDtypeStruct((s, 1024), BF16), scratch_shapes=[pltpu.VMEM((CHUNK, 256), F32)],
        compiler_params=_params(("parallel",)),
    )(z, z, z, cosq, sinq, cosq, sinq, gv, ws, bt, pw, psc, snk, _ones_bd())


def _mixer_bwd(name, z, dabc, cosq, sinq, gv, ws, bt, pw, psc, snk):
    s = z.shape[0]
    nb = s // CHUNK
    specs, cur = _mixer_specs(nb, True)
    specs = specs + [pl.BlockSpec((CHUNK, 1024), lambda i: (cur(i), 0))]
    full = lambda shape: pl.BlockSpec(shape, lambda i: (0,) * len(shape))
    acc_shapes = [(1, 256), (4, CHUNK, CHUNK), (CHUNK, 4), (256, 256), (1, 256), (1, 8)]

    def body(zc_ref, zpp_ref, zpkv_ref, cq_ref, sq_ref, cp_ref, sp_ref, gv_ref, ws_ref, bt_ref, pw_ref, psc_ref,
             snk_ref, bd_ref, dabc_ref, dz_ref, dgv_ref, dws_ref, dbt_ref, dpw_ref, dpsc_ref, dsnk_ref,
             cpool, ck, cv, dq_s, dkv_s, mix_s, dvn_s):
        step = pl.program_id(0)
        ci = nb - 1 - step

        @pl.when(step == 0)
        def _():
            for r in (dgv_ref, dws_ref, dbt_ref, dpw_ref, dpsc_ref, dsnk_ref, cpool, ck, cv):
                r[...] = jnp.zeros_like(r)

        zc = zc_ref[...]
        dabc = dabc_ref[...]
        zg = zc[:, :512]
        gel, th = _gelu_parts(zg)
        gp = _gelu_grad(zg, th)
        gvv = gv_ref[...]
        bd = bd_ref[...]
        u, r, xh, vn, wcs, causal, mixed = _gating(gel, gvv, ws_ref, bt_ref[...], bd, mix_s)
        da = dabc[:, :256]
        dm = da * u
        dmb = dm.astype(BF16)
        lane4 = lax.broadcasted_iota(jnp.int32, (CHUNK, 4), 1)
        dbt = jnp.zeros((CHUNK, 4), F32)
        for g in range(4):
            lo, hi = HEAD * g, HEAD * (g + 1)
            dws_ref[g] += jnp.where(causal, _dot(dmb[:, lo:hi], vn[:, lo:hi], NT), 0.0)
            dbt = dbt + jnp.where(lane4 == g, jnp.sum(dm[:, lo:hi], axis=-1, keepdims=True), 0.0)
            dvn_s[:, lo:hi] = _dot(wcs[g], dmb[:, lo:hi], TN)
        dbt_ref[...] += dbt
        dvn = dvn_s[...]
        dgv_ref[...] += jnp.sum(dvn * xh, axis=0, keepdims=True)
        dxh = dvn * gvv
        dvg = r * (dxh - xh * _group_mean(dxh * xh, bd))
        dz_ref[:, :256] = (da * mixed * gp[:, :256]).astype(BF16)
        dz_ref[:, 256:512] = (dvg * gp[:, 256:]).astype(BF16)
        pc = zc[:, 512:768]
        pp = jnp.where(ci > 0, zpp_ref[...], 0.0)
        pooled, cnt, lane = _pool_fwd(pc, pp, ci)
        pwb = pw_ref[...].astype(BF16)
        pooled_b = pooled.astype(BF16)
        mp = _dot(pooled_b, pwb)
        db = dabc[:, 256:512]
        dpsc_ref[...] += jnp.sum(db * mp, axis=0, keepdims=True)
        dmpb = (db * psc_ref[...]).astype(BF16)
        dpw_ref[...] += _dot(pooled_b, dmpb, TN)
        dpooled = _dot(dmpb, pwb, NT)
        davg = dpooled / cnt
        zero = jnp.zeros((CHUNK, 256), F32)
        d2, d4, d8, d16 = [jnp.concatenate([zero, jnp.where((lane >= 64 * k) & (lane < 64 * (k + 1)), davg, 0.0)],
                                           axis=0) for k in range(4)]
        g8 = d8 + d16 + pltpu.roll(d16, 2 * CHUNK - 8, 0)
        g4 = d4 + g8 + pltpu.roll(g8, 2 * CHUNK - 4, 0)
        g2 = d2 + g4 + pltpu.roll(g4, 2 * CHUNK - 2, 0)
        ge = g2 + pltpu.roll(g2, 2 * CHUNK - 1, 0)
        dz_ref[:, 512:768] = (ge[CHUNK:] - dpooled + cpool[...]).astype(BF16)
        cpool[...] = ge[:CHUNK]
        cc = cq_ref[...]
        sc = sq_ref[...]
        qr, kband, vband, valid = _attn_prep(zc, zpkv_ref[...], cc, sc, cp_ref[...], sp_ref[...], ci)
        snkv = snk_ref[...]
        lane8 = lax.broadcasted_iota(jnp.int32, (1, 8), 1)
        qlane = lax.broadcasted_iota(jnp.int32, (1, 4 * CHUNK), 1)
        dsnk = jnp.zeros((1, 8), F32)
        kb = kband.astype(BF16)
        vb = vband.astype(BF16)
        kt = kband.T * SCALE
        dqts = []
        for hk in range(2):
            kh = kb[:, HEAD * hk:HEAD * (hk + 1)]
            q4 = (_stack_heads(qr, 0, hk) * SCALE).astype(BF16)
            do4 = _stack_heads(dabc, 512, hk).astype(BF16)
            p, ps = _group_probs(kh, q4, valid, _sink_row(snkv, hk))
            dp = _dot(vb[:, HEAD * hk:HEAD * (hk + 1)], do4, NT)
            dd = jnp.sum(p * dp, axis=0, keepdims=True)
            dsink = -ps * dd
            for i in range(4):
                part = jnp.sum(jnp.where((qlane >= CHUNK * i) & (qlane < CHUNK * (i + 1)), dsink, 0.0),
                               axis=1, keepdims=True)
                dsnk = dsnk + jnp.where(lane8 == 4 * hk + i, part, 0.0)
            dsb = (p * (dp - dd)).astype(BF16)
            dqts.append(_dot(kt[HEAD * hk:HEAD * (hk + 1), :].astype(BF16), dsb))
            dkv_s[:, HEAD * hk:HEAD * (hk + 1)] = _dot(dsb, q4)
            dkv_s[:, 128 + HEAD * hk:128 + HEAD * (hk + 1)] = _dot(p.astype(BF16), do4)
        dq4 = jnp.concatenate(dqts, axis=0).T
        for hk in range(2):
            for i in range(4):
                h = 4 * hk + i
                dq_s[:, HEAD * h:HEAD * (h + 1)] = dq4[CHUNK * i:CHUNK * (i + 1), HEAD * hk:HEAD * (hk + 1)]
        dsnk_ref[...] += dsnk
        dqr = dq_s[...]
        dz_ref[:, 768:1280] = (dqr * jnp.concatenate([cc] * 4, axis=1)
                               + _rot_half(dqr * jnp.concatenate([sc] * 4, axis=1))).astype(BF16)
        dkv = dkv_s[...]
        dkr = dkv[CHUNK:, :128] + ck[...]
        dz_ref[:, 1280:1408] = (dkr * cc + _rot_half(dkr * sc)).astype(BF16)
        dz_ref[:, 1408:1536] = (dkv[CHUNK:, 128:] + cv[...]).astype(BF16)
        ck[...] = dkv[:CHUNK, :128]
        cv[...] = dkv[:CHUNK, 128:]

    return pl.pallas_call(
        body, name=name, grid=(nb,), in_specs=specs,
        out_specs=[pl.BlockSpec((CHUNK, 1536), lambda i: (cur(i), 0))] + [full(a) for a in acc_shapes],
        out_shape=[jax.ShapeDtypeStruct((s, 1536), BF16)] + [jax.ShapeDtypeStruct(a, F32) for a in acc_shapes],
        scratch_shapes=[pltpu.VMEM((CHUNK, 256), F32), pltpu.VMEM((CHUNK, 128), F32), pltpu.VMEM((CHUNK, 128), F32),
                        pltpu.VMEM((CHUNK, 512), F32), pltpu.VMEM((2 * CHUNK, 256), F32),
                        pltpu.VMEM((CHUNK, 256), F32), pltpu.VMEM((CHUNK, 256), F32)],
        compiler_params=_params(("arbitrary",)),
    )(z, z, z, cosq, sinq, cosq, sinq, gv, ws, bt, pw, psc, snk, _ones_bd(), dabc)


def _xattn_probs(qh, kh):
    s = _dot(qh, kh, NT) * (256 ** -0.5)
    e = jnp.exp(s - jnp.max(s, axis=-1, keepdims=True))
    return e * (1.0 / jnp.sum(e, axis=-1, keepdims=True))


def _xattn_fwd(name, q, kv, tq):
    s, d = q.shape
    mlen = kv.shape[0]

    def body(q_ref, kv_ref, o_ref):
        for h in range(4):
            lo, hi = 256 * h, 256 * (h + 1)
            p = _xattn_probs(q_ref[:, lo:hi], kv_ref[:, lo:hi])
            o_ref[:, lo:hi] = _dot(p.astype(BF16), kv_ref[:, d + lo:d + hi]).astype(BF16)

    blk = pl.BlockSpec((tq, d), lambda i: (i, 0))
    return pl.pallas_call(body, name=name, grid=(s // tq,),
                          in_specs=[blk, pl.BlockSpec((mlen, 2 * d), lambda i: (0, 0))], out_specs=blk,
                          out_shape=jax.ShapeDtypeStruct((s, d), BF16), compiler_params=_params(("parallel",)))(q, kv)


def _xattn_bwd(name, q, kv, do, tq):
    s, d = q.shape
    mlen = kv.shape[0]

    def body(q_ref, kv_ref, do_ref, dq_ref, dkv_ref):
        @pl.when(pl.program_id(0) == 0)
        def _():
            dkv_ref[...] = jnp.zeros_like(dkv_ref)

        for h in range(4):
            lo, hi = 256 * h, 256 * (h + 1)
            qh = q_ref[:, lo:hi]
            kh = kv_ref[:, lo:hi]
            vh = kv_ref[:, d + lo:d + hi]
            doh = do_ref[:, lo:hi]
            p = _xattn_probs(qh, kh)
            dp = _dot(doh, vh, NT)
            dsb = (p * (dp - jnp.sum(p * dp, axis=-1, keepdims=True)) * (256 ** -0.5)).astype(BF16)
            dq_ref[:, lo:hi] = _dot(dsb, kh).astype(BF16)
            dkv_ref[:, lo:hi] += _dot(dsb, qh, TN)
            dkv_ref[:, d + lo:d + hi] += _dot(p.astype(BF16), doh, TN)

    blk = pl.BlockSpec((tq, d), lambda i: (i, 0))
    kvb = pl.BlockSpec((mlen, 2 * d), lambda i: (0, 0))
    return pl.pallas_call(
        body, name=name, grid=(s // tq,), in_specs=[blk, kvb, blk], out_specs=[blk, kvb],
        out_shape=[jax.ShapeDtypeStruct((s, d), BF16), jax.ShapeDtypeStruct((mlen, 2 * d), F32)],
        compiler_params=_params(("arbitrary",)))(q, kv, do)


def _sigmoid(x):
    return 0.5 * (1.0 + jnp.tanh(0.5 * x))


def _ffn_up(name, h, wgu, layer, tm, tn):
    s, d = h.shape
    dff = wgu.shape[2] // 2
    nj = dff // tn

    def body(h_ref, wg_ref, wu_ref, g_ref, u_ref, a_ref):
        hv = h_ref[...]
        gate = _dot(hv, wg_ref[...])
        up = _dot(hv, wu_ref[...])
        g_ref[...] = gate.astype(BF16)
        u_ref[...] = up.astype(BF16)
        a_ref[...] = (gate * _sigmoid(gate) * up).astype(BF16)

    ob = pl.BlockSpec((tm, tn), lambda j, i: (i, j))
    sd = jax.ShapeDtypeStruct((s, dff), BF16)
    return pl.pallas_call(
        body, name=name, grid=(nj, s // tm),
        in_specs=[pl.BlockSpec((tm, d), lambda j, i: (i, 0)),
                  pl.BlockSpec((None, d, tn), lambda j, i: (layer, 0, j)),
                  pl.BlockSpec((None, d, tn), lambda j, i: (layer, 0, j + nj))],
        out_specs=[ob, ob, ob], out_shape=[sd, sd, sd], compiler_params=_params(("parallel", "parallel")),
    )(h, wgu, wgu)


def _ffn_act_bwd(name, dfn, wdown, layer, gate, up, tm):
    s, d = dfn.shape
    dff = gate.shape[1]

    def body(df_ref, wd_ref, g_ref, u_ref, o_ref):
        dact = _dot(df_ref[...], wd_ref[...], NT)
        gate = g_ref[...].astype(F32)
        sig = _sigmoid(gate)
        o_ref[:, :dff] = (dact * u_ref[...].astype(F32) * sig * (1.0 + gate * (1.0 - sig))).astype(BF16)
        o_ref[:, dff:] = (dact * gate * sig).astype(BF16)

    gb = pl.BlockSpec((tm, dff), lambda i: (i, 0))
    return pl.pallas_call(
        body, name=name, grid=(s // tm,),
        in_specs=[pl.BlockSpec((tm, d), lambda i: (i, 0)),
                  pl.BlockSpec((None, dff, d), lambda i: (layer, 0, 0)), gb, gb],
        out_specs=pl.BlockSpec((tm, 2 * dff), lambda i: (i, 0)),
        out_shape=jax.ShapeDtypeStruct((s, 2 * dff), BF16), compiler_params=_params(("parallel",)),
    )(dfn, wdown, gate, up)


def _place():
    return lax.axis_index("x"), lax.axis_index("y"), lax.axis_index("c")


def _other_chips(x, y):
    return [(1 - x, y), (x, 1 - y), (1 - x, 1 - y)]


def _region(ref, axis, chip, size):
    start = pl.multiple_of(chip * size, size)
    if axis == 1:
        return ref.at[:, pl.ds(start, size), :]
    return ref.at[:, :, pl.ds(start, size)]


ANY = pl.BlockSpec(memory_space=pl.ANY)


HBM = pl.BlockSpec(memory_space=pltpu.HBM)
SEM = pl.BlockSpec(memory_space=pltpu.SEMAPHORE)
EFFECT = pltpu.SideEffectType.DATAFLOW_SIDE_EFFECTING


def _in_hbm(a):
    return pltpu.with_memory_space_constraint(a, pltpu.HBM)


def _split_start(name, srcs, lands, ncopies, plan, after=None):
    ns, nl = len(srcs), len(lands)
    extra = [] if after is None else [after]

    def body(*refs):
        src, land = refs[:ns], refs[ns:ns + nl]
        send, recv = refs[ns + nl + len(extra)], refs[ns + nl + len(extra) + 1]
        token = refs[-1]
        x, y, c = _place()
        for k, (s_ref, d_ref, peer, _) in enumerate(plan(src, land, x, y, c)):
            pltpu.make_async_remote_copy(src_ref=s_ref, dst_ref=d_ref, send_sem=send.at[k], recv_sem=recv.at[k],
                                         device_id=peer, device_id_type=MESH).start()
        token[...] = jnp.zeros_like(token)

    ops = list(srcs) + list(lands)
    out = pl.pallas_call(
        body, name=name,
        out_shape=(pltpu.SemaphoreType.DMA((ncopies,)), pltpu.SemaphoreType.DMA((ncopies,)),
                   *[pltpu.HBM(a.shape, a.dtype) for a in ops], jax.ShapeDtypeStruct((8, 128), F32)),
        in_specs=(HBM,) * (ns + nl) + (ANY,) * len(extra),
        out_specs=(SEM, SEM) + (HBM,) * (ns + nl) + (pl.BlockSpec(memory_space=pltpu.VMEM),),
        input_output_aliases={i: 2 + i for i in range(ns + nl)},
        compiler_params=pltpu.CompilerParams(has_side_effects=EFFECT),
    )(*[_in_hbm(a) for a in ops], *extra)
    return out[0], out[1], list(out[2:2 + ns]), list(out[2 + ns:2 + ns + nl]), out[-1]


def _split_wait(name, send, recv, srcs, lands, after, plan):
    ns, nl = len(srcs), len(lands)

    def body(*refs):
        src, land = refs[:ns], refs[ns:ns + nl]
        send_ref, recv_ref = refs[ns + nl], refs[ns + nl + 1]
        x, y, c = _place()
        for k, (s_ref, _, _, got) in enumerate(plan(src, land, x, y, c)):
            cp = pltpu.make_async_remote_copy(src_ref=s_ref, dst_ref=got, send_sem=send_ref.at[k],
                                              recv_sem=recv_ref.at[k], device_id=(x, y, c), device_id_type=MESH)
            cp.wait_send()
            cp.wait_recv()

    ops = list(srcs) + list(lands)
    out = pl.pallas_call(
        body, name=name, out_shape=tuple(pltpu.HBM(a.shape, a.dtype) for a in ops),
        in_specs=(HBM,) * (ns + nl) + (SEM, SEM, ANY), out_specs=(HBM,) * (ns + nl),
        input_output_aliases={i: i for i in range(ns + nl)},
        compiler_params=pltpu.CompilerParams(has_side_effects=EFFECT),
    )(*ops, send, recv, after)
    return list(out[:ns]), list(out[ns:])


def _split_wait_start(name, send, recv, lands, after, wait_plan, ncopies, start_plan, carried=()):
    nl, nc = len(lands), len(carried)
    lands = list(lands) + list(carried)

    def body(*refs):
        land = refs[:nl]
        send_in, recv_in = refs[nl + nc], refs[nl + nc + 1]
        send_out, recv_out = refs[nl + nc + 3], refs[nl + nc + 4]
        x, y, c = _place()
        for k, (s_ref, _, _, got) in enumerate(wait_plan((), land, x, y, c)):
            cp = pltpu.make_async_remote_copy(src_ref=s_ref, dst_ref=got, send_sem=send_in.at[k],
                                              recv_sem=recv_in.at[k], device_id=(x, y, c), device_id_type=MESH)
            cp.wait_send()
            cp.wait_recv()
        for k, (s_ref, d_ref, peer, _) in enumerate(start_plan((), land, x, y, c)):
            pltpu.make_async_remote_copy(src_ref=s_ref, dst_ref=d_ref, send_sem=send_out.at[k],
                                         recv_sem=recv_out.at[k], device_id=peer, device_id_type=MESH).start()

    out = pl.pallas_call(
        body, name=name,
        out_shape=(pltpu.SemaphoreType.DMA((ncopies,)), pltpu.SemaphoreType.DMA((ncopies,)),
                   *[pltpu.HBM(a.shape, a.dtype) for a in lands]),
        in_specs=(HBM,) * (nl + nc) + (SEM, SEM, ANY), out_specs=(SEM, SEM) + (HBM,) * (nl + nc),
        input_output_aliases={i: 2 + i for i in range(nl + nc)},
        compiler_params=pltpu.CompilerParams(has_side_effects=EFFECT),
    )(*lands, send, recv, after)
    return out[0], out[1], list(out[2:2 + nl]), list(out[2 + nl:])


def _half(ref, axis, chip, size, layer, h):
    reg = _region(ref, axis, chip, size).at[pl.ds(layer, 1)]
    rows = reg.shape[1] // 2
    return reg.at[:, pl.ds(pl.multiple_of(h * rows, rows), rows), :]


def _gather_plan(axes, sizes, layer):
    def plan(src, land, x, y, c):
        me = 2 * x + y
        out = []
        for t in range(len(land)):
            mine = _half(land[t], axes[t], me, sizes[t], layer, c)
            for px, py in _other_chips(x, y):
                out.append((mine, mine, (px, py, c), _half(land[t], axes[t], 2 * px + py, sizes[t], layer, c)))
        return out
    return plan


def _forward_plan(axes, sizes, layer):
    def plan(src, land, x, y, c):
        out = []
        for t in range(len(land)):
            for px, py in _other_chips(x, y):
                got = _half(land[t], axes[t], 2 * px + py, sizes[t], layer, c)
                out.append((got, got, (x, y, 1 - c), _half(land[t], axes[t], 2 * px + py, sizes[t], layer, 1 - c)))
        return out
    return plan


def _place_own(name, w, axis, chip):
    nl, r, cs = w.shape
    tr = _rows_tile(r)
    nb = r // tr
    full = (nl, 4 * r, cs) if axis == 1 else (nl, r, 4 * cs)

    def body(m_ref, w_ref, o_ref):
        o_ref[...] = w_ref[...].astype(BF16)

    if axis == 1:
        ospec = pl.BlockSpec((None, tr, cs), lambda l, i, m: (l, m[0] * nb + i, 0))
    else:
        ospec = pl.BlockSpec((None, tr, cs), lambda l, i, m: (l, i, m[0]))
    return pl.pallas_call(
        body, name=name,
        grid_spec=pltpu.PrefetchScalarGridSpec(
            num_scalar_prefetch=1, grid=(nl, nb),
            in_specs=[pl.BlockSpec((None, tr, cs), lambda l, i, m: (l, i, 0))], out_specs=ospec),
        out_shape=jax.ShapeDtypeStruct(full, BF16), compiler_params=_params(("parallel", "parallel")),
    )(chip, w)


def _scatter_plan(axes, sizes):
    def plan(src, land, x, y, c):
        out = []
        for t in range(len(src)):
            for k, (px, py) in enumerate(_other_chips(x, y)):
                out.append((_region(src[t], axes[t], 2 * px + py, sizes[t]).at[0], land[t].at[k], (px, py, c),
                            land[t].at[k]))
        return out
    return plan


def _pair_plan(src, land, x, y, c):
    return [(src[t], land[t], (x, y, 1 - c), land[t]) for t in range(len(src))]


def _chip_sum(name, g, slots, axis, chip):
    _, r, cs = slots.shape
    tr = _rows_tile(r)
    nb = r // tr

    def body(m_ref, g_ref, s_ref, o_ref):
        acc = g_ref[...].astype(F32)
        for k in range(3):
            acc = acc + s_ref[k].astype(F32)
        o_ref[...] = acc.astype(BF16)

    if axis == 1:
        gspec = pl.BlockSpec((tr, cs), lambda i, m: (m[0] * nb + i, 0))
    else:
        gspec = pl.BlockSpec((tr, cs), lambda i, m: (i, m[0]))
    return pl.pallas_call(
        body, name=name,
        grid_spec=pltpu.PrefetchScalarGridSpec(
            num_scalar_prefetch=1, grid=(nb,),
            in_specs=[gspec, pl.BlockSpec((3, tr, cs), lambda i, m: (0, i, 0))],
            out_specs=pl.BlockSpec((tr, cs), lambda i, m: (i, 0))),
        out_shape=jax.ShapeDtypeStruct((r, cs), BF16), compiler_params=_params(("parallel",)),
    )(chip, g, slots)


def _pair_adamw(name, mine, theirs, w, m, v, layer, bufs):
    r, cs = mine.shape
    tr = _rows_tile(r)
    c1 = 1.0 - B1 ** STEP
    c2 = 1.0 - B2 ** STEP

    def body(a_ref, b_ref, w_ref, m_ref, v_ref, _g, _d, _m, _v, g_ref, d_ref, mo_ref, vo_ref):
        gv = a_ref[...].astype(F32) + b_ref[...].astype(F32)
        mn = B1 * m_ref[...] + (1.0 - B1) * gv
        vn = B2 * v_ref[...] + (1.0 - B2) * (gv * gv)
        g_ref[...] = gv
        mo_ref[...] = mn
        vo_ref[...] = vn
        d_ref[...] = -LR * ((mn / c1) / (jnp.sqrt(vn / c2) + ADAM_EPS) + WD * w_ref[...])

    blk = pl.BlockSpec((tr, cs), lambda i: (i, 0))
    lay = pl.BlockSpec((None, tr, cs), lambda i: (layer, i, 0))
    return pl.pallas_call(
        body, name=name, grid=(r // tr,), in_specs=[blk, blk, lay, lay, lay] + [ANY] * 4, out_specs=[lay] * 4,
        out_shape=[jax.ShapeDtypeStruct(b.shape, b.dtype) for b in bufs],
        input_output_aliases={5 + i: i for i in range(4)}, compiler_params=_params(("parallel",)),
    )(mine, theirs, w, m, v, *bufs)


def _allreduce_small(p):
    rows = p.shape[0]

    def body(p_ref, o_ref, sib, sums, send, recv):
        x, y, c = _place()
        me = 2 * x + y
        pair = pltpu.make_async_remote_copy(src_ref=p_ref, dst_ref=sib, send_sem=send.at[3], recv_sem=recv.at[3],
                                            device_id=(x, y, 1 - c), device_id_type=MESH)
        pair.start()
        pair.wait()
        sums[me] = (p_ref[...] + sib[...]).astype(BF16)
        cps = [pltpu.make_async_remote_copy(src_ref=sums.at[me], dst_ref=sums.at[me], send_sem=send.at[k],
                                            recv_sem=recv.at[k], device_id=(px, py, c), device_id_type=MESH)
               for k, (px, py) in enumerate(_other_chips(x, y))]
        for cp in cps:
            cp.start()
        for k, (px, py) in enumerate(_other_chips(x, y)):
            slot = sums.at[2 * px + py]
            pltpu.make_async_remote_copy(src_ref=slot, dst_ref=slot, send_sem=send.at[k], recv_sem=recv.at[k],
                                         device_id=(x, y, c), device_id_type=MESH).wait_recv()
        for cp in cps:
            cp.wait_send()
        acc = sums[0].astype(F32)
        for k in range(1, 4):
            acc = acc + sums[k].astype(F32)
        o_ref[...] = acc

    vm = pl.BlockSpec(memory_space=pltpu.VMEM)
    return pl.pallas_call(
        body, name="allreduce_small", in_specs=[vm], out_specs=vm, out_shape=jax.ShapeDtypeStruct(p.shape, F32),
        scratch_shapes=[pltpu.VMEM((rows, 128), F32), pltpu.VMEM((4, rows, 128), BF16),
                        pltpu.SemaphoreType.DMA((4,)), pltpu.SemaphoreType.DMA((4,))],
        compiler_params=pltpu.CompilerParams(vmem_limit_bytes=VMEM_LIMIT),
    )(p)


def _pack(parts):
    flat = []
    for p in parts:
        v = p.reshape(-1).astype(F32)
        flat.append(jnp.pad(v, (0, (-v.shape[0]) % 128)))
    v = jnp.concatenate(flat)
    v = jnp.pad(v, (0, (-v.shape[0]) % (512 * 128)))
    return v.reshape(-1, 128)


def _unpack(buf, like):
    out, r0 = [], 0
    for p in like:
        nelem = 1
        for s in p.shape:
            nelem *= s
        rows = -(-nelem // 128)
        blk = buf[r0:r0 + rows]
        if nelem % 128:
            blk = blk.reshape(-1)[:nelem]
        out.append(blk.reshape(p.shape))
        r0 += rows
    return out


def kernel(x, mem, positions, mem_norm_g, mix_pre_g, mix_post_g, w_in, gm_v_g, gm_w_s, gm_b_s, pool_w, pool_scale, attn_sinks, w_o, x_pre_g, x_post_g, w_xq, w_xkv, w_xo, ffn_pre_g, ffn_post_g, w_gate_up, w_down, loss_target, m_mem_norm_g, m_mix_pre_g, m_mix_post_g, m_w_in, m_gm_v_g, m_gm_w_s, m_gm_b_s, m_pool_w, m_pool_scale, m_attn_sinks, m_w_o, m_x_pre_g, m_x_post_g, m_w_xq, m_w_xkv, m_w_xo, m_ffn_pre_g, m_ffn_post_g, m_w_gate_up, m_w_down, v_mem_norm_g, v_mix_pre_g, v_mix_post_g, v_w_in, v_gm_v_g, v_gm_w_s, v_gm_b_s, v_pool_w, v_pool_scale, v_attn_sinks, v_w_o, v_x_pre_g, v_x_post_g, v_w_xq, v_w_xkv, v_w_xo, v_ffn_pre_g, v_ffn_post_g, v_w_gate_up, v_w_down):
    args = (x, mem, positions, mem_norm_g, mix_pre_g, mix_post_g, w_in, gm_v_g, gm_w_s, gm_b_s, pool_w, pool_scale, attn_sinks, w_o, x_pre_g, x_post_g, w_xq, w_xkv, w_xo, ffn_pre_g, ffn_post_g, w_gate_up, w_down)
    moms_m = (m_mem_norm_g, m_mix_pre_g, m_mix_post_g, m_w_in, m_gm_v_g, m_gm_w_s, m_gm_b_s, m_pool_w, m_pool_scale, m_attn_sinks, m_w_o, m_x_pre_g, m_x_post_g, m_w_xq, m_w_xkv, m_w_xo, m_ffn_pre_g, m_ffn_post_g, m_w_gate_up, m_w_down)
    moms_v = (v_mem_norm_g, v_mix_pre_g, v_mix_post_g, v_w_in, v_gm_v_g, v_gm_w_s, v_gm_b_s, v_pool_w, v_pool_scale, v_attn_sinks, v_w_o, v_x_pre_g, v_x_post_g, v_w_xq, v_w_xkv, v_w_xo, v_ffn_pre_g, v_ffn_post_g, v_w_gate_up, v_w_down)
    P = dict(zip(NAMES, args))
    P['loss_target'] = loss_target
    M = dict(zip(WEIGHTS, moms_m))
    V = dict(zip(WEIGHTS, moms_v))
    depth = w_in.shape[0]
    nbig = len(BIG)
    axes = [BIG_AXIS[n] for n in BIG]
    sizes = [P[n].shape[a] for n, a in zip(BIG, axes)]
    chip = (2 * lax.axis_index("x") + lax.axis_index("y")).astype(jnp.int32).reshape(1)

    groups = [['w_in'], ['w_o', 'w_xq', 'w_xkv', 'w_xo'], ['w_gate_up', 'w_down']]
    units = [(l, g) for l in range(depth) for g in groups]
    unit_of = {(l, n): i for i, (l, names) in enumerate(units) for n in names}
    ax = lambda names: [BIG_AXIS[n] for n in names]
    sz = lambda names: [P[n].shape[BIG_AXIS[n]] for n in names]

    full = {n: _place_own("place_" + n, P[n], BIG_AXIS[n], chip) for n in BIG}
    gathers = []
    prev, tok = None, None
    for i, (l, names) in enumerate(units):
        send, recv, _, land, tok = _split_start("gather_start%d" % i, [], [full[n] for n in names],
                                                3 * len(names), _gather_plan(ax(names), sz(names), l), after=prev)
        full.update(zip(names, land))
        gathers.append((send, recv))
        prev = land[0]
    P['first_dep'] = tok[:1, :1]
    forwards, gathered = {}, set()

    def forward_unit(i, after, carried=()):
        ul, unames = units[i]
        send, recv = gathers[i]
        send, recv, land, thru = _split_wait_start(
            "gather_pass%d" % i, send, recv, [full[n] for n in unames], after,
            _gather_plan(ax(unames), sz(unames), ul), 3 * len(unames), _forward_plan(ax(unames), sz(unames), ul),
            carried=[full[n] for n in carried])
        full.update(zip(unames, land))
        full.update(zip(carried, thru))
        forwards[i] = (send, recv)

    def weights_of(l, names, after):
        i = unit_of[(l, names[0])]
        if i not in gathered:
            _, unames = units[i]
            if i not in forwards:
                forward_unit(i, after)
            send, recv = forwards.pop(i)
            _, land = _split_wait("gather_wait%d" % i, send, recv, [], [full[n] for n in unames], after,
                                  _forward_plan(ax(unames), sz(unames), l))
            full.update(zip(unames, land))
            gathered.add(i)
            if len(groups) <= i + 1 < len(units):
                forward_unit(i + 1, after, carried=[n for n in unames if n not in units[i + 1][1]])
        return {n: (full[n], l) for n in names}

    outs = {n: [lax.empty(P[n].shape, F32) for _ in range(4)] for n in BIG}
    gunits = [(l, BIG) for l in range(depth - 1, 0, -1)] + [
        (0, g) for g in (['w_gate_up', 'w_down'], ['w_xq', 'w_xkv', 'w_xo'], ['w_o'], ['w_in'])]
    collected, scatters, pairs = {}, {}, {}

    def finish_scatter(i, after):
        _, names = gunits[i]
        send, recv, g_l, slots = scatters.pop(i)
        g_l, slots = _split_wait("scatter_wait%d" % i, send, recv, g_l, slots, after,
                                 _scatter_plan(ax(names), sz(names)))
        mine = [_chip_sum("chip_sum_" + n, g.reshape(g.shape[1:]), sl, BIG_AXIS[n], chip)
                for n, g, sl in zip(names, g_l, slots)]
        send, recv, mine, theirs, tok = _split_start("pair_start%d" % i, mine, [lax.empty(a.shape, BF16) for a in mine],
                                                     len(names), _pair_plan)
        pairs[i] = (send, recv, mine, theirs)
        return tok[:1, :1]

    def finish_pair(i, after):
        l, names = gunits[i]
        send, recv, mine, theirs = pairs.pop(i)
        mine, theirs = _split_wait("pair_wait%d" % i, send, recv, mine, theirs, after, _pair_plan)
        for n, a, b in zip(names, mine, theirs):
            outs[n] = _pair_adamw("adamw_" + n, a, b, P[n], M[n], V[n], l, outs[n])

    calls = {'n': 0}
    lag = 4

    def grads_of(l, g_part, after):
        collected.update({(l, n): g for n, g in g_part.items()})
        calls['n'] += 1
        now = calls['n']
        tok = jnp.zeros((1, 1), F32)
        for i, (ul, names) in enumerate(gunits):
            if ul != l or ('started', i) in collected or any((l, n) not in collected for n in names):
                continue
            collected[('started', i)] = now
            srcs = [collected[(l, n)].reshape((1,) + collected[(l, n)].shape) for n in names]
            send, recv, srcs, slots, t = _split_start("scatter_start%d" % i, srcs,
                                                      [lax.empty((3,) + P[n].shape[1:], BF16) for n in names],
                                                      3 * len(names), _scatter_plan(ax(names), sz(names)))
            scatters[i] = (send, recv, srcs, slots)
            tok = tok + t[:1, :1]
        for i in sorted(pairs):
            if collected[('summed', i)] + lag <= now:
                finish_pair(i, after)
        for i in sorted(scatters):
            if collected[('started', i)] + lag <= now:
                tok = tok + finish_scatter(i, after)
                collected[('summed', i)] = now
        return tok

    loss_part, dx, small_g = _fwd_bwd(P, weights_of, grads_of)
    loss = lax.psum(loss_part[0, 0], ("x", "y", "c"))
    grad_x = dx.reshape(x.shape)

    small_like = [P[n] for n in SMALL]
    gsum = _allreduce_small(_pack(small_g))
    dlt, mn, vn = _adamw("adamw_small", gsum, _pack(small_like), _pack([M[n] for n in SMALL]),
                         _pack([V[n] for n in SMALL]), 512)
    grads, deltas, new_m, new_v = {}, {}, {}, {}
    for name_map, buf in ((grads, gsum), (deltas, dlt), (new_m, mn), (new_v, vn)):
        for n, a in zip(SMALL, _unpack(buf, small_like)):
            name_map[n] = a

    for i in sorted(pairs):
        finish_pair(i, dlt)
    for i in sorted(scatters):
        finish_scatter(i, dlt)
    for i in sorted(pairs):
        finish_pair(i, dlt)
    for n in BIG:
        grads[n], deltas[n], new_m[n], new_v[n] = outs[n]

    return (loss, grad_x, *[grads[n] for n in WEIGHTS], *[deltas[n] for n in WEIGHTS],
            *[new_m[n] for n in WEIGHTS], *[new_v[n] for n in WEIGHTS])


def _fwd_bwd(P, weights_of, grads_of):
    (x, mem, positions, mem_norm_g, mix_pre_g, mix_post_g, w_in, gm_v_g, gm_w_s, gm_b_s, pool_w, pool_scale, attn_sinks,
     w_o, x_pre_g, x_post_g, w_xq, w_xkv, w_xo, ffn_pre_g, ffn_post_g, w_gate_up, w_down) = [P[n] for n in NAMES]
    x0 = x[0]
    s, d = x0.shape
    depth = w_in.shape[0]
    tgt = P['loss_target'][0]
    tmn = 256
    tmr = min(512, s)
    tkw = min(2048, s)

    half = HEAD // 2
    inv = ROPE_THETA ** (-jnp.arange(half, dtype=F32) / half)
    ang = positions[0].astype(F32)[:, None] * inv
    cos, sin = jnp.cos(ang), jnp.sin(ang)
    cosq = jnp.tile(jnp.concatenate([cos, cos], axis=-1), (1, 2))
    sinq = jnp.tile(jnp.concatenate([-sin, sin], axis=-1), (1, 2))

    row = lambda a, l: a[l].reshape(1, -1)
    memn = _prenorm("mem_norm", mem[0], mem_norm_g.reshape(1, d), tmn)
    pw_bd = []
    for l in range(depth):
        bd = jnp.zeros((256, 256), F32)
        for g in range(4):
            bd = lax.dynamic_update_slice(bd, pool_w[l, g], (64 * g, 64 * g))
        pw_bd.append(bd)

    saved = []
    xc = x0
    h = _prenorm("pre_norm0", x0, row(mix_pre_g, 0) + P['first_dep'], tmn)
    for l in range(depth):
        W = weights_of(l, ['w_in'], xc)
        sv = {'x0': xc, 'h1': h}
        z = _mm_nn("fwd_w_in", h, *W['w_in'], tm=1024, tn=512, tk=d, out_dtype=F32)
        abc = _mixer_fwd("mixer_fwd", z, cosq, sinq, row(gm_v_g, l), gm_w_s[l], gm_b_s[l].T, pw_bd[l],
                         row(pool_scale, l), row(attn_sinks, l))
        W.update(weights_of(l, ['w_o'], z))
        mix, xc, h = _mm_rows("fwd_w_o", abc, *W['w_o'], 'nn', tm=tmr, tk=d, rows_in=[xc],
                              params=[row(mix_post_g, l), row(x_pre_g, l)], rows_out=[BF16, F32, BF16],
                              epilogue=_post_pre_rows)
        sv.update(z=z, abc=abc, mix=mix, x1=xc, h2=h)
        W.update(weights_of(l, ['w_xq', 'w_xkv', 'w_xo'], xc))
        q = _mm_nn("fwd_w_xq", h, *W['w_xq'], tm=1024, tn=512, tk=d, out_dtype=BF16)
        kv = _mm_nn("fwd_w_xkv", memn, *W['w_xkv'], tm=256, tn=512, tk=d, out_dtype=BF16)
        o = _xattn_fwd("xattn_fwd", q, kv, 512)
        xo, xc, h = _mm_rows("fwd_w_xo", o, *W['w_xo'], 'nn', tm=tmr, tk=d, rows_in=[xc],
                             params=[row(x_post_g, l), row(ffn_pre_g, l)], rows_out=[BF16, F32, BF16],
                             epilogue=_post_pre_rows)
        sv.update(q=q, kv=kv, o=o, xo=xo, x2=xc, h3=h)
        W.update(weights_of(l, ['w_gate_up', 'w_down'], xc))
        dff = W['w_down'][0].shape[1]
        gate, up, act = _ffn_up("ffn_up", h, *W['w_gate_up'], 512, dff // 2)
        sv.update(gate=gate, up=up, act=act)
        if l + 1 < depth:
            f, xc, h = _mm_rows("fwd_w_down", act, *W['w_down'], 'nn', tm=tmr, tk=dff, rows_in=[xc],
                                params=[row(ffn_post_g, l), row(mix_pre_g, l + 1)], rows_out=[BF16, F32, BF16],
                                epilogue=_post_pre_rows)
            sv.update(f=f)
        saved.append(sv)
    gs = {n: [None] * depth for n in SMALL if n != 'mem_norm_g'}
    dx, dfn, gs['ffn_post_g'][depth - 1], loss_part = _mm_rows(
        "fwd_w_down_loss", saved[-1]['act'], *W['w_down'], 'nn', tm=tmr, tk=dff, rows_in=[xc, tgt],
        params=[row(ffn_post_g, depth - 1)], rows_out=[F32, BF16], n_sums=2, epilogue=_make_loss_rows(d))

    dmemn = None
    tok = jnp.zeros((1, 1), F32)
    for l in reversed(range(depth)):
        sv, W, G = saved[l], weights_of(l, BIG, dx), {}
        G['w_down'] = _mm_tn("dw_down", sv['act'], dfn, tm=dff // 2, tn=d, tk=tkw)
        dgu = _ffn_act_bwd("ffn_act_bwd", dfn, *W['w_down'], sv['gate'], sv['up'], 256)
        G['w_gate_up'] = _mm_tn("dw_gate_up", sv['h3'], dgu, tm=d, tn=dff // 2, tk=tkw)
        dx, dxo, gs['ffn_pre_g'][l], gs['x_post_g'][l] = _mm_rows(
            "bwd_w_gate_up", dgu, *W['w_gate_up'], 'nt', tm=tmr, tk=2 * dff, rows_in=[sv['x2'], dx, sv['xo']],
            params=[row(ffn_pre_g, l) + tok, row(x_post_g, l)], rows_out=[F32, BF16], n_sums=2, epilogue=_bwd_rows)
        tok = grads_of(l, {n: G[n] for n in ('w_gate_up', 'w_down')}, dx)
        G['w_xo'] = _mm_tn("dw_xo", sv['o'], dxo, tm=d, tn=d, tk=tkw)
        do = _mm_nt("bwd_w_xo", dxo, *W['w_xo'], tm=1024, tn=512, tk=d, out_dtype=BF16, dep=tok)
        dq, dkv = _xattn_bwd("xattn_bwd", sv['q'], sv['kv'], do, 512)
        dkv = dkv.astype(BF16)
        G['w_xkv'] = _mm_tn("dw_xkv", memn, dkv, tm=d, tn=d, tk=mem.shape[1])
        dmemn = _mm_nt("bwd_w_xkv", dkv, *W['w_xkv'], tm=mem.shape[1], tn=512, tk=2 * d, out_dtype=F32, add=dmemn)
        G['w_xq'] = _mm_tn("dw_xq", sv['h2'], dq, tm=d, tn=d, tk=tkw)
        dx, dmix, gs['x_pre_g'][l], gs['mix_post_g'][l] = _mm_rows(
            "bwd_w_xq", dq, *W['w_xq'], 'nt', tm=tmr, tk=d, rows_in=[sv['x1'], dx, sv['mix']],
            params=[row(x_pre_g, l), row(mix_post_g, l)], rows_out=[F32, BF16], n_sums=2, epilogue=_bwd_rows)
        tok = grads_of(l, {n: G[n] for n in ('w_xq', 'w_xkv', 'w_xo')}, dx)
        G['w_o'] = _mm_tn("dw_o", sv['abc'], dmix, tm=d, tn=d, tk=tkw)
        dabc = _mm_nt("bwd_w_o", dmix, *W['w_o'], tm=1024, tn=512, tk=d, out_dtype=F32, dep=tok)
        tok = grads_of(l, {'w_o': G['w_o']}, dabc)
        dz, dgv, dws, dbt, dpw, dpsc, dsnk = _mixer_bwd(
            "mixer_bwd", sv['z'], dabc, cosq, sinq, row(gm_v_g, l) + tok, gm_w_s[l], gm_b_s[l].T, pw_bd[l],
            row(pool_scale, l), row(attn_sinks, l))
        gs['gm_v_g'][l] = dgv
        gs['gm_w_s'][l] = dws
        gs['gm_b_s'][l] = dbt.T
        gs['pool_w'][l] = jnp.stack([dpw[64 * g:64 * (g + 1), 64 * g:64 * (g + 1)] for g in range(4)])
        gs['pool_scale'][l] = dpsc
        gs['attn_sinks'][l] = dsnk
        G['w_in'] = _mm_tn("dw_in", sv['h1'], dz, tm=d, tn=dz.shape[1], tk=tkw)
        if l > 0:
            dx, dfn, gs['mix_pre_g'][l], gs['ffn_post_g'][l - 1] = _mm_rows(
                "bwd_w_in", dz, *W['w_in'], 'nt', tm=tmr, tk=dz.shape[1], rows_in=[sv['x0'], dx, saved[l - 1]['f']],
                params=[row(mix_pre_g, l), row(ffn_post_g, l - 1)], rows_out=[F32, BF16], n_sums=2,
                epilogue=_bwd_rows)
        else:
            dx, gs['mix_pre_g'][l] = _mm_rows(
                "bwd_w_in_first", dz, *W['w_in'], 'nt', tm=tmr, tk=dz.shape[1], rows_in=[sv['x0'], dx],
                params=[row(mix_pre_g, l)], rows_out=[F32], n_sums=1, epilogue=_bwd_rows_first)
        tok = grads_of(l, {'w_in': G['w_in']}, dx)
    _, dg_mem = _norm_bwd("bwd_mem_norm", mem[0], mem_norm_g.reshape(1, d), dmemn, None, BF16, tmn)
    small_g = []
    for n in SMALL:
        if n == 'mem_norm_g':
            small_g.append(dg_mem.reshape(P[n].shape))
        else:
            small_g.append(jnp.stack([a.reshape(P[n].shape[1:]) for a in gs[n]]))
    return loss_part, dx, small_g
```

```python
import functools

import jax
import jax.numpy as jnp
from jax import lax
from jax.experimental import pallas as pl
from jax.experimental.pallas import tpu as pltpu

F32 = jnp.float32
BF16 = jnp.bfloat16
EPS = 1e-6
CHUNK = 128
HEAD = 64
ROPE_THETA = 10000.0
POOL_WINDOWS = (2, 4, 8, 16)
LR, B1, B2, ADAM_EPS, WD, STEP = 0.001, 0.9, 0.999, 1e-08, 0.01, 10
MESH = pl.DeviceIdType.MESH
VMEM_LIMIT = 56 * 1024 * 1024

NAMES = ['x', 'mem', 'positions', 'mem_norm_g', 'mix_pre_g', 'mix_post_g', 'w_in', 'gm_v_g', 'gm_w_s', 'gm_b_s',
         'pool_w', 'pool_scale', 'attn_sinks', 'w_o', 'x_pre_g', 'x_post_g', 'w_xq', 'w_xkv', 'w_xo', 'ffn_pre_g',
         'ffn_post_g', 'w_gate_up', 'w_down']
WEIGHTS = NAMES[3:]
BIG = ['w_in', 'w_o', 'w_xq', 'w_xkv', 'w_xo', 'w_gate_up', 'w_down']
BIG_AXIS = {'w_in': 2, 'w_o': 1, 'w_xq': 1, 'w_xkv': 2, 'w_xo': 1, 'w_gate_up': 2, 'w_down': 1}
SMALL = [n for n in WEIGHTS if n not in BIG]

NN = (((1,), (0,)), ((), ()))
NT = (((1,), (1,)), ((), ()))
TN = (((0,), (0,)), ((), ()))


def _dot(a, b, dims=NN):
    return lax.dot_general(a, b, dims, preferred_element_type=F32)


def _params(sem):
    return pltpu.CompilerParams(dimension_semantics=sem, vmem_limit_bytes=VMEM_LIMIT)


def _rows_tile(rows, limit=256):
    return max(t for t in range(16, limit + 1, 16) if rows % t == 0)


def _mm(name, a, a_spec, b, b_spec, dims, grid, nk, out_shape, out_spec, add=None, add_spec=None):
    acc_shape = out_spec.block_shape

    def body(*refs):
        a_ref, b_ref = refs[0], refs[1]
        pos = 2
        add_ref = None
        if add is not None:
            add_ref = refs[pos]
            pos += 1
        o_ref = refs[pos]
        part = _dot(a_ref[...].astype(BF16), b_ref[...].astype(BF16), dims)
        if nk == 1:
            if add_ref is not None:
                part = part + add_ref[...]
            o_ref[...] = part.astype(o_ref.dtype)
        else:
            acc_ref = refs[pos + 1]
            k = pl.program_id(2)

            @pl.when(k == 0)
            def _():
                acc_ref[...] = part if add_ref is None else part + add_ref[...]

            @pl.when(k > 0)
            def _():
                acc_ref[...] += part

            @pl.when(k == nk - 1)
            def _():
                o_ref[...] = acc_ref[...].astype(o_ref.dtype)

    ops, specs = [a, b], [a_spec, b_spec]
    if add is not None:
        ops.append(add)
        specs.append(add_spec)
    return pl.pallas_call(
        body, name=name, grid=grid, in_specs=specs, out_specs=out_spec, out_shape=out_shape,
        scratch_shapes=[pltpu.VMEM(acc_shape, F32)] if nk > 1 else [],
        compiler_params=_params(("parallel", "parallel", "arbitrary")),
    )(*ops)


def _wspec(block, layer, fn):
    return pl.BlockSpec((None,) + block, lambda i, j, k: (layer,) + fn(i, j, k))


def _mm_nn(name, a, w, layer, *, tm, tn, tk, out_dtype):
    m, kk = a.shape
    n = w.shape[2]
    tm = min(tm, m)
    nk = kk // tk
    return _mm(name, a, pl.BlockSpec((tm, tk), lambda i, j, k: (i, k)),
               w, _wspec((tk, tn), layer, lambda i, j, k: (k, j)), NN,
               (m // tm, n // tn, nk), nk, jax.ShapeDtypeStruct((m, n), out_dtype),
               pl.BlockSpec((tm, tn), lambda i, j, k: (i, j)))


def _mm_nt(name, a, w, layer, *, tm, tn, tk, out_dtype, add=None):
    m, kk = a.shape
    n = w.shape[1]
    tm = min(tm, m)
    nk = kk // tk
    ospec = pl.BlockSpec((tm, tn), lambda i, j, k: (i, j))
    return _mm(name, a, pl.BlockSpec((tm, tk), lambda i, j, k: (i, k)),
               w, _wspec((tn, tk), layer, lambda i, j, k: (j, k)), NT,
               (m // tm, n // tn, nk), nk, jax.ShapeDtypeStruct((m, n), out_dtype), ospec,
               add=add, add_spec=ospec if add is not None else None)


def _mm_tn(name, a, b, *, tm, tn, tk):
    kk, m = a.shape
    n = b.shape[1]
    tk = min(tk, kk)
    nk = kk // tk
    return _mm(name, a, pl.BlockSpec((tk, tm), lambda i, j, k: (k, i)),
               b, pl.BlockSpec((tk, tn), lambda i, j, k: (k, j)), TN,
               (m // tm, n // tn, nk), nk, jax.ShapeDtypeStruct((m, n), BF16),
               pl.BlockSpec((tm, tn), lambda i, j, k: (i, j)))


def _mm_rows(name, a, w, layer, mode, *, tm, rows_in=(), params=(), rows_out=(), n_sums=0, epilogue):
    m, kk = a.shape
    n = w.shape[2] if mode == 'nn' else w.shape[1]
    nr, npar, no = len(rows_in), len(params), len(rows_out)

    def body(*refs):
        a_ref, w_ref = refs[0], refs[1]
        rin = refs[2:2 + nr]
        par = refs[2 + nr:2 + nr + npar]
        outs = refs[2 + nr + npar:2 + nr + npar + no]
        sums = refs[2 + nr + npar + no:2 + nr + npar + no + n_sums]
        acc = _dot(a_ref[...], w_ref[...], NN if mode == 'nn' else NT)
        res, sm = epilogue(acc, [r[...] for r in rin], [p[...] for p in par])
        for r, v in zip(outs, res):
            r[...] = v.astype(r.dtype)

        @pl.when(pl.program_id(0) == 0)
        def _():
            for r in sums:
                r[...] = jnp.zeros_like(r)

        for r, v in zip(sums, sm):
            r[...] += v

    wblock = (None, kk, n) if mode == 'nn' else (None, n, kk)
    rowblk = pl.BlockSpec((tm, n), lambda i: (i, 0))
    one = pl.BlockSpec((1, n), lambda i: (0, 0))
    return pl.pallas_call(
        body, name=name, grid=(m // tm,),
        in_specs=[pl.BlockSpec((tm, kk), lambda i: (i, 0)),
                  pl.BlockSpec(wblock, lambda i: (layer, 0, 0), pipeline_mode=pl.Buffered(1))]
                 + [rowblk] * nr + [one] * npar,
        out_specs=[rowblk] * no + [one] * n_sums,
        out_shape=[jax.ShapeDtypeStruct((m, n), dt) for dt in rows_out] +
                  [jax.ShapeDtypeStruct((1, n), F32)] * n_sums,
        compiler_params=_params(("arbitrary",)),
    )(a, w, *rows_in, *params)


def _rstd(x):
    return lax.rsqrt(jnp.mean(x * x, axis=-1, keepdims=True) + EPS)


def _norm_back(xin, g, dy):
    r = _rstd(xin)
    xh = xin * r
    dyg = dy * g
    return r * (dyg - xh * jnp.mean(dyg * xh, axis=-1, keepdims=True)), jnp.sum(dy * xh, axis=0, keepdims=True)


def _plain_rows(acc, rows, pars):
    return [acc], []


def _post_pre_rows(y, rows, pars):
    xn = rows[0] + y * _rstd(y) * pars[0]
    return [y, xn, xn * _rstd(xn) * pars[1]], []


def _make_loss_rows(d):
    def fn(y, rows, pars):
        x, tgt = rows
        err = x + y * _rstd(y) * pars[0] - tgt
        dout = err * (1.0 / d)
        dy, dg = _norm_back(y, pars[0], dout)
        lsum = 0.5 * jnp.sum(jnp.mean(err * err, axis=-1, keepdims=True), axis=0, keepdims=True)
        return [dout, dy], [dg, jnp.broadcast_to(lsum, dg.shape)]
    return fn


def _bwd_rows(dh, rows, pars):
    xin, resid, yprev = rows
    dxa, dg_pre = _norm_back(xin, pars[0], dh)
    dx = resid + dxa
    dyp, dg_post = _norm_back(yprev.astype(F32), pars[1], dx)
    return [dx, dyp], [dg_pre, dg_post]


def _bwd_rows_first(dh, rows, pars):
    xin, resid = rows
    dxa, dg_pre = _norm_back(xin, pars[0], dh)
    return [resid + dxa], [dg_pre]


def _row(d):
    return pl.BlockSpec((1, d), lambda i: (0, 0))


def _prenorm(name, x, g, tm):
    m, d = x.shape

    def body(x_ref, g_ref, o_ref):
        xv = x_ref[...]
        o_ref[...] = (xv * _rstd(xv) * g_ref[...]).astype(BF16)

    blk = pl.BlockSpec((tm, d), lambda i: (i, 0))
    return pl.pallas_call(body, name=name, grid=(m // tm,), in_specs=[blk, _row(d)], out_specs=blk,
                          out_shape=jax.ShapeDtypeStruct((m, d), BF16), compiler_params=_params(("parallel",)))(x, g)


def _norm_bwd(name, xin, g, dy, resid, out_dtype, tm):
    m, d = xin.shape

    def body(*refs):
        if resid is None:
            x_ref, g_ref, dy_ref, dx_ref, dg_ref = refs
        else:
            x_ref, g_ref, dy_ref, r_ref, dx_ref, dg_ref = refs
        xv = x_ref[...]
        r = _rstd(xv)
        xh = xv * r
        dyv = dy_ref[...].astype(F32)
        dyg = dyv * g_ref[...]
        dx = r * (dyg - xh * jnp.mean(dyg * xh, axis=-1, keepdims=True))
        if resid is not None:
            dx = dx + r_ref[...]
        dx_ref[...] = dx.astype(dx_ref.dtype)

        @pl.when(pl.program_id(0) == 0)
        def _():
            dg_ref[...] = jnp.zeros_like(dg_ref)

        dg_ref[...] += jnp.sum(dyv * xh, axis=0, keepdims=True)

    blk = pl.BlockSpec((tm, d), lambda i: (i, 0))
    ops = [xin, g, dy] + ([] if resid is None else [resid])
    specs = [blk, _row(d), blk] + ([] if resid is None else [blk])
    return pl.pallas_call(
        body, name=name, grid=(m // tm,), in_specs=specs, out_specs=[blk, _row(d)],
        out_shape=[jax.ShapeDtypeStruct((m, d), out_dtype), jax.ShapeDtypeStruct((1, d), F32)],
        compiler_params=_params(("arbitrary",)))(*ops)


def _adamw(name, g, w, m, v, tr):
    rows, cols = g.shape
    c1 = 1.0 - B1 ** STEP
    c2 = 1.0 - B2 ** STEP

    def body(g_ref, w_ref, m_ref, v_ref, d_ref, mo_ref, vo_ref):
        gv = g_ref[...]
        mn = B1 * m_ref[...] + (1.0 - B1) * gv
        vn = B2 * v_ref[...] + (1.0 - B2) * (gv * gv)
        mo_ref[...] = mn
        vo_ref[...] = vn
        d_ref[...] = -LR * ((mn / c1) / (jnp.sqrt(vn / c2) + ADAM_EPS) + WD * w_ref[...])

    blk = pl.BlockSpec((tr, cols), lambda i: (i, 0))
    sd = jax.ShapeDtypeStruct((rows, cols), F32)
    return pl.pallas_call(body, name=name, grid=(rows // tr,), in_specs=[blk] * 4, out_specs=[blk] * 3,
                          out_shape=[sd, sd, sd], compiler_params=_params(("parallel",)))(g, w, m, v)


def _gelu_parts(x):
    c = 0.7978845608028654
    t = jnp.tanh(c * (x + 0.044715 * (x * x * x)))
    return 0.5 * x * (1.0 + t), t


def _gelu_grad(x, t):
    c = 0.7978845608028654
    return 0.5 * (1.0 + t) + 0.5 * x * (1.0 - t * t) * (c * (1.0 + 3.0 * 0.044715 * x * x))


def _rot_half(x):
    ax = x.ndim - 1
    w = x.shape[ax]
    lane = lax.broadcasted_iota(jnp.int32, x.shape, ax)
    return jnp.where((lane & 63) < 32, pltpu.roll(x, w - 32, ax), pltpu.roll(x, 32, ax))


def _group_mean(x, ones_bd):
    hi = x.astype(BF16)
    lo = (x - hi.astype(F32)).astype(BF16)
    return (_dot(hi, ones_bd) + _dot(lo, ones_bd)) * (1.0 / HEAD)


def _gating(gel, gv, ws_ref, bt, ones_bd, mix_s):
    u = gel[:, :256]
    v = gel[:, 256:]
    r = lax.rsqrt(_group_mean(v * v, ones_bd) + EPS)
    xh = v * r
    vn = (xh * gv).astype(BF16)
    row = lax.broadcasted_iota(jnp.int32, (CHUNK, CHUNK), 0)
    col = lax.broadcasted_iota(jnp.int32, (CHUNK, CHUNK), 1)
    causal = col <= row
    wcs = []
    for g in range(4):
        wc = jnp.where(causal, ws_ref[g], 0.0).astype(BF16)
        wcs.append(wc)
        mix_s[:, HEAD * g:HEAD * (g + 1)] = _dot(wc, vn[:, HEAD * g:HEAD * (g + 1)]) + bt[:, g:g + 1]
    return u, r, xh, vn, wcs, causal, mix_s[...]


def _lane_select(lane, vals):
    return jnp.where(lane < 64, vals[0], jnp.where(lane < 128, vals[1], jnp.where(lane < 192, vals[2], vals[3])))


def _pool_fwd(pc, pp, ci):
    ext = jnp.concatenate([pp, pc], axis=0)
    s2 = ext + pltpu.roll(ext, 1, 0)
    s4 = s2 + pltpu.roll(s2, 2, 0)
    s8 = s4 + pltpu.roll(s4, 4, 0)
    s16 = s8 + pltpu.roll(s8, 8, 0)
    t1 = ci * CHUNK + lax.broadcasted_iota(jnp.int32, (CHUNK, 1), 0) + 1
    lane = lax.broadcasted_iota(jnp.int32, (1, 256), 1)
    cnt = _lane_select(lane, [jnp.minimum(t1, w).astype(F32) for w in POOL_WINDOWS])
    ssel = _lane_select(lane, [s[CHUNK:] for s in (s2, s4, s8, s16)])
    return ssel / cnt - pc, cnt, lane


def _attn_prep(zc, zpkv, cc, sc, cp, sp, ci):
    q = zc[:, 768:1280]
    kc = zc[:, 1280:1408]
    vc = zc[:, 1408:1536]
    kp = zpkv[:, :128]
    vp = zpkv[:, 128:]
    qr = q * jnp.concatenate([cc] * 4, axis=1) + _rot_half(q) * jnp.concatenate([sc] * 4, axis=1)
    krc = kc * cc + _rot_half(kc) * sc
    krp = kp * cp + _rot_half(kp) * sp
    kband = jnp.concatenate([krp, krc], axis=0)
    vband = jnp.concatenate([vp, vc], axis=0)
    key = lax.broadcasted_iota(jnp.int32, (2 * CHUNK, 4 * CHUNK), 0)
    t = lax.broadcasted_iota(jnp.int32, (2 * CHUNK, 4 * CHUNK), 1) & (CHUNK - 1)
    valid = ((key < CHUNK) & (key > t) & (ci > 0)) | ((key >= CHUNK) & (key - CHUNK <= t))
    return qr, kband, vband, valid


SCALE = HEAD ** -0.5


def _stack_heads(x, base, hk):
    return jnp.concatenate([x[:, base + HEAD * (4 * hk + i):base + HEAD * (4 * hk + i + 1)] for i in range(4)], axis=0)


def _sink_row(snk, hk):
    lane = lax.broadcasted_iota(jnp.int32, (1, 4 * CHUNK), 1)
    s = [snk[:, 4 * hk + i:4 * hk + i + 1] for i in range(4)]
    return jnp.where(lane < CHUNK, s[0], jnp.where(lane < 2 * CHUNK, s[1], jnp.where(lane < 3 * CHUNK, s[2], s[3])))


def _group_probs(kh, q4, valid, sink4):
    s = jnp.where(valid, _dot(kh, q4, NT), -1e30)
    mx = jnp.maximum(jnp.max(s, axis=0, keepdims=True), sink4)
    e = jnp.exp(s - mx)
    es = jnp.exp(sink4 - mx)
    inv = 1.0 / (jnp.sum(e, axis=0, keepdims=True) + es)
    return e * inv, es * inv


def _mixer_specs(nb, rev):
    def cur(i):
        return nb - 1 - i if rev else i

    def prev(i):
        return jnp.maximum(cur(i) - 1, 0)

    full = lambda shape: pl.BlockSpec(shape, lambda i: (0,) * len(shape))
    specs = [
        pl.BlockSpec((CHUNK, 1536), lambda i: (cur(i), 0)),
        pl.BlockSpec((CHUNK, 256), lambda i: (prev(i), 2)),
        pl.BlockSpec((CHUNK, 256), lambda i: (prev(i), 5)),
        pl.BlockSpec((CHUNK, 128), lambda i: (cur(i), 0)),
        pl.BlockSpec((CHUNK, 128), lambda i: (cur(i), 0)),
        pl.BlockSpec((CHUNK, 128), lambda i: (prev(i), 0)),
        pl.BlockSpec((CHUNK, 128), lambda i: (prev(i), 0)),
        full((1, 256)), full((4, CHUNK, CHUNK)), full((CHUNK, 4)), full((256, 256)), full((1, 256)), full((1, 8)),
        full((256, 256)),
    ]
    return specs, cur


def _ones_bd():
    g = lax.broadcasted_iota(jnp.int32, (256, 256), 0) // HEAD == lax.broadcasted_iota(jnp.int32, (256, 256), 1) // HEAD
    return g.astype(BF16)


def _mixer_fwd(name, z, cosq, sinq, gv, ws, bt, pw, psc, snk):
    s = z.shape[0]
    nb = s // CHUNK
    specs, _ = _mixer_specs(nb, False)

    def body(zc_ref, zpp_ref, zpkv_ref, cq_ref, sq_ref, cp_ref, sp_ref, gv_ref, ws_ref, bt_ref, pw_ref, psc_ref,
             snk_ref, bd_ref, o_ref, mix_s):
        ci = pl.program_id(0)
        zc = zc_ref[...]
        gel, _ = _gelu_parts(zc[:, :512])
        u, _, _, _, _, _, mixed = _gating(gel, gv_ref[...], ws_ref, bt_ref[...], bd_ref[...], mix_s)
        o_ref[:, :256] = (u * mixed).astype(BF16)
        pp = jnp.where(ci > 0, zpp_ref[...], 0.0)
        pooled, _, _ = _pool_fwd(zc[:, 512:768], pp, ci)
        mp = _dot(pooled.astype(BF16), pw_ref[...].astype(BF16))
        o_ref[:, 256:512] = (mp * psc_ref[...]).astype(BF16)
        qr, kband, vband, valid = _attn_prep(zc, zpkv_ref[...], cq_ref[...], sq_ref[...], cp_ref[...], sp_ref[...], ci)
        snkv = snk_ref[...]
        kb = kband.astype(BF16)
        vt = vband.T
        ots = []
        for hk in range(2):
            q4 = (_stack_heads(qr, 0, hk) * SCALE).astype(BF16)
            p, _ = _group_probs(kb[:, HEAD * hk:HEAD * (hk + 1)], q4, valid, _sink_row(snkv, hk))
            ots.append(_dot(vt[HEAD * hk:HEAD * (hk + 1), :].astype(BF16), p.astype(BF16)))
        o = jnp.concatenate(ots, axis=0).T
        for hk in range(2):
            for i in range(4):
                h = 4 * hk + i
                o_ref[:, 512 + HEAD * h:512 + HEAD * (h + 1)] = o[CHUNK * i:CHUNK * (i + 1),
                                                                  HEAD * hk:HEAD * (hk + 1)].astype(BF16)

    return pl.pallas_call(
        body, name=name, grid=(nb,), in_specs=specs, out_specs=pl.BlockSpec((CHUNK, 1024), lambda i: (i, 0)),
        out_shape=jax.ShapeDtypeStruct((s, 1024), BF16), scratch_shapes=[pltpu.VMEM((CHUNK, 256), F32)],
        compiler_params=_params(("parallel",)),
    )(z, z, z, cosq, sinq, cosq, sinq, gv, ws, bt, pw, psc, snk, _ones_bd())


def _mixer_bwd(name, z, dabc, cosq, sinq, gv, ws, bt, pw, psc, snk):
    s = z.shape[0]
    nb = s // CHUNK
    specs, cur = _mixer_specs(nb, True)
    specs = specs + [pl.BlockSpec((CHUNK, 1024), lambda i: (cur(i), 0))]
    full = lambda shape: pl.BlockSpec(shape, lambda i: (0,) * len(shape))
    acc_shapes = [(1, 256), (4, CHUNK, CHUNK), (CHUNK, 4), (256, 256), (1, 256), (1, 8)]

    def body(zc_ref, zpp_ref, zpkv_ref, cq_ref, sq_ref, cp_ref, sp_ref, gv_ref, ws_ref, bt_ref, pw_ref, psc_ref,
             snk_ref, bd_ref, dabc_ref, dz_ref, dgv_ref, dws_ref, dbt_ref, dpw_ref, dpsc_ref, dsnk_ref,
             cpool, ck, cv, dq_s, dkv_s, mix_s, dvn_s):
        step = pl.program_id(0)
        ci = nb - 1 - step

        @pl.when(step == 0)
        def _():
            for r in (dgv_ref, dws_ref, dbt_ref, dpw_ref, dpsc_ref, dsnk_ref, cpool, ck, cv):
                r[...] = jnp.zeros_like(r)

        zc = zc_ref[...]
        dabc = dabc_ref[...]
        zg = zc[:, :512]
        gel, th = _gelu_parts(zg)
        gp = _gelu_grad(zg, th)
        gvv = gv_ref[...]
        bd = bd_ref[...]
        u, r, xh, vn, wcs, causal, mixed = _gating(gel, gvv, ws_ref, bt_ref[...], bd, mix_s)
        da = dabc[:, :256]
        dm = da * u
        dmb = dm.astype(BF16)
        lane4 = lax.broadcasted_iota(jnp.int32, (CHUNK, 4), 1)
        dbt = jnp.zeros((CHUNK, 4), F32)
        for g in range(4):
            lo, hi = HEAD * g, HEAD * (g + 1)
            dws_ref[g] += jnp.where(causal, _dot(dmb[:, lo:hi], vn[:, lo:hi], NT), 0.0)
            dbt = dbt + jnp.where(lane4 == g, jnp.sum(dm[:, lo:hi], axis=-1, keepdims=True), 0.0)
            dvn_s[:, lo:hi] = _dot(wcs[g], dmb[:, lo:hi], TN)
        dbt_ref[...] += dbt
        dvn = dvn_s[...]
        dgv_ref[...] += jnp.sum(dvn * xh, axis=0, keepdims=True)
        dxh = dvn * gvv
        dvg = r * (dxh - xh * _group_mean(dxh * xh, bd))
        dz_ref[:, :256] = (da * mixed * gp[:, :256]).astype(BF16)
        dz_ref[:, 256:512] = (dvg * gp[:, 256:]).astype(BF16)
        pc = zc[:, 512:768]
        pp = jnp.where(ci > 0, zpp_ref[...], 0.0)
        pooled, cnt, lane = _pool_fwd(pc, pp, ci)
        pwb = pw_ref[...].astype(BF16)
        pooled_b = pooled.astype(BF16)
        mp = _dot(pooled_b, pwb)
        db = dabc[:, 256:512]
        dpsc_ref[...] += jnp.sum(db * mp, axis=0, keepdims=True)
        dmpb = (db * psc_ref[...]).astype(BF16)
        dpw_ref[...] += _dot(pooled_b, dmpb, TN)
        dpooled = _dot(dmpb, pwb, NT)
        davg = dpooled / cnt
        zero = jnp.zeros((CHUNK, 256), F32)
        d2, d4, d8, d16 = [jnp.concatenate([zero, jnp.where((lane >= 64 * k) & (lane < 64 * (k + 1)), davg, 0.0)],
                                           axis=0) for k in range(4)]
        g8 = d8 + d16 + pltpu.roll(d16, 2 * CHUNK - 8, 0)
        g4 = d4 + g8 + pltpu.roll(g8, 2 * CHUNK - 4, 0)
        g2 = d2 + g4 + pltpu.roll(g4, 2 * CHUNK - 2, 0)
        ge = g2 + pltpu.roll(g2, 2 * CHUNK - 1, 0)
        dz_ref[:, 512:768] = (ge[CHUNK:] - dpooled + cpool[...]).astype(BF16)
        cpool[...] = ge[:CHUNK]
        cc = cq_ref[...]
        sc = sq_ref[...]
        qr, kband, vband, valid = _attn_prep(zc, zpkv_ref[...], cc, sc, cp_ref[...], sp_ref[...], ci)
        snkv = snk_ref[...]
        lane8 = lax.broadcasted_iota(jnp.int32, (1, 8), 1)
        qlane = lax.broadcasted_iota(jnp.int32, (1, 4 * CHUNK), 1)
        dsnk = jnp.zeros((1, 8), F32)
        kb = kband.astype(BF16)
        vb = vband.astype(BF16)
        kt = kband.T * SCALE
        dqts = []
        for hk in range(2):
            kh = kb[:, HEAD * hk:HEAD * (hk + 1)]
            q4 = (_stack_heads(qr, 0, hk) * SCALE).astype(BF16)
            do4 = _stack_heads(dabc, 512, hk).astype(BF16)
            p, ps = _group_probs(kh, q4, valid, _sink_row(snkv, hk))
            dp = _dot(vb[:, HEAD * hk:HEAD * (hk + 1)], do4, NT)
            dd = jnp.sum(p * dp, axis=0, keepdims=True)
            dsink = -ps * dd
            for i in range(4):
                part = jnp.sum(jnp.where((qlane >= CHUNK * i) & (qlane < CHUNK * (i + 1)), dsink, 0.0),
                               axis=1, keepdims=True)
                dsnk = dsnk + jnp.where(lane8 == 4 * hk + i, part, 0.0)
            dsb = (p * (dp - dd)).astype(BF16)
            dqts.append(_dot(kt[HEAD * hk:HEAD * (hk + 1), :].astype(BF16), dsb))
            dkv_s[:, HEAD * hk:HEAD * (hk + 1)] = _dot(dsb, q4)
            dkv_s[:, 128 + HEAD * hk:128 + HEAD * (hk + 1)] = _dot(p.astype(BF16), do4)
        dq4 = jnp.concatenate(dqts, axis=0).T
        for hk in range(2):
            for i in range(4):
                h = 4 * hk + i
                dq_s[:, HEAD * h:HEAD * (h + 1)] = dq4[CHUNK * i:CHUNK * (i + 1), HEAD * hk:HEAD * (hk + 1)]
        dsnk_ref[...] += dsnk
        dqr = dq_s[...]
        dz_ref[:, 768:1280] = (dqr * jnp.concatenate([cc] * 4, axis=1)
                               + _rot_half(dqr * jnp.concatenate([sc] * 4, axis=1))).astype(BF16)
        dkv = dkv_s[...]
        dkr = dkv[CHUNK:, :128] + ck[...]
        dz_ref[:, 1280:1408] = (dkr * cc + _rot_half(dkr * sc)).astype(BF16)
        dz_ref[:, 1408:1536] = (dkv[CHUNK:, 128:] + cv[...]).astype(BF16)
        ck[...] = dkv[:CHUNK, :128]
        cv[...] = dkv[:CHUNK, 128:]

    return pl.pallas_call(
        body, name=name, grid=(nb,), in_specs=specs,
        out_specs=[pl.BlockSpec((CHUNK, 1536), lambda i: (cur(i), 0))] + [full(a) for a in acc_shapes],
        out_shape=[jax.ShapeDtypeStruct((s, 1536), BF16)] + [jax.ShapeDtypeStruct(a, F32) for a in acc_shapes],
        scratch_shapes=[pltpu.VMEM((CHUNK, 256), F32), pltpu.VMEM((CHUNK, 128), F32), pltpu.VMEM((CHUNK, 128), F32),
                        pltpu.VMEM((CHUNK, 512), F32), pltpu.VMEM((2 * CHUNK, 256), F32),
                        pltpu.VMEM((CHUNK, 256), F32), pltpu.VMEM((CHUNK, 256), F32)],
        compiler_params=_params(("arbitrary",)),
    )(z, z, z, cosq, sinq, cosq, sinq, gv, ws, bt, pw, psc, snk, _ones_bd(), dabc)


def _xattn_probs(qh, kh):
    s = _dot(qh, kh, NT) * (256 ** -0.5)
    e = jnp.exp(s - jnp.max(s, axis=-1, keepdims=True))
    return e * (1.0 / jnp.sum(e, axis=-1, keepdims=True))


def _xattn_fwd(name, q, kv, tq):
    s, d = q.shape
    mlen = kv.shape[0]

    def body(q_ref, kv_ref, o_ref):
        for h in range(4):
            lo, hi = 256 * h, 256 * (h + 1)
            p = _xattn_probs(q_ref[:, lo:hi], kv_ref[:, lo:hi])
            o_ref[:, lo:hi] = _dot(p.astype(BF16), kv_ref[:, d + lo:d + hi]).astype(BF16)

    blk = pl.BlockSpec((tq, d), lambda i: (i, 0))
    return pl.pallas_call(body, name=name, grid=(s // tq,),
                          in_specs=[blk, pl.BlockSpec((mlen, 2 * d), lambda i: (0, 0))], out_specs=blk,
                          out_shape=jax.ShapeDtypeStruct((s, d), BF16), compiler_params=_params(("parallel",)))(q, kv)


def _xattn_bwd(name, q, kv, do, tq):
    s, d = q.shape
    mlen = kv.shape[0]

    def body(q_ref, kv_ref, do_ref, dq_ref, dkv_ref):
        @pl.when(pl.program_id(0) == 0)
        def _():
            dkv_ref[...] = jnp.zeros_like(dkv_ref)

        for h in range(4):
            lo, hi = 256 * h, 256 * (h + 1)
            qh = q_ref[:, lo:hi]
            kh = kv_ref[:, lo:hi]
            vh = kv_ref[:, d + lo:d + hi]
            doh = do_ref[:, lo:hi]
            p = _xattn_probs(qh, kh)
            dp = _dot(doh, vh, NT)
            dsb = (p * (dp - jnp.sum(p * dp, axis=-1, keepdims=True)) * (256 ** -0.5)).astype(BF16)
            dq_ref[:, lo:hi] = _dot(dsb, kh).astype(BF16)
            dkv_ref[:, lo:hi] += _dot(dsb, qh, TN)
            dkv_ref[:, d + lo:d + hi] += _dot(p.astype(BF16), doh, TN)

    blk = pl.BlockSpec((tq, d), lambda i: (i, 0))
    kvb = pl.BlockSpec((mlen, 2 * d), lambda i: (0, 0))
    return pl.pallas_call(
        body, name=name, grid=(s // tq,), in_specs=[blk, kvb, blk], out_specs=[blk, kvb],
        out_shape=[jax.ShapeDtypeStruct((s, d), BF16), jax.ShapeDtypeStruct((mlen, 2 * d), F32)],
        compiler_params=_params(("arbitrary",)))(q, kv, do)


def _sigmoid(x):
    return 0.5 * (1.0 + jnp.tanh(0.5 * x))


def _ffn_up(name, h, wgu, layer, tm, tn):
    s, d = h.shape
    dff = wgu.shape[2] // 2
    nj = dff // tn

    def body(h_ref, wg_ref, wu_ref, g_ref, u_ref, a_ref):
        hv = h_ref[...]
        gate = _dot(hv, wg_ref[...])
        up = _dot(hv, wu_ref[...])
        g_ref[...] = gate.astype(BF16)
        u_ref[...] = up.astype(BF16)
        a_ref[...] = (gate * _sigmoid(gate) * up).astype(BF16)

    ob = pl.BlockSpec((tm, tn), lambda j, i: (i, j))
    sd = jax.ShapeDtypeStruct((s, dff), BF16)
    return pl.pallas_call(
        body, name=name, grid=(nj, s // tm),
        in_specs=[pl.BlockSpec((tm, d), lambda j, i: (i, 0)),
                  pl.BlockSpec((None, d, tn), lambda j, i: (layer, 0, j)),
                  pl.BlockSpec((None, d, tn), lambda j, i: (layer, 0, j + nj))],
        out_specs=[ob, ob, ob], out_shape=[sd, sd, sd], compiler_params=_params(("parallel", "parallel")),
    )(h, wgu, wgu)


def _ffn_act_bwd(name, dfn, wdown, layer, gate, up, tm):
    s, d = dfn.shape
    dff = gate.shape[1]

    def body(df_ref, wd_ref, g_ref, u_ref, o_ref):
        dact = _dot(df_ref[...], wd_ref[...], NT)
        gate = g_ref[...].astype(F32)
        sig = _sigmoid(gate)
        o_ref[:, :dff] = (dact * u_ref[...].astype(F32) * sig * (1.0 + gate * (1.0 - sig))).astype(BF16)
        o_ref[:, dff:] = (dact * gate * sig).astype(BF16)

    gb = pl.BlockSpec((tm, dff), lambda i: (i, 0))
    return pl.pallas_call(
        body, name=name, grid=(s // tm,),
        in_specs=[pl.BlockSpec((tm, d), lambda i: (i, 0)),
                  pl.BlockSpec((None, dff, d), lambda i: (layer, 0, 0)), gb, gb],
        out_specs=pl.BlockSpec((tm, 2 * dff), lambda i: (i, 0)),
        out_shape=jax.ShapeDtypeStruct((s, 2 * dff), BF16), compiler_params=_params(("parallel",)),
    )(dfn, wdown, gate, up)


def _place():
    return lax.axis_index("x"), lax.axis_index("y"), lax.axis_index("c")


def _other_chips(x, y):
    return [(1 - x, y), (x, 1 - y), (1 - x, 1 - y)]


def _region(ref, axis, chip, size):
    start = pl.multiple_of(chip * size, size)
    if axis == 1:
        return ref.at[:, pl.ds(start, size), :]
    return ref.at[:, :, pl.ds(start, size)]


ANY = pl.BlockSpec(memory_space=pl.ANY)


HBM = pl.BlockSpec(memory_space=pltpu.HBM)
SEM = pl.BlockSpec(memory_space=pltpu.SEMAPHORE)
EFFECT = pltpu.SideEffectType.DATAFLOW_SIDE_EFFECTING


def _in_hbm(a):
    return pltpu.with_memory_space_constraint(a, pltpu.HBM)


def _split_start(name, srcs, lands, ncopies, plan, after=None):
    ns, nl = len(srcs), len(lands)
    extra = [] if after is None else [after]

    def body(*refs):
        src, land = refs[:ns], refs[ns:ns + nl]
        send, recv = refs[ns + nl + len(extra)], refs[ns + nl + len(extra) + 1]
        token = refs[-1]
        x, y, c = _place()
        for k, (s_ref, d_ref, peer, _) in enumerate(plan(src, land, x, y, c)):
            pltpu.make_async_remote_copy(src_ref=s_ref, dst_ref=d_ref, send_sem=send.at[k], recv_sem=recv.at[k],
                                         device_id=peer, device_id_type=MESH).start()
        token[...] = jnp.zeros_like(token)

    ops = list(srcs) + list(lands)
    out = pl.pallas_call(
        body, name=name,
        out_shape=(pltpu.SemaphoreType.DMA((ncopies,)), pltpu.SemaphoreType.DMA((ncopies,)),
                   *[pltpu.HBM(a.shape, a.dtype) for a in ops], jax.ShapeDtypeStruct((8, 128), F32)),
        in_specs=(HBM,) * (ns + nl) + (ANY,) * len(extra),
        out_specs=(SEM, SEM) + (HBM,) * (ns + nl) + (pl.BlockSpec(memory_space=pltpu.VMEM),),
        input_output_aliases={i: 2 + i for i in range(ns + nl)},
        compiler_params=pltpu.CompilerParams(has_side_effects=EFFECT),
    )(*[_in_hbm(a) for a in ops], *extra)
    return out[0], out[1], list(out[2:2 + ns]), list(out[2 + ns:2 + ns + nl]), out[-1]


def _split_wait(name, send, recv, srcs, lands, after, plan):
    ns, nl = len(srcs), len(lands)

    def body(*refs):
        src, land = refs[:ns], refs[ns:ns + nl]
        send_ref, recv_ref = refs[ns + nl], refs[ns + nl + 1]
        x, y, c = _place()
        for k, (s_ref, _, _, got) in enumerate(plan(src, land, x, y, c)):
            cp = pltpu.make_async_remote_copy(src_ref=s_ref, dst_ref=got, send_sem=send_ref.at[k],
                                              recv_sem=recv_ref.at[k], device_id=(x, y, c), device_id_type=MESH)
            cp.wait_send()
            cp.wait_recv()

    ops = list(srcs) + list(lands)
    out = pl.pallas_call(
        body, name=name, out_shape=tuple(pltpu.HBM(a.shape, a.dtype) for a in ops),
        in_specs=(HBM,) * (ns + nl) + (SEM, SEM, ANY), out_specs=(HBM,) * (ns + nl),
        input_output_aliases={i: i for i in range(ns + nl)},
        compiler_params=pltpu.CompilerParams(has_side_effects=EFFECT),
    )(*ops, send, recv, after)
    return list(out[:ns]), list(out[ns:])


def _split_wait_start(name, send, recv, lands, after, wait_plan, ncopies, start_plan, carried=()):
    nl, nc = len(lands), len(carried)
    lands = list(lands) + list(carried)

    def body(*refs):
        land = refs[:nl]
        send_in, recv_in = refs[nl + nc], refs[nl + nc + 1]
        send_out, recv_out = refs[nl + nc + 3], refs[nl + nc + 4]
        x, y, c = _place()
        for k, (s_ref, _, _, got) in enumerate(wait_plan((), land, x, y, c)):
            cp = pltpu.make_async_remote_copy(src_ref=s_ref, dst_ref=got, send_sem=send_in.at[k],
                                              recv_sem=recv_in.at[k], device_id=(x, y, c), device_id_type=MESH)
            cp.wait_send()
            cp.wait_recv()
        for k, (s_ref, d_ref, peer, _) in enumerate(start_plan((), land, x, y, c)):
            pltpu.make_async_remote_copy(src_ref=s_ref, dst_ref=d_ref, send_sem=send_out.at[k],
                                         recv_sem=recv_out.at[k], device_id=peer, device_id_type=MESH).start()

    out = pl.pallas_call(
        body, name=name,
        out_shape=(pltpu.SemaphoreType.DMA((ncopies,)), pltpu.SemaphoreType.DMA((ncopies,)),
                   *[pltpu.HBM(a.shape, a.dtype) for a in lands]),
        in_specs=(HBM,) * (nl + nc) + (SEM, SEM, ANY), out_specs=(SEM, SEM) + (HBM,) * (nl + nc),
        input_output_aliases={i: 2 + i for i in range(nl + nc)},
        compiler_params=pltpu.CompilerParams(has_side_effects=EFFECT),
    )(*lands, send, recv, after)
    return out[0], out[1], list(out[2:2 + nl]), list(out[2 + nl:])


def _half(ref, axis, chip, size, layer, h):
    reg = _region(ref, axis, chip, size).at[pl.ds(layer, 1)]
    rows = reg.shape[1] // 2
    return reg.at[:, pl.ds(pl.multiple_of(h * rows, rows), rows), :]


def _gather_plan(axes, sizes, layer):
    def plan(src, land, x, y, c):
        me = 2 * x + y
        out = []
        for t in range(len(land)):
            mine = _half(land[t], axes[t], me, sizes[t], layer, c)
            for px, py in _other_chips(x, y):
                out.append((mine, mine, (px, py, c), _half(land[t], axes[t], 2 * px + py, sizes[t], layer, c)))
        return out
    return plan


def _forward_plan(axes, sizes, layer):
    def plan(src, land, x, y, c):
        out = []
        for t in range(len(land)):
            for px, py in _other_chips(x, y):
                got = _half(land[t], axes[t], 2 * px + py, sizes[t], layer, c)
                out.append((got, got, (x, y, 1 - c), _half(land[t], axes[t], 2 * px + py, sizes[t], layer, 1 - c)))
        return out
    return plan


def _place_own(name, w, axis, chip):
    nl, r, cs = w.shape
    tr = _rows_tile(r)
    nb = r // tr
    full = (nl, 4 * r, cs) if axis == 1 else (nl, r, 4 * cs)

    def body(m_ref, w_ref, o_ref):
        o_ref[...] = w_ref[...].astype(BF16)

    if axis == 1:
        ospec = pl.BlockSpec((None, tr, cs), lambda l, i, m: (l, m[0] * nb + i, 0))
    else:
        ospec = pl.BlockSpec((None, tr, cs), lambda l, i, m: (l, i, m[0]))
    return pl.pallas_call(
        body, name=name,
        grid_spec=pltpu.PrefetchScalarGridSpec(
            num_scalar_prefetch=1, grid=(nl, nb),
            in_specs=[pl.BlockSpec((None, tr, cs), lambda l, i, m: (l, i, 0))], out_specs=ospec),
        out_shape=jax.ShapeDtypeStruct(full, BF16), compiler_params=_params(("parallel", "parallel")),
    )(chip, w)


def _scatter_plan(axes, sizes):
    def plan(src, land, x, y, c):
        out = []
        for t in range(len(src)):
            for k, (px, py) in enumerate(_other_chips(x, y)):
                out.append((_region(src[t], axes[t], 2 * px + py, sizes[t]).at[0], land[t].at[k], (px, py, c),
                            land[t].at[k]))
        return out
    return plan


def _pair_plan(src, land, x, y, c):
    return [(src[t], land[t], (x, y, 1 - c), land[t]) for t in range(len(src))]


def _chip_sum(name, g, slots, axis, chip):
    _, r, cs = slots.shape
    tr = _rows_tile(r)
    nb = r // tr

    def body(m_ref, g_ref, s_ref, o_ref):
        acc = g_ref[...].astype(F32)
        for k in range(3):
            acc = acc + s_ref[k].astype(F32)
        o_ref[...] = acc.astype(BF16)

    if axis == 1:
        gspec = pl.BlockSpec((tr, cs), lambda i, m: (m[0] * nb + i, 0))
    else:
        gspec = pl.BlockSpec((tr, cs), lambda i, m: (i, m[0]))
    return pl.pallas_call(
        body, name=name,
        grid_spec=pltpu.PrefetchScalarGridSpec(
            num_scalar_prefetch=1, grid=(nb,),
            in_specs=[gspec, pl.BlockSpec((3, tr, cs), lambda i, m: (0, i, 0))],
            out_specs=pl.BlockSpec((tr, cs), lambda i, m: (i, 0))),
        out_shape=jax.ShapeDtypeStruct((r, cs), BF16), compiler_params=_params(("parallel",)),
    )(chip, g, slots)


def _pair_adamw(name, mine, theirs, w, m, v, layer, bufs):
    r, cs = mine.shape
    tr = _rows_tile(r)
    c1 = 1.0 - B1 ** STEP
    c2 = 1.0 - B2 ** STEP

    def body(a_ref, b_ref, w_ref, m_ref, v_ref, _g, _d, _m, _v, g_ref, d_ref, mo_ref, vo_ref):
        gv = a_ref[...].astype(F32) + b_ref[...].astype(F32)
        mn = B1 * m_ref[...] + (1.0 - B1) * gv
        vn = B2 * v_ref[...] + (1.0 - B2) * (gv * gv)
        g_ref[...] = gv
        mo_ref[...] = mn
        vo_ref[...] = vn
        d_ref[...] = -LR * ((mn / c1) / (jnp.sqrt(vn / c2) + ADAM_EPS) + WD * w_ref[...])

    blk = pl.BlockSpec((tr, cs), lambda i: (i, 0))
    lay = pl.BlockSpec((None, tr, cs), lambda i: (layer, i, 0))
    return pl.pallas_call(
        body, name=name, grid=(r // tr,), in_specs=[blk, blk, lay, lay, lay] + [ANY] * 4, out_specs=[lay] * 4,
        out_shape=[jax.ShapeDtypeStruct(b.shape, b.dtype) for b in bufs],
        input_output_aliases={5 + i: i for i in range(4)}, compiler_params=_params(("parallel",)),
    )(mine, theirs, w, m, v, *bufs)


def _allreduce_small(p):
    rows = p.shape[0]

    def body(p_ref, o_ref, sib, sums, send, recv):
        x, y, c = _place()
        me = 2 * x + y
        pair = pltpu.make_async_remote_copy(src_ref=p_ref, dst_ref=sib, send_sem=send.at[3], recv_sem=recv.at[3],
                                            device_id=(x, y, 1 - c), device_id_type=MESH)
        pair.start()
        pair.wait()
        sums[me] = (p_ref[...] + sib[...]).astype(BF16)
        cps = [pltpu.make_async_remote_copy(src_ref=sums.at[me], dst_ref=sums.at[me], send_sem=send.at[k],
                                            recv_sem=recv.at[k], device_id=(px, py, c), device_id_type=MESH)
               for k, (px, py) in enumerate(_other_chips(x, y))]
        for cp in cps:
            cp.start()
        for k, (px, py) in enumerate(_other_chips(x, y)):
            slot = sums.at[2 * px + py]
            pltpu.make_async_remote_copy(src_ref=slot, dst_ref=slot, send_sem=send.at[k], recv_sem=recv.at[k],
                                         device_id=(x, y, c), device_id_type=MESH).wait_recv()
        for cp in cps:
            cp.wait_send()
        acc = sums[0].astype(F32)
        for k in range(1, 4):
            acc = acc + sums[k].astype(F32)
        o_ref[...] = acc

    vm = pl.BlockSpec(memory_space=pltpu.VMEM)
    return pl.pallas_call(
        body, name="allreduce_small", in_specs=[vm], out_specs=vm, out_shape=jax.ShapeDtypeStruct(p.shape, F32),
        scratch_shapes=[pltpu.VMEM((rows, 128), F32), pltpu.VMEM((4, rows, 128), BF16),
                        pltpu.SemaphoreType.DMA((4,)), pltpu.SemaphoreType.DMA((4,))],
        compiler_params=pltpu.CompilerParams(vmem_limit_bytes=VMEM_LIMIT),
    )(p)


def _pack(parts):
    flat = []
    for p in parts:
        v = p.reshape(-1).astype(F32)
        flat.append(jnp.pad(v, (0, (-v.shape[0]) % 128)))
    v = jnp.concatenate(flat)
    v = jnp.pad(v, (0, (-v.shape[0]) % (512 * 128)))
    return v.reshape(-1, 128)


def _unpack(buf, like):
    out, r0 = [], 0
    for p in like:
        nelem = 1
        for s in p.shape:
            nelem *= s
        rows = -(-nelem // 128)
        blk = buf[r0:r0 + rows]
        if nelem % 128:
            blk = blk.reshape(-1)[:nelem]
        out.append(blk.reshape(p.shape))
        r0 += rows
    return out


def kernel(x, mem, positions, mem_norm_g, mix_pre_g, mix_post_g, w_in, gm_v_g, gm_w_s, gm_b_s, pool_w, pool_scale, attn_sinks, w_o, x_pre_g, x_post_g, w_xq, w_xkv, w_xo, ffn_pre_g, ffn_post_g, w_gate_up, w_down, loss_target, m_mem_norm_g, m_mix_pre_g, m_mix_post_g, m_w_in, m_gm_v_g, m_gm_w_s, m_gm_b_s, m_pool_w, m_pool_scale, m_attn_sinks, m_w_o, m_x_pre_g, m_x_post_g, m_w_xq, m_w_xkv, m_w_xo, m_ffn_pre_g, m_ffn_post_g, m_w_gate_up, m_w_down, v_mem_norm_g, v_mix_pre_g, v_mix_post_g, v_w_in, v_gm_v_g, v_gm_w_s, v_gm_b_s, v_pool_w, v_pool_scale, v_attn_sinks, v_w_o, v_x_pre_g, v_x_post_g, v_w_xq, v_w_xkv, v_w_xo, v_ffn_pre_g, v_ffn_post_g, v_w_gate_up, v_w_down):
    args = (x, mem, positions, mem_norm_g, mix_pre_g, mix_post_g, w_in, gm_v_g, gm_w_s, gm_b_s, pool_w, pool_scale, attn_sinks, w_o, x_pre_g, x_post_g, w_xq, w_xkv, w_xo, ffn_pre_g, ffn_post_g, w_gate_up, w_down)
    moms_m = (m_mem_norm_g, m_mix_pre_g, m_mix_post_g, m_w_in, m_gm_v_g, m_gm_w_s, m_gm_b_s, m_pool_w, m_pool_scale, m_attn_sinks, m_w_o, m_x_pre_g, m_x_post_g, m_w_xq, m_w_xkv, m_w_xo, m_ffn_pre_g, m_ffn_post_g, m_w_gate_up, m_w_down)
    moms_v = (v_mem_norm_g, v_mix_pre_g, v_mix_post_g, v_w_in, v_gm_v_g, v_gm_w_s, v_gm_b_s, v_pool_w, v_pool_scale, v_attn_sinks, v_w_o, v_x_pre_g, v_x_post_g, v_w_xq, v_w_xkv, v_w_xo, v_ffn_pre_g, v_ffn_post_g, v_w_gate_up, v_w_down)
    P = dict(zip(NAMES, args))
    P['loss_target'] = loss_target
    M = dict(zip(WEIGHTS, moms_m))
    V = dict(zip(WEIGHTS, moms_v))
    depth = w_in.shape[0]
    nbig = len(BIG)
    axes = [BIG_AXIS[n] for n in BIG]
    sizes = [P[n].shape[a] for n, a in zip(BIG, axes)]
    chip = (2 * lax.axis_index("x") + lax.axis_index("y")).astype(jnp.int32).reshape(1)

    groups = [['w_in'], ['w_o', 'w_xq', 'w_xkv', 'w_xo'], ['w_gate_up', 'w_down']]
    units = [(l, g) for l in range(depth) for g in groups]
    unit_of = {(l, n): i for i, (l, names) in enumerate(units) for n in names}
    ax = lambda names: [BIG_AXIS[n] for n in names]
    sz = lambda names: [P[n].shape[BIG_AXIS[n]] for n in names]

    full = {n: _place_own("place_" + n, P[n], BIG_AXIS[n], chip) for n in BIG}
    gathers = []
    prev, tok = None, None
    for i, (l, names) in enumerate(units):
        send, recv, _, land, tok = _split_start("gather_start%d" % i, [], [full[n] for n in names],
                                                3 * len(names), _gather_plan(ax(names), sz(names), l), after=prev)
        full.update(zip(names, land))
        gathers.append((send, recv))
        prev = land[0]
    P['first_dep'] = tok[:1, :1]
    forwards, gathered = {}, set()

    def forward_unit(i, after, carried=()):
        ul, unames = units[i]
        send, recv = gathers[i]
        send, recv, land, thru = _split_wait_start(
            "gather_pass%d" % i, send, recv, [full[n] for n in unames], after,
            _gather_plan(ax(unames), sz(unames), ul), 3 * len(unames), _forward_plan(ax(unames), sz(unames), ul),
            carried=[full[n] for n in carried])
        full.update(zip(unames, land))
        full.update(zip(carried, thru))
        forwards[i] = (send, recv)

    def weights_of(l, names, after):
        i = unit_of[(l, names[0])]
        if i not in gathered:
            _, unames = units[i]
            if i not in forwards:
                forward_unit(i, after)
            send, recv = forwards.pop(i)
            _, land = _split_wait("gather_wait%d" % i, send, recv, [], [full[n] for n in unames], after,
                                  _forward_plan(ax(unames), sz(unames), l))
            full.update(zip(unames, land))
            gathered.add(i)
            if len(groups) <= i + 1 < len(units):
                forward_unit(i + 1, after, carried=[n for n in unames if n not in units[i + 1][1]])
        return {n: (full[n], l) for n in names}

    outs = {n: [lax.empty(P[n].shape, F32) for _ in range(4)] for n in BIG}
    gunits = [(l, BIG) for l in range(depth - 1, 0, -1)] + [
        (0, g) for g in (['w_gate_up', 'w_down'], ['w_xq', 'w_xkv', 'w_xo'], ['w_o'], ['w_in'])]
    collected, scatters, pairs = {}, {}, {}

    def finish_scatter(i, after):
        _, names = gunits[i]
        send, recv, g_l, slots = scatters.pop(i)
        g_l, slots = _split_wait("scatter_wait%d" % i, send, recv, g_l, slots, after,
                                 _scatter_plan(ax(names), sz(names)))
        mine = [_chip_sum("chip_sum_" + n, g.reshape(g.shape[1:]), sl, BIG_AXIS[n], chip)
                for n, g, sl in zip(names, g_l, slots)]
        send, recv, mine, theirs, tok = _split_start("pair_start%d" % i, mine, [lax.empty(a.shape, BF16) for a in mine],
                                                     len(names), _pair_plan)
        pairs[i] = (send, recv, mine, theirs)
        return tok[:1, :1]

    def finish_pair(i, after):
        l, names = gunits[i]
        send, recv, mine, theirs = pairs.pop(i)
        mine, theirs = _split_wait("pair_wait%d" % i, send, recv, mine, theirs, after, _pair_plan)
        for n, a, b in zip(names, mine, theirs):
            outs[n] = _pair_adamw("adamw_" + n, a, b, P[n], M[n], V[n], l, outs[n])

    calls = {'n': 0}
    lag = 4

    def grads_of(l, g_part, after):
        collected.update({(l, n): g for n, g in g_part.items()})
        calls['n'] += 1
        now = calls['n']
        tok = jnp.zeros((1, 1), F32)
        for i, (ul, names) in enumerate(gunits):
            if ul != l or ('started', i) in collected or any((l, n) not in collected for n in names):
                continue
            collected[('started', i)] = now
            srcs = [collected[(l, n)].reshape((1,) + collected[(l, n)].shape) for n in names]
            send, recv, srcs, slots, t = _split_start("scatter_start%d" % i, srcs,
                                                      [lax.empty((3,) + P[n].shape[1:], BF16) for n in names],
                                                      3 * len(names), _scatter_plan(ax(names), sz(names)))
            scatters[i] = (send, recv, srcs, slots)
            tok = tok + t[:1, :1]
        for i in sorted(pairs):
            if collected[('summed', i)] + lag <= now:
                finish_pair(i, after)
        for i in sorted(scatters):
            if collected[('started', i)] + lag <= now:
                tok = tok + finish_scatter(i, after)
                collected[('summed', i)] = now
        return tok

    loss_part, dx, small_g = _fwd_bwd(P, weights_of, grads_of)
    loss = lax.psum(loss_part[0, 0], ("x", "y", "c"))
    grad_x = dx.reshape(x.shape)

    small_like = [P[n] for n in SMALL]
    gsum = _allreduce_small(_pack(small_g))
    dlt, mn, vn = _adamw("adamw_small", gsum, _pack(small_like), _pack([M[n] for n in SMALL]),
                         _pack([V[n] for n in SMALL]), 512)
    grads, deltas, new_m, new_v = {}, {}, {}, {}
    for name_map, buf in ((grads, gsum), (deltas, dlt), (new_m, mn), (new_v, vn)):
        for n, a in zip(SMALL, _unpack(buf, small_like)):
            name_map[n] = a

    for i in sorted(pairs):
        finish_pair(i, dlt)
    for i in sorted(scatters):
        finish_scatter(i, dlt)
    for i in sorted(pairs):
        finish_pair(i, dlt)
    for n in BIG:
        grads[n], deltas[n], new_m[n], new_v[n] = outs[n]

    return (loss, grad_x, *[grads[n] for n in WEIGHTS], *[deltas[n] for n in WEIGHTS],
            *[new_m[n] for n in WEIGHTS], *[new_v[n] for n in WEIGHTS])


def _fwd_bwd(P, weights_of, grads_of):
    (x, mem, positions, mem_norm_g, mix_pre_g, mix_post_g, w_in, gm_v_g, gm_w_s, gm_b_s, pool_w, pool_scale, attn_sinks,
     w_o, x_pre_g, x_post_g, w_xq, w_xkv, w_xo, ffn_pre_g, ffn_post_g, w_gate_up, w_down) = [P[n] for n in NAMES]
    x0 = x[0]
    s, d = x0.shape
    depth = w_in.shape[0]
    tgt = P['loss_target'][0]
    tmn = 256
    tmr = min(512, s)
    tkw = min(2048, s)

    half = HEAD // 2
    inv = ROPE_THETA ** (-jnp.arange(half, dtype=F32) / half)
    ang = positions[0].astype(F32)[:, None] * inv
    cos, sin = jnp.cos(ang), jnp.sin(ang)
    cosq = jnp.tile(jnp.concatenate([cos, cos], axis=-1), (1, 2))
    sinq = jnp.tile(jnp.concatenate([-sin, sin], axis=-1), (1, 2))

    row = lambda a, l: a[l].reshape(1, -1)
    memn = _prenorm("mem_norm", mem[0], mem_norm_g.reshape(1, d), tmn)
    pw_bd = []
    for l in range(depth):
        bd = jnp.zeros((256, 256), F32)
        for g in range(4):
            bd = lax.dynamic_update_slice(bd, pool_w[l, g], (64 * g, 64 * g))
        pw_bd.append(bd)

    saved = []
    xc = x0
    h = _prenorm("pre_norm0", x0, row(mix_pre_g, 0) + P['first_dep'], tmn)
    for l in range(depth):
        W = weights_of(l, ['w_in'], xc)
        sv = {'x0': xc, 'h1': h}
        z, = _mm_rows("fwd_w_in", h, *W['w_in'], 'nn', tm=tmr, rows_out=[F32], epilogue=_plain_rows)
        abc = _mixer_fwd("mixer_fwd", z, cosq, sinq, row(gm_v_g, l), gm_w_s[l], gm_b_s[l].T, pw_bd[l],
                         row(pool_scale, l), row(attn_sinks, l))
        W.update(weights_of(l, ['w_o'], z))
        mix, xc, h = _mm_rows("fwd_w_o", abc, *W['w_o'], 'nn', tm=tmr, rows_in=[xc],
                              params=[row(mix_post_g, l), row(x_pre_g, l)], rows_out=[BF16, F32, BF16],
                              epilogue=_post_pre_rows)
        sv.update(z=z, abc=abc, mix=mix, x1=xc, h2=h)
        W.update(weights_of(l, ['w_xq', 'w_xkv', 'w_xo'], xc))
        q, = _mm_rows("fwd_w_xq", h, *W['w_xq'], 'nn', tm=tmr, rows_out=[BF16], epilogue=_plain_rows)
        kv = _mm_nn("fwd_w_xkv", memn, *W['w_xkv'], tm=256, tn=512, tk=d, out_dtype=BF16)
        o = _xattn_fwd("xattn_fwd", q, kv, 512)
        xo, xc, h = _mm_rows("fwd_w_xo", o, *W['w_xo'], 'nn', tm=tmr, rows_in=[xc],
                             params=[row(x_post_g, l), row(ffn_pre_g, l)], rows_out=[BF16, F32, BF16],
                             epilogue=_post_pre_rows)
        sv.update(q=q, kv=kv, o=o, xo=xo, x2=xc, h3=h)
        W.update(weights_of(l, ['w_gate_up', 'w_down'], xc))
        dff = W['w_down'][0].shape[1]
        gate, up, act = _ffn_up("ffn_up", h, *W['w_gate_up'], 512, dff // 2)
        sv.update(gate=gate, up=up, act=act)
        if l + 1 < depth:
            f, xc, h = _mm_rows("fwd_w_down", act, *W['w_down'], 'nn', tm=tmr, rows_in=[xc],
                                params=[row(ffn_post_g, l), row(mix_pre_g, l + 1)], rows_out=[BF16, F32, BF16],
                                epilogue=_post_pre_rows)
            sv.update(f=f)
        saved.append(sv)
    gs = {n: [None] * depth for n in SMALL if n != 'mem_norm_g'}
    dx, dfn, gs['ffn_post_g'][depth - 1], loss_part = _mm_rows(
        "fwd_w_down_loss", saved[-1]['act'], *W['w_down'], 'nn', tm=tmr, rows_in=[xc, tgt],
        params=[row(ffn_post_g, depth - 1)], rows_out=[F32, BF16], n_sums=2, epilogue=_make_loss_rows(d))

    dmemn = None
    tok = jnp.zeros((1, 1), F32)
    for l in reversed(range(depth)):
        sv, W, G = saved[l], weights_of(l, BIG, dx), {}
        G['w_down'] = _mm_tn("dw_down", sv['act'], dfn, tm=dff // 2, tn=d, tk=tkw)
        dgu = _ffn_act_bwd("ffn_act_bwd", dfn, *W['w_down'], sv['gate'], sv['up'], 256)
        G['w_gate_up'] = _mm_tn("dw_gate_up", sv['h3'], dgu, tm=d, tn=dff // 2, tk=tkw)
        dx, dxo, gs['ffn_pre_g'][l], gs['x_post_g'][l] = _mm_rows(
            "bwd_w_gate_up", dgu, *W['w_gate_up'], 'nt', tm=tmr, rows_in=[sv['x2'], dx, sv['xo']],
            params=[row(ffn_pre_g, l) + tok, row(x_post_g, l)], rows_out=[F32, BF16], n_sums=2, epilogue=_bwd_rows)
        tok = grads_of(l, {n: G[n] for n in ('w_gate_up', 'w_down')}, dx)
        G['w_xo'] = _mm_tn("dw_xo", sv['o'], dxo, tm=d, tn=d, tk=tkw)
        do, = _mm_rows("bwd_w_xo", dxo, *W['w_xo'], 'nt', tm=tmr, params=[jnp.zeros((1, d), F32) + tok],
                       rows_out=[BF16], epilogue=_plain_rows)
        dq, dkv = _xattn_bwd("xattn_bwd", sv['q'], sv['kv'], do, 512)
        dkv = dkv.astype(BF16)
        G['w_xkv'] = _mm_tn("dw_xkv", memn, dkv, tm=d, tn=d, tk=mem.shape[1])
        dmemn = _mm_nt("bwd_w_xkv", dkv, *W['w_xkv'], tm=mem.shape[1], tn=512, tk=2 * d, out_dtype=F32, add=dmemn)
        G['w_xq'] = _mm_tn("dw_xq", sv['h2'], dq, tm=d, tn=d, tk=tkw)
        dx, dmix, gs['x_pre_g'][l], gs['mix_post_g'][l] = _mm_rows(
            "bwd_w_xq", dq, *W['w_xq'], 'nt', tm=tmr, rows_in=[sv['x1'], dx, sv['mix']],
            params=[row(x_pre_g, l), row(mix_post_g, l)], rows_out=[F32, BF16], n_sums=2, epilogue=_bwd_rows)
        tok = grads_of(l, {n: G[n] for n in ('w_xq', 'w_xkv', 'w_xo')}, dx)
        G['w_o'] = _mm_tn("dw_o", sv['abc'], dmix, tm=d, tn=d, tk=tkw)
        dabc, = _mm_rows("bwd_w_o", dmix, *W['w_o'], 'nt', tm=tmr, params=[jnp.zeros((1, d), F32) + tok],
                         rows_out=[F32], epilogue=_plain_rows)
        tok = grads_of(l, {'w_o': G['w_o']}, dabc)
        dz, dgv, dws, dbt, dpw, dpsc, dsnk = _mixer_bwd(
            "mixer_bwd", sv['z'], dabc, cosq, sinq, row(gm_v_g, l) + tok, gm_w_s[l], gm_b_s[l].T, pw_bd[l],
            row(pool_scale, l), row(attn_sinks, l))
        gs['gm_v_g'][l] = dgv
        gs['gm_w_s'][l] = dws
        gs['gm_b_s'][l] = dbt.T
        gs['pool_w'][l] = jnp.stack([dpw[64 * g:64 * (g + 1), 64 * g:64 * (g + 1)] for g in range(4)])
        gs['pool_scale'][l] = dpsc
        gs['attn_sinks'][l] = dsnk
        G['w_in'] = _mm_tn("dw_in", sv['h1'], dz, tm=d, tn=dz.shape[1], tk=tkw)
        if l > 0:
            dx, dfn, gs['mix_pre_g'][l], gs['ffn_post_g'][l - 1] = _mm_rows(
                "bwd_w_in", dz, *W['w_in'], 'nt', tm=tmr, rows_in=[sv['x0'], dx, saved[l - 1]['f']],
                params=[row(mix_pre_g, l), row(ffn_post_g, l - 1)], rows_out=[F32, BF16], n_sums=2,
                epilogue=_bwd_rows)
        else:
            dx, gs['mix_pre_g'][l] = _mm_rows(
                "bwd_w_in_first", dz, *W['w_in'], 'nt', tm=tmr, rows_in=[sv['x0'], dx],
                params=[row(mix_pre_g, l)], rows_out=[F32], n_sums=1, epilogue=_bwd_rows_first)
        tok = grads_of(l, {'w_in': G['w_in']}, dx)
    _, dg_mem = _norm_bwd("bwd_mem_norm", mem[0], mem_norm_g.reshape(1, d) + tok, dmemn, None, BF16, tmn)
    small_g = []
    for n in SMALL:
        if n == 'mem_norm_g':
            small_g.append(dg_mem.reshape(P[n].shape))
        else:
            small_g.append(jnp.stack([a.reshape(P[n].shape[1:]) for a in gs[n]]))
    return loss_part, dx, small_g
```

```python
import functools

import jax
import jax.numpy as jnp
from jax import lax
from jax.experimental import pallas as pl
from jax.experimental.pallas import tpu as pltpu

F32 = jnp.float32
BF16 = jnp.bfloat16
EPS = 1e-6
CHUNK = 128
HEAD = 64
ROPE_THETA = 10000.0
POOL_WINDOWS = (2, 4, 8, 16)
LR, B1, B2, ADAM_EPS, WD, STEP = 0.001, 0.9, 0.999, 1e-08, 0.01, 10
MESH = pl.DeviceIdType.MESH
VMEM_LIMIT = 56 * 1024 * 1024

NAMES = ['x', 'mem', 'positions', 'mem_norm_g', 'mix_pre_g', 'mix_post_g', 'w_in', 'gm_v_g', 'gm_w_s', 'gm_b_s',
         'pool_w', 'pool_scale', 'attn_sinks', 'w_o', 'x_pre_g', 'x_post_g', 'w_xq', 'w_xkv', 'w_xo', 'ffn_pre_g',
         'ffn_post_g', 'w_gate_up', 'w_down']
WEIGHTS = NAMES[3:]
BIG = ['w_in', 'w_o', 'w_xq', 'w_xkv', 'w_xo', 'w_gate_up', 'w_down']
BIG_AXIS = {'w_in': 2, 'w_o': 1, 'w_xq': 1, 'w_xkv': 2, 'w_xo': 1, 'w_gate_up': 2, 'w_down': 1}
SMALL = [n for n in WEIGHTS if n not in BIG]

NN = (((1,), (0,)), ((), ()))
NT = (((1,), (1,)), ((), ()))
TN = (((0,), (0,)), ((), ()))


def _dot(a, b, dims=NN):
    return lax.dot_general(a, b, dims, preferred_element_type=F32)


def _params(sem):
    return pltpu.CompilerParams(dimension_semantics=sem, vmem_limit_bytes=VMEM_LIMIT)


STREAM_BLOCK_BYTES = 3 * 512 * 1024


def _rows_tile(rows, cols):
    limit = max(16, STREAM_BLOCK_BYTES // (4 * cols))
    return max(t for t in range(16, min(rows, limit) + 1, 16) if rows % t == 0)


def _mm(name, a, a_spec, b, b_spec, dims, grid, nk, out_shape, out_spec, add=None, add_spec=None):
    acc_shape = out_spec.block_shape

    def body(*refs):
        a_ref, b_ref = refs[0], refs[1]
        pos = 2
        add_ref = None
        if add is not None:
            add_ref = refs[pos]
            pos += 1
        o_ref = refs[pos]
        part = _dot(a_ref[...].astype(BF16), b_ref[...].astype(BF16), dims)
        if nk == 1:
            if add_ref is not None:
                part = part + add_ref[...]
            o_ref[...] = part.astype(o_ref.dtype)
        else:
            acc_ref = refs[pos + 1]
            k = pl.program_id(2)

            @pl.when(k == 0)
            def _():
                acc_ref[...] = part if add_ref is None else part + add_ref[...]

            @pl.when(k > 0)
            def _():
                acc_ref[...] += part

            @pl.when(k == nk - 1)
            def _():
                o_ref[...] = acc_ref[...].astype(o_ref.dtype)

    ops, specs = [a, b], [a_spec, b_spec]
    if add is not None:
        ops.append(add)
        specs.append(add_spec)
    return pl.pallas_call(
        body, name=name, grid=grid, in_specs=specs, out_specs=out_spec, out_shape=out_shape,
        scratch_shapes=[pltpu.VMEM(acc_shape, F32)] if nk > 1 else [],
        compiler_params=_params(("parallel", "parallel", "arbitrary")),
    )(*ops)


def _wspec(block, layer, fn):
    return pl.BlockSpec((None,) + block, lambda i, j, k: (layer,) + fn(i, j, k))


def _mm_nn(name, a, w, layer, *, tm, tn, tk, out_dtype):
    m, kk = a.shape
    n = w.shape[2]
    tm = min(tm, m)
    nk = kk // tk
    return _mm(name, a, pl.BlockSpec((tm, tk), lambda i, j, k: (i, k)),
               w, _wspec((tk, tn), layer, lambda i, j, k: (k, j)), NN,
               (m // tm, n // tn, nk), nk, jax.ShapeDtypeStruct((m, n), out_dtype),
               pl.BlockSpec((tm, tn), lambda i, j, k: (i, j)))


def _mm_nt(name, a, w, layer, *, tm, tn, tk, out_dtype, add=None):
    m, kk = a.shape
    n = w.shape[1]
    tm = min(tm, m)
    nk = kk // tk
    ospec = pl.BlockSpec((tm, tn), lambda i, j, k: (i, j))
    return _mm(name, a, pl.BlockSpec((tm, tk), lambda i, j, k: (i, k)),
               w, _wspec((tn, tk), layer, lambda i, j, k: (j, k)), NT,
               (m // tm, n // tn, nk), nk, jax.ShapeDtypeStruct((m, n), out_dtype), ospec,
               add=add, add_spec=ospec if add is not None else None)


def _mm_tn(name, a, b, *, tm, tn, tk):
    kk, m = a.shape
    n = b.shape[1]
    tk = min(tk, kk)
    nk = kk // tk
    return _mm(name, a, pl.BlockSpec((tk, tm), lambda i, j, k: (k, i)),
               b, pl.BlockSpec((tk, tn), lambda i, j, k: (k, j)), TN,
               (m // tm, n // tn, nk), nk, jax.ShapeDtypeStruct((m, n), BF16),
               pl.BlockSpec((tm, tn), lambda i, j, k: (i, j)))


def _mm_rows(name, a, w, layer, mode, *, tm, rows_in=(), params=(), rows_out=(), n_sums=0, epilogue):
    m, kk = a.shape
    n = w.shape[2] if mode == 'nn' else w.shape[1]
    nr, npar, no = len(rows_in), len(params), len(rows_out)

    def body(*refs):
        a_ref, w_ref = refs[0], refs[1]
        rin = refs[2:2 + nr]
        par = refs[2 + nr:2 + nr + npar]
        outs = refs[2 + nr + npar:2 + nr + npar + no]
        sums = refs[2 + nr + npar + no:2 + nr + npar + no + n_sums]
        acc = _dot(a_ref[...], w_ref[...], NN if mode == 'nn' else NT)
        res, sm = epilogue(acc, [r[...] for r in rin], [p[...] for p in par])
        for r, v in zip(outs, res):
            r[...] = v.astype(r.dtype)

        @pl.when(pl.program_id(0) == 0)
        def _():
            for r in sums:
                r[...] = jnp.zeros_like(r)

        for r, v in zip(sums, sm):
            r[...] += v

    wblock = (None, kk, n) if mode == 'nn' else (None, n, kk)
    rowblk = pl.BlockSpec((tm, n), lambda i: (i, 0))
    one = pl.BlockSpec((1, n), lambda i: (0, 0))
    return pl.pallas_call(
        body, name=name, grid=(m // tm,),
        in_specs=[pl.BlockSpec((tm, kk), lambda i: (i, 0)),
                  pl.BlockSpec(wblock, lambda i: (layer, 0, 0), pipeline_mode=pl.Buffered(1))]
                 + [rowblk] * nr + [one] * npar,
        out_specs=[rowblk] * no + [one] * n_sums,
        out_shape=[jax.ShapeDtypeStruct((m, n), dt) for dt in rows_out] +
                  [jax.ShapeDtypeStruct((1, n), F32)] * n_sums,
        compiler_params=_params(("arbitrary",)),
    )(a, w, *rows_in, *params)


def _rstd(x):
    return lax.rsqrt(jnp.mean(x * x, axis=-1, keepdims=True) + EPS)


def _norm_back(xin, g, dy):
    r = _rstd(xin)
    xh = xin * r
    dyg = dy * g
    return r * (dyg - xh * jnp.mean(dyg * xh, axis=-1, keepdims=True)), jnp.sum(dy * xh, axis=0, keepdims=True)


def _plain_rows(acc, rows, pars):
    return [acc], []


def _post_pre_rows(y, rows, pars):
    xn = rows[0] + y * _rstd(y) * pars[0]
    return [y, xn, xn * _rstd(xn) * pars[1]], []


def _make_loss_rows(d):
    def fn(y, rows, pars):
        x, tgt = rows
        err = x + y * _rstd(y) * pars[0] - tgt
        dout = err * (1.0 / d)
        dy, dg = _norm_back(y, pars[0], dout)
        lsum = 0.5 * jnp.sum(jnp.mean(err * err, axis=-1, keepdims=True), axis=0, keepdims=True)
        return [dout, dy], [dg, jnp.broadcast_to(lsum, dg.shape)]
    return fn


def _bwd_rows(dh, rows, pars):
    xin, resid, yprev = rows
    dxa, dg_pre = _norm_back(xin, pars[0], dh)
    dx = resid + dxa
    dyp, dg_post = _norm_back(yprev.astype(F32), pars[1], dx)
    return [dx, dyp], [dg_pre, dg_post]


def _bwd_rows_first(dh, rows, pars):
    xin, resid = rows
    dxa, dg_pre = _norm_back(xin, pars[0], dh)
    return [resid + dxa], [dg_pre]


def _row(d):
    return pl.BlockSpec((1, d), lambda i: (0, 0))


def _prenorm(name, x, g, tm):
    m, d = x.shape

    def body(x_ref, g_ref, o_ref):
        xv = x_ref[...]
        o_ref[...] = (xv * _rstd(xv) * g_ref[...]).astype(BF16)

    blk = pl.BlockSpec((tm, d), lambda i: (i, 0))
    return pl.pallas_call(body, name=name, grid=(m // tm,), in_specs=[blk, _row(d)], out_specs=blk,
                          out_shape=jax.ShapeDtypeStruct((m, d), BF16), compiler_params=_params(("parallel",)))(x, g)


def _norm_bwd(name, xin, g, dy, resid, out_dtype, tm):
    m, d = xin.shape

    def body(*refs):
        if resid is None:
            x_ref, g_ref, dy_ref, dx_ref, dg_ref = refs
        else:
            x_ref, g_ref, dy_ref, r_ref, dx_ref, dg_ref = refs
        xv = x_ref[...]
        r = _rstd(xv)
        xh = xv * r
        dyv = dy_ref[...].astype(F32)
        dyg = dyv * g_ref[...]
        dx = r * (dyg - xh * jnp.mean(dyg * xh, axis=-1, keepdims=True))
        if resid is not None:
            dx = dx + r_ref[...]
        dx_ref[...] = dx.astype(dx_ref.dtype)

        @pl.when(pl.program_id(0) == 0)
        def _():
            dg_ref[...] = jnp.zeros_like(dg_ref)

        dg_ref[...] += jnp.sum(dyv * xh, axis=0, keepdims=True)

    blk = pl.BlockSpec((tm, d), lambda i: (i, 0))
    ops = [xin, g, dy] + ([] if resid is None else [resid])
    specs = [blk, _row(d), blk] + ([] if resid is None else [blk])
    return pl.pallas_call(
        body, name=name, grid=(m // tm,), in_specs=specs, out_specs=[blk, _row(d)],
        out_shape=[jax.ShapeDtypeStruct((m, d), out_dtype), jax.ShapeDtypeStruct((1, d), F32)],
        compiler_params=_params(("arbitrary",)))(*ops)


def _adamw(name, g, w, m, v, tr):
    rows, cols = g.shape
    c1 = 1.0 - B1 ** STEP
    c2 = 1.0 - B2 ** STEP

    def body(g_ref, w_ref, m_ref, v_ref, d_ref, mo_ref, vo_ref):
        gv = g_ref[...]
        mn = B1 * m_ref[...] + (1.0 - B1) * gv
        vn = B2 * v_ref[...] + (1.0 - B2) * (gv * gv)
        mo_ref[...] = mn
        vo_ref[...] = vn
        d_ref[...] = -LR * ((mn / c1) / (jnp.sqrt(vn / c2) + ADAM_EPS) + WD * w_ref[...])

    blk = pl.BlockSpec((tr, cols), lambda i: (i, 0))
    sd = jax.ShapeDtypeStruct((rows, cols), F32)
    return pl.pallas_call(body, name=name, grid=(rows // tr,), in_specs=[blk] * 4, out_specs=[blk] * 3,
                          out_shape=[sd, sd, sd], compiler_params=_params(("parallel",)))(g, w, m, v)


def _gelu_parts(x):
    c = 0.7978845608028654
    t = jnp.tanh(c * (x + 0.044715 * (x * x * x)))
    return 0.5 * x * (1.0 + t), t


def _gelu_grad(x, t):
    c = 0.7978845608028654
    return 0.5 * (1.0 + t) + 0.5 * x * (1.0 - t * t) * (c * (1.0 + 3.0 * 0.044715 * x * x))


def _rot_half(x):
    ax = x.ndim - 1
    w = x.shape[ax]
    lane = lax.broadcasted_iota(jnp.int32, x.shape, ax)
    return jnp.where((lane & 63) < 32, pltpu.roll(x, w - 32, ax), pltpu.roll(x, 32, ax))


def _group_mean(x, ones_bd):
    hi = x.astype(BF16)
    lo = (x - hi.astype(F32)).astype(BF16)
    return (_dot(hi, ones_bd) + _dot(lo, ones_bd)) * (1.0 / HEAD)


def _gating(gel, gv, ws_ref, bt, ones_bd, mix_s):
    u = gel[:, :256]
    v = gel[:, 256:]
    r = lax.rsqrt(_group_mean(v * v, ones_bd) + EPS)
    xh = v * r
    vn = (xh * gv).astype(BF16)
    row = lax.broadcasted_iota(jnp.int32, (CHUNK, CHUNK), 0)
    col = lax.broadcasted_iota(jnp.int32, (CHUNK, CHUNK), 1)
    causal = col <= row
    wcs = []
    for g in range(4):
        wc = jnp.where(causal, ws_ref[g], 0.0).astype(BF16)
        wcs.append(wc)
        mix_s[:, HEAD * g:HEAD * (g + 1)] = _dot(wc, vn[:, HEAD * g:HEAD * (g + 1)]) + bt[:, g:g + 1]
    return u, r, xh, vn, wcs, causal, mix_s[...]


def _lane_select(lane, vals):
    return jnp.where(lane < 64, vals[0], jnp.where(lane < 128, vals[1], jnp.where(lane < 192, vals[2], vals[3])))


def _pool_fwd(pc, pp, ci):
    ext = jnp.concatenate([pp, pc], axis=0)
    s2 = ext + pltpu.roll(ext, 1, 0)
    s4 = s2 + pltpu.roll(s2, 2, 0)
    s8 = s4 + pltpu.roll(s4, 4, 0)
    s16 = s8 + pltpu.roll(s8, 8, 0)
    t1 = ci * CHUNK + lax.broadcasted_iota(jnp.int32, (CHUNK, 1), 0) + 1
    lane = lax.broadcasted_iota(jnp.int32, (1, 256), 1)
    cnt = _lane_select(lane, [jnp.minimum(t1, w).astype(F32) for w in POOL_WINDOWS])
    ssel = _lane_select(lane, [s[CHUNK:] for s in (s2, s4, s8, s16)])
    return ssel / cnt - pc, cnt, lane


def _attn_prep(zc, zpkv, cc, sc, cp, sp, ci):
    q = zc[:, 768:1280]
    kc = zc[:, 1280:1408]
    vc = zc[:, 1408:1536]
    kp = zpkv[:, :128]
    vp = zpkv[:, 128:]
    qr = q * jnp.concatenate([cc] * 4, axis=1) + _rot_half(q) * jnp.concatenate([sc] * 4, axis=1)
    krc = kc * cc + _rot_half(kc) * sc
    krp = kp * cp + _rot_half(kp) * sp
    kband = jnp.concatenate([krp, krc], axis=0)
    vband = jnp.concatenate([vp, vc], axis=0)
    key = lax.broadcasted_iota(jnp.int32, (2 * CHUNK, 4 * CHUNK), 0)
    t = lax.broadcasted_iota(jnp.int32, (2 * CHUNK, 4 * CHUNK), 1) & (CHUNK - 1)
    valid = ((key < CHUNK) & (key > t) & (ci > 0)) | ((key >= CHUNK) & (key - CHUNK <= t))
    return qr, kband, vband, valid


SCALE = HEAD ** -0.5


def _stack_heads(x, base, hk):
    return jnp.concatenate([x[:, base + HEAD * (4 * hk + i):base + HEAD * (4 * hk + i + 1)] for i in range(4)], axis=0)


def _sink_row(snk, hk):
    lane = lax.broadcasted_iota(jnp.int32, (1, 4 * CHUNK), 1)
    s = [snk[:, 4 * hk + i:4 * hk + i + 1] for i in range(4)]
    return jnp.where(lane < CHUNK, s[0], jnp.where(lane < 2 * CHUNK, s[1], jnp.where(lane < 3 * CHUNK, s[2], s[3])))


def _group_probs(kh, q4, valid, sink4):
    s = jnp.where(valid, _dot(kh, q4, NT), -1e30)
    mx = jnp.maximum(jnp.max(s, axis=0, keepdims=True), sink4)
    e = jnp.exp(s - mx)
    es = jnp.exp(sink4 - mx)
    inv = 1.0 / (jnp.sum(e, axis=0, keepdims=True) + es)
    return e * inv, es * inv


def _mixer_specs(nb, rev):
    def cur(i):
        return nb - 1 - i if rev else i

    def prev(i):
        return jnp.maximum(cur(i) - 1, 0)

    full = lambda shape: pl.BlockSpec(shape, lambda i: (0,) * len(shape))
    specs = [
        pl.BlockSpec((CHUNK, 1536), lambda i: (cur(i), 0)),
        pl.BlockSpec((CHUNK, 256), lambda i: (prev(i), 2)),
        pl.BlockSpec((CHUNK, 256), lambda i: (prev(i), 5)),
        pl.BlockSpec((CHUNK, 128), lambda i: (cur(i), 0)),
        pl.BlockSpec((CHUNK, 128), lambda i: (cur(i), 0)),
        pl.BlockSpec((CHUNK, 128), lambda i: (prev(i), 0)),
        pl.BlockSpec((CHUNK, 128), lambda i: (prev(i), 0)),
        full((1, 256)), full((4, CHUNK, CHUNK)), full((CHUNK, 4)), full((256, 256)), full((1, 256)), full((1, 8)),
        full((256, 256)),
    ]
    return specs, cur


def _ones_bd():
    g = lax.broadcasted_iota(jnp.int32, (256, 256), 0) // HEAD == lax.broadcasted_iota(jnp.int32, (256, 256), 1) // HEAD
    return g.astype(BF16)


def _mixer_fwd(name, z, cosq, sinq, gv, ws, bt, pw, psc, snk):
    s = z.shape[0]
    nb = s // CHUNK
    specs, _ = _mixer_specs(nb, False)

    def body(zc_ref, zpp_ref, zpkv_ref, cq_ref, sq_ref, cp_ref, sp_ref, gv_ref, ws_ref, bt_ref, pw_ref, psc_ref,
             snk_ref, bd_ref, o_ref, mix_s):
        ci = pl.program_id(0)
        zc = zc_ref[...]
        gel, _ = _gelu_parts(zc[:, :512])
        u, _, _, _, _, _, mixed = _gating(gel, gv_ref[...], ws_ref, bt_ref[...], bd_ref[...], mix_s)
        o_ref[:, :256] = (u * mixed).astype(BF16)
        pp = jnp.where(ci > 0, zpp_ref[...], 0.0)
        pooled, _, _ = _pool_fwd(zc[:, 512:768], pp, ci)
        mp = _dot(pooled.astype(BF16), pw_ref[...].astype(BF16))
        o_ref[:, 256:512] = (mp * psc_ref[...]).astype(BF16)
        qr, kband, vband, valid = _attn_prep(zc, zpkv_ref[...], cq_ref[...], sq_ref[...], cp_ref[...], sp_ref[...], ci)
        snkv = snk_ref[...]
        kb = kband.astype(BF16)
        vt = vband.T
        ots = []
        for hk in range(2):
            q4 = (_stack_heads(qr, 0, hk) * SCALE).astype(BF16)
            p, _ = _group_probs(kb[:, HEAD * hk:HEAD * (hk + 1)], q4, valid, _sink_row(snkv, hk))
            ots.append(_dot(vt[HEAD * hk:HEAD * (hk + 1), :].astype(BF16), p.astype(BF16)))
        o = jnp.concatenate(ots, axis=0).T
        for hk in range(2):
            for i in range(4):
                h = 4 * hk + i
                o_ref[:, 512 + HEAD * h:512 + HEAD * (h + 1)] = o[CHUNK * i:CHUNK * (i + 1),
                                                                  HEAD * hk:HEAD * (hk + 1)].astype(BF16)

    return pl.pallas_call(
        body, name=name, grid=(nb,), in_specs=specs, out_specs=pl.BlockSpec((CHUNK, 1024), lambda i: (i, 0)),
        out_shape=jax.ShapeDtypeStruct((s, 1024), BF16), scratch_shapes=[pltpu.VMEM((CHUNK, 256), F32)],
        compiler_params=_params(("parallel",)),
    )(z, z, z, cosq, sinq, cosq, sinq, gv, ws, bt, pw, psc, snk, _ones_bd())


def _mixer_bwd(name, z, dabc, cosq, sinq, gv, ws, bt, pw, psc, snk):
    s = z.shape[0]
    nb = s // CHUNK
    specs, cur = _mixer_specs(nb, True)
    specs = specs + [pl.BlockSpec((CHUNK, 1024), lambda i: (cur(i), 0))]
    full = lambda shape: pl.BlockSpec(shape, lambda i: (0,) * len(shape))
    acc_shapes = [(1, 256), (4, CHUNK, CHUNK), (CHUNK, 4), (256, 256), (1, 256), (1, 8)]

    def body(zc_ref, zpp_ref, zpkv_ref, cq_ref, sq_ref, cp_ref, sp_ref, gv_ref, ws_ref, bt_ref, pw_ref, psc_ref,
             snk_ref, bd_ref, dabc_ref, dz_ref, dgv_ref, dws_ref, dbt_ref, dpw_ref, dpsc_ref, dsnk_ref,
             cpool, ck, cv, dq_s, dkv_s, mix_s, dvn_s):
        step = pl.program_id(0)
        ci = nb - 1 - step

        @pl.when(step == 0)
        def _():
            for r in (dgv_ref, dws_ref, dbt_ref, dpw_ref, dpsc_ref, dsnk_ref, cpool, ck, cv):
                r[...] = jnp.zeros_like(r)

        zc = zc_ref[...]
        dabc = dabc_ref[...]
        zg = zc[:, :512]
        gel, th = _gelu_parts(zg)
        gp = _gelu_grad(zg, th)
        gvv = gv_ref[...]
        bd = bd_ref[...]
        u, r, xh, vn, wcs, causal, mixed = _gating(gel, gvv, ws_ref, bt_ref[...], bd, mix_s)
        da = dabc[:, :256]
        dm = da * u
        dmb = dm.astype(BF16)
        lane4 = lax.broadcasted_iota(jnp.int32, (CHUNK, 4), 1)
        dbt = jnp.zeros((CHUNK, 4), F32)
        for g in range(4):
            lo, hi = HEAD * g, HEAD * (g + 1)
            dws_ref[g] += jnp.where(causal, _dot(dmb[:, lo:hi], vn[:, lo:hi], NT), 0.0)
            dbt = dbt + jnp.where(lane4 == g, jnp.sum(dm[:, lo:hi], axis=-1, keepdims=True), 0.0)
            dvn_s[:, lo:hi] = _dot(wcs[g], dmb[:, lo:hi], TN)
        dbt_ref[...] += dbt
        dvn = dvn_s[...]
        dgv_ref[...] += jnp.sum(dvn * xh, axis=0, keepdims=True)
        dxh = dvn * gvv
        dvg = r * (dxh - xh * _group_mean(dxh * xh, bd))
        dz_ref[:, :256] = (da * mixed * gp[:, :256]).astype(BF16)
        dz_ref[:, 256:512] = (dvg * gp[:, 256:]).astype(BF16)
        pc = zc[:, 512:768]
        pp = jnp.where(ci > 0, zpp_ref[...], 0.0)
        pooled, cnt, lane = _pool_fwd(pc, pp, ci)
        pwb = pw_ref[...].astype(BF16)
        pooled_b = pooled.astype(BF16)
        mp = _dot(pooled_b, pwb)
        db = dabc[:, 256:512]
        dpsc_ref[...] += jnp.sum(db * mp, axis=0, keepdims=True)
        dmpb = (db * psc_ref[...]).astype(BF16)
        dpw_ref[...] += _dot(pooled_b, dmpb, TN)
        dpooled = _dot(dmpb, pwb, NT)
        davg = dpooled / cnt
        zero = jnp.zeros((CHUNK, 256), F32)
        d2, d4, d8, d16 = [jnp.concatenate([zero, jnp.where((lane >= 64 * k) & (lane < 64 * (k + 1)), davg, 0.0)],
                                           axis=0) for k in range(4)]
        g8 = d8 + d16 + pltpu.roll(d16, 2 * CHUNK - 8, 0)
        g4 = d4 + g8 + pltpu.roll(g8, 2 * CHUNK - 4, 0)
        g2 = d2 + g4 + pltpu.roll(g4, 2 * CHUNK - 2, 0)
        ge = g2 + pltpu.roll(g2, 2 * CHUNK - 1, 0)
        dz_ref[:, 512:768] = (ge[CHUNK:] - dpooled + cpool[...]).astype(BF16)
        cpool[...] = ge[:CHUNK]
        cc = cq_ref[...]
        sc = sq_ref[...]
        qr, kband, vband, valid = _attn_prep(zc, zpkv_ref[...], cc, sc, cp_ref[...], sp_ref[...], ci)
        snkv = snk_ref[...]
        lane8 = lax.broadcasted_iota(jnp.int32, (1, 8), 1)
        qlane = lax.broadcasted_iota(jnp.int32, (1, 4 * CHUNK), 1)
        dsnk = jnp.zeros((1, 8), F32)
        kb = kband.astype(BF16)
        vb = vband.astype(BF16)
        kt = kband.T * SCALE
        dqts = []
        for hk in range(2):
            kh = kb[:, HEAD * hk:HEAD * (hk + 1)]
            q4 = (_stack_heads(qr, 0, hk) * SCALE).astype(BF16)
            do4 = _stack_heads(dabc, 512, hk).astype(BF16)
            p, ps = _group_probs(kh, q4, valid, _sink_row(snkv, hk))
            dp = _dot(vb[:, HEAD * hk:HEAD * (hk + 1)], do4, NT)
            dd = jnp.sum(p * dp, axis=0, keepdims=True)
            dsink = -ps * dd
            for i in range(4):
                part = jnp.sum(jnp.where((qlane >= CHUNK * i) & (qlane < CHUNK * (i + 1)), dsink, 0.0),
                               axis=1, keepdims=True)
                dsnk = dsnk + jnp.where(lane8 == 4 * hk + i, part, 0.0)
            dsb = (p * (dp - dd)).astype(BF16)
            dqts.append(_dot(kt[HEAD * hk:HEAD * (hk + 1), :].astype(BF16), dsb))
            dkv_s[:, HEAD * hk:HEAD * (hk + 1)] = _dot(dsb, q4)
            dkv_s[:, 128 + HEAD * hk:128 + HEAD * (hk + 1)] = _dot(p.astype(BF16), do4)
        dq4 = jnp.concatenate(dqts, axis=0).T
        for hk in range(2):
            for i in range(4):
                h = 4 * hk + i
                dq_s[:, HEAD * h:HEAD * (h + 1)] = dq4[CHUNK * i:CHUNK * (i + 1), HEAD * hk:HEAD * (hk + 1)]
        dsnk_ref[...] += dsnk
        dqr = dq_s[...]
        dz_ref[:, 768:1280] = (dqr * jnp.concatenate([cc] * 4, axis=1)
                               + _rot_half(dqr * jnp.concatenate([sc] * 4, axis=1))).astype(BF16)
        dkv = dkv_s[...]
        dkr = dkv[CHUNK:, :128] + ck[...]
        dz_ref[:, 1280:1408] = (dkr * cc + _rot_half(dkr * sc)).astype(BF16)
        dz_ref[:, 1408:1536] = (dkv[CHUNK:, 128:] + cv[...]).astype(BF16)
        ck[...] = dkv[:CHUNK, :128]
        cv[...] = dkv[:CHUNK, 128:]

    return pl.pallas_call(
        body, name=name, grid=(nb,), in_specs=specs,
        out_specs=[pl.BlockSpec((CHUNK, 1536), lambda i: (cur(i), 0))] + [full(a) for a in acc_shapes],
        out_shape=[jax.ShapeDtypeStruct((s, 1536), BF16)] + [jax.ShapeDtypeStruct(a, F32) for a in acc_shapes],
        scratch_shapes=[pltpu.VMEM((CHUNK, 256), F32), pltpu.VMEM((CHUNK, 128), F32), pltpu.VMEM((CHUNK, 128), F32),
                        pltpu.VMEM((CHUNK, 512), F32), pltpu.VMEM((2 * CHUNK, 256), F32),
                        pltpu.VMEM((CHUNK, 256), F32), pltpu.VMEM((CHUNK, 256), F32)],
        compiler_params=_params(("arbitrary",)),
    )(z, z, z, cosq, sinq, cosq, sinq, gv, ws, bt, pw, psc, snk, _ones_bd(), dabc)


def _xattn_probs(qh, kh):
    s = _dot(qh, kh, NT) * (256 ** -0.5)
    e = jnp.exp(s - jnp.max(s, axis=-1, keepdims=True))
    return e * (1.0 / jnp.sum(e, axis=-1, keepdims=True))


def _xattn_fwd(name, q, kv, tq):
    s, d = q.shape
    mlen = kv.shape[0]

    def body(q_ref, kv_ref, o_ref):
        for h in range(4):
            lo, hi = 256 * h, 256 * (h + 1)
            p = _xattn_probs(q_ref[:, lo:hi], kv_ref[:, lo:hi])
            o_ref[:, lo:hi] = _dot(p.astype(BF16), kv_ref[:, d + lo:d + hi]).astype(BF16)

    blk = pl.BlockSpec((tq, d), lambda i: (i, 0))
    return pl.pallas_call(body, name=name, grid=(s // tq,),
                          in_specs=[blk, pl.BlockSpec((mlen, 2 * d), lambda i: (0, 0))], out_specs=blk,
                          out_shape=jax.ShapeDtypeStruct((s, d), BF16), compiler_params=_params(("parallel",)))(q, kv)


def _xattn_bwd(name, q, kv, do, tq):
    s, d = q.shape
    mlen = kv.shape[0]

    def body(q_ref, kv_ref, do_ref, dq_ref, dkv_ref):
        @pl.when(pl.program_id(0) == 0)
        def _():
            dkv_ref[...] = jnp.zeros_like(dkv_ref)

        for h in range(4):
            lo, hi = 256 * h, 256 * (h + 1)
            qh = q_ref[:, lo:hi]
            kh = kv_ref[:, lo:hi]
            vh = kv_ref[:, d + lo:d + hi]
            doh = do_ref[:, lo:hi]
            p = _xattn_probs(qh, kh)
            dp = _dot(doh, vh, NT)
            dsb = (p * (dp - jnp.sum(p * dp, axis=-1, keepdims=True)) * (256 ** -0.5)).astype(BF16)
            dq_ref[:, lo:hi] = _dot(dsb, kh).astype(BF16)
            dkv_ref[:, lo:hi] += _dot(dsb, qh, TN)
            dkv_ref[:, d + lo:d + hi] += _dot(p.astype(BF16), doh, TN)

    blk = pl.BlockSpec((tq, d), lambda i: (i, 0))
    kvb = pl.BlockSpec((mlen, 2 * d), lambda i: (0, 0))
    return pl.pallas_call(
        body, name=name, grid=(s // tq,), in_specs=[blk, kvb, blk], out_specs=[blk, kvb],
        out_shape=[jax.ShapeDtypeStruct((s, d), BF16), jax.ShapeDtypeStruct((mlen, 2 * d), F32)],
        compiler_params=_params(("arbitrary",)))(q, kv, do)


def _sigmoid(x):
    return 0.5 * (1.0 + jnp.tanh(0.5 * x))


def _ffn_up(name, h, wgu, layer, tm, tn):
    s, d = h.shape
    dff = wgu.shape[2] // 2
    nj = dff // tn

    def body(h_ref, wg_ref, wu_ref, g_ref, u_ref, a_ref):
        hv = h_ref[...]
        gate = _dot(hv, wg_ref[...])
        up = _dot(hv, wu_ref[...])
        g_ref[...] = gate.astype(BF16)
        u_ref[...] = up.astype(BF16)
        a_ref[...] = (gate * _sigmoid(gate) * up).astype(BF16)

    ob = pl.BlockSpec((tm, tn), lambda j, i: (i, j))
    sd = jax.ShapeDtypeStruct((s, dff), BF16)
    return pl.pallas_call(
        body, name=name, grid=(nj, s // tm),
        in_specs=[pl.BlockSpec((tm, d), lambda j, i: (i, 0)),
                  pl.BlockSpec((None, d, tn), lambda j, i: (layer, 0, j)),
                  pl.BlockSpec((None, d, tn), lambda j, i: (layer, 0, j + nj))],
        out_specs=[ob, ob, ob], out_shape=[sd, sd, sd], compiler_params=_params(("parallel", "parallel")),
    )(h, wgu, wgu)


def _ffn_act_bwd(name, dfn, wdown, layer, gate, up, tm):
    s, d = dfn.shape
    dff = gate.shape[1]

    def body(df_ref, wd_ref, g_ref, u_ref, o_ref):
        dact = _dot(df_ref[...], wd_ref[...], NT)
        gate = g_ref[...].astype(F32)
        sig = _sigmoid(gate)
        o_ref[:, :dff] = (dact * u_ref[...].astype(F32) * sig * (1.0 + gate * (1.0 - sig))).astype(BF16)
        o_ref[:, dff:] = (dact * gate * sig).astype(BF16)

    gb = pl.BlockSpec((tm, dff), lambda i: (i, 0))
    return pl.pallas_call(
        body, name=name, grid=(s // tm,),
        in_specs=[pl.BlockSpec((tm, d), lambda i: (i, 0)),
                  pl.BlockSpec((None, dff, d), lambda i: (layer, 0, 0)), gb, gb],
        out_specs=pl.BlockSpec((tm, 2 * dff), lambda i: (i, 0)),
        out_shape=jax.ShapeDtypeStruct((s, 2 * dff), BF16), compiler_params=_params(("parallel",)),
    )(dfn, wdown, gate, up)


def _place():
    return lax.axis_index("x"), lax.axis_index("y"), lax.axis_index("c")


def _other_chips(x, y):
    return [(1 - x, y), (x, 1 - y), (1 - x, 1 - y)]


def _region(ref, axis, chip, size):
    start = pl.multiple_of(chip * size, size)
    if axis == 1:
        return ref.at[:, pl.ds(start, size), :]
    return ref.at[:, :, pl.ds(start, size)]


ANY = pl.BlockSpec(memory_space=pl.ANY)


HBM = pl.BlockSpec(memory_space=pltpu.HBM)
SEM = pl.BlockSpec(memory_space=pltpu.SEMAPHORE)
EFFECT = pltpu.SideEffectType.DATAFLOW_SIDE_EFFECTING


def _in_hbm(a):
    return pltpu.with_memory_space_constraint(a, pltpu.HBM)


def _split_start(name, srcs, lands, ncopies, plan, after=None):
    ns, nl = len(srcs), len(lands)
    extra = [] if after is None else [after]

    def body(*refs):
        src, land = refs[:ns], refs[ns:ns + nl]
        send, recv = refs[ns + nl + len(extra)], refs[ns + nl + len(extra) + 1]
        token = refs[-1]
        x, y, c = _place()
        for k, (s_ref, d_ref, peer, _) in enumerate(plan(src, land, x, y, c)):
            pltpu.make_async_remote_copy(src_ref=s_ref, dst_ref=d_ref, send_sem=send.at[k], recv_sem=recv.at[k],
                                         device_id=peer, device_id_type=MESH).start()
        token[...] = jnp.zeros_like(token)

    ops = list(srcs) + list(lands)
    out = pl.pallas_call(
        body, name=name,
        out_shape=(pltpu.SemaphoreType.DMA((ncopies,)), pltpu.SemaphoreType.DMA((ncopies,)),
                   *[pltpu.HBM(a.shape, a.dtype) for a in ops], jax.ShapeDtypeStruct((8, 128), F32)),
        in_specs=(HBM,) * (ns + nl) + (ANY,) * len(extra),
        out_specs=(SEM, SEM) + (HBM,) * (ns + nl) + (pl.BlockSpec(memory_space=pltpu.VMEM),),
        input_output_aliases={i: 2 + i for i in range(ns + nl)},
        compiler_params=pltpu.CompilerParams(has_side_effects=EFFECT),
    )(*[_in_hbm(a) for a in ops], *extra)
    return out[0], out[1], list(out[2:2 + ns]), list(out[2 + ns:2 + ns + nl]), out[-1]


def _split_wait(name, send, recv, srcs, lands, after, plan):
    ns, nl = len(srcs), len(lands)

    def body(*refs):
        src, land = refs[:ns], refs[ns:ns + nl]
        send_ref, recv_ref = refs[ns + nl], refs[ns + nl + 1]
        x, y, c = _place()
        for k, (s_ref, _, _, got) in enumerate(plan(src, land, x, y, c)):
            cp = pltpu.make_async_remote_copy(src_ref=s_ref, dst_ref=got, send_sem=send_ref.at[k],
                                              recv_sem=recv_ref.at[k], device_id=(x, y, c), device_id_type=MESH)
            cp.wait_send()
            cp.wait_recv()

    ops = list(srcs) + list(lands)
    out = pl.pallas_call(
        body, name=name, out_shape=tuple(pltpu.HBM(a.shape, a.dtype) for a in ops),
        in_specs=(HBM,) * (ns + nl) + (SEM, SEM, ANY), out_specs=(HBM,) * (ns + nl),
        input_output_aliases={i: i for i in range(ns + nl)},
        compiler_params=pltpu.CompilerParams(has_side_effects=EFFECT),
    )(*ops, send, recv, after)
    return list(out[:ns]), list(out[ns:])


def _split_wait_start(name, send, recv, lands, after, wait_plan, ncopies, start_plan, carried=()):
    nl, nc = len(lands), len(carried)
    lands = list(lands) + list(carried)

    def body(*refs):
        land = refs[:nl]
        send_in, recv_in = refs[nl + nc], refs[nl + nc + 1]
        send_out, recv_out = refs[nl + nc + 3], refs[nl + nc + 4]
        x, y, c = _place()
        for k, (s_ref, _, _, got) in enumerate(wait_plan((), land, x, y, c)):
            cp = pltpu.make_async_remote_copy(src_ref=s_ref, dst_ref=got, send_sem=send_in.at[k],
                                              recv_sem=recv_in.at[k], device_id=(x, y, c), device_id_type=MESH)
            cp.wait_send()
            cp.wait_recv()
        for k, (s_ref, d_ref, peer, _) in enumerate(start_plan((), land, x, y, c)):
            pltpu.make_async_remote_copy(src_ref=s_ref, dst_ref=d_ref, send_sem=send_out.at[k],
                                         recv_sem=recv_out.at[k], device_id=peer, device_id_type=MESH).start()

    out = pl.pallas_call(
        body, name=name,
        out_shape=(pltpu.SemaphoreType.DMA((ncopies,)), pltpu.SemaphoreType.DMA((ncopies,)),
                   *[pltpu.HBM(a.shape, a.dtype) for a in lands]),
        in_specs=(HBM,) * (nl + nc) + (SEM, SEM, ANY), out_specs=(SEM, SEM) + (HBM,) * (nl + nc),
        input_output_aliases={i: 2 + i for i in range(nl + nc)},
        compiler_params=pltpu.CompilerParams(has_side_effects=EFFECT),
    )(*lands, send, recv, after)
    return out[0], out[1], list(out[2:2 + nl]), list(out[2 + nl:])


def _half(ref, axis, chip, size, layer, h):
    reg = _region(ref, axis, chip, size).at[pl.ds(layer, 1)]
    rows = reg.shape[1] // 2
    return reg.at[:, pl.ds(pl.multiple_of(h * rows, rows), rows), :]


def _gather_plan(axes, sizes, layer):
    def plan(src, land, x, y, c):
        me = 2 * x + y
        out = []
        for t in range(len(land)):
            mine = _half(land[t], axes[t], me, sizes[t], layer, c)
            for px, py in _other_chips(x, y):
                out.append((mine, mine, (px, py, c), _half(land[t], axes[t], 2 * px + py, sizes[t], layer, c)))
        return out
    return plan


def _forward_plan(axes, sizes, layer):
    def plan(src, land, x, y, c):
        out = []
        for t in range(len(land)):
            for px, py in _other_chips(x, y):
                got = _half(land[t], axes[t], 2 * px + py, sizes[t], layer, c)
                out.append((got, got, (x, y, 1 - c), _half(land[t], axes[t], 2 * px + py, sizes[t], layer, 1 - c)))
        return out
    return plan


def _place_own(name, w, axis, chip):
    nl, r, cs = w.shape
    tr = _rows_tile(r, cs)
    nb = r // tr
    full = (nl, 4 * r, cs) if axis == 1 else (nl, r, 4 * cs)

    def body(m_ref, w_ref, o_ref):
        o_ref[...] = w_ref[...].astype(BF16)

    if axis == 1:
        ospec = pl.BlockSpec((None, tr, cs), lambda l, i, m: (l, m[0] * nb + i, 0))
    else:
        ospec = pl.BlockSpec((None, tr, cs), lambda l, i, m: (l, i, m[0]))
    return pl.pallas_call(
        body, name=name,
        grid_spec=pltpu.PrefetchScalarGridSpec(
            num_scalar_prefetch=1, grid=(nl, nb),
            in_specs=[pl.BlockSpec((None, tr, cs), lambda l, i, m: (l, i, 0))], out_specs=ospec),
        out_shape=jax.ShapeDtypeStruct(full, BF16), compiler_params=_params(("parallel", "parallel")),
    )(chip, w)


def _scatter_plan(axes, sizes):
    def plan(src, land, x, y, c):
        out = []
        for t in range(len(src)):
            for k, (px, py) in enumerate(_other_chips(x, y)):
                out.append((_region(src[t], axes[t], 2 * px + py, sizes[t]).at[0], land[t].at[k], (px, py, c),
                            land[t].at[k]))
        return out
    return plan


def _pair_plan(src, land, x, y, c):
    return [(src[t], land[t], (x, y, 1 - c), land[t]) for t in range(len(src))]


def _chip_sum(name, g, slots, axis, chip):
    _, r, cs = slots.shape
    tr = _rows_tile(r, cs)
    nb = r // tr

    def body(m_ref, g_ref, s_ref, o_ref):
        acc = g_ref[...].astype(F32)
        for k in range(3):
            acc = acc + s_ref[k].astype(F32)
        o_ref[...] = acc.astype(BF16)

    if axis == 1:
        gspec = pl.BlockSpec((tr, cs), lambda i, m: (m[0] * nb + i, 0))
    else:
        gspec = pl.BlockSpec((tr, cs), lambda i, m: (i, m[0]))
    return pl.pallas_call(
        body, name=name,
        grid_spec=pltpu.PrefetchScalarGridSpec(
            num_scalar_prefetch=1, grid=(nb,),
            in_specs=[gspec, pl.BlockSpec((3, tr, cs), lambda i, m: (0, i, 0))],
            out_specs=pl.BlockSpec((tr, cs), lambda i, m: (i, 0))),
        out_shape=jax.ShapeDtypeStruct((r, cs), BF16), compiler_params=_params(("parallel",)),
    )(chip, g, slots)


def _pair_adamw(name, mine, theirs, w, m, v, layer, bufs):
    r, cs = mine.shape
    tr = _rows_tile(r, cs)
    c1 = 1.0 - B1 ** STEP
    c2 = 1.0 - B2 ** STEP

    def body(a_ref, b_ref, w_ref, m_ref, v_ref, _g, _d, _m, _v, g_ref, d_ref, mo_ref, vo_ref):
        gv = a_ref[...].astype(F32) + b_ref[...].astype(F32)
        mn = B1 * m_ref[...] + (1.0 - B1) * gv
        vn = B2 * v_ref[...] + (1.0 - B2) * (gv * gv)
        g_ref[...] = gv
        mo_ref[...] = mn
        vo_ref[...] = vn
        d_ref[...] = -LR * ((mn / c1) / (jnp.sqrt(vn / c2) + ADAM_EPS) + WD * w_ref[...])

    blk = pl.BlockSpec((tr, cs), lambda i: (i, 0))
    lay = pl.BlockSpec((None, tr, cs), lambda i: (layer, i, 0))
    return pl.pallas_call(
        body, name=name, grid=(r // tr,), in_specs=[blk, blk, lay, lay, lay] + [ANY] * 4, out_specs=[lay] * 4,
        out_shape=[jax.ShapeDtypeStruct(b.shape, b.dtype) for b in bufs],
        input_output_aliases={5 + i: i for i in range(4)}, compiler_params=_params(("parallel",)),
    )(mine, theirs, w, m, v, *bufs)


def _allreduce_small(p):
    rows = p.shape[0]
    half = rows // 2

    def body(p_ref, o_ref, sib, sums, send, recv):
        x, y, c = _place()
        me = 2 * x + y
        mine = pl.ds(pl.multiple_of(c * half, half), half)
        theirs = pl.ds(pl.multiple_of((1 - c) * half, half), half)
        pair = pltpu.make_async_remote_copy(src_ref=p_ref.at[theirs], dst_ref=sib, send_sem=send.at[3],
                                            recv_sem=recv.at[3], device_id=(x, y, 1 - c), device_id_type=MESH)
        pair.start()
        pair.wait()
        sums[me] = (p_ref[mine] + sib[...]).astype(BF16)
        cps = [pltpu.make_async_remote_copy(src_ref=sums.at[me], dst_ref=sums.at[me], send_sem=send.at[k],
                                            recv_sem=recv.at[k], device_id=(px, py, c), device_id_type=MESH)
               for k, (px, py) in enumerate(_other_chips(x, y))]
        for cp in cps:
            cp.start()
        for k, (px, py) in enumerate(_other_chips(x, y)):
            slot = sums.at[2 * px + py]
            pltpu.make_async_remote_copy(src_ref=slot, dst_ref=slot, send_sem=send.at[k], recv_sem=recv.at[k],
                                         device_id=(x, y, c), device_id_type=MESH).wait_recv()
        for cp in cps:
            cp.wait_send()
        acc = sums[0].astype(F32)
        for k in range(1, 4):
            acc = acc + sums[k].astype(F32)
        o_ref[mine] = acc
        back = pltpu.make_async_remote_copy(src_ref=o_ref.at[mine], dst_ref=o_ref.at[mine], send_sem=send.at[4],
                                            recv_sem=recv.at[4], device_id=(x, y, 1 - c), device_id_type=MESH)
        back.start()
        pltpu.make_async_remote_copy(src_ref=o_ref.at[theirs], dst_ref=o_ref.at[theirs], send_sem=send.at[4],
                                     recv_sem=recv.at[4], device_id=(x, y, c), device_id_type=MESH).wait_recv()
        back.wait_send()

    vm = pl.BlockSpec(memory_space=pltpu.VMEM)
    return pl.pallas_call(
        body, name="allreduce_small", in_specs=[vm], out_specs=vm, out_shape=jax.ShapeDtypeStruct(p.shape, F32),
        scratch_shapes=[pltpu.VMEM((half, 128), F32), pltpu.VMEM((4, half, 128), BF16),
                        pltpu.SemaphoreType.DMA((5,)), pltpu.SemaphoreType.DMA((5,))],
        compiler_params=pltpu.CompilerParams(vmem_limit_bytes=VMEM_LIMIT),
    )(p)


def _pack(parts):
    flat = []
    for p in parts:
        v = p.reshape(-1).astype(F32)
        flat.append(jnp.pad(v, (0, (-v.shape[0]) % 128)))
    v = jnp.concatenate(flat)
    v = jnp.pad(v, (0, (-v.shape[0]) % (512 * 128)))
    return v.reshape(-1, 128)


def _unpack(buf, like):
    out, r0 = [], 0
    for p in like:
        nelem = 1
        for s in p.shape:
            nelem *= s
        rows = -(-nelem // 128)
        blk = buf[r0:r0 + rows]
        if nelem % 128:
            blk = blk.reshape(-1)[:nelem]
        out.append(blk.reshape(p.shape))
        r0 += rows
    return out


def kernel(x, mem, positions, mem_norm_g, mix_pre_g, mix_post_g, w_in, gm_v_g, gm_w_s, gm_b_s, pool_w, pool_scale, attn_sinks, w_o, x_pre_g, x_post_g, w_xq, w_xkv, w_xo, ffn_pre_g, ffn_post_g, w_gate_up, w_down, loss_target, m_mem_norm_g, m_mix_pre_g, m_mix_post_g, m_w_in, m_gm_v_g, m_gm_w_s, m_gm_b_s, m_pool_w, m_pool_scale, m_attn_sinks, m_w_o, m_x_pre_g, m_x_post_g, m_w_xq, m_w_xkv, m_w_xo, m_ffn_pre_g, m_ffn_post_g, m_w_gate_up, m_w_down, v_mem_norm_g, v_mix_pre_g, v_mix_post_g, v_w_in, v_gm_v_g, v_gm_w_s, v_gm_b_s, v_pool_w, v_pool_scale, v_attn_sinks, v_w_o, v_x_pre_g, v_x_post_g, v_w_xq, v_w_xkv, v_w_xo, v_ffn_pre_g, v_ffn_post_g, v_w_gate_up, v_w_down):
    args = (x, mem, positions, mem_norm_g, mix_pre_g, mix_post_g, w_in, gm_v_g, gm_w_s, gm_b_s, pool_w, pool_scale, attn_sinks, w_o, x_pre_g, x_post_g, w_xq, w_xkv, w_xo, ffn_pre_g, ffn_post_g, w_gate_up, w_down)
    moms_m = (m_mem_norm_g, m_mix_pre_g, m_mix_post_g, m_w_in, m_gm_v_g, m_gm_w_s, m_gm_b_s, m_pool_w, m_pool_scale, m_attn_sinks, m_w_o, m_x_pre_g, m_x_post_g, m_w_xq, m_w_xkv, m_w_xo, m_ffn_pre_g, m_ffn_post_g, m_w_gate_up, m_w_down)
    moms_v = (v_mem_norm_g, v_mix_pre_g, v_mix_post_g, v_w_in, v_gm_v_g, v_gm_w_s, v_gm_b_s, v_pool_w, v_pool_scale, v_attn_sinks, v_w_o, v_x_pre_g, v_x_post_g, v_w_xq, v_w_xkv, v_w_xo, v_ffn_pre_g, v_ffn_post_g, v_w_gate_up, v_w_down)
    P = dict(zip(NAMES, args))
    P['loss_target'] = loss_target
    M = dict(zip(WEIGHTS, moms_m))
    V = dict(zip(WEIGHTS, moms_v))
    depth = w_in.shape[0]
    nbig = len(BIG)
    axes = [BIG_AXIS[n] for n in BIG]
    sizes = [P[n].shape[a] for n, a in zip(BIG, axes)]
    chip = (2 * lax.axis_index("x") + lax.axis_index("y")).astype(jnp.int32).reshape(1)

    groups = [['w_in'], ['w_o', 'w_xq', 'w_xkv', 'w_xo'], ['w_gate_up', 'w_down']]
    units = [(l, g) for l in range(depth) for g in groups]
    unit_of = {(l, n): i for i, (l, names) in enumerate(units) for n in names}
    ax = lambda names: [BIG_AXIS[n] for n in names]
    sz = lambda names: [P[n].shape[BIG_AXIS[n]] for n in names]

    full = {n: _place_own("place_" + n, P[n], BIG_AXIS[n], chip) for n in BIG}
    gathers = []
    prev, tok = None, None
    for i, (l, names) in enumerate(units):
        send, recv, _, land, tok = _split_start("gather_start%d" % i, [], [full[n] for n in names],
                                                3 * len(names), _gather_plan(ax(names), sz(names), l), after=prev)
        full.update(zip(names, land))
        gathers.append((send, recv))
        prev = land[0]
    P['first_dep'] = tok[:1, :1]
    forwards, gathered = {}, set()

    def forward_unit(i, after, carried=()):
        ul, unames = units[i]
        send, recv = gathers[i]
        send, recv, land, thru = _split_wait_start(
            "gather_pass%d" % i, send, recv, [full[n] for n in unames], after,
            _gather_plan(ax(unames), sz(unames), ul), 3 * len(unames), _forward_plan(ax(unames), sz(unames), ul),
            carried=[full[n] for n in carried])
        full.update(zip(unames, land))
        full.update(zip(carried, thru))
        forwards[i] = (send, recv)

    def weights_of(l, names, after):
        i = unit_of[(l, names[0])]
        if i not in gathered:
            _, unames = units[i]
            if i not in forwards:
                forward_unit(i, after)
            send, recv = forwards.pop(i)
            _, land = _split_wait("gather_wait%d" % i, send, recv, [], [full[n] for n in unames], after,
                                  _forward_plan(ax(unames), sz(unames), l))
            full.update(zip(unames, land))
            gathered.add(i)
            if len(groups) <= i + 1 < len(units):
                forward_unit(i + 1, after, carried=[n for n in unames if n not in units[i + 1][1]])
        return {n: (full[n], l) for n in names}

    outs = {n: [lax.empty(P[n].shape, F32) for _ in range(4)] for n in BIG}
    gunits = [(l, BIG) for l in range(depth - 1, 0, -1)] + [
        (0, g) for g in (['w_gate_up', 'w_down'], ['w_xq', 'w_xkv', 'w_xo'], ['w_o'], ['w_in'])]
    collected, scatters, pairs = {}, {}, {}

    def finish_scatter(i, after):
        _, names = gunits[i]
        send, recv, g_l, slots = scatters.pop(i)
        g_l, slots = _split_wait("scatter_wait%d" % i, send, recv, g_l, slots, after,
                                 _scatter_plan(ax(names), sz(names)))
        mine = [_chip_sum("chip_sum_" + n, g.reshape(g.shape[1:]), sl, BIG_AXIS[n], chip)
                for n, g, sl in zip(names, g_l, slots)]
        send, recv, mine, theirs, tok = _split_start("pair_start%d" % i, mine, [lax.empty(a.shape, BF16) for a in mine],
                                                     len(names), _pair_plan)
        pairs[i] = (send, recv, mine, theirs)
        return tok[:1, :1]

    def finish_pair(i, after):
        l, names = gunits[i]
        send, recv, mine, theirs = pairs.pop(i)
        mine, theirs = _split_wait("pair_wait%d" % i, send, recv, mine, theirs, after, _pair_plan)
        for n, a, b in zip(names, mine, theirs):
            outs[n] = _pair_adamw("adamw_" + n, a, b, P[n], M[n], V[n], l, outs[n])

    calls = {'n': 0}
    lag = 4

    def grads_of(l, g_part, after):
        collected.update({(l, n): g for n, g in g_part.items()})
        calls['n'] += 1
        now = calls['n']
        tok = jnp.zeros((1, 1), F32)
        for i, (ul, names) in enumerate(gunits):
            if ul != l or ('started', i) in collected or any((l, n) not in collected for n in names):
                continue
            collected[('started', i)] = now
            srcs = [collected[(l, n)].reshape((1,) + collected[(l, n)].shape) for n in names]
            send, recv, srcs, slots, t = _split_start("scatter_start%d" % i, srcs,
                                                      [lax.empty((3,) + P[n].shape[1:], BF16) for n in names],
                                                      3 * len(names), _scatter_plan(ax(names), sz(names)))
            scatters[i] = (send, recv, srcs, slots)
            tok = tok + t[:1, :1]
        for i in sorted(pairs):
            if collected[('summed', i)] + lag <= now:
                finish_pair(i, after)
        for i in sorted(scatters):
            if collected[('started', i)] + lag <= now:
                tok = tok + finish_scatter(i, after)
                collected[('summed', i)] = now
        return tok

    loss_part, dx, small_g = _fwd_bwd(P, weights_of, grads_of)
    loss = lax.psum(loss_part[0, 0], ("x", "y", "c"))
    grad_x = dx.reshape(x.shape)

    small_like = [P[n] for n in SMALL]
    gsum = _allreduce_small(_pack(small_g))
    dlt, mn, vn = _adamw("adamw_small", gsum, _pack(small_like), _pack([M[n] for n in SMALL]),
                         _pack([V[n] for n in SMALL]), 512)
    grads, deltas, new_m, new_v = {}, {}, {}, {}
    for name_map, buf in ((grads, gsum), (deltas, dlt), (new_m, mn), (new_v, vn)):
        for n, a in zip(SMALL, _unpack(buf, small_like)):
            name_map[n] = a

    for i in sorted(pairs):
        finish_pair(i, dlt)
    for i in sorted(scatters):
        finish_scatter(i, dlt)
    for i in sorted(pairs):
        finish_pair(i, dlt)
    for n in BIG:
        grads[n], deltas[n], new_m[n], new_v[n] = outs[n]

    return (loss, grad_x, *[grads[n] for n in WEIGHTS], *[deltas[n] for n in WEIGHTS],
            *[new_m[n] for n in WEIGHTS], *[new_v[n] for n in WEIGHTS])


def _fwd_bwd(P, weights_of, grads_of):
    (x, mem, positions, mem_norm_g, mix_pre_g, mix_post_g, w_in, gm_v_g, gm_w_s, gm_b_s, pool_w, pool_scale, attn_sinks,
     w_o, x_pre_g, x_post_g, w_xq, w_xkv, w_xo, ffn_pre_g, ffn_post_g, w_gate_up, w_down) = [P[n] for n in NAMES]
    x0 = x[0]
    s, d = x0.shape
    depth = w_in.shape[0]
    tgt = P['loss_target'][0]
    tmn = 256
    tmr = min(512, s)
    tmp = min(1024, s)
    tkw = min(2048, s)

    half = HEAD // 2
    inv = ROPE_THETA ** (-jnp.arange(half, dtype=F32) / half)
    ang = positions[0].astype(F32)[:, None] * inv
    cos, sin = jnp.cos(ang), jnp.sin(ang)
    cosq = jnp.tile(jnp.concatenate([cos, cos], axis=-1), (1, 2))
    sinq = jnp.tile(jnp.concatenate([-sin, sin], axis=-1), (1, 2))

    row = lambda a, l: a[l].reshape(1, -1)
    memn = _prenorm("mem_norm", mem[0], mem_norm_g.reshape(1, d), tmn)
    pw_bd = []
    for l in range(depth):
        bd = jnp.zeros((256, 256), F32)
        for g in range(4):
            bd = lax.dynamic_update_slice(bd, pool_w[l, g], (64 * g, 64 * g))
        pw_bd.append(bd)

    saved = []
    xc = x0
    h = _prenorm("pre_norm0", x0, row(mix_pre_g, 0) + P['first_dep'], tmn)
    for l in range(depth):
        W = weights_of(l, ['w_in'], xc)
        sv = {'x0': xc, 'h1': h}
        z, = _mm_rows("fwd_w_in", h, *W['w_in'], 'nn', tm=tmp, rows_out=[F32], epilogue=_plain_rows)
        abc = _mixer_fwd("mixer_fwd", z, cosq, sinq, row(gm_v_g, l), gm_w_s[l], gm_b_s[l].T, pw_bd[l],
                         row(pool_scale, l), row(attn_sinks, l))
        W.update(weights_of(l, ['w_o'], z))
        mix, xc, h = _mm_rows("fwd_w_o", abc, *W['w_o'], 'nn', tm=tmr, rows_in=[xc],
                              params=[row(mix_post_g, l), row(x_pre_g, l)], rows_out=[BF16, F32, BF16],
                              epilogue=_post_pre_rows)
        sv.update(z=z, abc=abc, mix=mix, x1=xc, h2=h)
        W.update(weights_of(l, ['w_xq', 'w_xkv', 'w_xo'], xc))
        q, = _mm_rows("fwd_w_xq", h, *W['w_xq'], 'nn', tm=tmp, rows_out=[BF16], epilogue=_plain_rows)
        kv = _mm_nn("fwd_w_xkv", memn, *W['w_xkv'], tm=256, tn=512, tk=d, out_dtype=BF16)
        o = _xattn_fwd("xattn_fwd", q, kv, 512)
        xo, xc, h = _mm_rows("fwd_w_xo", o, *W['w_xo'], 'nn', tm=tmr, rows_in=[xc],
                             params=[row(x_post_g, l), row(ffn_pre_g, l)], rows_out=[BF16, F32, BF16],
                             epilogue=_post_pre_rows)
        sv.update(q=q, kv=kv, o=o, xo=xo, x2=xc, h3=h)
        W.update(weights_of(l, ['w_gate_up', 'w_down'], xc))
        dff = W['w_down'][0].shape[1]
        gate, up, act = _ffn_up("ffn_up", h, *W['w_gate_up'], 512, dff // 2)
        sv.update(gate=gate, up=up, act=act)
        if l + 1 < depth:
            f, xc, h = _mm_rows("fwd_w_down", act, *W['w_down'], 'nn', tm=tmr, rows_in=[xc],
                                params=[row(ffn_post_g, l), row(mix_pre_g, l + 1)], rows_out=[BF16, F32, BF16],
                                epilogue=_post_pre_rows)
            sv.update(f=f)
        saved.append(sv)
    gs = {n: [None] * depth for n in SMALL if n != 'mem_norm_g'}
    dx, dfn, gs['ffn_post_g'][depth - 1], loss_part = _mm_rows(
        "fwd_w_down_loss", saved[-1]['act'], *W['w_down'], 'nn', tm=tmr, rows_in=[xc, tgt],
        params=[row(ffn_post_g, depth - 1)], rows_out=[F32, BF16], n_sums=2, epilogue=_make_loss_rows(d))

    dmemn = None
    tok = jnp.zeros((1, 1), F32)
    for l in reversed(range(depth)):
        sv, W, G = saved[l], weights_of(l, BIG, dx), {}
        G['w_down'] = _mm_tn("dw_down", sv['act'], dfn, tm=dff // 2, tn=d, tk=tkw)
        dgu = _ffn_act_bwd("ffn_act_bwd", dfn, *W['w_down'], sv['gate'], sv['up'], 256)
        G['w_gate_up'] = _mm_tn("dw_gate_up", sv['h3'], dgu, tm=d, tn=dff // 2, tk=tkw)
        dx, dxo, gs['ffn_pre_g'][l], gs['x_post_g'][l] = _mm_rows(
            "bwd_w_gate_up", dgu, *W['w_gate_up'], 'nt', tm=tmr, rows_in=[sv['x2'], dx, sv['xo']],
            params=[row(ffn_pre_g, l) + tok, row(x_post_g, l)], rows_out=[F32, BF16], n_sums=2, epilogue=_bwd_rows)
        tok = grads_of(l, {n: G[n] for n in ('w_gate_up', 'w_down')}, dx)
        G['w_xo'] = _mm_tn("dw_xo", sv['o'], dxo, tm=d, tn=d, tk=tkw)
        do, = _mm_rows("bwd_w_xo", dxo, *W['w_xo'], 'nt', tm=tmp, params=[jnp.zeros((1, d), F32) + tok],
                       rows_out=[BF16], epilogue=_plain_rows)
        dq, dkv = _xattn_bwd("xattn_bwd", sv['q'], sv['kv'], do, 512)
        dkv = dkv.astype(BF16)
        G['w_xkv'] = _mm_tn("dw_xkv", memn, dkv, tm=d, tn=d, tk=mem.shape[1])
        dmemn = _mm_nt("bwd_w_xkv", dkv, *W['w_xkv'], tm=mem.shape[1], tn=512, tk=2 * d, out_dtype=F32, add=dmemn)
        G['w_xq'] = _mm_tn("dw_xq", sv['h2'], dq, tm=d, tn=d, tk=tkw)
        dx, dmix, gs['x_pre_g'][l], gs['mix_post_g'][l] = _mm_rows(
            "bwd_w_xq", dq, *W['w_xq'], 'nt', tm=tmr, rows_in=[sv['x1'], dx, sv['mix']],
            params=[row(x_pre_g, l), row(mix_post_g, l)], rows_out=[F32, BF16], n_sums=2, epilogue=_bwd_rows)
        tok = grads_of(l, {n: G[n] for n in ('w_xq', 'w_xkv', 'w_xo')}, dx)
        G['w_o'] = _mm_tn("dw_o", sv['abc'], dmix, tm=d, tn=d, tk=tkw)
        dabc, = _mm_rows("bwd_w_o", dmix, *W['w_o'], 'nt', tm=tmp, params=[jnp.zeros((1, d), F32) + tok],
                         rows_out=[F32], epilogue=_plain_rows)
        tok = grads_of(l, {'w_o': G['w_o']}, dabc)
        dz, dgv, dws, dbt, dpw, dpsc, dsnk = _mixer_bwd(
            "mixer_bwd", sv['z'], dabc, cosq, sinq, row(gm_v_g, l) + tok, gm_w_s[l], gm_b_s[l].T, pw_bd[l],
            row(pool_scale, l), row(attn_sinks, l))
        gs['gm_v_g'][l] = dgv
        gs['gm_w_s'][l] = dws
        gs['gm_b_s'][l] = dbt.T
        gs['pool_w'][l] = jnp.stack([dpw[64 * g:64 * (g + 1), 64 * g:64 * (g + 1)] for g in range(4)])
        gs['pool_scale'][l] = dpsc
        gs['attn_sinks'][l] = dsnk
        G['w_in'] = _mm_tn("dw_in", sv['h1'], dz, tm=d, tn=dz.shape[1], tk=tkw)
        if l > 0:
            dx, dfn, gs['mix_pre_g'][l], gs['ffn_post_g'][l - 1] = _mm_rows(
                "bwd_w_in", dz, *W['w_in'], 'nt', tm=tmr, rows_in=[sv['x0'], dx, saved[l - 1]['f']],
                params=[row(mix_pre_g, l), row(ffn_post_g, l - 1)], rows_out=[F32, BF16], n_sums=2,
                epilogue=_bwd_rows)
        else:
            dx, gs['mix_pre_g'][l] = _mm_rows(
                "bwd_w_in_first", dz, *W['w_in'], 'nt', tm=tmr, rows_in=[sv['x0'], dx],
                params=[row(mix_pre_g, l)], rows_out=[F32], n_sums=1, epilogue=_bwd_rows_first)
        tok = grads_of(l, {'w_in': G['w_in']}, dx)
    _, dg_mem = _norm_bwd("bwd_mem_norm", mem[0], mem_norm_g.reshape(1, d) + tok, dmemn, None, BF16, tmn)
    small_g = []
    for n in SMALL:
        if n == 'mem_norm_g':
            small_g.append(dg_mem.reshape(P[n].shape))
        else:
            small_g.append(jnp.stack([a.reshape(P[n].shape[1:]) for a in gs[n]]))
    return loss_part, dx, small_g
```

```python
import functools

import jax
import jax.numpy as jnp
from jax import lax
from jax.experimental import pallas as pl
from jax.experimental.pallas import tpu as pltpu

F32 = jnp.float32
BF16 = jnp.bfloat16
EPS = 1e-6
CHUNK = 128
HEAD = 64
ROPE_THETA = 10000.0
POOL_WINDOWS = (2, 4, 8, 16)
LR, B1, B2, ADAM_EPS, WD, STEP = 0.001, 0.9, 0.999, 1e-08, 0.01, 10
MESH = pl.DeviceIdType.MESH
VMEM_LIMIT = 56 * 1024 * 1024

NAMES = ['x', 'mem', 'positions', 'mem_norm_g', 'mix_pre_g', 'mix_post_g', 'w_in', 'gm_v_g', 'gm_w_s', 'gm_b_s',
         'pool_w', 'pool_scale', 'attn_sinks', 'w_o', 'x_pre_g', 'x_post_g', 'w_xq', 'w_xkv', 'w_xo', 'ffn_pre_g',
         'ffn_post_g', 'w_gate_up', 'w_down']
WEIGHTS = NAMES[3:]
BIG = ['w_in', 'w_o', 'w_xq', 'w_xkv', 'w_xo', 'w_gate_up', 'w_down']
BIG_AXIS = {'w_in': 2, 'w_o': 1, 'w_xq': 1, 'w_xkv': 2, 'w_xo': 1, 'w_gate_up': 2, 'w_down': 1}
SMALL = [n for n in WEIGHTS if n not in BIG]

NN = (((1,), (0,)), ((), ()))
NT = (((1,), (1,)), ((), ()))
TN = (((0,), (0,)), ((), ()))


def _dot(a, b, dims=NN):
    return lax.dot_general(a, b, dims, preferred_element_type=F32)


def _params(sem):
    return pltpu.CompilerParams(dimension_semantics=sem, vmem_limit_bytes=VMEM_LIMIT)


STREAM_BLOCK_BYTES = 3 * 512 * 1024


def _rows_tile(rows, cols):
    limit = max(16, STREAM_BLOCK_BYTES // (4 * cols))
    return max(t for t in range(16, min(rows, limit) + 1, 16) if rows % t == 0)


def _mm(name, a, a_spec, b, b_spec, dims, grid, nk, out_shape, out_spec, add=None, add_spec=None):
    acc_shape = out_spec.block_shape

    def body(*refs):
        a_ref, b_ref = refs[0], refs[1]
        pos = 2
        add_ref = None
        if add is not None:
            add_ref = refs[pos]
            pos += 1
        o_ref = refs[pos]
        part = _dot(a_ref[...].astype(BF16), b_ref[...].astype(BF16), dims)
        if nk == 1:
            if add_ref is not None:
                part = part + add_ref[...]
            o_ref[...] = part.astype(o_ref.dtype)
        else:
            acc_ref = refs[pos + 1]
            k = pl.program_id(2)

            @pl.when(k == 0)
            def _():
                acc_ref[...] = part if add_ref is None else part + add_ref[...]

            @pl.when(k > 0)
            def _():
                acc_ref[...] += part

            @pl.when(k == nk - 1)
            def _():
                o_ref[...] = acc_ref[...].astype(o_ref.dtype)

    ops, specs = [a, b], [a_spec, b_spec]
    if add is not None:
        ops.append(add)
        specs.append(add_spec)
    return pl.pallas_call(
        body, name=name, grid=grid, in_specs=specs, out_specs=out_spec, out_shape=out_shape,
        scratch_shapes=[pltpu.VMEM(acc_shape, F32)] if nk > 1 else [],
        compiler_params=_params(("parallel", "parallel", "arbitrary")),
    )(*ops)


def _wspec(block, layer, fn):
    return pl.BlockSpec((None,) + block, lambda i, j, k: (layer,) + fn(i, j, k))


def _mm_nn(name, a, w, layer, *, tm, tn, tk, out_dtype):
    m, kk = a.shape
    n = w.shape[2]
    tm = min(tm, m)
    nk = kk // tk
    return _mm(name, a, pl.BlockSpec((tm, tk), lambda i, j, k: (i, k)),
               w, _wspec((tk, tn), layer, lambda i, j, k: (k, j)), NN,
               (m // tm, n // tn, nk), nk, jax.ShapeDtypeStruct((m, n), out_dtype),
               pl.BlockSpec((tm, tn), lambda i, j, k: (i, j)))


def _mm_nt(name, a, w, layer, *, tm, tn, tk, out_dtype, add=None):
    m, kk = a.shape
    n = w.shape[1]
    tm = min(tm, m)
    nk = kk // tk
    ospec = pl.BlockSpec((tm, tn), lambda i, j, k: (i, j))
    return _mm(name, a, pl.BlockSpec((tm, tk), lambda i, j, k: (i, k)),
               w, _wspec((tn, tk), layer, lambda i, j, k: (j, k)), NT,
               (m // tm, n // tn, nk), nk, jax.ShapeDtypeStruct((m, n), out_dtype), ospec,
               add=add, add_spec=ospec if add is not None else None)


def _mm_tn(name, a, b, *, tm, tn, tk):
    kk, m = a.shape
    n = b.shape[1]
    tk = min(tk, kk)
    nk = kk // tk
    return _mm(name, a, pl.BlockSpec((tk, tm), lambda i, j, k: (k, i)),
               b, pl.BlockSpec((tk, tn), lambda i, j, k: (k, j)), TN,
               (m // tm, n // tn, nk), nk, jax.ShapeDtypeStruct((m, n), BF16),
               pl.BlockSpec((tm, tn), lambda i, j, k: (i, j)))


def _mm_rows(name, a, w, layer, mode, *, tm, rows_in=(), params=(), rows_out=(), n_sums=0, epilogue):
    m, kk = a.shape
    n = w.shape[2] if mode == 'nn' else w.shape[1]
    nr, npar, no = len(rows_in), len(params), len(rows_out)

    def body(*refs):
        a_ref, w_ref = refs[0], refs[1]
        rin = refs[2:2 + nr]
        par = refs[2 + nr:2 + nr + npar]
        outs = refs[2 + nr + npar:2 + nr + npar + no]
        sums = refs[2 + nr + npar + no:2 + nr + npar + no + n_sums]
        acc = _dot(a_ref[...], w_ref[...], NN if mode == 'nn' else NT)
        res, sm = epilogue(acc, [r[...] for r in rin], [p[...] for p in par])
        for r, v in zip(outs, res):
            r[...] = v.astype(r.dtype)

        @pl.when(pl.program_id(0) == 0)
        def _():
            for r in sums:
                r[...] = jnp.zeros_like(r)

        for r, v in zip(sums, sm):
            r[...] += v

    wblock = (None, kk, n) if mode == 'nn' else (None, n, kk)
    rowblk = pl.BlockSpec((tm, n), lambda i: (i, 0))
    one = pl.BlockSpec((1, n), lambda i: (0, 0))
    return pl.pallas_call(
        body, name=name, grid=(m // tm,),
        in_specs=[pl.BlockSpec((tm, kk), lambda i: (i, 0)),
                  pl.BlockSpec(wblock, lambda i: (layer, 0, 0), pipeline_mode=pl.Buffered(1))]
                 + [rowblk] * nr + [one] * npar,
        out_specs=[rowblk] * no + [one] * n_sums,
        out_shape=[jax.ShapeDtypeStruct((m, n), dt) for dt in rows_out] +
                  [jax.ShapeDtypeStruct((1, n), F32)] * n_sums,
        compiler_params=_params(("arbitrary",)),
    )(a, w, *rows_in, *params)


def _rstd(x):
    return lax.rsqrt(jnp.mean(x * x, axis=-1, keepdims=True) + EPS)


def _norm_back(xin, g, dy):
    r = _rstd(xin)
    xh = xin * r
    dyg = dy * g
    return r * (dyg - xh * jnp.mean(dyg * xh, axis=-1, keepdims=True)), jnp.sum(dy * xh, axis=0, keepdims=True)


def _plain_rows(acc, rows, pars):
    return [acc], []


def _post_pre_rows(y, rows, pars):
    xn = rows[0] + y * _rstd(y) * pars[0]
    return [y, xn, xn * _rstd(xn) * pars[1]], []


def _make_loss_rows(d):
    def fn(y, rows, pars):
        x, tgt = rows
        err = x + y * _rstd(y) * pars[0] - tgt
        dout = err * (1.0 / d)
        dy, dg = _norm_back(y, pars[0], dout)
        lsum = 0.5 * jnp.sum(jnp.mean(err * err, axis=-1, keepdims=True), axis=0, keepdims=True)
        return [dout, dy], [dg, jnp.broadcast_to(lsum, dg.shape)]
    return fn


def _bwd_rows(dh, rows, pars):
    xin, resid, yprev = rows
    dxa, dg_pre = _norm_back(xin, pars[0], dh)
    dx = resid + dxa
    dyp, dg_post = _norm_back(yprev.astype(F32), pars[1], dx)
    return [dx, dyp], [dg_pre, dg_post]


def _bwd_rows_first(dh, rows, pars):
    xin, resid = rows
    dxa, dg_pre = _norm_back(xin, pars[0], dh)
    return [resid + dxa], [dg_pre]


def _row(d):
    return pl.BlockSpec((1, d), lambda i: (0, 0))


def _prenorm(name, x, g, tm):
    m, d = x.shape

    def body(x_ref, g_ref, o_ref):
        xv = x_ref[...]
        o_ref[...] = (xv * _rstd(xv) * g_ref[...]).astype(BF16)

    blk = pl.BlockSpec((tm, d), lambda i: (i, 0))
    return pl.pallas_call(body, name=name, grid=(m // tm,), in_specs=[blk, _row(d)], out_specs=blk,
                          out_shape=jax.ShapeDtypeStruct((m, d), BF16), compiler_params=_params(("parallel",)))(x, g)


def _norm_bwd(name, xin, g, dy, resid, out_dtype, tm):
    m, d = xin.shape

    def body(*refs):
        if resid is None:
            x_ref, g_ref, dy_ref, dx_ref, dg_ref = refs
        else:
            x_ref, g_ref, dy_ref, r_ref, dx_ref, dg_ref = refs
        xv = x_ref[...]
        r = _rstd(xv)
        xh = xv * r
        dyv = dy_ref[...].astype(F32)
        dyg = dyv * g_ref[...]
        dx = r * (dyg - xh * jnp.mean(dyg * xh, axis=-1, keepdims=True))
        if resid is not None:
            dx = dx + r_ref[...]
        dx_ref[...] = dx.astype(dx_ref.dtype)

        @pl.when(pl.program_id(0) == 0)
        def _():
            dg_ref[...] = jnp.zeros_like(dg_ref)

        dg_ref[...] += jnp.sum(dyv * xh, axis=0, keepdims=True)

    blk = pl.BlockSpec((tm, d), lambda i: (i, 0))
    ops = [xin, g, dy] + ([] if resid is None else [resid])
    specs = [blk, _row(d), blk] + ([] if resid is None else [blk])
    return pl.pallas_call(
        body, name=name, grid=(m // tm,), in_specs=specs, out_specs=[blk, _row(d)],
        out_shape=[jax.ShapeDtypeStruct((m, d), out_dtype), jax.ShapeDtypeStruct((1, d), F32)],
        compiler_params=_params(("arbitrary",)))(*ops)


def _adamw(name, g, w, m, v, tr):
    rows, cols = g.shape
    c1 = 1.0 - B1 ** STEP
    c2 = 1.0 - B2 ** STEP

    def body(g_ref, w_ref, m_ref, v_ref, d_ref, mo_ref, vo_ref):
        gv = g_ref[...]
        mn = B1 * m_ref[...] + (1.0 - B1) * gv
        vn = B2 * v_ref[...] + (1.0 - B2) * (gv * gv)
        mo_ref[...] = mn
        vo_ref[...] = vn
        d_ref[...] = -LR * ((mn / c1) / (jnp.sqrt(vn / c2) + ADAM_EPS) + WD * w_ref[...])

    blk = pl.BlockSpec((tr, cols), lambda i: (i, 0))
    sd = jax.ShapeDtypeStruct((rows, cols), F32)
    return pl.pallas_call(body, name=name, grid=(rows // tr,), in_specs=[blk] * 4, out_specs=[blk] * 3,
                          out_shape=[sd, sd, sd], compiler_params=_params(("parallel",)))(g, w, m, v)


def _gelu_parts(x):
    c = 0.7978845608028654
    t = jnp.tanh(c * (x + 0.044715 * (x * x * x)))
    return 0.5 * x * (1.0 + t), t


def _gelu_grad(x, t):
    c = 0.7978845608028654
    return 0.5 * (1.0 + t) + 0.5 * x * (1.0 - t * t) * (c * (1.0 + 3.0 * 0.044715 * x * x))


def _rot_half(x):
    ax = x.ndim - 1
    w = x.shape[ax]
    lane = lax.broadcasted_iota(jnp.int32, x.shape, ax)
    return jnp.where((lane & 63) < 32, pltpu.roll(x, w - 32, ax), pltpu.roll(x, 32, ax))


def _group_mean(x, ones_bd):
    hi = x.astype(BF16)
    lo = (x - hi.astype(F32)).astype(BF16)
    return (_dot(hi, ones_bd) + _dot(lo, ones_bd)) * (1.0 / HEAD)


def _gating(gel, gv, ws_ref, bt, ones_bd, mix_s):
    u = gel[:, :256]
    v = gel[:, 256:]
    r = lax.rsqrt(_group_mean(v * v, ones_bd) + EPS)
    xh = v * r
    vn = (xh * gv).astype(BF16)
    row = lax.broadcasted_iota(jnp.int32, (CHUNK, CHUNK), 0)
    col = lax.broadcasted_iota(jnp.int32, (CHUNK, CHUNK), 1)
    causal = col <= row
    wcs = []
    for g in range(4):
        wc = jnp.where(causal, ws_ref[g], 0.0).astype(BF16)
        wcs.append(wc)
        mix_s[:, HEAD * g:HEAD * (g + 1)] = _dot(wc, vn[:, HEAD * g:HEAD * (g + 1)]) + bt[:, g:g + 1]
    return u, r, xh, vn, wcs, causal, mix_s[...]


def _lane_select(lane, vals):
    return jnp.where(lane < 64, vals[0], jnp.where(lane < 128, vals[1], jnp.where(lane < 192, vals[2], vals[3])))


def _pool_fwd(pc, pp, ci):
    ext = jnp.concatenate([pp, pc], axis=0)
    s2 = ext + pltpu.roll(ext, 1, 0)
    s4 = s2 + pltpu.roll(s2, 2, 0)
    s8 = s4 + pltpu.roll(s4, 4, 0)
    s16 = s8 + pltpu.roll(s8, 8, 0)
    t1 = ci * CHUNK + lax.broadcasted_iota(jnp.int32, (CHUNK, 1), 0) + 1
    lane = lax.broadcasted_iota(jnp.int32, (1, 256), 1)
    cnt = _lane_select(lane, [jnp.minimum(t1, w).astype(F32) for w in POOL_WINDOWS])
    ssel = _lane_select(lane, [s[CHUNK:] for s in (s2, s4, s8, s16)])
    return ssel / cnt - pc, cnt, lane


def _attn_prep(zc, zpkv, cc, sc, cp, sp, ci):
    q = zc[:, 768:1280]
    kc = zc[:, 1280:1408]
    vc = zc[:, 1408:1536]
    kp = zpkv[:, :128]
    vp = zpkv[:, 128:]
    qr = q * jnp.concatenate([cc] * 4, axis=1) + _rot_half(q) * jnp.concatenate([sc] * 4, axis=1)
    krc = kc * cc + _rot_half(kc) * sc
    krp = kp * cp + _rot_half(kp) * sp
    kband = jnp.concatenate([krp, krc], axis=0)
    vband = jnp.concatenate([vp, vc], axis=0)
    key = lax.broadcasted_iota(jnp.int32, (2 * CHUNK, 4 * CHUNK), 0)
    t = lax.broadcasted_iota(jnp.int32, (2 * CHUNK, 4 * CHUNK), 1) & (CHUNK - 1)
    valid = ((key < CHUNK) & (key > t) & (ci > 0)) | ((key >= CHUNK) & (key - CHUNK <= t))
    return qr, kband, vband, valid


SCALE = HEAD ** -0.5


def _stack_heads(x, base, hk):
    return jnp.concatenate([x[:, base + HEAD * (4 * hk + i):base + HEAD * (4 * hk + i + 1)] for i in range(4)], axis=0)


def _sink_row(snk, hk):
    lane = lax.broadcasted_iota(jnp.int32, (1, 4 * CHUNK), 1)
    s = [snk[:, 4 * hk + i:4 * hk + i + 1] for i in range(4)]
    return jnp.where(lane < CHUNK, s[0], jnp.where(lane < 2 * CHUNK, s[1], jnp.where(lane < 3 * CHUNK, s[2], s[3])))


def _group_probs(kh, q4, valid, sink4):
    s = jnp.where(valid, _dot(kh, q4, NT), -1e30)
    mx = jnp.maximum(jnp.max(s, axis=0, keepdims=True), sink4)
    e = jnp.exp(s - mx)
    es = jnp.exp(sink4 - mx)
    inv = 1.0 / (jnp.sum(e, axis=0, keepdims=True) + es)
    return e * inv, es * inv


def _mixer_specs(nb, rev):
    def cur(i):
        return nb - 1 - i if rev else i

    def prev(i):
        return jnp.maximum(cur(i) - 1, 0)

    full = lambda shape: pl.BlockSpec(shape, lambda i: (0,) * len(shape))
    specs = [
        pl.BlockSpec((CHUNK, 1536), lambda i: (cur(i), 0)),
        pl.BlockSpec((CHUNK, 256), lambda i: (prev(i), 2)),
        pl.BlockSpec((CHUNK, 256), lambda i: (prev(i), 5)),
        pl.BlockSpec((CHUNK, 128), lambda i: (cur(i), 0)),
        pl.BlockSpec((CHUNK, 128), lambda i: (cur(i), 0)),
        pl.BlockSpec((CHUNK, 128), lambda i: (prev(i), 0)),
        pl.BlockSpec((CHUNK, 128), lambda i: (prev(i), 0)),
        full((1, 256)), full((4, CHUNK, CHUNK)), full((CHUNK, 4)), full((256, 256)), full((1, 256)), full((1, 8)),
        full((256, 256)),
    ]
    return specs, cur


def _ones_bd():
    g = lax.broadcasted_iota(jnp.int32, (256, 256), 0) // HEAD == lax.broadcasted_iota(jnp.int32, (256, 256), 1) // HEAD
    return g.astype(BF16)


def _mixer_fwd(name, z, cosq, sinq, gv, ws, bt, pw, psc, snk):
    s = z.shape[0]
    nb = s // CHUNK
    specs, _ = _mixer_specs(nb, False)

    def body(zc_ref, zpp_ref, zpkv_ref, cq_ref, sq_ref, cp_ref, sp_ref, gv_ref, ws_ref, bt_ref, pw_ref, psc_ref,
             snk_ref, bd_ref, o_ref, mix_s):
        ci = pl.program_id(0)
        zc = zc_ref[...]
        gel, _ = _gelu_parts(zc[:, :512])
        u, _, _, _, _, _, mixed = _gating(gel, gv_ref[...], ws_ref, bt_ref[...], bd_ref[...], mix_s)
        o_ref[:, :256] = (u * mixed).astype(BF16)
        pp = jnp.where(ci > 0, zpp_ref[...], 0.0)
        pooled, _, _ = _pool_fwd(zc[:, 512:768], pp, ci)
        mp = _dot(pooled.astype(BF16), pw_ref[...].astype(BF16))
        o_ref[:, 256:512] = (mp * psc_ref[...]).astype(BF16)
        qr, kband, vband, valid = _attn_prep(zc, zpkv_ref[...], cq_ref[...], sq_ref[...], cp_ref[...], sp_ref[...], ci)
        snkv = snk_ref[...]
        kb = kband.astype(BF16)
        vt = vband.T
        ots = []
        for hk in range(2):
            q4 = (_stack_heads(qr, 0, hk) * SCALE).astype(BF16)
            p, _ = _group_probs(kb[:, HEAD * hk:HEAD * (hk + 1)], q4, valid, _sink_row(snkv, hk))
            ots.append(_dot(vt[HEAD * hk:HEAD * (hk + 1), :].astype(BF16), p.astype(BF16)))
        o = jnp.concatenate(ots, axis=0).T
        for hk in range(2):
            for i in range(4):
                h = 4 * hk + i
                o_ref[:, 512 + HEAD * h:512 + HEAD * (h + 1)] = o[CHUNK * i:CHUNK * (i + 1),
                                                                  HEAD * hk:HEAD * (hk + 1)].astype(BF16)

    return pl.pallas_call(
        body, name=name, grid=(nb,), in_specs=specs, out_specs=pl.BlockSpec((CHUNK, 1024), lambda i: (i, 0)),
        out_shape=jax.ShapeDtypeStruct((s, 1024), BF16), scratch_shapes=[pltpu.VMEM((CHUNK, 256), F32)],
        compiler_params=_params(("parallel",)),
    )(z, z, z, cosq, sinq, cosq, sinq, gv, ws, bt, pw, psc, snk, _ones_bd())


def _mixer_bwd(name, z, dabc, cosq, sinq, gv, ws, bt, pw, psc, snk):
    s = z.shape[0]
    nb = s // CHUNK
    specs, cur = _mixer_specs(nb, True)
    specs = specs + [pl.BlockSpec((CHUNK, 1024), lambda i: (cur(i), 0))]
    full = lambda shape: pl.BlockSpec(shape, lambda i: (0,) * len(shape))
    acc_shapes = [(1, 256), (4, CHUNK, CHUNK), (CHUNK, 4), (256, 256), (1, 256), (1, 8)]

    def body(zc_ref, zpp_ref, zpkv_ref, cq_ref, sq_ref, cp_ref, sp_ref, gv_ref, ws_ref, bt_ref, pw_ref, psc_ref,
             snk_ref, bd_ref, dabc_ref, dz_ref, dgv_ref, dws_ref, dbt_ref, dpw_ref, dpsc_ref, dsnk_ref,
             cpool, ck, cv, dq_s, dkv_s, mix_s, dvn_s):
        step = pl.program_id(0)
        ci = nb - 1 - step

        @pl.when(step == 0)
        def _():
            for r in (dgv_ref, dws_ref, dbt_ref, dpw_ref, dpsc_ref, dsnk_ref, cpool, ck, cv):
                r[...] = jnp.zeros_like(r)

        zc = zc_ref[...]
        dabc = dabc_ref[...]
        zg = zc[:, :512]
        gel, th = _gelu_parts(zg)
        gp = _gelu_grad(zg, th)
        gvv = gv_ref[...]
        bd = bd_ref[...]
        u, r, xh, vn, wcs, causal, mixed = _gating(gel, gvv, ws_ref, bt_ref[...], bd, mix_s)
        da = dabc[:, :256]
        dm = da * u
        dmb = dm.astype(BF16)
        lane4 = lax.broadcasted_iota(jnp.int32, (CHUNK, 4), 1)
        dbt = jnp.zeros((CHUNK, 4), F32)
        for g in range(4):
            lo, hi = HEAD * g, HEAD * (g + 1)
            dws_ref[g] += jnp.where(causal, _dot(dmb[:, lo:hi], vn[:, lo:hi], NT), 0.0)
            dbt = dbt + jnp.where(lane4 == g, jnp.sum(dm[:, lo:hi], axis=-1, keepdims=True), 0.0)
            dvn_s[:, lo:hi] = _dot(wcs[g], dmb[:, lo:hi], TN)
        dbt_ref[...] += dbt
        dvn = dvn_s[...]
        dgv_ref[...] += jnp.sum(dvn * xh, axis=0, keepdims=True)
        dxh = dvn * gvv
        dvg = r * (dxh - xh * _group_mean(dxh * xh, bd))
        dz_ref[:, :256] = (da * mixed * gp[:, :256]).astype(BF16)
        dz_ref[:, 256:512] = (dvg * gp[:, 256:]).astype(BF16)
        pc = zc[:, 512:768]
        pp = jnp.where(ci > 0, zpp_ref[...], 0.0)
        pooled, cnt, lane = _pool_fwd(pc, pp, ci)
        pwb = pw_ref[...].astype(BF16)
        pooled_b = pooled.astype(BF16)
        mp = _dot(pooled_b, pwb)
        db = dabc[:, 256:512]
        dpsc_ref[...] += jnp.sum(db * mp, axis=0, keepdims=True)
        dmpb = (db * psc_ref[...]).astype(BF16)
        dpw_ref[...] += _dot(pooled_b, dmpb, TN)
        dpooled = _dot(dmpb, pwb, NT)
        davg = dpooled / cnt
        zero = jnp.zeros((CHUNK, 256), F32)
        d2, d4, d8, d16 = [jnp.concatenate([zero, jnp.where((lane >= 64 * k) & (lane < 64 * (k + 1)), davg, 0.0)],
                                           axis=0) for k in range(4)]
        g8 = d8 + d16 + pltpu.roll(d16, 2 * CHUNK - 8, 0)
        g4 = d4 + g8 + pltpu.roll(g8, 2 * CHUNK - 4, 0)
        g2 = d2 + g4 + pltpu.roll(g4, 2 * CHUNK - 2, 0)
        ge = g2 + pltpu.roll(g2, 2 * CHUNK - 1, 0)
        dz_ref[:, 512:768] = (ge[CHUNK:] - dpooled + cpool[...]).astype(BF16)
        cpool[...] = ge[:CHUNK]
        cc = cq_ref[...]
        sc = sq_ref[...]
        qr, kband, vband, valid = _attn_prep(zc, zpkv_ref[...], cc, sc, cp_ref[...], sp_ref[...], ci)
        snkv = snk_ref[...]
        lane8 = lax.broadcasted_iota(jnp.int32, (1, 8), 1)
        qlane = lax.broadcasted_iota(jnp.int32, (1, 4 * CHUNK), 1)
        dsnk = jnp.zeros((1, 8), F32)
        kb = kband.astype(BF16)
        vb = vband.astype(BF16)
        kt = kband.T * SCALE
        dqts = []
        for hk in range(2):
            kh = kb[:, HEAD * hk:HEAD * (hk + 1)]
            q4 = (_stack_heads(qr, 0, hk) * SCALE).astype(BF16)
            do4 = _stack_heads(dabc, 512, hk).astype(BF16)
            p, ps = _group_probs(kh, q4, valid, _sink_row(snkv, hk))
            dp = _dot(vb[:, HEAD * hk:HEAD * (hk + 1)], do4, NT)
            dd = jnp.sum(p * dp, axis=0, keepdims=True)
            dsink = -ps * dd
            for i in range(4):
                part = jnp.sum(jnp.where((qlane >= CHUNK * i) & (qlane < CHUNK * (i + 1)), dsink, 0.0),
                               axis=1, keepdims=True)
                dsnk = dsnk + jnp.where(lane8 == 4 * hk + i, part, 0.0)
            dsb = (p * (dp - dd)).astype(BF16)
            dqts.append(_dot(kt[HEAD * hk:HEAD * (hk + 1), :].astype(BF16), dsb))
            dkv_s[:, HEAD * hk:HEAD * (hk + 1)] = _dot(dsb, q4)
            dkv_s[:, 128 + HEAD * hk:128 + HEAD * (hk + 1)] = _dot(p.astype(BF16), do4)
        dq4 = jnp.concatenate(dqts, axis=0).T
        for hk in range(2):
            for i in range(4):
                h = 4 * hk + i
                dq_s[:, HEAD * h:HEAD * (h + 1)] = dq4[CHUNK * i:CHUNK * (i + 1), HEAD * hk:HEAD * (hk + 1)]
        dsnk_ref[...] += dsnk
        dqr = dq_s[...]
        dz_ref[:, 768:1280] = (dqr * jnp.concatenate([cc] * 4, axis=1)
                               + _rot_half(dqr * jnp.concatenate([sc] * 4, axis=1))).astype(BF16)
        dkv = dkv_s[...]
        dkr = dkv[CHUNK:, :128] + ck[...]
        dz_ref[:, 1280:1408] = (dkr * cc + _rot_half(dkr * sc)).astype(BF16)
        dz_ref[:, 1408:1536] = (dkv[CHUNK:, 128:] + cv[...]).astype(BF16)
        ck[...] = dkv[:CHUNK, :128]
        cv[...] = dkv[:CHUNK, 128:]

    return pl.pallas_call(
        body, name=name, grid=(nb,), in_specs=specs,
        out_specs=[pl.BlockSpec((CHUNK, 1536), lambda i: (cur(i), 0))] + [full(a) for a in acc_shapes],
        out_shape=[jax.ShapeDtypeStruct((s, 1536), BF16)] + [jax.ShapeDtypeStruct(a, F32) for a in acc_shapes],
        scratch_shapes=[pltpu.VMEM((CHUNK, 256), F32), pltpu.VMEM((CHUNK, 128), F32), pltpu.VMEM((CHUNK, 128), F32),
                        pltpu.VMEM((CHUNK, 512), F32), pltpu.VMEM((2 * CHUNK, 256), F32),
                        pltpu.VMEM((CHUNK, 256), F32), pltpu.VMEM((CHUNK, 256), F32)],
        compiler_params=_params(("arbitrary",)),
    )(z, z, z, cosq, sinq, cosq, sinq, gv, ws, bt, pw, psc, snk, _ones_bd(), dabc)


def _xattn_probs(qh, kh):
    s = _dot(qh, kh, NT) * (256 ** -0.5)
    e = jnp.exp(s - jnp.max(s, axis=-1, keepdims=True))
    return e * (1.0 / jnp.sum(e, axis=-1, keepdims=True))


def _xattn_fwd(name, q, kv, tq):
    s, d = q.shape
    mlen = kv.shape[0]

    def body(q_ref, kv_ref, o_ref):
        for h in range(4):
            lo, hi = 256 * h, 256 * (h + 1)
            p = _xattn_probs(q_ref[:, lo:hi], kv_ref[:, lo:hi])
            o_ref[:, lo:hi] = _dot(p.astype(BF16), kv_ref[:, d + lo:d + hi]).astype(BF16)

    blk = pl.BlockSpec((tq, d), lambda i: (i, 0))
    return pl.pallas_call(body, name=name, grid=(s // tq,),
                          in_specs=[blk, pl.BlockSpec((mlen, 2 * d), lambda i: (0, 0))], out_specs=blk,
                          out_shape=jax.ShapeDtypeStruct((s, d), BF16), compiler_params=_params(("parallel",)))(q, kv)


def _xattn_bwd(name, q, kv, do, tq):
    s, d = q.shape
    mlen = kv.shape[0]

    def body(q_ref, kv_ref, do_ref, dq_ref, dkv_ref):
        @pl.when(pl.program_id(0) == 0)
        def _():
            dkv_ref[...] = jnp.zeros_like(dkv_ref)

        for h in range(4):
            lo, hi = 256 * h, 256 * (h + 1)
            qh = q_ref[:, lo:hi]
            kh = kv_ref[:, lo:hi]
            vh = kv_ref[:, d + lo:d + hi]
            doh = do_ref[:, lo:hi]
            p = _xattn_probs(qh, kh)
            dp = _dot(doh, vh, NT)
            dsb = (p * (dp - jnp.sum(p * dp, axis=-1, keepdims=True)) * (256 ** -0.5)).astype(BF16)
            dq_ref[:, lo:hi] = _dot(dsb, kh).astype(BF16)
            dkv_ref[:, lo:hi] += _dot(dsb, qh, TN)
            dkv_ref[:, d + lo:d + hi] += _dot(p.astype(BF16), doh, TN)

    blk = pl.BlockSpec((tq, d), lambda i: (i, 0))
    kvb = pl.BlockSpec((mlen, 2 * d), lambda i: (0, 0))
    return pl.pallas_call(
        body, name=name, grid=(s // tq,), in_specs=[blk, kvb, blk], out_specs=[blk, kvb],
        out_shape=[jax.ShapeDtypeStruct((s, d), BF16), jax.ShapeDtypeStruct((mlen, 2 * d), F32)],
        compiler_params=_params(("arbitrary",)))(q, kv, do)


def _sigmoid(x):
    return 0.5 * (1.0 + jnp.tanh(0.5 * x))


def _ffn_up(name, h, wgu, layer, tm, tn):
    s, d = h.shape
    dff = wgu.shape[2] // 2
    nj = dff // tn

    def body(h_ref, wg_ref, wu_ref, g_ref, u_ref, a_ref):
        hv = h_ref[...]
        gate = _dot(hv, wg_ref[...])
        up = _dot(hv, wu_ref[...])
        g_ref[...] = gate.astype(BF16)
        u_ref[...] = up.astype(BF16)
        a_ref[...] = (gate * _sigmoid(gate) * up).astype(BF16)

    ob = pl.BlockSpec((tm, tn), lambda j, i: (i, j))
    sd = jax.ShapeDtypeStruct((s, dff), BF16)
    return pl.pallas_call(
        body, name=name, grid=(nj, s // tm),
        in_specs=[pl.BlockSpec((tm, d), lambda j, i: (i, 0)),
                  pl.BlockSpec((None, d, tn), lambda j, i: (layer, 0, j)),
                  pl.BlockSpec((None, d, tn), lambda j, i: (layer, 0, j + nj))],
        out_specs=[ob, ob, ob], out_shape=[sd, sd, sd], compiler_params=_params(("parallel", "parallel")),
    )(h, wgu, wgu)


def _ffn_act_bwd(name, dfn, wdown, layer, gate, up, tm):
    s, d = dfn.shape
    dff = gate.shape[1]

    def body(df_ref, wd_ref, g_ref, u_ref, o_ref):
        dact = _dot(df_ref[...], wd_ref[...], NT)
        gate = g_ref[...].astype(F32)
        sig = _sigmoid(gate)
        o_ref[:, :dff] = (dact * u_ref[...].astype(F32) * sig * (1.0 + gate * (1.0 - sig))).astype(BF16)
        o_ref[:, dff:] = (dact * gate * sig).astype(BF16)

    gb = pl.BlockSpec((tm, dff), lambda i: (i, 0))
    return pl.pallas_call(
        body, name=name, grid=(s // tm,),
        in_specs=[pl.BlockSpec((tm, d), lambda i: (i, 0)),
                  pl.BlockSpec((None, dff, d), lambda i: (layer, 0, 0)), gb, gb],
        out_specs=pl.BlockSpec((tm, 2 * dff), lambda i: (i, 0)),
        out_shape=jax.ShapeDtypeStruct((s, 2 * dff), BF16), compiler_params=_params(("parallel",)),
    )(dfn, wdown, gate, up)


def _place():
    return lax.axis_index("x"), lax.axis_index("y"), lax.axis_index("c")


def _other_chips(x, y):
    return [(1 - x, y), (x, 1 - y), (1 - x, 1 - y)]


def _region(ref, axis, chip, size):
    start = pl.multiple_of(chip * size, size)
    if axis == 1:
        return ref.at[:, pl.ds(start, size), :]
    return ref.at[:, :, pl.ds(start, size)]


ANY = pl.BlockSpec(memory_space=pl.ANY)


HBM = pl.BlockSpec(memory_space=pltpu.HBM)
SEM = pl.BlockSpec(memory_space=pltpu.SEMAPHORE)
EFFECT = pltpu.SideEffectType.DATAFLOW_SIDE_EFFECTING


def _in_hbm(a):
    return pltpu.with_memory_space_constraint(a, pltpu.HBM)


def _split_start(name, srcs, lands, ncopies, plan):
    ns, nl = len(srcs), len(lands)

    def body(*refs):
        src, land = refs[:ns], refs[ns:ns + nl]
        send, recv = refs[ns + nl], refs[ns + nl + 1]
        token = refs[-1]
        x, y, c = _place()
        for k, (s_ref, d_ref, peer, _) in enumerate(plan(src, land, x, y, c)):
            pltpu.make_async_remote_copy(src_ref=s_ref, dst_ref=d_ref, send_sem=send.at[k], recv_sem=recv.at[k],
                                         device_id=peer, device_id_type=MESH).start()
        token[...] = jnp.zeros_like(token)

    ops = list(srcs) + list(lands)
    out = pl.pallas_call(
        body, name=name,
        out_shape=(pltpu.SemaphoreType.DMA((ncopies,)), pltpu.SemaphoreType.DMA((ncopies,)),
                   *[pltpu.HBM(a.shape, a.dtype) for a in ops], jax.ShapeDtypeStruct((8, 128), F32)),
        in_specs=(HBM,) * (ns + nl),
        out_specs=(SEM, SEM) + (HBM,) * (ns + nl) + (pl.BlockSpec(memory_space=pltpu.VMEM),),
        input_output_aliases={i: 2 + i for i in range(ns + nl)},
        compiler_params=pltpu.CompilerParams(has_side_effects=EFFECT),
    )(*[_in_hbm(a) for a in ops])
    return out[0], out[1], list(out[2:2 + ns]), list(out[2 + ns:2 + ns + nl]), out[-1]


def _split_start_many(name, lands, jobs):
    nl, nj = len(lands), len(jobs)

    def body(*refs):
        land = refs[:nl]
        sems = refs[nl:nl + 2 * nj]
        token = refs[-1]
        x, y, c = _place()
        for j, (idx, _, plan) in enumerate(jobs):
            for k, (s_ref, d_ref, peer, _) in enumerate(plan((), [land[t] for t in idx], x, y, c)):
                pltpu.make_async_remote_copy(src_ref=s_ref, dst_ref=d_ref, send_sem=sems[2 * j].at[k],
                                             recv_sem=sems[2 * j + 1].at[k], device_id=peer,
                                             device_id_type=MESH).start()
        token[...] = jnp.zeros_like(token)

    sem_shapes = tuple(pltpu.SemaphoreType.DMA((n,)) for _, n, _ in jobs for _ in range(2))
    out = pl.pallas_call(
        body, name=name,
        out_shape=sem_shapes + tuple(pltpu.HBM(a.shape, a.dtype) for a in lands)
        + (jax.ShapeDtypeStruct((8, 128), F32),),
        in_specs=(HBM,) * nl,
        out_specs=(SEM,) * (2 * nj) + (HBM,) * nl + (pl.BlockSpec(memory_space=pltpu.VMEM),),
        input_output_aliases={i: 2 * nj + i for i in range(nl)},
        compiler_params=pltpu.CompilerParams(has_side_effects=EFFECT),
    )(*[_in_hbm(a) for a in lands])
    return [(out[2 * j], out[2 * j + 1]) for j in range(nj)], list(out[2 * nj:2 * nj + nl]), out[-1]


def _split_wait(name, send, recv, srcs, lands, after, plan):
    ns, nl = len(srcs), len(lands)

    def body(*refs):
        src, land = refs[:ns], refs[ns:ns + nl]
        send_ref, recv_ref = refs[ns + nl], refs[ns + nl + 1]
        x, y, c = _place()
        for k, (s_ref, _, _, got) in enumerate(plan(src, land, x, y, c)):
            cp = pltpu.make_async_remote_copy(src_ref=s_ref, dst_ref=got, send_sem=send_ref.at[k],
                                              recv_sem=recv_ref.at[k], device_id=(x, y, c), device_id_type=MESH)
            cp.wait_send()
            cp.wait_recv()

    ops = list(srcs) + list(lands)
    out = pl.pallas_call(
        body, name=name, out_shape=tuple(pltpu.HBM(a.shape, a.dtype) for a in ops),
        in_specs=(HBM,) * (ns + nl) + (SEM, SEM, ANY), out_specs=(HBM,) * (ns + nl),
        input_output_aliases={i: i for i in range(ns + nl)},
        compiler_params=pltpu.CompilerParams(has_side_effects=EFFECT),
    )(*ops, send, recv, after)
    return list(out[:ns]), list(out[ns:])


def _split_wait_start(name, send, recv, lands, after, wait_plan, ncopies, start_plan, carried=()):
    nl, nc = len(lands), len(carried)
    lands = list(lands) + list(carried)

    def body(*refs):
        land = refs[:nl]
        send_in, recv_in = refs[nl + nc], refs[nl + nc + 1]
        send_out, recv_out = refs[nl + nc + 3], refs[nl + nc + 4]
        x, y, c = _place()
        for k, (s_ref, _, _, got) in enumerate(wait_plan((), land, x, y, c)):
            cp = pltpu.make_async_remote_copy(src_ref=s_ref, dst_ref=got, send_sem=send_in.at[k],
                                              recv_sem=recv_in.at[k], device_id=(x, y, c), device_id_type=MESH)
            cp.wait_send()
            cp.wait_recv()
        for k, (s_ref, d_ref, peer, _) in enumerate(start_plan((), land, x, y, c)):
            pltpu.make_async_remote_copy(src_ref=s_ref, dst_ref=d_ref, send_sem=send_out.at[k],
                                         recv_sem=recv_out.at[k], device_id=peer, device_id_type=MESH).start()

    out = pl.pallas_call(
        body, name=name,
        out_shape=(pltpu.SemaphoreType.DMA((ncopies,)), pltpu.SemaphoreType.DMA((ncopies,)),
                   *[pltpu.HBM(a.shape, a.dtype) for a in lands]),
        in_specs=(HBM,) * (nl + nc) + (SEM, SEM, ANY), out_specs=(SEM, SEM) + (HBM,) * (nl + nc),
        input_output_aliases={i: 2 + i for i in range(nl + nc)},
        compiler_params=pltpu.CompilerParams(has_side_effects=EFFECT),
    )(*lands, send, recv, after)
    return out[0], out[1], list(out[2:2 + nl]), list(out[2 + nl:])


def _half(ref, axis, chip, size, layer, h):
    reg = _region(ref, axis, chip, size).at[pl.ds(layer, 1)]
    rows = reg.shape[1] // 2
    return reg.at[:, pl.ds(pl.multiple_of(h * rows, rows), rows), :]


def _gather_plan(axes, sizes, layer):
    def plan(src, land, x, y, c):
        me = 2 * x + y
        out = []
        for t in range(len(land)):
            mine = _half(land[t], axes[t], me, sizes[t], layer, c)
            for px, py in _other_chips(x, y):
                out.append((mine, mine, (px, py, c), _half(land[t], axes[t], 2 * px + py, sizes[t], layer, c)))
        return out
    return plan


def _forward_plan(axes, sizes, layer):
    def plan(src, land, x, y, c):
        out = []
        for t in range(len(land)):
            for px, py in _other_chips(x, y):
                got = _half(land[t], axes[t], 2 * px + py, sizes[t], layer, c)
                out.append((got, got, (x, y, 1 - c), _half(land[t], axes[t], 2 * px + py, sizes[t], layer, 1 - c)))
        return out
    return plan


def _place_own(name, w, axis, chip):
    nl, r, cs = w.shape
    tr = _rows_tile(r, cs)
    nb = r // tr
    full = (nl, 4 * r, cs) if axis == 1 else (nl, r, 4 * cs)

    def body(m_ref, w_ref, o_ref):
        o_ref[...] = w_ref[...].astype(BF16)

    if axis == 1:
        ospec = pl.BlockSpec((None, tr, cs), lambda l, i, m: (l, m[0] * nb + i, 0))
    else:
        ospec = pl.BlockSpec((None, tr, cs), lambda l, i, m: (l, i, m[0]))
    return pl.pallas_call(
        body, name=name,
        grid_spec=pltpu.PrefetchScalarGridSpec(
            num_scalar_prefetch=1, grid=(nl, nb),
            in_specs=[pl.BlockSpec((None, tr, cs), lambda l, i, m: (l, i, 0))], out_specs=ospec),
        out_shape=jax.ShapeDtypeStruct(full, BF16), compiler_params=_params(("parallel", "parallel")),
    )(chip, w)


def _scatter_plan(axes, sizes):
    def plan(src, land, x, y, c):
        out = []
        for t in range(len(src)):
            for k, (px, py) in enumerate(_other_chips(x, y)):
                out.append((_region(src[t], axes[t], 2 * px + py, sizes[t]).at[0], land[t].at[k], (px, py, c),
                            land[t].at[k]))
        return out
    return plan


def _pair_plan(src, land, x, y, c):
    return [(src[t], land[t], (x, y, 1 - c), land[t]) for t in range(len(src))]


UNIT_STEPS = 16


def _unit_rows(r):
    return min(t for t in range(16, r + 1, 16) if r % t == 0 and r // t <= UNIT_STEPS)


def _guarded(i, nb, steps, work):
    if nb == steps:
        work()
    else:
        pl.when(i < nb)(work)


def _unit_chip_sum(name, gs, slots, axes, chip):
    n = len(gs)
    dims = [s.shape[1:] for s in slots]
    trs = [_unit_rows(r) for r, _ in dims]
    nbs = [r // tr for (r, _), tr in zip(dims, trs)]
    steps = max(nbs)

    def body(m_ref, *refs):
        i = pl.program_id(0)
        for t in range(n):
            def work(t=t):
                acc = refs[t][...].astype(F32)
                for k in range(3):
                    acc = acc + refs[n + t][k].astype(F32)
                refs[2 * n + t][...] = acc.astype(BF16)
            _guarded(i, nbs[t], steps, work)

    gspecs, sspecs, ospecs = [], [], []
    for (r, cs), tr, nb, axis in zip(dims, trs, nbs, axes):
        if axis == 1:
            gspecs.append(pl.BlockSpec((tr, cs), lambda i, m, nb=nb: (m[0] * nb + jnp.minimum(i, nb - 1), 0)))
        else:
            gspecs.append(pl.BlockSpec((tr, cs), lambda i, m, nb=nb: (jnp.minimum(i, nb - 1), m[0])))
        sspecs.append(pl.BlockSpec((3, tr, cs), lambda i, m, nb=nb: (0, jnp.minimum(i, nb - 1), 0)))
        ospecs.append(pl.BlockSpec((tr, cs), lambda i, m, nb=nb: (jnp.minimum(i, nb - 1), 0)))
    return pl.pallas_call(
        body, name=name,
        grid_spec=pltpu.PrefetchScalarGridSpec(num_scalar_prefetch=1, grid=(steps,), in_specs=gspecs + sspecs,
                                               out_specs=ospecs),
        out_shape=[jax.ShapeDtypeStruct(d, BF16) for d in dims], compiler_params=_params(("arbitrary",)),
    )(chip, *gs, *slots)


def _unit_adamw(name, mine, theirs, ws, ms, vs, layer, bufs):
    n = len(mine)
    dims = [a.shape for a in mine]
    trs = [_unit_rows(r) for r, _ in dims]
    nbs = [r // tr for (r, _), tr in zip(dims, trs)]
    steps = max(nbs)
    c1 = 1.0 - B1 ** STEP
    c2 = 1.0 - B2 ** STEP

    def body(*refs):
        i = pl.program_id(0)
        outs = refs[9 * n:]
        for t in range(n):
            def work(t=t):
                a_ref, b_ref, w_ref, m_ref, v_ref = refs[5 * t:5 * t + 5]
                g_ref, d_ref, mo_ref, vo_ref = outs[4 * t:4 * t + 4]
                gv = a_ref[...].astype(F32) + b_ref[...].astype(F32)
                mn = B1 * m_ref[...] + (1.0 - B1) * gv
                vn = B2 * v_ref[...] + (1.0 - B2) * (gv * gv)
                g_ref[...] = gv
                mo_ref[...] = mn
                vo_ref[...] = vn
                d_ref[...] = -LR * ((mn / c1) / (jnp.sqrt(vn / c2) + ADAM_EPS) + WD * w_ref[...])
            _guarded(i, nbs[t], steps, work)

    in_specs, out_specs, ops = [], [], []
    for t, ((r, cs), tr, nb) in enumerate(zip(dims, trs, nbs)):
        blk = pl.BlockSpec((tr, cs), lambda i, nb=nb: (jnp.minimum(i, nb - 1), 0))
        lay = pl.BlockSpec((None, tr, cs), lambda i, nb=nb: (layer, jnp.minimum(i, nb - 1), 0))
        in_specs += [blk, blk, lay, lay, lay]
        out_specs += [lay] * 4
        ops += [mine[t], theirs[t], ws[t], ms[t], vs[t]]
    flat = [b for bs in bufs for b in bs]
    out = pl.pallas_call(
        body, name=name, grid=(steps,), in_specs=in_specs + [ANY] * (4 * n), out_specs=out_specs,
        out_shape=[jax.ShapeDtypeStruct(b.shape, b.dtype) for b in flat],
        input_output_aliases={5 * n + k: k for k in range(4 * n)}, compiler_params=_params(("arbitrary",)),
    )(*ops, *flat)
    return [list(out[4 * t:4 * t + 4]) for t in range(n)]


def _allreduce_small(p):
    rows = p.shape[0]
    half = rows // 2

    def body(p_ref, o_ref, sib, sums, send, recv):
        x, y, c = _place()
        me = 2 * x + y
        mine = pl.ds(pl.multiple_of(c * half, half), half)
        theirs = pl.ds(pl.multiple_of((1 - c) * half, half), half)
        pair = pltpu.make_async_remote_copy(src_ref=p_ref.at[theirs], dst_ref=sib, send_sem=send.at[3],
                                            recv_sem=recv.at[3], device_id=(x, y, 1 - c), device_id_type=MESH)
        pair.start()
        pair.wait()
        sums[me] = (p_ref[mine] + sib[...]).astype(BF16)
        cps = [pltpu.make_async_remote_copy(src_ref=sums.at[me], dst_ref=sums.at[me], send_sem=send.at[k],
                                            recv_sem=recv.at[k], device_id=(px, py, c), device_id_type=MESH)
               for k, (px, py) in enumerate(_other_chips(x, y))]
        for cp in cps:
            cp.start()
        for k, (px, py) in enumerate(_other_chips(x, y)):
            slot = sums.at[2 * px + py]
            pltpu.make_async_remote_copy(src_ref=slot, dst_ref=slot, send_sem=send.at[k], recv_sem=recv.at[k],
                                         device_id=(x, y, c), device_id_type=MESH).wait_recv()
        for cp in cps:
            cp.wait_send()
        acc = sums[0].astype(F32)
        for k in range(1, 4):
            acc = acc + sums[k].astype(F32)
        o_ref[mine] = acc
        back = pltpu.make_async_remote_copy(src_ref=o_ref.at[mine], dst_ref=o_ref.at[mine], send_sem=send.at[4],
                                            recv_sem=recv.at[4], device_id=(x, y, 1 - c), device_id_type=MESH)
        back.start()
        pltpu.make_async_remote_copy(src_ref=o_ref.at[theirs], dst_ref=o_ref.at[theirs], send_sem=send.at[4],
                                     recv_sem=recv.at[4], device_id=(x, y, c), device_id_type=MESH).wait_recv()
        back.wait_send()

    vm = pl.BlockSpec(memory_space=pltpu.VMEM)
    return pl.pallas_call(
        body, name="allreduce_small", in_specs=[vm], out_specs=vm, out_shape=jax.ShapeDtypeStruct(p.shape, F32),
        scratch_shapes=[pltpu.VMEM((half, 128), F32), pltpu.VMEM((4, half, 128), BF16),
                        pltpu.SemaphoreType.DMA((5,)), pltpu.SemaphoreType.DMA((5,))],
        compiler_params=pltpu.CompilerParams(vmem_limit_bytes=VMEM_LIMIT),
    )(p)


def _pack(parts):
    flat = []
    for p in parts:
        v = p.reshape(-1).astype(F32)
        flat.append(jnp.pad(v, (0, (-v.shape[0]) % 128)))
    v = jnp.concatenate(flat)
    v = jnp.pad(v, (0, (-v.shape[0]) % (512 * 128)))
    return v.reshape(-1, 128)


def _unpack(buf, like):
    out, r0 = [], 0
    for p in like:
        nelem = 1
        for s in p.shape:
            nelem *= s
        rows = -(-nelem // 128)
        blk = buf[r0:r0 + rows]
        if nelem % 128:
            blk = blk.reshape(-1)[:nelem]
        out.append(blk.reshape(p.shape))
        r0 += rows
    return out


def kernel(x, mem, positions, mem_norm_g, mix_pre_g, mix_post_g, w_in, gm_v_g, gm_w_s, gm_b_s, pool_w, pool_scale, attn_sinks, w_o, x_pre_g, x_post_g, w_xq, w_xkv, w_xo, ffn_pre_g, ffn_post_g, w_gate_up, w_down, loss_target, m_mem_norm_g, m_mix_pre_g, m_mix_post_g, m_w_in, m_gm_v_g, m_gm_w_s, m_gm_b_s, m_pool_w, m_pool_scale, m_attn_sinks, m_w_o, m_x_pre_g, m_x_post_g, m_w_xq, m_w_xkv, m_w_xo, m_ffn_pre_g, m_ffn_post_g, m_w_gate_up, m_w_down, v_mem_norm_g, v_mix_pre_g, v_mix_post_g, v_w_in, v_gm_v_g, v_gm_w_s, v_gm_b_s, v_pool_w, v_pool_scale, v_attn_sinks, v_w_o, v_x_pre_g, v_x_post_g, v_w_xq, v_w_xkv, v_w_xo, v_ffn_pre_g, v_ffn_post_g, v_w_gate_up, v_w_down):
    args = (x, mem, positions, mem_norm_g, mix_pre_g, mix_post_g, w_in, gm_v_g, gm_w_s, gm_b_s, pool_w, pool_scale, attn_sinks, w_o, x_pre_g, x_post_g, w_xq, w_xkv, w_xo, ffn_pre_g, ffn_post_g, w_gate_up, w_down)
    moms_m = (m_mem_norm_g, m_mix_pre_g, m_mix_post_g, m_w_in, m_gm_v_g, m_gm_w_s, m_gm_b_s, m_pool_w, m_pool_scale, m_attn_sinks, m_w_o, m_x_pre_g, m_x_post_g, m_w_xq, m_w_xkv, m_w_xo, m_ffn_pre_g, m_ffn_post_g, m_w_gate_up, m_w_down)
    moms_v = (v_mem_norm_g, v_mix_pre_g, v_mix_post_g, v_w_in, v_gm_v_g, v_gm_w_s, v_gm_b_s, v_pool_w, v_pool_scale, v_attn_sinks, v_w_o, v_x_pre_g, v_x_post_g, v_w_xq, v_w_xkv, v_w_xo, v_ffn_pre_g, v_ffn_post_g, v_w_gate_up, v_w_down)
    P = dict(zip(NAMES, args))
    P['loss_target'] = loss_target
    M = dict(zip(WEIGHTS, moms_m))
    V = dict(zip(WEIGHTS, moms_v))
    depth = w_in.shape[0]
    nbig = len(BIG)
    axes = [BIG_AXIS[n] for n in BIG]
    sizes = [P[n].shape[a] for n, a in zip(BIG, axes)]
    chip = (2 * lax.axis_index("x") + lax.axis_index("y")).astype(jnp.int32).reshape(1)

    groups = [['w_in'], ['w_o', 'w_xq', 'w_xkv', 'w_xo'], ['w_gate_up', 'w_down']]
    units = [(l, g) for l in range(depth) for g in groups]
    unit_of = {(l, n): i for i, (l, names) in enumerate(units) for n in names}
    ax = lambda names: [BIG_AXIS[n] for n in names]
    sz = lambda names: [P[n].shape[BIG_AXIS[n]] for n in names]

    full = {n: _place_own("place_" + n, P[n], BIG_AXIS[n], chip) for n in BIG}
    gathers, land, tok = _split_start_many(
        "gather_start", [full[n] for n in BIG],
        [([BIG.index(n) for n in names], 3 * len(names), _gather_plan(ax(names), sz(names), l)) for l, names in units])
    full.update(zip(BIG, land))
    P['first_dep'] = tok[:1, :1]
    forwards, gathered = {}, set()

    def forward_unit(i, after, carried=()):
        ul, unames = units[i]
        send, recv = gathers[i]
        send, recv, land, thru = _split_wait_start(
            "gather_pass%d" % i, send, recv, [full[n] for n in unames], after,
            _gather_plan(ax(unames), sz(unames), ul), 3 * len(unames), _forward_plan(ax(unames), sz(unames), ul),
            carried=[full[n] for n in carried])
        full.update(zip(unames, land))
        full.update(zip(carried, thru))
        forwards[i] = (send, recv)

    def weights_of(l, names, after):
        i = unit_of[(l, names[0])]
        if i not in gathered:
            _, unames = units[i]
            if i not in forwards:
                forward_unit(i, after)
            send, recv = forwards.pop(i)
            _, land = _split_wait("gather_wait%d" % i, send, recv, [], [full[n] for n in unames], after,
                                  _forward_plan(ax(unames), sz(unames), l))
            full.update(zip(unames, land))
            gathered.add(i)
            if len(groups) <= i + 1 < len(units):
                forward_unit(i + 1, after, carried=[n for n in unames if n not in units[i + 1][1]])
        return {n: (full[n], l) for n in names}

    outs = {n: [lax.empty(P[n].shape, F32) for _ in range(4)] for n in BIG}
    gunits = [(l, BIG) for l in range(depth - 1, 0, -1)] + [
        (0, g) for g in (['w_gate_up', 'w_down'], ['w_xq', 'w_xkv', 'w_xo'], ['w_o'], ['w_in'])]
    collected, scatters, pairs = {}, {}, {}

    def finish_scatter(i, after):
        _, names = gunits[i]
        send, recv, g_l, slots = scatters.pop(i)
        g_l, slots = _split_wait("scatter_wait%d" % i, send, recv, g_l, slots, after,
                                 _scatter_plan(ax(names), sz(names)))
        mine = _unit_chip_sum("chip_sum", [g.reshape(g.shape[1:]) for g in g_l], slots, ax(names), chip)
        send, recv, mine, theirs, tok = _split_start("pair_start%d" % i, mine, [lax.empty(a.shape, BF16) for a in mine],
                                                     len(names), _pair_plan)
        pairs[i] = (send, recv, mine, theirs)
        return tok[:1, :1]

    def finish_pair(i, after):
        l, names = gunits[i]
        send, recv, mine, theirs = pairs.pop(i)
        mine, theirs = _split_wait("pair_wait%d" % i, send, recv, mine, theirs, after, _pair_plan)
        new = _unit_adamw("adamw", mine, theirs, [P[n] for n in names], [M[n] for n in names], [V[n] for n in names],
                          l, [outs[n] for n in names])
        outs.update(zip(names, new))

    calls = {'n': 0}
    lag = 4

    def grads_of(l, g_part, after):
        collected.update({(l, n): g for n, g in g_part.items()})
        calls['n'] += 1
        now = calls['n']
        tok = jnp.zeros((1, 1), F32)
        for i, (ul, names) in enumerate(gunits):
            if ul != l or ('started', i) in collected or any((l, n) not in collected for n in names):
                continue
            collected[('started', i)] = now
            srcs = [collected[(l, n)].reshape((1,) + collected[(l, n)].shape) for n in names]
            send, recv, srcs, slots, t = _split_start("scatter_start%d" % i, srcs,
                                                      [lax.empty((3,) + P[n].shape[1:], BF16) for n in names],
                                                      3 * len(names), _scatter_plan(ax(names), sz(names)))
            scatters[i] = (send, recv, srcs, slots)
            tok = tok + t[:1, :1]
        for i in sorted(pairs):
            if collected[('summed', i)] + lag <= now:
                finish_pair(i, after)
        for i in sorted(scatters):
            if collected[('started', i)] + lag <= now:
                tok = tok + finish_scatter(i, after)
                collected[('summed', i)] = now
        return tok

    loss_part, dx, small_g = _fwd_bwd(P, weights_of, grads_of)
    loss = lax.psum(loss_part[0, 0], ("x", "y", "c"))
    grad_x = dx.reshape(x.shape)

    small_like = [P[n] for n in SMALL]
    gsum = _allreduce_small(_pack(small_g))
    dlt, mn, vn = _adamw("adamw_small", gsum, _pack(small_like), _pack([M[n] for n in SMALL]),
                         _pack([V[n] for n in SMALL]), 512)
    grads, deltas, new_m, new_v = {}, {}, {}, {}
    for name_map, buf in ((grads, gsum), (deltas, dlt), (new_m, mn), (new_v, vn)):
        for n, a in zip(SMALL, _unpack(buf, small_like)):
            name_map[n] = a

    for i in sorted(pairs):
        finish_pair(i, dlt)
    for i in sorted(scatters):
        finish_scatter(i, dlt)
    for i in sorted(pairs):
        finish_pair(i, dlt)
    for n in BIG:
        grads[n], deltas[n], new_m[n], new_v[n] = outs[n]

    return (loss, grad_x, *[grads[n] for n in WEIGHTS], *[deltas[n] for n in WEIGHTS],
            *[new_m[n] for n in WEIGHTS], *[new_v[n] for n in WEIGHTS])


def _fwd_bwd(P, weights_of, grads_of):
    (x, mem, positions, mem_norm_g, mix_pre_g, mix_post_g, w_in, gm_v_g, gm_w_s, gm_b_s, pool_w, pool_scale, attn_sinks,
     w_o, x_pre_g, x_post_g, w_xq, w_xkv, w_xo, ffn_pre_g, ffn_post_g, w_gate_up, w_down) = [P[n] for n in NAMES]
    x0 = x[0]
    s, d = x0.shape
    depth = w_in.shape[0]
    tgt = P['loss_target'][0]
    tmn = 256
    tmr = min(512, s)
    tmp = min(1024, s)
    tkw = min(2048, s)

    half = HEAD // 2
    inv = ROPE_THETA ** (-jnp.arange(half, dtype=F32) / half)
    ang = positions[0].astype(F32)[:, None] * inv
    cos, sin = jnp.cos(ang), jnp.sin(ang)
    cosq = jnp.tile(jnp.concatenate([cos, cos], axis=-1), (1, 2))
    sinq = jnp.tile(jnp.concatenate([-sin, sin], axis=-1), (1, 2))

    row = lambda a, l: a[l].reshape(1, -1)
    memn = _prenorm("mem_norm", mem[0], mem_norm_g.reshape(1, d), tmn)
    pw_bd = []
    for l in range(depth):
        bd = jnp.zeros((256, 256), F32)
        for g in range(4):
            bd = lax.dynamic_update_slice(bd, pool_w[l, g], (64 * g, 64 * g))
        pw_bd.append(bd)

    saved = []
    xc = x0
    h = _prenorm("pre_norm0", x0, row(mix_pre_g, 0) + P['first_dep'], tmn)
    for l in range(depth):
        W = weights_of(l, ['w_in'], xc)
        sv = {'x0': xc, 'h1': h}
        z, = _mm_rows("fwd_w_in", h, *W['w_in'], 'nn', tm=tmp, rows_out=[F32], epilogue=_plain_rows)
        abc = _mixer_fwd("mixer_fwd", z, cosq, sinq, row(gm_v_g, l), gm_w_s[l], gm_b_s[l].T, pw_bd[l],
                         row(pool_scale, l), row(attn_sinks, l))
        W.update(weights_of(l, ['w_o'], z))
        mix, xc, h = _mm_rows("fwd_w_o", abc, *W['w_o'], 'nn', tm=tmr, rows_in=[xc],
                              params=[row(mix_post_g, l), row(x_pre_g, l)], rows_out=[BF16, F32, BF16],
                              epilogue=_post_pre_rows)
        sv.update(z=z, abc=abc, mix=mix, x1=xc, h2=h)
        W.update(weights_of(l, ['w_xq', 'w_xkv', 'w_xo'], xc))
        q, = _mm_rows("fwd_w_xq", h, *W['w_xq'], 'nn', tm=tmp, rows_out=[BF16], epilogue=_plain_rows)
        kv = _mm_nn("fwd_w_xkv", memn, *W['w_xkv'], tm=256, tn=512, tk=d, out_dtype=BF16)
        o = _xattn_fwd("xattn_fwd", q, kv, 512)
        xo, xc, h = _mm_rows("fwd_w_xo", o, *W['w_xo'], 'nn', tm=tmr, rows_in=[xc],
                             params=[row(x_post_g, l), row(ffn_pre_g, l)], rows_out=[BF16, F32, BF16],
                             epilogue=_post_pre_rows)
        sv.update(q=q, kv=kv, o=o, xo=xo, x2=xc, h3=h)
        W.update(weights_of(l, ['w_gate_up', 'w_down'], xc))
        dff = W['w_down'][0].shape[1]
        gate, up, act = _ffn_up("ffn_up", h, *W['w_gate_up'], 512, dff // 2)
        sv.update(gate=gate, up=up, act=act)
        if l + 1 < depth:
            f, xc, h = _mm_rows("fwd_w_down", act, *W['w_down'], 'nn', tm=tmr, rows_in=[xc],
                                params=[row(ffn_post_g, l), row(mix_pre_g, l + 1)], rows_out=[BF16, F32, BF16],
                                epilogue=_post_pre_rows)
            sv.update(f=f)
        saved.append(sv)
    gs = {n: [None] * depth for n in SMALL if n != 'mem_norm_g'}
    dx, dfn, gs['ffn_post_g'][depth - 1], loss_part = _mm_rows(
        "fwd_w_down_loss", saved[-1]['act'], *W['w_down'], 'nn', tm=tmr, rows_in=[xc, tgt],
        params=[row(ffn_post_g, depth - 1)], rows_out=[F32, BF16], n_sums=2, epilogue=_make_loss_rows(d))

    dmemn = None
    tok = jnp.zeros((1, 1), F32)
    for l in reversed(range(depth)):
        sv, W, G = saved[l], weights_of(l, BIG, dx), {}
        G['w_down'] = _mm_tn("dw_down", sv['act'], dfn, tm=dff // 2, tn=d, tk=tkw)
        dgu = _ffn_act_bwd("ffn_act_bwd", dfn, *W['w_down'], sv['gate'], sv['up'], 256)
        G['w_gate_up'] = _mm_tn("dw_gate_up", sv['h3'], dgu, tm=d, tn=dff // 2, tk=tkw)
        dx, dxo, gs['ffn_pre_g'][l], gs['x_post_g'][l] = _mm_rows(
            "bwd_w_gate_up", dgu, *W['w_gate_up'], 'nt', tm=tmr, rows_in=[sv['x2'], dx, sv['xo']],
            params=[row(ffn_pre_g, l) + tok, row(x_post_g, l)], rows_out=[F32, BF16], n_sums=2, epilogue=_bwd_rows)
        tok = grads_of(l, {n: G[n] for n in ('w_gate_up', 'w_down')}, dx)
        G['w_xo'] = _mm_tn("dw_xo", sv['o'], dxo, tm=d, tn=d, tk=tkw)
        do, = _mm_rows("bwd_w_xo", dxo, *W['w_xo'], 'nt', tm=tmp, params=[jnp.zeros((1, d), F32) + tok],
                       rows_out=[BF16], epilogue=_plain_rows)
        dq, dkv = _xattn_bwd("xattn_bwd", sv['q'], sv['kv'], do, 512)
        dkv = dkv.astype(BF16)
        G['w_xkv'] = _mm_tn("dw_xkv", memn, dkv, tm=d, tn=d, tk=mem.shape[1])
        dmemn = _mm_nt("bwd_w_xkv", dkv, *W['w_xkv'], tm=mem.shape[1], tn=512, tk=2 * d, out_dtype=F32, add=dmemn)
        G['w_xq'] = _mm_tn("dw_xq", sv['h2'], dq, tm=d, tn=d, tk=tkw)
        dx, dmix, gs['x_pre_g'][l], gs['mix_post_g'][l] = _mm_rows(
            "bwd_w_xq", dq, *W['w_xq'], 'nt', tm=tmr, rows_in=[sv['x1'], dx, sv['mix']],
            params=[row(x_pre_g, l), row(mix_post_g, l)], rows_out=[F32, BF16], n_sums=2, epilogue=_bwd_rows)
        tok = grads_of(l, {n: G[n] for n in ('w_xq', 'w_xkv', 'w_xo')}, dx)
        G['w_o'] = _mm_tn("dw_o", sv['abc'], dmix, tm=d, tn=d, tk=tkw)
        dabc, = _mm_rows("bwd_w_o", dmix, *W['w_o'], 'nt', tm=tmp, params=[jnp.zeros((1, d), F32) + tok],
                         rows_out=[F32], epilogue=_plain_rows)
        tok = grads_of(l, {'w_o': G['w_o']}, dabc)
        dz, dgv, dws, dbt, dpw, dpsc, dsnk = _mixer_bwd(
            "mixer_bwd", sv['z'], dabc, cosq, sinq, row(gm_v_g, l) + tok, gm_w_s[l], gm_b_s[l].T, pw_bd[l],
            row(pool_scale, l), row(attn_sinks, l))
        gs['gm_v_g'][l] = dgv
        gs['gm_w_s'][l] = dws
        gs['gm_b_s'][l] = dbt.T
        gs['pool_w'][l] = jnp.stack([dpw[64 * g:64 * (g + 1), 64 * g:64 * (g + 1)] for g in range(4)])
        gs['pool_scale'][l] = dpsc
        gs['attn_sinks'][l] = dsnk
        G['w_in'] = _mm_tn("dw_in", sv['h1'], dz, tm=d, tn=dz.shape[1], tk=tkw)
        if l > 0:
            dx, dfn, gs['mix_pre_g'][l], gs['ffn_post_g'][l - 1] = _mm_rows(
                "bwd_w_in", dz, *W['w_in'], 'nt', tm=tmr, rows_in=[sv['x0'], dx, saved[l - 1]['f']],
                params=[row(mix_pre_g, l), row(ffn_post_g, l - 1)], rows_out=[F32, BF16], n_sums=2,
                epilogue=_bwd_rows)
        else:
            dx, gs['mix_pre_g'][l] = _mm_rows(
                "bwd_w_in_first", dz, *W['w_in'], 'nt', tm=tmr, rows_in=[sv['x0'], dx],
                params=[row(mix_pre_g, l)], rows_out=[F32], n_sums=1, epilogue=_bwd_rows_first)
        tok = grads_of(l, {'w_in': G['w_in']}, dx)
    _, dg_mem = _norm_bwd("bwd_mem_norm", mem[0], mem_norm_g.reshape(1, d) + tok, dmemn, None, BF16, tmn)
    small_g = []
    for n in SMALL:
        if n == 'mem_norm_g':
            small_g.append(dg_mem.reshape(P[n].shape))
        else:
            small_g.append(jnp.stack([a.reshape(P[n].shape[1:]) for a in gs[n]]))
    return loss_part, dx, small_g
```

```python
import functools

import jax
import jax.numpy as jnp
from jax import lax
from jax.experimental import pallas as pl
from jax.experimental.pallas import tpu as pltpu

F32 = jnp.float32
BF16 = jnp.bfloat16
EPS = 1e-6
CHUNK = 128
HEAD = 64
ROPE_THETA = 10000.0
POOL_WINDOWS = (2, 4, 8, 16)
LR, B1, B2, ADAM_EPS, WD, STEP = 0.001, 0.9, 0.999, 1e-08, 0.01, 10
MESH = pl.DeviceIdType.MESH
VMEM_LIMIT = 56 * 1024 * 1024

NAMES = ['x', 'mem', 'positions', 'mem_norm_g', 'mix_pre_g', 'mix_post_g', 'w_in', 'gm_v_g', 'gm_w_s', 'gm_b_s',
         'pool_w', 'pool_scale', 'attn_sinks', 'w_o', 'x_pre_g', 'x_post_g', 'w_xq', 'w_xkv', 'w_xo', 'ffn_pre_g',
         'ffn_post_g', 'w_gate_up', 'w_down']
WEIGHTS = NAMES[3:]
BIG = ['w_in', 'w_o', 'w_xq', 'w_xkv', 'w_xo', 'w_gate_up', 'w_down']
BIG_AXIS = {'w_in': 2, 'w_o': 1, 'w_xq': 1, 'w_xkv': 2, 'w_xo': 1, 'w_gate_up': 2, 'w_down': 1}
SMALL = [n for n in WEIGHTS if n not in BIG]

NN = (((1,), (0,)), ((), ()))
NT = (((1,), (1,)), ((), ()))
TN = (((0,), (0,)), ((), ()))


def _dot(a, b, dims=NN):
    return lax.dot_general(a, b, dims, preferred_element_type=F32)


def _params(sem):
    return pltpu.CompilerParams(dimension_semantics=sem, vmem_limit_bytes=VMEM_LIMIT)


STREAM_BLOCK_BYTES = 3 * 512 * 1024


def _rows_tile(rows, cols):
    limit = max(16, STREAM_BLOCK_BYTES // (4 * cols))
    return max(t for t in range(16, min(rows, limit) + 1, 16) if rows % t == 0)


def _mm(name, a, a_spec, b, b_spec, dims, grid, nk, out_shape, out_spec, add=None, add_spec=None):
    acc_shape = out_spec.block_shape

    def body(*refs):
        a_ref, b_ref = refs[0], refs[1]
        pos = 2
        add_ref = None
        if add is not None:
            add_ref = refs[pos]
            pos += 1
        o_ref = refs[pos]
        part = _dot(a_ref[...].astype(BF16), b_ref[...].astype(BF16), dims)
        if nk == 1:
            if add_ref is not None:
                part = part + add_ref[...]
            o_ref[...] = part.astype(o_ref.dtype)
        else:
            acc_ref = refs[pos + 1]
            k = pl.program_id(2)

            @pl.when(k == 0)
            def _():
                acc_ref[...] = part if add_ref is None else part + add_ref[...]

            @pl.when(k > 0)
            def _():
                acc_ref[...] += part

            @pl.when(k == nk - 1)
            def _():
                o_ref[...] = acc_ref[...].astype(o_ref.dtype)

    ops, specs = [a, b], [a_spec, b_spec]
    if add is not None:
        ops.append(add)
        specs.append(add_spec)
    return pl.pallas_call(
        body, name=name, grid=grid, in_specs=specs, out_specs=out_spec, out_shape=out_shape,
        scratch_shapes=[pltpu.VMEM(acc_shape, F32)] if nk > 1 else [],
        compiler_params=_params(("parallel", "parallel", "arbitrary")),
    )(*ops)


def _wspec(block, layer, fn):
    return pl.BlockSpec((None,) + block, lambda i, j, k: (layer,) + fn(i, j, k))


def _mm_nn(name, a, w, layer, *, tm, tn, tk, out_dtype):
    m, kk = a.shape
    n = w.shape[2]
    tm = min(tm, m)
    nk = kk // tk
    return _mm(name, a, pl.BlockSpec((tm, tk), lambda i, j, k: (i, k)),
               w, _wspec((tk, tn), layer, lambda i, j, k: (k, j)), NN,
               (m // tm, n // tn, nk), nk, jax.ShapeDtypeStruct((m, n), out_dtype),
               pl.BlockSpec((tm, tn), lambda i, j, k: (i, j)))


def _mm_nt(name, a, w, layer, *, tm, tn, tk, out_dtype, add=None):
    m, kk = a.shape
    n = w.shape[1]
    tm = min(tm, m)
    nk = kk // tk
    ospec = pl.BlockSpec((tm, tn), lambda i, j, k: (i, j))
    return _mm(name, a, pl.BlockSpec((tm, tk), lambda i, j, k: (i, k)),
               w, _wspec((tn, tk), layer, lambda i, j, k: (j, k)), NT,
               (m // tm, n // tn, nk), nk, jax.ShapeDtypeStruct((m, n), out_dtype), ospec,
               add=add, add_spec=ospec if add is not None else None)


def _mm_tn(name, a, b, *, tm, tn, tk):
    kk, m = a.shape
    n = b.shape[1]
    tk = min(tk, kk)
    nk = kk // tk
    return _mm(name, a, pl.BlockSpec((tk, tm), lambda i, j, k: (k, i)),
               b, pl.BlockSpec((tk, tn), lambda i, j, k: (k, j)), TN,
               (m // tm, n // tn, nk), nk, jax.ShapeDtypeStruct((m, n), BF16),
               pl.BlockSpec((tm, tn), lambda i, j, k: (i, j)))


def _mm_rows(name, a, w, layer, mode, *, tm, rows_in=(), params=(), rows_out=(), n_sums=0, epilogue):
    m, kk = a.shape
    n = w.shape[2] if mode == 'nn' else w.shape[1]
    nr, npar, no = len(rows_in), len(params), len(rows_out)

    def body(*refs):
        a_ref, w_ref = refs[0], refs[1]
        rin = refs[2:2 + nr]
        par = refs[2 + nr:2 + nr + npar]
        outs = refs[2 + nr + npar:2 + nr + npar + no]
        sums = refs[2 + nr + npar + no:2 + nr + npar + no + n_sums]
        acc = _dot(a_ref[...], w_ref[...], NN if mode == 'nn' else NT)
        res, sm = epilogue(acc, [r[...] for r in rin], [p[...] for p in par])
        for r, v in zip(outs, res):
            r[...] = v.astype(r.dtype)

        @pl.when(pl.program_id(0) == 0)
        def _():
            for r in sums:
                r[...] = jnp.zeros_like(r)

        for r, v in zip(sums, sm):
            r[...] += v

    wblock = (None, kk, n) if mode == 'nn' else (None, n, kk)
    rowblk = pl.BlockSpec((tm, n), lambda i: (i, 0))
    one = pl.BlockSpec((1, n), lambda i: (0, 0))
    return pl.pallas_call(
        body, name=name, grid=(m // tm,),
        in_specs=[pl.BlockSpec((tm, kk), lambda i: (i, 0)),
                  pl.BlockSpec(wblock, lambda i: (layer, 0, 0), pipeline_mode=pl.Buffered(1))]
                 + [rowblk] * nr + [one] * npar,
        out_specs=[rowblk] * no + [one] * n_sums,
        out_shape=[jax.ShapeDtypeStruct((m, n), dt) for dt in rows_out] +
                  [jax.ShapeDtypeStruct((1, n), F32)] * n_sums,
        compiler_params=_params(("arbitrary",)),
    )(a, w, *rows_in, *params)


def _rstd(x):
    return lax.rsqrt(jnp.mean(x * x, axis=-1, keepdims=True) + EPS)


def _norm_back(xin, g, dy):
    r = _rstd(xin)
    xh = xin * r
    dyg = dy * g
    return r * (dyg - xh * jnp.mean(dyg * xh, axis=-1, keepdims=True)), jnp.sum(dy * xh, axis=0, keepdims=True)


def _plain_rows(acc, rows, pars):
    return [acc], []


def _post_pre_rows(y, rows, pars):
    xn = rows[0] + y * _rstd(y) * pars[0]
    return [y, xn, xn * _rstd(xn) * pars[1]], []


def _make_loss_rows(d):
    def fn(y, rows, pars):
        x, tgt = rows
        err = x + y * _rstd(y) * pars[0] - tgt
        dout = err * (1.0 / d)
        dy, dg = _norm_back(y, pars[0], dout)
        lsum = 0.5 * jnp.sum(jnp.mean(err * err, axis=-1, keepdims=True), axis=0, keepdims=True)
        return [dout, dy], [dg, jnp.broadcast_to(lsum, dg.shape)]
    return fn


def _bwd_rows(dh, rows, pars):
    xin, resid, yprev = rows
    dxa, dg_pre = _norm_back(xin, pars[0], dh)
    dx = resid + dxa
    dyp, dg_post = _norm_back(yprev.astype(F32), pars[1], dx)
    return [dx, dyp], [dg_pre, dg_post]


def _bwd_rows_first(dh, rows, pars):
    xin, resid = rows
    dxa, dg_pre = _norm_back(xin, pars[0], dh)
    return [resid + dxa], [dg_pre]


def _row(d):
    return pl.BlockSpec((1, d), lambda i: (0, 0))


def _prenorm(name, x, g, tm):
    m, d = x.shape

    def body(x_ref, g_ref, o_ref):
        xv = x_ref[...]
        o_ref[...] = (xv * _rstd(xv) * g_ref[...]).astype(BF16)

    blk = pl.BlockSpec((tm, d), lambda i: (i, 0))
    return pl.pallas_call(body, name=name, grid=(m // tm,), in_specs=[blk, _row(d)], out_specs=blk,
                          out_shape=jax.ShapeDtypeStruct((m, d), BF16), compiler_params=_params(("parallel",)))(x, g)


def _norm_bwd(name, xin, g, dy, resid, out_dtype, tm):
    m, d = xin.shape

    def body(*refs):
        if resid is None:
            x_ref, g_ref, dy_ref, dx_ref, dg_ref = refs
        else:
            x_ref, g_ref, dy_ref, r_ref, dx_ref, dg_ref = refs
        xv = x_ref[...]
        r = _rstd(xv)
        xh = xv * r
        dyv = dy_ref[...].astype(F32)
        dyg = dyv * g_ref[...]
        dx = r * (dyg - xh * jnp.mean(dyg * xh, axis=-1, keepdims=True))
        if resid is not None:
            dx = dx + r_ref[...]
        dx_ref[...] = dx.astype(dx_ref.dtype)

        @pl.when(pl.program_id(0) == 0)
        def _():
            dg_ref[...] = jnp.zeros_like(dg_ref)

        dg_ref[...] += jnp.sum(dyv * xh, axis=0, keepdims=True)

    blk = pl.BlockSpec((tm, d), lambda i: (i, 0))
    ops = [xin, g, dy] + ([] if resid is None else [resid])
    specs = [blk, _row(d), blk] + ([] if resid is None else [blk])
    return pl.pallas_call(
        body, name=name, grid=(m // tm,), in_specs=specs, out_specs=[blk, _row(d)],
        out_shape=[jax.ShapeDtypeStruct((m, d), out_dtype), jax.ShapeDtypeStruct((1, d), F32)],
        compiler_params=_params(("arbitrary",)))(*ops)


def _gelu_parts(x):
    c = 0.7978845608028654
    t = jnp.tanh(c * (x + 0.044715 * (x * x * x)))
    return 0.5 * x * (1.0 + t), t


def _gelu_grad(x, t):
    c = 0.7978845608028654
    return 0.5 * (1.0 + t) + 0.5 * x * (1.0 - t * t) * (c * (1.0 + 3.0 * 0.044715 * x * x))


def _rot_half(x):
    ax = x.ndim - 1
    w = x.shape[ax]
    lane = lax.broadcasted_iota(jnp.int32, x.shape, ax)
    return jnp.where((lane & 63) < 32, pltpu.roll(x, w - 32, ax), pltpu.roll(x, 32, ax))


def _group_mean(x, ones_bd):
    hi = x.astype(BF16)
    lo = (x - hi.astype(F32)).astype(BF16)
    return (_dot(hi, ones_bd) + _dot(lo, ones_bd)) * (1.0 / HEAD)


def _gating(gel, gv, ws_ref, bt, ones_bd, mix_s):
    u = gel[:, :256]
    v = gel[:, 256:]
    r = lax.rsqrt(_group_mean(v * v, ones_bd) + EPS)
    xh = v * r
    vn = (xh * gv).astype(BF16)
    row = lax.broadcasted_iota(jnp.int32, (CHUNK, CHUNK), 0)
    col = lax.broadcasted_iota(jnp.int32, (CHUNK, CHUNK), 1)
    causal = col <= row
    wcs = []
    for g in range(4):
        wc = jnp.where(causal, ws_ref[g], 0.0).astype(BF16)
        wcs.append(wc)
        mix_s[:, HEAD * g:HEAD * (g + 1)] = _dot(wc, vn[:, HEAD * g:HEAD * (g + 1)]) + bt[:, g:g + 1]
    return u, r, xh, vn, wcs, causal, mix_s[...]


def _lane_select(lane, vals):
    return jnp.where(lane < 64, vals[0], jnp.where(lane < 128, vals[1], jnp.where(lane < 192, vals[2], vals[3])))


def _pool_fwd(pc, pp, ci):
    ext = jnp.concatenate([pp, pc], axis=0)
    s2 = ext + pltpu.roll(ext, 1, 0)
    s4 = s2 + pltpu.roll(s2, 2, 0)
    s8 = s4 + pltpu.roll(s4, 4, 0)
    s16 = s8 + pltpu.roll(s8, 8, 0)
    t1 = ci * CHUNK + lax.broadcasted_iota(jnp.int32, (CHUNK, 1), 0) + 1
    lane = lax.broadcasted_iota(jnp.int32, (1, 256), 1)
    cnt = _lane_select(lane, [jnp.minimum(t1, w).astype(F32) for w in POOL_WINDOWS])
    ssel = _lane_select(lane, [s[CHUNK:] for s in (s2, s4, s8, s16)])
    return ssel / cnt - pc, cnt, lane


def _attn_prep(zc, zpkv, cc, sc, cp, sp, ci):
    q = zc[:, 768:1280]
    kc = zc[:, 1280:1408]
    vc = zc[:, 1408:1536]
    kp = zpkv[:, :128]
    vp = zpkv[:, 128:]
    qr = q * jnp.concatenate([cc] * 4, axis=1) + _rot_half(q) * jnp.concatenate([sc] * 4, axis=1)
    krc = kc * cc + _rot_half(kc) * sc
    krp = kp * cp + _rot_half(kp) * sp
    kband = jnp.concatenate([krp, krc], axis=0)
    vband = jnp.concatenate([vp, vc], axis=0)
    key = lax.broadcasted_iota(jnp.int32, (2 * CHUNK, 4 * CHUNK), 0)
    t = lax.broadcasted_iota(jnp.int32, (2 * CHUNK, 4 * CHUNK), 1) & (CHUNK - 1)
    valid = ((key < CHUNK) & (key > t) & (ci > 0)) | ((key >= CHUNK) & (key - CHUNK <= t))
    return qr, kband, vband, valid


SCALE = HEAD ** -0.5


def _stack_heads(x, base, hk):
    return jnp.concatenate([x[:, base + HEAD * (4 * hk + i):base + HEAD * (4 * hk + i + 1)] for i in range(4)], axis=0)


def _sink_row(snk, hk):
    lane = lax.broadcasted_iota(jnp.int32, (1, 4 * CHUNK), 1)
    s = [snk[:, 4 * hk + i:4 * hk + i + 1] for i in range(4)]
    return jnp.where(lane < CHUNK, s[0], jnp.where(lane < 2 * CHUNK, s[1], jnp.where(lane < 3 * CHUNK, s[2], s[3])))


def _group_probs(kh, q4, valid, sink4):
    s = jnp.where(valid, _dot(kh, q4, NT), -1e30)
    mx = jnp.maximum(jnp.max(s, axis=0, keepdims=True), sink4)
    e = jnp.exp(s - mx)
    es = jnp.exp(sink4 - mx)
    inv = 1.0 / (jnp.sum(e, axis=0, keepdims=True) + es)
    return e * inv, es * inv


def _mixer_specs(nb, rev):
    def cur(i):
        return nb - 1 - i if rev else i

    def prev(i):
        return jnp.maximum(cur(i) - 1, 0)

    full = lambda shape: pl.BlockSpec(shape, lambda i: (0,) * len(shape))
    specs = [
        pl.BlockSpec((CHUNK, 1536), lambda i: (cur(i), 0)),
        pl.BlockSpec((CHUNK, 256), lambda i: (prev(i), 2)),
        pl.BlockSpec((CHUNK, 256), lambda i: (prev(i), 5)),
        pl.BlockSpec((CHUNK, 128), lambda i: (cur(i), 0)),
        pl.BlockSpec((CHUNK, 128), lambda i: (cur(i), 0)),
        pl.BlockSpec((CHUNK, 128), lambda i: (prev(i), 0)),
        pl.BlockSpec((CHUNK, 128), lambda i: (prev(i), 0)),
        full((1, 256)), full((4, CHUNK, CHUNK)), full((CHUNK, 4)), full((256, 256)), full((1, 256)), full((1, 8)),
        full((256, 256)),
    ]
    return specs, cur


def _ones_bd():
    g = lax.broadcasted_iota(jnp.int32, (256, 256), 0) // HEAD == lax.broadcasted_iota(jnp.int32, (256, 256), 1) // HEAD
    return g.astype(BF16)


def _mixer_fwd(name, z, cosq, sinq, gv, ws, bt, pw, psc, snk):
    s = z.shape[0]
    nb = s // CHUNK
    specs, _ = _mixer_specs(nb, False)

    def body(zc_ref, zpp_ref, zpkv_ref, cq_ref, sq_ref, cp_ref, sp_ref, gv_ref, ws_ref, bt_ref, pw_ref, psc_ref,
             snk_ref, bd_ref, o_ref, mix_s):
        ci = pl.program_id(0)
        zc = zc_ref[...]
        gel, _ = _gelu_parts(zc[:, :512])
        u, _, _, _, _, _, mixed = _gating(gel, gv_ref[...], ws_ref, bt_ref[...], bd_ref[...], mix_s)
        o_ref[:, :256] = (u * mixed).astype(BF16)
        pp = jnp.where(ci > 0, zpp_ref[...], 0.0)
        pooled, _, _ = _pool_fwd(zc[:, 512:768], pp, ci)
        mp = _dot(pooled.astype(BF16), pw_ref[...].astype(BF16))
        o_ref[:, 256:512] = (mp * psc_ref[...]).astype(BF16)
        qr, kband, vband, valid = _attn_prep(zc, zpkv_ref[...], cq_ref[...], sq_ref[...], cp_ref[...], sp_ref[...], ci)
        snkv = snk_ref[...]
        kb = kband.astype(BF16)
        vt = vband.T
        ots = []
        for hk in range(2):
            q4 = (_stack_heads(qr, 0, hk) * SCALE).astype(BF16)
            p, _ = _group_probs(kb[:, HEAD * hk:HEAD * (hk + 1)], q4, valid, _sink_row(snkv, hk))
            ots.append(_dot(vt[HEAD * hk:HEAD * (hk + 1), :].astype(BF16), p.astype(BF16)))
        o = jnp.concatenate(ots, axis=0).T
        for hk in range(2):
            for i in range(4):
                h = 4 * hk + i
                o_ref[:, 512 + HEAD * h:512 + HEAD * (h + 1)] = o[CHUNK * i:CHUNK * (i + 1),
                                                                  HEAD * hk:HEAD * (hk + 1)].astype(BF16)

    return pl.pallas_call(
        body, name=name, grid=(nb,), in_specs=specs, out_specs=pl.BlockSpec((CHUNK, 1024), lambda i: (i, 0)),
        out_shape=jax.ShapeDtypeStruct((s, 1024), BF16), scratch_shapes=[pltpu.VMEM((CHUNK, 256), F32)],
        compiler_params=_params(("parallel",)),
    )(z, z, z, cosq, sinq, cosq, sinq, gv, ws, bt, pw, psc, snk, _ones_bd())


def _mixer_bwd(name, z, dabc, cosq, sinq, gv, ws, bt, pw, psc, snk):
    s = z.shape[0]
    nb = s // CHUNK
    specs, cur = _mixer_specs(nb, True)
    specs = specs + [pl.BlockSpec((CHUNK, 1024), lambda i: (cur(i), 0))]
    full = lambda shape: pl.BlockSpec(shape, lambda i: (0,) * len(shape))
    acc_shapes = [(1, 256), (4, CHUNK, CHUNK), (CHUNK, 4), (256, 256), (1, 256), (1, 8)]

    def body(zc_ref, zpp_ref, zpkv_ref, cq_ref, sq_ref, cp_ref, sp_ref, gv_ref, ws_ref, bt_ref, pw_ref, psc_ref,
             snk_ref, bd_ref, dabc_ref, dz_ref, dgv_ref, dws_ref, dbt_ref, dpw_ref, dpsc_ref, dsnk_ref,
             cpool, ck, cv, dq_s, dkv_s, mix_s, dvn_s):
        step = pl.program_id(0)
        ci = nb - 1 - step

        @pl.when(step == 0)
        def _():
            for r in (dgv_ref, dws_ref, dbt_ref, dpw_ref, dpsc_ref, dsnk_ref, cpool, ck, cv):
                r[...] = jnp.zeros_like(r)

        zc = zc_ref[...]
        dabc = dabc_ref[...]
        zg = zc[:, :512]
        gel, th = _gelu_parts(zg)
        gp = _gelu_grad(zg, th)
        gvv = gv_ref[...]
        bd = bd_ref[...]
        u, r, xh, vn, wcs, causal, mixed = _gating(gel, gvv, ws_ref, bt_ref[...], bd, mix_s)
        da = dabc[:, :256]
        dm = da * u
        dmb = dm.astype(BF16)
        lane4 = lax.broadcasted_iota(jnp.int32, (CHUNK, 4), 1)
        dbt = jnp.zeros((CHUNK, 4), F32)
        for g in range(4):
            lo, hi = HEAD * g, HEAD * (g + 1)
            dws_ref[g] += jnp.where(causal, _dot(dmb[:, lo:hi], vn[:, lo:hi], NT), 0.0)
            dbt = dbt + jnp.where(lane4 == g, jnp.sum(dm[:, lo:hi], axis=-1, keepdims=True), 0.0)
            dvn_s[:, lo:hi] = _dot(wcs[g], dmb[:, lo:hi], TN)
        dbt_ref[...] += dbt
        dvn = dvn_s[...]
        dgv_ref[...] += jnp.sum(dvn * xh, axis=0, keepdims=True)
        dxh = dvn * gvv
        dvg = r * (dxh - xh * _group_mean(dxh * xh, bd))
        dz_ref[:, :256] = (da * mixed * gp[:, :256]).astype(BF16)
        dz_ref[:, 256:512] = (dvg * gp[:, 256:]).astype(BF16)
        pc = zc[:, 512:768]
        pp = jnp.where(ci > 0, zpp_ref[...], 0.0)
        pooled, cnt, lane = _pool_fwd(pc, pp, ci)
        pwb = pw_ref[...].astype(BF16)
        pooled_b = pooled.astype(BF16)
        mp = _dot(pooled_b, pwb)
        db = dabc[:, 256:512]
        dpsc_ref[...] += jnp.sum(db * mp, axis=0, keepdims=True)
        dmpb = (db * psc_ref[...]).astype(BF16)
        dpw_ref[...] += _dot(pooled_b, dmpb, TN)
        dpooled = _dot(dmpb, pwb, NT)
        davg = dpooled / cnt
        zero = jnp.zeros((CHUNK, 256), F32)
        d2, d4, d8, d16 = [jnp.concatenate([zero, jnp.where((lane >= 64 * k) & (lane < 64 * (k + 1)), davg, 0.0)],
                                           axis=0) for k in range(4)]
        g8 = d8 + d16 + pltpu.roll(d16, 2 * CHUNK - 8, 0)
        g4 = d4 + g8 + pltpu.roll(g8, 2 * CHUNK - 4, 0)
        g2 = d2 + g4 + pltpu.roll(g4, 2 * CHUNK - 2, 0)
        ge = g2 + pltpu.roll(g2, 2 * CHUNK - 1, 0)
        dz_ref[:, 512:768] = (ge[CHUNK:] - dpooled + cpool[...]).astype(BF16)
        cpool[...] = ge[:CHUNK]
        cc = cq_ref[...]
        sc = sq_ref[...]
        qr, kband, vband, valid = _attn_prep(zc, zpkv_ref[...], cc, sc, cp_ref[...], sp_ref[...], ci)
        snkv = snk_ref[...]
        lane8 = lax.broadcasted_iota(jnp.int32, (1, 8), 1)
        qlane = lax.broadcasted_iota(jnp.int32, (1, 4 * CHUNK), 1)
        dsnk = jnp.zeros((1, 8), F32)
        kb = kband.astype(BF16)
        vb = vband.astype(BF16)
        kt = kband.T * SCALE
        dqts = []
        for hk in range(2):
            kh = kb[:, HEAD * hk:HEAD * (hk + 1)]
            q4 = (_stack_heads(qr, 0, hk) * SCALE).astype(BF16)
            do4 = _stack_heads(dabc, 512, hk).astype(BF16)
            p, ps = _group_probs(kh, q4, valid, _sink_row(snkv, hk))
            dp = _dot(vb[:, HEAD * hk:HEAD * (hk + 1)], do4, NT)
            dd = jnp.sum(p * dp, axis=0, keepdims=True)
            dsink = -ps * dd
            for i in range(4):
                part = jnp.sum(jnp.where((qlane >= CHUNK * i) & (qlane < CHUNK * (i + 1)), dsink, 0.0),
                               axis=1, keepdims=True)
                dsnk = dsnk + jnp.where(lane8 == 4 * hk + i, part, 0.0)
            dsb = (p * (dp - dd)).astype(BF16)
            dqts.append(_dot(kt[HEAD * hk:HEAD * (hk + 1), :].astype(BF16), dsb))
            dkv_s[:, HEAD * hk:HEAD * (hk + 1)] = _dot(dsb, q4)
            dkv_s[:, 128 + HEAD * hk:128 + HEAD * (hk + 1)] = _dot(p.astype(BF16), do4)
        dq4 = jnp.concatenate(dqts, axis=0).T
        for hk in range(2):
            for i in range(4):
                h = 4 * hk + i
                dq_s[:, HEAD * h:HEAD * (h + 1)] = dq4[CHUNK * i:CHUNK * (i + 1), HEAD * hk:HEAD * (hk + 1)]
        dsnk_ref[...] += dsnk
        dqr = dq_s[...]
        dz_ref[:, 768:1280] = (dqr * jnp.concatenate([cc] * 4, axis=1)
                               + _rot_half(dqr * jnp.concatenate([sc] * 4, axis=1))).astype(BF16)
        dkv = dkv_s[...]
        dkr = dkv[CHUNK:, :128] + ck[...]
        dz_ref[:, 1280:1408] = (dkr * cc + _rot_half(dkr * sc)).astype(BF16)
        dz_ref[:, 1408:1536] = (dkv[CHUNK:, 128:] + cv[...]).astype(BF16)
        ck[...] = dkv[:CHUNK, :128]
        cv[...] = dkv[:CHUNK, 128:]

    return pl.pallas_call(
        body, name=name, grid=(nb,), in_specs=specs,
        out_specs=[pl.BlockSpec((CHUNK, 1536), lambda i: (cur(i), 0))] + [full(a) for a in acc_shapes],
        out_shape=[jax.ShapeDtypeStruct((s, 1536), BF16)] + [jax.ShapeDtypeStruct(a, F32) for a in acc_shapes],
        scratch_shapes=[pltpu.VMEM((CHUNK, 256), F32), pltpu.VMEM((CHUNK, 128), F32), pltpu.VMEM((CHUNK, 128), F32),
                        pltpu.VMEM((CHUNK, 512), F32), pltpu.VMEM((2 * CHUNK, 256), F32),
                        pltpu.VMEM((CHUNK, 256), F32), pltpu.VMEM((CHUNK, 256), F32)],
        compiler_params=_params(("arbitrary",)),
    )(z, z, z, cosq, sinq, cosq, sinq, gv, ws, bt, pw, psc, snk, _ones_bd(), dabc)


def _xattn_probs(qh, kh):
    s = _dot(qh, kh, NT) * (256 ** -0.5)
    e = jnp.exp(s - jnp.max(s, axis=-1, keepdims=True))
    return e * (1.0 / jnp.sum(e, axis=-1, keepdims=True))


def _xattn_fwd(name, q, kv, tq):
    s, d = q.shape
    mlen = kv.shape[0]

    def body(q_ref, kv_ref, o_ref):
        for h in range(4):
            lo, hi = 256 * h, 256 * (h + 1)
            p = _xattn_probs(q_ref[:, lo:hi], kv_ref[:, lo:hi])
            o_ref[:, lo:hi] = _dot(p.astype(BF16), kv_ref[:, d + lo:d + hi]).astype(BF16)

    blk = pl.BlockSpec((tq, d), lambda i: (i, 0))
    return pl.pallas_call(body, name=name, grid=(s // tq,),
                          in_specs=[blk, pl.BlockSpec((mlen, 2 * d), lambda i: (0, 0))], out_specs=blk,
                          out_shape=jax.ShapeDtypeStruct((s, d), BF16), compiler_params=_params(("parallel",)))(q, kv)


def _xattn_bwd(name, q, kv, do, tq):
    s, d = q.shape
    mlen = kv.shape[0]

    def body(q_ref, kv_ref, do_ref, dq_ref, dkv_ref):
        @pl.when(pl.program_id(0) == 0)
        def _():
            dkv_ref[...] = jnp.zeros_like(dkv_ref)

        for h in range(4):
            lo, hi = 256 * h, 256 * (h + 1)
            qh = q_ref[:, lo:hi]
            kh = kv_ref[:, lo:hi]
            vh = kv_ref[:, d + lo:d + hi]
            doh = do_ref[:, lo:hi]
            p = _xattn_probs(qh, kh)
            dp = _dot(doh, vh, NT)
            dsb = (p * (dp - jnp.sum(p * dp, axis=-1, keepdims=True)) * (256 ** -0.5)).astype(BF16)
            dq_ref[:, lo:hi] = _dot(dsb, kh).astype(BF16)
            dkv_ref[:, lo:hi] += _dot(dsb, qh, TN)
            dkv_ref[:, d + lo:d + hi] += _dot(p.astype(BF16), doh, TN)

    blk = pl.BlockSpec((tq, d), lambda i: (i, 0))
    kvb = pl.BlockSpec((mlen, 2 * d), lambda i: (0, 0))
    return pl.pallas_call(
        body, name=name, grid=(s // tq,), in_specs=[blk, kvb, blk], out_specs=[blk, kvb],
        out_shape=[jax.ShapeDtypeStruct((s, d), BF16), jax.ShapeDtypeStruct((mlen, 2 * d), F32)],
        compiler_params=_params(("arbitrary",)))(q, kv, do)


def _sigmoid(x):
    return 0.5 * (1.0 + jnp.tanh(0.5 * x))


def _ffn_up(name, h, wgu, layer, tm, tn):
    s, d = h.shape
    dff = wgu.shape[2] // 2
    nj = dff // tn

    def body(h_ref, wg_ref, wu_ref, g_ref, u_ref, a_ref):
        hv = h_ref[...]
        gate = _dot(hv, wg_ref[...])
        up = _dot(hv, wu_ref[...])
        g_ref[...] = gate.astype(BF16)
        u_ref[...] = up.astype(BF16)
        a_ref[...] = (gate * _sigmoid(gate) * up).astype(BF16)

    ob = pl.BlockSpec((tm, tn), lambda j, i: (i, j))
    sd = jax.ShapeDtypeStruct((s, dff), BF16)
    return pl.pallas_call(
        body, name=name, grid=(nj, s // tm),
        in_specs=[pl.BlockSpec((tm, d), lambda j, i: (i, 0)),
                  pl.BlockSpec((None, d, tn), lambda j, i: (layer, 0, j)),
                  pl.BlockSpec((None, d, tn), lambda j, i: (layer, 0, j + nj))],
        out_specs=[ob, ob, ob], out_shape=[sd, sd, sd], compiler_params=_params(("parallel", "parallel")),
    )(h, wgu, wgu)


def _ffn_act_bwd(name, dfn, wdown, layer, gate, up, tm):
    s, d = dfn.shape
    dff = gate.shape[1]

    def body(df_ref, wd_ref, g_ref, u_ref, o_ref):
        dact = _dot(df_ref[...], wd_ref[...], NT)
        gate = g_ref[...].astype(F32)
        sig = _sigmoid(gate)
        o_ref[:, :dff] = (dact * u_ref[...].astype(F32) * sig * (1.0 + gate * (1.0 - sig))).astype(BF16)
        o_ref[:, dff:] = (dact * gate * sig).astype(BF16)

    gb = pl.BlockSpec((tm, dff), lambda i: (i, 0))
    return pl.pallas_call(
        body, name=name, grid=(s // tm,),
        in_specs=[pl.BlockSpec((tm, d), lambda i: (i, 0)),
                  pl.BlockSpec((None, dff, d), lambda i: (layer, 0, 0)), gb, gb],
        out_specs=pl.BlockSpec((tm, 2 * dff), lambda i: (i, 0)),
        out_shape=jax.ShapeDtypeStruct((s, 2 * dff), BF16), compiler_params=_params(("parallel",)),
    )(dfn, wdown, gate, up)


def _place():
    return lax.axis_index("x"), lax.axis_index("y"), lax.axis_index("c")


def _other_chips(x, y):
    return [(1 - x, y), (x, 1 - y), (1 - x, 1 - y)]


def _region(ref, axis, chip, size):
    start = pl.multiple_of(chip * size, size)
    if axis == 1:
        return ref.at[:, pl.ds(start, size), :]
    return ref.at[:, :, pl.ds(start, size)]


ANY = pl.BlockSpec(memory_space=pl.ANY)


HBM = pl.BlockSpec(memory_space=pltpu.HBM)
SEM = pl.BlockSpec(memory_space=pltpu.SEMAPHORE)
EFFECT = pltpu.SideEffectType.DATAFLOW_SIDE_EFFECTING


def _in_hbm(a):
    return pltpu.with_memory_space_constraint(a, pltpu.HBM)


def _split_start(name, srcs, lands, ncopies, plan):
    ns, nl = len(srcs), len(lands)

    def body(*refs):
        src, land = refs[:ns], refs[ns:ns + nl]
        send, recv = refs[ns + nl], refs[ns + nl + 1]
        token = refs[-1]
        x, y, c = _place()
        for k, (s_ref, d_ref, peer, _) in enumerate(plan(src, land, x, y, c)):
            pltpu.make_async_remote_copy(src_ref=s_ref, dst_ref=d_ref, send_sem=send.at[k], recv_sem=recv.at[k],
                                         device_id=peer, device_id_type=MESH).start()
        token[...] = jnp.zeros_like(token)

    ops = list(srcs) + list(lands)
    out = pl.pallas_call(
        body, name=name,
        out_shape=(pltpu.SemaphoreType.DMA((ncopies,)), pltpu.SemaphoreType.DMA((ncopies,)),
                   *[pltpu.HBM(a.shape, a.dtype) for a in ops], jax.ShapeDtypeStruct((8, 128), F32)),
        in_specs=(HBM,) * (ns + nl),
        out_specs=(SEM, SEM) + (HBM,) * (ns + nl) + (pl.BlockSpec(memory_space=pltpu.VMEM),),
        input_output_aliases={i: 2 + i for i in range(ns + nl)},
        compiler_params=pltpu.CompilerParams(has_side_effects=EFFECT),
    )(*[_in_hbm(a) for a in ops])
    return out[0], out[1], list(out[2:2 + ns]), list(out[2 + ns:2 + ns + nl]), out[-1]


def _split_start_many(name, lands, jobs):
    nl, nj = len(lands), len(jobs)

    def body(*refs):
        land = refs[:nl]
        sems = refs[nl:nl + 2 * nj]
        token = refs[-1]
        x, y, c = _place()
        for j, (idx, _, plan) in enumerate(jobs):
            for k, (s_ref, d_ref, peer, _) in enumerate(plan((), [land[t] for t in idx], x, y, c)):
                pltpu.make_async_remote_copy(src_ref=s_ref, dst_ref=d_ref, send_sem=sems[2 * j].at[k],
                                             recv_sem=sems[2 * j + 1].at[k], device_id=peer,
                                             device_id_type=MESH).start()
        token[...] = jnp.zeros_like(token)

    sem_shapes = tuple(pltpu.SemaphoreType.DMA((n,)) for _, n, _ in jobs for _ in range(2))
    out = pl.pallas_call(
        body, name=name,
        out_shape=sem_shapes + tuple(pltpu.HBM(a.shape, a.dtype) for a in lands)
        + (jax.ShapeDtypeStruct((8, 128), F32),),
        in_specs=(HBM,) * nl,
        out_specs=(SEM,) * (2 * nj) + (HBM,) * nl + (pl.BlockSpec(memory_space=pltpu.VMEM),),
        input_output_aliases={i: 2 * nj + i for i in range(nl)},
        compiler_params=pltpu.CompilerParams(has_side_effects=EFFECT),
    )(*[_in_hbm(a) for a in lands])
    return [(out[2 * j], out[2 * j + 1]) for j in range(nj)], list(out[2 * nj:2 * nj + nl]), out[-1]


def _split_wait(name, send, recv, srcs, lands, after, plan):
    ns, nl = len(srcs), len(lands)

    def body(*refs):
        src, land = refs[:ns], refs[ns:ns + nl]
        send_ref, recv_ref = refs[ns + nl], refs[ns + nl + 1]
        x, y, c = _place()
        for k, (s_ref, _, _, got) in enumerate(plan(src, land, x, y, c)):
            cp = pltpu.make_async_remote_copy(src_ref=s_ref, dst_ref=got, send_sem=send_ref.at[k],
                                              recv_sem=recv_ref.at[k], device_id=(x, y, c), device_id_type=MESH)
            cp.wait_send()
            cp.wait_recv()

    ops = list(srcs) + list(lands)
    out = pl.pallas_call(
        body, name=name, out_shape=tuple(pltpu.HBM(a.shape, a.dtype) for a in ops),
        in_specs=(HBM,) * (ns + nl) + (SEM, SEM, ANY), out_specs=(HBM,) * (ns + nl),
        input_output_aliases={i: i for i in range(ns + nl)},
        compiler_params=pltpu.CompilerParams(has_side_effects=EFFECT),
    )(*ops, send, recv, after)
    return list(out[:ns]), list(out[ns:])


def _split_wait_start(name, send, recv, lands, after, wait_plan, ncopies, start_plan, carried=()):
    nl, nc = len(lands), len(carried)
    lands = list(lands) + list(carried)

    def body(*refs):
        land = refs[:nl]
        send_in, recv_in = refs[nl + nc], refs[nl + nc + 1]
        send_out, recv_out = refs[nl + nc + 3], refs[nl + nc + 4]
        x, y, c = _place()
        for k, (s_ref, _, _, got) in enumerate(wait_plan((), land, x, y, c)):
            cp = pltpu.make_async_remote_copy(src_ref=s_ref, dst_ref=got, send_sem=send_in.at[k],
                                              recv_sem=recv_in.at[k], device_id=(x, y, c), device_id_type=MESH)
            cp.wait_send()
            cp.wait_recv()
        for k, (s_ref, d_ref, peer, _) in enumerate(start_plan((), land, x, y, c)):
            pltpu.make_async_remote_copy(src_ref=s_ref, dst_ref=d_ref, send_sem=send_out.at[k],
                                         recv_sem=recv_out.at[k], device_id=peer, device_id_type=MESH).start()

    out = pl.pallas_call(
        body, name=name,
        out_shape=(pltpu.SemaphoreType.DMA((ncopies,)), pltpu.SemaphoreType.DMA((ncopies,)),
                   *[pltpu.HBM(a.shape, a.dtype) for a in lands]),
        in_specs=(HBM,) * (nl + nc) + (SEM, SEM, ANY), out_specs=(SEM, SEM) + (HBM,) * (nl + nc),
        input_output_aliases={i: 2 + i for i in range(nl + nc)},
        compiler_params=pltpu.CompilerParams(has_side_effects=EFFECT),
    )(*lands, send, recv, after)
    return out[0], out[1], list(out[2:2 + nl]), list(out[2 + nl:])


def _half(ref, axis, chip, size, layer, h):
    reg = _region(ref, axis, chip, size).at[pl.ds(layer, 1)]
    rows = reg.shape[1] // 2
    return reg.at[:, pl.ds(pl.multiple_of(h * rows, rows), rows), :]


def _gather_plan(axes, sizes, layer):
    def plan(src, land, x, y, c):
        me = 2 * x + y
        out = []
        for t in range(len(land)):
            mine = _half(land[t], axes[t], me, sizes[t], layer, c)
            for px, py in _other_chips(x, y):
                out.append((mine, mine, (px, py, c), _half(land[t], axes[t], 2 * px + py, sizes[t], layer, c)))
        return out
    return plan


def _forward_plan(axes, sizes, layer):
    def plan(src, land, x, y, c):
        out = []
        for t in range(len(land)):
            for px, py in _other_chips(x, y):
                got = _half(land[t], axes[t], 2 * px + py, sizes[t], layer, c)
                out.append((got, got, (x, y, 1 - c), _half(land[t], axes[t], 2 * px + py, sizes[t], layer, 1 - c)))
        return out
    return plan


def _place_own(name, w, axis, chip):
    nl, r, cs = w.shape
    tr = _rows_tile(r, cs)
    nb = r // tr
    full = (nl, 4 * r, cs) if axis == 1 else (nl, r, 4 * cs)

    def body(m_ref, w_ref, o_ref):
        o_ref[...] = w_ref[...].astype(BF16)

    if axis == 1:
        ospec = pl.BlockSpec((None, tr, cs), lambda l, i, m: (l, m[0] * nb + i, 0))
    else:
        ospec = pl.BlockSpec((None, tr, cs), lambda l, i, m: (l, i, m[0]))
    return pl.pallas_call(
        body, name=name,
        grid_spec=pltpu.PrefetchScalarGridSpec(
            num_scalar_prefetch=1, grid=(nl, nb),
            in_specs=[pl.BlockSpec((None, tr, cs), lambda l, i, m: (l, i, 0))], out_specs=ospec),
        out_shape=jax.ShapeDtypeStruct(full, BF16), compiler_params=_params(("parallel", "parallel")),
    )(chip, w)


def _scatter_plan(axes, sizes):
    def plan(src, land, x, y, c):
        out = []
        for t in range(len(src)):
            for k, (px, py) in enumerate(_other_chips(x, y)):
                out.append((_region(src[t], axes[t], 2 * px + py, sizes[t]).at[0], land[t].at[k], (px, py, c),
                            land[t].at[k]))
        return out
    return plan


def _pair_plan(src, land, x, y, c):
    return [(src[t], land[t], (x, y, 1 - c), land[t]) for t in range(len(src))]


UNIT_STEPS = 16


def _unit_rows(r):
    return min(t for t in range(16, r + 1, 16) if r % t == 0 and r // t <= UNIT_STEPS)


def _guarded(i, nb, steps, work):
    if nb == steps:
        work()
    else:
        pl.when(i < nb)(work)


def _unit_chip_sum(name, gs, slots, axes, chip):
    n = len(gs)
    dims = [s.shape[1:] for s in slots]
    trs = [_unit_rows(r) for r, _ in dims]
    nbs = [r // tr for (r, _), tr in zip(dims, trs)]
    steps = max(nbs)

    def body(m_ref, *refs):
        i = pl.program_id(0)
        for t in range(n):
            def work(t=t):
                acc = refs[t][...].astype(F32)
                for k in range(3):
                    acc = acc + refs[n + t][k].astype(F32)
                refs[2 * n + t][...] = acc.astype(BF16)
            _guarded(i, nbs[t], steps, work)

    gspecs, sspecs, ospecs = [], [], []
    for (r, cs), tr, nb, axis in zip(dims, trs, nbs, axes):
        if axis == 1:
            gspecs.append(pl.BlockSpec((tr, cs), lambda i, m, nb=nb: (m[0] * nb + jnp.minimum(i, nb - 1), 0)))
        else:
            gspecs.append(pl.BlockSpec((tr, cs), lambda i, m, nb=nb: (jnp.minimum(i, nb - 1), m[0])))
        sspecs.append(pl.BlockSpec((3, tr, cs), lambda i, m, nb=nb: (0, jnp.minimum(i, nb - 1), 0)))
        ospecs.append(pl.BlockSpec((tr, cs), lambda i, m, nb=nb: (jnp.minimum(i, nb - 1), 0)))
    return pl.pallas_call(
        body, name=name,
        grid_spec=pltpu.PrefetchScalarGridSpec(num_scalar_prefetch=1, grid=(steps,), in_specs=gspecs + sspecs,
                                               out_specs=ospecs),
        out_shape=[jax.ShapeDtypeStruct(d, BF16) for d in dims], compiler_params=_params(("arbitrary",)),
    )(chip, *gs, *slots)


def _unit_adamw(name, mine, theirs, ws, ms, vs, layer, bufs):
    n = len(mine)
    dims = [a.shape for a in mine]
    trs = [_unit_rows(r) for r, _ in dims]
    nbs = [r // tr for (r, _), tr in zip(dims, trs)]
    steps = max(nbs)
    c1 = 1.0 - B1 ** STEP
    c2 = 1.0 - B2 ** STEP

    def body(*refs):
        i = pl.program_id(0)
        outs = refs[9 * n:]
        for t in range(n):
            def work(t=t):
                a_ref, b_ref, w_ref, m_ref, v_ref = refs[5 * t:5 * t + 5]
                g_ref, d_ref, mo_ref, vo_ref = outs[4 * t:4 * t + 4]
                gv = a_ref[...].astype(F32) + b_ref[...].astype(F32)
                mn = B1 * m_ref[...] + (1.0 - B1) * gv
                vn = B2 * v_ref[...] + (1.0 - B2) * (gv * gv)
                g_ref[...] = gv
                mo_ref[...] = mn
                vo_ref[...] = vn
                d_ref[...] = -LR * ((mn / c1) / (jnp.sqrt(vn / c2) + ADAM_EPS) + WD * w_ref[...])
            _guarded(i, nbs[t], steps, work)

    in_specs, out_specs, ops = [], [], []
    for t, ((r, cs), tr, nb) in enumerate(zip(dims, trs, nbs)):
        blk = pl.BlockSpec((tr, cs), lambda i, nb=nb: (jnp.minimum(i, nb - 1), 0))
        lay = pl.BlockSpec((None, tr, cs), lambda i, nb=nb: (layer, jnp.minimum(i, nb - 1), 0))
        in_specs += [blk, blk, lay, lay, lay]
        out_specs += [lay] * 4
        ops += [mine[t], theirs[t], ws[t], ms[t], vs[t]]
    flat = [b for bs in bufs for b in bs]
    out = pl.pallas_call(
        body, name=name, grid=(steps,), in_specs=in_specs + [ANY] * (4 * n), out_specs=out_specs,
        out_shape=[jax.ShapeDtypeStruct(b.shape, b.dtype) for b in flat],
        input_output_aliases={5 * n + k: k for k in range(4 * n)}, compiler_params=_params(("arbitrary",)),
    )(*ops, *flat)
    return [list(out[4 * t:4 * t + 4]) for t in range(n)]


def _small_pair_sum(p):
    rows = p.shape[0]
    half = rows // 2

    def body(p_ref, o_ref, sib, send, recv):
        x, y, c = _place()
        mine = pl.ds(pl.multiple_of(c * half, half), half)
        theirs = pl.ds(pl.multiple_of((1 - c) * half, half), half)
        pair = pltpu.make_async_remote_copy(src_ref=p_ref.at[theirs], dst_ref=sib, send_sem=send, recv_sem=recv,
                                            device_id=(x, y, 1 - c), device_id_type=MESH)
        pair.start()
        pair.wait()
        o_ref[...] = (p_ref[mine] + sib[...]).astype(BF16)

    vm = pl.BlockSpec(memory_space=pltpu.VMEM)
    return pl.pallas_call(
        body, name="small_pair_sum", in_specs=[vm], out_specs=vm, out_shape=jax.ShapeDtypeStruct((half, 128), BF16),
        scratch_shapes=[pltpu.VMEM((half, 128), F32), pltpu.SemaphoreType.DMA, pltpu.SemaphoreType.DMA],
        compiler_params=pltpu.CompilerParams(vmem_limit_bytes=VMEM_LIMIT),
    )(p)


def _small_plan(src, land, x, y, c):
    return [(src[0], land[0].at[k], (px, py, c), land[0].at[k]) for k, (px, py) in enumerate(_other_chips(x, y))]


def _small_total(own, slots):
    half = own.shape[0]

    def body(own_ref, slots_ref, o_ref, sums, send, recv):
        x, y, c = _place()
        me = 2 * x + y
        mine = pl.ds(pl.multiple_of(c * half, half), half)
        theirs = pl.ds(pl.multiple_of((1 - c) * half, half), half)
        sums[me] = own_ref[...]
        for k, (px, py) in enumerate(_other_chips(x, y)):
            sums[2 * px + py] = slots_ref[k]
        acc = sums[0].astype(F32)
        for k in range(1, 4):
            acc = acc + sums[k].astype(F32)
        o_ref[mine] = acc
        back = pltpu.make_async_remote_copy(src_ref=o_ref.at[mine], dst_ref=o_ref.at[mine], send_sem=send,
                                            recv_sem=recv, device_id=(x, y, 1 - c), device_id_type=MESH)
        back.start()
        pltpu.make_async_remote_copy(src_ref=o_ref.at[theirs], dst_ref=o_ref.at[theirs], send_sem=send,
                                     recv_sem=recv, device_id=(x, y, c), device_id_type=MESH).wait_recv()
        back.wait_send()

    vm = pl.BlockSpec(memory_space=pltpu.VMEM)
    return pl.pallas_call(
        body, name="small_total", in_specs=[vm, vm], out_specs=vm,
        out_shape=jax.ShapeDtypeStruct((2 * half, 128), F32),
        scratch_shapes=[pltpu.VMEM((4, half, 128), BF16), pltpu.SemaphoreType.DMA, pltpu.SemaphoreType.DMA],
        compiler_params=pltpu.CompilerParams(vmem_limit_bytes=VMEM_LIMIT),
    )(own, slots)


def _small_adamw(gs, ws, ms, vs):
    n = len(gs)
    c1 = 1.0 - B1 ** STEP
    c2 = 1.0 - B2 ** STEP

    def body(*refs):
        for t in range(n):
            gv, wv = refs[t][...], refs[n + t][...]
            d_ref, mo_ref, vo_ref = refs[4 * n + 3 * t:4 * n + 3 * t + 3]
            mn = B1 * refs[2 * n + t][...] + (1.0 - B1) * gv
            vn = B2 * refs[3 * n + t][...] + (1.0 - B2) * (gv * gv)
            mo_ref[...] = mn
            vo_ref[...] = vn
            d_ref[...] = -LR * ((mn / c1) / (jnp.sqrt(vn / c2) + ADAM_EPS) + WD * wv)

    vm = pl.BlockSpec(memory_space=pltpu.VMEM)
    out = pl.pallas_call(
        body, name="adamw_small", in_specs=[vm] * (4 * n), out_specs=[vm] * (3 * n),
        out_shape=[jax.ShapeDtypeStruct(w.shape, F32) for w in ws for _ in range(3)],
        compiler_params=pltpu.CompilerParams(vmem_limit_bytes=VMEM_LIMIT),
    )(*gs, *ws, *ms, *vs)
    return [tuple(out[3 * t:3 * t + 3]) for t in range(n)]


def _pack(parts):
    flat = []
    for p in parts:
        v = p.reshape(-1).astype(F32)
        flat.append(jnp.pad(v, (0, (-v.shape[0]) % 128)))
    v = jnp.concatenate(flat)
    v = jnp.pad(v, (0, (-v.shape[0]) % (512 * 128)))
    return v.reshape(-1, 128)


def _unpack(buf, like):
    out, r0 = [], 0
    for p in like:
        nelem = 1
        for s in p.shape:
            nelem *= s
        rows = -(-nelem // 128)
        blk = buf[r0:r0 + rows]
        if nelem % 128:
            blk = blk.reshape(-1)[:nelem]
        out.append(blk.reshape(p.shape))
        r0 += rows
    return out


def kernel(x, mem, positions, mem_norm_g, mix_pre_g, mix_post_g, w_in, gm_v_g, gm_w_s, gm_b_s, pool_w, pool_scale, attn_sinks, w_o, x_pre_g, x_post_g, w_xq, w_xkv, w_xo, ffn_pre_g, ffn_post_g, w_gate_up, w_down, loss_target, m_mem_norm_g, m_mix_pre_g, m_mix_post_g, m_w_in, m_gm_v_g, m_gm_w_s, m_gm_b_s, m_pool_w, m_pool_scale, m_attn_sinks, m_w_o, m_x_pre_g, m_x_post_g, m_w_xq, m_w_xkv, m_w_xo, m_ffn_pre_g, m_ffn_post_g, m_w_gate_up, m_w_down, v_mem_norm_g, v_mix_pre_g, v_mix_post_g, v_w_in, v_gm_v_g, v_gm_w_s, v_gm_b_s, v_pool_w, v_pool_scale, v_attn_sinks, v_w_o, v_x_pre_g, v_x_post_g, v_w_xq, v_w_xkv, v_w_xo, v_ffn_pre_g, v_ffn_post_g, v_w_gate_up, v_w_down):
    args = (x, mem, positions, mem_norm_g, mix_pre_g, mix_post_g, w_in, gm_v_g, gm_w_s, gm_b_s, pool_w, pool_scale, attn_sinks, w_o, x_pre_g, x_post_g, w_xq, w_xkv, w_xo, ffn_pre_g, ffn_post_g, w_gate_up, w_down)
    moms_m = (m_mem_norm_g, m_mix_pre_g, m_mix_post_g, m_w_in, m_gm_v_g, m_gm_w_s, m_gm_b_s, m_pool_w, m_pool_scale, m_attn_sinks, m_w_o, m_x_pre_g, m_x_post_g, m_w_xq, m_w_xkv, m_w_xo, m_ffn_pre_g, m_ffn_post_g, m_w_gate_up, m_w_down)
    moms_v = (v_mem_norm_g, v_mix_pre_g, v_mix_post_g, v_w_in, v_gm_v_g, v_gm_w_s, v_gm_b_s, v_pool_w, v_pool_scale, v_attn_sinks, v_w_o, v_x_pre_g, v_x_post_g, v_w_xq, v_w_xkv, v_w_xo, v_ffn_pre_g, v_ffn_post_g, v_w_gate_up, v_w_down)
    P = dict(zip(NAMES, args))
    P['loss_target'] = loss_target
    M = dict(zip(WEIGHTS, moms_m))
    V = dict(zip(WEIGHTS, moms_v))
    depth = w_in.shape[0]
    nbig = len(BIG)
    axes = [BIG_AXIS[n] for n in BIG]
    sizes = [P[n].shape[a] for n, a in zip(BIG, axes)]
    chip = (2 * lax.axis_index("x") + lax.axis_index("y")).astype(jnp.int32).reshape(1)

    groups = [['w_in'], ['w_o', 'w_xq', 'w_xkv', 'w_xo'], ['w_gate_up', 'w_down']]
    units = [(l, g) for l in range(depth) for g in groups]
    unit_of = {(l, n): i for i, (l, names) in enumerate(units) for n in names}
    ax = lambda names: [BIG_AXIS[n] for n in names]
    sz = lambda names: [P[n].shape[BIG_AXIS[n]] for n in names]

    full = {n: _place_own("place_" + n, P[n], BIG_AXIS[n], chip) for n in BIG}
    gathers, land, tok = _split_start_many(
        "gather_start", [full[n] for n in BIG],
        [([BIG.index(n) for n in names], 3 * len(names), _gather_plan(ax(names), sz(names), l)) for l, names in units])
    full.update(zip(BIG, land))
    P['first_dep'] = tok[:1, :1]
    forwards, gathered = {}, set()

    def forward_unit(i, after, carried=()):
        ul, unames = units[i]
        send, recv = gathers[i]
        send, recv, land, thru = _split_wait_start(
            "gather_pass%d" % i, send, recv, [full[n] for n in unames], after,
            _gather_plan(ax(unames), sz(unames), ul), 3 * len(unames), _forward_plan(ax(unames), sz(unames), ul),
            carried=[full[n] for n in carried])
        full.update(zip(unames, land))
        full.update(zip(carried, thru))
        forwards[i] = (send, recv)

    def weights_of(l, names, after):
        i = unit_of[(l, names[0])]
        if i not in gathered:
            _, unames = units[i]
            if i not in forwards:
                forward_unit(i, after)
            send, recv = forwards.pop(i)
            _, land = _split_wait("gather_wait%d" % i, send, recv, [], [full[n] for n in unames], after,
                                  _forward_plan(ax(unames), sz(unames), l))
            full.update(zip(unames, land))
            gathered.add(i)
            if len(groups) <= i + 1 < len(units):
                forward_unit(i + 1, after, carried=[n for n in unames if n not in units[i + 1][1]])
        return {n: (full[n], l) for n in names}

    outs = {n: [lax.empty(P[n].shape, F32) for _ in range(4)] for n in BIG}
    gunits = [(l, BIG) for l in range(depth - 1, 0, -1)] + [
        (0, g) for g in (['w_gate_up', 'w_down'], ['w_xq', 'w_xkv', 'w_xo'], ['w_o'], ['w_in'])]
    collected, scatters, pairs = {}, {}, {}

    def finish_scatter(i, after):
        _, names = gunits[i]
        send, recv, g_l, slots = scatters.pop(i)
        g_l, slots = _split_wait("scatter_wait%d" % i, send, recv, g_l, slots, after,
                                 _scatter_plan(ax(names), sz(names)))
        mine = _unit_chip_sum("chip_sum", [g.reshape(g.shape[1:]) for g in g_l], slots, ax(names), chip)
        send, recv, mine, theirs, tok = _split_start("pair_start%d" % i, mine, [lax.empty(a.shape, BF16) for a in mine],
                                                     len(names), _pair_plan)
        pairs[i] = (send, recv, mine, theirs)
        return tok[:1, :1]

    def finish_pair(i, after):
        l, names = gunits[i]
        send, recv, mine, theirs = pairs.pop(i)
        mine, theirs = _split_wait("pair_wait%d" % i, send, recv, mine, theirs, after, _pair_plan)
        new = _unit_adamw("adamw", mine, theirs, [P[n] for n in names], [M[n] for n in names], [V[n] for n in names],
                          l, [outs[n] for n in names])
        outs.update(zip(names, new))

    calls = {'n': 0}
    lag = 4

    def grads_of(l, g_part, after):
        collected.update({(l, n): g for n, g in g_part.items()})
        calls['n'] += 1
        now = calls['n']
        tok = jnp.zeros((1, 1), F32)
        for i, (ul, names) in enumerate(gunits):
            if ul != l or ('started', i) in collected or any((l, n) not in collected for n in names):
                continue
            collected[('started', i)] = now
            srcs = [collected[(l, n)].reshape((1,) + collected[(l, n)].shape) for n in names]
            send, recv, srcs, slots, t = _split_start("scatter_start%d" % i, srcs,
                                                      [lax.empty((3,) + P[n].shape[1:], BF16) for n in names],
                                                      3 * len(names), _scatter_plan(ax(names), sz(names)))
            scatters[i] = (send, recv, srcs, slots)
            tok = tok + t[:1, :1]
        for i in sorted(pairs):
            if collected[('summed', i)] + lag <= now:
                finish_pair(i, after)
        for i in sorted(scatters):
            if collected[('started', i)] + lag <= now:
                tok = tok + finish_scatter(i, after)
                collected[('summed', i)] = now
        return tok

    loss_part, dx, small_g = _fwd_bwd(P, weights_of, grads_of)
    loss = lax.psum(loss_part[0, 0], ("x", "y", "c"))
    grad_x = dx.reshape(x.shape)

    small_like = [P[n] for n in SMALL]
    half_sum = _small_pair_sum(_pack(small_g))
    send, recv, (half_sum,), (slots,), _ = _split_start(
        "small_start", [half_sum], [lax.empty((3,) + half_sum.shape, BF16)], 3, _small_plan)

    for i in sorted(pairs):
        finish_pair(i, half_sum)
    for i in sorted(scatters):
        finish_scatter(i, half_sum)
    for i in sorted(pairs):
        finish_pair(i, half_sum)
    grads, deltas, new_m, new_v = {}, {}, {}, {}
    for n in BIG:
        grads[n], deltas[n], new_m[n], new_v[n] = outs[n]

    (half_sum,), (slots,) = _split_wait("small_wait", send, recv, [half_sum], [slots], grads[BIG[0]], _small_plan)
    two_d = lambda a: a.reshape(-1, a.shape[-1])
    gs = _unpack(_small_total(half_sum, slots), small_like)
    upd = _small_adamw([two_d(g) for g in gs], [two_d(P[n]) for n in SMALL], [two_d(M[n]) for n in SMALL],
                       [two_d(V[n]) for n in SMALL])
    for n, g, (dlt, mn, vn) in zip(SMALL, gs, upd):
        shape = P[n].shape
        grads[n], deltas[n], new_m[n], new_v[n] = g, dlt.reshape(shape), mn.reshape(shape), vn.reshape(shape)

    return (loss, grad_x, *[grads[n] for n in WEIGHTS], *[deltas[n] for n in WEIGHTS],
            *[new_m[n] for n in WEIGHTS], *[new_v[n] for n in WEIGHTS])


def _fwd_bwd(P, weights_of, grads_of):
    (x, mem, positions, mem_norm_g, mix_pre_g, mix_post_g, w_in, gm_v_g, gm_w_s, gm_b_s, pool_w, pool_scale, attn_sinks,
     w_o, x_pre_g, x_post_g, w_xq, w_xkv, w_xo, ffn_pre_g, ffn_post_g, w_gate_up, w_down) = [P[n] for n in NAMES]
    x0 = x[0]
    s, d = x0.shape
    depth = w_in.shape[0]
    tgt = P['loss_target'][0]
    tmn = 256
    tmr = min(512, s)
    tmp = min(1024, s)
    tkw = min(2048, s)

    half = HEAD // 2
    inv = ROPE_THETA ** (-jnp.arange(half, dtype=F32) / half)
    ang = positions[0].astype(F32)[:, None] * inv
    cos, sin = jnp.cos(ang), jnp.sin(ang)
    cosq = jnp.tile(jnp.concatenate([cos, cos], axis=-1), (1, 2))
    sinq = jnp.tile(jnp.concatenate([-sin, sin], axis=-1), (1, 2))

    row = lambda a, l: a[l].reshape(1, -1)
    memn = _prenorm("mem_norm", mem[0], mem_norm_g.reshape(1, d), tmn)
    pw_bd = []
    for l in range(depth):
        bd = jnp.zeros((256, 256), F32)
        for g in range(4):
            bd = lax.dynamic_update_slice(bd, pool_w[l, g], (64 * g, 64 * g))
        pw_bd.append(bd)

    saved = []
    xc = x0
    h = _prenorm("pre_norm0", x0, row(mix_pre_g, 0) + P['first_dep'], tmn)
    for l in range(depth):
        W = weights_of(l, ['w_in'], xc)
        sv = {'x0': xc, 'h1': h}
        z, = _mm_rows("fwd_w_in", h, *W['w_in'], 'nn', tm=tmp, rows_out=[F32], epilogue=_plain_rows)
        abc = _mixer_fwd("mixer_fwd", z, cosq, sinq, row(gm_v_g, l), gm_w_s[l], gm_b_s[l].T, pw_bd[l],
                         row(pool_scale, l), row(attn_sinks, l))
        W.update(weights_of(l, ['w_o'], z))
        mix, xc, h = _mm_rows("fwd_w_o", abc, *W['w_o'], 'nn', tm=tmr, rows_in=[xc],
                              params=[row(mix_post_g, l), row(x_pre_g, l)], rows_out=[BF16, F32, BF16],
                              epilogue=_post_pre_rows)
        sv.update(z=z, abc=abc, mix=mix, x1=xc, h2=h)
        W.update(weights_of(l, ['w_xq', 'w_xkv', 'w_xo'], xc))
        q, = _mm_rows("fwd_w_xq", h, *W['w_xq'], 'nn', tm=tmp, rows_out=[BF16], epilogue=_plain_rows)
        kv = _mm_nn("fwd_w_xkv", memn, *W['w_xkv'], tm=256, tn=512, tk=d, out_dtype=BF16)
        o = _xattn_fwd("xattn_fwd", q, kv, 512)
        xo, xc, h = _mm_rows("fwd_w_xo", o, *W['w_xo'], 'nn', tm=tmr, rows_in=[xc],
                             params=[row(x_post_g, l), row(ffn_pre_g, l)], rows_out=[BF16, F32, BF16],
                             epilogue=_post_pre_rows)
        sv.update(q=q, kv=kv, o=o, xo=xo, x2=xc, h3=h)
        W.update(weights_of(l, ['w_gate_up', 'w_down'], xc))
        dff = W['w_down'][0].shape[1]
        gate, up, act = _ffn_up("ffn_up", h, *W['w_gate_up'], 512, dff // 2)
        sv.update(gate=gate, up=up, act=act)
        if l + 1 < depth:
            f, xc, h = _mm_rows("fwd_w_down", act, *W['w_down'], 'nn', tm=tmr, rows_in=[xc],
                                params=[row(ffn_post_g, l), row(mix_pre_g, l + 1)], rows_out=[BF16, F32, BF16],
                                epilogue=_post_pre_rows)
            sv.update(f=f)
        saved.append(sv)
    gs = {n: [None] * depth for n in SMALL if n != 'mem_norm_g'}
    dx, dfn, gs['ffn_post_g'][depth - 1], loss_part = _mm_rows(
        "fwd_w_down_loss", saved[-1]['act'], *W['w_down'], 'nn', tm=tmr, rows_in=[xc, tgt],
        params=[row(ffn_post_g, depth - 1)], rows_out=[F32, BF16], n_sums=2, epilogue=_make_loss_rows(d))

    dmemn = None
    tok = jnp.zeros((1, 1), F32)
    for l in reversed(range(depth)):
        sv, W, G = saved[l], weights_of(l, BIG, dx), {}
        G['w_down'] = _mm_tn("dw_down", sv['act'], dfn, tm=dff // 2, tn=d, tk=tkw)
        dgu = _ffn_act_bwd("ffn_act_bwd", dfn, *W['w_down'], sv['gate'], sv['up'], 256)
        G['w_gate_up'] = _mm_tn("dw_gate_up", sv['h3'], dgu, tm=d, tn=dff // 2, tk=tkw)
        dx, dxo, gs['ffn_pre_g'][l], gs['x_post_g'][l] = _mm_rows(
            "bwd_w_gate_up", dgu, *W['w_gate_up'], 'nt', tm=tmr, rows_in=[sv['x2'], dx, sv['xo']],
            params=[row(ffn_pre_g, l) + tok, row(x_post_g, l)], rows_out=[F32, BF16], n_sums=2, epilogue=_bwd_rows)
        tok = grads_of(l, {n: G[n] for n in ('w_gate_up', 'w_down')}, dx)
        G['w_xo'] = _mm_tn("dw_xo", sv['o'], dxo, tm=d, tn=d, tk=tkw)
        do, = _mm_rows("bwd_w_xo", dxo, *W['w_xo'], 'nt', tm=tmp, params=[jnp.zeros((1, d), F32) + tok],
                       rows_out=[BF16], epilogue=_plain_rows)
        dq, dkv = _xattn_bwd("xattn_bwd", sv['q'], sv['kv'], do, 512)
        dkv = dkv.astype(BF16)
        G['w_xkv'] = _mm_tn("dw_xkv", memn, dkv, tm=d, tn=d, tk=mem.shape[1])
        dmemn = _mm_nt("bwd_w_xkv", dkv, *W['w_xkv'], tm=mem.shape[1], tn=512, tk=2 * d, out_dtype=F32, add=dmemn)
        G['w_xq'] = _mm_tn("dw_xq", sv['h2'], dq, tm=d, tn=d, tk=tkw)
        dx, dmix, gs['x_pre_g'][l], gs['mix_post_g'][l] = _mm_rows(
            "bwd_w_xq", dq, *W['w_xq'], 'nt', tm=tmr, rows_in=[sv['x1'], dx, sv['mix']],
            params=[row(x_pre_g, l), row(mix_post_g, l)], rows_out=[F32, BF16], n_sums=2, epilogue=_bwd_rows)
        tok = grads_of(l, {n: G[n] for n in ('w_xq', 'w_xkv', 'w_xo')}, dx)
        G['w_o'] = _mm_tn("dw_o", sv['abc'], dmix, tm=d, tn=d, tk=tkw)
        dabc, = _mm_rows("bwd_w_o", dmix, *W['w_o'], 'nt', tm=tmp, params=[jnp.zeros((1, d), F32) + tok],
                         rows_out=[F32], epilogue=_plain_rows)
        tok = grads_of(l, {'w_o': G['w_o']}, dabc)
        dz, dgv, dws, dbt, dpw, dpsc, dsnk = _mixer_bwd(
            "mixer_bwd", sv['z'], dabc, cosq, sinq, row(gm_v_g, l) + tok, gm_w_s[l], gm_b_s[l].T, pw_bd[l],
            row(pool_scale, l), row(attn_sinks, l))
        gs['gm_v_g'][l] = dgv
        gs['gm_w_s'][l] = dws
        gs['gm_b_s'][l] = dbt.T
        gs['pool_w'][l] = jnp.stack([dpw[64 * g:64 * (g + 1), 64 * g:64 * (g + 1)] for g in range(4)])
        gs['pool_scale'][l] = dpsc
        gs['attn_sinks'][l] = dsnk
        G['w_in'] = _mm_tn("dw_in", sv['h1'], dz, tm=d, tn=dz.shape[1], tk=tkw)
        if l > 0:
            dx, dfn, gs['mix_pre_g'][l], gs['ffn_post_g'][l - 1] = _mm_rows(
                "bwd_w_in", dz, *W['w_in'], 'nt', tm=tmr, rows_in=[sv['x0'], dx, saved[l - 1]['f']],
                params=[row(mix_pre_g, l), row(ffn_post_g, l - 1)], rows_out=[F32, BF16], n_sums=2,
                epilogue=_bwd_rows)
        else:
            dx, gs['mix_pre_g'][l] = _mm_rows(
                "bwd_w_in_first", dz, *W['w_in'], 'nt', tm=tmr, rows_in=[sv['x0'], dx],
                params=[row(mix_pre_g, l)], rows_out=[F32], n_sums=1, epilogue=_bwd_rows_first)
        tok = grads_of(l, {'w_in': G['w_in']}, dx)
    _, dg_mem = _norm_bwd("bwd_mem_norm", mem[0], mem_norm_g.reshape(1, d) + tok, dmemn, None, BF16, tmn)
    small_g = []
    for n in SMALL:
        if n == 'mem_norm_g':
            small_g.append(dg_mem.reshape(P[n].shape))
        else:
            small_g.append(jnp.stack([a.reshape(P[n].shape[1:]) for a in gs[n]]))
    return loss_part, dx, small_g
```

```python
import functools

import jax
import jax.numpy as jnp
from jax import lax
from jax.experimental import pallas as pl
from jax.experimental.pallas import tpu as pltpu

F32 = jnp.float32
BF16 = jnp.bfloat16
EPS = 1e-6
CHUNK = 128
HEAD = 64
ROPE_THETA = 10000.0
POOL_WINDOWS = (2, 4, 8, 16)
LR, B1, B2, ADAM_EPS, WD, STEP = 0.001, 0.9, 0.999, 1e-08, 0.01, 10
MESH = pl.DeviceIdType.MESH
VMEM_LIMIT = 56 * 1024 * 1024

NAMES = ['x', 'mem', 'positions', 'mem_norm_g', 'mix_pre_g', 'mix_post_g', 'w_in', 'gm_v_g', 'gm_w_s', 'gm_b_s',
         'pool_w', 'pool_scale', 'attn_sinks', 'w_o', 'x_pre_g', 'x_post_g', 'w_xq', 'w_xkv', 'w_xo', 'ffn_pre_g',
         'ffn_post_g', 'w_gate_up', 'w_down']
WEIGHTS = NAMES[3:]
BIG = ['w_in', 'w_o', 'w_xq', 'w_xkv', 'w_xo', 'w_gate_up', 'w_down']
BIG_AXIS = {'w_in': 2, 'w_o': 1, 'w_xq': 1, 'w_xkv': 2, 'w_xo': 1, 'w_gate_up': 2, 'w_down': 1}
SMALL = [n for n in WEIGHTS if n not in BIG]

NN = (((1,), (0,)), ((), ()))
NT = (((1,), (1,)), ((), ()))
TN = (((0,), (0,)), ((), ()))


def _dot(a, b, dims=NN):
    return lax.dot_general(a, b, dims, preferred_element_type=F32)


def _params(sem):
    return pltpu.CompilerParams(dimension_semantics=sem, vmem_limit_bytes=VMEM_LIMIT)


STREAM_BLOCK_BYTES = 3 * 512 * 1024


def _rows_tile(rows, cols):
    limit = max(16, STREAM_BLOCK_BYTES // (4 * cols))
    return max(t for t in range(16, min(rows, limit) + 1, 16) if rows % t == 0)


def _mm(name, a, a_spec, b, b_spec, dims, grid, nk, out_shape, out_spec, add=None, add_spec=None):
    acc_shape = out_spec.block_shape

    def body(*refs):
        a_ref, b_ref = refs[0], refs[1]
        pos = 2
        add_ref = None
        if add is not None:
            add_ref = refs[pos]
            pos += 1
        o_ref = refs[pos]
        part = _dot(a_ref[...].astype(BF16), b_ref[...].astype(BF16), dims)
        if nk == 1:
            if add_ref is not None:
                part = part + add_ref[...]
            o_ref[...] = part.astype(o_ref.dtype)
        else:
            acc_ref = refs[pos + 1]
            k = pl.program_id(2)

            @pl.when(k == 0)
            def _():
                acc_ref[...] = part if add_ref is None else part + add_ref[...]

            @pl.when(k > 0)
            def _():
                acc_ref[...] += part

            @pl.when(k == nk - 1)
            def _():
                o_ref[...] = acc_ref[...].astype(o_ref.dtype)

    ops, specs = [a, b], [a_spec, b_spec]
    if add is not None:
        ops.append(add)
        specs.append(add_spec)
    return pl.pallas_call(
        body, name=name, grid=grid, in_specs=specs, out_specs=out_spec, out_shape=out_shape,
        scratch_shapes=[pltpu.VMEM(acc_shape, F32)] if nk > 1 else [],
        compiler_params=_params(("parallel", "parallel", "arbitrary")),
    )(*ops)


def _wspec(block, layer, fn):
    return pl.BlockSpec((None,) + block, lambda i, j, k: (layer,) + fn(i, j, k))


def _mm_nn(name, a, w, layer, *, tm, tn, tk, out_dtype):
    m, kk = a.shape
    n = w.shape[2]
    tm = min(tm, m)
    nk = kk // tk
    return _mm(name, a, pl.BlockSpec((tm, tk), lambda i, j, k: (i, k)),
               w, _wspec((tk, tn), layer, lambda i, j, k: (k, j)), NN,
               (m // tm, n // tn, nk), nk, jax.ShapeDtypeStruct((m, n), out_dtype),
               pl.BlockSpec((tm, tn), lambda i, j, k: (i, j)))


def _mm_nt(name, a, w, layer, *, tm, tn, tk, out_dtype, add=None):
    m, kk = a.shape
    n = w.shape[1]
    tm = min(tm, m)
    nk = kk // tk
    ospec = pl.BlockSpec((tm, tn), lambda i, j, k: (i, j))
    return _mm(name, a, pl.BlockSpec((tm, tk), lambda i, j, k: (i, k)),
               w, _wspec((tn, tk), layer, lambda i, j, k: (j, k)), NT,
               (m // tm, n // tn, nk), nk, jax.ShapeDtypeStruct((m, n), out_dtype), ospec,
               add=add, add_spec=ospec if add is not None else None)


def _mm_tn(name, a, b, *, tm, tn, tk):
    kk, m = a.shape
    n = b.shape[1]
    tk = min(tk, kk)
    nk = kk // tk
    return _mm(name, a, pl.BlockSpec((tk, tm), lambda i, j, k: (k, i)),
               b, pl.BlockSpec((tk, tn), lambda i, j, k: (k, j)), TN,
               (m // tm, n // tn, nk), nk, jax.ShapeDtypeStruct((m, n), BF16),
               pl.BlockSpec((tm, tn), lambda i, j, k: (i, j)))


def _mm_rows(name, a, w, layer, mode, *, tm, rows_in=(), params=(), rows_out=(), n_sums=0, epilogue):
    m, kk = a.shape
    n = w.shape[2] if mode == 'nn' else w.shape[1]
    nr, npar, no = len(rows_in), len(params), len(rows_out)

    def body(*refs):
        a_ref, w_ref = refs[0], refs[1]
        rin = refs[2:2 + nr]
        par = refs[2 + nr:2 + nr + npar]
        outs = refs[2 + nr + npar:2 + nr + npar + no]
        sums = refs[2 + nr + npar + no:2 + nr + npar + no + n_sums]
        acc = _dot(a_ref[...], w_ref[...], NN if mode == 'nn' else NT)
        res, sm = epilogue(acc, [r[...] for r in rin], [p[...] for p in par])
        for r, v in zip(outs, res):
            r[...] = v.astype(r.dtype)

        @pl.when(pl.program_id(0) == 0)
        def _():
            for r in sums:
                r[...] = jnp.zeros_like(r)

        for r, v in zip(sums, sm):
            r[...] += v

    wblock = (None, kk, n) if mode == 'nn' else (None, n, kk)
    rowblk = pl.BlockSpec((tm, n), lambda i: (i, 0))
    one = pl.BlockSpec((1, n), lambda i: (0, 0))
    return pl.pallas_call(
        body, name=name, grid=(m // tm,),
        in_specs=[pl.BlockSpec((tm, kk), lambda i: (i, 0)),
                  pl.BlockSpec(wblock, lambda i: (layer, 0, 0), pipeline_mode=pl.Buffered(1))]
                 + [rowblk] * nr + [one] * npar,
        out_specs=[rowblk] * no + [one] * n_sums,
        out_shape=[jax.ShapeDtypeStruct((m, n), dt) for dt in rows_out] +
                  [jax.ShapeDtypeStruct((1, n), F32)] * n_sums,
        compiler_params=_params(("arbitrary",)),
    )(a, w, *rows_in, *params)


def _rstd(x):
    return lax.rsqrt(jnp.mean(x * x, axis=-1, keepdims=True) + EPS)


def _norm_back(xin, g, dy):
    r = _rstd(xin)
    xh = xin * r
    dyg = dy * g
    return r * (dyg - xh * jnp.mean(dyg * xh, axis=-1, keepdims=True)), jnp.sum(dy * xh, axis=0, keepdims=True)


def _plain_rows(acc, rows, pars):
    return [acc], []


def _post_pre_rows(y, rows, pars):
    xn = rows[0] + y * _rstd(y) * pars[0]
    return [y, xn, xn * _rstd(xn) * pars[1]], []


def _make_loss_rows(d):
    def fn(y, rows, pars):
        x, tgt = rows
        err = x + y * _rstd(y) * pars[0] - tgt
        dout = err * (1.0 / d)
        dy, dg = _norm_back(y, pars[0], dout)
        lsum = 0.5 * jnp.sum(jnp.mean(err * err, axis=-1, keepdims=True), axis=0, keepdims=True)
        return [dout, dy], [dg, jnp.broadcast_to(lsum, dg.shape)]
    return fn


def _bwd_rows(dh, rows, pars):
    xin, resid, yprev = rows
    dxa, dg_pre = _norm_back(xin, pars[0], dh)
    dx = resid + dxa
    dyp, dg_post = _norm_back(yprev.astype(F32), pars[1], dx)
    return [dx, dyp], [dg_pre, dg_post]


def _bwd_rows_first(dh, rows, pars):
    xin, resid = rows
    dxa, dg_pre = _norm_back(xin, pars[0], dh)
    return [resid + dxa], [dg_pre]


def _row(d):
    return pl.BlockSpec((1, d), lambda i: (0, 0))


def _prenorm(name, x, g, tm):
    m, d = x.shape

    def body(x_ref, g_ref, o_ref):
        xv = x_ref[...]
        o_ref[...] = (xv * _rstd(xv) * g_ref[...]).astype(BF16)

    blk = pl.BlockSpec((tm, d), lambda i: (i, 0))
    return pl.pallas_call(body, name=name, grid=(m // tm,), in_specs=[blk, _row(d)], out_specs=blk,
                          out_shape=jax.ShapeDtypeStruct((m, d), BF16), compiler_params=_params(("parallel",)))(x, g)


def _norm_bwd(name, xin, g, dy, resid, out_dtype, tm):
    m, d = xin.shape

    def body(*refs):
        if resid is None:
            x_ref, g_ref, dy_ref, dx_ref, dg_ref = refs
        else:
            x_ref, g_ref, dy_ref, r_ref, dx_ref, dg_ref = refs
        xv = x_ref[...]
        r = _rstd(xv)
        xh = xv * r
        dyv = dy_ref[...].astype(F32)
        dyg = dyv * g_ref[...]
        dx = r * (dyg - xh * jnp.mean(dyg * xh, axis=-1, keepdims=True))
        if resid is not None:
            dx = dx + r_ref[...]
        dx_ref[...] = dx.astype(dx_ref.dtype)

        @pl.when(pl.program_id(0) == 0)
        def _():
            dg_ref[...] = jnp.zeros_like(dg_ref)

        dg_ref[...] += jnp.sum(dyv * xh, axis=0, keepdims=True)

    blk = pl.BlockSpec((tm, d), lambda i: (i, 0))
    ops = [xin, g, dy] + ([] if resid is None else [resid])
    specs = [blk, _row(d), blk] + ([] if resid is None else [blk])
    return pl.pallas_call(
        body, name=name, grid=(m // tm,), in_specs=specs, out_specs=[blk, _row(d)],
        out_shape=[jax.ShapeDtypeStruct((m, d), out_dtype), jax.ShapeDtypeStruct((1, d), F32)],
        compiler_params=_params(("arbitrary",)))(*ops)


def _gelu_parts(x):
    c = 0.7978845608028654
    t = jnp.tanh(c * (x + 0.044715 * (x * x * x)))
    return 0.5 * x * (1.0 + t), t


def _gelu_grad(x, t):
    c = 0.7978845608028654
    return 0.5 * (1.0 + t) + 0.5 * x * (1.0 - t * t) * (c * (1.0 + 3.0 * 0.044715 * x * x))


def _rot_half(x):
    ax = x.ndim - 1
    w = x.shape[ax]
    lane = lax.broadcasted_iota(jnp.int32, x.shape, ax)
    return jnp.where((lane & 63) < 32, pltpu.roll(x, w - 32, ax), pltpu.roll(x, 32, ax))


def _group_mean(x, ones_bd):
    hi = x.astype(BF16)
    lo = (x - hi.astype(F32)).astype(BF16)
    return (_dot(hi, ones_bd) + _dot(lo, ones_bd)) * (1.0 / HEAD)


def _gating(gel, gv, ws_ref, bt, ones_bd, mix_s):
    u = gel[:, :256]
    v = gel[:, 256:]
    r = lax.rsqrt(_group_mean(v * v, ones_bd) + EPS)
    xh = v * r
    vn = (xh * gv).astype(BF16)
    row = lax.broadcasted_iota(jnp.int32, (CHUNK, CHUNK), 0)
    col = lax.broadcasted_iota(jnp.int32, (CHUNK, CHUNK), 1)
    causal = col <= row
    wcs = []
    for g in range(4):
        wc = jnp.where(causal, ws_ref[g], 0.0).astype(BF16)
        wcs.append(wc)
        mix_s[:, HEAD * g:HEAD * (g + 1)] = _dot(wc, vn[:, HEAD * g:HEAD * (g + 1)]) + bt[:, g:g + 1]
    return u, r, xh, vn, wcs, causal, mix_s[...]


def _lane_select(lane, vals):
    return jnp.where(lane < 64, vals[0], jnp.where(lane < 128, vals[1], jnp.where(lane < 192, vals[2], vals[3])))


def _pool_fwd(pc, pp, ci):
    ext = jnp.concatenate([pp, pc], axis=0)
    s2 = ext + pltpu.roll(ext, 1, 0)
    s4 = s2 + pltpu.roll(s2, 2, 0)
    s8 = s4 + pltpu.roll(s4, 4, 0)
    s16 = s8 + pltpu.roll(s8, 8, 0)
    t1 = ci * CHUNK + lax.broadcasted_iota(jnp.int32, (CHUNK, 1), 0) + 1
    lane = lax.broadcasted_iota(jnp.int32, (1, 256), 1)
    cnt = _lane_select(lane, [jnp.minimum(t1, w).astype(F32) for w in POOL_WINDOWS])
    ssel = _lane_select(lane, [s[CHUNK:] for s in (s2, s4, s8, s16)])
    return ssel / cnt - pc, cnt, lane


def _attn_prep(zc, zpkv, cc, sc, cp, sp):
    q = zc[:, 768:1280]
    kc = zc[:, 1280:1408]
    vc = zc[:, 1408:1536]
    kp = zpkv[:, :128]
    vp = zpkv[:, 128:]
    qr = q * jnp.concatenate([cc] * 4, axis=1) + _rot_half(q) * jnp.concatenate([sc] * 4, axis=1)
    krc = kc * cc + _rot_half(kc) * sc
    krp = kp * cp + _rot_half(kp) * sp
    kband = jnp.concatenate([krp, krc], axis=0)
    vband = jnp.concatenate([vp, vc], axis=0)
    return qr, kband, vband


def _attn_bias():
    key = lax.broadcasted_iota(jnp.int32, (2 * CHUNK, 4 * CHUNK), 0)
    t = lax.broadcasted_iota(jnp.int32, (2 * CHUNK, 4 * CHUNK), 1) & (CHUNK - 1)
    cur = (key >= CHUNK) & (key - CHUNK <= t)
    prev = (key < CHUNK) & (key > t)
    return jnp.stack([jnp.where(cur, 0.0, -1e30), jnp.where(cur | prev, 0.0, -1e30)]).astype(F32)


SCALE = HEAD ** -0.5


def _stack_heads(x, base, hk):
    return jnp.concatenate([x[:, base + HEAD * (4 * hk + i):base + HEAD * (4 * hk + i + 1)] for i in range(4)], axis=0)


def _sink_row(snk, hk):
    lane = lax.broadcasted_iota(jnp.int32, (1, 4 * CHUNK), 1)
    s = [snk[:, 4 * hk + i:4 * hk + i + 1] for i in range(4)]
    return jnp.where(lane < CHUNK, s[0], jnp.where(lane < 2 * CHUNK, s[1], jnp.where(lane < 3 * CHUNK, s[2], s[3])))


def _group_probs(kh, q4, bias, sink4):
    s = _dot(kh, q4, NT) + bias
    mx = jnp.maximum(jnp.max(s, axis=0, keepdims=True), sink4)
    e = jnp.exp(s - mx)
    es = jnp.exp(sink4 - mx)
    inv = 1.0 / (jnp.sum(e, axis=0, keepdims=True) + es)
    return e * inv, es * inv


def _mixer_specs(nb, rev):
    def cur(i):
        return nb - 1 - i if rev else i

    def prev(i):
        return jnp.maximum(cur(i) - 1, 0)

    full = lambda shape: pl.BlockSpec(shape, lambda i: (0,) * len(shape))
    specs = [
        pl.BlockSpec((CHUNK, 1536), lambda i: (cur(i), 0)),
        pl.BlockSpec((CHUNK, 256), lambda i: (prev(i), 2)),
        pl.BlockSpec((CHUNK, 256), lambda i: (prev(i), 5)),
        pl.BlockSpec((CHUNK, 128), lambda i: (cur(i), 0)),
        pl.BlockSpec((CHUNK, 128), lambda i: (cur(i), 0)),
        pl.BlockSpec((CHUNK, 128), lambda i: (prev(i), 0)),
        pl.BlockSpec((CHUNK, 128), lambda i: (prev(i), 0)),
        full((1, 256)), full((4, CHUNK, CHUNK)), full((CHUNK, 4)), full((256, 256)), full((1, 256)), full((1, 8)),
        full((256, 256)),
        pl.BlockSpec((None, 2 * CHUNK, 4 * CHUNK), lambda i: (jnp.minimum(cur(i), 1), 0, 0)),
    ]
    return specs, cur


def _ones_bd():
    g = lax.broadcasted_iota(jnp.int32, (256, 256), 0) // HEAD == lax.broadcasted_iota(jnp.int32, (256, 256), 1) // HEAD
    return g.astype(BF16)


def _mixer_fwd(name, z, cosq, sinq, gv, ws, bt, pw, psc, snk):
    s = z.shape[0]
    nb = s // CHUNK
    specs, _ = _mixer_specs(nb, False)

    def body(zc_ref, zpp_ref, zpkv_ref, cq_ref, sq_ref, cp_ref, sp_ref, gv_ref, ws_ref, bt_ref, pw_ref, psc_ref,
             snk_ref, bd_ref, bias_ref, o_ref, mix_s):
        ci = pl.program_id(0)
        zc = zc_ref[...]
        gel, _ = _gelu_parts(zc[:, :512])
        u, _, _, _, _, _, mixed = _gating(gel, gv_ref[...], ws_ref, bt_ref[...], bd_ref[...], mix_s)
        o_ref[:, :256] = (u * mixed).astype(BF16)
        pp = jnp.where(ci > 0, zpp_ref[...], 0.0)
        pooled, _, _ = _pool_fwd(zc[:, 512:768], pp, ci)
        mp = _dot(pooled.astype(BF16), pw_ref[...].astype(BF16))
        o_ref[:, 256:512] = (mp * psc_ref[...]).astype(BF16)
        qr, kband, vband = _attn_prep(zc, zpkv_ref[...], cq_ref[...], sq_ref[...], cp_ref[...], sp_ref[...])
        bias = bias_ref[...]
        snkv = snk_ref[...]
        kb = kband.astype(BF16)
        vt = vband.T
        ots = []
        for hk in range(2):
            q4 = (_stack_heads(qr, 0, hk) * SCALE).astype(BF16)
            p, _ = _group_probs(kb[:, HEAD * hk:HEAD * (hk + 1)], q4, bias, _sink_row(snkv, hk))
            ots.append(_dot(vt[HEAD * hk:HEAD * (hk + 1), :].astype(BF16), p.astype(BF16)))
        o = jnp.concatenate(ots, axis=0).T
        for hk in range(2):
            for i in range(4):
                h = 4 * hk + i
                o_ref[:, 512 + HEAD * h:512 + HEAD * (h + 1)] = o[CHUNK * i:CHUNK * (i + 1),
                                                                  HEAD * hk:HEAD * (hk + 1)].astype(BF16)

    return pl.pallas_call(
        body, name=name, grid=(nb,), in_specs=specs, out_specs=pl.BlockSpec((CHUNK, 1024), lambda i: (i, 0)),
        out_shape=jax.ShapeDtypeStruct((s, 1024), BF16), scratch_shapes=[pltpu.VMEM((CHUNK, 256), F32)],
        compiler_params=_params(("parallel",)),
    )(z, z, z, cosq, sinq, cosq, sinq, gv, ws, bt, pw, psc, snk, _ones_bd(), _attn_bias())


def _mixer_bwd(name, z, dabc, cosq, sinq, gv, ws, bt, pw, psc, snk):
    s = z.shape[0]
    nb = s // CHUNK
    specs, cur = _mixer_specs(nb, True)
    specs = specs + [pl.BlockSpec((CHUNK, 1024), lambda i: (cur(i), 0))]
    full = lambda shape: pl.BlockSpec(shape, lambda i: (0,) * len(shape))
    acc_shapes = [(1, 256), (4, CHUNK, CHUNK), (CHUNK, 4), (256, 256), (1, 256), (1, 8)]

    def body(zc_ref, zpp_ref, zpkv_ref, cq_ref, sq_ref, cp_ref, sp_ref, gv_ref, ws_ref, bt_ref, pw_ref, psc_ref,
             snk_ref, bd_ref, bias_ref, dabc_ref, dz_ref, dgv_ref, dws_ref, dbt_ref, dpw_ref, dpsc_ref, dsnk_ref,
             cpool, ck, cv, dq_s, dkv_s, mix_s, dvn_s):
        step = pl.program_id(0)
        ci = nb - 1 - step

        @pl.when(step == 0)
        def _():
            for r in (dgv_ref, dws_ref, dbt_ref, dpw_ref, dpsc_ref, dsnk_ref, cpool, ck, cv):
                r[...] = jnp.zeros_like(r)

        zc = zc_ref[...]
        dabc = dabc_ref[...]
        zg = zc[:, :512]
        gel, th = _gelu_parts(zg)
        gp = _gelu_grad(zg, th)
        gvv = gv_ref[...]
        bd = bd_ref[...]
        u, r, xh, vn, wcs, causal, mixed = _gating(gel, gvv, ws_ref, bt_ref[...], bd, mix_s)
        da = dabc[:, :256]
        dm = da * u
        dmb = dm.astype(BF16)
        lane4 = lax.broadcasted_iota(jnp.int32, (CHUNK, 4), 1)
        dbt = jnp.zeros((CHUNK, 4), F32)
        for g in range(4):
            lo, hi = HEAD * g, HEAD * (g + 1)
            dws_ref[g] += jnp.where(causal, _dot(dmb[:, lo:hi], vn[:, lo:hi], NT), 0.0)
            dbt = dbt + jnp.where(lane4 == g, jnp.sum(dm[:, lo:hi], axis=-1, keepdims=True), 0.0)
            dvn_s[:, lo:hi] = _dot(wcs[g], dmb[:, lo:hi], TN)
        dbt_ref[...] += dbt
        dvn = dvn_s[...]
        dgv_ref[...] += jnp.sum(dvn * xh, axis=0, keepdims=True)
        dxh = dvn * gvv
        dvg = r * (dxh - xh * _group_mean(dxh * xh, bd))
        dz_ref[:, :256] = (da * mixed * gp[:, :256]).astype(BF16)
        dz_ref[:, 256:512] = (dvg * gp[:, 256:]).astype(BF16)
        pc = zc[:, 512:768]
        pp = jnp.where(ci > 0, zpp_ref[...], 0.0)
        pooled, cnt, lane = _pool_fwd(pc, pp, ci)
        pwb = pw_ref[...].astype(BF16)
        pooled_b = pooled.astype(BF16)
        mp = _dot(pooled_b, pwb)
        db = dabc[:, 256:512]
        dpsc_ref[...] += jnp.sum(db * mp, axis=0, keepdims=True)
        dmpb = (db * psc_ref[...]).astype(BF16)
        dpw_ref[...] += _dot(pooled_b, dmpb, TN)
        dpooled = _dot(dmpb, pwb, NT)
        davg = dpooled / cnt
        zero = jnp.zeros((CHUNK, 256), F32)
        d2, d4, d8, d16 = [jnp.concatenate([zero, jnp.where((lane >= 64 * k) & (lane < 64 * (k + 1)), davg, 0.0)],
                                           axis=0) for k in range(4)]
        g8 = d8 + d16 + pltpu.roll(d16, 2 * CHUNK - 8, 0)
        g4 = d4 + g8 + pltpu.roll(g8, 2 * CHUNK - 4, 0)
        g2 = d2 + g4 + pltpu.roll(g4, 2 * CHUNK - 2, 0)
        ge = g2 + pltpu.roll(g2, 2 * CHUNK - 1, 0)
        dz_ref[:, 512:768] = (ge[CHUNK:] - dpooled + cpool[...]).astype(BF16)
        cpool[...] = ge[:CHUNK]
        cc = cq_ref[...]
        sc = sq_ref[...]
        qr, kband, vband = _attn_prep(zc, zpkv_ref[...], cc, sc, cp_ref[...], sp_ref[...])
        bias = bias_ref[...]
        snkv = snk_ref[...]
        lane8 = lax.broadcasted_iota(jnp.int32, (1, 8), 1)
        qlane = lax.broadcasted_iota(jnp.int32, (1, 4 * CHUNK), 1)
        dsnk = jnp.zeros((1, 8), F32)
        kb = kband.astype(BF16)
        vb = vband.astype(BF16)
        kt = kband.T * SCALE
        dqts = []
        for hk in range(2):
            kh = kb[:, HEAD * hk:HEAD * (hk + 1)]
            q4 = (_stack_heads(qr, 0, hk) * SCALE).astype(BF16)
            do4 = _stack_heads(dabc, 512, hk).astype(BF16)
            p, ps = _group_probs(kh, q4, bias, _sink_row(snkv, hk))
            dp = _dot(vb[:, HEAD * hk:HEAD * (hk + 1)], do4, NT)
            dd = jnp.sum(p * dp, axis=0, keepdims=True)
            dsink = -ps * dd
            for i in range(4):
                part = jnp.sum(jnp.where((qlane >= CHUNK * i) & (qlane < CHUNK * (i + 1)), dsink, 0.0),
                               axis=1, keepdims=True)
                dsnk = dsnk + jnp.where(lane8 == 4 * hk + i, part, 0.0)
            dsb = (p * (dp - dd)).astype(BF16)
            dqts.append(_dot(kt[HEAD * hk:HEAD * (hk + 1), :].astype(BF16), dsb))
            dkv_s[:, HEAD * hk:HEAD * (hk + 1)] = _dot(dsb, q4)
            dkv_s[:, 128 + HEAD * hk:128 + HEAD * (hk + 1)] = _dot(p.astype(BF16), do4)
        dq4 = jnp.concatenate(dqts, axis=0).T
        for hk in range(2):
            for i in range(4):
                h = 4 * hk + i
                dq_s[:, HEAD * h:HEAD * (h + 1)] = dq4[CHUNK * i:CHUNK * (i + 1), HEAD * hk:HEAD * (hk + 1)]
        dsnk_ref[...] += dsnk
        dqr = dq_s[...]
        dz_ref[:, 768:1280] = (dqr * jnp.concatenate([cc] * 4, axis=1)
                               + _rot_half(dqr * jnp.concatenate([sc] * 4, axis=1))).astype(BF16)
        dkv = dkv_s[...]
        dkr = dkv[CHUNK:, :128] + ck[...]
        dz_ref[:, 1280:1408] = (dkr * cc + _rot_half(dkr * sc)).astype(BF16)
        dz_ref[:, 1408:1536] = (dkv[CHUNK:, 128:] + cv[...]).astype(BF16)
        ck[...] = dkv[:CHUNK, :128]
        cv[...] = dkv[:CHUNK, 128:]

    return pl.pallas_call(
        body, name=name, grid=(nb,), in_specs=specs,
        out_specs=[pl.BlockSpec((CHUNK, 1536), lambda i: (cur(i), 0))] + [full(a) for a in acc_shapes],
        out_shape=[jax.ShapeDtypeStruct((s, 1536), BF16)] + [jax.ShapeDtypeStruct(a, F32) for a in acc_shapes],
        scratch_shapes=[pltpu.VMEM((CHUNK, 256), F32), pltpu.VMEM((CHUNK, 128), F32), pltpu.VMEM((CHUNK, 128), F32),
                        pltpu.VMEM((CHUNK, 512), F32), pltpu.VMEM((2 * CHUNK, 256), F32),
                        pltpu.VMEM((CHUNK, 256), F32), pltpu.VMEM((CHUNK, 256), F32)],
        compiler_params=_params(("arbitrary",)),
    )(z, z, z, cosq, sinq, cosq, sinq, gv, ws, bt, pw, psc, snk, _ones_bd(), _attn_bias(), dabc)


def _xattn_probs(qh, kh):
    s = _dot(qh, kh, NT) * (256 ** -0.5)
    e = jnp.exp(s - jnp.max(s, axis=-1, keepdims=True))
    return e * (1.0 / jnp.sum(e, axis=-1, keepdims=True))


def _xattn_fwd(name, q, kv, tq):
    s, d = q.shape
    mlen = kv.shape[0]

    def body(q_ref, kv_ref, o_ref):
        for h in range(4):
            lo, hi = 256 * h, 256 * (h + 1)
            p = _xattn_probs(q_ref[:, lo:hi], kv_ref[:, lo:hi])
            o_ref[:, lo:hi] = _dot(p.astype(BF16), kv_ref[:, d + lo:d + hi]).astype(BF16)

    blk = pl.BlockSpec((tq, d), lambda i: (i, 0))
    return pl.pallas_call(body, name=name, grid=(s // tq,),
                          in_specs=[blk, pl.BlockSpec((mlen, 2 * d), lambda i: (0, 0))], out_specs=blk,
                          out_shape=jax.ShapeDtypeStruct((s, d), BF16), compiler_params=_params(("parallel",)))(q, kv)


def _xattn_bwd(name, q, kv, do, tq):
    s, d = q.shape
    mlen = kv.shape[0]

    def body(q_ref, kv_ref, do_ref, dq_ref, dkv_ref):
        @pl.when(pl.program_id(0) == 0)
        def _():
            dkv_ref[...] = jnp.zeros_like(dkv_ref)

        for h in range(4):
            lo, hi = 256 * h, 256 * (h + 1)
            qh = q_ref[:, lo:hi]
            kh = kv_ref[:, lo:hi]
            vh = kv_ref[:, d + lo:d + hi]
            doh = do_ref[:, lo:hi]
            p = _xattn_probs(qh, kh)
            dp = _dot(doh, vh, NT)
            dsb = (p * (dp - jnp.sum(p * dp, axis=-1, keepdims=True)) * (256 ** -0.5)).astype(BF16)
            dq_ref[:, lo:hi] = _dot(dsb, kh).astype(BF16)
            dkv_ref[:, lo:hi] += _dot(dsb, qh, TN)
            dkv_ref[:, d + lo:d + hi] += _dot(p.astype(BF16), doh, TN)

    blk = pl.BlockSpec((tq, d), lambda i: (i, 0))
    kvb = pl.BlockSpec((mlen, 2 * d), lambda i: (0, 0))
    return pl.pallas_call(
        body, name=name, grid=(s // tq,), in_specs=[blk, kvb, blk], out_specs=[blk, kvb],
        out_shape=[jax.ShapeDtypeStruct((s, d), BF16), jax.ShapeDtypeStruct((mlen, 2 * d), F32)],
        compiler_params=_params(("arbitrary",)))(q, kv, do)


def _sigmoid(x):
    return 0.5 * (1.0 + jnp.tanh(0.5 * x))


def _ffn_up(name, h, wgu, layer, tm, tn):
    s, d = h.shape
    dff = wgu.shape[2] // 2
    nj = dff // tn

    def body(h_ref, wg_ref, wu_ref, g_ref, u_ref, a_ref):
        hv = h_ref[...]
        gate = _dot(hv, wg_ref[...])
        up = _dot(hv, wu_ref[...])
        g_ref[...] = gate.astype(BF16)
        u_ref[...] = up.astype(BF16)
        a_ref[...] = (gate * _sigmoid(gate) * up).astype(BF16)

    ob = pl.BlockSpec((tm, tn), lambda j, i: (i, j))
    sd = jax.ShapeDtypeStruct((s, dff), BF16)
    return pl.pallas_call(
        body, name=name, grid=(nj, s // tm),
        in_specs=[pl.BlockSpec((tm, d), lambda j, i: (i, 0)),
                  pl.BlockSpec((None, d, tn), lambda j, i: (layer, 0, j)),
                  pl.BlockSpec((None, d, tn), lambda j, i: (layer, 0, j + nj))],
        out_specs=[ob, ob, ob], out_shape=[sd, sd, sd], compiler_params=_params(("parallel", "parallel")),
    )(h, wgu, wgu)


def _ffn_act_bwd(name, dfn, wdown, layer, gate, up, tm):
    s, d = dfn.shape
    dff = gate.shape[1]

    def body(df_ref, wd_ref, g_ref, u_ref, o_ref):
        dact = _dot(df_ref[...], wd_ref[...], NT)
        gate = g_ref[...].astype(F32)
        sig = _sigmoid(gate)
        o_ref[:, :dff] = (dact * u_ref[...].astype(F32) * sig * (1.0 + gate * (1.0 - sig))).astype(BF16)
        o_ref[:, dff:] = (dact * gate * sig).astype(BF16)

    gb = pl.BlockSpec((tm, dff), lambda i: (i, 0))
    return pl.pallas_call(
        body, name=name, grid=(s // tm,),
        in_specs=[pl.BlockSpec((tm, d), lambda i: (i, 0)),
                  pl.BlockSpec((None, dff, d), lambda i: (layer, 0, 0)), gb, gb],
        out_specs=pl.BlockSpec((tm, 2 * dff), lambda i: (i, 0)),
        out_shape=jax.ShapeDtypeStruct((s, 2 * dff), BF16), compiler_params=_params(("parallel",)),
    )(dfn, wdown, gate, up)


def _place():
    return lax.axis_index("x"), lax.axis_index("y"), lax.axis_index("c")


def _other_chips(x, y):
    return [(1 - x, y), (x, 1 - y), (1 - x, 1 - y)]


def _region(ref, axis, chip, size):
    start = pl.multiple_of(chip * size, size)
    if axis == 1:
        return ref.at[:, pl.ds(start, size), :]
    return ref.at[:, :, pl.ds(start, size)]


ANY = pl.BlockSpec(memory_space=pl.ANY)


HBM = pl.BlockSpec(memory_space=pltpu.HBM)
SEM = pl.BlockSpec(memory_space=pltpu.SEMAPHORE)
EFFECT = pltpu.SideEffectType.DATAFLOW_SIDE_EFFECTING


def _in_hbm(a):
    return pltpu.with_memory_space_constraint(a, pltpu.HBM)


def _split_start(name, srcs, lands, ncopies, plan):
    ns, nl = len(srcs), len(lands)

    def body(*refs):
        src, land = refs[:ns], refs[ns:ns + nl]
        send, recv = refs[ns + nl], refs[ns + nl + 1]
        token = refs[-1]
        x, y, c = _place()
        for k, (s_ref, d_ref, peer, _) in enumerate(plan(src, land, x, y, c)):
            pltpu.make_async_remote_copy(src_ref=s_ref, dst_ref=d_ref, send_sem=send.at[k], recv_sem=recv.at[k],
                                         device_id=peer, device_id_type=MESH).start()
        token[...] = jnp.zeros_like(token)

    ops = list(srcs) + list(lands)
    out = pl.pallas_call(
        body, name=name,
        out_shape=(pltpu.SemaphoreType.DMA((ncopies,)), pltpu.SemaphoreType.DMA((ncopies,)),
                   *[pltpu.HBM(a.shape, a.dtype) for a in ops], jax.ShapeDtypeStruct((8, 128), F32)),
        in_specs=(HBM,) * (ns + nl),
        out_specs=(SEM, SEM) + (HBM,) * (ns + nl) + (pl.BlockSpec(memory_space=pltpu.VMEM),),
        input_output_aliases={i: 2 + i for i in range(ns + nl)},
        compiler_params=pltpu.CompilerParams(has_side_effects=EFFECT),
    )(*[_in_hbm(a) for a in ops])
    return out[0], out[1], list(out[2:2 + ns]), list(out[2 + ns:2 + ns + nl]), out[-1]


def _split_start_many(name, lands, jobs):
    nl, nj = len(lands), len(jobs)

    def body(*refs):
        land = refs[:nl]
        sems = refs[nl:nl + 2 * nj]
        token = refs[-1]
        x, y, c = _place()
        for j, (idx, _, plan) in enumerate(jobs):
            for k, (s_ref, d_ref, peer, _) in enumerate(plan((), [land[t] for t in idx], x, y, c)):
                pltpu.make_async_remote_copy(src_ref=s_ref, dst_ref=d_ref, send_sem=sems[2 * j].at[k],
                                             recv_sem=sems[2 * j + 1].at[k], device_id=peer,
                                             device_id_type=MESH).start()
        token[...] = jnp.zeros_like(token)

    sem_shapes = tuple(pltpu.SemaphoreType.DMA((n,)) for _, n, _ in jobs for _ in range(2))
    out = pl.pallas_call(
        body, name=name,
        out_shape=sem_shapes + tuple(pltpu.HBM(a.shape, a.dtype) for a in lands)
        + (jax.ShapeDtypeStruct((8, 128), F32),),
        in_specs=(HBM,) * nl,
        out_specs=(SEM,) * (2 * nj) + (HBM,) * nl + (pl.BlockSpec(memory_space=pltpu.VMEM),),
        input_output_aliases={i: 2 * nj + i for i in range(nl)},
        compiler_params=pltpu.CompilerParams(has_side_effects=EFFECT),
    )(*[_in_hbm(a) for a in lands])
    return [(out[2 * j], out[2 * j + 1]) for j in range(nj)], list(out[2 * nj:2 * nj + nl]), out[-1]


def _split_wait(name, send, recv, srcs, lands, after, plan):
    ns, nl = len(srcs), len(lands)

    def body(*refs):
        src, land = refs[:ns], refs[ns:ns + nl]
        send_ref, recv_ref = refs[ns + nl], refs[ns + nl + 1]
        x, y, c = _place()
        for k, (s_ref, _, _, got) in enumerate(plan(src, land, x, y, c)):
            cp = pltpu.make_async_remote_copy(src_ref=s_ref, dst_ref=got, send_sem=send_ref.at[k],
                                              recv_sem=recv_ref.at[k], device_id=(x, y, c), device_id_type=MESH)
            cp.wait_send()
            cp.wait_recv()

    ops = list(srcs) + list(lands)
    out = pl.pallas_call(
        body, name=name, out_shape=tuple(pltpu.HBM(a.shape, a.dtype) for a in ops),
        in_specs=(HBM,) * (ns + nl) + (SEM, SEM, ANY), out_specs=(HBM,) * (ns + nl),
        input_output_aliases={i: i for i in range(ns + nl)},
        compiler_params=pltpu.CompilerParams(has_side_effects=EFFECT),
    )(*ops, send, recv, after)
    return list(out[:ns]), list(out[ns:])


def _split_wait_start(name, send, recv, lands, after, wait_plan, ncopies, start_plan, carried=()):
    nl, nc = len(lands), len(carried)
    lands = list(lands) + list(carried)

    def body(*refs):
        land = refs[:nl]
        send_in, recv_in = refs[nl + nc], refs[nl + nc + 1]
        send_out, recv_out = refs[nl + nc + 3], refs[nl + nc + 4]
        x, y, c = _place()
        for k, (s_ref, _, _, got) in enumerate(wait_plan((), land, x, y, c)):
            cp = pltpu.make_async_remote_copy(src_ref=s_ref, dst_ref=got, send_sem=send_in.at[k],
                                              recv_sem=recv_in.at[k], device_id=(x, y, c), device_id_type=MESH)
            cp.wait_send()
            cp.wait_recv()
        for k, (s_ref, d_ref, peer, _) in enumerate(start_plan((), land, x, y, c)):
            pltpu.make_async_remote_copy(src_ref=s_ref, dst_ref=d_ref, send_sem=send_out.at[k],
                                         recv_sem=recv_out.at[k], device_id=peer, device_id_type=MESH).start()

    out = pl.pallas_call(
        body, name=name,
        out_shape=(pltpu.SemaphoreType.DMA((ncopies,)), pltpu.SemaphoreType.DMA((ncopies,)),
                   *[pltpu.HBM(a.shape, a.dtype) for a in lands]),
        in_specs=(HBM,) * (nl + nc) + (SEM, SEM, ANY), out_specs=(SEM, SEM) + (HBM,) * (nl + nc),
        input_output_aliases={i: 2 + i for i in range(nl + nc)},
        compiler_params=pltpu.CompilerParams(has_side_effects=EFFECT),
    )(*lands, send, recv, after)
    return out[0], out[1], list(out[2:2 + nl]), list(out[2 + nl:])


def _half(ref, axis, chip, size, layer, h):
    reg = _region(ref, axis, chip, size).at[pl.ds(layer, 1)]
    rows = reg.shape[1] // 2
    return reg.at[:, pl.ds(pl.multiple_of(h * rows, rows), rows), :]


def _gather_plan(axes, sizes, layer):
    def plan(src, land, x, y, c):
        me = 2 * x + y
        out = []
        for t in range(len(land)):
            mine = _half(land[t], axes[t], me, sizes[t], layer, c)
            for px, py in _other_chips(x, y):
                out.append((mine, mine, (px, py, c), _half(land[t], axes[t], 2 * px + py, sizes[t], layer, c)))
        return out
    return plan


def _forward_plan(axes, sizes, layer):
    def plan(src, land, x, y, c):
        out = []
        for t in range(len(land)):
            for px, py in _other_chips(x, y):
                got = _half(land[t], axes[t], 2 * px + py, sizes[t], layer, c)
                out.append((got, got, (x, y, 1 - c), _half(land[t], axes[t], 2 * px + py, sizes[t], layer, 1 - c)))
        return out
    return plan


def _place_own(name, w, axis, chip):
    nl, r, cs = w.shape
    tr = _rows_tile(r, cs)
    nb = r // tr
    full = (nl, 4 * r, cs) if axis == 1 else (nl, r, 4 * cs)

    def body(m_ref, w_ref, o_ref):
        o_ref[...] = w_ref[...].astype(BF16)

    if axis == 1:
        ospec = pl.BlockSpec((None, tr, cs), lambda l, i, m: (l, m[0] * nb + i, 0))
    else:
        ospec = pl.BlockSpec((None, tr, cs), lambda l, i, m: (l, i, m[0]))
    return pl.pallas_call(
        body, name=name,
        grid_spec=pltpu.PrefetchScalarGridSpec(
            num_scalar_prefetch=1, grid=(nl, nb),
            in_specs=[pl.BlockSpec((None, tr, cs), lambda l, i, m: (l, i, 0))], out_specs=ospec),
        out_shape=jax.ShapeDtypeStruct(full, BF16), compiler_params=_params(("parallel", "parallel")),
    )(chip, w)


def _scatter_plan(axes, sizes):
    def plan(src, land, x, y, c):
        out = []
        for t in range(len(src)):
            for k, (px, py) in enumerate(_other_chips(x, y)):
                out.append((_region(src[t], axes[t], 2 * px + py, sizes[t]).at[0], land[t].at[k], (px, py, c),
                            land[t].at[k]))
        return out
    return plan


def _pair_plan(src, land, x, y, c):
    return [(src[t], land[t], (x, y, 1 - c), land[t]) for t in range(len(src))]


UNIT_STEPS = 16


def _unit_rows(r):
    return min(t for t in range(16, r + 1, 16) if r % t == 0 and r // t <= UNIT_STEPS)


def _guarded(i, nb, steps, work):
    if nb == steps:
        work()
    else:
        pl.when(i < nb)(work)


def _unit_chip_sum(name, gs, slots, axes, chip):
    n = len(gs)
    dims = [s.shape[1:] for s in slots]
    trs = [_unit_rows(r) for r, _ in dims]
    nbs = [r // tr for (r, _), tr in zip(dims, trs)]
    steps = max(nbs)

    def body(m_ref, *refs):
        i = pl.program_id(0)
        for t in range(n):
            def work(t=t):
                acc = refs[t][...].astype(F32)
                for k in range(3):
                    acc = acc + refs[n + t][k].astype(F32)
                refs[2 * n + t][...] = acc.astype(BF16)
            _guarded(i, nbs[t], steps, work)

    gspecs, sspecs, ospecs = [], [], []
    for (r, cs), tr, nb, axis in zip(dims, trs, nbs, axes):
        if axis == 1:
            gspecs.append(pl.BlockSpec((tr, cs), lambda i, m, nb=nb: (m[0] * nb + jnp.minimum(i, nb - 1), 0)))
        else:
            gspecs.append(pl.BlockSpec((tr, cs), lambda i, m, nb=nb: (jnp.minimum(i, nb - 1), m[0])))
        sspecs.append(pl.BlockSpec((3, tr, cs), lambda i, m, nb=nb: (0, jnp.minimum(i, nb - 1), 0)))
        ospecs.append(pl.BlockSpec((tr, cs), lambda i, m, nb=nb: (jnp.minimum(i, nb - 1), 0)))
    return pl.pallas_call(
        body, name=name,
        grid_spec=pltpu.PrefetchScalarGridSpec(num_scalar_prefetch=1, grid=(steps,), in_specs=gspecs + sspecs,
                                               out_specs=ospecs),
        out_shape=[jax.ShapeDtypeStruct(d, BF16) for d in dims], compiler_params=_params(("arbitrary",)),
    )(chip, *gs, *slots)


def _unit_adamw(name, mine, theirs, ws, ms, vs, layer, bufs):
    n = len(mine)
    dims = [a.shape for a in mine]
    trs = [_unit_rows(r) for r, _ in dims]
    nbs = [r // tr for (r, _), tr in zip(dims, trs)]
    steps = max(nbs)
    c1 = 1.0 - B1 ** STEP
    c2 = 1.0 - B2 ** STEP

    def body(*refs):
        i = pl.program_id(0)
        outs = refs[9 * n:]
        for t in range(n):
            def work(t=t):
                a_ref, b_ref, w_ref, m_ref, v_ref = refs[5 * t:5 * t + 5]
                g_ref, d_ref, mo_ref, vo_ref = outs[4 * t:4 * t + 4]
                gv = a_ref[...].astype(F32) + b_ref[...].astype(F32)
                mn = B1 * m_ref[...] + (1.0 - B1) * gv
                vn = B2 * v_ref[...] + (1.0 - B2) * (gv * gv)
                g_ref[...] = gv
                mo_ref[...] = mn
                vo_ref[...] = vn
                d_ref[...] = -LR * ((mn / c1) / (jnp.sqrt(vn / c2) + ADAM_EPS) + WD * w_ref[...])
            _guarded(i, nbs[t], steps, work)

    in_specs, out_specs, ops = [], [], []
    for t, ((r, cs), tr, nb) in enumerate(zip(dims, trs, nbs)):
        blk = pl.BlockSpec((tr, cs), lambda i, nb=nb: (jnp.minimum(i, nb - 1), 0))
        lay = pl.BlockSpec((None, tr, cs), lambda i, nb=nb: (layer, jnp.minimum(i, nb - 1), 0))
        in_specs += [blk, blk, lay, lay, lay]
        out_specs += [lay] * 4
        ops += [mine[t], theirs[t], ws[t], ms[t], vs[t]]
    flat = [b for bs in bufs for b in bs]
    out = pl.pallas_call(
        body, name=name, grid=(steps,), in_specs=in_specs + [ANY] * (4 * n), out_specs=out_specs,
        out_shape=[jax.ShapeDtypeStruct(b.shape, b.dtype) for b in flat],
        input_output_aliases={5 * n + k: k for k in range(4 * n)}, compiler_params=_params(("arbitrary",)),
    )(*ops, *flat)
    return [list(out[4 * t:4 * t + 4]) for t in range(n)]


def _small_pair_sum(p):
    rows = p.shape[0]
    half = rows // 2

    def body(p_ref, o_ref, sib, send, recv):
        x, y, c = _place()
        mine = pl.ds(pl.multiple_of(c * half, half), half)
        theirs = pl.ds(pl.multiple_of((1 - c) * half, half), half)
        pair = pltpu.make_async_remote_copy(src_ref=p_ref.at[theirs], dst_ref=sib, send_sem=send, recv_sem=recv,
                                            device_id=(x, y, 1 - c), device_id_type=MESH)
        pair.start()
        pair.wait()
        o_ref[...] = (p_ref[mine] + sib[...]).astype(BF16)

    vm = pl.BlockSpec(memory_space=pltpu.VMEM)
    return pl.pallas_call(
        body, name="small_pair_sum", in_specs=[vm], out_specs=vm, out_shape=jax.ShapeDtypeStruct((half, 128), BF16),
        scratch_shapes=[pltpu.VMEM((half, 128), F32), pltpu.SemaphoreType.DMA, pltpu.SemaphoreType.DMA],
        compiler_params=pltpu.CompilerParams(vmem_limit_bytes=VMEM_LIMIT),
    )(p)


def _small_plan(src, land, x, y, c):
    return [(src[0], land[0].at[k], (px, py, c), land[0].at[k]) for k, (px, py) in enumerate(_other_chips(x, y))]


def _small_total(own, slots):
    half = own.shape[0]

    def body(own_ref, slots_ref, o_ref, sums, send, recv):
        x, y, c = _place()
        me = 2 * x + y
        mine = pl.ds(pl.multiple_of(c * half, half), half)
        theirs = pl.ds(pl.multiple_of((1 - c) * half, half), half)
        sums[me] = own_ref[...]
        for k, (px, py) in enumerate(_other_chips(x, y)):
            sums[2 * px + py] = slots_ref[k]
        acc = sums[0].astype(F32)
        for k in range(1, 4):
            acc = acc + sums[k].astype(F32)
        o_ref[mine] = acc
        back = pltpu.make_async_remote_copy(src_ref=o_ref.at[mine], dst_ref=o_ref.at[mine], send_sem=send,
                                            recv_sem=recv, device_id=(x, y, 1 - c), device_id_type=MESH)
        back.start()
        pltpu.make_async_remote_copy(src_ref=o_ref.at[theirs], dst_ref=o_ref.at[theirs], send_sem=send,
                                     recv_sem=recv, device_id=(x, y, c), device_id_type=MESH).wait_recv()
        back.wait_send()

    vm = pl.BlockSpec(memory_space=pltpu.VMEM)
    return pl.pallas_call(
        body, name="small_total", in_specs=[vm, vm], out_specs=vm,
        out_shape=jax.ShapeDtypeStruct((2 * half, 128), F32),
        scratch_shapes=[pltpu.VMEM((4, half, 128), BF16), pltpu.SemaphoreType.DMA, pltpu.SemaphoreType.DMA],
        compiler_params=pltpu.CompilerParams(vmem_limit_bytes=VMEM_LIMIT),
    )(own, slots)


def _small_adamw(gs, ws, ms, vs):
    n = len(gs)
    c1 = 1.0 - B1 ** STEP
    c2 = 1.0 - B2 ** STEP

    def body(*refs):
        for t in range(n):
            gv, wv = refs[t][...], refs[n + t][...]
            d_ref, mo_ref, vo_ref = refs[4 * n + 3 * t:4 * n + 3 * t + 3]
            mn = B1 * refs[2 * n + t][...] + (1.0 - B1) * gv
            vn = B2 * refs[3 * n + t][...] + (1.0 - B2) * (gv * gv)
            mo_ref[...] = mn
            vo_ref[...] = vn
            d_ref[...] = -LR * ((mn / c1) / (jnp.sqrt(vn / c2) + ADAM_EPS) + WD * wv)

    vm = pl.BlockSpec(memory_space=pltpu.VMEM)
    out = pl.pallas_call(
        body, name="adamw_small", in_specs=[vm] * (4 * n), out_specs=[vm] * (3 * n),
        out_shape=[jax.ShapeDtypeStruct(w.shape, F32) for w in ws for _ in range(3)],
        compiler_params=pltpu.CompilerParams(vmem_limit_bytes=VMEM_LIMIT),
    )(*gs, *ws, *ms, *vs)
    return [tuple(out[3 * t:3 * t + 3]) for t in range(n)]


def _pack(parts):
    flat = []
    for p in parts:
        v = p.reshape(-1).astype(F32)
        flat.append(jnp.pad(v, (0, (-v.shape[0]) % 128)))
    v = jnp.concatenate(flat)
    v = jnp.pad(v, (0, (-v.shape[0]) % (512 * 128)))
    return v.reshape(-1, 128)


def _unpack(buf, like):
    out, r0 = [], 0
    for p in like:
        nelem = 1
        for s in p.shape:
            nelem *= s
        rows = -(-nelem // 128)
        blk = buf[r0:r0 + rows]
        if nelem % 128:
            blk = blk.reshape(-1)[:nelem]
        out.append(blk.reshape(p.shape))
        r0 += rows
    return out


def kernel(x, mem, positions, mem_norm_g, mix_pre_g, mix_post_g, w_in, gm_v_g, gm_w_s, gm_b_s, pool_w, pool_scale, attn_sinks, w_o, x_pre_g, x_post_g, w_xq, w_xkv, w_xo, ffn_pre_g, ffn_post_g, w_gate_up, w_down, loss_target, m_mem_norm_g, m_mix_pre_g, m_mix_post_g, m_w_in, m_gm_v_g, m_gm_w_s, m_gm_b_s, m_pool_w, m_pool_scale, m_attn_sinks, m_w_o, m_x_pre_g, m_x_post_g, m_w_xq, m_w_xkv, m_w_xo, m_ffn_pre_g, m_ffn_post_g, m_w_gate_up, m_w_down, v_mem_norm_g, v_mix_pre_g, v_mix_post_g, v_w_in, v_gm_v_g, v_gm_w_s, v_gm_b_s, v_pool_w, v_pool_scale, v_attn_sinks, v_w_o, v_x_pre_g, v_x_post_g, v_w_xq, v_w_xkv, v_w_xo, v_ffn_pre_g, v_ffn_post_g, v_w_gate_up, v_w_down):
    args = (x, mem, positions, mem_norm_g, mix_pre_g, mix_post_g, w_in, gm_v_g, gm_w_s, gm_b_s, pool_w, pool_scale, attn_sinks, w_o, x_pre_g, x_post_g, w_xq, w_xkv, w_xo, ffn_pre_g, ffn_post_g, w_gate_up, w_down)
    moms_m = (m_mem_norm_g, m_mix_pre_g, m_mix_post_g, m_w_in, m_gm_v_g, m_gm_w_s, m_gm_b_s, m_pool_w, m_pool_scale, m_attn_sinks, m_w_o, m_x_pre_g, m_x_post_g, m_w_xq, m_w_xkv, m_w_xo, m_ffn_pre_g, m_ffn_post_g, m_w_gate_up, m_w_down)
    moms_v = (v_mem_norm_g, v_mix_pre_g, v_mix_post_g, v_w_in, v_gm_v_g, v_gm_w_s, v_gm_b_s, v_pool_w, v_pool_scale, v_attn_sinks, v_w_o, v_x_pre_g, v_x_post_g, v_w_xq, v_w_xkv, v_w_xo, v_ffn_pre_g, v_ffn_post_g, v_w_gate_up, v_w_down)
    P = dict(zip(NAMES, args))
    P['loss_target'] = loss_target
    M = dict(zip(WEIGHTS, moms_m))
    V = dict(zip(WEIGHTS, moms_v))
    depth = w_in.shape[0]
    nbig = len(BIG)
    axes = [BIG_AXIS[n] for n in BIG]
    sizes = [P[n].shape[a] for n, a in zip(BIG, axes)]
    chip = (2 * lax.axis_index("x") + lax.axis_index("y")).astype(jnp.int32).reshape(1)

    groups = [['w_in'], ['w_o', 'w_xq', 'w_xkv', 'w_xo'], ['w_gate_up', 'w_down']]
    units = [(l, g) for l in range(depth) for g in groups]
    unit_of = {(l, n): i for i, (l, names) in enumerate(units) for n in names}
    ax = lambda names: [BIG_AXIS[n] for n in names]
    sz = lambda names: [P[n].shape[BIG_AXIS[n]] for n in names]

    full = {n: _place_own("place_" + n, P[n], BIG_AXIS[n], chip) for n in BIG}
    gathers, land, tok = _split_start_many(
        "gather_start", [full[n] for n in BIG],
        [([BIG.index(n) for n in names], 3 * len(names), _gather_plan(ax(names), sz(names), l)) for l, names in units])
    full.update(zip(BIG, land))
    P['first_dep'] = tok[:1, :1]
    forwards, gathered = {}, set()

    def forward_unit(i, after, carried=()):
        ul, unames = units[i]
        send, recv = gathers[i]
        send, recv, land, thru = _split_wait_start(
            "gather_pass%d" % i, send, recv, [full[n] for n in unames], after,
            _gather_plan(ax(unames), sz(unames), ul), 3 * len(unames), _forward_plan(ax(unames), sz(unames), ul),
            carried=[full[n] for n in carried])
        full.update(zip(unames, land))
        full.update(zip(carried, thru))
        forwards[i] = (send, recv)

    def weights_of(l, names, after):
        i = unit_of[(l, names[0])]
        if i not in gathered:
            _, unames = units[i]
            if i not in forwards:
                forward_unit(i, after)
            send, recv = forwards.pop(i)
            _, land = _split_wait("gather_wait%d" % i, send, recv, [], [full[n] for n in unames], after,
                                  _forward_plan(ax(unames), sz(unames), l))
            full.update(zip(unames, land))
            gathered.add(i)
            if len(groups) <= i + 1 < len(units):
                forward_unit(i + 1, after, carried=[n for n in unames if n not in units[i + 1][1]])
        return {n: (full[n], l) for n in names}

    outs = {n: [lax.empty(P[n].shape, F32) for _ in range(4)] for n in BIG}
    gunits = [(l, BIG) for l in range(depth - 1, 0, -1)] + [
        (0, g) for g in (['w_gate_up', 'w_down'], ['w_xq', 'w_xkv', 'w_xo'], ['w_o'], ['w_in'])]
    collected, scatters, pairs = {}, {}, {}

    def finish_scatter(i, after):
        _, names = gunits[i]
        send, recv, g_l, slots = scatters.pop(i)
        g_l, slots = _split_wait("scatter_wait%d" % i, send, recv, g_l, slots, after,
                                 _scatter_plan(ax(names), sz(names)))
        mine = _unit_chip_sum("chip_sum", [g.reshape(g.shape[1:]) for g in g_l], slots, ax(names), chip)
        send, recv, mine, theirs, tok = _split_start("pair_start%d" % i, mine, [lax.empty(a.shape, BF16) for a in mine],
                                                     len(names), _pair_plan)
        pairs[i] = (send, recv, mine, theirs)
        return tok[:1, :1]

    def finish_pair(i, after):
        l, names = gunits[i]
        send, recv, mine, theirs = pairs.pop(i)
        mine, theirs = _split_wait("pair_wait%d" % i, send, recv, mine, theirs, after, _pair_plan)
        new = _unit_adamw("adamw", mine, theirs, [P[n] for n in names], [M[n] for n in names], [V[n] for n in names],
                          l, [outs[n] for n in names])
        outs.update(zip(names, new))

    calls = {'n': 0}
    lag = 4

    def grads_of(l, g_part, after):
        collected.update({(l, n): g for n, g in g_part.items()})
        calls['n'] += 1
        now = calls['n']
        tok = jnp.zeros((1, 1), F32)
        for i, (ul, names) in enumerate(gunits):
            if ul != l or ('started', i) in collected or any((l, n) not in collected for n in names):
                continue
            collected[('started', i)] = now
            srcs = [collected[(l, n)].reshape((1,) + collected[(l, n)].shape) for n in names]
            send, recv, srcs, slots, t = _split_start("scatter_start%d" % i, srcs,
                                                      [lax.empty((3,) + P[n].shape[1:], BF16) for n in names],
                                                      3 * len(names), _scatter_plan(ax(names), sz(names)))
            scatters[i] = (send, recv, srcs, slots)
            tok = tok + t[:1, :1]
        for i in sorted(pairs):
            if collected[('summed', i)] + lag <= now:
                finish_pair(i, after)
        for i in sorted(scatters):
            if collected[('started', i)] + lag <= now:
                tok = tok + finish_scatter(i, after)
                collected[('summed', i)] = now
        return tok

    loss_part, dx, small_g = _fwd_bwd(P, weights_of, grads_of)
    loss = lax.psum(loss_part[0, 0], ("x", "y", "c"))
    grad_x = dx.reshape(x.shape)

    small_like = [P[n] for n in SMALL]
    half_sum = _small_pair_sum(_pack(small_g))
    send, recv, (half_sum,), (slots,), _ = _split_start(
        "small_start", [half_sum], [lax.empty((3,) + half_sum.shape, BF16)], 3, _small_plan)

    for i in sorted(pairs):
        finish_pair(i, half_sum)
    for i in sorted(scatters):
        finish_scatter(i, half_sum)
    for i in sorted(pairs):
        finish_pair(i, half_sum)
    grads, deltas, new_m, new_v = {}, {}, {}, {}
    for n in BIG:
        grads[n], deltas[n], new_m[n], new_v[n] = outs[n]

    (half_sum,), (slots,) = _split_wait("small_wait", send, recv, [half_sum], [slots], grads[BIG[0]], _small_plan)
    two_d = lambda a: a.reshape(-1, a.shape[-1])
    gs = _unpack(_small_total(half_sum, slots), small_like)
    upd = _small_adamw([two_d(g) for g in gs], [two_d(P[n]) for n in SMALL], [two_d(M[n]) for n in SMALL],
                       [two_d(V[n]) for n in SMALL])
    for n, g, (dlt, mn, vn) in zip(SMALL, gs, upd):
        shape = P[n].shape
        grads[n], deltas[n], new_m[n], new_v[n] = g, dlt.reshape(shape), mn.reshape(shape), vn.reshape(shape)

    return (loss, grad_x, *[grads[n] for n in WEIGHTS], *[deltas[n] for n in WEIGHTS],
            *[new_m[n] for n in WEIGHTS], *[new_v[n] for n in WEIGHTS])


def _fwd_bwd(P, weights_of, grads_of):
    (x, mem, positions, mem_norm_g, mix_pre_g, mix_post_g, w_in, gm_v_g, gm_w_s, gm_b_s, pool_w, pool_scale, attn_sinks,
     w_o, x_pre_g, x_post_g, w_xq, w_xkv, w_xo, ffn_pre_g, ffn_post_g, w_gate_up, w_down) = [P[n] for n in NAMES]
    x0 = x[0]
    s, d = x0.shape
    depth = w_in.shape[0]
    tgt = P['loss_target'][0]
    tmn = 256
    tmr = min(512, s)
    tmp = min(1024, s)
    tkw = min(2048, s)

    half = HEAD // 2
    inv = ROPE_THETA ** (-jnp.arange(half, dtype=F32) / half)
    ang = positions[0].astype(F32)[:, None] * inv
    cos, sin = jnp.cos(ang), jnp.sin(ang)
    cosq = jnp.tile(jnp.concatenate([cos, cos], axis=-1), (1, 2))
    sinq = jnp.tile(jnp.concatenate([-sin, sin], axis=-1), (1, 2))

    row = lambda a, l: a[l].reshape(1, -1)
    memn = _prenorm("mem_norm", mem[0], mem_norm_g.reshape(1, d), tmn)
    pw_bd = []
    for l in range(depth):
        bd = jnp.zeros((256, 256), F32)
        for g in range(4):
            bd = lax.dynamic_update_slice(bd, pool_w[l, g], (64 * g, 64 * g))
        pw_bd.append(bd)

    saved = []
    xc = x0
    h = _prenorm("pre_norm0", x0, row(mix_pre_g, 0) + P['first_dep'], tmn)
    for l in range(depth):
        W = weights_of(l, ['w_in'], xc)
        sv = {'x0': xc, 'h1': h}
        z, = _mm_rows("fwd_w_in", h, *W['w_in'], 'nn', tm=tmp, rows_out=[F32], epilogue=_plain_rows)
        abc = _mixer_fwd("mixer_fwd", z, cosq, sinq, row(gm_v_g, l), gm_w_s[l], gm_b_s[l].T, pw_bd[l],
                         row(pool_scale, l), row(attn_sinks, l))
        W.update(weights_of(l, ['w_o'], z))
        mix, xc, h = _mm_rows("fwd_w_o", abc, *W['w_o'], 'nn', tm=tmr, rows_in=[xc],
                              params=[row(mix_post_g, l), row(x_pre_g, l)], rows_out=[BF16, F32, BF16],
                              epilogue=_post_pre_rows)
        sv.update(z=z, abc=abc, mix=mix, x1=xc, h2=h)
        W.update(weights_of(l, ['w_xq', 'w_xkv', 'w_xo'], xc))
        q, = _mm_rows("fwd_w_xq", h, *W['w_xq'], 'nn', tm=tmp, rows_out=[BF16], epilogue=_plain_rows)
        kv = _mm_nn("fwd_w_xkv", memn, *W['w_xkv'], tm=256, tn=512, tk=d, out_dtype=BF16)
        o = _xattn_fwd("xattn_fwd", q, kv, 512)
        xo, xc, h = _mm_rows("fwd_w_xo", o, *W['w_xo'], 'nn', tm=tmr, rows_in=[xc],
                             params=[row(x_post_g, l), row(ffn_pre_g, l)], rows_out=[BF16, F32, BF16],
                             epilogue=_post_pre_rows)
        sv.update(q=q, kv=kv, o=o, xo=xo, x2=xc, h3=h)
        W.update(weights_of(l, ['w_gate_up', 'w_down'], xc))
        dff = W['w_down'][0].shape[1]
        gate, up, act = _ffn_up("ffn_up", h, *W['w_gate_up'], 512, dff // 2)
        sv.update(gate=gate, up=up, act=act)
        if l + 1 < depth:
            f, xc, h = _mm_rows("fwd_w_down", act, *W['w_down'], 'nn', tm=tmr, rows_in=[xc],
                                params=[row(ffn_post_g, l), row(mix_pre_g, l + 1)], rows_out=[BF16, F32, BF16],
                                epilogue=_post_pre_rows)
            sv.update(f=f)
        saved.append(sv)
    gs = {n: [None] * depth for n in SMALL if n != 'mem_norm_g'}
    dx, dfn, gs['ffn_post_g'][depth - 1], loss_part = _mm_rows(
        "fwd_w_down_loss", saved[-1]['act'], *W['w_down'], 'nn', tm=tmr, rows_in=[xc, tgt],
        params=[row(ffn_post_g, depth - 1)], rows_out=[F32, BF16], n_sums=2, epilogue=_make_loss_rows(d))

    dmemn = None
    tok = jnp.zeros((1, 1), F32)
    for l in reversed(range(depth)):
        sv, W, G = saved[l], weights_of(l, BIG, dx), {}
        G['w_down'] = _mm_tn("dw_down", sv['act'], dfn, tm=dff // 2, tn=d, tk=tkw)
        dgu = _ffn_act_bwd("ffn_act_bwd", dfn, *W['w_down'], sv['gate'], sv['up'], 256)
        G['w_gate_up'] = _mm_tn("dw_gate_up", sv['h3'], dgu, tm=d, tn=dff // 2, tk=tkw)
        dx, dxo, gs['ffn_pre_g'][l], gs['x_post_g'][l] = _mm_rows(
            "bwd_w_gate_up", dgu, *W['w_gate_up'], 'nt', tm=tmr, rows_in=[sv['x2'], dx, sv['xo']],
            params=[row(ffn_pre_g, l) + tok, row(x_post_g, l)], rows_out=[F32, BF16], n_sums=2, epilogue=_bwd_rows)
        tok = grads_of(l, {n: G[n] for n in ('w_gate_up', 'w_down')}, dx)
        G['w_xo'] = _mm_tn("dw_xo", sv['o'], dxo, tm=d, tn=d, tk=tkw)
        do, = _mm_rows("bwd_w_xo", dxo, *W['w_xo'], 'nt', tm=tmp, params=[jnp.zeros((1, d), F32) + tok],
                       rows_out=[BF16], epilogue=_plain_rows)
        dq, dkv = _xattn_bwd("xattn_bwd", sv['q'], sv['kv'], do, 512)
        dkv = dkv.astype(BF16)
        G['w_xkv'] = _mm_tn("dw_xkv", memn, dkv, tm=d, tn=d, tk=mem.shape[1])
        dmemn = _mm_nt("bwd_w_xkv", dkv, *W['w_xkv'], tm=mem.shape[1], tn=512, tk=2 * d, out_dtype=F32, add=dmemn)
        G['w_xq'] = _mm_tn("dw_xq", sv['h2'], dq, tm=d, tn=d, tk=tkw)
        dx, dmix, gs['x_pre_g'][l], gs['mix_post_g'][l] = _mm_rows(
            "bwd_w_xq", dq, *W['w_xq'], 'nt', tm=tmr, rows_in=[sv['x1'], dx, sv['mix']],
            params=[row(x_pre_g, l), row(mix_post_g, l)], rows_out=[F32, BF16], n_sums=2, epilogue=_bwd_rows)
        tok = grads_of(l, {n: G[n] for n in ('w_xq', 'w_xkv', 'w_xo')}, dx)
        G['w_o'] = _mm_tn("dw_o", sv['abc'], dmix, tm=d, tn=d, tk=tkw)
        dabc, = _mm_rows("bwd_w_o", dmix, *W['w_o'], 'nt', tm=tmp, params=[jnp.zeros((1, d), F32) + tok],
                         rows_out=[F32], epilogue=_plain_rows)
        tok = grads_of(l, {'w_o': G['w_o']}, dabc)
        dz, dgv, dws, dbt, dpw, dpsc, dsnk = _mixer_bwd(
            "mixer_bwd", sv['z'], dabc, cosq, sinq, row(gm_v_g, l) + tok, gm_w_s[l], gm_b_s[l].T, pw_bd[l],
            row(pool_scale, l), row(attn_sinks, l))
        gs['gm_v_g'][l] = dgv
        gs['gm_w_s'][l] = dws
        gs['gm_b_s'][l] = dbt.T
        gs['pool_w'][l] = jnp.stack([dpw[64 * g:64 * (g + 1), 64 * g:64 * (g + 1)] for g in range(4)])
        gs['pool_scale'][l] = dpsc
        gs['attn_sinks'][l] = dsnk
        G['w_in'] = _mm_tn("dw_in", sv['h1'], dz, tm=d, tn=dz.shape[1], tk=tkw)
        if l > 0:
            dx, dfn, gs['mix_pre_g'][l], gs['ffn_post_g'][l - 1] = _mm_rows(
                "bwd_w_in", dz, *W['w_in'], 'nt', tm=tmr, rows_in=[sv['x0'], dx, saved[l - 1]['f']],
                params=[row(mix_pre_g, l), row(ffn_post_g, l - 1)], rows_out=[F32, BF16], n_sums=2,
                epilogue=_bwd_rows)
        else:
            dx, gs['mix_pre_g'][l] = _mm_rows(
                "bwd_w_in_first", dz, *W['w_in'], 'nt', tm=tmr, rows_in=[sv['x0'], dx],
                params=[row(mix_pre_g, l)], rows_out=[F32], n_sums=1, epilogue=_bwd_rows_first)
        tok = grads_of(l, {'w_in': G['w_in']}, dx)
    _, dg_mem = _norm_bwd("bwd_mem_norm", mem[0], mem_norm_g.reshape(1, d) + tok, dmemn, None, BF16, tmn)
    small_g = []
    for n in SMALL:
        if n == 'mem_norm_g':
            small_g.append(dg_mem.reshape(P[n].shape))
        else:
            small_g.append(jnp.stack([a.reshape(P[n].shape[1:]) for a in gs[n]]))
    return loss_part, dx, small_g
```

```python
import functools

import jax
import jax.numpy as jnp
from jax import lax
from jax.experimental import pallas as pl
from jax.experimental.pallas import tpu as pltpu

F32 = jnp.float32
BF16 = jnp.bfloat16
EPS = 1e-6
CHUNK = 128
HEAD = 64
ROPE_THETA = 10000.0
POOL_WINDOWS = (2, 4, 8, 16)
LR, B1, B2, ADAM_EPS, WD, STEP = 0.001, 0.9, 0.999, 1e-08, 0.01, 10
MESH = pl.DeviceIdType.MESH
VMEM_LIMIT = 56 * 1024 * 1024

NAMES = ['x', 'mem', 'positions', 'mem_norm_g', 'mix_pre_g', 'mix_post_g', 'w_in', 'gm_v_g', 'gm_w_s', 'gm_b_s',
         'pool_w', 'pool_scale', 'attn_sinks', 'w_o', 'x_pre_g', 'x_post_g', 'w_xq', 'w_xkv', 'w_xo', 'ffn_pre_g',
         'ffn_post_g', 'w_gate_up', 'w_down']
WEIGHTS = NAMES[3:]
BIG = ['w_in', 'w_o', 'w_xq', 'w_xkv', 'w_xo', 'w_gate_up', 'w_down']
BIG_AXIS = {'w_in': 2, 'w_o': 1, 'w_xq': 1, 'w_xkv': 2, 'w_xo': 1, 'w_gate_up': 2, 'w_down': 1}
SMALL = [n for n in WEIGHTS if n not in BIG]

NN = (((1,), (0,)), ((), ()))
NT = (((1,), (1,)), ((), ()))
TN = (((0,), (0,)), ((), ()))


def _dot(a, b, dims=NN):
    return lax.dot_general(a, b, dims, preferred_element_type=F32)


def _params(sem):
    return pltpu.CompilerParams(dimension_semantics=sem, vmem_limit_bytes=VMEM_LIMIT)


STREAM_BLOCK_BYTES = 3 * 512 * 1024


def _rows_tile(rows, cols):
    limit = max(16, STREAM_BLOCK_BYTES // (4 * cols))
    return max(t for t in range(16, min(rows, limit) + 1, 16) if rows % t == 0)


def _mm(name, a, a_spec, b, b_spec, dims, grid, nk, out_shape, out_spec, add=None, add_spec=None):
    acc_shape = out_spec.block_shape

    def body(*refs):
        a_ref, b_ref = refs[0], refs[1]
        pos = 2
        add_ref = None
        if add is not None:
            add_ref = refs[pos]
            pos += 1
        o_ref = refs[pos]
        part = _dot(a_ref[...].astype(BF16), b_ref[...].astype(BF16), dims)
        if nk == 1:
            if add_ref is not None:
                part = part + add_ref[...]
            o_ref[...] = part.astype(o_ref.dtype)
        else:
            acc_ref = refs[pos + 1]
            k = pl.program_id(2)

            @pl.when(k == 0)
            def _():
                acc_ref[...] = part if add_ref is None else part + add_ref[...]

            @pl.when(k > 0)
            def _():
                acc_ref[...] += part

            @pl.when(k == nk - 1)
            def _():
                o_ref[...] = acc_ref[...].astype(o_ref.dtype)

    ops, specs = [a, b], [a_spec, b_spec]
    if add is not None:
        ops.append(add)
        specs.append(add_spec)
    return pl.pallas_call(
        body, name=name, grid=grid, in_specs=specs, out_specs=out_spec, out_shape=out_shape,
        scratch_shapes=[pltpu.VMEM(acc_shape, F32)] if nk > 1 else [],
        compiler_params=_params(("parallel", "parallel", "arbitrary")),
    )(*ops)


def _wspec(block, layer, fn):
    return pl.BlockSpec((None,) + block, lambda i, j, k: (layer,) + fn(i, j, k))


def _mm_nn(name, a, w, layer, *, tm, tn, tk, out_dtype):
    m, kk = a.shape
    n = w.shape[2]
    tm = min(tm, m)
    nk = kk // tk
    return _mm(name, a, pl.BlockSpec((tm, tk), lambda i, j, k: (i, k)),
               w, _wspec((tk, tn), layer, lambda i, j, k: (k, j)), NN,
               (m // tm, n // tn, nk), nk, jax.ShapeDtypeStruct((m, n), out_dtype),
               pl.BlockSpec((tm, tn), lambda i, j, k: (i, j)))


def _mm_nt(name, a, w, layer, *, tm, tn, tk, out_dtype, add=None):
    m, kk = a.shape
    n = w.shape[1]
    tm = min(tm, m)
    nk = kk // tk
    ospec = pl.BlockSpec((tm, tn), lambda i, j, k: (i, j))
    return _mm(name, a, pl.BlockSpec((tm, tk), lambda i, j, k: (i, k)),
               w, _wspec((tn, tk), layer, lambda i, j, k: (j, k)), NT,
               (m // tm, n // tn, nk), nk, jax.ShapeDtypeStruct((m, n), out_dtype), ospec,
               add=add, add_spec=ospec if add is not None else None)


def _mm_tn(name, a, b, *, tm, tn, tk):
    kk, m = a.shape
    n = b.shape[1]
    tk = min(tk, kk)
    nk = kk // tk
    return _mm(name, a, pl.BlockSpec((tk, tm), lambda i, j, k: (k, i)),
               b, pl.BlockSpec((tk, tn), lambda i, j, k: (k, j)), TN,
               (m // tm, n // tn, nk), nk, jax.ShapeDtypeStruct((m, n), BF16),
               pl.BlockSpec((tm, tn), lambda i, j, k: (i, j)))


def _mm_rows(name, a, w, layer, mode, *, tm, rows_in=(), params=(), rows_out=(), n_sums=0, epilogue):
    m, kk = a.shape
    n = w.shape[2] if mode == 'nn' else w.shape[1]
    nr, npar, no = len(rows_in), len(params), len(rows_out)
    out_dims = [(dt, n) if not isinstance(dt, tuple) else dt for dt in rows_out]

    def body(*refs):
        a_ref, w_ref = refs[0], refs[1]
        rin = refs[2:2 + nr]
        par = refs[2 + nr:2 + nr + npar]
        outs = refs[2 + nr + npar:2 + nr + npar + no]
        sums = refs[2 + nr + npar + no:2 + nr + npar + no + n_sums]
        acc = _dot(a_ref[...], w_ref[...], NN if mode == 'nn' else NT)
        res, sm = epilogue(acc, [r[...] for r in rin], [p[...] for p in par])
        for r, v in zip(outs, res):
            r[...] = v.astype(r.dtype)

        @pl.when(pl.program_id(0) == 0)
        def _():
            for r in sums:
                r[...] = jnp.zeros_like(r)

        for r, v in zip(sums, sm):
            r[...] += v

    wblock = (None, kk, n) if mode == 'nn' else (None, n, kk)
    rowblk = lambda width: pl.BlockSpec((tm, width), lambda i: (i, 0))
    one = pl.BlockSpec((1, n), lambda i: (0, 0))
    return pl.pallas_call(
        body, name=name, grid=(m // tm,),
        in_specs=[pl.BlockSpec((tm, kk), lambda i: (i, 0)),
                  pl.BlockSpec(wblock, lambda i: (layer, 0, 0), pipeline_mode=pl.Buffered(1))]
                 + [rowblk(r.shape[1]) for r in rows_in] + [one] * npar,
        out_specs=[rowblk(width) for _, width in out_dims] + [one] * n_sums,
        out_shape=[jax.ShapeDtypeStruct((m, width), dt) for dt, width in out_dims] +
                  [jax.ShapeDtypeStruct((1, n), F32)] * n_sums,
        compiler_params=_params(("arbitrary",)),
    )(a, w, *rows_in, *params)


def _rstd(x):
    return lax.rsqrt(jnp.mean(x * x, axis=-1, keepdims=True) + EPS)


def _norm_back(xin, r, g, dy):
    xh = xin * r
    dyg = dy * g
    return r * (dyg - xh * jnp.mean(dyg * xh, axis=-1, keepdims=True)), jnp.sum(dy * xh, axis=0, keepdims=True)


def _plain_rows(acc, rows, pars):
    return [acc], []


def _post_pre_rows(y, rows, pars):
    ry = _rstd(y)
    xn = rows[0] + y * ry * pars[0]
    rx = _rstd(xn)
    return [y, xn, xn * rx * pars[1], ry, rx], []


def _make_loss_rows(d):
    def fn(y, rows, pars):
        x, tgt = rows
        ry = _rstd(y)
        err = x + y * ry * pars[0] - tgt
        dout = err * (1.0 / d)
        dy, dg = _norm_back(y, ry, pars[0], dout)
        lsum = 0.5 * jnp.sum(jnp.mean(err * err, axis=-1, keepdims=True), axis=0, keepdims=True)
        return [dout, dy], [dg, jnp.broadcast_to(lsum, dg.shape)]
    return fn


def _bwd_rows(dh, rows, pars):
    xin, resid, yprev, rx, ry = rows
    dxa, dg_pre = _norm_back(xin, rx, pars[0], dh)
    dx = resid + dxa
    dyp, dg_post = _norm_back(yprev.astype(F32), ry, pars[1], dx)
    return [dx, dyp], [dg_pre, dg_post]


def _bwd_rows_first(dh, rows, pars):
    xin, resid, rx = rows
    dxa, dg_pre = _norm_back(xin, rx, pars[0], dh)
    return [resid + dxa], [dg_pre]


def _row(d):
    return pl.BlockSpec((1, d), lambda i: (0, 0))


def _prenorm(name, x, g, tm):
    m, d = x.shape

    def body(x_ref, g_ref, o_ref, r_ref):
        xv = x_ref[...]
        r = _rstd(xv)
        o_ref[...] = (xv * r * g_ref[...]).astype(BF16)
        r_ref[...] = r

    blk = pl.BlockSpec((tm, d), lambda i: (i, 0))
    return pl.pallas_call(
        body, name=name, grid=(m // tm,), in_specs=[blk, _row(d)],
        out_specs=[blk, pl.BlockSpec((tm, 1), lambda i: (i, 0))],
        out_shape=[jax.ShapeDtypeStruct((m, d), BF16), jax.ShapeDtypeStruct((m, 1), F32)],
        compiler_params=_params(("parallel",)))(x, g)


def _norm_bwd(name, xin, g, dy, resid, out_dtype, tm):
    m, d = xin.shape

    def body(*refs):
        if resid is None:
            x_ref, g_ref, dy_ref, dx_ref, dg_ref = refs
        else:
            x_ref, g_ref, dy_ref, r_ref, dx_ref, dg_ref = refs
        xv = x_ref[...]
        r = _rstd(xv)
        xh = xv * r
        dyv = dy_ref[...].astype(F32)
        dyg = dyv * g_ref[...]
        dx = r * (dyg - xh * jnp.mean(dyg * xh, axis=-1, keepdims=True))
        if resid is not None:
            dx = dx + r_ref[...]
        dx_ref[...] = dx.astype(dx_ref.dtype)

        @pl.when(pl.program_id(0) == 0)
        def _():
            dg_ref[...] = jnp.zeros_like(dg_ref)

        dg_ref[...] += jnp.sum(dyv * xh, axis=0, keepdims=True)

    blk = pl.BlockSpec((tm, d), lambda i: (i, 0))
    ops = [xin, g, dy] + ([] if resid is None else [resid])
    specs = [blk, _row(d), blk] + ([] if resid is None else [blk])
    return pl.pallas_call(
        body, name=name, grid=(m // tm,), in_specs=specs, out_specs=[blk, _row(d)],
        out_shape=[jax.ShapeDtypeStruct((m, d), out_dtype), jax.ShapeDtypeStruct((1, d), F32)],
        compiler_params=_params(("arbitrary",)))(*ops)


def _gelu_parts(x):
    c = 0.7978845608028654
    t = jnp.tanh(c * (x + 0.044715 * (x * x * x)))
    return 0.5 * x * (1.0 + t), t


def _gelu_grad(x, t):
    c = 0.7978845608028654
    return 0.5 * (1.0 + t) + 0.5 * x * (1.0 - t * t) * (c * (1.0 + 3.0 * 0.044715 * x * x))


def _rot_half(x):
    ax = x.ndim - 1
    w = x.shape[ax]
    lane = lax.broadcasted_iota(jnp.int32, x.shape, ax)
    return jnp.where((lane & 63) < 32, pltpu.roll(x, w - 32, ax), pltpu.roll(x, 32, ax))


def _group_mean(x, ones_bd):
    hi = x.astype(BF16)
    lo = (x - hi.astype(F32)).astype(BF16)
    return (_dot(hi, ones_bd) + _dot(lo, ones_bd)) * (1.0 / HEAD)


def _gating(gel, gv, ws_ref, bt, ones_bd, mix_s):
    u = gel[:, :256]
    v = gel[:, 256:]
    r = lax.rsqrt(_group_mean(v * v, ones_bd) + EPS)
    xh = v * r
    vn = (xh * gv).astype(BF16)
    row = lax.broadcasted_iota(jnp.int32, (CHUNK, CHUNK), 0)
    col = lax.broadcasted_iota(jnp.int32, (CHUNK, CHUNK), 1)
    causal = col <= row
    wcs = []
    for g in range(4):
        wc = jnp.where(causal, ws_ref[g], 0.0).astype(BF16)
        wcs.append(wc)
        mix_s[:, HEAD * g:HEAD * (g + 1)] = _dot(wc, vn[:, HEAD * g:HEAD * (g + 1)]) + bt[:, g:g + 1]
    return u, r, xh, vn, wcs, causal, mix_s[...]


def _lane_select(lane, vals):
    return jnp.where(lane < 64, vals[0], jnp.where(lane < 128, vals[1], jnp.where(lane < 192, vals[2], vals[3])))


def _pool_fwd(pc, pp, ci):
    ext = jnp.concatenate([pp, pc], axis=0)
    s2 = ext + pltpu.roll(ext, 1, 0)
    s4 = s2 + pltpu.roll(s2, 2, 0)
    s8 = s4 + pltpu.roll(s4, 4, 0)
    s16 = s8 + pltpu.roll(s8, 8, 0)
    t1 = ci * CHUNK + lax.broadcasted_iota(jnp.int32, (CHUNK, 1), 0) + 1
    lane = lax.broadcasted_iota(jnp.int32, (1, 256), 1)
    cnt = _lane_select(lane, [jnp.minimum(t1, w).astype(F32) for w in POOL_WINDOWS])
    ssel = _lane_select(lane, [s[CHUNK:] for s in (s2, s4, s8, s16)])
    return ssel / cnt - pc, cnt, lane


def _attn_prep(zc, zpkv, cc, sc, cp, sp):
    q = zc[:, 768:1280]
    kc = zc[:, 1280:1408]
    vc = zc[:, 1408:1536]
    kp = zpkv[:, :128]
    vp = zpkv[:, 128:]
    qr = q * jnp.concatenate([cc] * 4, axis=1) + _rot_half(q) * jnp.concatenate([sc] * 4, axis=1)
    krc = kc * cc + _rot_half(kc) * sc
    krp = kp * cp + _rot_half(kp) * sp
    kband = jnp.concatenate([krp, krc], axis=0)
    vband = jnp.concatenate([vp, vc], axis=0)
    return qr, kband, vband


def _attn_bias():
    key = lax.broadcasted_iota(jnp.int32, (2 * CHUNK, 4 * CHUNK), 0)
    t = lax.broadcasted_iota(jnp.int32, (2 * CHUNK, 4 * CHUNK), 1) & (CHUNK - 1)
    cur = (key >= CHUNK) & (key - CHUNK <= t)
    prev = (key < CHUNK) & (key > t)
    return jnp.stack([jnp.where(cur, 0.0, -1e30), jnp.where(cur | prev, 0.0, -1e30)]).astype(F32)


SCALE = HEAD ** -0.5


def _stack_heads(x, base, hk):
    return jnp.concatenate([x[:, base + HEAD * (4 * hk + i):base + HEAD * (4 * hk + i + 1)] for i in range(4)], axis=0)


def _sink_row(snk, hk):
    lane = lax.broadcasted_iota(jnp.int32, (1, 4 * CHUNK), 1)
    s = [snk[:, 4 * hk + i:4 * hk + i + 1] for i in range(4)]
    return jnp.where(lane < CHUNK, s[0], jnp.where(lane < 2 * CHUNK, s[1], jnp.where(lane < 3 * CHUNK, s[2], s[3])))


def _group_probs(kh, q4, bias, sink4):
    s = _dot(kh, q4, NT) + bias
    mx = jnp.maximum(jnp.max(s, axis=0, keepdims=True), sink4)
    e = jnp.exp(s - mx)
    es = jnp.exp(sink4 - mx)
    inv = 1.0 / (jnp.sum(e, axis=0, keepdims=True) + es)
    return e * inv, es * inv


def _mixer_specs(nb, rev):
    def cur(i):
        return nb - 1 - i if rev else i

    def prev(i):
        return jnp.maximum(cur(i) - 1, 0)

    full = lambda shape: pl.BlockSpec(shape, lambda i: (0,) * len(shape))
    specs = [
        pl.BlockSpec((CHUNK, 1536), lambda i: (cur(i), 0)),
        pl.BlockSpec((CHUNK, 256), lambda i: (prev(i), 2)),
        pl.BlockSpec((CHUNK, 256), lambda i: (prev(i), 5)),
        pl.BlockSpec((CHUNK, 128), lambda i: (cur(i), 0)),
        pl.BlockSpec((CHUNK, 128), lambda i: (cur(i), 0)),
        pl.BlockSpec((CHUNK, 128), lambda i: (prev(i), 0)),
        pl.BlockSpec((CHUNK, 128), lambda i: (prev(i), 0)),
        full((1, 256)), full((4, CHUNK, CHUNK)), full((CHUNK, 4)), full((256, 256)), full((1, 256)), full((1, 8)),
        full((256, 256)),
        pl.BlockSpec((None, 2 * CHUNK, 4 * CHUNK), lambda i: (jnp.minimum(cur(i), 1), 0, 0)),
    ]
    return specs, cur


def _ones_bd():
    g = lax.broadcasted_iota(jnp.int32, (256, 256), 0) // HEAD == lax.broadcasted_iota(jnp.int32, (256, 256), 1) // HEAD
    return g.astype(BF16)


def _mixer_fwd(name, z, cosq, sinq, gv, ws, bt, pw, psc, snk):
    s = z.shape[0]
    nb = s // CHUNK
    specs, _ = _mixer_specs(nb, False)

    def body(zc_ref, zpp_ref, zpkv_ref, cq_ref, sq_ref, cp_ref, sp_ref, gv_ref, ws_ref, bt_ref, pw_ref, psc_ref,
             snk_ref, bd_ref, bias_ref, o_ref, mix_s):
        ci = pl.program_id(0)
        zc = zc_ref[...]
        gel, _ = _gelu_parts(zc[:, :512])
        u, _, _, _, _, _, mixed = _gating(gel, gv_ref[...], ws_ref, bt_ref[...], bd_ref[...], mix_s)
        o_ref[:, :256] = (u * mixed).astype(BF16)
        pp = jnp.where(ci > 0, zpp_ref[...], 0.0)
        pooled, _, _ = _pool_fwd(zc[:, 512:768], pp, ci)
        mp = _dot(pooled.astype(BF16), pw_ref[...].astype(BF16))
        o_ref[:, 256:512] = (mp * psc_ref[...]).astype(BF16)
        qr, kband, vband = _attn_prep(zc, zpkv_ref[...], cq_ref[...], sq_ref[...], cp_ref[...], sp_ref[...])
        bias = bias_ref[...]
        snkv = snk_ref[...]
        kb = kband.astype(BF16)
        vt = vband.T
        ots = []
        for hk in range(2):
            q4 = (_stack_heads(qr, 0, hk) * SCALE).astype(BF16)
            p, _ = _group_probs(kb[:, HEAD * hk:HEAD * (hk + 1)], q4, bias, _sink_row(snkv, hk))
            ots.append(_dot(vt[HEAD * hk:HEAD * (hk + 1), :].astype(BF16), p.astype(BF16)))
        o = jnp.concatenate(ots, axis=0).T
        for hk in range(2):
            for i in range(4):
                h = 4 * hk + i
                o_ref[:, 512 + HEAD * h:512 + HEAD * (h + 1)] = o[CHUNK * i:CHUNK * (i + 1),
                                                                  HEAD * hk:HEAD * (hk + 1)].astype(BF16)

    return pl.pallas_call(
        body, name=name, grid=(nb,), in_specs=specs, out_specs=pl.BlockSpec((CHUNK, 1024), lambda i: (i, 0)),
        out_shape=jax.ShapeDtypeStruct((s, 1024), BF16), scratch_shapes=[pltpu.VMEM((CHUNK, 256), F32)],
        compiler_params=_params(("parallel",)),
    )(z, z, z, cosq, sinq, cosq, sinq, gv, ws, bt, pw, psc, snk, _ones_bd(), _attn_bias())


def _mixer_bwd(name, z, dabc, cosq, sinq, gv, ws, bt, pw, psc, snk):
    s = z.shape[0]
    nb = s // CHUNK
    specs, cur = _mixer_specs(nb, True)
    specs = specs + [pl.BlockSpec((CHUNK, 1024), lambda i: (cur(i), 0))]
    full = lambda shape: pl.BlockSpec(shape, lambda i: (0,) * len(shape))
    acc_shapes = [(1, 256), (4, CHUNK, CHUNK), (CHUNK, 4), (256, 256), (1, 256), (1, 8)]

    def body(zc_ref, zpp_ref, zpkv_ref, cq_ref, sq_ref, cp_ref, sp_ref, gv_ref, ws_ref, bt_ref, pw_ref, psc_ref,
             snk_ref, bd_ref, bias_ref, dabc_ref, dz_ref, dgv_ref, dws_ref, dbt_ref, dpw_ref, dpsc_ref, dsnk_ref,
             cpool, ck, cv, dq_s, dkv_s, mix_s, dvn_s):
        step = pl.program_id(0)
        ci = nb - 1 - step

        @pl.when(step == 0)
        def _():
            for r in (dgv_ref, dws_ref, dbt_ref, dpw_ref, dpsc_ref, dsnk_ref, cpool, ck, cv):
                r[...] = jnp.zeros_like(r)

        zc = zc_ref[...]
        dabc = dabc_ref[...]
        zg = zc[:, :512]
        gel, th = _gelu_parts(zg)
        gp = _gelu_grad(zg, th)
        gvv = gv_ref[...]
        bd = bd_ref[...]
        u, r, xh, vn, wcs, causal, mixed = _gating(gel, gvv, ws_ref, bt_ref[...], bd, mix_s)
        da = dabc[:, :256]
        dm = da * u
        dmb = dm.astype(BF16)
        lane4 = lax.broadcasted_iota(jnp.int32, (CHUNK, 4), 1)
        dbt = jnp.zeros((CHUNK, 4), F32)
        for g in range(4):
            lo, hi = HEAD * g, HEAD * (g + 1)
            dws_ref[g] += jnp.where(causal, _dot(dmb[:, lo:hi], vn[:, lo:hi], NT), 0.0)
            dbt = dbt + jnp.where(lane4 == g, jnp.sum(dm[:, lo:hi], axis=-1, keepdims=True), 0.0)
            dvn_s[:, lo:hi] = _dot(wcs[g], dmb[:, lo:hi], TN)
        dbt_ref[...] += dbt
        dvn = dvn_s[...]
        dgv_ref[...] += jnp.sum(dvn * xh, axis=0, keepdims=True)
        dxh = dvn * gvv
        dvg = r * (dxh - xh * _group_mean(dxh * xh, bd))
        dz_ref[:, :256] = (da * mixed * gp[:, :256]).astype(BF16)
        dz_ref[:, 256:512] = (dvg * gp[:, 256:]).astype(BF16)
        pc = zc[:, 512:768]
        pp = jnp.where(ci > 0, zpp_ref[...], 0.0)
        pooled, cnt, lane = _pool_fwd(pc, pp, ci)
        pwb = pw_ref[...].astype(BF16)
        pooled_b = pooled.astype(BF16)
        mp = _dot(pooled_b, pwb)
        db = dabc[:, 256:512]
        dpsc_ref[...] += jnp.sum(db * mp, axis=0, keepdims=True)
        dmpb = (db * psc_ref[...]).astype(BF16)
        dpw_ref[...] += _dot(pooled_b, dmpb, TN)
        dpooled = _dot(dmpb, pwb, NT)
        davg = dpooled / cnt
        zero = jnp.zeros((CHUNK, 256), F32)
        d2, d4, d8, d16 = [jnp.concatenate([zero, jnp.where((lane >= 64 * k) & (lane < 64 * (k + 1)), davg, 0.0)],
                                           axis=0) for k in range(4)]
        g8 = d8 + d16 + pltpu.roll(d16, 2 * CHUNK - 8, 0)
        g4 = d4 + g8 + pltpu.roll(g8, 2 * CHUNK - 4, 0)
        g2 = d2 + g4 + pltpu.roll(g4, 2 * CHUNK - 2, 0)
        ge = g2 + pltpu.roll(g2, 2 * CHUNK - 1, 0)
        dz_ref[:, 512:768] = (ge[CHUNK:] - dpooled + cpool[...]).astype(BF16)
        cpool[...] = ge[:CHUNK]
        cc = cq_ref[...]
        sc = sq_ref[...]
        qr, kband, vband = _attn_prep(zc, zpkv_ref[...], cc, sc, cp_ref[...], sp_ref[...])
        bias = bias_ref[...]
        snkv = snk_ref[...]
        lane8 = lax.broadcasted_iota(jnp.int32, (1, 8), 1)
        qlane = lax.broadcasted_iota(jnp.int32, (1, 4 * CHUNK), 1)
        dsnk = jnp.zeros((1, 8), F32)
        kb = kband.astype(BF16)
        vb = vband.astype(BF16)
        kt = kband.T * SCALE
        dqts = []
        for hk in range(2):
            kh = kb[:, HEAD * hk:HEAD * (hk + 1)]
            q4 = (_stack_heads(qr, 0, hk) * SCALE).astype(BF16)
            do4 = _stack_heads(dabc, 512, hk).astype(BF16)
            p, ps = _group_probs(kh, q4, bias, _sink_row(snkv, hk))
            dp = _dot(vb[:, HEAD * hk:HEAD * (hk + 1)], do4, NT)
            dd = jnp.sum(p * dp, axis=0, keepdims=True)
            dsink = -ps * dd
            for i in range(4):
                part = jnp.sum(jnp.where((qlane >= CHUNK * i) & (qlane < CHUNK * (i + 1)), dsink, 0.0),
                               axis=1, keepdims=True)
                dsnk = dsnk + jnp.where(lane8 == 4 * hk + i, part, 0.0)
            dsb = (p * (dp - dd)).astype(BF16)
            dqts.append(_dot(kt[HEAD * hk:HEAD * (hk + 1), :].astype(BF16), dsb))
            dkv_s[:, HEAD * hk:HEAD * (hk + 1)] = _dot(dsb, q4)
            dkv_s[:, 128 + HEAD * hk:128 + HEAD * (hk + 1)] = _dot(p.astype(BF16), do4)
        dq4 = jnp.concatenate(dqts, axis=0).T
        for hk in range(2):
            for i in range(4):
                h = 4 * hk + i
                dq_s[:, HEAD * h:HEAD * (h + 1)] = dq4[CHUNK * i:CHUNK * (i + 1), HEAD * hk:HEAD * (hk + 1)]
        dsnk_ref[...] += dsnk
        dqr = dq_s[...]
        dz_ref[:, 768:1280] = (dqr * jnp.concatenate([cc] * 4, axis=1)
                               + _rot_half(dqr * jnp.concatenate([sc] * 4, axis=1))).astype(BF16)
        dkv = dkv_s[...]
        dkr = dkv[CHUNK:, :128] + ck[...]
        dz_ref[:, 1280:1408] = (dkr * cc + _rot_half(dkr * sc)).astype(BF16)
        dz_ref[:, 1408:1536] = (dkv[CHUNK:, 128:] + cv[...]).astype(BF16)
        ck[...] = dkv[:CHUNK, :128]
        cv[...] = dkv[:CHUNK, 128:]

    return pl.pallas_call(
        body, name=name, grid=(nb,), in_specs=specs,
        out_specs=[pl.BlockSpec((CHUNK, 1536), lambda i: (cur(i), 0))] + [full(a) for a in acc_shapes],
        out_shape=[jax.ShapeDtypeStruct((s, 1536), BF16)] + [jax.ShapeDtypeStruct(a, F32) for a in acc_shapes],
        scratch_shapes=[pltpu.VMEM((CHUNK, 256), F32), pltpu.VMEM((CHUNK, 128), F32), pltpu.VMEM((CHUNK, 128), F32),
                        pltpu.VMEM((CHUNK, 512), F32), pltpu.VMEM((2 * CHUNK, 256), F32),
                        pltpu.VMEM((CHUNK, 256), F32), pltpu.VMEM((CHUNK, 256), F32)],
        compiler_params=_params(("arbitrary",)),
    )(z, z, z, cosq, sinq, cosq, sinq, gv, ws, bt, pw, psc, snk, _ones_bd(), _attn_bias(), dabc)


def _xattn_probs(qh, kh):
    s = _dot(qh, kh, NT) * (256 ** -0.5)
    e = jnp.exp(s - jnp.max(s, axis=-1, keepdims=True))
    return e * (1.0 / jnp.sum(e, axis=-1, keepdims=True))


def _xattn_fwd(name, q, kv, tq):
    s, d = q.shape
    mlen = kv.shape[0]

    def body(q_ref, kv_ref, o_ref):
        for h in range(4):
            lo, hi = 256 * h, 256 * (h + 1)
            p = _xattn_probs(q_ref[:, lo:hi], kv_ref[:, lo:hi])
            o_ref[:, lo:hi] = _dot(p.astype(BF16), kv_ref[:, d + lo:d + hi]).astype(BF16)

    blk = pl.BlockSpec((tq, d), lambda i: (i, 0))
    return pl.pallas_call(body, name=name, grid=(s // tq,),
                          in_specs=[blk, pl.BlockSpec((mlen, 2 * d), lambda i: (0, 0))], out_specs=blk,
                          out_shape=jax.ShapeDtypeStruct((s, d), BF16), compiler_params=_params(("parallel",)))(q, kv)


def _xattn_bwd(name, q, kv, do, tq):
    s, d = q.shape
    mlen = kv.shape[0]

    def body(q_ref, kv_ref, do_ref, dq_ref, dkv_ref):
        @pl.when(pl.program_id(0) == 0)
        def _():
            dkv_ref[...] = jnp.zeros_like(dkv_ref)

        for h in range(4):
            lo, hi = 256 * h, 256 * (h + 1)
            qh = q_ref[:, lo:hi]
            kh = kv_ref[:, lo:hi]
            vh = kv_ref[:, d + lo:d + hi]
            doh = do_ref[:, lo:hi]
            p = _xattn_probs(qh, kh)
            dp = _dot(doh, vh, NT)
            dsb = (p * (dp - jnp.sum(p * dp, axis=-1, keepdims=True)) * (256 ** -0.5)).astype(BF16)
            dq_ref[:, lo:hi] = _dot(dsb, kh).astype(BF16)
            dkv_ref[:, lo:hi] += _dot(dsb, qh, TN)
            dkv_ref[:, d + lo:d + hi] += _dot(p.astype(BF16), doh, TN)

    blk = pl.BlockSpec((tq, d), lambda i: (i, 0))
    kvb = pl.BlockSpec((mlen, 2 * d), lambda i: (0, 0))
    return pl.pallas_call(
        body, name=name, grid=(s // tq,), in_specs=[blk, kvb, blk], out_specs=[blk, kvb],
        out_shape=[jax.ShapeDtypeStruct((s, d), BF16), jax.ShapeDtypeStruct((mlen, 2 * d), F32)],
        compiler_params=_params(("arbitrary",)))(q, kv, do)


def _sigmoid(x):
    return 0.5 * (1.0 + jnp.tanh(0.5 * x))


def _ffn_up(name, h, wgu, layer, tm, tn):
    s, d = h.shape
    dff = wgu.shape[2] // 2
    nj = dff // tn

    def body(h_ref, wg_ref, wu_ref, g_ref, u_ref, a_ref):
        hv = h_ref[...]
        gate = _dot(hv, wg_ref[...])
        up = _dot(hv, wu_ref[...])
        g_ref[...] = gate.astype(BF16)
        u_ref[...] = up.astype(BF16)
        a_ref[...] = (gate * _sigmoid(gate) * up).astype(BF16)

    ob = pl.BlockSpec((tm, tn), lambda j, i: (i, j))
    sd = jax.ShapeDtypeStruct((s, dff), BF16)
    return pl.pallas_call(
        body, name=name, grid=(nj, s // tm),
        in_specs=[pl.BlockSpec((tm, d), lambda j, i: (i, 0)),
                  pl.BlockSpec((None, d, tn), lambda j, i: (layer, 0, j)),
                  pl.BlockSpec((None, d, tn), lambda j, i: (layer, 0, j + nj))],
        out_specs=[ob, ob, ob], out_shape=[sd, sd, sd], compiler_params=_params(("parallel", "parallel")),
    )(h, wgu, wgu)


def _ffn_act_bwd(name, dfn, wdown, layer, gate, up, tm):
    s, d = dfn.shape
    dff = gate.shape[1]

    def body(df_ref, wd_ref, g_ref, u_ref, o_ref):
        dact = _dot(df_ref[...], wd_ref[...], NT)
        gate = g_ref[...].astype(F32)
        sig = _sigmoid(gate)
        o_ref[:, :dff] = (dact * u_ref[...].astype(F32) * sig * (1.0 + gate * (1.0 - sig))).astype(BF16)
        o_ref[:, dff:] = (dact * gate * sig).astype(BF16)

    gb = pl.BlockSpec((tm, dff), lambda i: (i, 0))
    return pl.pallas_call(
        body, name=name, grid=(s // tm,),
        in_specs=[pl.BlockSpec((tm, d), lambda i: (i, 0)),
                  pl.BlockSpec((None, dff, d), lambda i: (layer, 0, 0)), gb, gb],
        out_specs=pl.BlockSpec((tm, 2 * dff), lambda i: (i, 0)),
        out_shape=jax.ShapeDtypeStruct((s, 2 * dff), BF16), compiler_params=_params(("parallel",)),
    )(dfn, wdown, gate, up)


def _place():
    return lax.axis_index("x"), lax.axis_index("y"), lax.axis_index("c")


def _other_chips(x, y):
    return [(1 - x, y), (x, 1 - y), (1 - x, 1 - y)]


def _region(ref, axis, chip, size):
    start = pl.multiple_of(chip * size, size)
    if axis == 1:
        return ref.at[:, pl.ds(start, size), :]
    return ref.at[:, :, pl.ds(start, size)]


ANY = pl.BlockSpec(memory_space=pl.ANY)


HBM = pl.BlockSpec(memory_space=pltpu.HBM)
SEM = pl.BlockSpec(memory_space=pltpu.SEMAPHORE)
EFFECT = pltpu.SideEffectType.DATAFLOW_SIDE_EFFECTING


def _in_hbm(a):
    return pltpu.with_memory_space_constraint(a, pltpu.HBM)


def _split_start(name, srcs, lands, ncopies, plan):
    ns, nl = len(srcs), len(lands)

    def body(*refs):
        src, land = refs[:ns], refs[ns:ns + nl]
        send, recv = refs[ns + nl], refs[ns + nl + 1]
        token = refs[-1]
        x, y, c = _place()
        for k, (s_ref, d_ref, peer, _) in enumerate(plan(src, land, x, y, c)):
            pltpu.make_async_remote_copy(src_ref=s_ref, dst_ref=d_ref, send_sem=send.at[k], recv_sem=recv.at[k],
                                         device_id=peer, device_id_type=MESH).start()
        token[...] = jnp.zeros_like(token)

    ops = list(srcs) + list(lands)
    out = pl.pallas_call(
        body, name=name,
        out_shape=(pltpu.SemaphoreType.DMA((ncopies,)), pltpu.SemaphoreType.DMA((ncopies,)),
                   *[pltpu.HBM(a.shape, a.dtype) for a in ops], jax.ShapeDtypeStruct((8, 128), F32)),
        in_specs=(HBM,) * (ns + nl),
        out_specs=(SEM, SEM) + (HBM,) * (ns + nl) + (pl.BlockSpec(memory_space=pltpu.VMEM),),
        input_output_aliases={i: 2 + i for i in range(ns + nl)},
        compiler_params=pltpu.CompilerParams(has_side_effects=EFFECT),
    )(*[_in_hbm(a) for a in ops])
    return out[0], out[1], list(out[2:2 + ns]), list(out[2 + ns:2 + ns + nl]), out[-1]


def _split_start_many(name, lands, jobs):
    nl, nj = len(lands), len(jobs)

    def body(*refs):
        land = refs[:nl]
        sems = refs[nl:nl + 2 * nj]
        token = refs[-1]
        x, y, c = _place()
        for j, (idx, _, plan) in enumerate(jobs):
            for k, (s_ref, d_ref, peer, _) in enumerate(plan((), [land[t] for t in idx], x, y, c)):
                pltpu.make_async_remote_copy(src_ref=s_ref, dst_ref=d_ref, send_sem=sems[2 * j].at[k],
                                             recv_sem=sems[2 * j + 1].at[k], device_id=peer,
                                             device_id_type=MESH).start()
        token[...] = jnp.zeros_like(token)

    sem_shapes = tuple(pltpu.SemaphoreType.DMA((n,)) for _, n, _ in jobs for _ in range(2))
    out = pl.pallas_call(
        body, name=name,
        out_shape=sem_shapes + tuple(pltpu.HBM(a.shape, a.dtype) for a in lands)
        + (jax.ShapeDtypeStruct((8, 128), F32),),
        in_specs=(HBM,) * nl,
        out_specs=(SEM,) * (2 * nj) + (HBM,) * nl + (pl.BlockSpec(memory_space=pltpu.VMEM),),
        input_output_aliases={i: 2 * nj + i for i in range(nl)},
        compiler_params=pltpu.CompilerParams(has_side_effects=EFFECT),
    )(*[_in_hbm(a) for a in lands])
    return [(out[2 * j], out[2 * j + 1]) for j in range(nj)], list(out[2 * nj:2 * nj + nl]), out[-1]


def _split_wait(name, send, recv, srcs, lands, after, plan):
    ns, nl = len(srcs), len(lands)

    def body(*refs):
        src, land = refs[:ns], refs[ns:ns + nl]
        send_ref, recv_ref = refs[ns + nl], refs[ns + nl + 1]
        x, y, c = _place()
        for k, (s_ref, _, _, got) in enumerate(plan(src, land, x, y, c)):
            cp = pltpu.make_async_remote_copy(src_ref=s_ref, dst_ref=got, send_sem=send_ref.at[k],
                                              recv_sem=recv_ref.at[k], device_id=(x, y, c), device_id_type=MESH)
            cp.wait_send()
            cp.wait_recv()

    ops = list(srcs) + list(lands)
    out = pl.pallas_call(
        body, name=name, out_shape=tuple(pltpu.HBM(a.shape, a.dtype) for a in ops),
        in_specs=(HBM,) * (ns + nl) + (SEM, SEM, ANY), out_specs=(HBM,) * (ns + nl),
        input_output_aliases={i: i for i in range(ns + nl)},
        compiler_params=pltpu.CompilerParams(has_side_effects=EFFECT),
    )(*ops, send, recv, after)
    return list(out[:ns]), list(out[ns:])


def _split_wait_start(name, send, recv, lands, after, wait_plan, ncopies, start_plan, carried=()):
    nl, nc = len(lands), len(carried)
    lands = list(lands) + list(carried)

    def body(*refs):
        land = refs[:nl]
        send_in, recv_in = refs[nl + nc], refs[nl + nc + 1]
        send_out, recv_out = refs[nl + nc + 3], refs[nl + nc + 4]
        x, y, c = _place()
        for k, (s_ref, _, _, got) in enumerate(wait_plan((), land, x, y, c)):
            cp = pltpu.make_async_remote_copy(src_ref=s_ref, dst_ref=got, send_sem=send_in.at[k],
                                              recv_sem=recv_in.at[k], device_id=(x, y, c), device_id_type=MESH)
            cp.wait_send()
            cp.wait_recv()
        for k, (s_ref, d_ref, peer, _) in enumerate(start_plan((), land, x, y, c)):
            pltpu.make_async_remote_copy(src_ref=s_ref, dst_ref=d_ref, send_sem=send_out.at[k],
                                         recv_sem=recv_out.at[k], device_id=peer, device_id_type=MESH).start()

    out = pl.pallas_call(
        body, name=name,
        out_shape=(pltpu.SemaphoreType.DMA((ncopies,)), pltpu.SemaphoreType.DMA((ncopies,)),
                   *[pltpu.HBM(a.shape, a.dtype) for a in lands]),
        in_specs=(HBM,) * (nl + nc) + (SEM, SEM, ANY), out_specs=(SEM, SEM) + (HBM,) * (nl + nc),
        input_output_aliases={i: 2 + i for i in range(nl + nc)},
        compiler_params=pltpu.CompilerParams(has_side_effects=EFFECT),
    )(*lands, send, recv, after)
    return out[0], out[1], list(out[2:2 + nl]), list(out[2 + nl:])


def _half(ref, axis, chip, size, layer, h):
    reg = _region(ref, axis, chip, size).at[pl.ds(layer, 1)]
    rows = reg.shape[1] // 2
    return reg.at[:, pl.ds(pl.multiple_of(h * rows, rows), rows), :]


def _gather_plan(axes, sizes, layer):
    def plan(src, land, x, y, c):
        me = 2 * x + y
        out = []
        for t in range(len(land)):
            mine = _half(land[t], axes[t], me, sizes[t], layer, c)
            for px, py in _other_chips(x, y):
                out.append((mine, mine, (px, py, c), _half(land[t], axes[t], 2 * px + py, sizes[t], layer, c)))
        return out
    return plan


def _forward_plan(axes, sizes, layer):
    def plan(src, land, x, y, c):
        out = []
        for t in range(len(land)):
            for px, py in _other_chips(x, y):
                got = _half(land[t], axes[t], 2 * px + py, sizes[t], layer, c)
                out.append((got, got, (x, y, 1 - c), _half(land[t], axes[t], 2 * px + py, sizes[t], layer, 1 - c)))
        return out
    return plan


def _place_own(name, w, axis, chip):
    nl, r, cs = w.shape
    tr = _rows_tile(r, cs)
    nb = r // tr
    full = (nl, 4 * r, cs) if axis == 1 else (nl, r, 4 * cs)

    def body(m_ref, w_ref, o_ref):
        o_ref[...] = w_ref[...].astype(BF16)

    if axis == 1:
        ospec = pl.BlockSpec((None, tr, cs), lambda l, i, m: (l, m[0] * nb + i, 0))
    else:
        ospec = pl.BlockSpec((None, tr, cs), lambda l, i, m: (l, i, m[0]))
    return pl.pallas_call(
        body, name=name,
        grid_spec=pltpu.PrefetchScalarGridSpec(
            num_scalar_prefetch=1, grid=(nl, nb),
            in_specs=[pl.BlockSpec((None, tr, cs), lambda l, i, m: (l, i, 0))], out_specs=ospec),
        out_shape=jax.ShapeDtypeStruct(full, BF16), compiler_params=_params(("parallel", "parallel")),
    )(chip, w)


def _scatter_plan(axes, sizes):
    def plan(src, land, x, y, c):
        out = []
        for t in range(len(src)):
            for k, (px, py) in enumerate(_other_chips(x, y)):
                out.append((_region(src[t], axes[t], 2 * px + py, sizes[t]).at[0], land[t].at[k], (px, py, c),
                            land[t].at[k]))
        return out
    return plan


def _pair_plan(src, land, x, y, c):
    return [(src[t], land[t], (x, y, 1 - c), land[t]) for t in range(len(src))]


UNIT_STEPS = 16


def _unit_rows(r):
    return min(t for t in range(16, r + 1, 16) if r % t == 0 and r // t <= UNIT_STEPS)


def _guarded(i, nb, steps, work):
    if nb == steps:
        work()
    else:
        pl.when(i < nb)(work)


def _unit_chip_sum(name, gs, slots, axes, chip):
    n = len(gs)
    dims = [s.shape[1:] for s in slots]
    trs = [_unit_rows(r) for r, _ in dims]
    nbs = [r // tr for (r, _), tr in zip(dims, trs)]
    steps = max(nbs)

    def body(m_ref, *refs):
        i = pl.program_id(0)
        for t in range(n):
            def work(t=t):
                acc = refs[t][...].astype(F32)
                for k in range(3):
                    acc = acc + refs[n + t][k].astype(F32)
                refs[2 * n + t][...] = acc.astype(BF16)
            _guarded(i, nbs[t], steps, work)

    gspecs, sspecs, ospecs = [], [], []
    for (r, cs), tr, nb, axis in zip(dims, trs, nbs, axes):
        if axis == 1:
            gspecs.append(pl.BlockSpec((tr, cs), lambda i, m, nb=nb: (m[0] * nb + jnp.minimum(i, nb - 1), 0)))
        else:
            gspecs.append(pl.BlockSpec((tr, cs), lambda i, m, nb=nb: (jnp.minimum(i, nb - 1), m[0])))
        sspecs.append(pl.BlockSpec((3, tr, cs), lambda i, m, nb=nb: (0, jnp.minimum(i, nb - 1), 0)))
        ospecs.append(pl.BlockSpec((tr, cs), lambda i, m, nb=nb: (jnp.minimum(i, nb - 1), 0)))
    return pl.pallas_call(
        body, name=name,
        grid_spec=pltpu.PrefetchScalarGridSpec(num_scalar_prefetch=1, grid=(steps,), in_specs=gspecs + sspecs,
                                               out_specs=ospecs),
        out_shape=[jax.ShapeDtypeStruct(d, BF16) for d in dims], compiler_params=_params(("arbitrary",)),
    )(chip, *gs, *slots)


def _unit_adamw(name, mine, theirs, ws, ms, vs, layer, bufs):
    n = len(mine)
    dims = [a.shape for a in mine]
    trs = [_unit_rows(r) for r, _ in dims]
    nbs = [r // tr for (r, _), tr in zip(dims, trs)]
    steps = max(nbs)
    c1 = 1.0 - B1 ** STEP
    c2 = 1.0 - B2 ** STEP

    def body(*refs):
        i = pl.program_id(0)
        outs = refs[9 * n:]
        for t in range(n):
            def work(t=t):
                a_ref, b_ref, w_ref, m_ref, v_ref = refs[5 * t:5 * t + 5]
                g_ref, d_ref, mo_ref, vo_ref = outs[4 * t:4 * t + 4]
                gv = a_ref[...].astype(F32) + b_ref[...].astype(F32)
                mn = B1 * m_ref[...] + (1.0 - B1) * gv
                vn = B2 * v_ref[...] + (1.0 - B2) * (gv * gv)
                g_ref[...] = gv
                mo_ref[...] = mn
                vo_ref[...] = vn
                d_ref[...] = -LR * ((mn / c1) / (jnp.sqrt(vn / c2) + ADAM_EPS) + WD * w_ref[...])
            _guarded(i, nbs[t], steps, work)

    in_specs, out_specs, ops = [], [], []
    for t, ((r, cs), tr, nb) in enumerate(zip(dims, trs, nbs)):
        blk = pl.BlockSpec((tr, cs), lambda i, nb=nb: (jnp.minimum(i, nb - 1), 0))
        lay = pl.BlockSpec((None, tr, cs), lambda i, nb=nb: (layer, jnp.minimum(i, nb - 1), 0))
        in_specs += [blk, blk, lay, lay, lay]
        out_specs += [lay] * 4
        ops += [mine[t], theirs[t], ws[t], ms[t], vs[t]]
    flat = [b for bs in bufs for b in bs]
    out = pl.pallas_call(
        body, name=name, grid=(steps,), in_specs=in_specs + [ANY] * (4 * n), out_specs=out_specs,
        out_shape=[jax.ShapeDtypeStruct(b.shape, b.dtype) for b in flat],
        input_output_aliases={5 * n + k: k for k in range(4 * n)}, compiler_params=_params(("arbitrary",)),
    )(*ops, *flat)
    return [list(out[4 * t:4 * t + 4]) for t in range(n)]


def _small_pair_sum(p):
    rows = p.shape[0]
    half = rows // 2

    def body(p_ref, o_ref, sib, send, recv):
        x, y, c = _place()
        mine = pl.ds(pl.multiple_of(c * half, half), half)
        theirs = pl.ds(pl.multiple_of((1 - c) * half, half), half)
        pair = pltpu.make_async_remote_copy(src_ref=p_ref.at[theirs], dst_ref=sib, send_sem=send, recv_sem=recv,
                                            device_id=(x, y, 1 - c), device_id_type=MESH)
        pair.start()
        pair.wait()
        o_ref[...] = (p_ref[mine] + sib[...]).astype(BF16)

    vm = pl.BlockSpec(memory_space=pltpu.VMEM)
    return pl.pallas_call(
        body, name="small_pair_sum", in_specs=[vm], out_specs=vm, out_shape=jax.ShapeDtypeStruct((half, 128), BF16),
        scratch_shapes=[pltpu.VMEM((half, 128), F32), pltpu.SemaphoreType.DMA, pltpu.SemaphoreType.DMA],
        compiler_params=pltpu.CompilerParams(vmem_limit_bytes=VMEM_LIMIT),
    )(p)


def _small_plan(src, land, x, y, c):
    return [(src[0], land[0].at[k], (px, py, c), land[0].at[k]) for k, (px, py) in enumerate(_other_chips(x, y))]


def _small_total(own, slots):
    half = own.shape[0]

    def body(own_ref, slots_ref, o_ref, sums, send, recv):
        x, y, c = _place()
        me = 2 * x + y
        mine = pl.ds(pl.multiple_of(c * half, half), half)
        theirs = pl.ds(pl.multiple_of((1 - c) * half, half), half)
        sums[me] = own_ref[...]
        for k, (px, py) in enumerate(_other_chips(x, y)):
            sums[2 * px + py] = slots_ref[k]
        acc = sums[0].astype(F32)
        for k in range(1, 4):
            acc = acc + sums[k].astype(F32)
        o_ref[mine] = acc
        back = pltpu.make_async_remote_copy(src_ref=o_ref.at[mine], dst_ref=o_ref.at[mine], send_sem=send,
                                            recv_sem=recv, device_id=(x, y, 1 - c), device_id_type=MESH)
        back.start()
        pltpu.make_async_remote_copy(src_ref=o_ref.at[theirs], dst_ref=o_ref.at[theirs], send_sem=send,
                                     recv_sem=recv, device_id=(x, y, c), device_id_type=MESH).wait_recv()
        back.wait_send()

    vm = pl.BlockSpec(memory_space=pltpu.VMEM)
    return pl.pallas_call(
        body, name="small_total", in_specs=[vm, vm], out_specs=vm,
        out_shape=jax.ShapeDtypeStruct((2 * half, 128), F32),
        scratch_shapes=[pltpu.VMEM((4, half, 128), BF16), pltpu.SemaphoreType.DMA, pltpu.SemaphoreType.DMA],
        compiler_params=pltpu.CompilerParams(vmem_limit_bytes=VMEM_LIMIT),
    )(own, slots)


def _small_adamw(gs, ws, ms, vs):
    n = len(gs)
    c1 = 1.0 - B1 ** STEP
    c2 = 1.0 - B2 ** STEP

    def body(*refs):
        for t in range(n):
            gv, wv = refs[t][...], refs[n + t][...]
            d_ref, mo_ref, vo_ref = refs[4 * n + 3 * t:4 * n + 3 * t + 3]
            mn = B1 * refs[2 * n + t][...] + (1.0 - B1) * gv
            vn = B2 * refs[3 * n + t][...] + (1.0 - B2) * (gv * gv)
            mo_ref[...] = mn
            vo_ref[...] = vn
            d_ref[...] = -LR * ((mn / c1) / (jnp.sqrt(vn / c2) + ADAM_EPS) + WD * wv)

    vm = pl.BlockSpec(memory_space=pltpu.VMEM)
    out = pl.pallas_call(
        body, name="adamw_small", in_specs=[vm] * (4 * n), out_specs=[vm] * (3 * n),
        out_shape=[jax.ShapeDtypeStruct(w.shape, F32) for w in ws for _ in range(3)],
        compiler_params=pltpu.CompilerParams(vmem_limit_bytes=VMEM_LIMIT),
    )(*gs, *ws, *ms, *vs)
    return [tuple(out[3 * t:3 * t + 3]) for t in range(n)]


def _pack(parts):
    flat = []
    for p in parts:
        v = p.reshape(-1).astype(F32)
        flat.append(jnp.pad(v, (0, (-v.shape[0]) % 128)))
    v = jnp.concatenate(flat)
    v = jnp.pad(v, (0, (-v.shape[0]) % (512 * 128)))
    return v.reshape(-1, 128)


def _unpack(buf, like):
    out, r0 = [], 0
    for p in like:
        nelem = 1
        for s in p.shape:
            nelem *= s
        rows = -(-nelem // 128)
        blk = buf[r0:r0 + rows]
        if nelem % 128:
            blk = blk.reshape(-1)[:nelem]
        out.append(blk.reshape(p.shape))
        r0 += rows
    return out


def kernel(x, mem, positions, mem_norm_g, mix_pre_g, mix_post_g, w_in, gm_v_g, gm_w_s, gm_b_s, pool_w, pool_scale, attn_sinks, w_o, x_pre_g, x_post_g, w_xq, w_xkv, w_xo, ffn_pre_g, ffn_post_g, w_gate_up, w_down, loss_target, m_mem_norm_g, m_mix_pre_g, m_mix_post_g, m_w_in, m_gm_v_g, m_gm_w_s, m_gm_b_s, m_pool_w, m_pool_scale, m_attn_sinks, m_w_o, m_x_pre_g, m_x_post_g, m_w_xq, m_w_xkv, m_w_xo, m_ffn_pre_g, m_ffn_post_g, m_w_gate_up, m_w_down, v_mem_norm_g, v_mix_pre_g, v_mix_post_g, v_w_in, v_gm_v_g, v_gm_w_s, v_gm_b_s, v_pool_w, v_pool_scale, v_attn_sinks, v_w_o, v_x_pre_g, v_x_post_g, v_w_xq, v_w_xkv, v_w_xo, v_ffn_pre_g, v_ffn_post_g, v_w_gate_up, v_w_down):
    args = (x, mem, positions, mem_norm_g, mix_pre_g, mix_post_g, w_in, gm_v_g, gm_w_s, gm_b_s, pool_w, pool_scale, attn_sinks, w_o, x_pre_g, x_post_g, w_xq, w_xkv, w_xo, ffn_pre_g, ffn_post_g, w_gate_up, w_down)
    moms_m = (m_mem_norm_g, m_mix_pre_g, m_mix_post_g, m_w_in, m_gm_v_g, m_gm_w_s, m_gm_b_s, m_pool_w, m_pool_scale, m_attn_sinks, m_w_o, m_x_pre_g, m_x_post_g, m_w_xq, m_w_xkv, m_w_xo, m_ffn_pre_g, m_ffn_post_g, m_w_gate_up, m_w_down)
    moms_v = (v_mem_norm_g, v_mix_pre_g, v_mix_post_g, v_w_in, v_gm_v_g, v_gm_w_s, v_gm_b_s, v_pool_w, v_pool_scale, v_attn_sinks, v_w_o, v_x_pre_g, v_x_post_g, v_w_xq, v_w_xkv, v_w_xo, v_ffn_pre_g, v_ffn_post_g, v_w_gate_up, v_w_down)
    P = dict(zip(NAMES, args))
    P['loss_target'] = loss_target
    M = dict(zip(WEIGHTS, moms_m))
    V = dict(zip(WEIGHTS, moms_v))
    depth = w_in.shape[0]
    nbig = len(BIG)
    axes = [BIG_AXIS[n] for n in BIG]
    sizes = [P[n].shape[a] for n, a in zip(BIG, axes)]
    chip = (2 * lax.axis_index("x") + lax.axis_index("y")).astype(jnp.int32).reshape(1)

    groups = [['w_in'], ['w_o', 'w_xq', 'w_xkv', 'w_xo'], ['w_gate_up', 'w_down']]
    units = [(l, g) for l in range(depth) for g in groups]
    unit_of = {(l, n): i for i, (l, names) in enumerate(units) for n in names}
    ax = lambda names: [BIG_AXIS[n] for n in names]
    sz = lambda names: [P[n].shape[BIG_AXIS[n]] for n in names]

    full = {n: _place_own("place_" + n, P[n], BIG_AXIS[n], chip) for n in BIG}
    gathers, land, tok = _split_start_many(
        "gather_start", [full[n] for n in BIG],
        [([BIG.index(n) for n in names], 3 * len(names), _gather_plan(ax(names), sz(names), l)) for l, names in units])
    full.update(zip(BIG, land))
    P['first_dep'] = tok[:1, :1]
    forwards, gathered = {}, set()

    def forward_unit(i, after, carried=()):
        ul, unames = units[i]
        send, recv = gathers[i]
        send, recv, land, thru = _split_wait_start(
            "gather_pass%d" % i, send, recv, [full[n] for n in unames], after,
            _gather_plan(ax(unames), sz(unames), ul), 3 * len(unames), _forward_plan(ax(unames), sz(unames), ul),
            carried=[full[n] for n in carried])
        full.update(zip(unames, land))
        full.update(zip(carried, thru))
        forwards[i] = (send, recv)

    def weights_of(l, names, after):
        i = unit_of[(l, names[0])]
        if i not in gathered:
            _, unames = units[i]
            if i not in forwards:
                forward_unit(i, after)
            send, recv = forwards.pop(i)
            _, land = _split_wait("gather_wait%d" % i, send, recv, [], [full[n] for n in unames], after,
                                  _forward_plan(ax(unames), sz(unames), l))
            full.update(zip(unames, land))
            gathered.add(i)
            if len(groups) <= i + 1 < len(units):
                forward_unit(i + 1, after, carried=[n for n in unames if n not in units[i + 1][1]])
        return {n: (full[n], l) for n in names}

    outs = {n: [lax.empty(P[n].shape, F32) for _ in range(4)] for n in BIG}
    gunits = [(l, BIG) for l in range(depth - 1, 0, -1)] + [
        (0, g) for g in (['w_gate_up', 'w_down'], ['w_xq', 'w_xkv', 'w_xo'], ['w_o'], ['w_in'])]
    collected, scatters, pairs = {}, {}, {}

    def finish_scatter(i, after):
        _, names = gunits[i]
        send, recv, g_l, slots = scatters.pop(i)
        g_l, slots = _split_wait("scatter_wait%d" % i, send, recv, g_l, slots, after,
                                 _scatter_plan(ax(names), sz(names)))
        mine = _unit_chip_sum("chip_sum", [g.reshape(g.shape[1:]) for g in g_l], slots, ax(names), chip)
        send, recv, mine, theirs, tok = _split_start("pair_start%d" % i, mine, [lax.empty(a.shape, BF16) for a in mine],
                                                     len(names), _pair_plan)
        pairs[i] = (send, recv, mine, theirs)
        return tok[:1, :1]

    def finish_pair(i, after):
        l, names = gunits[i]
        send, recv, mine, theirs = pairs.pop(i)
        mine, theirs = _split_wait("pair_wait%d" % i, send, recv, mine, theirs, after, _pair_plan)
        new = _unit_adamw("adamw", mine, theirs, [P[n] for n in names], [M[n] for n in names], [V[n] for n in names],
                          l, [outs[n] for n in names])
        outs.update(zip(names, new))

    calls = {'n': 0}
    lag = 4

    def grads_of(l, g_part, after):
        collected.update({(l, n): g for n, g in g_part.items()})
        calls['n'] += 1
        now = calls['n']
        tok = jnp.zeros((1, 1), F32)
        for i, (ul, names) in enumerate(gunits):
            if ul != l or ('started', i) in collected or any((l, n) not in collected for n in names):
                continue
            collected[('started', i)] = now
            srcs = [collected[(l, n)].reshape((1,) + collected[(l, n)].shape) for n in names]
            send, recv, srcs, slots, t = _split_start("scatter_start%d" % i, srcs,
                                                      [lax.empty((3,) + P[n].shape[1:], BF16) for n in names],
                                                      3 * len(names), _scatter_plan(ax(names), sz(names)))
            scatters[i] = (send, recv, srcs, slots)
            tok = tok + t[:1, :1]
        for i in sorted(pairs):
            if collected[('summed', i)] + lag <= now:
                finish_pair(i, after)
        for i in sorted(scatters):
            if collected[('started', i)] + lag <= now:
                tok = tok + finish_scatter(i, after)
                collected[('summed', i)] = now
        return tok

    loss_part, dx, small_g = _fwd_bwd(P, weights_of, grads_of)
    loss = lax.psum(loss_part[0, 0], ("x", "y", "c"))
    grad_x = dx.reshape(x.shape)

    small_like = [P[n] for n in SMALL]
    half_sum = _small_pair_sum(_pack(small_g))
    send, recv, (half_sum,), (slots,), _ = _split_start(
        "small_start", [half_sum], [lax.empty((3,) + half_sum.shape, BF16)], 3, _small_plan)

    for i in sorted(pairs):
        finish_pair(i, half_sum)
    for i in sorted(scatters):
        finish_scatter(i, half_sum)
    for i in sorted(pairs):
        finish_pair(i, half_sum)
    grads, deltas, new_m, new_v = {}, {}, {}, {}
    for n in BIG:
        grads[n], deltas[n], new_m[n], new_v[n] = outs[n]

    (half_sum,), (slots,) = _split_wait("small_wait", send, recv, [half_sum], [slots], grads[BIG[0]], _small_plan)
    two_d = lambda a: a.reshape(-1, a.shape[-1])
    gs = _unpack(_small_total(half_sum, slots), small_like)
    upd = _small_adamw([two_d(g) for g in gs], [two_d(P[n]) for n in SMALL], [two_d(M[n]) for n in SMALL],
                       [two_d(V[n]) for n in SMALL])
    for n, g, (dlt, mn, vn) in zip(SMALL, gs, upd):
        shape = P[n].shape
        grads[n], deltas[n], new_m[n], new_v[n] = g, dlt.reshape(shape), mn.reshape(shape), vn.reshape(shape)

    return (loss, grad_x, *[grads[n] for n in WEIGHTS], *[deltas[n] for n in WEIGHTS],
            *[new_m[n] for n in WEIGHTS], *[new_v[n] for n in WEIGHTS])


def _fwd_bwd(P, weights_of, grads_of):
    (x, mem, positions, mem_norm_g, mix_pre_g, mix_post_g, w_in, gm_v_g, gm_w_s, gm_b_s, pool_w, pool_scale, attn_sinks,
     w_o, x_pre_g, x_post_g, w_xq, w_xkv, w_xo, ffn_pre_g, ffn_post_g, w_gate_up, w_down) = [P[n] for n in NAMES]
    x0 = x[0]
    s, d = x0.shape
    depth = w_in.shape[0]
    tgt = P['loss_target'][0]
    tmn = 256
    tmr = min(512, s)
    tmp = min(1024, s)
    tkw = min(2048, s)

    half = HEAD // 2
    inv = ROPE_THETA ** (-jnp.arange(half, dtype=F32) / half)
    ang = positions[0].astype(F32)[:, None] * inv
    cos, sin = jnp.cos(ang), jnp.sin(ang)
    cosq = jnp.tile(jnp.concatenate([cos, cos], axis=-1), (1, 2))
    sinq = jnp.tile(jnp.concatenate([-sin, sin], axis=-1), (1, 2))

    row = lambda a, l: a[l].reshape(1, -1)
    memn, _ = _prenorm("mem_norm", mem[0], mem_norm_g.reshape(1, d), tmn)
    post_pre = [BF16, F32, BF16, (F32, 1), (F32, 1)]
    pw_bd = []
    for l in range(depth):
        bd = jnp.zeros((256, 256), F32)
        for g in range(4):
            bd = lax.dynamic_update_slice(bd, pool_w[l, g], (64 * g, 64 * g))
        pw_bd.append(bd)

    saved = []
    xc = x0
    h, rx = _prenorm("pre_norm0", x0, row(mix_pre_g, 0) + P['first_dep'], tmn)
    for l in range(depth):
        W = weights_of(l, ['w_in'], xc)
        sv = {'x0': xc, 'h1': h, 'r0': rx}
        z, = _mm_rows("fwd_w_in", h, *W['w_in'], 'nn', tm=tmp, rows_out=[F32], epilogue=_plain_rows)
        abc = _mixer_fwd("mixer_fwd", z, cosq, sinq, row(gm_v_g, l), gm_w_s[l], gm_b_s[l].T, pw_bd[l],
                         row(pool_scale, l), row(attn_sinks, l))
        W.update(weights_of(l, ['w_o'], z))
        mix, xc, h, ry, rx = _mm_rows("fwd_w_o", abc, *W['w_o'], 'nn', tm=tmr, rows_in=[xc],
                                      params=[row(mix_post_g, l), row(x_pre_g, l)], rows_out=post_pre,
                                      epilogue=_post_pre_rows)
        sv.update(z=z, abc=abc, mix=mix, x1=xc, h2=h, r_mix=ry, r1=rx)
        W.update(weights_of(l, ['w_xq', 'w_xkv', 'w_xo'], xc))
        q, = _mm_rows("fwd_w_xq", h, *W['w_xq'], 'nn', tm=tmp, rows_out=[BF16], epilogue=_plain_rows)
        kv = _mm_nn("fwd_w_xkv", memn, *W['w_xkv'], tm=256, tn=512, tk=d, out_dtype=BF16)
        o = _xattn_fwd("xattn_fwd", q, kv, 512)
        xo, xc, h, ry, rx = _mm_rows("fwd_w_xo", o, *W['w_xo'], 'nn', tm=tmr, rows_in=[xc],
                                     params=[row(x_post_g, l), row(ffn_pre_g, l)], rows_out=post_pre,
                                     epilogue=_post_pre_rows)
        sv.update(q=q, kv=kv, o=o, xo=xo, x2=xc, h3=h, r_xo=ry, r2=rx)
        W.update(weights_of(l, ['w_gate_up', 'w_down'], xc))
        dff = W['w_down'][0].shape[1]
        gate, up, act = _ffn_up("ffn_up", h, *W['w_gate_up'], 512, dff // 2)
        sv.update(gate=gate, up=up, act=act)
        if l + 1 < depth:
            f, xc, h, ry, rx = _mm_rows("fwd_w_down", act, *W['w_down'], 'nn', tm=tmr, rows_in=[xc],
                                        params=[row(ffn_post_g, l), row(mix_pre_g, l + 1)], rows_out=post_pre,
                                        epilogue=_post_pre_rows)
            sv.update(f=f, r_f=ry)
        saved.append(sv)
    gs = {n: [None] * depth for n in SMALL if n != 'mem_norm_g'}
    dx, dfn, gs['ffn_post_g'][depth - 1], loss_part = _mm_rows(
        "fwd_w_down_loss", saved[-1]['act'], *W['w_down'], 'nn', tm=tmr, rows_in=[xc, tgt],
        params=[row(ffn_post_g, depth - 1)], rows_out=[F32, BF16], n_sums=2, epilogue=_make_loss_rows(d))

    dmemn = None
    tok = jnp.zeros((1, 1), F32)
    for l in reversed(range(depth)):
        sv, W, G = saved[l], weights_of(l, BIG, dx), {}
        G['w_down'] = _mm_tn("dw_down", sv['act'], dfn, tm=dff // 2, tn=d, tk=tkw)
        dgu = _ffn_act_bwd("ffn_act_bwd", dfn, *W['w_down'], sv['gate'], sv['up'], 256)
        G['w_gate_up'] = _mm_tn("dw_gate_up", sv['h3'], dgu, tm=d, tn=dff // 2, tk=tkw)
        dx, dxo, gs['ffn_pre_g'][l], gs['x_post_g'][l] = _mm_rows(
            "bwd_w_gate_up", dgu, *W['w_gate_up'], 'nt', tm=tmr, rows_in=[sv['x2'], dx, sv['xo'], sv['r2'], sv['r_xo']],
            params=[row(ffn_pre_g, l) + tok, row(x_post_g, l)], rows_out=[F32, BF16], n_sums=2, epilogue=_bwd_rows)
        tok = grads_of(l, {n: G[n] for n in ('w_gate_up', 'w_down')}, dx)
        G['w_xo'] = _mm_tn("dw_xo", sv['o'], dxo, tm=d, tn=d, tk=tkw)
        do, = _mm_rows("bwd_w_xo", dxo, *W['w_xo'], 'nt', tm=tmp, params=[jnp.zeros((1, d), F32) + tok],
                       rows_out=[BF16], epilogue=_plain_rows)
        dq, dkv = _xattn_bwd("xattn_bwd", sv['q'], sv['kv'], do, 512)
        dkv = dkv.astype(BF16)
        G['w_xkv'] = _mm_tn("dw_xkv", memn, dkv, tm=d, tn=d, tk=mem.shape[1])
        dmemn = _mm_nt("bwd_w_xkv", dkv, *W['w_xkv'], tm=mem.shape[1], tn=512, tk=2 * d, out_dtype=F32, add=dmemn)
        G['w_xq'] = _mm_tn("dw_xq", sv['h2'], dq, tm=d, tn=d, tk=tkw)
        dx, dmix, gs['x_pre_g'][l], gs['mix_post_g'][l] = _mm_rows(
            "bwd_w_xq", dq, *W['w_xq'], 'nt', tm=tmr, rows_in=[sv['x1'], dx, sv['mix'], sv['r1'], sv['r_mix']],
            params=[row(x_pre_g, l), row(mix_post_g, l)], rows_out=[F32, BF16], n_sums=2, epilogue=_bwd_rows)
        tok = grads_of(l, {n: G[n] for n in ('w_xq', 'w_xkv', 'w_xo')}, dx)
        G['w_o'] = _mm_tn("dw_o", sv['abc'], dmix, tm=d, tn=d, tk=tkw)
        dabc, = _mm_rows("bwd_w_o", dmix, *W['w_o'], 'nt', tm=tmp, params=[jnp.zeros((1, d), F32) + tok],
                         rows_out=[F32], epilogue=_plain_rows)
        tok = grads_of(l, {'w_o': G['w_o']}, dabc)
        dz, dgv, dws, dbt, dpw, dpsc, dsnk = _mixer_bwd(
            "mixer_bwd", sv['z'], dabc, cosq, sinq, row(gm_v_g, l) + tok, gm_w_s[l], gm_b_s[l].T, pw_bd[l],
            row(pool_scale, l), row(attn_sinks, l))
        gs['gm_v_g'][l] = dgv
        gs['gm_w_s'][l] = dws
        gs['gm_b_s'][l] = dbt.T
        gs['pool_w'][l] = jnp.stack([dpw[64 * g:64 * (g + 1), 64 * g:64 * (g + 1)] for g in range(4)])
        gs['pool_scale'][l] = dpsc
        gs['attn_sinks'][l] = dsnk
        G['w_in'] = _mm_tn("dw_in", sv['h1'], dz, tm=d, tn=dz.shape[1], tk=tkw)
        if l > 0:
            dx, dfn, gs['mix_pre_g'][l], gs['ffn_post_g'][l - 1] = _mm_rows(
                "bwd_w_in", dz, *W['w_in'], 'nt', tm=tmr, rows_in=[sv['x0'], dx, saved[l - 1]['f'], sv['r0'], saved[l - 1]['r_f']],
                params=[row(mix_pre_g, l), row(ffn_post_g, l - 1)], rows_out=[F32, BF16], n_sums=2,
                epilogue=_bwd_rows)
        else:
            dx, gs['mix_pre_g'][l] = _mm_rows(
                "bwd_w_in_first", dz, *W['w_in'], 'nt', tm=tmr, rows_in=[sv['x0'], dx, sv['r0']],
                params=[row(mix_pre_g, l)], rows_out=[F32], n_sums=1, epilogue=_bwd_rows_first)
        tok = grads_of(l, {'w_in': G['w_in']}, dx)
    _, dg_mem = _norm_bwd("bwd_mem_norm", mem[0], mem_norm_g.reshape(1, d) + tok, dmemn, None, BF16, tmn)
    small_g = []
    for n in SMALL:
        if n == 'mem_norm_g':
            small_g.append(dg_mem.reshape(P[n].shape))
        else:
            small_g.append(jnp.stack([a.reshape(P[n].shape[1:]) for a in gs[n]]))
    return loss_part, dx, small_g
```

```python
import functools

import jax
import jax.numpy as jnp
from jax import lax
from jax.experimental import pallas as pl
from jax.experimental.pallas import tpu as pltpu

F32 = jnp.float32
BF16 = jnp.bfloat16
EPS = 1e-6
CHUNK = 128
HEAD = 64
ROPE_THETA = 10000.0
POOL_WINDOWS = (2, 4, 8, 16)
LR, B1, B2, ADAM_EPS, WD, STEP = 0.001, 0.9, 0.999, 1e-08, 0.01, 10
MESH = pl.DeviceIdType.MESH
VMEM_LIMIT = 56 * 1024 * 1024

NAMES = ['x', 'mem', 'positions', 'mem_norm_g', 'mix_pre_g', 'mix_post_g', 'w_in', 'gm_v_g', 'gm_w_s', 'gm_b_s',
         'pool_w', 'pool_scale', 'attn_sinks', 'w_o', 'x_pre_g', 'x_post_g', 'w_xq', 'w_xkv', 'w_xo', 'ffn_pre_g',
         'ffn_post_g', 'w_gate_up', 'w_down']
WEIGHTS = NAMES[3:]
BIG = ['w_in', 'w_o', 'w_xq', 'w_xkv', 'w_xo', 'w_gate_up', 'w_down']
BIG_AXIS = {'w_in': 2, 'w_o': 1, 'w_xq': 1, 'w_xkv': 2, 'w_xo': 1, 'w_gate_up': 2, 'w_down': 1}
SMALL = [n for n in WEIGHTS if n not in BIG]

NN = (((1,), (0,)), ((), ()))
NT = (((1,), (1,)), ((), ()))
TN = (((0,), (0,)), ((), ()))


def _dot(a, b, dims=NN):
    return lax.dot_general(a, b, dims, preferred_element_type=F32)


def _params(sem):
    return pltpu.CompilerParams(dimension_semantics=sem, vmem_limit_bytes=VMEM_LIMIT)


STREAM_BLOCK_BYTES = 3 * 512 * 1024


def _rows_tile(rows, cols):
    limit = max(16, STREAM_BLOCK_BYTES // (4 * cols))
    return max(t for t in range(16, min(rows, limit) + 1, 16) if rows % t == 0)


def _mm(name, a, a_spec, b, b_spec, dims, grid, nk, out_shape, out_spec, add=None, add_spec=None):
    acc_shape = out_spec.block_shape

    def body(*refs):
        a_ref, b_ref = refs[0], refs[1]
        pos = 2
        add_ref = None
        if add is not None:
            add_ref = refs[pos]
            pos += 1
        o_ref = refs[pos]
        part = _dot(a_ref[...].astype(BF16), b_ref[...].astype(BF16), dims)
        if nk == 1:
            if add_ref is not None:
                part = part + add_ref[...]
            o_ref[...] = part.astype(o_ref.dtype)
        else:
            acc_ref = refs[pos + 1]
            k = pl.program_id(2)

            @pl.when(k == 0)
            def _():
                acc_ref[...] = part if add_ref is None else part + add_ref[...]

            @pl.when(k > 0)
            def _():
                acc_ref[...] += part

            @pl.when(k == nk - 1)
            def _():
                o_ref[...] = acc_ref[...].astype(o_ref.dtype)

    ops, specs = [a, b], [a_spec, b_spec]
    if add is not None:
        ops.append(add)
        specs.append(add_spec)
    return pl.pallas_call(
        body, name=name, grid=grid, in_specs=specs, out_specs=out_spec, out_shape=out_shape,
        scratch_shapes=[pltpu.VMEM(acc_shape, F32)] if nk > 1 else [],
        compiler_params=_params(("parallel", "parallel", "arbitrary")),
    )(*ops)


def _wspec(block, layer, fn):
    return pl.BlockSpec((None,) + block, lambda i, j, k: (layer,) + fn(i, j, k))


def _mm_nn(name, a, w, layer, *, tm, tn, tk, out_dtype):
    m, kk = a.shape
    n = w.shape[2]
    tm = min(tm, m)
    nk = kk // tk
    return _mm(name, a, pl.BlockSpec((tm, tk), lambda i, j, k: (i, k)),
               w, _wspec((tk, tn), layer, lambda i, j, k: (k, j)), NN,
               (m // tm, n // tn, nk), nk, jax.ShapeDtypeStruct((m, n), out_dtype),
               pl.BlockSpec((tm, tn), lambda i, j, k: (i, j)))


def _mm_nt(name, a, w, layer, *, tm, tn, tk, out_dtype, add=None):
    m, kk = a.shape
    n = w.shape[1]
    tm = min(tm, m)
    nk = kk // tk
    ospec = pl.BlockSpec((tm, tn), lambda i, j, k: (i, j))
    return _mm(name, a, pl.BlockSpec((tm, tk), lambda i, j, k: (i, k)),
               w, _wspec((tn, tk), layer, lambda i, j, k: (j, k)), NT,
               (m // tm, n // tn, nk), nk, jax.ShapeDtypeStruct((m, n), out_dtype), ospec,
               add=add, add_spec=ospec if add is not None else None)


def _mm_tn(name, a, b, *, tm, tn, tk):
    kk, m = a.shape
    n = b.shape[1]
    tk = min(tk, kk)
    nk = kk // tk
    return _mm(name, a, pl.BlockSpec((tk, tm), lambda i, j, k: (k, i)),
               b, pl.BlockSpec((tk, tn), lambda i, j, k: (k, j)), TN,
               (m // tm, n // tn, nk), nk, jax.ShapeDtypeStruct((m, n), BF16),
               pl.BlockSpec((tm, tn), lambda i, j, k: (i, j)))


def _mm_rows(name, a, w, layer, mode, *, tm, rows_in=(), params=(), rows_out=(), n_sums=0, epilogue):
    m, kk = a.shape
    n = w.shape[2] if mode == 'nn' else w.shape[1]
    nr, npar, no = len(rows_in), len(params), len(rows_out)

    def body(*refs):
        a_ref, w_ref = refs[0], refs[1]
        rin = refs[2:2 + nr]
        par = refs[2 + nr:2 + nr + npar]
        outs = refs[2 + nr + npar:2 + nr + npar + no]
        sums = refs[2 + nr + npar + no:2 + nr + npar + no + n_sums]
        acc = _dot(a_ref[...], w_ref[...], NN if mode == 'nn' else NT)
        res, sm = epilogue(acc, [r[...] for r in rin], [p[...] for p in par])
        for r, v in zip(outs, res):
            r[...] = v.astype(r.dtype)

        @pl.when(pl.program_id(0) == 0)
        def _():
            for r in sums:
                r[...] = jnp.zeros_like(r)

        for r, v in zip(sums, sm):
            r[...] += v

    wblock = (None, kk, n) if mode == 'nn' else (None, n, kk)
    rowblk = pl.BlockSpec((tm, n), lambda i: (i, 0))
    one = pl.BlockSpec((1, n), lambda i: (0, 0))
    return pl.pallas_call(
        body, name=name, grid=(m // tm,),
        in_specs=[pl.BlockSpec((tm, kk), lambda i: (i, 0)),
                  pl.BlockSpec(wblock, lambda i: (layer, 0, 0), pipeline_mode=pl.Buffered(1))]
                 + [rowblk] * nr + [one] * npar,
        out_specs=[rowblk] * no + [one] * n_sums,
        out_shape=[jax.ShapeDtypeStruct((m, n), dt) for dt in rows_out] +
                  [jax.ShapeDtypeStruct((1, n), F32)] * n_sums,
        compiler_params=_params(("arbitrary",)),
    )(a, w, *rows_in, *params)


def _rstd(x):
    return lax.rsqrt(jnp.mean(x * x, axis=-1, keepdims=True) + EPS)


def _norm_back(xin, g, dy):
    r = _rstd(xin)
    xh = xin * r
    dyg = dy * g
    return r * (dyg - xh * jnp.mean(dyg * xh, axis=-1, keepdims=True)), jnp.sum(dy * xh, axis=0, keepdims=True)


def _plain_rows(acc, rows, pars):
    return [acc], []


def _post_pre_rows(y, rows, pars):
    xn = rows[0] + y * _rstd(y) * pars[0]
    return [y, xn, xn * _rstd(xn) * pars[1]], []


def _make_loss_rows(d):
    def fn(y, rows, pars):
        x, tgt = rows
        err = x + y * _rstd(y) * pars[0] - tgt
        dout = err * (1.0 / d)
        dy, dg = _norm_back(y, pars[0], dout)
        lsum = 0.5 * jnp.sum(jnp.mean(err * err, axis=-1, keepdims=True), axis=0, keepdims=True)
        return [dout, dy], [dg, jnp.broadcast_to(lsum, dg.shape)]
    return fn


def _bwd_rows(dh, rows, pars):
    xin, resid, yprev = rows
    dxa, dg_pre = _norm_back(xin, pars[0], dh)
    dx = resid + dxa
    dyp, dg_post = _norm_back(yprev.astype(F32), pars[1], dx)
    return [dx, dyp], [dg_pre, dg_post]


def _bwd_rows_first(dh, rows, pars):
    xin, resid = rows
    dxa, dg_pre = _norm_back(xin, pars[0], dh)
    return [resid + dxa], [dg_pre]


def _row(d):
    return pl.BlockSpec((1, d), lambda i: (0, 0))


def _prenorm(name, x, g, tm):
    m, d = x.shape

    def body(x_ref, g_ref, o_ref):
        xv = x_ref[...]
        o_ref[...] = (xv * _rstd(xv) * g_ref[...]).astype(BF16)

    blk = pl.BlockSpec((tm, d), lambda i: (i, 0))
    return pl.pallas_call(body, name=name, grid=(m // tm,), in_specs=[blk, _row(d)], out_specs=blk,
                          out_shape=jax.ShapeDtypeStruct((m, d), BF16), compiler_params=_params(("parallel",)))(x, g)


def _norm_bwd(name, xin, g, dy, resid, out_dtype, tm):
    m, d = xin.shape

    def body(*refs):
        if resid is None:
            x_ref, g_ref, dy_ref, dx_ref, dg_ref = refs
        else:
            x_ref, g_ref, dy_ref, r_ref, dx_ref, dg_ref = refs
        xv = x_ref[...]
        r = _rstd(xv)
        xh = xv * r
        dyv = dy_ref[...].astype(F32)
        dyg = dyv * g_ref[...]
        dx = r * (dyg - xh * jnp.mean(dyg * xh, axis=-1, keepdims=True))
        if resid is not None:
            dx = dx + r_ref[...]
        dx_ref[...] = dx.astype(dx_ref.dtype)

        @pl.when(pl.program_id(0) == 0)
        def _():
            dg_ref[...] = jnp.zeros_like(dg_ref)

        dg_ref[...] += jnp.sum(dyv * xh, axis=0, keepdims=True)

    blk = pl.BlockSpec((tm, d), lambda i: (i, 0))
    ops = [xin, g, dy] + ([] if resid is None else [resid])
    specs = [blk, _row(d), blk] + ([] if resid is None else [blk])
    return pl.pallas_call(
        body, name=name, grid=(m // tm,), in_specs=specs, out_specs=[blk, _row(d)],
        out_shape=[jax.ShapeDtypeStruct((m, d), out_dtype), jax.ShapeDtypeStruct((1, d), F32)],
        compiler_params=_params(("arbitrary",)))(*ops)


def _gelu_parts(x):
    c = 0.7978845608028654
    t = jnp.tanh(c * (x + 0.044715 * (x * x * x)))
    return 0.5 * x * (1.0 + t), t


def _gelu_grad(x, t):
    c = 0.7978845608028654
    return 0.5 * (1.0 + t) + 0.5 * x * (1.0 - t * t) * (c * (1.0 + 3.0 * 0.044715 * x * x))


def _rot_half(x):
    ax = x.ndim - 1
    w = x.shape[ax]
    lane = lax.broadcasted_iota(jnp.int32, x.shape, ax)
    return jnp.where((lane & 63) < 32, pltpu.roll(x, w - 32, ax), pltpu.roll(x, 32, ax))


def _group_mean(x, ones_bd):
    hi = x.astype(BF16)
    lo = (x - hi.astype(F32)).astype(BF16)
    return (_dot(hi, ones_bd) + _dot(lo, ones_bd)) * (1.0 / HEAD)


def _gating(gel, gv, ws_ref, bt, ones_bd, mix_s):
    u = gel[:, :256]
    v = gel[:, 256:]
    r = lax.rsqrt(_group_mean(v * v, ones_bd) + EPS)
    xh = v * r
    vn = (xh * gv).astype(BF16)
    row = lax.broadcasted_iota(jnp.int32, (CHUNK, CHUNK), 0)
    col = lax.broadcasted_iota(jnp.int32, (CHUNK, CHUNK), 1)
    causal = col <= row
    wcs = []
    for g in range(4):
        wc = jnp.where(causal, ws_ref[g], 0.0).astype(BF16)
        wcs.append(wc)
        mix_s[:, HEAD * g:HEAD * (g + 1)] = _dot(wc, vn[:, HEAD * g:HEAD * (g + 1)]) + bt[:, g:g + 1]
    return u, r, xh, vn, wcs, causal, mix_s[...]


def _lane_select(lane, vals):
    return jnp.where(lane < 64, vals[0], jnp.where(lane < 128, vals[1], jnp.where(lane < 192, vals[2], vals[3])))


def _pool_fwd(pc, pp, ci):
    ext = jnp.concatenate([pp, pc], axis=0)
    s2 = ext + pltpu.roll(ext, 1, 0)
    s4 = s2 + pltpu.roll(s2, 2, 0)
    s8 = s4 + pltpu.roll(s4, 4, 0)
    s16 = s8 + pltpu.roll(s8, 8, 0)
    t1 = ci * CHUNK + lax.broadcasted_iota(jnp.int32, (CHUNK, 1), 0) + 1
    lane = lax.broadcasted_iota(jnp.int32, (1, 256), 1)
    cnt = _lane_select(lane, [jnp.minimum(t1, w).astype(F32) for w in POOL_WINDOWS])
    ssel = _lane_select(lane, [s[CHUNK:] for s in (s2, s4, s8, s16)])
    return ssel / cnt - pc, cnt, lane


def _attn_prep(zc, zpkv, cc, sc, cp, sp, ci):
    q = zc[:, 768:1280]
    kc = zc[:, 1280:1408]
    vc = zc[:, 1408:1536]
    kp = zpkv[:, :128]
    vp = zpkv[:, 128:]
    qr = q * jnp.concatenate([cc] * 4, axis=1) + _rot_half(q) * jnp.concatenate([sc] * 4, axis=1)
    krc = kc * cc + _rot_half(kc) * sc
    krp = kp * cp + _rot_half(kp) * sp
    kband = jnp.concatenate([krp, krc], axis=0)
    vband = jnp.concatenate([vp, vc], axis=0)
    key = lax.broadcasted_iota(jnp.int32, (2 * CHUNK, 4 * CHUNK), 0)
    t = lax.broadcasted_iota(jnp.int32, (2 * CHUNK, 4 * CHUNK), 1) & (CHUNK - 1)
    valid = ((key < CHUNK) & (key > t) & (ci > 0)) | ((key >= CHUNK) & (key - CHUNK <= t))
    return qr, kband, vband, valid


SCALE = HEAD ** -0.5


def _stack_heads(x, base, hk):
    return jnp.concatenate([x[:, base + HEAD * (4 * hk + i):base + HEAD * (4 * hk + i + 1)] for i in range(4)], axis=0)


def _sink_row(snk, hk):
    lane = lax.broadcasted_iota(jnp.int32, (1, 4 * CHUNK), 1)
    s = [snk[:, 4 * hk + i:4 * hk + i + 1] for i in range(4)]
    return jnp.where(lane < CHUNK, s[0], jnp.where(lane < 2 * CHUNK, s[1], jnp.where(lane < 3 * CHUNK, s[2], s[3])))


def _group_probs(kh, q4, valid, sink4):
    s = jnp.where(valid, _dot(kh, q4, NT), -1e30)
    mx = jnp.maximum(jnp.max(s, axis=0, keepdims=True), sink4)
    e = jnp.exp(s - mx)
    es = jnp.exp(sink4 - mx)
    inv = 1.0 / (jnp.sum(e, axis=0, keepdims=True) + es)
    return e * inv, es * inv


def _mixer_specs(nb, rev):
    def cur(i):
        return nb - 1 - i if rev else i

    def prev(i):
        return jnp.maximum(cur(i) - 1, 0)

    full = lambda shape: pl.BlockSpec(shape, lambda i: (0,) * len(shape))
    specs = [
        pl.BlockSpec((CHUNK, 1536), lambda i: (cur(i), 0)),
        pl.BlockSpec((CHUNK, 256), lambda i: (prev(i), 2)),
        pl.BlockSpec((CHUNK, 256), lambda i: (prev(i), 5)),
        pl.BlockSpec((CHUNK, 128), lambda i: (cur(i), 0)),
        pl.BlockSpec((CHUNK, 128), lambda i: (cur(i), 0)),
        pl.BlockSpec((CHUNK, 128), lambda i: (prev(i), 0)),
        pl.BlockSpec((CHUNK, 128), lambda i: (prev(i), 0)),
        full((1, 256)), full((4, CHUNK, CHUNK)), full((CHUNK, 4)), full((256, 256)), full((1, 256)), full((1, 8)),
        full((256, 256)),
    ]
    return specs, cur


def _ones_bd():
    g = lax.broadcasted_iota(jnp.int32, (256, 256), 0) // HEAD == lax.broadcasted_iota(jnp.int32, (256, 256), 1) // HEAD
    return g.astype(BF16)


def _mixer_fwd(name, z, cosq, sinq, gv, ws, bt, pw, psc, snk):
    s = z.shape[0]
    nb = s // CHUNK
    specs, _ = _mixer_specs(nb, False)

    def body(zc_ref, zpp_ref, zpkv_ref, cq_ref, sq_ref, cp_ref, sp_ref, gv_ref, ws_ref, bt_ref, pw_ref, psc_ref,
             snk_ref, bd_ref, o_ref, mix_s):
        ci = pl.program_id(0)
        zc = zc_ref[...]
        gel, _ = _gelu_parts(zc[:, :512])
        u, _, _, _, _, _, mixed = _gating(gel, gv_ref[...], ws_ref, bt_ref[...], bd_ref[...], mix_s)
        o_ref[:, :256] = (u * mixed).astype(BF16)
        pp = jnp.where(ci > 0, zpp_ref[...], 0.0)
        pooled, _, _ = _pool_fwd(zc[:, 512:768], pp, ci)
        mp = _dot(pooled.astype(BF16), pw_ref[...].astype(BF16))
        o_ref[:, 256:512] = (mp * psc_ref[...]).astype(BF16)
        qr, kband, vband, valid = _attn_prep(zc, zpkv_ref[...], cq_ref[...], sq_ref[...], cp_ref[...], sp_ref[...], ci)
        snkv = snk_ref[...]
        kb = kband.astype(BF16)
        vt = vband.T
        ots = []
        for hk in range(2):
            q4 = (_stack_heads(qr, 0, hk) * SCALE).astype(BF16)
            p, _ = _group_probs(kb[:, HEAD * hk:HEAD * (hk + 1)], q4, valid, _sink_row(snkv, hk))
            ots.append(_dot(vt[HEAD * hk:HEAD * (hk + 1), :].astype(BF16), p.astype(BF16)))
        o = jnp.concatenate(ots, axis=0).T
        for hk in range(2):
            for i in range(4):
                h = 4 * hk + i
                o_ref[:, 512 + HEAD * h:512 + HEAD * (h + 1)] = o[CHUNK * i:CHUNK * (i + 1),
                                                                  HEAD * hk:HEAD * (hk + 1)].astype(BF16)

    return pl.pallas_call(
        body, name=name, grid=(nb,), in_specs=specs, out_specs=pl.BlockSpec((CHUNK, 1024), lambda i: (i, 0)),
        out_shape=jax.ShapeDtypeStruct((s, 1024), BF16), scratch_shapes=[pltpu.VMEM((CHUNK, 256), F32)],
        compiler_params=_params(("parallel",)),
    )(z, z, z, cosq, sinq, cosq, sinq, gv, ws, bt, pw, psc, snk, _ones_bd())


def _mixer_bwd(name, z, dabc, cosq, sinq, gv, ws, bt, pw, psc, snk):
    s = z.shape[0]
    nb = s // CHUNK
    specs, cur = _mixer_specs(nb, True)
    specs = specs + [pl.BlockSpec((CHUNK, 1024), lambda i: (cur(i), 0))]
    full = lambda shape: pl.BlockSpec(shape, lambda i: (0,) * len(shape))
    acc_shapes = [(1, 256), (4, CHUNK, CHUNK), (CHUNK, 4), (256, 256), (1, 256), (1, 8)]

    def body(zc_ref, zpp_ref, zpkv_ref, cq_ref, sq_ref, cp_ref, sp_ref, gv_ref, ws_ref, bt_ref, pw_ref, psc_ref,
             snk_ref, bd_ref, dabc_ref, dz_ref, dgv_ref, dws_ref, dbt_ref, dpw_ref, dpsc_ref, dsnk_ref,
             cpool, ck, cv, dq_s, dkv_s, mix_s, dvn_s):
        step = pl.program_id(0)
        ci = nb - 1 - step

        @pl.when(step == 0)
        def _():
            for r in (dgv_ref, dws_ref, dbt_ref, dpw_ref, dpsc_ref, dsnk_ref, cpool, ck, cv):
                r[...] = jnp.zeros_like(r)

        zc = zc_ref[...]
        dabc = dabc_ref[...]
        zg = zc[:, :512]
        gel, th = _gelu_parts(zg)
        gp = _gelu_grad(zg, th)
        gvv = gv_ref[...]
        bd = bd_ref[...]
        u, r, xh, vn, wcs, causal, mixed = _gating(gel, gvv, ws_ref, bt_ref[...], bd, mix_s)
        da = dabc[:, :256]
        dm = da * u
        dmb = dm.astype(BF16)
        lane4 = lax.broadcasted_iota(jnp.int32, (CHUNK, 4), 1)
        dbt = jnp.zeros((CHUNK, 4), F32)
        for g in range(4):
            lo, hi = HEAD * g, HEAD * (g + 1)
            dws_ref[g] += jnp.where(causal, _dot(dmb[:, lo:hi], vn[:, lo:hi], NT), 0.0)
            dbt = dbt + jnp.where(lane4 == g, jnp.sum(dm[:, lo:hi], axis=-1, keepdims=True), 0.0)
            dvn_s[:, lo:hi] = _dot(wcs[g], dmb[:, lo:hi], TN)
        dbt_ref[...] += dbt
        dvn = dvn_s[...]
        dgv_ref[...] += jnp.sum(dvn * xh, axis=0, keepdims=True)
        dxh = dvn * gvv
        dvg = r * (dxh - xh * _group_mean(dxh * xh, bd))
        dz_ref[:, :256] = (da * mixed * gp[:, :256]).astype(BF16)
        dz_ref[:, 256:512] = (dvg * gp[:, 256:]).astype(BF16)
        pc = zc[:, 512:768]
        pp = jnp.where(ci > 0, zpp_ref[...], 0.0)
        pooled, cnt, lane = _pool_fwd(pc, pp, ci)
        pwb = pw_ref[...].astype(BF16)
        pooled_b = pooled.astype(BF16)
        mp = _dot(pooled_b, pwb)
        db = dabc[:, 256:512]
        dpsc_ref[...] += jnp.sum(db * mp, axis=0, keepdims=True)
        dmpb = (db * psc_ref[...]).astype(BF16)
        dpw_ref[...] += _dot(pooled_b, dmpb, TN)
        dpooled = _dot(dmpb, pwb, NT)
        davg = dpooled / cnt
        zero = jnp.zeros((CHUNK, 256), F32)
        d2, d4, d8, d16 = [jnp.concatenate([zero, jnp.where((lane >= 64 * k) & (lane < 64 * (k + 1)), davg, 0.0)],
                                           axis=0) for k in range(4)]
        g8 = d8 + d16 + pltpu.roll(d16, 2 * CHUNK - 8, 0)
        g4 = d4 + g8 + pltpu.roll(g8, 2 * CHUNK - 4, 0)
        g2 = d2 + g4 + pltpu.roll(g4, 2 * CHUNK - 2, 0)
        ge = g2 + pltpu.roll(g2, 2 * CHUNK - 1, 0)
        dz_ref[:, 512:768] = (ge[CHUNK:] - dpooled + cpool[...]).astype(BF16)
        cpool[...] = ge[:CHUNK]
        cc = cq_ref[...]
        sc = sq_ref[...]
        qr, kband, vband, valid = _attn_prep(zc, zpkv_ref[...], cc, sc, cp_ref[...], sp_ref[...], ci)
        snkv = snk_ref[...]
        lane8 = lax.broadcasted_iota(jnp.int32, (1, 8), 1)
        qlane = lax.broadcasted_iota(jnp.int32, (1, 4 * CHUNK), 1)
        dsnk = jnp.zeros((1, 8), F32)
        kb = kband.astype(BF16)
        vb = vband.astype(BF16)
        kt = kband.T * SCALE
        dqts = []
        for hk in range(2):
            kh = kb[:, HEAD * hk:HEAD * (hk + 1)]
            q4 = (_stack_heads(qr, 0, hk) * SCALE).astype(BF16)
            do4 = _stack_heads(dabc, 512, hk).astype(BF16)
            p, ps = _group_probs(kh, q4, valid, _sink_row(snkv, hk))
            dp = _dot(vb[:, HEAD * hk:HEAD * (hk + 1)], do4, NT)
            dd = jnp.sum(p * dp, axis=0, keepdims=True)
            dsink = -ps * dd
            for i in range(4):
                part = jnp.sum(jnp.where((qlane >= CHUNK * i) & (qlane < CHUNK * (i + 1)), dsink, 0.0),
                               axis=1, keepdims=True)
                dsnk = dsnk + jnp.where(lane8 == 4 * hk + i, part, 0.0)
            dsb = (p * (dp - dd)).astype(BF16)
            dqts.append(_dot(kt[HEAD * hk:HEAD * (hk + 1), :].astype(BF16), dsb))
            dkv_s[:, HEAD * hk:HEAD * (hk + 1)] = _dot(dsb, q4)
            dkv_s[:, 128 + HEAD * hk:128 + HEAD * (hk + 1)] = _dot(p.astype(BF16), do4)
        dq4 = jnp.concatenate(dqts, axis=0).T
        for hk in range(2):
            for i in range(4):
                h = 4 * hk + i
                dq_s[:, HEAD * h:HEAD * (h + 1)] = dq4[CHUNK * i:CHUNK * (i + 1), HEAD * hk:HEAD * (hk + 1)]
        dsnk_ref[...] += dsnk
        dqr = dq_s[...]
        dz_ref[:, 768:1280] = (dqr * jnp.concatenate([cc] * 4, axis=1)
                               + _rot_half(dqr * jnp.concatenate([sc] * 4, axis=1))).astype(BF16)
        dkv = dkv_s[...]
        dkr = dkv[CHUNK:, :128] + ck[...]
        dz_ref[:, 1280:1408] = (dkr * cc + _rot_half(dkr * sc)).astype(BF16)
        dz_ref[:, 1408:1536] = (dkv[CHUNK:, 128:] + cv[...]).astype(BF16)
        ck[...] = dkv[:CHUNK, :128]
        cv[...] = dkv[:CHUNK, 128:]

    return pl.pallas_call(
        body, name=name, grid=(nb,), in_specs=specs,
        out_specs=[pl.BlockSpec((CHUNK, 1536), lambda i: (cur(i), 0))] + [full(a) for a in acc_shapes],
        out_shape=[jax.ShapeDtypeStruct((s, 1536), BF16)] + [jax.ShapeDtypeStruct(a, F32) for a in acc_shapes],
        scratch_shapes=[pltpu.VMEM((CHUNK, 256), F32), pltpu.VMEM((CHUNK, 128), F32), pltpu.VMEM((CHUNK, 128), F32),
                        pltpu.VMEM((CHUNK, 512), F32), pltpu.VMEM((2 * CHUNK, 256), F32),
                        pltpu.VMEM((CHUNK, 256), F32), pltpu.VMEM((CHUNK, 256), F32)],
        compiler_params=_params(("arbitrary",)),
    )(z, z, z, cosq, sinq, cosq, sinq, gv, ws, bt, pw, psc, snk, _ones_bd(), dabc)


def _xattn_probs(qh, kh):
    s = _dot(qh, kh, NT) * (256 ** -0.5)
    e = jnp.exp(s - jnp.max(s, axis=-1, keepdims=True))
    return e * (1.0 / jnp.sum(e, axis=-1, keepdims=True))


def _xattn_fwd(name, q, kv, tq):
    s, d = q.shape
    mlen = kv.shape[0]

    def body(q_ref, kv_ref, o_ref):
        for h in range(4):
            lo, hi = 256 * h, 256 * (h + 1)
            p = _xattn_probs(q_ref[:, lo:hi], kv_ref[:, lo:hi])
            o_ref[:, lo:hi] = _dot(p.astype(BF16), kv_ref[:, d + lo:d + hi]).astype(BF16)

    blk = pl.BlockSpec((tq, d), lambda i: (i, 0))
    return pl.pallas_call(body, name=name, grid=(s // tq,),
                          in_specs=[blk, pl.BlockSpec((mlen, 2 * d), lambda i: (0, 0))], out_specs=blk,
                          out_shape=jax.ShapeDtypeStruct((s, d), BF16), compiler_params=_params(("parallel",)))(q, kv)


def _xattn_bwd(name, q, kv, do, tq):
    s, d = q.shape
    mlen = kv.shape[0]

    def body(q_ref, kv_ref, do_ref, dq_ref, dkv_ref):
        @pl.when(pl.program_id(0) == 0)
        def _():
            dkv_ref[...] = jnp.zeros_like(dkv_ref)

        for h in range(4):
            lo, hi = 256 * h, 256 * (h + 1)
            qh = q_ref[:, lo:hi]
            kh = kv_ref[:, lo:hi]
            vh = kv_ref[:, d + lo:d + hi]
            doh = do_ref[:, lo:hi]
            p = _xattn_probs(qh, kh)
            dp = _dot(doh, vh, NT)
            dsb = (p * (dp - jnp.sum(p * dp, axis=-1, keepdims=True)) * (256 ** -0.5)).astype(BF16)
            dq_ref[:, lo:hi] = _dot(dsb, kh).astype(BF16)
            dkv_ref[:, lo:hi] += _dot(dsb, qh, TN)
            dkv_ref[:, d + lo:d + hi] += _dot(p.astype(BF16), doh, TN)

    blk = pl.BlockSpec((tq, d), lambda i: (i, 0))
    kvb = pl.BlockSpec((mlen, 2 * d), lambda i: (0, 0))
    return pl.pallas_call(
        body, name=name, grid=(s // tq,), in_specs=[blk, kvb, blk], out_specs=[blk, kvb],
        out_shape=[jax.ShapeDtypeStruct((s, d), BF16), jax.ShapeDtypeStruct((mlen, 2 * d), F32)],
        compiler_params=_params(("arbitrary",)))(q, kv, do)


def _sigmoid(x):
    return 0.5 * (1.0 + jnp.tanh(0.5 * x))


def _ffn_up(name, h, wgu, layer, tm, tn):
    s, d = h.shape
    dff = wgu.shape[2] // 2
    nj = dff // tn

    def body(h_ref, wg_ref, wu_ref, g_ref, u_ref, a_ref):
        hv = h_ref[...]
        gate = _dot(hv, wg_ref[...])
        up = _dot(hv, wu_ref[...])
        g_ref[...] = gate.astype(BF16)
        u_ref[...] = up.astype(BF16)
        a_ref[...] = (gate * _sigmoid(gate) * up).astype(BF16)

    ob = pl.BlockSpec((tm, tn), lambda j, i: (i, j))
    sd = jax.ShapeDtypeStruct((s, dff), BF16)
    return pl.pallas_call(
        body, name=name, grid=(nj, s // tm),
        in_specs=[pl.BlockSpec((tm, d), lambda j, i: (i, 0)),
                  pl.BlockSpec((None, d, tn), lambda j, i: (layer, 0, j)),
                  pl.BlockSpec((None, d, tn), lambda j, i: (layer, 0, j + nj))],
        out_specs=[ob, ob, ob], out_shape=[sd, sd, sd], compiler_params=_params(("parallel", "parallel")),
    )(h, wgu, wgu)


def _ffn_act_bwd(name, dfn, wdown, layer, gate, up, tm):
    s, d = dfn.shape
    dff = gate.shape[1]

    tn = 256

    def body(df_ref, wd_ref, g_ref, u_ref, o_ref):
        df = df_ref[...]
        for lo in range(0, dff, tn):
            dact = _dot(df, wd_ref[lo:lo + tn, :], NT).astype(BF16)
            gate = g_ref[:, lo:lo + tn]
            sig = _sigmoid(gate)
            gs = gate * sig
            o_ref[:, lo:lo + tn] = dact * u_ref[:, lo:lo + tn] * (sig + gs * (1.0 - sig))
            o_ref[:, dff + lo:dff + lo + tn] = dact * gs

    gb = pl.BlockSpec((tm, dff), lambda i: (i, 0))
    return pl.pallas_call(
        body, name=name, grid=(s // tm,),
        in_specs=[pl.BlockSpec((tm, d), lambda i: (i, 0)),
                  pl.BlockSpec((None, dff, d), lambda i: (layer, 0, 0)), gb, gb],
        out_specs=pl.BlockSpec((tm, 2 * dff), lambda i: (i, 0)),
        out_shape=jax.ShapeDtypeStruct((s, 2 * dff), BF16), compiler_params=_params(("parallel",)),
    )(dfn, wdown, gate, up)


def _place():
    return lax.axis_index("x"), lax.axis_index("y"), lax.axis_index("c")


def _other_chips(x, y):
    return [(1 - x, y), (x, 1 - y), (1 - x, 1 - y)]


def _region(ref, axis, chip, size):
    start = pl.multiple_of(chip * size, size)
    if axis == 1:
        return ref.at[:, pl.ds(start, size), :]
    return ref.at[:, :, pl.ds(start, size)]


ANY = pl.BlockSpec(memory_space=pl.ANY)


HBM = pl.BlockSpec(memory_space=pltpu.HBM)
SEM = pl.BlockSpec(memory_space=pltpu.SEMAPHORE)
EFFECT = pltpu.SideEffectType.DATAFLOW_SIDE_EFFECTING


def _in_hbm(a):
    return pltpu.with_memory_space_constraint(a, pltpu.HBM)


def _split_start(name, srcs, lands, ncopies, plan):
    ns, nl = len(srcs), len(lands)

    def body(*refs):
        src, land = refs[:ns], refs[ns:ns + nl]
        send, recv = refs[ns + nl], refs[ns + nl + 1]
        token = refs[-1]
        x, y, c = _place()
        for k, (s_ref, d_ref, peer, _) in enumerate(plan(src, land, x, y, c)):
            pltpu.make_async_remote_copy(src_ref=s_ref, dst_ref=d_ref, send_sem=send.at[k], recv_sem=recv.at[k],
                                         device_id=peer, device_id_type=MESH).start()
        token[...] = jnp.zeros_like(token)

    ops = list(srcs) + list(lands)
    out = pl.pallas_call(
        body, name=name,
        out_shape=(pltpu.SemaphoreType.DMA((ncopies,)), pltpu.SemaphoreType.DMA((ncopies,)),
                   *[pltpu.HBM(a.shape, a.dtype) for a in ops], jax.ShapeDtypeStruct((8, 128), F32)),
        in_specs=(HBM,) * (ns + nl),
        out_specs=(SEM, SEM) + (HBM,) * (ns + nl) + (pl.BlockSpec(memory_space=pltpu.VMEM),),
        input_output_aliases={i: 2 + i for i in range(ns + nl)},
        compiler_params=pltpu.CompilerParams(has_side_effects=EFFECT),
    )(*[_in_hbm(a) for a in ops])
    return out[0], out[1], list(out[2:2 + ns]), list(out[2 + ns:2 + ns + nl]), out[-1]


def _split_start_many(name, lands, jobs):
    nl, nj = len(lands), len(jobs)

    def body(*refs):
        land = refs[:nl]
        sems = refs[nl:nl + 2 * nj]
        token = refs[-1]
        x, y, c = _place()
        for j, (idx, _, plan) in enumerate(jobs):
            for k, (s_ref, d_ref, peer, _) in enumerate(plan((), [land[t] for t in idx], x, y, c)):
                pltpu.make_async_remote_copy(src_ref=s_ref, dst_ref=d_ref, send_sem=sems[2 * j].at[k],
                                             recv_sem=sems[2 * j + 1].at[k], device_id=peer,
                                             device_id_type=MESH).start()
        token[...] = jnp.zeros_like(token)

    sem_shapes = tuple(pltpu.SemaphoreType.DMA((n,)) for _, n, _ in jobs for _ in range(2))
    out = pl.pallas_call(
        body, name=name,
        out_shape=sem_shapes + tuple(pltpu.HBM(a.shape, a.dtype) for a in lands)
        + (jax.ShapeDtypeStruct((8, 128), F32),),
        in_specs=(HBM,) * nl,
        out_specs=(SEM,) * (2 * nj) + (HBM,) * nl + (pl.BlockSpec(memory_space=pltpu.VMEM),),
        input_output_aliases={i: 2 * nj + i for i in range(nl)},
        compiler_params=pltpu.CompilerParams(has_side_effects=EFFECT),
    )(*[_in_hbm(a) for a in lands])
    return [(out[2 * j], out[2 * j + 1]) for j in range(nj)], list(out[2 * nj:2 * nj + nl]), out[-1]


def _split_wait(name, send, recv, srcs, lands, after, plan):
    ns, nl = len(srcs), len(lands)

    def body(*refs):
        src, land = refs[:ns], refs[ns:ns + nl]
        send_ref, recv_ref = refs[ns + nl], refs[ns + nl + 1]
        x, y, c = _place()
        for k, (s_ref, _, _, got) in enumerate(plan(src, land, x, y, c)):
            cp = pltpu.make_async_remote_copy(src_ref=s_ref, dst_ref=got, send_sem=send_ref.at[k],
                                              recv_sem=recv_ref.at[k], device_id=(x, y, c), device_id_type=MESH)
            cp.wait_send()
            cp.wait_recv()

    ops = list(srcs) + list(lands)
    out = pl.pallas_call(
        body, name=name, out_shape=tuple(pltpu.HBM(a.shape, a.dtype) for a in ops),
        in_specs=(HBM,) * (ns + nl) + (SEM, SEM, ANY), out_specs=(HBM,) * (ns + nl),
        input_output_aliases={i: i for i in range(ns + nl)},
        compiler_params=pltpu.CompilerParams(has_side_effects=EFFECT),
    )(*ops, send, recv, after)
    return list(out[:ns]), list(out[ns:])


def _split_wait_start(name, send, recv, lands, after, wait_plan, ncopies, start_plan, carried=()):
    nl, nc = len(lands), len(carried)
    lands = list(lands) + list(carried)

    def body(*refs):
        land = refs[:nl]
        send_in, recv_in = refs[nl + nc], refs[nl + nc + 1]
        send_out, recv_out = refs[nl + nc + 3], refs[nl + nc + 4]
        x, y, c = _place()
        for k, (s_ref, _, _, got) in enumerate(wait_plan((), land, x, y, c)):
            cp = pltpu.make_async_remote_copy(src_ref=s_ref, dst_ref=got, send_sem=send_in.at[k],
                                              recv_sem=recv_in.at[k], device_id=(x, y, c), device_id_type=MESH)
            cp.wait_send()
            cp.wait_recv()
        for k, (s_ref, d_ref, peer, _) in enumerate(start_plan((), land, x, y, c)):
            pltpu.make_async_remote_copy(src_ref=s_ref, dst_ref=d_ref, send_sem=send_out.at[k],
                                         recv_sem=recv_out.at[k], device_id=peer, device_id_type=MESH).start()

    out = pl.pallas_call(
        body, name=name,
        out_shape=(pltpu.SemaphoreType.DMA((ncopies,)), pltpu.SemaphoreType.DMA((ncopies,)),
                   *[pltpu.HBM(a.shape, a.dtype) for a in lands]),
        in_specs=(HBM,) * (nl + nc) + (SEM, SEM, ANY), out_specs=(SEM, SEM) + (HBM,) * (nl + nc),
        input_output_aliases={i: 2 + i for i in range(nl + nc)},
        compiler_params=pltpu.CompilerParams(has_side_effects=EFFECT),
    )(*lands, send, recv, after)
    return out[0], out[1], list(out[2:2 + nl]), list(out[2 + nl:])


def _half(ref, axis, chip, size, layer, h):
    reg = _region(ref, axis, chip, size).at[pl.ds(layer, 1)]
    rows = reg.shape[1] // 2
    return reg.at[:, pl.ds(pl.multiple_of(h * rows, rows), rows), :]


def _gather_plan(axes, sizes, layer):
    def plan(src, land, x, y, c):
        me = 2 * x + y
        out = []
        for t in range(len(land)):
            mine = _half(land[t], axes[t], me, sizes[t], layer, c)
            for px, py in _other_chips(x, y):
                out.append((mine, mine, (px, py, c), _half(land[t], axes[t], 2 * px + py, sizes[t], layer, c)))
        return out
    return plan


def _forward_plan(axes, sizes, layer):
    def plan(src, land, x, y, c):
        out = []
        for t in range(len(land)):
            for px, py in _other_chips(x, y):
                got = _half(land[t], axes[t], 2 * px + py, sizes[t], layer, c)
                out.append((got, got, (x, y, 1 - c), _half(land[t], axes[t], 2 * px + py, sizes[t], layer, 1 - c)))
        return out
    return plan


def _place_own(name, w, axis, chip):
    nl, r, cs = w.shape
    tr = _rows_tile(r, cs)
    nb = r // tr
    full = (nl, 4 * r, cs) if axis == 1 else (nl, r, 4 * cs)

    def body(m_ref, w_ref, o_ref):
        o_ref[...] = w_ref[...].astype(BF16)

    if axis == 1:
        ospec = pl.BlockSpec((None, tr, cs), lambda l, i, m: (l, m[0] * nb + i, 0))
    else:
        ospec = pl.BlockSpec((None, tr, cs), lambda l, i, m: (l, i, m[0]))
    return pl.pallas_call(
        body, name=name,
        grid_spec=pltpu.PrefetchScalarGridSpec(
            num_scalar_prefetch=1, grid=(nl, nb),
            in_specs=[pl.BlockSpec((None, tr, cs), lambda l, i, m: (l, i, 0))], out_specs=ospec),
        out_shape=jax.ShapeDtypeStruct(full, BF16), compiler_params=_params(("parallel", "parallel")),
    )(chip, w)


def _scatter_plan(axes, sizes):
    def plan(src, land, x, y, c):
        out = []
        for t in range(len(src)):
            for k, (px, py) in enumerate(_other_chips(x, y)):
                out.append((_region(src[t], axes[t], 2 * px + py, sizes[t]).at[0], land[t].at[k], (px, py, c),
                            land[t].at[k]))
        return out
    return plan


def _pair_plan(src, land, x, y, c):
    return [(src[t], land[t], (x, y, 1 - c), land[t]) for t in range(len(src))]


UNIT_STEPS = 16


def _unit_rows(r):
    return min(t for t in range(16, r + 1, 16) if r % t == 0 and r // t <= UNIT_STEPS)


def _guarded(i, nb, steps, work):
    if nb == steps:
        work()
    else:
        pl.when(i < nb)(work)


def _unit_chip_sum(name, gs, slots, axes, chip):
    n = len(gs)
    dims = [s.shape[1:] for s in slots]
    trs = [_unit_rows(r) for r, _ in dims]
    nbs = [r // tr for (r, _), tr in zip(dims, trs)]
    steps = max(nbs)

    def body(m_ref, *refs):
        i = pl.program_id(0)
        for t in range(n):
            def work(t=t):
                acc = refs[t][...].astype(F32)
                for k in range(3):
                    acc = acc + refs[n + t][k].astype(F32)
                refs[2 * n + t][...] = acc.astype(BF16)
            _guarded(i, nbs[t], steps, work)

    gspecs, sspecs, ospecs = [], [], []
    for (r, cs), tr, nb, axis in zip(dims, trs, nbs, axes):
        if axis == 1:
            gspecs.append(pl.BlockSpec((tr, cs), lambda i, m, nb=nb: (m[0] * nb + jnp.minimum(i, nb - 1), 0)))
        else:
            gspecs.append(pl.BlockSpec((tr, cs), lambda i, m, nb=nb: (jnp.minimum(i, nb - 1), m[0])))
        sspecs.append(pl.BlockSpec((3, tr, cs), lambda i, m, nb=nb: (0, jnp.minimum(i, nb - 1), 0)))
        ospecs.append(pl.BlockSpec((tr, cs), lambda i, m, nb=nb: (jnp.minimum(i, nb - 1), 0)))
    return pl.pallas_call(
        body, name=name,
        grid_spec=pltpu.PrefetchScalarGridSpec(num_scalar_prefetch=1, grid=(steps,), in_specs=gspecs + sspecs,
                                               out_specs=ospecs),
        out_shape=[jax.ShapeDtypeStruct(d, BF16) for d in dims], compiler_params=_params(("arbitrary",)),
    )(chip, *gs, *slots)


def _unit_adamw(name, mine, theirs, ws, ms, vs, layer, bufs):
    n = len(mine)
    dims = [a.shape for a in mine]
    trs = [_unit_rows(r) for r, _ in dims]
    nbs = [r // tr for (r, _), tr in zip(dims, trs)]
    steps = max(nbs)
    c1 = 1.0 - B1 ** STEP
    c2 = 1.0 - B2 ** STEP

    def body(*refs):
        i = pl.program_id(0)
        outs = refs[9 * n:]
        for t in range(n):
            def work(t=t):
                a_ref, b_ref, w_ref, m_ref, v_ref = refs[5 * t:5 * t + 5]
                g_ref, d_ref, mo_ref, vo_ref = outs[4 * t:4 * t + 4]
                gv = a_ref[...].astype(F32) + b_ref[...].astype(F32)
                mn = B1 * m_ref[...] + (1.0 - B1) * gv
                vn = B2 * v_ref[...] + (1.0 - B2) * (gv * gv)
                g_ref[...] = gv
                mo_ref[...] = mn
                vo_ref[...] = vn
                d_ref[...] = -LR * ((mn / c1) / (jnp.sqrt(vn / c2) + ADAM_EPS) + WD * w_ref[...])
            _guarded(i, nbs[t], steps, work)

    in_specs, out_specs, ops = [], [], []
    for t, ((r, cs), tr, nb) in enumerate(zip(dims, trs, nbs)):
        blk = pl.BlockSpec((tr, cs), lambda i, nb=nb: (jnp.minimum(i, nb - 1), 0))
        lay = pl.BlockSpec((None, tr, cs), lambda i, nb=nb: (layer, jnp.minimum(i, nb - 1), 0))
        in_specs += [blk, blk, lay, lay, lay]
        out_specs += [lay] * 4
        ops += [mine[t], theirs[t], ws[t], ms[t], vs[t]]
    flat = [b for bs in bufs for b in bs]
    out = pl.pallas_call(
        body, name=name, grid=(steps,), in_specs=in_specs + [ANY] * (4 * n), out_specs=out_specs,
        out_shape=[jax.ShapeDtypeStruct(b.shape, b.dtype) for b in flat],
        input_output_aliases={5 * n + k: k for k in range(4 * n)}, compiler_params=_params(("arbitrary",)),
    )(*ops, *flat)
    return [list(out[4 * t:4 * t + 4]) for t in range(n)]


def _small_pair_sum(p):
    rows = p.shape[0]
    half = rows // 2

    def body(p_ref, o_ref, sib, send, recv):
        x, y, c = _place()
        mine = pl.ds(pl.multiple_of(c * half, half), half)
        theirs = pl.ds(pl.multiple_of((1 - c) * half, half), half)
        pair = pltpu.make_async_remote_copy(src_ref=p_ref.at[theirs], dst_ref=sib, send_sem=send, recv_sem=recv,
                                            device_id=(x, y, 1 - c), device_id_type=MESH)
        pair.start()
        pair.wait()
        o_ref[...] = (p_ref[mine] + sib[...]).astype(BF16)

    vm = pl.BlockSpec(memory_space=pltpu.VMEM)
    return pl.pallas_call(
        body, name="small_pair_sum", in_specs=[vm], out_specs=vm, out_shape=jax.ShapeDtypeStruct((half, 128), BF16),
        scratch_shapes=[pltpu.VMEM((half, 128), F32), pltpu.SemaphoreType.DMA, pltpu.SemaphoreType.DMA],
        compiler_params=pltpu.CompilerParams(vmem_limit_bytes=VMEM_LIMIT),
    )(p)


def _small_plan(src, land, x, y, c):
    return [(src[0], land[0].at[k], (px, py, c), land[0].at[k]) for k, (px, py) in enumerate(_other_chips(x, y))]


def _small_total(own, slots):
    half = own.shape[0]

    def body(own_ref, slots_ref, o_ref, sums, send, recv):
        x, y, c = _place()
        me = 2 * x + y
        mine = pl.ds(pl.multiple_of(c * half, half), half)
        theirs = pl.ds(pl.multiple_of((1 - c) * half, half), half)
        sums[me] = own_ref[...]
        for k, (px, py) in enumerate(_other_chips(x, y)):
            sums[2 * px + py] = slots_ref[k]
        acc = sums[0].astype(F32)
        for k in range(1, 4):
            acc = acc + sums[k].astype(F32)
        o_ref[mine] = acc
        back = pltpu.make_async_remote_copy(src_ref=o_ref.at[mine], dst_ref=o_ref.at[mine], send_sem=send,
                                            recv_sem=recv, device_id=(x, y, 1 - c), device_id_type=MESH)
        back.start()
        pltpu.make_async_remote_copy(src_ref=o_ref.at[theirs], dst_ref=o_ref.at[theirs], send_sem=send,
                                     recv_sem=recv, device_id=(x, y, c), device_id_type=MESH).wait_recv()
        back.wait_send()

    vm = pl.BlockSpec(memory_space=pltpu.VMEM)
    return pl.pallas_call(
        body, name="small_total", in_specs=[vm, vm], out_specs=vm,
        out_shape=jax.ShapeDtypeStruct((2 * half, 128), F32),
        scratch_shapes=[pltpu.VMEM((4, half, 128), BF16), pltpu.SemaphoreType.DMA, pltpu.SemaphoreType.DMA],
        compiler_params=pltpu.CompilerParams(vmem_limit_bytes=VMEM_LIMIT),
    )(own, slots)


def _small_adamw(gs, ws, ms, vs):
    n = len(gs)
    c1 = 1.0 - B1 ** STEP
    c2 = 1.0 - B2 ** STEP

    def body(*refs):
        for t in range(n):
            gv, wv = refs[t][...], refs[n + t][...]
            d_ref, mo_ref, vo_ref = refs[4 * n + 3 * t:4 * n + 3 * t + 3]
            mn = B1 * refs[2 * n + t][...] + (1.0 - B1) * gv
            vn = B2 * refs[3 * n + t][...] + (1.0 - B2) * (gv * gv)
            mo_ref[...] = mn
            vo_ref[...] = vn
            d_ref[...] = -LR * ((mn / c1) / (jnp.sqrt(vn / c2) + ADAM_EPS) + WD * wv)

    vm = pl.BlockSpec(memory_space=pltpu.VMEM)
    out = pl.pallas_call(
        body, name="adamw_small", in_specs=[vm] * (4 * n), out_specs=[vm] * (3 * n),
        out_shape=[jax.ShapeDtypeStruct(w.shape, F32) for w in ws for _ in range(3)],
        compiler_params=pltpu.CompilerParams(vmem_limit_bytes=VMEM_LIMIT),
    )(*gs, *ws, *ms, *vs)
    return [tuple(out[3 * t:3 * t + 3]) for t in range(n)]


def _pack(parts):
    flat = []
    for p in parts:
        v = p.reshape(-1).astype(F32)
        flat.append(jnp.pad(v, (0, (-v.shape[0]) % 128)))
    v = jnp.concatenate(flat)
    v = jnp.pad(v, (0, (-v.shape[0]) % (512 * 128)))
    return v.reshape(-1, 128)


def _unpack(buf, like):
    out, r0 = [], 0
    for p in like:
        nelem = 1
        for s in p.shape:
            nelem *= s
        rows = -(-nelem // 128)
        blk = buf[r0:r0 + rows]
        if nelem % 128:
            blk = blk.reshape(-1)[:nelem]
        out.append(blk.reshape(p.shape))
        r0 += rows
    return out


def kernel(x, mem, positions, mem_norm_g, mix_pre_g, mix_post_g, w_in, gm_v_g, gm_w_s, gm_b_s, pool_w, pool_scale, attn_sinks, w_o, x_pre_g, x_post_g, w_xq, w_xkv, w_xo, ffn_pre_g, ffn_post_g, w_gate_up, w_down, loss_target, m_mem_norm_g, m_mix_pre_g, m_mix_post_g, m_w_in, m_gm_v_g, m_gm_w_s, m_gm_b_s, m_pool_w, m_pool_scale, m_attn_sinks, m_w_o, m_x_pre_g, m_x_post_g, m_w_xq, m_w_xkv, m_w_xo, m_ffn_pre_g, m_ffn_post_g, m_w_gate_up, m_w_down, v_mem_norm_g, v_mix_pre_g, v_mix_post_g, v_w_in, v_gm_v_g, v_gm_w_s, v_gm_b_s, v_pool_w, v_pool_scale, v_attn_sinks, v_w_o, v_x_pre_g, v_x_post_g, v_w_xq, v_w_xkv, v_w_xo, v_ffn_pre_g, v_ffn_post_g, v_w_gate_up, v_w_down):
    args = (x, mem, positions, mem_norm_g, mix_pre_g, mix_post_g, w_in, gm_v_g, gm_w_s, gm_b_s, pool_w, pool_scale, attn_sinks, w_o, x_pre_g, x_post_g, w_xq, w_xkv, w_xo, ffn_pre_g, ffn_post_g, w_gate_up, w_down)
    moms_m = (m_mem_norm_g, m_mix_pre_g, m_mix_post_g, m_w_in, m_gm_v_g, m_gm_w_s, m_gm_b_s, m_pool_w, m_pool_scale, m_attn_sinks, m_w_o, m_x_pre_g, m_x_post_g, m_w_xq, m_w_xkv, m_w_xo, m_ffn_pre_g, m_ffn_post_g, m_w_gate_up, m_w_down)
    moms_v = (v_mem_norm_g, v_mix_pre_g, v_mix_post_g, v_w_in, v_gm_v_g, v_gm_w_s, v_gm_b_s, v_pool_w, v_pool_scale, v_attn_sinks, v_w_o, v_x_pre_g, v_x_post_g, v_w_xq, v_w_xkv, v_w_xo, v_ffn_pre_g, v_ffn_post_g, v_w_gate_up, v_w_down)
    P = dict(zip(NAMES, args))
    P['loss_target'] = loss_target
    M = dict(zip(WEIGHTS, moms_m))
    V = dict(zip(WEIGHTS, moms_v))
    depth = w_in.shape[0]
    nbig = len(BIG)
    axes = [BIG_AXIS[n] for n in BIG]
    sizes = [P[n].shape[a] for n, a in zip(BIG, axes)]
    chip = (2 * lax.axis_index("x") + lax.axis_index("y")).astype(jnp.int32).reshape(1)

    groups = [['w_in'], ['w_o', 'w_xq', 'w_xkv', 'w_xo'], ['w_gate_up', 'w_down']]
    units = [(l, g) for l in range(depth) for g in groups]
    unit_of = {(l, n): i for i, (l, names) in enumerate(units) for n in names}
    ax = lambda names: [BIG_AXIS[n] for n in names]
    sz = lambda names: [P[n].shape[BIG_AXIS[n]] for n in names]

    full = {n: _place_own("place_" + n, P[n], BIG_AXIS[n], chip) for n in BIG}
    gathers, land, tok = _split_start_many(
        "gather_start", [full[n] for n in BIG],
        [([BIG.index(n) for n in names], 3 * len(names), _gather_plan(ax(names), sz(names), l)) for l, names in units])
    full.update(zip(BIG, land))
    P['first_dep'] = tok[:1, :1]
    forwards, gathered = {}, set()

    def forward_unit(i, after, carried=()):
        ul, unames = units[i]
        send, recv = gathers[i]
        send, recv, land, thru = _split_wait_start(
            "gather_pass%d" % i, send, recv, [full[n] for n in unames], after,
            _gather_plan(ax(unames), sz(unames), ul), 3 * len(unames), _forward_plan(ax(unames), sz(unames), ul),
            carried=[full[n] for n in carried])
        full.update(zip(unames, land))
        full.update(zip(carried, thru))
        forwards[i] = (send, recv)

    def weights_of(l, names, after):
        i = unit_of[(l, names[0])]
        if i not in gathered:
            _, unames = units[i]
            if i not in forwards:
                forward_unit(i, after)
            send, recv = forwards.pop(i)
            _, land = _split_wait("gather_wait%d" % i, send, recv, [], [full[n] for n in unames], after,
                                  _forward_plan(ax(unames), sz(unames), l))
            full.update(zip(unames, land))
            gathered.add(i)
            if len(groups) <= i + 1 < len(units):
                forward_unit(i + 1, after, carried=[n for n in unames if n not in units[i + 1][1]])
        return {n: (full[n], l) for n in names}

    outs = {n: [lax.empty(P[n].shape, F32) for _ in range(4)] for n in BIG}
    gunits = [(l, BIG) for l in range(depth - 1, 0, -1)] + [
        (0, g) for g in (['w_gate_up', 'w_down'], ['w_xq', 'w_xkv', 'w_xo'], ['w_o'], ['w_in'])]
    collected, scatters, pairs = {}, {}, {}

    def finish_scatter(i, after):
        _, names = gunits[i]
        send, recv, g_l, slots = scatters.pop(i)
        g_l, slots = _split_wait("scatter_wait%d" % i, send, recv, g_l, slots, after,
                                 _scatter_plan(ax(names), sz(names)))
        mine = _unit_chip_sum("chip_sum", [g.reshape(g.shape[1:]) for g in g_l], slots, ax(names), chip)
        send, recv, mine, theirs, tok = _split_start("pair_start%d" % i, mine, [lax.empty(a.shape, BF16) for a in mine],
                                                     len(names), _pair_plan)
        pairs[i] = (send, recv, mine, theirs)
        return tok[:1, :1]

    def finish_pair(i, after):
        l, names = gunits[i]
        send, recv, mine, theirs = pairs.pop(i)
        mine, theirs = _split_wait("pair_wait%d" % i, send, recv, mine, theirs, after, _pair_plan)
        new = _unit_adamw("adamw", mine, theirs, [P[n] for n in names], [M[n] for n in names], [V[n] for n in names],
                          l, [outs[n] for n in names])
        outs.update(zip(names, new))

    calls = {'n': 0}
    lag = 4

    def grads_of(l, g_part, after):
        collected.update({(l, n): g for n, g in g_part.items()})
        calls['n'] += 1
        now = calls['n']
        tok = jnp.zeros((1, 1), F32)
        for i, (ul, names) in enumerate(gunits):
            if ul != l or ('started', i) in collected or any((l, n) not in collected for n in names):
                continue
            collected[('started', i)] = now
            srcs = [collected[(l, n)].reshape((1,) + collected[(l, n)].shape) for n in names]
            send, recv, srcs, slots, t = _split_start("scatter_start%d" % i, srcs,
                                                      [lax.empty((3,) + P[n].shape[1:], BF16) for n in names],
                                                      3 * len(names), _scatter_plan(ax(names), sz(names)))
            scatters[i] = (send, recv, srcs, slots)
            tok = tok + t[:1, :1]
        for i in sorted(pairs):
            if collected[('summed', i)] + lag <= now:
                finish_pair(i, after)
        for i in sorted(scatters):
            if collected[('started', i)] + lag <= now:
                tok = tok + finish_scatter(i, after)
                collected[('summed', i)] = now
        return tok

    loss_part, dx, small_g = _fwd_bwd(P, weights_of, grads_of)
    loss = lax.psum(loss_part[0, 0], ("x", "y", "c"))
    grad_x = dx.reshape(x.shape)

    small_like = [P[n] for n in SMALL]
    half_sum = _small_pair_sum(_pack(small_g))
    send, recv, (half_sum,), (slots,), _ = _split_start(
        "small_start", [half_sum], [lax.empty((3,) + half_sum.shape, BF16)], 3, _small_plan)

    for i in sorted(pairs):
        finish_pair(i, half_sum)
    for i in sorted(scatters):
        finish_scatter(i, half_sum)
    for i in sorted(pairs):
        finish_pair(i, half_sum)
    grads, deltas, new_m, new_v = {}, {}, {}, {}
    for n in BIG:
        grads[n], deltas[n], new_m[n], new_v[n] = outs[n]

    (half_sum,), (slots,) = _split_wait("small_wait", send, recv, [half_sum], [slots], grads[BIG[0]], _small_plan)
    two_d = lambda a: a.reshape(-1, a.shape[-1])
    gs = _unpack(_small_total(half_sum, slots), small_like)
    upd = _small_adamw([two_d(g) for g in gs], [two_d(P[n]) for n in SMALL], [two_d(M[n]) for n in SMALL],
                       [two_d(V[n]) for n in SMALL])
    for n, g, (dlt, mn, vn) in zip(SMALL, gs, upd):
        shape = P[n].shape
        grads[n], deltas[n], new_m[n], new_v[n] = g, dlt.reshape(shape), mn.reshape(shape), vn.reshape(shape)

    return (loss, grad_x, *[grads[n] for n in WEIGHTS], *[deltas[n] for n in WEIGHTS],
            *[new_m[n] for n in WEIGHTS], *[new_v[n] for n in WEIGHTS])


def _fwd_bwd(P, weights_of, grads_of):
    (x, mem, positions, mem_norm_g, mix_pre_g, mix_post_g, w_in, gm_v_g, gm_w_s, gm_b_s, pool_w, pool_scale, attn_sinks,
     w_o, x_pre_g, x_post_g, w_xq, w_xkv, w_xo, ffn_pre_g, ffn_post_g, w_gate_up, w_down) = [P[n] for n in NAMES]
    x0 = x[0]
    s, d = x0.shape
    depth = w_in.shape[0]
    tgt = P['loss_target'][0]
    tmn = 256
    tmr = min(512, s)
    tmp = min(1024, s)
    tkw = min(2048, s)

    half = HEAD // 2
    inv = ROPE_THETA ** (-jnp.arange(half, dtype=F32) / half)
    ang = positions[0].astype(F32)[:, None] * inv
    cos, sin = jnp.cos(ang), jnp.sin(ang)
    cosq = jnp.tile(jnp.concatenate([cos, cos], axis=-1), (1, 2))
    sinq = jnp.tile(jnp.concatenate([-sin, sin], axis=-1), (1, 2))

    row = lambda a, l: a[l].reshape(1, -1)
    memn = _prenorm("mem_norm", mem[0], mem_norm_g.reshape(1, d), tmn)
    pw_bd = []
    for l in range(depth):
        bd = jnp.zeros((256, 256), F32)
        for g in range(4):
            bd = lax.dynamic_update_slice(bd, pool_w[l, g], (64 * g, 64 * g))
        pw_bd.append(bd)

    saved = []
    xc = x0
    h = _prenorm("pre_norm0", x0, row(mix_pre_g, 0) + P['first_dep'], tmn)
    for l in range(depth):
        W = weights_of(l, ['w_in'], xc)
        sv = {'x0': xc, 'h1': h}
        z, = _mm_rows("fwd_w_in", h, *W['w_in'], 'nn', tm=tmp, rows_out=[F32], epilogue=_plain_rows)
        abc = _mixer_fwd("mixer_fwd", z, cosq, sinq, row(gm_v_g, l), gm_w_s[l], gm_b_s[l].T, pw_bd[l],
                         row(pool_scale, l), row(attn_sinks, l))
        W.update(weights_of(l, ['w_o'], z))
        mix, xc, h = _mm_rows("fwd_w_o", abc, *W['w_o'], 'nn', tm=tmr, rows_in=[xc],
                              params=[row(mix_post_g, l), row(x_pre_g, l)], rows_out=[BF16, F32, BF16],
                              epilogue=_post_pre_rows)
        sv.update(z=z, abc=abc, mix=mix, x1=xc, h2=h)
        W.update(weights_of(l, ['w_xq', 'w_xkv', 'w_xo'], xc))
        q, = _mm_rows("fwd_w_xq", h, *W['w_xq'], 'nn', tm=tmp, rows_out=[BF16], epilogue=_plain_rows)
        kv = _mm_nn("fwd_w_xkv", memn, *W['w_xkv'], tm=256, tn=512, tk=d, out_dtype=BF16)
        o = _xattn_fwd("xattn_fwd", q, kv, 512)
        xo, xc, h = _mm_rows("fwd_w_xo", o, *W['w_xo'], 'nn', tm=tmr, rows_in=[xc],
                             params=[row(x_post_g, l), row(ffn_pre_g, l)], rows_out=[BF16, F32, BF16],
                             epilogue=_post_pre_rows)
        sv.update(q=q, kv=kv, o=o, xo=xo, x2=xc, h3=h)
        W.update(weights_of(l, ['w_gate_up', 'w_down'], xc))
        dff = W['w_down'][0].shape[1]
        gate, up, act = _ffn_up("ffn_up", h, *W['w_gate_up'], 512, dff // 2)
        sv.update(gate=gate, up=up, act=act)
        if l + 1 < depth:
            f, xc, h = _mm_rows("fwd_w_down", act, *W['w_down'], 'nn', tm=tmr, rows_in=[xc],
                                params=[row(ffn_post_g, l), row(mix_pre_g, l + 1)], rows_out=[BF16, F32, BF16],
                                epilogue=_post_pre_rows)
            sv.update(f=f)
        saved.append(sv)
    gs = {n: [None] * depth for n in SMALL if n != 'mem_norm_g'}
    dx, dfn, gs['ffn_post_g'][depth - 1], loss_part = _mm_rows(
        "fwd_w_down_loss", saved[-1]['act'], *W['w_down'], 'nn', tm=tmr, rows_in=[xc, tgt],
        params=[row(ffn_post_g, depth - 1)], rows_out=[F32, BF16], n_sums=2, epilogue=_make_loss_rows(d))

    dmemn = None
    tok = jnp.zeros((1, 1), F32)
    for l in reversed(range(depth)):
        sv, W, G = saved[l], weights_of(l, BIG, dx), {}
        G['w_down'] = _mm_tn("dw_down", sv['act'], dfn, tm=dff // 2, tn=d, tk=tkw)
        dgu = _ffn_act_bwd("ffn_act_bwd", dfn, *W['w_down'], sv['gate'], sv['up'], 256)
        G['w_gate_up'] = _mm_tn("dw_gate_up", sv['h3'], dgu, tm=d, tn=dff // 2, tk=tkw)
        dx, dxo, gs['ffn_pre_g'][l], gs['x_post_g'][l] = _mm_rows(
            "bwd_w_gate_up", dgu, *W['w_gate_up'], 'nt', tm=tmr, rows_in=[sv['x2'], dx, sv['xo']],
            params=[row(ffn_pre_g, l) + tok, row(x_post_g, l)], rows_out=[F32, BF16], n_sums=2, epilogue=_bwd_rows)
        tok = grads_of(l, {n: G[n] for n in ('w_gate_up', 'w_down')}, dx)
        G['w_xo'] = _mm_tn("dw_xo", sv['o'], dxo, tm=d, tn=d, tk=tkw)
        do, = _mm_rows("bwd_w_xo", dxo, *W['w_xo'], 'nt', tm=tmp, params=[jnp.zeros((1, d), F32) + tok],
                       rows_out=[BF16], epilogue=_plain_rows)
        dq, dkv = _xattn_bwd("xattn_bwd", sv['q'], sv['kv'], do, 512)
        dkv = dkv.astype(BF16)
        G['w_xkv'] = _mm_tn("dw_xkv", memn, dkv, tm=d, tn=d, tk=mem.shape[1])
        dmemn = _mm_nt("bwd_w_xkv", dkv, *W['w_xkv'], tm=mem.shape[1], tn=512, tk=2 * d, out_dtype=F32, add=dmemn)
        G['w_xq'] = _mm_tn("dw_xq", sv['h2'], dq, tm=d, tn=d, tk=tkw)
        dx, dmix, gs['x_pre_g'][l], gs['mix_post_g'][l] = _mm_rows(
            "bwd_w_xq", dq, *W['w_xq'], 'nt', tm=tmr, rows_in=[sv['x1'], dx, sv['mix']],
            params=[row(x_pre_g, l), row(mix_post_g, l)], rows_out=[F32, BF16], n_sums=2, epilogue=_bwd_rows)
        tok = grads_of(l, {n: G[n] for n in ('w_xq', 'w_xkv', 'w_xo')}, dx)
        G['w_o'] = _mm_tn("dw_o", sv['abc'], dmix, tm=d, tn=d, tk=tkw)
        dabc, = _mm_rows("bwd_w_o", dmix, *W['w_o'], 'nt', tm=tmp, params=[jnp.zeros((1, d), F32) + tok],
                         rows_out=[F32], epilogue=_plain_rows)
        tok = grads_of(l, {'w_o': G['w_o']}, dabc)
        dz, dgv, dws, dbt, dpw, dpsc, dsnk = _mixer_bwd(
            "mixer_bwd", sv['z'], dabc, cosq, sinq, row(gm_v_g, l) + tok, gm_w_s[l], gm_b_s[l].T, pw_bd[l],
            row(pool_scale, l), row(attn_sinks, l))
        gs['gm_v_g'][l] = dgv
        gs['gm_w_s'][l] = dws
        gs['gm_b_s'][l] = dbt.T
        gs['pool_w'][l] = jnp.stack([dpw[64 * g:64 * (g + 1), 64 * g:64 * (g + 1)] for g in range(4)])
        gs['pool_scale'][l] = dpsc
        gs['attn_sinks'][l] = dsnk
        G['w_in'] = _mm_tn("dw_in", sv['h1'], dz, tm=d, tn=dz.shape[1], tk=tkw)
        if l > 0:
            dx, dfn, gs['mix_pre_g'][l], gs['ffn_post_g'][l - 1] = _mm_rows(
                "bwd_w_in", dz, *W['w_in'], 'nt', tm=tmr, rows_in=[sv['x0'], dx, saved[l - 1]['f']],
                params=[row(mix_pre_g, l), row(ffn_post_g, l - 1)], rows_out=[F32, BF16], n_sums=2,
                epilogue=_bwd_rows)
        else:
            dx, gs['mix_pre_g'][l] = _mm_rows(
                "bwd_w_in_first", dz, *W['w_in'], 'nt', tm=tmr, rows_in=[sv['x0'], dx],
                params=[row(mix_pre_g, l)], rows_out=[F32], n_sums=1, epilogue=_bwd_rows_first)
        tok = grads_of(l, {'w_in': G['w_in']}, dx)
    _, dg_mem = _norm_bwd("bwd_mem_norm", mem[0], mem_norm_g.reshape(1, d) + tok, dmemn, None, BF16, tmn)
    small_g = []
    for n in SMALL:
        if n == 'mem_norm_g':
            small_g.append(dg_mem.reshape(P[n].shape))
        else:
            small_g.append(jnp.stack([a.reshape(P[n].shape[1:]) for a in gs[n]]))
    return loss_part, dx, small_g
```

```python
import functools

import jax
import jax.numpy as jnp
from jax import lax
from jax.experimental import pallas as pl
from jax.experimental.pallas import tpu as pltpu

F32 = jnp.float32
BF16 = jnp.bfloat16
EPS = 1e-6
CHUNK = 128
HEAD = 64
ROPE_THETA = 10000.0
POOL_WINDOWS = (2, 4, 8, 16)
LR, B1, B2, ADAM_EPS, WD, STEP = 0.001, 0.9, 0.999, 1e-08, 0.01, 10
MESH = pl.DeviceIdType.MESH
VMEM_LIMIT = 56 * 1024 * 1024

NAMES = ['x', 'mem', 'positions', 'mem_norm_g', 'mix_pre_g', 'mix_post_g', 'w_in', 'gm_v_g', 'gm_w_s', 'gm_b_s',
         'pool_w', 'pool_scale', 'attn_sinks', 'w_o', 'x_pre_g', 'x_post_g', 'w_xq', 'w_xkv', 'w_xo', 'ffn_pre_g',
         'ffn_post_g', 'w_gate_up', 'w_down']
WEIGHTS = NAMES[3:]
BIG = ['w_in', 'w_o', 'w_xq', 'w_xkv', 'w_xo', 'w_gate_up', 'w_down']
BIG_AXIS = {'w_in': 2, 'w_o': 1, 'w_xq': 1, 'w_xkv': 2, 'w_xo': 1, 'w_gate_up': 2, 'w_down': 1}
SMALL = [n for n in WEIGHTS if n not in BIG]

NN = (((1,), (0,)), ((), ()))
NT = (((1,), (1,)), ((), ()))
TN = (((0,), (0,)), ((), ()))


def _dot(a, b, dims=NN):
    return lax.dot_general(a, b, dims, preferred_element_type=F32)


def _params(sem):
    return pltpu.CompilerParams(dimension_semantics=sem, vmem_limit_bytes=VMEM_LIMIT)


STREAM_BLOCK_BYTES = 3 * 512 * 1024


def _rows_tile(rows, cols):
    limit = max(16, STREAM_BLOCK_BYTES // (4 * cols))
    return max(t for t in range(16, min(rows, limit) + 1, 16) if rows % t == 0)


def _mm(name, a, a_spec, b, b_spec, dims, grid, nk, out_shape, out_spec, add=None, add_spec=None):
    acc_shape = out_spec.block_shape

    def body(*refs):
        a_ref, b_ref = refs[0], refs[1]
        pos = 2
        add_ref = None
        if add is not None:
            add_ref = refs[pos]
            pos += 1
        o_ref = refs[pos]
        part = _dot(a_ref[...].astype(BF16), b_ref[...].astype(BF16), dims)
        if nk == 1:
            if add_ref is not None:
                part = part + add_ref[...]
            o_ref[...] = part.astype(o_ref.dtype)
        else:
            acc_ref = refs[pos + 1]
            k = pl.program_id(2)

            @pl.when(k == 0)
            def _():
                acc_ref[...] = part if add_ref is None else part + add_ref[...]

            @pl.when(k > 0)
            def _():
                acc_ref[...] += part

            @pl.when(k == nk - 1)
            def _():
                o_ref[...] = acc_ref[...].astype(o_ref.dtype)

    ops, specs = [a, b], [a_spec, b_spec]
    if add is not None:
        ops.append(add)
        specs.append(add_spec)
    return pl.pallas_call(
        body, name=name, grid=grid, in_specs=specs, out_specs=out_spec, out_shape=out_shape,
        scratch_shapes=[pltpu.VMEM(acc_shape, F32)] if nk > 1 else [],
        compiler_params=_params(("parallel", "parallel", "arbitrary")),
    )(*ops)


def _wspec(block, layer, fn):
    return pl.BlockSpec((None,) + block, lambda i, j, k: (layer,) + fn(i, j, k))


def _mm_nn(name, a, w, layer, *, tm, tn, tk, out_dtype):
    m, kk = a.shape
    n = w.shape[2]
    tm = min(tm, m)
    nk = kk // tk
    return _mm(name, a, pl.BlockSpec((tm, tk), lambda i, j, k: (i, k)),
               w, _wspec((tk, tn), layer, lambda i, j, k: (k, j)), NN,
               (m // tm, n // tn, nk), nk, jax.ShapeDtypeStruct((m, n), out_dtype),
               pl.BlockSpec((tm, tn), lambda i, j, k: (i, j)))


def _mm_nt(name, a, w, layer, *, tm, tn, tk, out_dtype, add=None):
    m, kk = a.shape
    n = w.shape[1]
    tm = min(tm, m)
    nk = kk // tk
    ospec = pl.BlockSpec((tm, tn), lambda i, j, k: (i, j))
    return _mm(name, a, pl.BlockSpec((tm, tk), lambda i, j, k: (i, k)),
               w, _wspec((tn, tk), layer, lambda i, j, k: (j, k)), NT,
               (m // tm, n // tn, nk), nk, jax.ShapeDtypeStruct((m, n), out_dtype), ospec,
               add=add, add_spec=ospec if add is not None else None)


def _mm_tn(name, a, b, *, tm, tn, tk):
    kk, m = a.shape
    n = b.shape[1]
    tk = min(tk, kk)
    nk = kk // tk
    return _mm(name, a, pl.BlockSpec((tk, tm), lambda i, j, k: (k, i)),
               b, pl.BlockSpec((tk, tn), lambda i, j, k: (k, j)), TN,
               (m // tm, n // tn, nk), nk, jax.ShapeDtypeStruct((m, n), BF16),
               pl.BlockSpec((tm, tn), lambda i, j, k: (i, j)))


def _mm_rows(name, a, w, layer, mode, *, tm, rows_in=(), params=(), rows_out=(), n_sums=0, epilogue):
    m, kk = a.shape
    n = w.shape[2] if mode == 'nn' else w.shape[1]
    nr, npar, no = len(rows_in), len(params), len(rows_out)

    def body(*refs):
        a_ref, w_ref = refs[0], refs[1]
        rin = refs[2:2 + nr]
        par = refs[2 + nr:2 + nr + npar]
        outs = refs[2 + nr + npar:2 + nr + npar + no]
        sums = refs[2 + nr + npar + no:2 + nr + npar + no + n_sums]
        acc = _dot(a_ref[...], w_ref[...], NN if mode == 'nn' else NT)
        res, sm = epilogue(acc, [r[...] for r in rin], [p[...] for p in par])
        for r, v in zip(outs, res):
            r[...] = v.astype(r.dtype)

        @pl.when(pl.program_id(0) == 0)
        def _():
            for r in sums:
                r[...] = jnp.zeros_like(r)

        for r, v in zip(sums, sm):
            r[...] += v

    wblock = (None, kk, n) if mode == 'nn' else (None, n, kk)
    rowblk = pl.BlockSpec((tm, n), lambda i: (i, 0))
    one = pl.BlockSpec((1, n), lambda i: (0, 0))
    return pl.pallas_call(
        body, name=name, grid=(m // tm,),
        in_specs=[pl.BlockSpec((tm, kk), lambda i: (i, 0)),
                  pl.BlockSpec(wblock, lambda i: (layer, 0, 0), pipeline_mode=pl.Buffered(1))]
                 + [rowblk] * nr + [one] * npar,
        out_specs=[rowblk] * no + [one] * n_sums,
        out_shape=[jax.ShapeDtypeStruct((m, n), dt) for dt in rows_out] +
                  [jax.ShapeDtypeStruct((1, n), F32)] * n_sums,
        compiler_params=_params(("arbitrary",)),
    )(a, w, *rows_in, *params)


def _rstd(x):
    return lax.rsqrt(jnp.mean(x * x, axis=-1, keepdims=True) + EPS)


def _norm_back(xin, g, dy):
    r = _rstd(xin)
    xh = xin * r
    dyg = dy * g
    return r * (dyg - xh * jnp.mean(dyg * xh, axis=-1, keepdims=True)), jnp.sum(dy * xh, axis=0, keepdims=True)


def _plain_rows(acc, rows, pars):
    return [acc], []


def _post_pre_rows(y, rows, pars):
    xn = rows[0] + y * _rstd(y) * pars[0]
    return [y, xn, xn * _rstd(xn) * pars[1]], []


def _make_loss_rows(d):
    def fn(y, rows, pars):
        x, tgt = rows
        err = x + y * _rstd(y) * pars[0] - tgt
        dout = err * (1.0 / d)
        dy, dg = _norm_back(y, pars[0], dout)
        lsum = 0.5 * jnp.sum(jnp.mean(err * err, axis=-1, keepdims=True), axis=0, keepdims=True)
        return [dout, dy], [dg, jnp.broadcast_to(lsum, dg.shape)]
    return fn


def _bwd_rows(dh, rows, pars):
    xin, resid, yprev = rows
    dxa, dg_pre = _norm_back(xin, pars[0], dh)
    dx = resid + dxa
    dyp, dg_post = _norm_back(yprev.astype(F32), pars[1], dx)
    return [dx, dyp], [dg_pre, dg_post]


def _bwd_rows_first(dh, rows, pars):
    xin, resid = rows
    dxa, dg_pre = _norm_back(xin, pars[0], dh)
    return [resid + dxa], [dg_pre]


def _row(d):
    return pl.BlockSpec((1, d), lambda i: (0, 0))


def _prenorm(name, x, g, tm):
    m, d = x.shape

    def body(x_ref, g_ref, o_ref):
        xv = x_ref[...]
        o_ref[...] = (xv * _rstd(xv) * g_ref[...]).astype(BF16)

    blk = pl.BlockSpec((tm, d), lambda i: (i, 0))
    return pl.pallas_call(body, name=name, grid=(m // tm,), in_specs=[blk, _row(d)], out_specs=blk,
                          out_shape=jax.ShapeDtypeStruct((m, d), BF16), compiler_params=_params(("parallel",)))(x, g)


def _norm_bwd(name, xin, g, dy, resid, out_dtype, tm):
    m, d = xin.shape

    def body(*refs):
        if resid is None:
            x_ref, g_ref, dy_ref, dx_ref, dg_ref = refs
        else:
            x_ref, g_ref, dy_ref, r_ref, dx_ref, dg_ref = refs
        xv = x_ref[...]
        r = _rstd(xv)
        xh = xv * r
        dyv = dy_ref[...].astype(F32)
        dyg = dyv * g_ref[...]
        dx = r * (dyg - xh * jnp.mean(dyg * xh, axis=-1, keepdims=True))
        if resid is not None:
            dx = dx + r_ref[...]
        dx_ref[...] = dx.astype(dx_ref.dtype)

        @pl.when(pl.program_id(0) == 0)
        def _():
            dg_ref[...] = jnp.zeros_like(dg_ref)

        dg_ref[...] += jnp.sum(dyv * xh, axis=0, keepdims=True)

    blk = pl.BlockSpec((tm, d), lambda i: (i, 0))
    ops = [xin, g, dy] + ([] if resid is None else [resid])
    specs = [blk, _row(d), blk] + ([] if resid is None else [blk])
    return pl.pallas_call(
        body, name=name, grid=(m // tm,), in_specs=specs, out_specs=[blk, _row(d)],
        out_shape=[jax.ShapeDtypeStruct((m, d), out_dtype), jax.ShapeDtypeStruct((1, d), F32)],
        compiler_params=_params(("arbitrary",)))(*ops)


def _gelu_parts(x):
    c = 0.7978845608028654
    t = jnp.tanh(c * (x + 0.044715 * (x * x * x)))
    return 0.5 * x * (1.0 + t), t


def _gelu_grad(x, t):
    c = 0.7978845608028654
    return 0.5 * (1.0 + t) + 0.5 * x * (1.0 - t * t) * (c * (1.0 + 3.0 * 0.044715 * x * x))


def _rot_half(x):
    ax = x.ndim - 1
    w = x.shape[ax]
    lane = lax.broadcasted_iota(jnp.int32, x.shape, ax)
    return jnp.where((lane & 63) < 32, pltpu.roll(x, w - 32, ax), pltpu.roll(x, 32, ax))


def _group_mean(x, ones_bd):
    hi = x.astype(BF16)
    lo = (x - hi.astype(F32)).astype(BF16)
    return (_dot(hi, ones_bd) + _dot(lo, ones_bd)) * (1.0 / HEAD)


def _gating(gel, gv, ws_ref, bt, ones_bd, mix_s):
    u = gel[:, :256]
    v = gel[:, 256:]
    r = lax.rsqrt(_group_mean(v * v, ones_bd) + EPS)
    xh = v * r
    vn = (xh * gv).astype(BF16)
    row = lax.broadcasted_iota(jnp.int32, (CHUNK, CHUNK), 0)
    col = lax.broadcasted_iota(jnp.int32, (CHUNK, CHUNK), 1)
    causal = col <= row
    wcs = []
    for g in range(4):
        wc = jnp.where(causal, ws_ref[g], 0.0).astype(BF16)
        wcs.append(wc)
        mix_s[:, HEAD * g:HEAD * (g + 1)] = _dot(wc, vn[:, HEAD * g:HEAD * (g + 1)]) + bt[:, g:g + 1]
    return u, r, xh, vn, wcs, causal, mix_s[...]


def _lane_select(lane, vals):
    return jnp.where(lane < 64, vals[0], jnp.where(lane < 128, vals[1], jnp.where(lane < 192, vals[2], vals[3])))


def _pool_fwd(pc, pp, ci):
    ext = jnp.concatenate([pp, pc], axis=0)
    s2 = ext + pltpu.roll(ext, 1, 0)
    s4 = s2 + pltpu.roll(s2, 2, 0)
    s8 = s4 + pltpu.roll(s4, 4, 0)
    s16 = s8 + pltpu.roll(s8, 8, 0)
    t1 = ci * CHUNK + lax.broadcasted_iota(jnp.int32, (CHUNK, 1), 0) + 1
    lane = lax.broadcasted_iota(jnp.int32, (1, 256), 1)
    cnt = _lane_select(lane, [jnp.minimum(t1, w).astype(F32) for w in POOL_WINDOWS])
    ssel = _lane_select(lane, [s[CHUNK:] for s in (s2, s4, s8, s16)])
    return ssel / cnt - pc, cnt, lane


def _attn_prep(zc, zpkv, cc, sc, cp, sp, ci):
    q = zc[:, 768:1280]
    kc = zc[:, 1280:1408]
    vc = zc[:, 1408:1536]
    kp = zpkv[:, :128]
    vp = zpkv[:, 128:]
    qr = q * jnp.concatenate([cc] * 4, axis=1) + _rot_half(q) * jnp.concatenate([sc] * 4, axis=1)
    krc = kc * cc + _rot_half(kc) * sc
    krp = kp * cp + _rot_half(kp) * sp
    kband = jnp.concatenate([krp, krc], axis=0)
    vband = jnp.concatenate([vp, vc], axis=0)
    key = lax.broadcasted_iota(jnp.int32, (2 * CHUNK, 4 * CHUNK), 0)
    t = lax.broadcasted_iota(jnp.int32, (2 * CHUNK, 4 * CHUNK), 1) & (CHUNK - 1)
    valid = ((key < CHUNK) & (key > t) & (ci > 0)) | ((key >= CHUNK) & (key - CHUNK <= t))
    return qr, kband, vband, valid


SCALE = HEAD ** -0.5


def _stack_heads(x, base, hk):
    return jnp.concatenate([x[:, base + HEAD * (4 * hk + i):base + HEAD * (4 * hk + i + 1)] for i in range(4)], axis=0)


def _sink_row(snk, hk):
    lane = lax.broadcasted_iota(jnp.int32, (1, 4 * CHUNK), 1)
    s = [snk[:, 4 * hk + i:4 * hk + i + 1] for i in range(4)]
    return jnp.where(lane < CHUNK, s[0], jnp.where(lane < 2 * CHUNK, s[1], jnp.where(lane < 3 * CHUNK, s[2], s[3])))


def _group_probs(kh, q4, valid, sink4):
    s = jnp.where(valid, _dot(kh, q4, NT), -1e30)
    mx = jnp.maximum(jnp.max(s, axis=0, keepdims=True), sink4)
    e = jnp.exp(s - mx)
    es = jnp.exp(sink4 - mx)
    inv = 1.0 / (jnp.sum(e, axis=0, keepdims=True) + es)
    return e * inv, es * inv


def _mixer_specs(nb, rev):
    def cur(i):
        return nb - 1 - i if rev else i

    def prev(i):
        return jnp.maximum(cur(i) - 1, 0)

    full = lambda shape: pl.BlockSpec(shape, lambda i: (0,) * len(shape))
    specs = [
        pl.BlockSpec((CHUNK, 1536), lambda i: (cur(i), 0)),
        pl.BlockSpec((CHUNK, 256), lambda i: (prev(i), 2)),
        pl.BlockSpec((CHUNK, 256), lambda i: (prev(i), 5)),
        pl.BlockSpec((CHUNK, 128), lambda i: (cur(i), 0)),
        pl.BlockSpec((CHUNK, 128), lambda i: (cur(i), 0)),
        pl.BlockSpec((CHUNK, 128), lambda i: (prev(i), 0)),
        pl.BlockSpec((CHUNK, 128), lambda i: (prev(i), 0)),
        full((1, 256)), full((4, CHUNK, CHUNK)), full((CHUNK, 4)), full((256, 256)), full((1, 256)), full((1, 8)),
        full((256, 256)),
    ]
    return specs, cur


def _ones_bd():
    g = lax.broadcasted_iota(jnp.int32, (256, 256), 0) // HEAD == lax.broadcasted_iota(jnp.int32, (256, 256), 1) // HEAD
    return g.astype(BF16)


def _mixer_fwd(name, z, cosq, sinq, gv, ws, bt, pw, psc, snk):
    s = z.shape[0]
    nb = s // CHUNK
    specs, _ = _mixer_specs(nb, False)

    def body(zc_ref, zpp_ref, zpkv_ref, cq_ref, sq_ref, cp_ref, sp_ref, gv_ref, ws_ref, bt_ref, pw_ref, psc_ref,
             snk_ref, bd_ref, o_ref, mix_s):
        ci = pl.program_id(0)
        zc = zc_ref[...]
        gel, _ = _gelu_parts(zc[:, :512])
        u, _, _, _, _, _, mixed = _gating(gel, gv_ref[...], ws_ref, bt_ref[...], bd_ref[...], mix_s)
        o_ref[:, :256] = (u * mixed).astype(BF16)
        pp = jnp.where(ci > 0, zpp_ref[...], 0.0)
        pooled, _, _ = _pool_fwd(zc[:, 512:768], pp, ci)
        mp = _dot(pooled.astype(BF16), pw_ref[...].astype(BF16))
        o_ref[:, 256:512] = (mp * psc_ref[...]).astype(BF16)
        qr, kband, vband, valid = _attn_prep(zc, zpkv_ref[...], cq_ref[...], sq_ref[...], cp_ref[...], sp_ref[...], ci)
        snkv = snk_ref[...]
        kb = kband.astype(BF16)
        vt = vband.T
        ots = []
        for hk in range(2):
            q4 = (_stack_heads(qr, 0, hk) * SCALE).astype(BF16)
            p, _ = _group_probs(kb[:, HEAD * hk:HEAD * (hk + 1)], q4, valid, _sink_row(snkv, hk))
            ots.append(_dot(vt[HEAD * hk:HEAD * (hk + 1), :].astype(BF16), p.astype(BF16)))
        o = jnp.concatenate(ots, axis=0).T
        for hk in range(2):
            for i in range(4):
                h = 4 * hk + i
                o_ref[:, 512 + HEAD * h:512 + HEAD * (h + 1)] = o[CHUNK * i:CHUNK * (i + 1),
                                                                  HEAD * hk:HEAD * (hk + 1)].astype(BF16)

    return pl.pallas_call(
        body, name=name, grid=(nb,), in_specs=specs, out_specs=pl.BlockSpec((CHUNK, 1024), lambda i: (i, 0)),
        out_shape=jax.ShapeDtypeStruct((s, 1024), BF16), scratch_shapes=[pltpu.VMEM((CHUNK, 256), F32)],
        compiler_params=_params(("parallel",)),
    )(z, z, z, cosq, sinq, cosq, sinq, gv, ws, bt, pw, psc, snk, _ones_bd())


def _mixer_bwd(name, z, dabc, cosq, sinq, gv, ws, bt, pw, psc, snk):
    s = z.shape[0]
    nb = s // CHUNK
    specs, cur = _mixer_specs(nb, True)
    specs = specs + [pl.BlockSpec((CHUNK, 1024), lambda i: (cur(i), 0))]
    full = lambda shape: pl.BlockSpec(shape, lambda i: (0,) * len(shape))
    acc_shapes = [(1, 256), (4, CHUNK, CHUNK), (CHUNK, 4), (256, 256), (1, 256), (1, 8)]

    def body(zc_ref, zpp_ref, zpkv_ref, cq_ref, sq_ref, cp_ref, sp_ref, gv_ref, ws_ref, bt_ref, pw_ref, psc_ref,
             snk_ref, bd_ref, dabc_ref, dz_ref, dgv_ref, dws_ref, dbt_ref, dpw_ref, dpsc_ref, dsnk_ref,
             cpool, ck, cv, dq_s, dkv_s, mix_s, dvn_s):
        step = pl.program_id(0)
        ci = nb - 1 - step

        @pl.when(step == 0)
        def _():
            for r in (dgv_ref, dws_ref, dbt_ref, dpw_ref, dpsc_ref, dsnk_ref, cpool, ck, cv):
                r[...] = jnp.zeros_like(r)

        zc = zc_ref[...]
        dabc = dabc_ref[...]
        zg = zc[:, :512]
        gel, th = _gelu_parts(zg)
        gp = _gelu_grad(zg, th)
        gvv = gv_ref[...]
        bd = bd_ref[...]
        u, r, xh, vn, wcs, causal, mixed = _gating(gel, gvv, ws_ref, bt_ref[...], bd, mix_s)
        da = dabc[:, :256]
        dm = da * u
        dmb = dm.astype(BF16)
        lane4 = lax.broadcasted_iota(jnp.int32, (CHUNK, 4), 1)
        dbt = jnp.zeros((CHUNK, 4), F32)
        for g in range(4):
            lo, hi = HEAD * g, HEAD * (g + 1)
            dws_ref[g] += jnp.where(causal, _dot(dmb[:, lo:hi], vn[:, lo:hi], NT), 0.0)
            dbt = dbt + jnp.where(lane4 == g, jnp.sum(dm[:, lo:hi], axis=-1, keepdims=True), 0.0)
            dvn_s[:, lo:hi] = _dot(wcs[g], dmb[:, lo:hi], TN)
        dbt_ref[...] += dbt
        dvn = dvn_s[...]
        dgv_ref[...] += jnp.sum(dvn * xh, axis=0, keepdims=True)
        dxh = dvn * gvv
        dvg = r * (dxh - xh * _group_mean(dxh * xh, bd))
        dz_ref[:, :256] = (da * mixed * gp[:, :256]).astype(BF16)
        dz_ref[:, 256:512] = (dvg * gp[:, 256:]).astype(BF16)
        pc = zc[:, 512:768]
        pp = jnp.where(ci > 0, zpp_ref[...], 0.0)
        pooled, cnt, lane = _pool_fwd(pc, pp, ci)
        pwb = pw_ref[...].astype(BF16)
        pooled_b = pooled.astype(BF16)
        mp = _dot(pooled_b, pwb)
        db = dabc[:, 256:512]
        dpsc_ref[...] += jnp.sum(db * mp, axis=0, keepdims=True)
        dmpb = (db * psc_ref[...]).astype(BF16)
        dpw_ref[...] += _dot(pooled_b, dmpb, TN)
        dpooled = _dot(dmpb, pwb, NT)
        davg = dpooled / cnt
        zero = jnp.zeros((CHUNK, 256), F32)
        d2, d4, d8, d16 = [jnp.concatenate([zero, jnp.where((lane >= 64 * k) & (lane < 64 * (k + 1)), davg, 0.0)],
                                           axis=0) for k in range(4)]
        g8 = d8 + d16 + pltpu.roll(d16, 2 * CHUNK - 8, 0)
        g4 = d4 + g8 + pltpu.roll(g8, 2 * CHUNK - 4, 0)
        g2 = d2 + g4 + pltpu.roll(g4, 2 * CHUNK - 2, 0)
        ge = g2 + pltpu.roll(g2, 2 * CHUNK - 1, 0)
        dz_ref[:, 512:768] = (ge[CHUNK:] - dpooled + cpool[...]).astype(BF16)
        cpool[...] = ge[:CHUNK]
        cc = cq_ref[...]
        sc = sq_ref[...]
        qr, kband, vband, valid = _attn_prep(zc, zpkv_ref[...], cc, sc, cp_ref[...], sp_ref[...], ci)
        snkv = snk_ref[...]
        lane8 = lax.broadcasted_iota(jnp.int32, (1, 8), 1)
        qlane = lax.broadcasted_iota(jnp.int32, (1, 4 * CHUNK), 1)
        dsnk = jnp.zeros((1, 8), F32)
        kb = kband.astype(BF16)
        vb = vband.astype(BF16)
        kt = kband.T * SCALE
        dqts = []
        for hk in range(2):
            kh = kb[:, HEAD * hk:HEAD * (hk + 1)]
            q4 = (_stack_heads(qr, 0, hk) * SCALE).astype(BF16)
            do4 = _stack_heads(dabc, 512, hk).astype(BF16)
            p, ps = _group_probs(kh, q4, valid, _sink_row(snkv, hk))
            dp = _dot(vb[:, HEAD * hk:HEAD * (hk + 1)], do4, NT)
            dd = jnp.sum(p * dp, axis=0, keepdims=True)
            dsink = -ps * dd
            for i in range(4):
                part = jnp.sum(jnp.where((qlane >= CHUNK * i) & (qlane < CHUNK * (i + 1)), dsink, 0.0),
                               axis=1, keepdims=True)
                dsnk = dsnk + jnp.where(lane8 == 4 * hk + i, part, 0.0)
            dsb = (p * (dp - dd)).astype(BF16)
            dqts.append(_dot(kt[HEAD * hk:HEAD * (hk + 1), :].astype(BF16), dsb))
            dkv_s[:, HEAD * hk:HEAD * (hk + 1)] = _dot(dsb, q4)
            dkv_s[:, 128 + HEAD * hk:128 + HEAD * (hk + 1)] = _dot(p.astype(BF16), do4)
        dq4 = jnp.concatenate(dqts, axis=0).T
        for hk in range(2):
            for i in range(4):
                h = 4 * hk + i
                dq_s[:, HEAD * h:HEAD * (h + 1)] = dq4[CHUNK * i:CHUNK * (i + 1), HEAD * hk:HEAD * (hk + 1)]
        dsnk_ref[...] += dsnk
        dqr = dq_s[...]
        dz_ref[:, 768:1280] = (dqr * jnp.concatenate([cc] * 4, axis=1)
                               + _rot_half(dqr * jnp.concatenate([sc] * 4, axis=1))).astype(BF16)
        dkv = dkv_s[...]
        dkr = dkv[CHUNK:, :128] + ck[...]
        dz_ref[:, 1280:1408] = (dkr * cc + _rot_half(dkr * sc)).astype(BF16)
        dz_ref[:, 1408:1536] = (dkv[CHUNK:, 128:] + cv[...]).astype(BF16)
        ck[...] = dkv[:CHUNK, :128]
        cv[...] = dkv[:CHUNK, 128:]

    return pl.pallas_call(
        body, name=name, grid=(nb,), in_specs=specs,
        out_specs=[pl.BlockSpec((CHUNK, 1536), lambda i: (cur(i), 0))] + [full(a) for a in acc_shapes],
        out_shape=[jax.ShapeDtypeStruct((s, 1536), BF16)] + [jax.ShapeDtypeStruct(a, F32) for a in acc_shapes],
        scratch_shapes=[pltpu.VMEM((CHUNK, 256), F32), pltpu.VMEM((CHUNK, 128), F32), pltpu.VMEM((CHUNK, 128), F32),
                        pltpu.VMEM((CHUNK, 512), F32), pltpu.VMEM((2 * CHUNK, 256), F32),
                        pltpu.VMEM((CHUNK, 256), F32), pltpu.VMEM((CHUNK, 256), F32)],
        compiler_params=_params(("arbitrary",)),
    )(z, z, z, cosq, sinq, cosq, sinq, gv, ws, bt, pw, psc, snk, _ones_bd(), dabc)


def _xattn_probs(qh, kh):
    s = _dot(qh, kh, NT) * (256 ** -0.5)
    e = jnp.exp(s - jnp.max(s, axis=-1, keepdims=True))
    return e * (1.0 / jnp.sum(e, axis=-1, keepdims=True))


def _xattn_fwd(name, q, kv, tq):
    s, d = q.shape
    mlen = kv.shape[0]

    def body(q_ref, kv_ref, o_ref):
        for h in range(4):
            lo, hi = 256 * h, 256 * (h + 1)
            p = _xattn_probs(q_ref[:, lo:hi], kv_ref[:, lo:hi])
            o_ref[:, lo:hi] = _dot(p.astype(BF16), kv_ref[:, d + lo:d + hi]).astype(BF16)

    blk = pl.BlockSpec((tq, d), lambda i: (i, 0))
    return pl.pallas_call(body, name=name, grid=(s // tq,),
                          in_specs=[blk, pl.BlockSpec((mlen, 2 * d), lambda i: (0, 0))], out_specs=blk,
                          out_shape=jax.ShapeDtypeStruct((s, d), BF16), compiler_params=_params(("parallel",)))(q, kv)


def _xattn_bwd(name, q, kv, do, tq):
    s, d = q.shape
    mlen = kv.shape[0]

    def body(q_ref, kv_ref, do_ref, dq_ref, dkv_ref):
        @pl.when(pl.program_id(0) == 0)
        def _():
            dkv_ref[...] = jnp.zeros_like(dkv_ref)

        for h in range(4):
            lo, hi = 256 * h, 256 * (h + 1)
            qh = q_ref[:, lo:hi]
            kh = kv_ref[:, lo:hi]
            vh = kv_ref[:, d + lo:d + hi]
            doh = do_ref[:, lo:hi]
            p = _xattn_probs(qh, kh)
            dp = _dot(doh, vh, NT)
            dsb = (p * (dp - jnp.sum(p * dp, axis=-1, keepdims=True)) * (256 ** -0.5)).astype(BF16)
            dq_ref[:, lo:hi] = _dot(dsb, kh).astype(BF16)
            dkv_ref[:, lo:hi] += _dot(dsb, qh, TN)
            dkv_ref[:, d + lo:d + hi] += _dot(p.astype(BF16), doh, TN)

    blk = pl.BlockSpec((tq, d), lambda i: (i, 0))
    kvb = pl.BlockSpec((mlen, 2 * d), lambda i: (0, 0))
    return pl.pallas_call(
        body, name=name, grid=(s // tq,), in_specs=[blk, kvb, blk], out_specs=[blk, kvb],
        out_shape=[jax.ShapeDtypeStruct((s, d), BF16), jax.ShapeDtypeStruct((mlen, 2 * d), F32)],
        compiler_params=_params(("arbitrary",)))(q, kv, do)


def _sigmoid(x):
    return 0.5 * (1.0 + jnp.tanh(0.5 * x))


def _ffn_up(name, h, wgu, layer, tm, tn):
    s, d = h.shape
    dff = wgu.shape[2] // 2
    nj = dff // tn

    def body(h_ref, wg_ref, wu_ref, g_ref, u_ref, a_ref):
        hv = h_ref[...]
        gate = _dot(hv, wg_ref[...])
        up = _dot(hv, wu_ref[...])
        g_ref[...] = gate.astype(BF16)
        u_ref[...] = up.astype(BF16)
        a_ref[...] = (gate * _sigmoid(gate) * up).astype(BF16)

    ob = pl.BlockSpec((tm, tn), lambda j, i: (i, j))
    sd = jax.ShapeDtypeStruct((s, dff), BF16)
    return pl.pallas_call(
        body, name=name, grid=(nj, s // tm),
        in_specs=[pl.BlockSpec((tm, d), lambda j, i: (i, 0)),
                  pl.BlockSpec((None, d, tn), lambda j, i: (layer, 0, j)),
                  pl.BlockSpec((None, d, tn), lambda j, i: (layer, 0, j + nj))],
        out_specs=[ob, ob, ob], out_shape=[sd, sd, sd], compiler_params=_params(("parallel", "parallel")),
    )(h, wgu, wgu)


def _ffn_act_bwd(name, dfn, wdown, layer, gate, up, tm):
    s, d = dfn.shape
    dff = gate.shape[1]

    tn = 256

    def body(df_ref, wd_ref, g_ref, u_ref, o_ref):
        df = df_ref[...]
        for lo in range(0, dff, tn):
            dact = _dot(df, wd_ref[lo:lo + tn, :], NT).astype(BF16)
            gate = g_ref[:, lo:lo + tn]
            sig = _sigmoid(gate)
            gs = gate * sig
            o_ref[:, lo:lo + tn] = dact * u_ref[:, lo:lo + tn] * (sig + gs * (1.0 - sig))
            o_ref[:, dff + lo:dff + lo + tn] = dact * gs

    gb = pl.BlockSpec((tm, dff), lambda i: (i, 0))
    return pl.pallas_call(
        body, name=name, grid=(s // tm,),
        in_specs=[pl.BlockSpec((tm, d), lambda i: (i, 0)),
                  pl.BlockSpec((None, dff, d), lambda i: (layer, 0, 0)), gb, gb],
        out_specs=pl.BlockSpec((tm, 2 * dff), lambda i: (i, 0)),
        out_shape=jax.ShapeDtypeStruct((s, 2 * dff), BF16), compiler_params=_params(("parallel",)),
    )(dfn, wdown, gate, up)


def _place():
    return lax.axis_index("x"), lax.axis_index("y"), lax.axis_index("c")


def _other_chips(x, y):
    return [(1 - x, y), (x, 1 - y), (1 - x, 1 - y)]


def _region(ref, axis, chip, size):
    start = pl.multiple_of(chip * size, size)
    if axis == 1:
        return ref.at[:, pl.ds(start, size), :]
    return ref.at[:, :, pl.ds(start, size)]


ANY = pl.BlockSpec(memory_space=pl.ANY)


HBM = pl.BlockSpec(memory_space=pltpu.HBM)
SEM = pl.BlockSpec(memory_space=pltpu.SEMAPHORE)
EFFECT = pltpu.SideEffectType.DATAFLOW_SIDE_EFFECTING


def _in_hbm(a):
    return pltpu.with_memory_space_constraint(a, pltpu.HBM)


def _split_start(name, srcs, lands, ncopies, plan):
    ns, nl = len(srcs), len(lands)

    def body(*refs):
        src, land = refs[:ns], refs[ns:ns + nl]
        send, recv = refs[ns + nl], refs[ns + nl + 1]
        token = refs[-1]
        x, y, c = _place()
        for k, (s_ref, d_ref, peer, _) in enumerate(plan(src, land, x, y, c)):
            pltpu.make_async_remote_copy(src_ref=s_ref, dst_ref=d_ref, send_sem=send.at[k], recv_sem=recv.at[k],
                                         device_id=peer, device_id_type=MESH).start()
        token[...] = jnp.zeros_like(token)

    ops = list(srcs) + list(lands)
    out = pl.pallas_call(
        body, name=name,
        out_shape=(pltpu.SemaphoreType.DMA((ncopies,)), pltpu.SemaphoreType.DMA((ncopies,)),
                   *[pltpu.HBM(a.shape, a.dtype) for a in ops], jax.ShapeDtypeStruct((8, 128), F32)),
        in_specs=(HBM,) * (ns + nl),
        out_specs=(SEM, SEM) + (HBM,) * (ns + nl) + (pl.BlockSpec(memory_space=pltpu.VMEM),),
        input_output_aliases={i: 2 + i for i in range(ns + nl)},
        compiler_params=pltpu.CompilerParams(has_side_effects=EFFECT),
    )(*[_in_hbm(a) for a in ops])
    return out[0], out[1], list(out[2:2 + ns]), list(out[2 + ns:2 + ns + nl]), out[-1]


def _split_start_many(name, lands, jobs):
    nl, nj = len(lands), len(jobs)

    def body(*refs):
        land = refs[:nl]
        sems = refs[nl:nl + 2 * nj]
        token = refs[-1]
        x, y, c = _place()
        for j, (idx, _, plan) in enumerate(jobs):
            for k, (s_ref, d_ref, peer, _) in enumerate(plan((), [land[t] for t in idx], x, y, c)):
                pltpu.make_async_remote_copy(src_ref=s_ref, dst_ref=d_ref, send_sem=sems[2 * j].at[k],
                                             recv_sem=sems[2 * j + 1].at[k], device_id=peer,
                                             device_id_type=MESH).start()
        token[...] = jnp.zeros_like(token)

    sem_shapes = tuple(pltpu.SemaphoreType.DMA((n,)) for _, n, _ in jobs for _ in range(2))
    out = pl.pallas_call(
        body, name=name,
        out_shape=sem_shapes + tuple(pltpu.HBM(a.shape, a.dtype) for a in lands)
        + (jax.ShapeDtypeStruct((8, 128), F32),),
        in_specs=(HBM,) * nl,
        out_specs=(SEM,) * (2 * nj) + (HBM,) * nl + (pl.BlockSpec(memory_space=pltpu.VMEM),),
        input_output_aliases={i: 2 * nj + i for i in range(nl)},
        compiler_params=pltpu.CompilerParams(has_side_effects=EFFECT),
    )(*[_in_hbm(a) for a in lands])
    return [(out[2 * j], out[2 * j + 1]) for j in range(nj)], list(out[2 * nj:2 * nj + nl]), out[-1]


def _split_wait(name, send, recv, srcs, lands, after, plan):
    ns, nl = len(srcs), len(lands)

    def body(*refs):
        src, land = refs[:ns], refs[ns:ns + nl]
        send_ref, recv_ref = refs[ns + nl], refs[ns + nl + 1]
        x, y, c = _place()
        for k, (s_ref, _, _, got) in enumerate(plan(src, land, x, y, c)):
            cp = pltpu.make_async_remote_copy(src_ref=s_ref, dst_ref=got, send_sem=send_ref.at[k],
                                              recv_sem=recv_ref.at[k], device_id=(x, y, c), device_id_type=MESH)
            cp.wait_send()
            cp.wait_recv()

    ops = list(srcs) + list(lands)
    out = pl.pallas_call(
        body, name=name, out_shape=tuple(pltpu.HBM(a.shape, a.dtype) for a in ops),
        in_specs=(HBM,) * (ns + nl) + (SEM, SEM, ANY), out_specs=(HBM,) * (ns + nl),
        input_output_aliases={i: i for i in range(ns + nl)},
        compiler_params=pltpu.CompilerParams(has_side_effects=EFFECT),
    )(*ops, send, recv, after)
    return list(out[:ns]), list(out[ns:])


def _split_wait_start(name, send, recv, lands, after, wait_plan, ncopies, start_plan, carried=()):
    nl, nc = len(lands), len(carried)
    lands = list(lands) + list(carried)

    def body(*refs):
        land = refs[:nl]
        send_in, recv_in = refs[nl + nc], refs[nl + nc + 1]
        send_out, recv_out = refs[nl + nc + 3], refs[nl + nc + 4]
        x, y, c = _place()
        for k, (s_ref, _, _, got) in enumerate(wait_plan((), land, x, y, c)):
            cp = pltpu.make_async_remote_copy(src_ref=s_ref, dst_ref=got, send_sem=send_in.at[k],
                                              recv_sem=recv_in.at[k], device_id=(x, y, c), device_id_type=MESH)
            cp.wait_send()
            cp.wait_recv()
        for k, (s_ref, d_ref, peer, _) in enumerate(start_plan((), land, x, y, c)):
            pltpu.make_async_remote_copy(src_ref=s_ref, dst_ref=d_ref, send_sem=send_out.at[k],
                                         recv_sem=recv_out.at[k], device_id=peer, device_id_type=MESH).start()

    out = pl.pallas_call(
        body, name=name,
        out_shape=(pltpu.SemaphoreType.DMA((ncopies,)), pltpu.SemaphoreType.DMA((ncopies,)),
                   *[pltpu.HBM(a.shape, a.dtype) for a in lands]),
        in_specs=(HBM,) * (nl + nc) + (SEM, SEM, ANY), out_specs=(SEM, SEM) + (HBM,) * (nl + nc),
        input_output_aliases={i: 2 + i for i in range(nl + nc)},
        compiler_params=pltpu.CompilerParams(has_side_effects=EFFECT),
    )(*lands, send, recv, after)
    return out[0], out[1], list(out[2:2 + nl]), list(out[2 + nl:])


def _half(ref, axis, chip, size, layer, h):
    reg = _region(ref, axis, chip, size).at[pl.ds(layer, 1)]
    rows = reg.shape[1] // 2
    return reg.at[:, pl.ds(pl.multiple_of(h * rows, rows), rows), :]


def _gather_plan(axes, sizes, layer):
    def plan(src, land, x, y, c):
        me = 2 * x + y
        out = []
        for t in range(len(land)):
            mine = _half(land[t], axes[t], me, sizes[t], layer, c)
            for px, py in _other_chips(x, y):
                out.append((mine, mine, (px, py, c), _half(land[t], axes[t], 2 * px + py, sizes[t], layer, c)))
        return out
    return plan


def _forward_plan(axes, sizes, layer):
    def plan(src, land, x, y, c):
        out = []
        for t in range(len(land)):
            for px, py in _other_chips(x, y):
                got = _half(land[t], axes[t], 2 * px + py, sizes[t], layer, c)
                out.append((got, got, (x, y, 1 - c), _half(land[t], axes[t], 2 * px + py, sizes[t], layer, 1 - c)))
        return out
    return plan


def _place_own(name, w, axis, chip):
    nl, r, cs = w.shape
    tr = _rows_tile(r, cs)
    nb = r // tr
    full = (nl, 4 * r, cs) if axis == 1 else (nl, r, 4 * cs)

    def body(m_ref, w_ref, o_ref):
        o_ref[...] = w_ref[...].astype(BF16)

    if axis == 1:
        ospec = pl.BlockSpec((None, tr, cs), lambda l, i, m: (l, m[0] * nb + i, 0))
    else:
        ospec = pl.BlockSpec((None, tr, cs), lambda l, i, m: (l, i, m[0]))
    return pl.pallas_call(
        body, name=name,
        grid_spec=pltpu.PrefetchScalarGridSpec(
            num_scalar_prefetch=1, grid=(nl, nb),
            in_specs=[pl.BlockSpec((None, tr, cs), lambda l, i, m: (l, i, 0))], out_specs=ospec),
        out_shape=jax.ShapeDtypeStruct(full, BF16), compiler_params=_params(("parallel", "parallel")),
    )(chip, w)


def _scatter_plan(axes, sizes):
    def plan(src, land, x, y, c):
        out = []
        for t in range(len(src)):
            for k, (px, py) in enumerate(_other_chips(x, y)):
                out.append((_region(src[t], axes[t], 2 * px + py, sizes[t]).at[0], land[t].at[k], (px, py, c),
                            land[t].at[k]))
        return out
    return plan


def _pair_plan(src, land, x, y, c):
    return [(src[t], land[t], (x, y, 1 - c), land[t]) for t in range(len(src))]


UNIT_STEPS = 16


def _unit_rows(r):
    return min(t for t in range(16, r + 1, 16) if r % t == 0 and r // t <= UNIT_STEPS)


def _guarded(i, nb, steps, work):
    if nb == steps:
        work()
    else:
        pl.when(i < nb)(work)


def _unit_chip_sum(name, gs, slots, axes, chip):
    n = len(gs)
    dims = [s.shape[1:] for s in slots]
    trs = [_unit_rows(r) for r, _ in dims]
    nbs = [r // tr for (r, _), tr in zip(dims, trs)]
    steps = max(nbs)

    def body(m_ref, *refs):
        i = pl.program_id(0)
        for t in range(n):
            def work(t=t):
                acc = refs[t][...].astype(F32)
                for k in range(3):
                    acc = acc + refs[n + t][k].astype(F32)
                refs[2 * n + t][...] = acc.astype(BF16)
            _guarded(i, nbs[t], steps, work)

    gspecs, sspecs, ospecs = [], [], []
    for (r, cs), tr, nb, axis in zip(dims, trs, nbs, axes):
        if axis == 1:
            gspecs.append(pl.BlockSpec((tr, cs), lambda i, m, nb=nb: (m[0] * nb + jnp.minimum(i, nb - 1), 0)))
        else:
            gspecs.append(pl.BlockSpec((tr, cs), lambda i, m, nb=nb: (jnp.minimum(i, nb - 1), m[0])))
        sspecs.append(pl.BlockSpec((3, tr, cs), lambda i, m, nb=nb: (0, jnp.minimum(i, nb - 1), 0)))
        ospecs.append(pl.BlockSpec((tr, cs), lambda i, m, nb=nb: (jnp.minimum(i, nb - 1), 0)))
    return pl.pallas_call(
        body, name=name,
        grid_spec=pltpu.PrefetchScalarGridSpec(num_scalar_prefetch=1, grid=(steps,), in_specs=gspecs + sspecs,
                                               out_specs=ospecs),
        out_shape=[jax.ShapeDtypeStruct(d, BF16) for d in dims], compiler_params=_params(("arbitrary",)),
    )(chip, *gs, *slots)


def _unit_adamw(name, mine, theirs, ws, ms, vs, layer, bufs):
    n = len(mine)
    dims = [a.shape for a in mine]
    trs = [_unit_rows(r) for r, _ in dims]
    nbs = [r // tr for (r, _), tr in zip(dims, trs)]
    steps = max(nbs)
    c1 = 1.0 - B1 ** STEP
    c2 = 1.0 - B2 ** STEP

    def body(*refs):
        i = pl.program_id(0)
        outs = refs[9 * n:]
        for t in range(n):
            def work(t=t):
                a_ref, b_ref, w_ref, m_ref, v_ref = refs[5 * t:5 * t + 5]
                g_ref, d_ref, mo_ref, vo_ref = outs[4 * t:4 * t + 4]
                gv = a_ref[...].astype(F32) + b_ref[...].astype(F32)
                mn = B1 * m_ref[...] + (1.0 - B1) * gv
                vn = B2 * v_ref[...] + (1.0 - B2) * (gv * gv)
                g_ref[...] = gv
                mo_ref[...] = mn
                vo_ref[...] = vn
                d_ref[...] = -LR * ((mn / c1) / (jnp.sqrt(vn / c2) + ADAM_EPS) + WD * w_ref[...])
            _guarded(i, nbs[t], steps, work)

    in_specs, out_specs, ops = [], [], []
    for t, ((r, cs), tr, nb) in enumerate(zip(dims, trs, nbs)):
        blk = pl.BlockSpec((tr, cs), lambda i, nb=nb: (jnp.minimum(i, nb - 1), 0))
        lay = pl.BlockSpec((None, tr, cs), lambda i, nb=nb: (layer, jnp.minimum(i, nb - 1), 0))
        in_specs += [blk, blk, lay, lay, lay]
        out_specs += [lay] * 4
        ops += [mine[t], theirs[t], ws[t], ms[t], vs[t]]
    flat = [b for bs in bufs for b in bs]
    out = pl.pallas_call(
        body, name=name, grid=(steps,), in_specs=in_specs + [ANY] * (4 * n), out_specs=out_specs,
        out_shape=[jax.ShapeDtypeStruct(b.shape, b.dtype) for b in flat],
        input_output_aliases={5 * n + k: k for k in range(4 * n)}, compiler_params=_params(("arbitrary",)),
    )(*ops, *flat)
    return [list(out[4 * t:4 * t + 4]) for t in range(n)]


def _small_pair_sum(p):
    rows = p.shape[0]
    half = rows // 2

    def body(p_ref, o_ref, sib, send, recv):
        x, y, c = _place()
        mine = pl.ds(pl.multiple_of(c * half, half), half)
        theirs = pl.ds(pl.multiple_of((1 - c) * half, half), half)
        pair = pltpu.make_async_remote_copy(src_ref=p_ref.at[theirs], dst_ref=sib, send_sem=send, recv_sem=recv,
                                            device_id=(x, y, 1 - c), device_id_type=MESH)
        pair.start()
        pair.wait()
        o_ref[...] = (p_ref[mine] + sib[...]).astype(BF16)

    vm = pl.BlockSpec(memory_space=pltpu.VMEM)
    return pl.pallas_call(
        body, name="small_pair_sum", in_specs=[vm], out_specs=vm, out_shape=jax.ShapeDtypeStruct((half, 128), BF16),
        scratch_shapes=[pltpu.VMEM((half, 128), F32), pltpu.SemaphoreType.DMA, pltpu.SemaphoreType.DMA],
        compiler_params=pltpu.CompilerParams(vmem_limit_bytes=VMEM_LIMIT),
    )(p)


def _small_plan(src, land, x, y, c):
    return [(src[0], land[0].at[k], (px, py, c), land[0].at[k]) for k, (px, py) in enumerate(_other_chips(x, y))]


def _small_total(own, slots):
    half = own.shape[0]

    def body(own_ref, slots_ref, o_ref, sums, send, recv):
        x, y, c = _place()
        me = 2 * x + y
        mine = pl.ds(pl.multiple_of(c * half, half), half)
        theirs = pl.ds(pl.multiple_of((1 - c) * half, half), half)
        sums[me] = own_ref[...]
        for k, (px, py) in enumerate(_other_chips(x, y)):
            sums[2 * px + py] = slots_ref[k]
        acc = sums[0].astype(F32)
        for k in range(1, 4):
            acc = acc + sums[k].astype(F32)
        o_ref[mine] = acc
        back = pltpu.make_async_remote_copy(src_ref=o_ref.at[mine], dst_ref=o_ref.at[mine], send_sem=send,
                                            recv_sem=recv, device_id=(x, y, 1 - c), device_id_type=MESH)
        back.start()
        pltpu.make_async_remote_copy(src_ref=o_ref.at[theirs], dst_ref=o_ref.at[theirs], send_sem=send,
                                     recv_sem=recv, device_id=(x, y, c), device_id_type=MESH).wait_recv()
        back.wait_send()

    vm = pl.BlockSpec(memory_space=pltpu.VMEM)
    return pl.pallas_call(
        body, name="small_total", in_specs=[vm, vm], out_specs=vm,
        out_shape=jax.ShapeDtypeStruct((2 * half, 128), F32),
        scratch_shapes=[pltpu.VMEM((4, half, 128), BF16), pltpu.SemaphoreType.DMA, pltpu.SemaphoreType.DMA],
        compiler_params=pltpu.CompilerParams(vmem_limit_bytes=VMEM_LIMIT),
    )(own, slots)


def _small_adamw(gs, ws, ms, vs):
    n = len(gs)
    c1 = 1.0 - B1 ** STEP
    c2 = 1.0 - B2 ** STEP

    def body(*refs):
        for t in range(n):
            gv, wv = refs[t][...], refs[n + t][...]
            d_ref, mo_ref, vo_ref = refs[4 * n + 3 * t:4 * n + 3 * t + 3]
            mn = B1 * refs[2 * n + t][...] + (1.0 - B1) * gv
            vn = B2 * refs[3 * n + t][...] + (1.0 - B2) * (gv * gv)
            mo_ref[...] = mn
            vo_ref[...] = vn
            d_ref[...] = -LR * ((mn / c1) / (jnp.sqrt(vn / c2) + ADAM_EPS) + WD * wv)

    vm = pl.BlockSpec(memory_space=pltpu.VMEM)
    out = pl.pallas_call(
        body, name="adamw_small", in_specs=[vm] * (4 * n), out_specs=[vm] * (3 * n),
        out_shape=[jax.ShapeDtypeStruct(w.shape, F32) for w in ws for _ in range(3)],
        compiler_params=pltpu.CompilerParams(vmem_limit_bytes=VMEM_LIMIT),
    )(*gs, *ws, *ms, *vs)
    return [tuple(out[3 * t:3 * t + 3]) for t in range(n)]


def _pack(parts):
    flat = []
    for p in parts:
        v = p.reshape(-1).astype(F32)
        flat.append(jnp.pad(v, (0, (-v.shape[0]) % 128)))
    v = jnp.concatenate(flat)
    v = jnp.pad(v, (0, (-v.shape[0]) % (512 * 128)))
    return v.reshape(-1, 128)


def _unpack(buf, like):
    out, r0 = [], 0
    for p in like:
        nelem = 1
        for s in p.shape:
            nelem *= s
        rows = -(-nelem // 128)
        blk = buf[r0:r0 + rows]
        if nelem % 128:
            blk = blk.reshape(-1)[:nelem]
        out.append(blk.reshape(p.shape))
        r0 += rows
    return out


def kernel(x, mem, positions, mem_norm_g, mix_pre_g, mix_post_g, w_in, gm_v_g, gm_w_s, gm_b_s, pool_w, pool_scale, attn_sinks, w_o, x_pre_g, x_post_g, w_xq, w_xkv, w_xo, ffn_pre_g, ffn_post_g, w_gate_up, w_down, loss_target, m_mem_norm_g, m_mix_pre_g, m_mix_post_g, m_w_in, m_gm_v_g, m_gm_w_s, m_gm_b_s, m_pool_w, m_pool_scale, m_attn_sinks, m_w_o, m_x_pre_g, m_x_post_g, m_w_xq, m_w_xkv, m_w_xo, m_ffn_pre_g, m_ffn_post_g, m_w_gate_up, m_w_down, v_mem_norm_g, v_mix_pre_g, v_mix_post_g, v_w_in, v_gm_v_g, v_gm_w_s, v_gm_b_s, v_pool_w, v_pool_scale, v_attn_sinks, v_w_o, v_x_pre_g, v_x_post_g, v_w_xq, v_w_xkv, v_w_xo, v_ffn_pre_g, v_ffn_post_g, v_w_gate_up, v_w_down):
    args = (x, mem, positions, mem_norm_g, mix_pre_g, mix_post_g, w_in, gm_v_g, gm_w_s, gm_b_s, pool_w, pool_scale, attn_sinks, w_o, x_pre_g, x_post_g, w_xq, w_xkv, w_xo, ffn_pre_g, ffn_post_g, w_gate_up, w_down)
    moms_m = (m_mem_norm_g, m_mix_pre_g, m_mix_post_g, m_w_in, m_gm_v_g, m_gm_w_s, m_gm_b_s, m_pool_w, m_pool_scale, m_attn_sinks, m_w_o, m_x_pre_g, m_x_post_g, m_w_xq, m_w_xkv, m_w_xo, m_ffn_pre_g, m_ffn_post_g, m_w_gate_up, m_w_down)
    moms_v = (v_mem_norm_g, v_mix_pre_g, v_mix_post_g, v_w_in, v_gm_v_g, v_gm_w_s, v_gm_b_s, v_pool_w, v_pool_scale, v_attn_sinks, v_w_o, v_x_pre_g, v_x_post_g, v_w_xq, v_w_xkv, v_w_xo, v_ffn_pre_g, v_ffn_post_g, v_w_gate_up, v_w_down)
    P = dict(zip(NAMES, args))
    P['loss_target'] = loss_target
    M = dict(zip(WEIGHTS, moms_m))
    V = dict(zip(WEIGHTS, moms_v))
    depth = w_in.shape[0]
    nbig = len(BIG)
    axes = [BIG_AXIS[n] for n in BIG]
    sizes = [P[n].shape[a] for n, a in zip(BIG, axes)]
    chip = (2 * lax.axis_index("x") + lax.axis_index("y")).astype(jnp.int32).reshape(1)

    groups = [['w_in'], ['w_o', 'w_xq', 'w_xkv', 'w_xo'], ['w_gate_up', 'w_down']]
    units = [(l, g) for l in range(depth) for g in groups]
    unit_of = {(l, n): i for i, (l, names) in enumerate(units) for n in names}
    ax = lambda names: [BIG_AXIS[n] for n in names]
    sz = lambda names: [P[n].shape[BIG_AXIS[n]] for n in names]

    full = {n: _place_own("place_" + n, P[n], BIG_AXIS[n], chip) for n in BIG}
    gathers, land, tok = _split_start_many(
        "gather_start", [full[n] for n in BIG],
        [([BIG.index(n) for n in names], 3 * len(names), _gather_plan(ax(names), sz(names), l)) for l, names in units])
    full.update(zip(BIG, land))
    P['first_dep'] = tok[:1, :1]
    forwards, gathered = {}, set()

    def forward_unit(i, after, carried=()):
        ul, unames = units[i]
        send, recv = gathers[i]
        send, recv, land, thru = _split_wait_start(
            "gather_pass%d" % i, send, recv, [full[n] for n in unames], after,
            _gather_plan(ax(unames), sz(unames), ul), 3 * len(unames), _forward_plan(ax(unames), sz(unames), ul),
            carried=[full[n] for n in carried])
        full.update(zip(unames, land))
        full.update(zip(carried, thru))
        forwards[i] = (send, recv)

    def weights_of(l, names, after):
        i = unit_of[(l, names[0])]
        if i not in gathered:
            _, unames = units[i]
            if i not in forwards:
                forward_unit(i, after)
            send, recv = forwards.pop(i)
            _, land = _split_wait("gather_wait%d" % i, send, recv, [], [full[n] for n in unames], after,
                                  _forward_plan(ax(unames), sz(unames), l))
            full.update(zip(unames, land))
            gathered.add(i)
            if len(groups) <= i + 1 < len(units):
                forward_unit(i + 1, after, carried=[n for n in unames if n not in units[i + 1][1]])
        return {n: (full[n], l) for n in names}

    outs = {n: [lax.empty(P[n].shape, F32) for _ in range(4)] for n in BIG}
    gunits = [(l, BIG) for l in range(depth - 1, 0, -1)] + [
        (0, g) for g in (['w_gate_up', 'w_down'], ['w_xq', 'w_xkv', 'w_xo'], ['w_o'], ['w_in'])]
    collected, scatters, pairs = {}, {}, {}

    def finish_scatter(i, after):
        _, names = gunits[i]
        send, recv, g_l, slots = scatters.pop(i)
        g_l, slots = _split_wait("scatter_wait%d" % i, send, recv, g_l, slots, after,
                                 _scatter_plan(ax(names), sz(names)))
        mine = _unit_chip_sum("chip_sum", [g.reshape(g.shape[1:]) for g in g_l], slots, ax(names), chip)
        send, recv, mine, theirs, tok = _split_start("pair_start%d" % i, mine, [lax.empty(a.shape, BF16) for a in mine],
                                                     len(names), _pair_plan)
        pairs[i] = (send, recv, mine, theirs)
        return tok[:1, :1]

    def finish_pair(i, after):
        l, names = gunits[i]
        send, recv, mine, theirs = pairs.pop(i)
        mine, theirs = _split_wait("pair_wait%d" % i, send, recv, mine, theirs, after, _pair_plan)
        new = _unit_adamw("adamw", mine, theirs, [P[n] for n in names], [M[n] for n in names], [V[n] for n in names],
                          l, [outs[n] for n in names])
        outs.update(zip(names, new))

    calls = {'n': 0}
    lag = 4

    def grads_of(l, g_part, after):
        collected.update({(l, n): g for n, g in g_part.items()})
        calls['n'] += 1
        now = calls['n']
        tok = jnp.zeros((1, 1), F32)
        for i, (ul, names) in enumerate(gunits):
            if ul != l or ('started', i) in collected or any((l, n) not in collected for n in names):
                continue
            collected[('started', i)] = now
            srcs = [collected[(l, n)].reshape((1,) + collected[(l, n)].shape) for n in names]
            send, recv, srcs, slots, t = _split_start("scatter_start%d" % i, srcs,
                                                      [lax.empty((3,) + P[n].shape[1:], BF16) for n in names],
                                                      3 * len(names), _scatter_plan(ax(names), sz(names)))
            scatters[i] = (send, recv, srcs, slots)
            tok = tok + t[:1, :1]
        for i in sorted(pairs):
            if collected[('summed', i)] + lag <= now:
                finish_pair(i, after)
        for i in sorted(scatters):
            if collected[('started', i)] + lag <= now:
                tok = tok + finish_scatter(i, after)
                collected[('summed', i)] = now
        return tok

    loss_part, dx, small_g = _fwd_bwd(P, weights_of, grads_of)
    loss = lax.psum(loss_part[0, 0], ("x", "y", "c"))
    grad_x = dx.reshape(x.shape)

    small_like = [P[n] for n in SMALL]
    half_sum = _small_pair_sum(_pack(small_g))
    send, recv, (half_sum,), (slots,), _ = _split_start(
        "small_start", [half_sum], [lax.empty((3,) + half_sum.shape, BF16)], 3, _small_plan)

    for i in sorted(pairs):
        finish_pair(i, half_sum)
    for i in sorted(scatters):
        finish_scatter(i, half_sum)
    for i in sorted(pairs):
        finish_pair(i, half_sum)
    grads, deltas, new_m, new_v = {}, {}, {}, {}
    for n in BIG:
        grads[n], deltas[n], new_m[n], new_v[n] = outs[n]

    (half_sum,), (slots,) = _split_wait("small_wait", send, recv, [half_sum], [slots], grads[BIG[0]], _small_plan)
    two_d = lambda a: a.reshape(-1, a.shape[-1])
    gs = _unpack(_small_total(half_sum, slots), small_like)
    upd = _small_adamw([two_d(g) for g in gs], [two_d(P[n]) for n in SMALL], [two_d(M[n]) for n in SMALL],
                       [two_d(V[n]) for n in SMALL])
    for n, g, (dlt, mn, vn) in zip(SMALL, gs, upd):
        shape = P[n].shape
        grads[n], deltas[n], new_m[n], new_v[n] = g, dlt.reshape(shape), mn.reshape(shape), vn.reshape(shape)

    return (loss, grad_x, *[grads[n] for n in WEIGHTS], *[deltas[n] for n in WEIGHTS],
            *[new_m[n] for n in WEIGHTS], *[new_v[n] for n in WEIGHTS])


def _fwd_bwd(P, weights_of, grads_of):
    (x, mem, positions, mem_norm_g, mix_pre_g, mix_post_g, w_in, gm_v_g, gm_w_s, gm_b_s, pool_w, pool_scale, attn_sinks,
     w_o, x_pre_g, x_post_g, w_xq, w_xkv, w_xo, ffn_pre_g, ffn_post_g, w_gate_up, w_down) = [P[n] for n in NAMES]
    x0 = x[0]
    s, d = x0.shape
    depth = w_in.shape[0]
    tgt = P['loss_target'][0]
    tmn = 256
    tmr = min(512, s)
    tmp = min(1024, s)
    tkw = min(2048, s)

    half = HEAD // 2
    inv = ROPE_THETA ** (-jnp.arange(half, dtype=F32) / half)
    ang = positions[0].astype(F32)[:, None] * inv
    cos, sin = jnp.cos(ang), jnp.sin(ang)
    cosq = jnp.tile(jnp.concatenate([cos, cos], axis=-1), (1, 2))
    sinq = jnp.tile(jnp.concatenate([-sin, sin], axis=-1), (1, 2))

    row = lambda a, l: a[l].reshape(1, -1)
    memn = _prenorm("mem_norm", mem[0], mem_norm_g.reshape(1, d), tmn)
    pw_all = jnp.where(_ones_bd() > 0, jnp.tile(pool_w.reshape(depth, 4 * HEAD, HEAD), (1, 1, 4)), 0.0)
    pw_bd = [pw_all[l] for l in range(depth)]

    saved = []
    xc = x0
    h = _prenorm("pre_norm0", x0, row(mix_pre_g, 0) + P['first_dep'], tmn)
    for l in range(depth):
        W = weights_of(l, ['w_in'], xc)
        sv = {'x0': xc, 'h1': h}
        z, = _mm_rows("fwd_w_in", h, *W['w_in'], 'nn', tm=tmp, rows_out=[F32], epilogue=_plain_rows)
        abc = _mixer_fwd("mixer_fwd", z, cosq, sinq, row(gm_v_g, l), gm_w_s[l], gm_b_s[l].T, pw_bd[l],
                         row(pool_scale, l), row(attn_sinks, l))
        W.update(weights_of(l, ['w_o'], z))
        mix, xc, h = _mm_rows("fwd_w_o", abc, *W['w_o'], 'nn', tm=tmr, rows_in=[xc],
                              params=[row(mix_post_g, l), row(x_pre_g, l)], rows_out=[BF16, F32, BF16],
                              epilogue=_post_pre_rows)
        sv.update(z=z, abc=abc, mix=mix, x1=xc, h2=h)
        W.update(weights_of(l, ['w_xq', 'w_xkv', 'w_xo'], xc))
        q, = _mm_rows("fwd_w_xq", h, *W['w_xq'], 'nn', tm=tmp, rows_out=[BF16], epilogue=_plain_rows)
        kv = _mm_nn("fwd_w_xkv", memn, *W['w_xkv'], tm=256, tn=512, tk=d, out_dtype=BF16)
        o = _xattn_fwd("xattn_fwd", q, kv, 512)
        xo, xc, h = _mm_rows("fwd_w_xo", o, *W['w_xo'], 'nn', tm=tmr, rows_in=[xc],
                             params=[row(x_post_g, l), row(ffn_pre_g, l)], rows_out=[BF16, F32, BF16],
                             epilogue=_post_pre_rows)
        sv.update(q=q, kv=kv, o=o, xo=xo, x2=xc, h3=h)
        W.update(weights_of(l, ['w_gate_up', 'w_down'], xc))
        dff = W['w_down'][0].shape[1]
        gate, up, act = _ffn_up("ffn_up", h, *W['w_gate_up'], 512, dff // 2)
        sv.update(gate=gate, up=up, act=act)
        if l + 1 < depth:
            f, xc, h = _mm_rows("fwd_w_down", act, *W['w_down'], 'nn', tm=tmr, rows_in=[xc],
                                params=[row(ffn_post_g, l), row(mix_pre_g, l + 1)], rows_out=[BF16, F32, BF16],
                                epilogue=_post_pre_rows)
            sv.update(f=f)
        saved.append(sv)
    gs = {n: [None] * depth for n in SMALL if n != 'mem_norm_g'}
    dx, dfn, gs['ffn_post_g'][depth - 1], loss_part = _mm_rows(
        "fwd_w_down_loss", saved[-1]['act'], *W['w_down'], 'nn', tm=tmr, rows_in=[xc, tgt],
        params=[row(ffn_post_g, depth - 1)], rows_out=[F32, BF16], n_sums=2, epilogue=_make_loss_rows(d))

    dmemn = None
    tok = jnp.zeros((1, 1), F32)
    for l in reversed(range(depth)):
        sv, W, G = saved[l], weights_of(l, BIG, dx), {}
        G['w_down'] = _mm_tn("dw_down", sv['act'], dfn, tm=dff // 2, tn=d, tk=tkw)
        dgu = _ffn_act_bwd("ffn_act_bwd", dfn, *W['w_down'], sv['gate'], sv['up'], 256)
        G['w_gate_up'] = _mm_tn("dw_gate_up", sv['h3'], dgu, tm=d, tn=dff // 2, tk=tkw)
        dx, dxo, gs['ffn_pre_g'][l], gs['x_post_g'][l] = _mm_rows(
            "bwd_w_gate_up", dgu, *W['w_gate_up'], 'nt', tm=tmr, rows_in=[sv['x2'], dx, sv['xo']],
            params=[row(ffn_pre_g, l) + tok, row(x_post_g, l)], rows_out=[F32, BF16], n_sums=2, epilogue=_bwd_rows)
        tok = grads_of(l, {n: G[n] for n in ('w_gate_up', 'w_down')}, dx)
        G['w_xo'] = _mm_tn("dw_xo", sv['o'], dxo, tm=d, tn=d, tk=tkw)
        do, = _mm_rows("bwd_w_xo", dxo, *W['w_xo'], 'nt', tm=tmp, params=[jnp.zeros((1, d), F32) + tok],
                       rows_out=[BF16], epilogue=_plain_rows)
        dq, dkv = _xattn_bwd("xattn_bwd", sv['q'], sv['kv'], do, 512)
        dkv = dkv.astype(BF16)
        G['w_xkv'] = _mm_tn("dw_xkv", memn, dkv, tm=d, tn=d, tk=mem.shape[1])
        dmemn = _mm_nt("bwd_w_xkv", dkv, *W['w_xkv'], tm=mem.shape[1], tn=512, tk=2 * d, out_dtype=F32, add=dmemn)
        G['w_xq'] = _mm_tn("dw_xq", sv['h2'], dq, tm=d, tn=d, tk=tkw)
        dx, dmix, gs['x_pre_g'][l], gs['mix_post_g'][l] = _mm_rows(
            "bwd_w_xq", dq, *W['w_xq'], 'nt', tm=tmr, rows_in=[sv['x1'], dx, sv['mix']],
            params=[row(x_pre_g, l), row(mix_post_g, l)], rows_out=[F32, BF16], n_sums=2, epilogue=_bwd_rows)
        tok = grads_of(l, {n: G[n] for n in ('w_xq', 'w_xkv', 'w_xo')}, dx)
        G['w_o'] = _mm_tn("dw_o", sv['abc'], dmix, tm=d, tn=d, tk=tkw)
        dabc, = _mm_rows("bwd_w_o", dmix, *W['w_o'], 'nt', tm=tmp, params=[jnp.zeros((1, d), F32) + tok],
                         rows_out=[F32], epilogue=_plain_rows)
        tok = grads_of(l, {'w_o': G['w_o']}, dabc)
        dz, dgv, dws, dbt, dpw, dpsc, dsnk = _mixer_bwd(
            "mixer_bwd", sv['z'], dabc, cosq, sinq, row(gm_v_g, l) + tok, gm_w_s[l], gm_b_s[l].T, pw_bd[l],
            row(pool_scale, l), row(attn_sinks, l))
        gs['gm_v_g'][l] = dgv
        gs['gm_w_s'][l] = dws
        gs['gm_b_s'][l] = dbt.T
        gs['pool_w'][l] = dpw
        gs['pool_scale'][l] = dpsc
        gs['attn_sinks'][l] = dsnk
        G['w_in'] = _mm_tn("dw_in", sv['h1'], dz, tm=d, tn=dz.shape[1], tk=tkw)
        if l > 0:
            dx, dfn, gs['mix_pre_g'][l], gs['ffn_post_g'][l - 1] = _mm_rows(
                "bwd_w_in", dz, *W['w_in'], 'nt', tm=tmr, rows_in=[sv['x0'], dx, saved[l - 1]['f']],
                params=[row(mix_pre_g, l), row(ffn_post_g, l - 1)], rows_out=[F32, BF16], n_sums=2,
                epilogue=_bwd_rows)
        else:
            dx, gs['mix_pre_g'][l] = _mm_rows(
                "bwd_w_in_first", dz, *W['w_in'], 'nt', tm=tmr, rows_in=[sv['x0'], dx],
                params=[row(mix_pre_g, l)], rows_out=[F32], n_sums=1, epilogue=_bwd_rows_first)
        tok = grads_of(l, {'w_in': G['w_in']}, dx)
    _, dg_mem = _norm_bwd("bwd_mem_norm", mem[0], mem_norm_g.reshape(1, d) + tok, dmemn, None, BF16, tmn)
    small_g = []
    for n in SMALL:
        if n == 'mem_norm_g':
            small_g.append(dg_mem.reshape(P[n].shape))
        elif n == 'pool_w':
            blocks = jnp.stack(gs[n]).reshape(depth, 4, HEAD, 4, HEAD)
            same = jnp.eye(4, dtype=jnp.bool_)[None, :, None, :, None]
            small_g.append(jnp.sum(jnp.where(same, blocks, 0.0), axis=3))
        else:
            small_g.append(jnp.stack([a.reshape(P[n].shape[1:]) for a in gs[n]]))
    return loss_part, dx, small_g
```

```python
import functools

import jax
import jax.numpy as jnp
from jax import lax
from jax.experimental import pallas as pl
from jax.experimental.pallas import tpu as pltpu

F32 = jnp.float32
BF16 = jnp.bfloat16
EPS = 1e-6
CHUNK = 128
HEAD = 64
ROPE_THETA = 10000.0
POOL_WINDOWS = (2, 4, 8, 16)
LR, B1, B2, ADAM_EPS, WD, STEP = 0.001, 0.9, 0.999, 1e-08, 0.01, 10
MESH = pl.DeviceIdType.MESH
VMEM_LIMIT = 56 * 1024 * 1024

NAMES = ['x', 'mem', 'positions', 'mem_norm_g', 'mix_pre_g', 'mix_post_g', 'w_in', 'gm_v_g', 'gm_w_s', 'gm_b_s',
         'pool_w', 'pool_scale', 'attn_sinks', 'w_o', 'x_pre_g', 'x_post_g', 'w_xq', 'w_xkv', 'w_xo', 'ffn_pre_g',
         'ffn_post_g', 'w_gate_up', 'w_down']
WEIGHTS = NAMES[3:]
BIG = ['w_in', 'w_o', 'w_xq', 'w_xkv', 'w_xo', 'w_gate_up', 'w_down']
BIG_AXIS = {'w_in': 2, 'w_o': 1, 'w_xq': 1, 'w_xkv': 2, 'w_xo': 1, 'w_gate_up': 2, 'w_down': 1}
SMALL = [n for n in WEIGHTS if n not in BIG]

NN = (((1,), (0,)), ((), ()))
NT = (((1,), (1,)), ((), ()))
TN = (((0,), (0,)), ((), ()))


def _dot(a, b, dims=NN):
    return lax.dot_general(a, b, dims, preferred_element_type=F32)


def _params(sem):
    return pltpu.CompilerParams(dimension_semantics=sem, vmem_limit_bytes=VMEM_LIMIT)


STREAM_BLOCK_BYTES = 3 * 512 * 1024


def _rows_tile(rows, cols):
    limit = max(16, STREAM_BLOCK_BYTES // (4 * cols))
    return max(t for t in range(16, min(rows, limit) + 1, 16) if rows % t == 0)


def _mm(name, a, a_spec, b, b_spec, dims, grid, nk, out_shape, out_spec, add=None, add_spec=None):
    acc_shape = out_spec.block_shape

    def body(*refs):
        a_ref, b_ref = refs[0], refs[1]
        pos = 2
        add_ref = None
        if add is not None:
            add_ref = refs[pos]
            pos += 1
        o_ref = refs[pos]
        part = _dot(a_ref[...].astype(BF16), b_ref[...].astype(BF16), dims)
        if nk == 1:
            if add_ref is not None:
                part = part + add_ref[...]
            o_ref[...] = part.astype(o_ref.dtype)
        else:
            acc_ref = refs[pos + 1]
            k = pl.program_id(2)

            @pl.when(k == 0)
            def _():
                acc_ref[...] = part if add_ref is None else part + add_ref[...]

            @pl.when(k > 0)
            def _():
                acc_ref[...] += part

            @pl.when(k == nk - 1)
            def _():
                o_ref[...] = acc_ref[...].astype(o_ref.dtype)

    ops, specs = [a, b], [a_spec, b_spec]
    if add is not None:
        ops.append(add)
        specs.append(add_spec)
    return pl.pallas_call(
        body, name=name, grid=grid, in_specs=specs, out_specs=out_spec, out_shape=out_shape,
        scratch_shapes=[pltpu.VMEM(acc_shape, F32)] if nk > 1 else [],
        compiler_params=_params(("parallel", "parallel", "arbitrary")),
    )(*ops)


def _wspec(block, layer, fn):
    return pl.BlockSpec((None,) + block, lambda i, j, k: (layer,) + fn(i, j, k))


def _mm_nn(name, a, w, layer, *, tm, tn, tk, out_dtype):
    m, kk = a.shape
    n = w.shape[2]
    tm = min(tm, m)
    nk = kk // tk
    return _mm(name, a, pl.BlockSpec((tm, tk), lambda i, j, k: (i, k)),
               w, _wspec((tk, tn), layer, lambda i, j, k: (k, j)), NN,
               (m // tm, n // tn, nk), nk, jax.ShapeDtypeStruct((m, n), out_dtype),
               pl.BlockSpec((tm, tn), lambda i, j, k: (i, j)))


def _mm_nt(name, a, w, layer, *, tm, tn, tk, out_dtype, add=None):
    m, kk = a.shape
    n = w.shape[1]
    tm = min(tm, m)
    nk = kk // tk
    ospec = pl.BlockSpec((tm, tn), lambda i, j, k: (i, j))
    return _mm(name, a, pl.BlockSpec((tm, tk), lambda i, j, k: (i, k)),
               w, _wspec((tn, tk), layer, lambda i, j, k: (j, k)), NT,
               (m // tm, n // tn, nk), nk, jax.ShapeDtypeStruct((m, n), out_dtype), ospec,
               add=add, add_spec=ospec if add is not None else None)


def _mm_tn(name, a, b, *, tm, tn, tk):
    kk, m = a.shape
    n = b.shape[1]
    tk = min(tk, kk)
    nk = kk // tk
    return _mm(name, a, pl.BlockSpec((tk, tm), lambda i, j, k: (k, i)),
               b, pl.BlockSpec((tk, tn), lambda i, j, k: (k, j)), TN,
               (m // tm, n // tn, nk), nk, jax.ShapeDtypeStruct((m, n), BF16),
               pl.BlockSpec((tm, tn), lambda i, j, k: (i, j)))


def _mm_rows(name, a, w, layer, mode, *, tm, rows_in=(), params=(), rows_out=(), n_sums=0, epilogue):
    m, kk = a.shape
    n = w.shape[2] if mode == 'nn' else w.shape[1]
    nr, npar, no = len(rows_in), len(params), len(rows_out)

    def body(*refs):
        a_ref, w_ref = refs[0], refs[1]
        rin = refs[2:2 + nr]
        par = refs[2 + nr:2 + nr + npar]
        outs = refs[2 + nr + npar:2 + nr + npar + no]
        sums = refs[2 + nr + npar + no:2 + nr + npar + no + n_sums]
        @pl.when(pl.program_id(0) == 0)
        def _():
            for r in sums:
                r[...] = jnp.zeros_like(r)

        rows = tm // 2 if rows_in else tm
        for r0 in range(0, tm, rows):
            acc = _dot(a_ref[r0:r0 + rows, :], w_ref[...], NN if mode == 'nn' else NT)
            res, sm = epilogue(acc, [r[r0:r0 + rows, :] for r in rin], [p[...] for p in par])
            for r, v in zip(outs, res):
                r[r0:r0 + rows, :] = v.astype(r.dtype)
            for r, v in zip(sums, sm):
                r[...] += v

    wblock = (None, kk, n) if mode == 'nn' else (None, n, kk)
    rowblk = pl.BlockSpec((tm, n), lambda i: (i, 0))
    one = pl.BlockSpec((1, n), lambda i: (0, 0))
    return pl.pallas_call(
        body, name=name, grid=(m // tm,),
        in_specs=[pl.BlockSpec((tm, kk), lambda i: (i, 0)),
                  pl.BlockSpec(wblock, lambda i: (layer, 0, 0), pipeline_mode=pl.Buffered(1))]
                 + [rowblk] * nr + [one] * npar,
        out_specs=[rowblk] * no + [one] * n_sums,
        out_shape=[jax.ShapeDtypeStruct((m, n), dt) for dt in rows_out] +
                  [jax.ShapeDtypeStruct((1, n), F32)] * n_sums,
        compiler_params=_params(("arbitrary",)),
    )(a, w, *rows_in, *params)


def _rstd(x):
    return lax.rsqrt(jnp.mean(x * x, axis=-1, keepdims=True) + EPS)


def _norm_back(xin, g, dy):
    r = _rstd(xin)
    xh = xin * r
    dyg = dy * g
    return r * (dyg - xh * jnp.mean(dyg * xh, axis=-1, keepdims=True)), jnp.sum(dy * xh, axis=0, keepdims=True)


def _plain_rows(acc, rows, pars):
    return [acc], []


def _post_pre_rows(y, rows, pars):
    xn = rows[0] + y * _rstd(y) * pars[0]
    return [y, xn, xn * _rstd(xn) * pars[1]], []


def _make_loss_rows(d):
    def fn(y, rows, pars):
        x, tgt = rows
        err = x + y * _rstd(y) * pars[0] - tgt
        dout = err * (1.0 / d)
        dy, dg = _norm_back(y, pars[0], dout)
        lsum = 0.5 * jnp.sum(jnp.mean(err * err, axis=-1, keepdims=True), axis=0, keepdims=True)
        return [dout, dy], [dg, jnp.broadcast_to(lsum, dg.shape)]
    return fn


def _bwd_rows(dh, rows, pars):
    xin, resid, yprev = rows
    dxa, dg_pre = _norm_back(xin, pars[0], dh)
    dx = resid + dxa
    dyp, dg_post = _norm_back(yprev.astype(F32), pars[1], dx)
    return [dx, dyp], [dg_pre, dg_post]


def _bwd_rows_first(dh, rows, pars):
    xin, resid = rows
    dxa, dg_pre = _norm_back(xin, pars[0], dh)
    return [resid + dxa], [dg_pre]


def _row(d):
    return pl.BlockSpec((1, d), lambda i: (0, 0))


def _prenorm(name, x, g, tm):
    m, d = x.shape

    def body(x_ref, g_ref, o_ref):
        xv = x_ref[...]
        o_ref[...] = (xv * _rstd(xv) * g_ref[...]).astype(BF16)

    blk = pl.BlockSpec((tm, d), lambda i: (i, 0))
    return pl.pallas_call(body, name=name, grid=(m // tm,), in_specs=[blk, _row(d)], out_specs=blk,
                          out_shape=jax.ShapeDtypeStruct((m, d), BF16), compiler_params=_params(("parallel",)))(x, g)


def _norm_bwd(name, xin, g, dy, resid, out_dtype, tm):
    m, d = xin.shape

    def body(*refs):
        if resid is None:
            x_ref, g_ref, dy_ref, dx_ref, dg_ref = refs
        else:
            x_ref, g_ref, dy_ref, r_ref, dx_ref, dg_ref = refs
        xv = x_ref[...]
        r = _rstd(xv)
        xh = xv * r
        dyv = dy_ref[...].astype(F32)
        dyg = dyv * g_ref[...]
        dx = r * (dyg - xh * jnp.mean(dyg * xh, axis=-1, keepdims=True))
        if resid is not None:
            dx = dx + r_ref[...]
        dx_ref[...] = dx.astype(dx_ref.dtype)

        @pl.when(pl.program_id(0) == 0)
        def _():
            dg_ref[...] = jnp.zeros_like(dg_ref)

        dg_ref[...] += jnp.sum(dyv * xh, axis=0, keepdims=True)

    blk = pl.BlockSpec((tm, d), lambda i: (i, 0))
    ops = [xin, g, dy] + ([] if resid is None else [resid])
    specs = [blk, _row(d), blk] + ([] if resid is None else [blk])
    return pl.pallas_call(
        body, name=name, grid=(m // tm,), in_specs=specs, out_specs=[blk, _row(d)],
        out_shape=[jax.ShapeDtypeStruct((m, d), out_dtype), jax.ShapeDtypeStruct((1, d), F32)],
        compiler_params=_params(("arbitrary",)))(*ops)


def _gelu_parts(x):
    c = 0.7978845608028654
    t = jnp.tanh(c * (x + 0.044715 * (x * x * x)))
    return 0.5 * x * (1.0 + t), t


def _gelu_grad(x, t):
    c = 0.7978845608028654
    return 0.5 * (1.0 + t) + 0.5 * x * (1.0 - t * t) * (c * (1.0 + 3.0 * 0.044715 * x * x))


def _rot_half(x):
    ax = x.ndim - 1
    w = x.shape[ax]
    lane = lax.broadcasted_iota(jnp.int32, x.shape, ax)
    return jnp.where((lane & 63) < 32, pltpu.roll(x, w - 32, ax), pltpu.roll(x, 32, ax))


def _group_mean(x, ones_bd):
    hi = x.astype(BF16)
    lo = (x - hi.astype(F32)).astype(BF16)
    return (_dot(hi, ones_bd) + _dot(lo, ones_bd)) * (1.0 / HEAD)


def _gating(gel, gv, ws_ref, bt, ones_bd, mix_s):
    u = gel[:, :256]
    v = gel[:, 256:]
    r = lax.rsqrt(_group_mean(v * v, ones_bd) + EPS)
    xh = v * r
    vn = (xh * gv).astype(BF16)
    row = lax.broadcasted_iota(jnp.int32, (CHUNK, CHUNK), 0)
    col = lax.broadcasted_iota(jnp.int32, (CHUNK, CHUNK), 1)
    causal = col <= row
    wcs = []
    for g in range(4):
        wc = jnp.where(causal, ws_ref[g], 0.0).astype(BF16)
        wcs.append(wc)
        mix_s[:, HEAD * g:HEAD * (g + 1)] = _dot(wc, vn[:, HEAD * g:HEAD * (g + 1)]) + bt[:, g:g + 1]
    return u, r, xh, vn, wcs, causal, mix_s[...]


def _lane_select(lane, vals):
    return jnp.where(lane < 64, vals[0], jnp.where(lane < 128, vals[1], jnp.where(lane < 192, vals[2], vals[3])))


def _pool_fwd(pc, pp, ci):
    ext = jnp.concatenate([pp, pc], axis=0)
    s2 = ext + pltpu.roll(ext, 1, 0)
    s4 = s2 + pltpu.roll(s2, 2, 0)
    s8 = s4 + pltpu.roll(s4, 4, 0)
    s16 = s8 + pltpu.roll(s8, 8, 0)
    t1 = ci * CHUNK + lax.broadcasted_iota(jnp.int32, (CHUNK, 1), 0) + 1
    lane = lax.broadcasted_iota(jnp.int32, (1, 256), 1)
    cnt = _lane_select(lane, [jnp.minimum(t1, w).astype(F32) for w in POOL_WINDOWS])
    ssel = _lane_select(lane, [s[CHUNK:] for s in (s2, s4, s8, s16)])
    return ssel / cnt - pc, cnt, lane


def _attn_prep(zc, zpkv, cc, sc, cp, sp, ci):
    q = zc[:, 768:1280]
    kc = zc[:, 1280:1408]
    vc = zc[:, 1408:1536]
    kp = zpkv[:, :128]
    vp = zpkv[:, 128:]
    qr = q * jnp.concatenate([cc] * 4, axis=1) + _rot_half(q) * jnp.concatenate([sc] * 4, axis=1)
    krc = kc * cc + _rot_half(kc) * sc
    krp = kp * cp + _rot_half(kp) * sp
    kband = jnp.concatenate([krp, krc], axis=0)
    vband = jnp.concatenate([vp, vc], axis=0)
    key = lax.broadcasted_iota(jnp.int32, (2 * CHUNK, 4 * CHUNK), 0)
    t = lax.broadcasted_iota(jnp.int32, (2 * CHUNK, 4 * CHUNK), 1) & (CHUNK - 1)
    valid = ((key < CHUNK) & (key > t) & (ci > 0)) | ((key >= CHUNK) & (key - CHUNK <= t))
    return qr, kband, vband, valid


SCALE = HEAD ** -0.5


def _stack_heads(x, base, hk):
    return jnp.concatenate([x[:, base + HEAD * (4 * hk + i):base + HEAD * (4 * hk + i + 1)] for i in range(4)], axis=0)


def _sink_row(snk, hk):
    lane = lax.broadcasted_iota(jnp.int32, (1, 4 * CHUNK), 1)
    s = [snk[:, 4 * hk + i:4 * hk + i + 1] for i in range(4)]
    return jnp.where(lane < CHUNK, s[0], jnp.where(lane < 2 * CHUNK, s[1], jnp.where(lane < 3 * CHUNK, s[2], s[3])))


def _group_probs(kh, q4, valid, sink4):
    s = jnp.where(valid, _dot(kh, q4, NT), -1e30)
    mx = jnp.maximum(jnp.max(s, axis=0, keepdims=True), sink4)
    e = jnp.exp(s - mx)
    es = jnp.exp(sink4 - mx)
    inv = 1.0 / (jnp.sum(e, axis=0, keepdims=True) + es)
    return e * inv, es * inv


def _mixer_specs(nb, rev):
    def cur(i):
        return nb - 1 - i if rev else i

    def prev(i):
        return jnp.maximum(cur(i) - 1, 0)

    full = lambda shape: pl.BlockSpec(shape, lambda i: (0,) * len(shape))
    specs = [
        pl.BlockSpec((CHUNK, 1536), lambda i: (cur(i), 0)),
        pl.BlockSpec((CHUNK, 256), lambda i: (prev(i), 2)),
        pl.BlockSpec((CHUNK, 256), lambda i: (prev(i), 5)),
        pl.BlockSpec((CHUNK, 128), lambda i: (cur(i), 0)),
        pl.BlockSpec((CHUNK, 128), lambda i: (cur(i), 0)),
        pl.BlockSpec((CHUNK, 128), lambda i: (prev(i), 0)),
        pl.BlockSpec((CHUNK, 128), lambda i: (prev(i), 0)),
        full((1, 256)), full((4, CHUNK, CHUNK)), full((CHUNK, 4)), full((256, 256)), full((1, 256)), full((1, 8)),
        full((256, 256)),
    ]
    return specs, cur


def _ones_bd():
    g = lax.broadcasted_iota(jnp.int32, (256, 256), 0) // HEAD == lax.broadcasted_iota(jnp.int32, (256, 256), 1) // HEAD
    return g.astype(BF16)


def _mixer_fwd(name, z, cosq, sinq, gv, ws, bt, pw, psc, snk):
    s = z.shape[0]
    nb = s // CHUNK
    specs, _ = _mixer_specs(nb, False)

    def body(zc_ref, zpp_ref, zpkv_ref, cq_ref, sq_ref, cp_ref, sp_ref, gv_ref, ws_ref, bt_ref, pw_ref, psc_ref,
             snk_ref, bd_ref, o_ref, mix_s):
        ci = pl.program_id(0)
        zc = zc_ref[...]
        gel, _ = _gelu_parts(zc[:, :512])
        u, _, _, _, _, _, mixed = _gating(gel, gv_ref[...], ws_ref, bt_ref[...], bd_ref[...], mix_s)
        o_ref[:, :256] = (u * mixed).astype(BF16)
        pp = jnp.where(ci > 0, zpp_ref[...], 0.0)
        pooled, _, _ = _pool_fwd(zc[:, 512:768], pp, ci)
        mp = _dot(pooled.astype(BF16), pw_ref[...].astype(BF16))
        o_ref[:, 256:512] = (mp * psc_ref[...]).astype(BF16)
        qr, kband, vband, valid = _attn_prep(zc, zpkv_ref[...], cq_ref[...], sq_ref[...], cp_ref[...], sp_ref[...], ci)
        snkv = snk_ref[...]
        kb = kband.astype(BF16)
        vt = vband.T
        ots = []
        for hk in range(2):
            q4 = (_stack_heads(qr, 0, hk) * SCALE).astype(BF16)
            p, _ = _group_probs(kb[:, HEAD * hk:HEAD * (hk + 1)], q4, valid, _sink_row(snkv, hk))
            ots.append(_dot(vt[HEAD * hk:HEAD * (hk + 1), :].astype(BF16), p.astype(BF16)))
        o = jnp.concatenate(ots, axis=0).T
        for hk in range(2):
            for i in range(4):
                h = 4 * hk + i
                o_ref[:, 512 + HEAD * h:512 + HEAD * (h + 1)] = o[CHUNK * i:CHUNK * (i + 1),
                                                                  HEAD * hk:HEAD * (hk + 1)].astype(BF16)

    return pl.pallas_call(
        body, name=name, grid=(nb,), in_specs=specs, out_specs=pl.BlockSpec((CHUNK, 1024), lambda i: (i, 0)),
        out_shape=jax.ShapeDtypeStruct((s, 1024), BF16), scratch_shapes=[pltpu.VMEM((CHUNK, 256), F32)],
        compiler_params=_params(("parallel",)),
    )(z, z, z, cosq, sinq, cosq, sinq, gv, ws, bt, pw, psc, snk, _ones_bd())


def _mixer_bwd(name, z, dabc, cosq, sinq, gv, ws, bt, pw, psc, snk):
    s = z.shape[0]
    nb = s // CHUNK
    specs, cur = _mixer_specs(nb, True)
    specs = specs + [pl.BlockSpec((CHUNK, 1024), lambda i: (cur(i), 0))]
    full = lambda shape: pl.BlockSpec(shape, lambda i: (0,) * len(shape))
    acc_shapes = [(1, 256), (4, CHUNK, CHUNK), (CHUNK, 4), (256, 256), (1, 256), (1, 8)]

    def body(zc_ref, zpp_ref, zpkv_ref, cq_ref, sq_ref, cp_ref, sp_ref, gv_ref, ws_ref, bt_ref, pw_ref, psc_ref,
             snk_ref, bd_ref, dabc_ref, dz_ref, dgv_ref, dws_ref, dbt_ref, dpw_ref, dpsc_ref, dsnk_ref,
             cpool, ck, cv, dq_s, dkv_s, mix_s, dvn_s):
        step = pl.program_id(0)
        ci = nb - 1 - step

        @pl.when(step == 0)
        def _():
            for r in (dgv_ref, dws_ref, dbt_ref, dpw_ref, dpsc_ref, dsnk_ref, cpool, ck, cv):
                r[...] = jnp.zeros_like(r)

        zc = zc_ref[...]
        dabc = dabc_ref[...]
        zg = zc[:, :512]
        gel, th = _gelu_parts(zg)
        gp = _gelu_grad(zg, th)
        gvv = gv_ref[...]
        bd = bd_ref[...]
        u, r, xh, vn, wcs, causal, mixed = _gating(gel, gvv, ws_ref, bt_ref[...], bd, mix_s)
        da = dabc[:, :256]
        dm = da * u
        dmb = dm.astype(BF16)
        lane4 = lax.broadcasted_iota(jnp.int32, (CHUNK, 4), 1)
        dbt = jnp.zeros((CHUNK, 4), F32)
        for g in range(4):
            lo, hi = HEAD * g, HEAD * (g + 1)
            dws_ref[g] += jnp.where(causal, _dot(dmb[:, lo:hi], vn[:, lo:hi], NT), 0.0)
            dbt = dbt + jnp.where(lane4 == g, jnp.sum(dm[:, lo:hi], axis=-1, keepdims=True), 0.0)
            dvn_s[:, lo:hi] = _dot(wcs[g], dmb[:, lo:hi], TN)
        dbt_ref[...] += dbt
        dvn = dvn_s[...]
        dgv_ref[...] += jnp.sum(dvn * xh, axis=0, keepdims=True)
        dxh = dvn * gvv
        dvg = r * (dxh - xh * _group_mean(dxh * xh, bd))
        dz_ref[:, :256] = (da * mixed * gp[:, :256]).astype(BF16)
        dz_ref[:, 256:512] = (dvg * gp[:, 256:]).astype(BF16)
        pc = zc[:, 512:768]
        pp = jnp.where(ci > 0, zpp_ref[...], 0.0)
        pooled, cnt, lane = _pool_fwd(pc, pp, ci)
        pwb = pw_ref[...].astype(BF16)
        pooled_b = pooled.astype(BF16)
        mp = _dot(pooled_b, pwb)
        db = dabc[:, 256:512]
        dpsc_ref[...] += jnp.sum(db * mp, axis=0, keepdims=True)
        dmpb = (db * psc_ref[...]).astype(BF16)
        dpw_ref[...] += _dot(pooled_b, dmpb, TN)
        dpooled = _dot(dmpb, pwb, NT)
        davg = dpooled / cnt
        zero = jnp.zeros((CHUNK, 256), F32)
        d2, d4, d8, d16 = [jnp.concatenate([zero, jnp.where((lane >= 64 * k) & (lane < 64 * (k + 1)), davg, 0.0)],
                                           axis=0) for k in range(4)]
        g8 = d8 + d16 + pltpu.roll(d16, 2 * CHUNK - 8, 0)
        g4 = d4 + g8 + pltpu.roll(g8, 2 * CHUNK - 4, 0)
        g2 = d2 + g4 + pltpu.roll(g4, 2 * CHUNK - 2, 0)
        ge = g2 + pltpu.roll(g2, 2 * CHUNK - 1, 0)
        dz_ref[:, 512:768] = (ge[CHUNK:] - dpooled + cpool[...]).astype(BF16)
        cpool[...] = ge[:CHUNK]
        cc = cq_ref[...]
        sc = sq_ref[...]
        qr, kband, vband, valid = _attn_prep(zc, zpkv_ref[...], cc, sc, cp_ref[...], sp_ref[...], ci)
        snkv = snk_ref[...]
        lane8 = lax.broadcasted_iota(jnp.int32, (1, 8), 1)
        qlane = lax.broadcasted_iota(jnp.int32, (1, 4 * CHUNK), 1)
        dsnk = jnp.zeros((1, 8), F32)
        kb = kband.astype(BF16)
        vb = vband.astype(BF16)
        kt = kband.T * SCALE
        dqts = []
        for hk in range(2):
            kh = kb[:, HEAD * hk:HEAD * (hk + 1)]
            q4 = (_stack_heads(qr, 0, hk) * SCALE).astype(BF16)
            do4 = _stack_heads(dabc, 512, hk).astype(BF16)
            p, ps = _group_probs(kh, q4, valid, _sink_row(snkv, hk))
            dp = _dot(vb[:, HEAD * hk:HEAD * (hk + 1)], do4, NT)
            dd = jnp.sum(p * dp, axis=0, keepdims=True)
            dsink = -ps * dd
            for i in range(4):
                part = jnp.sum(jnp.where((qlane >= CHUNK * i) & (qlane < CHUNK * (i + 1)), dsink, 0.0),
                               axis=1, keepdims=True)
                dsnk = dsnk + jnp.where(lane8 == 4 * hk + i, part, 0.0)
            dsb = (p * (dp - dd)).astype(BF16)
            dqts.append(_dot(kt[HEAD * hk:HEAD * (hk + 1), :].astype(BF16), dsb))
            dkv_s[:, HEAD * hk:HEAD * (hk + 1)] = _dot(dsb, q4)
            dkv_s[:, 128 + HEAD * hk:128 + HEAD * (hk + 1)] = _dot(p.astype(BF16), do4)
        dq4 = jnp.concatenate(dqts, axis=0).T
        for hk in range(2):
            for i in range(4):
                h = 4 * hk + i
                dq_s[:, HEAD * h:HEAD * (h + 1)] = dq4[CHUNK * i:CHUNK * (i + 1), HEAD * hk:HEAD * (hk + 1)]
        dsnk_ref[...] += dsnk
        dqr = dq_s[...]
        dz_ref[:, 768:1280] = (dqr * jnp.concatenate([cc] * 4, axis=1)
                               + _rot_half(dqr * jnp.concatenate([sc] * 4, axis=1))).astype(BF16)
        dkv = dkv_s[...]
        dkr = dkv[CHUNK:, :128] + ck[...]
        dz_ref[:, 1280:1408] = (dkr * cc + _rot_half(dkr * sc)).astype(BF16)
        dz_ref[:, 1408:1536] = (dkv[CHUNK:, 128:] + cv[...]).astype(BF16)
        ck[...] = dkv[:CHUNK, :128]
        cv[...] = dkv[:CHUNK, 128:]

    return pl.pallas_call(
        body, name=name, grid=(nb,), in_specs=specs,
        out_specs=[pl.BlockSpec((CHUNK, 1536), lambda i: (cur(i), 0))] + [full(a) for a in acc_shapes],
        out_shape=[jax.ShapeDtypeStruct((s, 1536), BF16)] + [jax.ShapeDtypeStruct(a, F32) for a in acc_shapes],
        scratch_shapes=[pltpu.VMEM((CHUNK, 256), F32), pltpu.VMEM((CHUNK, 128), F32), pltpu.VMEM((CHUNK, 128), F32),
                        pltpu.VMEM((CHUNK, 512), F32), pltpu.VMEM((2 * CHUNK, 256), F32),
                        pltpu.VMEM((CHUNK, 256), F32), pltpu.VMEM((CHUNK, 256), F32)],
        compiler_params=_params(("arbitrary",)),
    )(z, z, z, cosq, sinq, cosq, sinq, gv, ws, bt, pw, psc, snk, _ones_bd(), dabc)


def _xattn_probs(qh, kh):
    s = _dot(qh, kh, NT) * (256 ** -0.5)
    e = jnp.exp(s - jnp.max(s, axis=-1, keepdims=True))
    return e * (1.0 / jnp.sum(e, axis=-1, keepdims=True))


def _xattn_fwd(name, q, kv, tq):
    s, d = q.shape
    mlen = kv.shape[0]

    def body(q_ref, kv_ref, o_ref):
        for h in range(4):
            lo, hi = 256 * h, 256 * (h + 1)
            p = _xattn_probs(q_ref[:, lo:hi], kv_ref[:, lo:hi])
            o_ref[:, lo:hi] = _dot(p.astype(BF16), kv_ref[:, d + lo:d + hi]).astype(BF16)

    blk = pl.BlockSpec((tq, d), lambda i: (i, 0))
    return pl.pallas_call(body, name=name, grid=(s // tq,),
                          in_specs=[blk, pl.BlockSpec((mlen, 2 * d), lambda i: (0, 0))], out_specs=blk,
                          out_shape=jax.ShapeDtypeStruct((s, d), BF16), compiler_params=_params(("parallel",)))(q, kv)


def _xattn_bwd(name, q, kv, do, tq):
    s, d = q.shape
    mlen = kv.shape[0]

    def body(q_ref, kv_ref, do_ref, dq_ref, dkv_ref):
        @pl.when(pl.program_id(0) == 0)
        def _():
            dkv_ref[...] = jnp.zeros_like(dkv_ref)

        for h in range(4):
            lo, hi = 256 * h, 256 * (h + 1)
            qh = q_ref[:, lo:hi]
            kh = kv_ref[:, lo:hi]
            vh = kv_ref[:, d + lo:d + hi]
            doh = do_ref[:, lo:hi]
            p = _xattn_probs(qh, kh)
            dp = _dot(doh, vh, NT)
            dsb = (p * (dp - jnp.sum(p * dp, axis=-1, keepdims=True)) * (256 ** -0.5)).astype(BF16)
            dq_ref[:, lo:hi] = _dot(dsb, kh).astype(BF16)
            dkv_ref[:, lo:hi] += _dot(dsb, qh, TN)
            dkv_ref[:, d + lo:d + hi] += _dot(p.astype(BF16), doh, TN)

    blk = pl.BlockSpec((tq, d), lambda i: (i, 0))
    kvb = pl.BlockSpec((mlen, 2 * d), lambda i: (0, 0))
    return pl.pallas_call(
        body, name=name, grid=(s // tq,), in_specs=[blk, kvb, blk], out_specs=[blk, kvb],
        out_shape=[jax.ShapeDtypeStruct((s, d), BF16), jax.ShapeDtypeStruct((mlen, 2 * d), F32)],
        compiler_params=_params(("arbitrary",)))(q, kv, do)


def _sigmoid(x):
    return 0.5 * (1.0 + jnp.tanh(0.5 * x))


def _ffn_up(name, h, wgu, layer, tm, tn):
    s, d = h.shape
    dff = wgu.shape[2] // 2
    nj = dff // tn

    def body(h_ref, wg_ref, wu_ref, g_ref, u_ref, a_ref):
        hv = h_ref[...]
        gate = _dot(hv, wg_ref[...])
        up = _dot(hv, wu_ref[...])
        g_ref[...] = gate.astype(BF16)
        u_ref[...] = up.astype(BF16)
        a_ref[...] = (gate * _sigmoid(gate) * up).astype(BF16)

    ob = pl.BlockSpec((tm, tn), lambda j, i: (i, j))
    sd = jax.ShapeDtypeStruct((s, dff), BF16)
    return pl.pallas_call(
        body, name=name, grid=(nj, s // tm),
        in_specs=[pl.BlockSpec((tm, d), lambda j, i: (i, 0)),
                  pl.BlockSpec((None, d, tn), lambda j, i: (layer, 0, j)),
                  pl.BlockSpec((None, d, tn), lambda j, i: (layer, 0, j + nj))],
        out_specs=[ob, ob, ob], out_shape=[sd, sd, sd], compiler_params=_params(("parallel", "parallel")),
    )(h, wgu, wgu)


def _ffn_act_bwd(name, dfn, wdown, layer, gate, up, tm):
    s, d = dfn.shape
    dff = gate.shape[1]

    tn = 256

    def body(df_ref, wd_ref, g_ref, u_ref, o_ref):
        df = df_ref[...]
        for lo in range(0, dff, tn):
            dact = _dot(df, wd_ref[lo:lo + tn, :], NT).astype(BF16)
            gate = g_ref[:, lo:lo + tn]
            sig = _sigmoid(gate)
            gs = gate * sig
            o_ref[:, lo:lo + tn] = dact * u_ref[:, lo:lo + tn] * (sig + gs * (1.0 - sig))
            o_ref[:, dff + lo:dff + lo + tn] = dact * gs

    gb = pl.BlockSpec((tm, dff), lambda i: (i, 0))
    return pl.pallas_call(
        body, name=name, grid=(s // tm,),
        in_specs=[pl.BlockSpec((tm, d), lambda i: (i, 0)),
                  pl.BlockSpec((None, dff, d), lambda i: (layer, 0, 0)), gb, gb],
        out_specs=pl.BlockSpec((tm, 2 * dff), lambda i: (i, 0)),
        out_shape=jax.ShapeDtypeStruct((s, 2 * dff), BF16), compiler_params=_params(("parallel",)),
    )(dfn, wdown, gate, up)


def _place():
    return lax.axis_index("x"), lax.axis_index("y"), lax.axis_index("c")


def _other_chips(x, y):
    return [(1 - x, y), (x, 1 - y), (1 - x, 1 - y)]


def _region(ref, axis, chip, size):
    start = pl.multiple_of(chip * size, size)
    if axis == 1:
        return ref.at[:, pl.ds(start, size), :]
    return ref.at[:, :, pl.ds(start, size)]


ANY = pl.BlockSpec(memory_space=pl.ANY)


HBM = pl.BlockSpec(memory_space=pltpu.HBM)
SEM = pl.BlockSpec(memory_space=pltpu.SEMAPHORE)
EFFECT = pltpu.SideEffectType.DATAFLOW_SIDE_EFFECTING


def _in_hbm(a):
    return pltpu.with_memory_space_constraint(a, pltpu.HBM)


def _split_start(name, srcs, lands, ncopies, plan):
    ns, nl = len(srcs), len(lands)

    def body(*refs):
        src, land = refs[:ns], refs[ns:ns + nl]
        send, recv = refs[ns + nl], refs[ns + nl + 1]
        token = refs[-1]
        x, y, c = _place()
        for k, (s_ref, d_ref, peer, _) in enumerate(plan(src, land, x, y, c)):
            pltpu.make_async_remote_copy(src_ref=s_ref, dst_ref=d_ref, send_sem=send.at[k], recv_sem=recv.at[k],
                                         device_id=peer, device_id_type=MESH).start()
        token[...] = jnp.zeros_like(token)

    ops = list(srcs) + list(lands)
    out = pl.pallas_call(
        body, name=name,
        out_shape=(pltpu.SemaphoreType.DMA((ncopies,)), pltpu.SemaphoreType.DMA((ncopies,)),
                   *[pltpu.HBM(a.shape, a.dtype) for a in ops], jax.ShapeDtypeStruct((8, 128), F32)),
        in_specs=(HBM,) * (ns + nl),
        out_specs=(SEM, SEM) + (HBM,) * (ns + nl) + (pl.BlockSpec(memory_space=pltpu.VMEM),),
        input_output_aliases={i: 2 + i for i in range(ns + nl)},
        compiler_params=pltpu.CompilerParams(has_side_effects=EFFECT),
    )(*[_in_hbm(a) for a in ops])
    return out[0], out[1], list(out[2:2 + ns]), list(out[2 + ns:2 + ns + nl]), out[-1]


def _split_start_many(name, lands, jobs):
    nl, nj = len(lands), len(jobs)

    def body(*refs):
        land = refs[:nl]
        sems = refs[nl:nl + 2 * nj]
        token = refs[-1]
        x, y, c = _place()
        for j, (idx, _, plan) in enumerate(jobs):
            for k, (s_ref, d_ref, peer, _) in enumerate(plan((), [land[t] for t in idx], x, y, c)):
                pltpu.make_async_remote_copy(src_ref=s_ref, dst_ref=d_ref, send_sem=sems[2 * j].at[k],
                                             recv_sem=sems[2 * j + 1].at[k], device_id=peer,
                                             device_id_type=MESH).start()
        token[...] = jnp.zeros_like(token)

    sem_shapes = tuple(pltpu.SemaphoreType.DMA((n,)) for _, n, _ in jobs for _ in range(2))
    out = pl.pallas_call(
        body, name=name,
        out_shape=sem_shapes + tuple(pltpu.HBM(a.shape, a.dtype) for a in lands)
        + (jax.ShapeDtypeStruct((8, 128), F32),),
        in_specs=(HBM,) * nl,
        out_specs=(SEM,) * (2 * nj) + (HBM,) * nl + (pl.BlockSpec(memory_space=pltpu.VMEM),),
        input_output_aliases={i: 2 * nj + i for i in range(nl)},
        compiler_params=pltpu.CompilerParams(has_side_effects=EFFECT),
    )(*[_in_hbm(a) for a in lands])
    return [(out[2 * j], out[2 * j + 1]) for j in range(nj)], list(out[2 * nj:2 * nj + nl]), out[-1]


def _split_wait(name, send, recv, srcs, lands, after, plan):
    ns, nl = len(srcs), len(lands)

    def body(*refs):
        src, land = refs[:ns], refs[ns:ns + nl]
        send_ref, recv_ref = refs[ns + nl], refs[ns + nl + 1]
        x, y, c = _place()
        for k, (s_ref, _, _, got) in enumerate(plan(src, land, x, y, c)):
            cp = pltpu.make_async_remote_copy(src_ref=s_ref, dst_ref=got, send_sem=send_ref.at[k],
                                              recv_sem=recv_ref.at[k], device_id=(x, y, c), device_id_type=MESH)
            cp.wait_send()
            cp.wait_recv()

    ops = list(srcs) + list(lands)
    out = pl.pallas_call(
        body, name=name, out_shape=tuple(pltpu.HBM(a.shape, a.dtype) for a in ops),
        in_specs=(HBM,) * (ns + nl) + (SEM, SEM, ANY), out_specs=(HBM,) * (ns + nl),
        input_output_aliases={i: i for i in range(ns + nl)},
        compiler_params=pltpu.CompilerParams(has_side_effects=EFFECT),
    )(*ops, send, recv, after)
    return list(out[:ns]), list(out[ns:])


def _split_wait_start(name, send, recv, lands, after, wait_plan, ncopies, start_plan, carried=()):
    nl, nc = len(lands), len(carried)
    lands = list(lands) + list(carried)

    def body(*refs):
        land = refs[:nl]
        send_in, recv_in = refs[nl + nc], refs[nl + nc + 1]
        send_out, recv_out = refs[nl + nc + 3], refs[nl + nc + 4]
        x, y, c = _place()
        for k, (s_ref, _, _, got) in enumerate(wait_plan((), land, x, y, c)):
            cp = pltpu.make_async_remote_copy(src_ref=s_ref, dst_ref=got, send_sem=send_in.at[k],
                                              recv_sem=recv_in.at[k], device_id=(x, y, c), device_id_type=MESH)
            cp.wait_send()
            cp.wait_recv()
        for k, (s_ref, d_ref, peer, _) in enumerate(start_plan((), land, x, y, c)):
            pltpu.make_async_remote_copy(src_ref=s_ref, dst_ref=d_ref, send_sem=send_out.at[k],
                                         recv_sem=recv_out.at[k], device_id=peer, device_id_type=MESH).start()

    out = pl.pallas_call(
        body, name=name,
        out_shape=(pltpu.SemaphoreType.DMA((ncopies,)), pltpu.SemaphoreType.DMA((ncopies,)),
                   *[pltpu.HBM(a.shape, a.dtype) for a in lands]),
        in_specs=(HBM,) * (nl + nc) + (SEM, SEM, ANY), out_specs=(SEM, SEM) + (HBM,) * (nl + nc),
        input_output_aliases={i: 2 + i for i in range(nl + nc)},
        compiler_params=pltpu.CompilerParams(has_side_effects=EFFECT),
    )(*lands, send, recv, after)
    return out[0], out[1], list(out[2:2 + nl]), list(out[2 + nl:])


def _half(ref, axis, chip, size, layer, h):
    reg = _region(ref, axis, chip, size).at[pl.ds(layer, 1)]
    rows = reg.shape[1] // 2
    return reg.at[:, pl.ds(pl.multiple_of(h * rows, rows), rows), :]


def _gather_plan(axes, sizes, layer):
    def plan(src, land, x, y, c):
        me = 2 * x + y
        out = []
        for t in range(len(land)):
            mine = _half(land[t], axes[t], me, sizes[t], layer, c)
            for px, py in _other_chips(x, y):
                out.append((mine, mine, (px, py, c), _half(land[t], axes[t], 2 * px + py, sizes[t], layer, c)))
        return out
    return plan


def _forward_plan(axes, sizes, layer):
    def plan(src, land, x, y, c):
        out = []
        for t in range(len(land)):
            for px, py in _other_chips(x, y):
                got = _half(land[t], axes[t], 2 * px + py, sizes[t], layer, c)
                out.append((got, got, (x, y, 1 - c), _half(land[t], axes[t], 2 * px + py, sizes[t], layer, 1 - c)))
        return out
    return plan


def _place_own(name, w, axis, chip):
    nl, r, cs = w.shape
    tr = _rows_tile(r, cs)
    nb = r // tr
    full = (nl, 4 * r, cs) if axis == 1 else (nl, r, 4 * cs)

    def body(m_ref, w_ref, o_ref):
        o_ref[...] = w_ref[...].astype(BF16)

    if axis == 1:
        ospec = pl.BlockSpec((None, tr, cs), lambda l, i, m: (l, m[0] * nb + i, 0))
    else:
        ospec = pl.BlockSpec((None, tr, cs), lambda l, i, m: (l, i, m[0]))
    return pl.pallas_call(
        body, name=name,
        grid_spec=pltpu.PrefetchScalarGridSpec(
            num_scalar_prefetch=1, grid=(nl, nb),
            in_specs=[pl.BlockSpec((None, tr, cs), lambda l, i, m: (l, i, 0))], out_specs=ospec),
        out_shape=jax.ShapeDtypeStruct(full, BF16), compiler_params=_params(("parallel", "parallel")),
    )(chip, w)


def _scatter_plan(axes, sizes):
    def plan(src, land, x, y, c):
        out = []
        for t in range(len(src)):
            for k, (px, py) in enumerate(_other_chips(x, y)):
                out.append((_region(src[t], axes[t], 2 * px + py, sizes[t]).at[0], land[t].at[k], (px, py, c),
                            land[t].at[k]))
        return out
    return plan


def _pair_plan(src, land, x, y, c):
    return [(src[t], land[t], (x, y, 1 - c), land[t]) for t in range(len(src))]


UNIT_STEPS = 16


def _unit_rows(r):
    return min(t for t in range(16, r + 1, 16) if r % t == 0 and r // t <= UNIT_STEPS)


def _guarded(i, nb, steps, work):
    if nb == steps:
        work()
    else:
        pl.when(i < nb)(work)


def _unit_chip_sum(name, gs, slots, axes, chip):
    n = len(gs)
    dims = [s.shape[1:] for s in slots]
    trs = [_unit_rows(r) for r, _ in dims]
    nbs = [r // tr for (r, _), tr in zip(dims, trs)]
    steps = max(nbs)

    def body(m_ref, *refs):
        i = pl.program_id(0)
        for t in range(n):
            def work(t=t):
                acc = refs[t][...].astype(F32)
                for k in range(3):
                    acc = acc + refs[n + t][k].astype(F32)
                refs[2 * n + t][...] = acc.astype(BF16)
            _guarded(i, nbs[t], steps, work)

    gspecs, sspecs, ospecs = [], [], []
    for (r, cs), tr, nb, axis in zip(dims, trs, nbs, axes):
        if axis == 1:
            gspecs.append(pl.BlockSpec((tr, cs), lambda i, m, nb=nb: (m[0] * nb + jnp.minimum(i, nb - 1), 0)))
        else:
            gspecs.append(pl.BlockSpec((tr, cs), lambda i, m, nb=nb: (jnp.minimum(i, nb - 1), m[0])))
        sspecs.append(pl.BlockSpec((3, tr, cs), lambda i, m, nb=nb: (0, jnp.minimum(i, nb - 1), 0)))
        ospecs.append(pl.BlockSpec((tr, cs), lambda i, m, nb=nb: (jnp.minimum(i, nb - 1), 0)))
    return pl.pallas_call(
        body, name=name,
        grid_spec=pltpu.PrefetchScalarGridSpec(num_scalar_prefetch=1, grid=(steps,), in_specs=gspecs + sspecs,
                                               out_specs=ospecs),
        out_shape=[jax.ShapeDtypeStruct(d, BF16) for d in dims], compiler_params=_params(("arbitrary",)),
    )(chip, *gs, *slots)


def _unit_adamw(name, mine, theirs, ws, ms, vs, layer, bufs):
    n = len(mine)
    dims = [a.shape for a in mine]
    trs = [_unit_rows(r) for r, _ in dims]
    nbs = [r // tr for (r, _), tr in zip(dims, trs)]
    steps = max(nbs)
    c1 = 1.0 - B1 ** STEP
    c2 = 1.0 - B2 ** STEP

    def body(*refs):
        i = pl.program_id(0)
        outs = refs[9 * n:]
        for t in range(n):
            def work(t=t):
                a_ref, b_ref, w_ref, m_ref, v_ref = refs[5 * t:5 * t + 5]
                g_ref, d_ref, mo_ref, vo_ref = outs[4 * t:4 * t + 4]
                gv = a_ref[...].astype(F32) + b_ref[...].astype(F32)
                mn = B1 * m_ref[...] + (1.0 - B1) * gv
                vn = B2 * v_ref[...] + (1.0 - B2) * (gv * gv)
                g_ref[...] = gv
                mo_ref[...] = mn
                vo_ref[...] = vn
                d_ref[...] = -LR * ((mn / c1) / (jnp.sqrt(vn / c2) + ADAM_EPS) + WD * w_ref[...])
            _guarded(i, nbs[t], steps, work)

    in_specs, out_specs, ops = [], [], []
    for t, ((r, cs), tr, nb) in enumerate(zip(dims, trs, nbs)):
        blk = pl.BlockSpec((tr, cs), lambda i, nb=nb: (jnp.minimum(i, nb - 1), 0))
        lay = pl.BlockSpec((None, tr, cs), lambda i, nb=nb: (layer, jnp.minimum(i, nb - 1), 0))
        in_specs += [blk, blk, lay, lay, lay]
        out_specs += [lay] * 4
        ops += [mine[t], theirs[t], ws[t], ms[t], vs[t]]
    flat = [b for bs in bufs for b in bs]
    out = pl.pallas_call(
        body, name=name, grid=(steps,), in_specs=in_specs + [ANY] * (4 * n), out_specs=out_specs,
        out_shape=[jax.ShapeDtypeStruct(b.shape, b.dtype) for b in flat],
        input_output_aliases={5 * n + k: k for k in range(4 * n)}, compiler_params=_params(("arbitrary",)),
    )(*ops, *flat)
    return [list(out[4 * t:4 * t + 4]) for t in range(n)]


def _small_pair_sum(p):
    rows = p.shape[0]
    half = rows // 2

    def body(p_ref, o_ref, sib, send, recv):
        x, y, c = _place()
        mine = pl.ds(pl.multiple_of(c * half, half), half)
        theirs = pl.ds(pl.multiple_of((1 - c) * half, half), half)
        pair = pltpu.make_async_remote_copy(src_ref=p_ref.at[theirs], dst_ref=sib, send_sem=send, recv_sem=recv,
                                            device_id=(x, y, 1 - c), device_id_type=MESH)
        pair.start()
        pair.wait()
        o_ref[...] = (p_ref[mine] + sib[...]).astype(BF16)

    vm = pl.BlockSpec(memory_space=pltpu.VMEM)
    return pl.pallas_call(
        body, name="small_pair_sum", in_specs=[vm], out_specs=vm, out_shape=jax.ShapeDtypeStruct((half, 128), BF16),
        scratch_shapes=[pltpu.VMEM((half, 128), F32), pltpu.SemaphoreType.DMA, pltpu.SemaphoreType.DMA],
        compiler_params=pltpu.CompilerParams(vmem_limit_bytes=VMEM_LIMIT),
    )(p)


def _small_plan(src, land, x, y, c):
    return [(src[0], land[0].at[k], (px, py, c), land[0].at[k]) for k, (px, py) in enumerate(_other_chips(x, y))]


def _small_total(own, slots):
    half = own.shape[0]

    def body(own_ref, slots_ref, o_ref, sums, send, recv):
        x, y, c = _place()
        me = 2 * x + y
        mine = pl.ds(pl.multiple_of(c * half, half), half)
        theirs = pl.ds(pl.multiple_of((1 - c) * half, half), half)
        sums[me] = own_ref[...]
        for k, (px, py) in enumerate(_other_chips(x, y)):
            sums[2 * px + py] = slots_ref[k]
        acc = sums[0].astype(F32)
        for k in range(1, 4):
            acc = acc + sums[k].astype(F32)
        o_ref[mine] = acc
        back = pltpu.make_async_remote_copy(src_ref=o_ref.at[mine], dst_ref=o_ref.at[mine], send_sem=send,
                                            recv_sem=recv, device_id=(x, y, 1 - c), device_id_type=MESH)
        back.start()
        pltpu.make_async_remote_copy(src_ref=o_ref.at[theirs], dst_ref=o_ref.at[theirs], send_sem=send,
                                     recv_sem=recv, device_id=(x, y, c), device_id_type=MESH).wait_recv()
        back.wait_send()

    vm = pl.BlockSpec(memory_space=pltpu.VMEM)
    return pl.pallas_call(
        body, name="small_total", in_specs=[vm, vm], out_specs=vm,
        out_shape=jax.ShapeDtypeStruct((2 * half, 128), F32),
        scratch_shapes=[pltpu.VMEM((4, half, 128), BF16), pltpu.SemaphoreType.DMA, pltpu.SemaphoreType.DMA],
        compiler_params=pltpu.CompilerParams(vmem_limit_bytes=VMEM_LIMIT),
    )(own, slots)


def _small_adamw(gs, ws, ms, vs):
    n = len(gs)
    c1 = 1.0 - B1 ** STEP
    c2 = 1.0 - B2 ** STEP

    def body(*refs):
        for t in range(n):
            gv, wv = refs[t][...], refs[n + t][...]
            d_ref, mo_ref, vo_ref = refs[4 * n + 3 * t:4 * n + 3 * t + 3]
            mn = B1 * refs[2 * n + t][...] + (1.0 - B1) * gv
            vn = B2 * refs[3 * n + t][...] + (1.0 - B2) * (gv * gv)
            mo_ref[...] = mn
            vo_ref[...] = vn
            d_ref[...] = -LR * ((mn / c1) / (jnp.sqrt(vn / c2) + ADAM_EPS) + WD * wv)

    vm = pl.BlockSpec(memory_space=pltpu.VMEM)
    out = pl.pallas_call(
        body, name="adamw_small", in_specs=[vm] * (4 * n), out_specs=[vm] * (3 * n),
        out_shape=[jax.ShapeDtypeStruct(w.shape, F32) for w in ws for _ in range(3)],
        compiler_params=pltpu.CompilerParams(vmem_limit_bytes=VMEM_LIMIT),
    )(*gs, *ws, *ms, *vs)
    return [tuple(out[3 * t:3 * t + 3]) for t in range(n)]


def _pack(parts):
    flat = []
    for p in parts:
        v = p.reshape(-1).astype(F32)
        flat.append(jnp.pad(v, (0, (-v.shape[0]) % 128)))
    v = jnp.concatenate(flat)
    v = jnp.pad(v, (0, (-v.shape[0]) % (512 * 128)))
    return v.reshape(-1, 128)


def _unpack(buf, like):
    out, r0 = [], 0
    for p in like:
        nelem = 1
        for s in p.shape:
            nelem *= s
        rows = -(-nelem // 128)
        blk = buf[r0:r0 + rows]
        if nelem % 128:
            blk = blk.reshape(-1)[:nelem]
        out.append(blk.reshape(p.shape))
        r0 += rows
    return out


def kernel(x, mem, positions, mem_norm_g, mix_pre_g, mix_post_g, w_in, gm_v_g, gm_w_s, gm_b_s, pool_w, pool_scale, attn_sinks, w_o, x_pre_g, x_post_g, w_xq, w_xkv, w_xo, ffn_pre_g, ffn_post_g, w_gate_up, w_down, loss_target, m_mem_norm_g, m_mix_pre_g, m_mix_post_g, m_w_in, m_gm_v_g, m_gm_w_s, m_gm_b_s, m_pool_w, m_pool_scale, m_attn_sinks, m_w_o, m_x_pre_g, m_x_post_g, m_w_xq, m_w_xkv, m_w_xo, m_ffn_pre_g, m_ffn_post_g, m_w_gate_up, m_w_down, v_mem_norm_g, v_mix_pre_g, v_mix_post_g, v_w_in, v_gm_v_g, v_gm_w_s, v_gm_b_s, v_pool_w, v_pool_scale, v_attn_sinks, v_w_o, v_x_pre_g, v_x_post_g, v_w_xq, v_w_xkv, v_w_xo, v_ffn_pre_g, v_ffn_post_g, v_w_gate_up, v_w_down):
    args = (x, mem, positions, mem_norm_g, mix_pre_g, mix_post_g, w_in, gm_v_g, gm_w_s, gm_b_s, pool_w, pool_scale, attn_sinks, w_o, x_pre_g, x_post_g, w_xq, w_xkv, w_xo, ffn_pre_g, ffn_post_g, w_gate_up, w_down)
    moms_m = (m_mem_norm_g, m_mix_pre_g, m_mix_post_g, m_w_in, m_gm_v_g, m_gm_w_s, m_gm_b_s, m_pool_w, m_pool_scale, m_attn_sinks, m_w_o, m_x_pre_g, m_x_post_g, m_w_xq, m_w_xkv, m_w_xo, m_ffn_pre_g, m_ffn_post_g, m_w_gate_up, m_w_down)
    moms_v = (v_mem_norm_g, v_mix_pre_g, v_mix_post_g, v_w_in, v_gm_v_g, v_gm_w_s, v_gm_b_s, v_pool_w, v_pool_scale, v_attn_sinks, v_w_o, v_x_pre_g, v_x_post_g, v_w_xq, v_w_xkv, v_w_xo, v_ffn_pre_g, v_ffn_post_g, v_w_gate_up, v_w_down)
    P = dict(zip(NAMES, args))
    P['loss_target'] = loss_target
    M = dict(zip(WEIGHTS, moms_m))
    V = dict(zip(WEIGHTS, moms_v))
    depth = w_in.shape[0]
    nbig = len(BIG)
    axes = [BIG_AXIS[n] for n in BIG]
    sizes = [P[n].shape[a] for n, a in zip(BIG, axes)]
    chip = (2 * lax.axis_index("x") + lax.axis_index("y")).astype(jnp.int32).reshape(1)

    groups = [['w_in'], ['w_o', 'w_xq', 'w_xkv', 'w_xo'], ['w_gate_up', 'w_down']]
    units = [(l, g) for l in range(depth) for g in groups]
    unit_of = {(l, n): i for i, (l, names) in enumerate(units) for n in names}
    ax = lambda names: [BIG_AXIS[n] for n in names]
    sz = lambda names: [P[n].shape[BIG_AXIS[n]] for n in names]

    full = {n: _place_own("place_" + n, P[n], BIG_AXIS[n], chip) for n in BIG}
    gathers, land, tok = _split_start_many(
        "gather_start", [full[n] for n in BIG],
        [([BIG.index(n) for n in names], 3 * len(names), _gather_plan(ax(names), sz(names), l)) for l, names in units])
    full.update(zip(BIG, land))
    P['first_dep'] = tok[:1, :1]
    forwards, gathered = {}, set()

    def forward_unit(i, after, carried=()):
        ul, unames = units[i]
        send, recv = gathers[i]
        send, recv, land, thru = _split_wait_start(
            "gather_pass%d" % i, send, recv, [full[n] for n in unames], after,
            _gather_plan(ax(unames), sz(unames), ul), 3 * len(unames), _forward_plan(ax(unames), sz(unames), ul),
            carried=[full[n] for n in carried])
        full.update(zip(unames, land))
        full.update(zip(carried, thru))
        forwards[i] = (send, recv)

    def weights_of(l, names, after):
        i = unit_of[(l, names[0])]
        if i not in gathered:
            _, unames = units[i]
            if i not in forwards:
                forward_unit(i, after)
            send, recv = forwards.pop(i)
            _, land = _split_wait("gather_wait%d" % i, send, recv, [], [full[n] for n in unames], after,
                                  _forward_plan(ax(unames), sz(unames), l))
            full.update(zip(unames, land))
            gathered.add(i)
            if len(groups) <= i + 1 < len(units):
                forward_unit(i + 1, after, carried=[n for n in unames if n not in units[i + 1][1]])
        return {n: (full[n], l) for n in names}

    outs = {n: [lax.empty(P[n].shape, F32) for _ in range(4)] for n in BIG}
    gunits = [(l, BIG) for l in range(depth - 1, 0, -1)] + [
        (0, g) for g in (['w_gate_up', 'w_down'], ['w_xq', 'w_xkv', 'w_xo'], ['w_o'], ['w_in'])]
    collected, scatters, pairs = {}, {}, {}

    def finish_scatter(i, after):
        _, names = gunits[i]
        send, recv, g_l, slots = scatters.pop(i)
        g_l, slots = _split_wait("scatter_wait%d" % i, send, recv, g_l, slots, after,
                                 _scatter_plan(ax(names), sz(names)))
        mine = _unit_chip_sum("chip_sum", [g.reshape(g.shape[1:]) for g in g_l], slots, ax(names), chip)
        send, recv, mine, theirs, tok = _split_start("pair_start%d" % i, mine, [lax.empty(a.shape, BF16) for a in mine],
                                                     len(names), _pair_plan)
        pairs[i] = (send, recv, mine, theirs)
        return tok[:1, :1]

    def finish_pair(i, after):
        l, names = gunits[i]
        send, recv, mine, theirs = pairs.pop(i)
        mine, theirs = _split_wait("pair_wait%d" % i, send, recv, mine, theirs, after, _pair_plan)
        new = _unit_adamw("adamw", mine, theirs, [P[n] for n in names], [M[n] for n in names], [V[n] for n in names],
                          l, [outs[n] for n in names])
        outs.update(zip(names, new))

    calls = {'n': 0}
    lag = 4

    def grads_of(l, g_part, after):
        collected.update({(l, n): g for n, g in g_part.items()})
        calls['n'] += 1
        now = calls['n']
        tok = jnp.zeros((1, 1), F32)
        for i, (ul, names) in enumerate(gunits):
            if ul != l or ('started', i) in collected or any((l, n) not in collected for n in names):
                continue
            collected[('started', i)] = now
            srcs = [collected[(l, n)].reshape((1,) + collected[(l, n)].shape) for n in names]
            send, recv, srcs, slots, t = _split_start("scatter_start%d" % i, srcs,
                                                      [lax.empty((3,) + P[n].shape[1:], BF16) for n in names],
                                                      3 * len(names), _scatter_plan(ax(names), sz(names)))
            scatters[i] = (send, recv, srcs, slots)
            tok = tok + t[:1, :1]
        for i in sorted(pairs):
            if collected[('summed', i)] + lag <= now:
                finish_pair(i, after)
        for i in sorted(scatters):
            if collected[('started', i)] + lag <= now:
                tok = tok + finish_scatter(i, after)
                collected[('summed', i)] = now
        return tok

    loss_part, dx, small_g = _fwd_bwd(P, weights_of, grads_of)
    loss = lax.psum(loss_part[0, 0], ("x", "y", "c"))
    grad_x = dx.reshape(x.shape)

    small_like = [P[n] for n in SMALL]
    half_sum = _small_pair_sum(_pack(small_g))
    send, recv, (half_sum,), (slots,), _ = _split_start(
        "small_start", [half_sum], [lax.empty((3,) + half_sum.shape, BF16)], 3, _small_plan)

    for i in sorted(pairs):
        finish_pair(i, half_sum)
    for i in sorted(scatters):
        finish_scatter(i, half_sum)
    for i in sorted(pairs):
        finish_pair(i, half_sum)
    grads, deltas, new_m, new_v = {}, {}, {}, {}
    for n in BIG:
        grads[n], deltas[n], new_m[n], new_v[n] = outs[n]

    (half_sum,), (slots,) = _split_wait("small_wait", send, recv, [half_sum], [slots], grads[BIG[0]], _small_plan)
    two_d = lambda a: a.reshape(-1, a.shape[-1])
    gs = _unpack(_small_total(half_sum, slots), small_like)
    upd = _small_adamw([two_d(g) for g in gs], [two_d(P[n]) for n in SMALL], [two_d(M[n]) for n in SMALL],
                       [two_d(V[n]) for n in SMALL])
    for n, g, (dlt, mn, vn) in zip(SMALL, gs, upd):
        shape = P[n].shape
        grads[n], deltas[n], new_m[n], new_v[n] = g, dlt.reshape(shape), mn.reshape(shape), vn.reshape(shape)

    return (loss, grad_x, *[grads[n] for n in WEIGHTS], *[deltas[n] for n in WEIGHTS],
            *[new_m[n] for n in WEIGHTS], *[new_v[n] for n in WEIGHTS])


def _fwd_bwd(P, weights_of, grads_of):
    (x, mem, positions, mem_norm_g, mix_pre_g, mix_post_g, w_in, gm_v_g, gm_w_s, gm_b_s, pool_w, pool_scale, attn_sinks,
     w_o, x_pre_g, x_post_g, w_xq, w_xkv, w_xo, ffn_pre_g, ffn_post_g, w_gate_up, w_down) = [P[n] for n in NAMES]
    x0 = x[0]
    s, d = x0.shape
    depth = w_in.shape[0]
    tgt = P['loss_target'][0]
    tmn = 256
    tmr = min(512, s)
    tmp = min(1024, s)
    tkw = min(2048, s)

    half = HEAD // 2
    inv = ROPE_THETA ** (-jnp.arange(half, dtype=F32) / half)
    ang = positions[0].astype(F32)[:, None] * inv
    cos, sin = jnp.cos(ang), jnp.sin(ang)
    cosq = jnp.tile(jnp.concatenate([cos, cos], axis=-1), (1, 2))
    sinq = jnp.tile(jnp.concatenate([-sin, sin], axis=-1), (1, 2))

    row = lambda a, l: a[l].reshape(1, -1)
    memn = _prenorm("mem_norm", mem[0], mem_norm_g.reshape(1, d), tmn)
    pw_all = jnp.where(_ones_bd() > 0, jnp.tile(pool_w.reshape(depth, 4 * HEAD, HEAD), (1, 1, 4)), 0.0)
    pw_bd = [pw_all[l] for l in range(depth)]

    saved = []
    xc = x0
    h = _prenorm("pre_norm0", x0, row(mix_pre_g, 0) + P['first_dep'], tmn)
    for l in range(depth):
        W = weights_of(l, ['w_in'], xc)
        sv = {'x0': xc, 'h1': h}
        z, = _mm_rows("fwd_w_in", h, *W['w_in'], 'nn', tm=tmp, rows_out=[F32], epilogue=_plain_rows)
        abc = _mixer_fwd("mixer_fwd", z, cosq, sinq, row(gm_v_g, l), gm_w_s[l], gm_b_s[l].T, pw_bd[l],
                         row(pool_scale, l), row(attn_sinks, l))
        W.update(weights_of(l, ['w_o'], z))
        mix, xc, h = _mm_rows("fwd_w_o", abc, *W['w_o'], 'nn', tm=tmr, rows_in=[xc],
                              params=[row(mix_post_g, l), row(x_pre_g, l)], rows_out=[BF16, F32, BF16],
                              epilogue=_post_pre_rows)
        sv.update(z=z, abc=abc, mix=mix, x1=xc, h2=h)
        W.update(weights_of(l, ['w_xq', 'w_xkv', 'w_xo'], xc))
        q, = _mm_rows("fwd_w_xq", h, *W['w_xq'], 'nn', tm=tmp, rows_out=[BF16], epilogue=_plain_rows)
        kv = _mm_nn("fwd_w_xkv", memn, *W['w_xkv'], tm=256, tn=512, tk=d, out_dtype=BF16)
        o = _xattn_fwd("xattn_fwd", q, kv, 512)
        xo, xc, h = _mm_rows("fwd_w_xo", o, *W['w_xo'], 'nn', tm=tmr, rows_in=[xc],
                             params=[row(x_post_g, l), row(ffn_pre_g, l)], rows_out=[BF16, F32, BF16],
                             epilogue=_post_pre_rows)
        sv.update(q=q, kv=kv, o=o, xo=xo, x2=xc, h3=h)
        W.update(weights_of(l, ['w_gate_up', 'w_down'], xc))
        dff = W['w_down'][0].shape[1]
        gate, up, act = _ffn_up("ffn_up", h, *W['w_gate_up'], 512, dff // 2)
        sv.update(gate=gate, up=up, act=act)
        if l + 1 < depth:
            f, xc, h = _mm_rows("fwd_w_down", act, *W['w_down'], 'nn', tm=tmr, rows_in=[xc],
                                params=[row(ffn_post_g, l), row(mix_pre_g, l + 1)], rows_out=[BF16, F32, BF16],
                                epilogue=_post_pre_rows)
            sv.update(f=f)
        saved.append(sv)
    gs = {n: [None] * depth for n in SMALL if n != 'mem_norm_g'}
    dx, dfn, gs['ffn_post_g'][depth - 1], loss_part = _mm_rows(
        "fwd_w_down_loss", saved[-1]['act'], *W['w_down'], 'nn', tm=tmr, rows_in=[xc, tgt],
        params=[row(ffn_post_g, depth - 1)], rows_out=[F32, BF16], n_sums=2, epilogue=_make_loss_rows(d))

    dmemn = None
    tok = jnp.zeros((1, 1), F32)
    for l in reversed(range(depth)):
        sv, W, G = saved[l], weights_of(l, BIG, dx), {}
        G['w_down'] = _mm_tn("dw_down", sv['act'], dfn, tm=dff // 2, tn=d, tk=tkw)
        dgu = _ffn_act_bwd("ffn_act_bwd", dfn, *W['w_down'], sv['gate'], sv['up'], 256)
        G['w_gate_up'] = _mm_tn("dw_gate_up", sv['h3'], dgu, tm=d, tn=dff // 2, tk=tkw)
        dx, dxo, gs['ffn_pre_g'][l], gs['x_post_g'][l] = _mm_rows(
            "bwd_w_gate_up", dgu, *W['w_gate_up'], 'nt', tm=tmr, rows_in=[sv['x2'], dx, sv['xo']],
            params=[row(ffn_pre_g, l) + tok, row(x_post_g, l)], rows_out=[F32, BF16], n_sums=2, epilogue=_bwd_rows)
        tok = grads_of(l, {n: G[n] for n in ('w_gate_up', 'w_down')}, dx)
        G['w_xo'] = _mm_tn("dw_xo", sv['o'], dxo, tm=d, tn=d, tk=tkw)
        do, = _mm_rows("bwd_w_xo", dxo, *W['w_xo'], 'nt', tm=tmp, params=[jnp.zeros((1, d), F32) + tok],
                       rows_out=[BF16], epilogue=_plain_rows)
        dq, dkv = _xattn_bwd("xattn_bwd", sv['q'], sv['kv'], do, 512)
        dkv = dkv.astype(BF16)
        G['w_xkv'] = _mm_tn("dw_xkv", memn, dkv, tm=d, tn=d, tk=mem.shape[1])
        dmemn = _mm_nt("bwd_w_xkv", dkv, *W['w_xkv'], tm=mem.shape[1], tn=512, tk=2 * d, out_dtype=F32, add=dmemn)
        G['w_xq'] = _mm_tn("dw_xq", sv['h2'], dq, tm=d, tn=d, tk=tkw)
        dx, dmix, gs['x_pre_g'][l], gs['mix_post_g'][l] = _mm_rows(
            "bwd_w_xq", dq, *W['w_xq'], 'nt', tm=tmr, rows_in=[sv['x1'], dx, sv['mix']],
            params=[row(x_pre_g, l), row(mix_post_g, l)], rows_out=[F32, BF16], n_sums=2, epilogue=_bwd_rows)
        tok = grads_of(l, {n: G[n] for n in ('w_xq', 'w_xkv', 'w_xo')}, dx)
        G['w_o'] = _mm_tn("dw_o", sv['abc'], dmix, tm=d, tn=d, tk=tkw)
        dabc, = _mm_rows("bwd_w_o", dmix, *W['w_o'], 'nt', tm=tmp, params=[jnp.zeros((1, d), F32) + tok],
                         rows_out=[F32], epilogue=_plain_rows)
        tok = grads_of(l, {'w_o': G['w_o']}, dabc)
        dz, dgv, dws, dbt, dpw, dpsc, dsnk = _mixer_bwd(
            "mixer_bwd", sv['z'], dabc, cosq, sinq, row(gm_v_g, l) + tok, gm_w_s[l], gm_b_s[l].T, pw_bd[l],
            row(pool_scale, l), row(attn_sinks, l))
        gs['gm_v_g'][l] = dgv
        gs['gm_w_s'][l] = dws
        gs['gm_b_s'][l] = dbt.T
        gs['pool_w'][l] = dpw
        gs['pool_scale'][l] = dpsc
        gs['attn_sinks'][l] = dsnk
        G['w_in'] = _mm_tn("dw_in", sv['h1'], dz, tm=d, tn=dz.shape[1], tk=tkw)
        if l > 0:
            dx, dfn, gs['mix_pre_g'][l], gs['ffn_post_g'][l - 1] = _mm_rows(
                "bwd_w_in", dz, *W['w_in'], 'nt', tm=tmr, rows_in=[sv['x0'], dx, saved[l - 1]['f']],
                params=[row(mix_pre_g, l), row(ffn_post_g, l - 1)], rows_out=[F32, BF16], n_sums=2,
                epilogue=_bwd_rows)
        else:
            dx, gs['mix_pre_g'][l] = _mm_rows(
                "bwd_w_in_first", dz, *W['w_in'], 'nt', tm=tmr, rows_in=[sv['x0'], dx],
                params=[row(mix_pre_g, l)], rows_out=[F32], n_sums=1, epilogue=_bwd_rows_first)
        tok = grads_of(l, {'w_in': G['w_in']}, dx)
    _, dg_mem = _norm_bwd("bwd_mem_norm", mem[0], mem_norm_g.reshape(1, d) + tok, dmemn, None, BF16, tmn)
    small_g = []
    for n in SMALL:
        if n == 'mem_norm_g':
            small_g.append(dg_mem.reshape(P[n].shape))
        elif n == 'pool_w':
            blocks = jnp.stack(gs[n]).reshape(depth, 4, HEAD, 4, HEAD)
            same = jnp.eye(4, dtype=jnp.bool_)[None, :, None, :, None]
            small_g.append(jnp.sum(jnp.where(same, blocks, 0.0), axis=3))
        else:
            small_g.append(jnp.stack([a.reshape(P[n].shape[1:]) for a in gs[n]]))
    return loss_part, dx, small_g
```

```python
import functools

import jax
import jax.numpy as jnp
from jax import lax
from jax.experimental import pallas as pl
from jax.experimental.pallas import tpu as pltpu

F32 = jnp.float32
BF16 = jnp.bfloat16
EPS = 1e-6
CHUNK = 128
HEAD = 64
ROPE_THETA = 10000.0
POOL_WINDOWS = (2, 4, 8, 16)
LR, B1, B2, ADAM_EPS, WD, STEP = 0.001, 0.9, 0.999, 1e-08, 0.01, 10
MESH = pl.DeviceIdType.MESH
VMEM_LIMIT = 56 * 1024 * 1024

NAMES = ['x', 'mem', 'positions', 'mem_norm_g', 'mix_pre_g', 'mix_post_g', 'w_in', 'gm_v_g', 'gm_w_s', 'gm_b_s',
         'pool_w', 'pool_scale', 'attn_sinks', 'w_o', 'x_pre_g', 'x_post_g', 'w_xq', 'w_xkv', 'w_xo', 'ffn_pre_g',
         'ffn_post_g', 'w_gate_up', 'w_down']
WEIGHTS = NAMES[3:]
BIG = ['w_in', 'w_o', 'w_xq', 'w_xkv', 'w_xo', 'w_gate_up', 'w_down']
BIG_AXIS = {'w_in': 2, 'w_o': 1, 'w_xq': 1, 'w_xkv': 2, 'w_xo': 1, 'w_gate_up': 2, 'w_down': 1}
SMALL = [n for n in WEIGHTS if n not in BIG]

NN = (((1,), (0,)), ((), ()))
NT = (((1,), (1,)), ((), ()))
TN = (((0,), (0,)), ((), ()))


def _dot(a, b, dims=NN):
    return lax.dot_general(a, b, dims, preferred_element_type=F32)


def _params(sem):
    return pltpu.CompilerParams(dimension_semantics=sem, vmem_limit_bytes=VMEM_LIMIT)


STREAM_BLOCK_BYTES = 3 * 512 * 1024


def _rows_tile(rows, cols):
    limit = max(16, STREAM_BLOCK_BYTES // (4 * cols))
    return max(t for t in range(16, min(rows, limit) + 1, 16) if rows % t == 0)


def _mm(name, a, a_spec, b, b_spec, dims, grid, nk, out_shape, out_spec, add=None, add_spec=None):
    acc_shape = out_spec.block_shape

    def body(*refs):
        a_ref, b_ref = refs[0], refs[1]
        pos = 2
        add_ref = None
        if add is not None:
            add_ref = refs[pos]
            pos += 1
        o_ref = refs[pos]
        part = _dot(a_ref[...].astype(BF16), b_ref[...].astype(BF16), dims)
        if nk == 1:
            if add_ref is not None:
                part = part + add_ref[...]
            o_ref[...] = part.astype(o_ref.dtype)
        else:
            acc_ref = refs[pos + 1]
            k = pl.program_id(2)

            @pl.when(k == 0)
            def _():
                acc_ref[...] = part if add_ref is None else part + add_ref[...]

            @pl.when(k > 0)
            def _():
                acc_ref[...] += part

            @pl.when(k == nk - 1)
            def _():
                o_ref[...] = acc_ref[...].astype(o_ref.dtype)

    ops, specs = [a, b], [a_spec, b_spec]
    if add is not None:
        ops.append(add)
        specs.append(add_spec)
    return pl.pallas_call(
        body, name=name, grid=grid, in_specs=specs, out_specs=out_spec, out_shape=out_shape,
        scratch_shapes=[pltpu.VMEM(acc_shape, F32)] if nk > 1 else [],
        compiler_params=_params(("parallel", "parallel", "arbitrary")),
    )(*ops)


def _wspec(block, layer, fn):
    return pl.BlockSpec((None,) + block, lambda i, j, k: (layer,) + fn(i, j, k))


def _mm_nn(name, a, w, layer, *, tm, tn, tk, out_dtype):
    m, kk = a.shape
    n = w.shape[2]
    tm = min(tm, m)
    nk = kk // tk
    return _mm(name, a, pl.BlockSpec((tm, tk), lambda i, j, k: (i, k)),
               w, _wspec((tk, tn), layer, lambda i, j, k: (k, j)), NN,
               (m // tm, n // tn, nk), nk, jax.ShapeDtypeStruct((m, n), out_dtype),
               pl.BlockSpec((tm, tn), lambda i, j, k: (i, j)))


def _mm_nt(name, a, w, layer, *, tm, tn, tk, out_dtype, add=None):
    m, kk = a.shape
    n = w.shape[1]
    tm = min(tm, m)
    nk = kk // tk
    ospec = pl.BlockSpec((tm, tn), lambda i, j, k: (i, j))
    return _mm(name, a, pl.BlockSpec((tm, tk), lambda i, j, k: (i, k)),
               w, _wspec((tn, tk), layer, lambda i, j, k: (j, k)), NT,
               (m // tm, n // tn, nk), nk, jax.ShapeDtypeStruct((m, n), out_dtype), ospec,
               add=add, add_spec=ospec if add is not None else None)


def _mm_tn(name, a, b, *, tm, tn, tk):
    kk, m = a.shape
    n = b.shape[1]
    tk = min(tk, kk)
    nk = kk // tk
    return _mm(name, a, pl.BlockSpec((tk, tm), lambda i, j, k: (k, i)),
               b, pl.BlockSpec((tk, tn), lambda i, j, k: (k, j)), TN,
               (m // tm, n // tn, nk), nk, jax.ShapeDtypeStruct((m, n), BF16),
               pl.BlockSpec((tm, tn), lambda i, j, k: (i, j)))


def _mm_rows(name, a, w, layer, mode, *, tm, rows_in=(), params=(), rows_out=(), n_sums=0, epilogue):
    m, kk = a.shape
    n = w.shape[2] if mode == 'nn' else w.shape[1]
    nr, npar, no = len(rows_in), len(params), len(rows_out)

    def body(*refs):
        a_ref, w_ref = refs[0], refs[1]
        rin = refs[2:2 + nr]
        par = refs[2 + nr:2 + nr + npar]
        outs = refs[2 + nr + npar:2 + nr + npar + no]
        sums = refs[2 + nr + npar + no:2 + nr + npar + no + n_sums]
        acc = _dot(a_ref[...], w_ref[...], NN if mode == 'nn' else NT)
        res, sm = epilogue(acc, [r[...] for r in rin], [p[...] for p in par])
        for r, v in zip(outs, res):
            r[...] = v.astype(r.dtype)

        @pl.when(pl.program_id(0) == 0)
        def _():
            for r in sums:
                r[...] = jnp.zeros_like(r)

        for r, v in zip(sums, sm):
            r[...] += v

    wblock = (None, kk, n) if mode == 'nn' else (None, n, kk)
    rowblk = pl.BlockSpec((tm, n), lambda i: (i, 0))
    one = pl.BlockSpec((1, n), lambda i: (0, 0))
    return pl.pallas_call(
        body, name=name, grid=(m // tm,),
        in_specs=[pl.BlockSpec((tm, kk), lambda i: (i, 0)),
                  pl.BlockSpec(wblock, lambda i: (layer, 0, 0), pipeline_mode=pl.Buffered(1))]
                 + [rowblk] * nr + [one] * npar,
        out_specs=[rowblk] * no + [one] * n_sums,
        out_shape=[jax.ShapeDtypeStruct((m, n), dt) for dt in rows_out] +
                  [jax.ShapeDtypeStruct((1, n), F32)] * n_sums,
        compiler_params=_params(("arbitrary",)),
    )(a, w, *rows_in, *params)


def _rstd(x):
    return lax.rsqrt(jnp.mean(x * x, axis=-1, keepdims=True) + EPS)


def _norm_back(xin, g, dy):
    r = _rstd(xin)
    xh = xin * r
    dyg = dy * g
    return r * (dyg - xh * jnp.mean(dyg * xh, axis=-1, keepdims=True)), jnp.sum(dy * xh, axis=0, keepdims=True)


def _plain_rows(acc, rows, pars):
    return [acc], []


def _post_pre_rows(y, rows, pars):
    xn = rows[0] + y * _rstd(y) * pars[0]
    return [y, xn, xn * _rstd(xn) * pars[1]], []


def _make_loss_rows(d):
    def fn(y, rows, pars):
        x, tgt = rows
        err = x + y * _rstd(y) * pars[0] - tgt
        dout = err * (1.0 / d)
        dy, dg = _norm_back(y, pars[0], dout)
        lsum = 0.5 * jnp.sum(jnp.mean(err * err, axis=-1, keepdims=True), axis=0, keepdims=True)
        return [dout, dy], [dg, jnp.broadcast_to(lsum, dg.shape)]
    return fn


def _bwd_rows(dh, rows, pars):
    xin, resid, yprev = rows
    dxa, dg_pre = _norm_back(xin, pars[0], dh)
    dx = resid + dxa
    dyp, dg_post = _norm_back(yprev.astype(F32), pars[1], dx)
    return [dx, dyp], [dg_pre, dg_post]


def _bwd_rows_first(dh, rows, pars):
    xin, resid = rows
    dxa, dg_pre = _norm_back(xin, pars[0], dh)
    return [resid + dxa], [dg_pre]


def _row(d):
    return pl.BlockSpec((1, d), lambda i: (0, 0))


def _prenorm(name, x, g, tm):
    m, d = x.shape

    def body(x_ref, g_ref, o_ref):
        xv = x_ref[...]
        o_ref[...] = (xv * _rstd(xv) * g_ref[...]).astype(BF16)

    blk = pl.BlockSpec((tm, d), lambda i: (i, 0))
    return pl.pallas_call(body, name=name, grid=(m // tm,), in_specs=[blk, _row(d)], out_specs=blk,
                          out_shape=jax.ShapeDtypeStruct((m, d), BF16), compiler_params=_params(("parallel",)))(x, g)


def _norm_bwd(name, xin, g, dy, resid, out_dtype, tm):
    m, d = xin.shape

    def body(*refs):
        if resid is None:
            x_ref, g_ref, dy_ref, dx_ref, dg_ref = refs
        else:
            x_ref, g_ref, dy_ref, r_ref, dx_ref, dg_ref = refs
        xv = x_ref[...]
        r = _rstd(xv)
        xh = xv * r
        dyv = dy_ref[...].astype(F32)
        dyg = dyv * g_ref[...]
        dx = r * (dyg - xh * jnp.mean(dyg * xh, axis=-1, keepdims=True))
        if resid is not None:
            dx = dx + r_ref[...]
        dx_ref[...] = dx.astype(dx_ref.dtype)

        @pl.when(pl.program_id(0) == 0)
        def _():
            dg_ref[...] = jnp.zeros_like(dg_ref)

        dg_ref[...] += jnp.sum(dyv * xh, axis=0, keepdims=True)

    blk = pl.BlockSpec((tm, d), lambda i: (i, 0))
    ops = [xin, g, dy] + ([] if resid is None else [resid])
    specs = [blk, _row(d), blk] + ([] if resid is None else [blk])
    return pl.pallas_call(
        body, name=name, grid=(m // tm,), in_specs=specs, out_specs=[blk, _row(d)],
        out_shape=[jax.ShapeDtypeStruct((m, d), out_dtype), jax.ShapeDtypeStruct((1, d), F32)],
        compiler_params=_params(("arbitrary",)))(*ops)


def _gelu_parts(x):
    c = 0.7978845608028654
    t = jnp.tanh(c * (x + 0.044715 * (x * x * x)))
    return 0.5 * x * (1.0 + t), t


def _gelu_grad(x, t):
    c = 0.7978845608028654
    return 0.5 * (1.0 + t) + 0.5 * x * (1.0 - t * t) * (c * (1.0 + 3.0 * 0.044715 * x * x))


def _rot_half(x):
    ax = x.ndim - 1
    w = x.shape[ax]
    lane = lax.broadcasted_iota(jnp.int32, x.shape, ax)
    return jnp.where((lane & 63) < 32, pltpu.roll(x, w - 32, ax), pltpu.roll(x, 32, ax))


def _group_mean(x, ones_bd):
    hi = x.astype(BF16)
    lo = (x - hi.astype(F32)).astype(BF16)
    return (_dot(hi, ones_bd) + _dot(lo, ones_bd)) * (1.0 / HEAD)


def _gating(gel, gv, ws_ref, bt, ones_bd, mix_s):
    u = gel[:, :256]
    v = gel[:, 256:]
    r = lax.rsqrt(_group_mean(v * v, ones_bd) + EPS)
    xh = v * r
    vn = (xh * gv).astype(BF16)
    row = lax.broadcasted_iota(jnp.int32, (CHUNK, CHUNK), 0)
    col = lax.broadcasted_iota(jnp.int32, (CHUNK, CHUNK), 1)
    causal = col <= row
    wcs = []
    for g in range(4):
        wc = jnp.where(causal, ws_ref[g], 0.0).astype(BF16)
        wcs.append(wc)
        mix_s[:, HEAD * g:HEAD * (g + 1)] = _dot(wc, vn[:, HEAD * g:HEAD * (g + 1)]) + bt[:, g:g + 1]
    return u, r, xh, vn, wcs, causal, mix_s[...]


def _lane_select(lane, vals):
    return jnp.where(lane < 64, vals[0], jnp.where(lane < 128, vals[1], jnp.where(lane < 192, vals[2], vals[3])))


def _pool_fwd(pc, pp, ci):
    ext = jnp.concatenate([pp, pc], axis=0)
    s2 = ext + pltpu.roll(ext, 1, 0)
    s4 = s2 + pltpu.roll(s2, 2, 0)
    s8 = s4 + pltpu.roll(s4, 4, 0)
    s16 = s8 + pltpu.roll(s8, 8, 0)
    t1 = ci * CHUNK + lax.broadcasted_iota(jnp.int32, (CHUNK, 1), 0) + 1
    lane = lax.broadcasted_iota(jnp.int32, (1, 256), 1)
    cnt = _lane_select(lane, [jnp.minimum(t1, w).astype(F32) for w in POOL_WINDOWS])
    ssel = _lane_select(lane, [s[CHUNK:] for s in (s2, s4, s8, s16)])
    return ssel / cnt - pc, cnt, lane


def _attn_prep(zc, zpkv, cc, sc, cp, sp, ci):
    q = zc[:, 768:1280]
    kc = zc[:, 1280:1408]
    vc = zc[:, 1408:1536]
    kp = zpkv[:, :128]
    vp = zpkv[:, 128:]
    qr = q * jnp.concatenate([cc] * 4, axis=1) + _rot_half(q) * jnp.concatenate([sc] * 4, axis=1)
    krc = kc * cc + _rot_half(kc) * sc
    krp = kp * cp + _rot_half(kp) * sp
    kband = jnp.concatenate([krp, krc], axis=0)
    vband = jnp.concatenate([vp, vc], axis=0)
    key = lax.broadcasted_iota(jnp.int32, (2 * CHUNK, 4 * CHUNK), 0)
    t = lax.broadcasted_iota(jnp.int32, (2 * CHUNK, 4 * CHUNK), 1) & (CHUNK - 1)
    valid = ((key < CHUNK) & (key > t) & (ci > 0)) | ((key >= CHUNK) & (key - CHUNK <= t))
    return qr, kband, vband, valid


SCALE = HEAD ** -0.5


def _stack_heads(x, base, hk):
    return jnp.concatenate([x[:, base + HEAD * (4 * hk + i):base + HEAD * (4 * hk + i + 1)] for i in range(4)], axis=0)


def _sink_row(snk, hk):
    lane = lax.broadcasted_iota(jnp.int32, (1, 4 * CHUNK), 1)
    s = [snk[:, 4 * hk + i:4 * hk + i + 1] for i in range(4)]
    return jnp.where(lane < CHUNK, s[0], jnp.where(lane < 2 * CHUNK, s[1], jnp.where(lane < 3 * CHUNK, s[2], s[3])))


def _group_probs(kh, q4, valid, sink4):
    s = jnp.where(valid, _dot(kh, q4, NT), -1e30)
    mx = jnp.maximum(jnp.max(s, axis=0, keepdims=True), sink4)
    e = jnp.exp(s - mx)
    es = jnp.exp(sink4 - mx)
    inv = 1.0 / (jnp.sum(e, axis=0, keepdims=True) + es)
    return e * inv, es * inv


def _mixer_specs(nb, rev):
    def cur(i):
        return nb - 1 - i if rev else i

    def prev(i):
        return jnp.maximum(cur(i) - 1, 0)

    full = lambda shape: pl.BlockSpec(shape, lambda i: (0,) * len(shape))
    specs = [
        pl.BlockSpec((CHUNK, 1536), lambda i: (cur(i), 0)),
        pl.BlockSpec((CHUNK, 256), lambda i: (prev(i), 2)),
        pl.BlockSpec((CHUNK, 256), lambda i: (prev(i), 5)),
        pl.BlockSpec((CHUNK, 128), lambda i: (cur(i), 0)),
        pl.BlockSpec((CHUNK, 128), lambda i: (cur(i), 0)),
        pl.BlockSpec((CHUNK, 128), lambda i: (prev(i), 0)),
        pl.BlockSpec((CHUNK, 128), lambda i: (prev(i), 0)),
        full((1, 256)), full((4, CHUNK, CHUNK)), full((CHUNK, 4)), full((256, 256)), full((1, 256)), full((1, 8)),
        full((256, 256)),
    ]
    return specs, cur


def _ones_bd():
    g = lax.broadcasted_iota(jnp.int32, (256, 256), 0) // HEAD == lax.broadcasted_iota(jnp.int32, (256, 256), 1) // HEAD
    return g.astype(BF16)


def _mixer_fwd(name, z, cosq, sinq, gv, ws, bt, pw, psc, snk):
    s = z.shape[0]
    nb = s // CHUNK
    specs, _ = _mixer_specs(nb, False)

    def body(zc_ref, zpp_ref, zpkv_ref, cq_ref, sq_ref, cp_ref, sp_ref, gv_ref, ws_ref, bt_ref, pw_ref, psc_ref,
             snk_ref, bd_ref, o_ref, mix_s):
        ci = pl.program_id(0)
        zc = zc_ref[...]
        gel, _ = _gelu_parts(zc[:, :512])
        u, _, _, _, _, _, mixed = _gating(gel, gv_ref[...], ws_ref, bt_ref[...], bd_ref[...], mix_s)
        o_ref[:, :256] = (u * mixed).astype(BF16)
        pp = jnp.where(ci > 0, zpp_ref[...], 0.0)
        pooled, _, _ = _pool_fwd(zc[:, 512:768], pp, ci)
        mp = _dot(pooled.astype(BF16), pw_ref[...].astype(BF16))
        o_ref[:, 256:512] = (mp * psc_ref[...]).astype(BF16)
        qr, kband, vband, valid = _attn_prep(zc, zpkv_ref[...], cq_ref[...], sq_ref[...], cp_ref[...], sp_ref[...], ci)
        snkv = snk_ref[...]
        kb = kband.astype(BF16)
        vt = vband.T
        ots = []
        for hk in range(2):
            q4 = (_stack_heads(qr, 0, hk) * SCALE).astype(BF16)
            p, _ = _group_probs(kb[:, HEAD * hk:HEAD * (hk + 1)], q4, valid, _sink_row(snkv, hk))
            ots.append(_dot(vt[HEAD * hk:HEAD * (hk + 1), :].astype(BF16), p.astype(BF16)))
        o = jnp.concatenate(ots, axis=0).T
        for hk in range(2):
            for i in range(4):
                h = 4 * hk + i
                o_ref[:, 512 + HEAD * h:512 + HEAD * (h + 1)] = o[CHUNK * i:CHUNK * (i + 1),
                                                                  HEAD * hk:HEAD * (hk + 1)].astype(BF16)

    return pl.pallas_call(
        body, name=name, grid=(nb,), in_specs=specs, out_specs=pl.BlockSpec((CHUNK, 1024), lambda i: (i, 0)),
        out_shape=jax.ShapeDtypeStruct((s, 1024), BF16), scratch_shapes=[pltpu.VMEM((CHUNK, 256), F32)],
        compiler_params=_params(("parallel",)),
    )(z, z, z, cosq, sinq, cosq, sinq, gv, ws, bt, pw, psc, snk, _ones_bd())


def _mixer_bwd(name, z, dabc, cosq, sinq, gv, ws, bt, pw, psc, snk):
    s = z.shape[0]
    nb = s // CHUNK
    specs, cur = _mixer_specs(nb, True)
    specs = specs + [pl.BlockSpec((CHUNK, 1024), lambda i: (cur(i), 0))]
    full = lambda shape: pl.BlockSpec(shape, lambda i: (0,) * len(shape))
    acc_shapes = [(1, 256), (4, CHUNK, CHUNK), (CHUNK, 4), (256, 256), (1, 256), (1, 8)]

    def body(zc_ref, zpp_ref, zpkv_ref, cq_ref, sq_ref, cp_ref, sp_ref, gv_ref, ws_ref, bt_ref, pw_ref, psc_ref,
             snk_ref, bd_ref, dabc_ref, dz_ref, dgv_ref, dws_ref, dbt_ref, dpw_ref, dpsc_ref, dsnk_ref,
             cpool, ck, cv, dq_s, dkv_s, mix_s, dvn_s):
        step = pl.program_id(0)
        ci = nb - 1 - step

        @pl.when(step == 0)
        def _():
            for r in (dgv_ref, dws_ref, dbt_ref, dpw_ref, dpsc_ref, dsnk_ref, cpool, ck, cv):
                r[...] = jnp.zeros_like(r)

        zc = zc_ref[...]
        dabc = dabc_ref[...]
        zg = zc[:, :512]
        gel, th = _gelu_parts(zg)
        gp = _gelu_grad(zg, th)
        gvv = gv_ref[...]
        bd = bd_ref[...]
        u, r, xh, vn, wcs, causal, mixed = _gating(gel, gvv, ws_ref, bt_ref[...], bd, mix_s)
        da = dabc[:, :256]
        dm = da * u
        dmb = dm.astype(BF16)
        lane4 = lax.broadcasted_iota(jnp.int32, (CHUNK, 4), 1)
        dbt = jnp.zeros((CHUNK, 4), F32)
        for g in range(4):
            lo, hi = HEAD * g, HEAD * (g + 1)
            dws_ref[g] += jnp.where(causal, _dot(dmb[:, lo:hi], vn[:, lo:hi], NT), 0.0)
            dbt = dbt + jnp.where(lane4 == g, jnp.sum(dm[:, lo:hi], axis=-1, keepdims=True), 0.0)
            dvn_s[:, lo:hi] = _dot(wcs[g], dmb[:, lo:hi], TN)
        dbt_ref[...] += dbt
        dvn = dvn_s[...]
        dgv_ref[...] += jnp.sum(dvn * xh, axis=0, keepdims=True)
        dxh = dvn * gvv
        dvg = r * (dxh - xh * _group_mean(dxh * xh, bd))
        dz_ref[:, :256] = (da * mixed * gp[:, :256]).astype(BF16)
        dz_ref[:, 256:512] = (dvg * gp[:, 256:]).astype(BF16)
        pc = zc[:, 512:768]
        pp = jnp.where(ci > 0, zpp_ref[...], 0.0)
        pooled, cnt, lane = _pool_fwd(pc, pp, ci)
        pwb = pw_ref[...].astype(BF16)
        pooled_b = pooled.astype(BF16)
        mp = _dot(pooled_b, pwb)
        db = dabc[:, 256:512]
        dpsc_ref[...] += jnp.sum(db * mp, axis=0, keepdims=True)
        dmpb = (db * psc_ref[...]).astype(BF16)
        dpw_ref[...] += _dot(pooled_b, dmpb, TN)
        dpooled = _dot(dmpb, pwb, NT)
        davg = dpooled / cnt
        zero = jnp.zeros((CHUNK, 256), F32)
        d2, d4, d8, d16 = [jnp.concatenate([zero, jnp.where((lane >= 64 * k) & (lane < 64 * (k + 1)), davg, 0.0)],
                                           axis=0) for k in range(4)]
        g8 = d8 + d16 + pltpu.roll(d16, 2 * CHUNK - 8, 0)
        g4 = d4 + g8 + pltpu.roll(g8, 2 * CHUNK - 4, 0)
        g2 = d2 + g4 + pltpu.roll(g4, 2 * CHUNK - 2, 0)
        ge = g2 + pltpu.roll(g2, 2 * CHUNK - 1, 0)
        dz_ref[:, 512:768] = (ge[CHUNK:] - dpooled + cpool[...]).astype(BF16)
        cpool[...] = ge[:CHUNK]
        cc = cq_ref[...]
        sc = sq_ref[...]
        qr, kband, vband, valid = _attn_prep(zc, zpkv_ref[...], cc, sc, cp_ref[...], sp_ref[...], ci)
        snkv = snk_ref[...]
        lane8 = lax.broadcasted_iota(jnp.int32, (1, 8), 1)
        qlane = lax.broadcasted_iota(jnp.int32, (1, 4 * CHUNK), 1)
        dsnk = jnp.zeros((1, 8), F32)
        kb = kband.astype(BF16)
        vb = vband.astype(BF16)
        kt = kband.T * SCALE
        dqts = []
        for hk in range(2):
            kh = kb[:, HEAD * hk:HEAD * (hk + 1)]
            q4 = (_stack_heads(qr, 0, hk) * SCALE).astype(BF16)
            do4 = _stack_heads(dabc, 512, hk).astype(BF16)
            p, ps = _group_probs(kh, q4, valid, _sink_row(snkv, hk))
            dp = _dot(vb[:, HEAD * hk:HEAD * (hk + 1)], do4, NT)
            dd = jnp.sum(p * dp, axis=0, keepdims=True)
            dsink = -ps * dd
            for i in range(4):
                part = jnp.sum(jnp.where((qlane >= CHUNK * i) & (qlane < CHUNK * (i + 1)), dsink, 0.0),
                               axis=1, keepdims=True)
                dsnk = dsnk + jnp.where(lane8 == 4 * hk + i, part, 0.0)
            dsb = (p * (dp - dd)).astype(BF16)
            dqts.append(_dot(kt[HEAD * hk:HEAD * (hk + 1), :].astype(BF16), dsb))
            dkv_s[:, HEAD * hk:HEAD * (hk + 1)] = _dot(dsb, q4)
            dkv_s[:, 128 + HEAD * hk:128 + HEAD * (hk + 1)] = _dot(p.astype(BF16), do4)
        dq4 = jnp.concatenate(dqts, axis=0).T
        for hk in range(2):
            for i in range(4):
                h = 4 * hk + i
                dq_s[:, HEAD * h:HEAD * (h + 1)] = dq4[CHUNK * i:CHUNK * (i + 1), HEAD * hk:HEAD * (hk + 1)]
        dsnk_ref[...] += dsnk
        dqr = dq_s[...]
        dz_ref[:, 768:1280] = (dqr * jnp.concatenate([cc] * 4, axis=1)
                               + _rot_half(dqr * jnp.concatenate([sc] * 4, axis=1))).astype(BF16)
        dkv = dkv_s[...]
        dkr = dkv[CHUNK:, :128] + ck[...]
        dz_ref[:, 1280:1408] = (dkr * cc + _rot_half(dkr * sc)).astype(BF16)
        dz_ref[:, 1408:1536] = (dkv[CHUNK:, 128:] + cv[...]).astype(BF16)
        ck[...] = dkv[:CHUNK, :128]
        cv[...] = dkv[:CHUNK, 128:]

    return pl.pallas_call(
        body, name=name, grid=(nb,), in_specs=specs,
        out_specs=[pl.BlockSpec((CHUNK, 1536), lambda i: (cur(i), 0))] + [full(a) for a in acc_shapes],
        out_shape=[jax.ShapeDtypeStruct((s, 1536), BF16)] + [jax.ShapeDtypeStruct(a, F32) for a in acc_shapes],
        scratch_shapes=[pltpu.VMEM((CHUNK, 256), F32), pltpu.VMEM((CHUNK, 128), F32), pltpu.VMEM((CHUNK, 128), F32),
                        pltpu.VMEM((CHUNK, 512), F32), pltpu.VMEM((2 * CHUNK, 256), F32),
                        pltpu.VMEM((CHUNK, 256), F32), pltpu.VMEM((CHUNK, 256), F32)],
        compiler_params=_params(("arbitrary",)),
    )(z, z, z, cosq, sinq, cosq, sinq, gv, ws, bt, pw, psc, snk, _ones_bd(), dabc)


def _xattn_probs(qh, kh):
    s = _dot(qh, kh, NT) * (256 ** -0.5)
    e = jnp.exp(s - jnp.max(s, axis=-1, keepdims=True))
    return e * (1.0 / jnp.sum(e, axis=-1, keepdims=True))


def _xattn_fwd(name, q, kv, tq):
    s, d = q.shape
    mlen = kv.shape[0]

    def body(q_ref, kv_ref, o_ref):
        for h in range(4):
            lo, hi = 256 * h, 256 * (h + 1)
            p = _xattn_probs(q_ref[:, lo:hi], kv_ref[:, lo:hi])
            o_ref[:, lo:hi] = _dot(p.astype(BF16), kv_ref[:, d + lo:d + hi]).astype(BF16)

    blk = pl.BlockSpec((tq, d), lambda i: (i, 0))
    return pl.pallas_call(body, name=name, grid=(s // tq,),
                          in_specs=[blk, pl.BlockSpec((mlen, 2 * d), lambda i: (0, 0))], out_specs=blk,
                          out_shape=jax.ShapeDtypeStruct((s, d), BF16), compiler_params=_params(("parallel",)))(q, kv)


def _xattn_bwd(name, q, kv, do, tq):
    s, d = q.shape
    mlen = kv.shape[0]

    def body(q_ref, kv_ref, do_ref, dq_ref, dkv_ref):
        @pl.when(pl.program_id(0) == 0)
        def _():
            dkv_ref[...] = jnp.zeros_like(dkv_ref)

        for h in range(4):
            lo, hi = 256 * h, 256 * (h + 1)
            qh = q_ref[:, lo:hi]
            kh = kv_ref[:, lo:hi]
            vh = kv_ref[:, d + lo:d + hi]
            doh = do_ref[:, lo:hi]
            p = _xattn_probs(qh, kh)
            dp = _dot(doh, vh, NT)
            dsb = (p * (dp - jnp.sum(p * dp, axis=-1, keepdims=True)) * (256 ** -0.5)).astype(BF16)
            dq_ref[:, lo:hi] = _dot(dsb, kh).astype(BF16)
            dkv_ref[:, lo:hi] += _dot(dsb, qh, TN)
            dkv_ref[:, d + lo:d + hi] += _dot(p.astype(BF16), doh, TN)

    blk = pl.BlockSpec((tq, d), lambda i: (i, 0))
    kvb = pl.BlockSpec((mlen, 2 * d), lambda i: (0, 0))
    return pl.pallas_call(
        body, name=name, grid=(s // tq,), in_specs=[blk, kvb, blk], out_specs=[blk, kvb],
        out_shape=[jax.ShapeDtypeStruct((s, d), BF16), jax.ShapeDtypeStruct((mlen, 2 * d), F32)],
        compiler_params=_params(("arbitrary",)))(q, kv, do)


def _sigmoid(x):
    return 0.5 * (1.0 + jnp.tanh(0.5 * x))


def _ffn_up(name, h, wgu, layer, tm, tn):
    s, d = h.shape
    dff = wgu.shape[2] // 2
    nj = dff // tn

    def body(h_ref, wg_ref, wu_ref, g_ref, u_ref, a_ref):
        hv = h_ref[...]
        gate = _dot(hv, wg_ref[...])
        up = _dot(hv, wu_ref[...])
        g_ref[...] = gate.astype(BF16)
        u_ref[...] = up.astype(BF16)
        a_ref[...] = (gate * _sigmoid(gate) * up).astype(BF16)

    ob = pl.BlockSpec((tm, tn), lambda j, i: (i, j))
    sd = jax.ShapeDtypeStruct((s, dff), BF16)
    return pl.pallas_call(
        body, name=name, grid=(nj, s // tm),
        in_specs=[pl.BlockSpec((tm, d), lambda j, i: (i, 0)),
                  pl.BlockSpec((None, d, tn), lambda j, i: (layer, 0, j)),
                  pl.BlockSpec((None, d, tn), lambda j, i: (layer, 0, j + nj))],
        out_specs=[ob, ob, ob], out_shape=[sd, sd, sd], compiler_params=_params(("parallel", "parallel")),
    )(h, wgu, wgu)


def _ffn_act_bwd(name, dfn, wdown, layer, gate, up, tm):
    s, d = dfn.shape
    dff = gate.shape[1]

    tn = 256

    def body(df_ref, wd_ref, g_ref, u_ref, o_ref):
        df = df_ref[...]
        for lo in range(0, dff, tn):
            dact = _dot(df, wd_ref[lo:lo + tn, :], NT).astype(BF16)
            gate = g_ref[:, lo:lo + tn]
            sig = _sigmoid(gate)
            gs = gate * sig
            o_ref[:, lo:lo + tn] = dact * u_ref[:, lo:lo + tn] * (sig + gs * (1.0 - sig))
            o_ref[:, dff + lo:dff + lo + tn] = dact * gs

    gb = pl.BlockSpec((tm, dff), lambda i: (i, 0))
    return pl.pallas_call(
        body, name=name, grid=(s // tm,),
        in_specs=[pl.BlockSpec((tm, d), lambda i: (i, 0)),
                  pl.BlockSpec((None, dff, d), lambda i: (layer, 0, 0)), gb, gb],
        out_specs=pl.BlockSpec((tm, 2 * dff), lambda i: (i, 0)),
        out_shape=jax.ShapeDtypeStruct((s, 2 * dff), BF16), compiler_params=_params(("parallel",)),
    )(dfn, wdown, gate, up)


def _place():
    return lax.axis_index("x"), lax.axis_index("y"), lax.axis_index("c")


def _other_chips(x, y):
    return [(1 - x, y), (x, 1 - y), (1 - x, 1 - y)]


def _region(ref, axis, chip, size):
    start = pl.multiple_of(chip * size, size)
    if axis == 1:
        return ref.at[:, pl.ds(start, size), :]
    return ref.at[:, :, pl.ds(start, size)]


ANY = pl.BlockSpec(memory_space=pl.ANY)


HBM = pl.BlockSpec(memory_space=pltpu.HBM)
SEM = pl.BlockSpec(memory_space=pltpu.SEMAPHORE)
EFFECT = pltpu.SideEffectType.DATAFLOW_SIDE_EFFECTING


def _in_hbm(a):
    return pltpu.with_memory_space_constraint(a, pltpu.HBM)


def _split_start(name, srcs, lands, ncopies, plan):
    ns, nl = len(srcs), len(lands)

    def body(*refs):
        src, land = refs[:ns], refs[ns:ns + nl]
        send, recv = refs[ns + nl], refs[ns + nl + 1]
        token = refs[-1]
        x, y, c = _place()
        for k, (s_ref, d_ref, peer, _) in enumerate(plan(src, land, x, y, c)):
            pltpu.make_async_remote_copy(src_ref=s_ref, dst_ref=d_ref, send_sem=send.at[k], recv_sem=recv.at[k],
                                         device_id=peer, device_id_type=MESH).start()
        token[...] = jnp.zeros_like(token)

    ops = list(srcs) + list(lands)
    out = pl.pallas_call(
        body, name=name,
        out_shape=(pltpu.SemaphoreType.DMA((ncopies,)), pltpu.SemaphoreType.DMA((ncopies,)),
                   *[pltpu.HBM(a.shape, a.dtype) for a in ops], jax.ShapeDtypeStruct((8, 128), F32)),
        in_specs=(HBM,) * (ns + nl),
        out_specs=(SEM, SEM) + (HBM,) * (ns + nl) + (pl.BlockSpec(memory_space=pltpu.VMEM),),
        input_output_aliases={i: 2 + i for i in range(ns + nl)},
        compiler_params=pltpu.CompilerParams(has_side_effects=EFFECT),
    )(*[_in_hbm(a) for a in ops])
    return out[0], out[1], list(out[2:2 + ns]), list(out[2 + ns:2 + ns + nl]), out[-1]


def _split_start_many(name, lands, jobs):
    nl, nj = len(lands), len(jobs)

    def body(*refs):
        land = refs[:nl]
        sems = refs[nl:nl + 2 * nj]
        token = refs[-1]
        x, y, c = _place()
        for j, (idx, _, plan) in enumerate(jobs):
            for k, (s_ref, d_ref, peer, _) in enumerate(plan((), [land[t] for t in idx], x, y, c)):
                pltpu.make_async_remote_copy(src_ref=s_ref, dst_ref=d_ref, send_sem=sems[2 * j].at[k],
                                             recv_sem=sems[2 * j + 1].at[k], device_id=peer,
                                             device_id_type=MESH).start()
        token[...] = jnp.zeros_like(token)

    sem_shapes = tuple(pltpu.SemaphoreType.DMA((n,)) for _, n, _ in jobs for _ in range(2))
    out = pl.pallas_call(
        body, name=name,
        out_shape=sem_shapes + tuple(pltpu.HBM(a.shape, a.dtype) for a in lands)
        + (jax.ShapeDtypeStruct((8, 128), F32),),
        in_specs=(HBM,) * nl,
        out_specs=(SEM,) * (2 * nj) + (HBM,) * nl + (pl.BlockSpec(memory_space=pltpu.VMEM),),
        input_output_aliases={i: 2 * nj + i for i in range(nl)},
        compiler_params=pltpu.CompilerParams(has_side_effects=EFFECT),
    )(*[_in_hbm(a) for a in lands])
    return [(out[2 * j], out[2 * j + 1]) for j in range(nj)], list(out[2 * nj:2 * nj + nl]), out[-1]


def _split_wait(name, send, recv, srcs, lands, after, plan):
    ns, nl = len(srcs), len(lands)

    def body(*refs):
        src, land = refs[:ns], refs[ns:ns + nl]
        send_ref, recv_ref = refs[ns + nl], refs[ns + nl + 1]
        x, y, c = _place()
        for k, (s_ref, _, _, got) in enumerate(plan(src, land, x, y, c)):
            cp = pltpu.make_async_remote_copy(src_ref=s_ref, dst_ref=got, send_sem=send_ref.at[k],
                                              recv_sem=recv_ref.at[k], device_id=(x, y, c), device_id_type=MESH)
            cp.wait_send()
            cp.wait_recv()

    ops = list(srcs) + list(lands)
    out = pl.pallas_call(
        body, name=name, out_shape=tuple(pltpu.HBM(a.shape, a.dtype) for a in ops),
        in_specs=(HBM,) * (ns + nl) + (SEM, SEM, ANY), out_specs=(HBM,) * (ns + nl),
        input_output_aliases={i: i for i in range(ns + nl)},
        compiler_params=pltpu.CompilerParams(has_side_effects=EFFECT),
    )(*ops, send, recv, after)
    return list(out[:ns]), list(out[ns:])


def _split_wait_start(name, send, recv, lands, after, wait_plan, ncopies, start_plan, carried=()):
    nl, nc = len(lands), len(carried)
    lands = list(lands) + list(carried)

    def body(*refs):
        land = refs[:nl]
        send_in, recv_in = refs[nl + nc], refs[nl + nc + 1]
        send_out, recv_out = refs[nl + nc + 3], refs[nl + nc + 4]
        x, y, c = _place()
        for k, (s_ref, _, _, got) in enumerate(wait_plan((), land, x, y, c)):
            cp = pltpu.make_async_remote_copy(src_ref=s_ref, dst_ref=got, send_sem=send_in.at[k],
                                              recv_sem=recv_in.at[k], device_id=(x, y, c), device_id_type=MESH)
            cp.wait_send()
            cp.wait_recv()
        for k, (s_ref, d_ref, peer, _) in enumerate(start_plan((), land, x, y, c)):
            pltpu.make_async_remote_copy(src_ref=s_ref, dst_ref=d_ref, send_sem=send_out.at[k],
                                         recv_sem=recv_out.at[k], device_id=peer, device_id_type=MESH).start()

    out = pl.pallas_call(
        body, name=name,
        out_shape=(pltpu.SemaphoreType.DMA((ncopies,)), pltpu.SemaphoreType.DMA((ncopies,)),
                   *[pltpu.HBM(a.shape, a.dtype) for a in lands]),
        in_specs=(HBM,) * (nl + nc) + (SEM, SEM, ANY), out_specs=(SEM, SEM) + (HBM,) * (nl + nc),
        input_output_aliases={i: 2 + i for i in range(nl + nc)},
        compiler_params=pltpu.CompilerParams(has_side_effects=EFFECT),
    )(*lands, send, recv, after)
    return out[0], out[1], list(out[2:2 + nl]), list(out[2 + nl:])


def _half(ref, axis, chip, size, layer, h):
    reg = _region(ref, axis, chip, size).at[pl.ds(layer, 1)]
    rows = reg.shape[1] // 2
    return reg.at[:, pl.ds(pl.multiple_of(h * rows, rows), rows), :]


def _gather_plan(axes, sizes, layer):
    def plan(src, land, x, y, c):
        me = 2 * x + y
        out = []
        for t in range(len(land)):
            mine = _half(land[t], axes[t], me, sizes[t], layer, c)
            for px, py in _other_chips(x, y):
                out.append((mine, mine, (px, py, c), _half(land[t], axes[t], 2 * px + py, sizes[t], layer, c)))
        return out
    return plan


def _forward_plan(axes, sizes, layer):
    def plan(src, land, x, y, c):
        out = []
        for t in range(len(land)):
            for px, py in _other_chips(x, y):
                got = _half(land[t], axes[t], 2 * px + py, sizes[t], layer, c)
                out.append((got, got, (x, y, 1 - c), _half(land[t], axes[t], 2 * px + py, sizes[t], layer, 1 - c)))
        return out
    return plan


def _place_own(name, w, axis, chip):
    nl, r, cs = w.shape
    tr = _rows_tile(r, cs)
    nb = r // tr
    full = (nl, 4 * r, cs) if axis == 1 else (nl, r, 4 * cs)

    def body(m_ref, w_ref, o_ref):
        o_ref[...] = w_ref[...].astype(BF16)

    if axis == 1:
        ospec = pl.BlockSpec((None, tr, cs), lambda l, i, m: (l, m[0] * nb + i, 0))
    else:
        ospec = pl.BlockSpec((None, tr, cs), lambda l, i, m: (l, i, m[0]))
    return pl.pallas_call(
        body, name=name,
        grid_spec=pltpu.PrefetchScalarGridSpec(
            num_scalar_prefetch=1, grid=(nl, nb),
            in_specs=[pl.BlockSpec((None, tr, cs), lambda l, i, m: (l, i, 0))], out_specs=ospec),
        out_shape=jax.ShapeDtypeStruct(full, BF16), compiler_params=_params(("parallel", "parallel")),
    )(chip, w)


def _scatter_plan(axes, sizes):
    def plan(src, land, x, y, c):
        out = []
        for t in range(len(src)):
            for k, (px, py) in enumerate(_other_chips(x, y)):
                out.append((_region(src[t], axes[t], 2 * px + py, sizes[t]).at[0], land[t].at[k], (px, py, c),
                            land[t].at[k]))
        return out
    return plan


def _pair_plan(src, land, x, y, c):
    return [(src[t], land[t], (x, y, 1 - c), land[t]) for t in range(len(src))]


UNIT_STEPS = 16


def _unit_rows(r):
    return min(t for t in range(16, r + 1, 16) if r % t == 0 and r // t <= UNIT_STEPS)


def _guarded(i, nb, steps, work):
    if nb == steps:
        work()
    else:
        pl.when(i < nb)(work)


def _unit_chip_sum(name, gs, slots, axes, chip):
    n = len(gs)
    dims = [s.shape[1:] for s in slots]
    trs = [_unit_rows(r) for r, _ in dims]
    nbs = [r // tr for (r, _), tr in zip(dims, trs)]
    steps = max(nbs)

    def body(m_ref, *refs):
        i = pl.program_id(0)
        for t in range(n):
            def work(t=t):
                acc = refs[t][...].astype(F32)
                for k in range(3):
                    acc = acc + refs[n + t][k].astype(F32)
                refs[2 * n + t][...] = acc.astype(BF16)
            _guarded(i, nbs[t], steps, work)

    gspecs, sspecs, ospecs = [], [], []
    for (r, cs), tr, nb, axis in zip(dims, trs, nbs, axes):
        if axis == 1:
            gspecs.append(pl.BlockSpec((tr, cs), lambda i, m, nb=nb: (m[0] * nb + jnp.minimum(i, nb - 1), 0)))
        else:
            gspecs.append(pl.BlockSpec((tr, cs), lambda i, m, nb=nb: (jnp.minimum(i, nb - 1), m[0])))
        sspecs.append(pl.BlockSpec((3, tr, cs), lambda i, m, nb=nb: (0, jnp.minimum(i, nb - 1), 0)))
        ospecs.append(pl.BlockSpec((tr, cs), lambda i, m, nb=nb: (jnp.minimum(i, nb - 1), 0)))
    return pl.pallas_call(
        body, name=name,
        grid_spec=pltpu.PrefetchScalarGridSpec(num_scalar_prefetch=1, grid=(steps,), in_specs=gspecs + sspecs,
                                               out_specs=ospecs),
        out_shape=[jax.ShapeDtypeStruct(d, BF16) for d in dims], compiler_params=_params(("arbitrary",)),
    )(chip, *gs, *slots)


def _unit_adamw(name, mine, theirs, ws, ms, vs, layer, bufs):
    n = len(mine)
    dims = [a.shape for a in mine]
    trs = [_unit_rows(r) for r, _ in dims]
    nbs = [r // tr for (r, _), tr in zip(dims, trs)]
    steps = max(nbs)
    c1 = 1.0 - B1 ** STEP
    c2 = 1.0 - B2 ** STEP

    def body(*refs):
        i = pl.program_id(0)
        outs = refs[9 * n:]
        outs[4 * n][...] = jnp.zeros_like(outs[4 * n])
        for t in range(n):
            def work(t=t):
                a_ref, b_ref, w_ref, m_ref, v_ref = refs[5 * t:5 * t + 5]
                g_ref, d_ref, mo_ref, vo_ref = outs[4 * t:4 * t + 4]
                gv = a_ref[...].astype(F32) + b_ref[...].astype(F32)
                mn = B1 * m_ref[...] + (1.0 - B1) * gv
                vn = B2 * v_ref[...] + (1.0 - B2) * (gv * gv)
                g_ref[...] = gv
                mo_ref[...] = mn
                vo_ref[...] = vn
                d_ref[...] = -LR * ((mn / c1) / (jnp.sqrt(vn / c2) + ADAM_EPS) + WD * w_ref[...])
            _guarded(i, nbs[t], steps, work)

    in_specs, out_specs, ops = [], [], []
    for t, ((r, cs), tr, nb) in enumerate(zip(dims, trs, nbs)):
        blk = pl.BlockSpec((tr, cs), lambda i, nb=nb: (jnp.minimum(i, nb - 1), 0))
        lay = pl.BlockSpec((None, tr, cs), lambda i, nb=nb: (layer, jnp.minimum(i, nb - 1), 0))
        in_specs += [blk, blk, lay, lay, lay]
        out_specs += [lay] * 4
        ops += [mine[t], theirs[t], ws[t], ms[t], vs[t]]
    flat = [b for bs in bufs for b in bs]
    out = pl.pallas_call(
        body, name=name, grid=(steps,), in_specs=in_specs + [ANY] * (4 * n),
        out_specs=out_specs + [pl.BlockSpec((8, 128), lambda i: (0, 0))],
        out_shape=[jax.ShapeDtypeStruct(b.shape, b.dtype) for b in flat] + [jax.ShapeDtypeStruct((8, 128), F32)],
        input_output_aliases={5 * n + k: k for k in range(4 * n)}, compiler_params=_params(("arbitrary",)),
    )(*ops, *flat)
    return [list(out[4 * t:4 * t + 4]) for t in range(n)], out[4 * n][:1, :1]


def _small_pair_sum(p):
    rows = p.shape[0]
    half = rows // 2

    def body(p_ref, o_ref, sib, send, recv):
        x, y, c = _place()
        mine = pl.ds(pl.multiple_of(c * half, half), half)
        theirs = pl.ds(pl.multiple_of((1 - c) * half, half), half)
        pair = pltpu.make_async_remote_copy(src_ref=p_ref.at[theirs], dst_ref=sib, send_sem=send, recv_sem=recv,
                                            device_id=(x, y, 1 - c), device_id_type=MESH)
        pair.start()
        pair.wait()
        o_ref[...] = (p_ref[mine] + sib[...]).astype(BF16)

    vm = pl.BlockSpec(memory_space=pltpu.VMEM)
    return pl.pallas_call(
        body, name="small_pair_sum", in_specs=[vm], out_specs=vm, out_shape=jax.ShapeDtypeStruct((half, 128), BF16),
        scratch_shapes=[pltpu.VMEM((half, 128), F32), pltpu.SemaphoreType.DMA, pltpu.SemaphoreType.DMA],
        compiler_params=pltpu.CompilerParams(vmem_limit_bytes=VMEM_LIMIT),
    )(p)


def _small_plan(src, land, x, y, c):
    return [(src[0], land[0].at[k], (px, py, c), land[0].at[k]) for k, (px, py) in enumerate(_other_chips(x, y))]


def _small_total(own, slots):
    half = own.shape[0]

    def body(own_ref, slots_ref, o_ref, sums, send, recv):
        x, y, c = _place()
        me = 2 * x + y
        mine = pl.ds(pl.multiple_of(c * half, half), half)
        theirs = pl.ds(pl.multiple_of((1 - c) * half, half), half)
        sums[me] = own_ref[...]
        for k, (px, py) in enumerate(_other_chips(x, y)):
            sums[2 * px + py] = slots_ref[k]
        acc = sums[0].astype(F32)
        for k in range(1, 4):
            acc = acc + sums[k].astype(F32)
        o_ref[mine] = acc
        back = pltpu.make_async_remote_copy(src_ref=o_ref.at[mine], dst_ref=o_ref.at[mine], send_sem=send,
                                            recv_sem=recv, device_id=(x, y, 1 - c), device_id_type=MESH)
        back.start()
        pltpu.make_async_remote_copy(src_ref=o_ref.at[theirs], dst_ref=o_ref.at[theirs], send_sem=send,
                                     recv_sem=recv, device_id=(x, y, c), device_id_type=MESH).wait_recv()
        back.wait_send()

    vm = pl.BlockSpec(memory_space=pltpu.VMEM)
    return pl.pallas_call(
        body, name="small_total", in_specs=[vm, vm], out_specs=vm,
        out_shape=jax.ShapeDtypeStruct((2 * half, 128), F32),
        scratch_shapes=[pltpu.VMEM((4, half, 128), BF16), pltpu.SemaphoreType.DMA, pltpu.SemaphoreType.DMA],
        compiler_params=pltpu.CompilerParams(vmem_limit_bytes=VMEM_LIMIT),
    )(own, slots)


def _small_adamw(gs, ws, ms, vs):
    n = len(gs)
    c1 = 1.0 - B1 ** STEP
    c2 = 1.0 - B2 ** STEP

    def body(*refs):
        for t in range(n):
            gv, wv = refs[t][...], refs[n + t][...]
            d_ref, mo_ref, vo_ref = refs[4 * n + 3 * t:4 * n + 3 * t + 3]
            mn = B1 * refs[2 * n + t][...] + (1.0 - B1) * gv
            vn = B2 * refs[3 * n + t][...] + (1.0 - B2) * (gv * gv)
            mo_ref[...] = mn
            vo_ref[...] = vn
            d_ref[...] = -LR * ((mn / c1) / (jnp.sqrt(vn / c2) + ADAM_EPS) + WD * wv)

    vm = pl.BlockSpec(memory_space=pltpu.VMEM)
    out = pl.pallas_call(
        body, name="adamw_small", in_specs=[vm] * (4 * n), out_specs=[vm] * (3 * n),
        out_shape=[jax.ShapeDtypeStruct(w.shape, F32) for w in ws for _ in range(3)],
        compiler_params=pltpu.CompilerParams(vmem_limit_bytes=VMEM_LIMIT),
    )(*gs, *ws, *ms, *vs)
    return [tuple(out[3 * t:3 * t + 3]) for t in range(n)]


def _pack(parts):
    flat = []
    for p in parts:
        v = p.reshape(-1).astype(F32)
        flat.append(jnp.pad(v, (0, (-v.shape[0]) % 128)))
    v = jnp.concatenate(flat)
    v = jnp.pad(v, (0, (-v.shape[0]) % (512 * 128)))
    return v.reshape(-1, 128)


def _unpack(buf, like):
    out, r0 = [], 0
    for p in like:
        nelem = 1
        for s in p.shape:
            nelem *= s
        rows = -(-nelem // 128)
        blk = buf[r0:r0 + rows]
        if nelem % 128:
            blk = blk.reshape(-1)[:nelem]
        out.append(blk.reshape(p.shape))
        r0 += rows
    return out


def kernel(x, mem, positions, mem_norm_g, mix_pre_g, mix_post_g, w_in, gm_v_g, gm_w_s, gm_b_s, pool_w, pool_scale, attn_sinks, w_o, x_pre_g, x_post_g, w_xq, w_xkv, w_xo, ffn_pre_g, ffn_post_g, w_gate_up, w_down, loss_target, m_mem_norm_g, m_mix_pre_g, m_mix_post_g, m_w_in, m_gm_v_g, m_gm_w_s, m_gm_b_s, m_pool_w, m_pool_scale, m_attn_sinks, m_w_o, m_x_pre_g, m_x_post_g, m_w_xq, m_w_xkv, m_w_xo, m_ffn_pre_g, m_ffn_post_g, m_w_gate_up, m_w_down, v_mem_norm_g, v_mix_pre_g, v_mix_post_g, v_w_in, v_gm_v_g, v_gm_w_s, v_gm_b_s, v_pool_w, v_pool_scale, v_attn_sinks, v_w_o, v_x_pre_g, v_x_post_g, v_w_xq, v_w_xkv, v_w_xo, v_ffn_pre_g, v_ffn_post_g, v_w_gate_up, v_w_down):
    args = (x, mem, positions, mem_norm_g, mix_pre_g, mix_post_g, w_in, gm_v_g, gm_w_s, gm_b_s, pool_w, pool_scale, attn_sinks, w_o, x_pre_g, x_post_g, w_xq, w_xkv, w_xo, ffn_pre_g, ffn_post_g, w_gate_up, w_down)
    moms_m = (m_mem_norm_g, m_mix_pre_g, m_mix_post_g, m_w_in, m_gm_v_g, m_gm_w_s, m_gm_b_s, m_pool_w, m_pool_scale, m_attn_sinks, m_w_o, m_x_pre_g, m_x_post_g, m_w_xq, m_w_xkv, m_w_xo, m_ffn_pre_g, m_ffn_post_g, m_w_gate_up, m_w_down)
    moms_v = (v_mem_norm_g, v_mix_pre_g, v_mix_post_g, v_w_in, v_gm_v_g, v_gm_w_s, v_gm_b_s, v_pool_w, v_pool_scale, v_attn_sinks, v_w_o, v_x_pre_g, v_x_post_g, v_w_xq, v_w_xkv, v_w_xo, v_ffn_pre_g, v_ffn_post_g, v_w_gate_up, v_w_down)
    P = dict(zip(NAMES, args))
    P['loss_target'] = loss_target
    M = dict(zip(WEIGHTS, moms_m))
    V = dict(zip(WEIGHTS, moms_v))
    depth = w_in.shape[0]
    nbig = len(BIG)
    axes = [BIG_AXIS[n] for n in BIG]
    sizes = [P[n].shape[a] for n, a in zip(BIG, axes)]
    chip = (2 * lax.axis_index("x") + lax.axis_index("y")).astype(jnp.int32).reshape(1)

    groups = [['w_in'], ['w_o', 'w_xq', 'w_xkv', 'w_xo'], ['w_gate_up', 'w_down']]
    units = [(l, g) for l in range(depth) for g in groups]
    unit_of = {(l, n): i for i, (l, names) in enumerate(units) for n in names}
    ax = lambda names: [BIG_AXIS[n] for n in names]
    sz = lambda names: [P[n].shape[BIG_AXIS[n]] for n in names]

    full = {n: _place_own("place_" + n, P[n], BIG_AXIS[n], chip) for n in BIG}
    gathers, land, tok = _split_start_many(
        "gather_start", [full[n] for n in BIG],
        [([BIG.index(n) for n in names], 3 * len(names), _gather_plan(ax(names), sz(names), l)) for l, names in units])
    full.update(zip(BIG, land))
    P['first_dep'] = tok[:1, :1]
    forwards, gathered = {}, set()

    def forward_unit(i, after, carried=()):
        ul, unames = units[i]
        send, recv = gathers[i]
        send, recv, land, thru = _split_wait_start(
            "gather_pass%d" % i, send, recv, [full[n] for n in unames], after,
            _gather_plan(ax(unames), sz(unames), ul), 3 * len(unames), _forward_plan(ax(unames), sz(unames), ul),
            carried=[full[n] for n in carried])
        full.update(zip(unames, land))
        full.update(zip(carried, thru))
        forwards[i] = (send, recv)

    def weights_of(l, names, after):
        i = unit_of[(l, names[0])]
        if i not in gathered:
            _, unames = units[i]
            if i not in forwards:
                forward_unit(i, after)
            send, recv = forwards.pop(i)
            _, land = _split_wait("gather_wait%d" % i, send, recv, [], [full[n] for n in unames], after,
                                  _forward_plan(ax(unames), sz(unames), l))
            full.update(zip(unames, land))
            gathered.add(i)
            if len(groups) <= i + 1 < len(units):
                forward_unit(i + 1, after, carried=[n for n in unames if n not in units[i + 1][1]])
        return {n: (full[n], l) for n in names}

    outs = {n: [lax.empty(P[n].shape, F32) for _ in range(4)] for n in BIG}
    gunits = [(l, BIG) for l in range(depth - 1, 0, -1)] + [
        (0, g) for g in (['w_gate_up', 'w_down'], ['w_xq', 'w_xkv', 'w_xo'], ['w_o'], ['w_in'])]
    collected, scatters, pairs = {}, {}, {}

    def finish_scatter(i, after):
        _, names = gunits[i]
        send, recv, g_l, slots = scatters.pop(i)
        g_l, slots = _split_wait("scatter_wait%d" % i, send, recv, g_l, slots, after,
                                 _scatter_plan(ax(names), sz(names)))
        mine = _unit_chip_sum("chip_sum", [g.reshape(g.shape[1:]) for g in g_l], slots, ax(names), chip)
        send, recv, mine, theirs, tok = _split_start("pair_start%d" % i, mine, [lax.empty(a.shape, BF16) for a in mine],
                                                     len(names), _pair_plan)
        pairs[i] = (send, recv, mine, theirs)
        return tok[:1, :1]

    def finish_pair(i, after):
        l, names = gunits[i]
        send, recv, mine, theirs = pairs.pop(i)
        mine, theirs = _split_wait("pair_wait%d" % i, send, recv, mine, theirs, after, _pair_plan)
        new, tok = _unit_adamw("adamw", mine, theirs, [P[n] for n in names], [M[n] for n in names],
                               [V[n] for n in names], l, [outs[n] for n in names])
        outs.update(zip(names, new))
        return tok

    calls = {'n': 0}
    lag = 4

    def grads_of(l, g_part, after):
        collected.update({(l, n): g for n, g in g_part.items()})
        calls['n'] += 1
        now = calls['n']
        tok = jnp.zeros((1, 1), F32)
        for i, (ul, names) in enumerate(gunits):
            if ul != l or ('started', i) in collected or any((l, n) not in collected for n in names):
                continue
            collected[('started', i)] = now
            srcs = [collected[(l, n)].reshape((1,) + collected[(l, n)].shape) for n in names]
            send, recv, srcs, slots, t = _split_start("scatter_start%d" % i, srcs,
                                                      [lax.empty((3,) + P[n].shape[1:], BF16) for n in names],
                                                      3 * len(names), _scatter_plan(ax(names), sz(names)))
            scatters[i] = (send, recv, srcs, slots)
            tok = tok + t[:1, :1]
        for i in sorted(scatters):
            if collected[('started', i)] + lag <= now:
                tok = tok + finish_scatter(i, after)
                collected[('summed', i)] = now
        return tok

    loss_part, dx, small_g = _fwd_bwd(P, weights_of, grads_of)
    loss = lax.psum(loss_part[0, 0], ("x", "y", "c"))
    grad_x = dx.reshape(x.shape)

    small_like = [P[n] for n in SMALL]
    half_sum = _small_pair_sum(_pack(small_g))
    send, recv, (half_sum,), (slots,), _ = _split_start(
        "small_start", [half_sum], [lax.empty((3,) + half_sum.shape, BF16)], 3, _small_plan)

    behind = lambda tok: half_sum + tok.astype(BF16)
    ready, after = sorted(pairs), half_sum
    for i in sorted(scatters):
        finish_scatter(i, after)
        if ready:
            after = behind(finish_pair(ready.pop(0), pairs[i][2][0]))
    for i in sorted(pairs):
        after = behind(finish_pair(i, after))
    grads, deltas, new_m, new_v = {}, {}, {}, {}
    for n in BIG:
        grads[n], deltas[n], new_m[n], new_v[n] = outs[n]

    (half_sum,), (slots,) = _split_wait("small_wait", send, recv, [half_sum], [slots], grads[BIG[0]], _small_plan)
    two_d = lambda a: a.reshape(-1, a.shape[-1])
    gs = _unpack(_small_total(half_sum, slots), small_like)
    upd = _small_adamw([two_d(g) for g in gs], [two_d(P[n]) for n in SMALL], [two_d(M[n]) for n in SMALL],
                       [two_d(V[n]) for n in SMALL])
    for n, g, (dlt, mn, vn) in zip(SMALL, gs, upd):
        shape = P[n].shape
        grads[n], deltas[n], new_m[n], new_v[n] = g, dlt.reshape(shape), mn.reshape(shape), vn.reshape(shape)

    return (loss, grad_x, *[grads[n] for n in WEIGHTS], *[deltas[n] for n in WEIGHTS],
            *[new_m[n] for n in WEIGHTS], *[new_v[n] for n in WEIGHTS])


def _fwd_bwd(P, weights_of, grads_of):
    (x, mem, positions, mem_norm_g, mix_pre_g, mix_post_g, w_in, gm_v_g, gm_w_s, gm_b_s, pool_w, pool_scale, attn_sinks,
     w_o, x_pre_g, x_post_g, w_xq, w_xkv, w_xo, ffn_pre_g, ffn_post_g, w_gate_up, w_down) = [P[n] for n in NAMES]
    x0 = x[0]
    s, d = x0.shape
    depth = w_in.shape[0]
    tgt = P['loss_target'][0]
    tmn = 256
    tmr = min(512, s)
    tmp = min(1024, s)
    tkw = min(2048, s)

    half = HEAD // 2
    inv = ROPE_THETA ** (-jnp.arange(half, dtype=F32) / half)
    ang = positions[0].astype(F32)[:, None] * inv
    cos, sin = jnp.cos(ang), jnp.sin(ang)
    cosq = jnp.tile(jnp.concatenate([cos, cos], axis=-1), (1, 2))
    sinq = jnp.tile(jnp.concatenate([-sin, sin], axis=-1), (1, 2))

    row = lambda a, l: a[l].reshape(1, -1)
    memn = _prenorm("mem_norm", mem[0], mem_norm_g.reshape(1, d), tmn)
    pw_all = jnp.where(_ones_bd() > 0, jnp.tile(pool_w.reshape(depth, 4 * HEAD, HEAD), (1, 1, 4)), 0.0)
    pw_bd = [pw_all[l] for l in range(depth)]

    saved = []
    xc = x0
    h = _prenorm("pre_norm0", x0, row(mix_pre_g, 0) + P['first_dep'], tmn)
    for l in range(depth):
        W = weights_of(l, ['w_in'], xc)
        sv = {'x0': xc, 'h1': h}
        z, = _mm_rows("fwd_w_in", h, *W['w_in'], 'nn', tm=tmp, rows_out=[F32], epilogue=_plain_rows)
        abc = _mixer_fwd("mixer_fwd", z, cosq, sinq, row(gm_v_g, l), gm_w_s[l], gm_b_s[l].T, pw_bd[l],
                         row(pool_scale, l), row(attn_sinks, l))
        W.update(weights_of(l, ['w_o'], z))
        mix, xc, h = _mm_rows("fwd_w_o", abc, *W['w_o'], 'nn', tm=tmr, rows_in=[xc],
                              params=[row(mix_post_g, l), row(x_pre_g, l)], rows_out=[BF16, F32, BF16],
                              epilogue=_post_pre_rows)
        sv.update(z=z, abc=abc, mix=mix, x1=xc, h2=h)
        W.update(weights_of(l, ['w_xq', 'w_xkv', 'w_xo'], xc))
        q, = _mm_rows("fwd_w_xq", h, *W['w_xq'], 'nn', tm=tmp, rows_out=[BF16], epilogue=_plain_rows)
        kv = _mm_nn("fwd_w_xkv", memn, *W['w_xkv'], tm=256, tn=512, tk=d, out_dtype=BF16)
        o = _xattn_fwd("xattn_fwd", q, kv, 512)
        xo, xc, h = _mm_rows("fwd_w_xo", o, *W['w_xo'], 'nn', tm=tmr, rows_in=[xc],
                             params=[row(x_post_g, l), row(ffn_pre_g, l)], rows_out=[BF16, F32, BF16],
                             epilogue=_post_pre_rows)
        sv.update(q=q, kv=kv, o=o, xo=xo, x2=xc, h3=h)
        W.update(weights_of(l, ['w_gate_up', 'w_down'], xc))
        dff = W['w_down'][0].shape[1]
        gate, up, act = _ffn_up("ffn_up", h, *W['w_gate_up'], 512, dff // 2)
        sv.update(gate=gate, up=up, act=act)
        if l + 1 < depth:
            f, xc, h = _mm_rows("fwd_w_down", act, *W['w_down'], 'nn', tm=tmr, rows_in=[xc],
                                params=[row(ffn_post_g, l), row(mix_pre_g, l + 1)], rows_out=[BF16, F32, BF16],
                                epilogue=_post_pre_rows)
            sv.update(f=f)
        saved.append(sv)
    gs = {n: [None] * depth for n in SMALL if n != 'mem_norm_g'}
    dx, dfn, gs['ffn_post_g'][depth - 1], loss_part = _mm_rows(
        "fwd_w_down_loss", saved[-1]['act'], *W['w_down'], 'nn', tm=tmr, rows_in=[xc, tgt],
        params=[row(ffn_post_g, depth - 1)], rows_out=[F32, BF16], n_sums=2, epilogue=_make_loss_rows(d))

    dmemn = None
    tok = jnp.zeros((1, 1), F32)
    for l in reversed(range(depth)):
        sv, W, G = saved[l], weights_of(l, BIG, dx), {}
        G['w_down'] = _mm_tn("dw_down", sv['act'], dfn, tm=dff // 2, tn=d, tk=tkw)
        dgu = _ffn_act_bwd("ffn_act_bwd", dfn, *W['w_down'], sv['gate'], sv['up'], 256)
        G['w_gate_up'] = _mm_tn("dw_gate_up", sv['h3'], dgu, tm=d, tn=dff // 2, tk=tkw)
        dx, dxo, gs['ffn_pre_g'][l], gs['x_post_g'][l] = _mm_rows(
            "bwd_w_gate_up", dgu, *W['w_gate_up'], 'nt', tm=tmr, rows_in=[sv['x2'], dx, sv['xo']],
            params=[row(ffn_pre_g, l) + tok, row(x_post_g, l)], rows_out=[F32, BF16], n_sums=2, epilogue=_bwd_rows)
        tok = grads_of(l, {n: G[n] for n in ('w_gate_up', 'w_down')}, dx)
        G['w_xo'] = _mm_tn("dw_xo", sv['o'], dxo, tm=d, tn=d, tk=tkw)
        do, = _mm_rows("bwd_w_xo", dxo, *W['w_xo'], 'nt', tm=tmp, params=[jnp.zeros((1, d), F32) + tok],
                       rows_out=[BF16], epilogue=_plain_rows)
        dq, dkv = _xattn_bwd("xattn_bwd", sv['q'], sv['kv'], do, 512)
        dkv = dkv.astype(BF16)
        G['w_xkv'] = _mm_tn("dw_xkv", memn, dkv, tm=d, tn=d, tk=mem.shape[1])
        dmemn = _mm_nt("bwd_w_xkv", dkv, *W['w_xkv'], tm=mem.shape[1], tn=512, tk=2 * d, out_dtype=F32, add=dmemn)
        G['w_xq'] = _mm_tn("dw_xq", sv['h2'], dq, tm=d, tn=d, tk=tkw)
        dx, dmix, gs['x_pre_g'][l], gs['mix_post_g'][l] = _mm_rows(
            "bwd_w_xq", dq, *W['w_xq'], 'nt', tm=tmr, rows_in=[sv['x1'], dx, sv['mix']],
            params=[row(x_pre_g, l), row(mix_post_g, l)], rows_out=[F32, BF16], n_sums=2, epilogue=_bwd_rows)
        tok = grads_of(l, {n: G[n] for n in ('w_xq', 'w_xkv', 'w_xo')}, dx)
        G['w_o'] = _mm_tn("dw_o", sv['abc'], dmix, tm=d, tn=d, tk=tkw)
        dabc, = _mm_rows("bwd_w_o", dmix, *W['w_o'], 'nt', tm=tmp, params=[jnp.zeros((1, d), F32) + tok],
                         rows_out=[F32], epilogue=_plain_rows)
        tok = grads_of(l, {'w_o': G['w_o']}, dabc)
        dz, dgv, dws, dbt, dpw, dpsc, dsnk = _mixer_bwd(
            "mixer_bwd", sv['z'], dabc, cosq, sinq, row(gm_v_g, l) + tok, gm_w_s[l], gm_b_s[l].T, pw_bd[l],
            row(pool_scale, l), row(attn_sinks, l))
        gs['gm_v_g'][l] = dgv
        gs['gm_w_s'][l] = dws
        gs['gm_b_s'][l] = dbt.T
        gs['pool_w'][l] = dpw
        gs['pool_scale'][l] = dpsc
        gs['attn_sinks'][l] = dsnk
        G['w_in'] = _mm_tn("dw_in", sv['h1'], dz, tm=d, tn=dz.shape[1], tk=tkw)
        if l > 0:
            dx, dfn, gs['mix_pre_g'][l], gs['ffn_post_g'][l - 1] = _mm_rows(
                "bwd_w_in", dz, *W['w_in'], 'nt', tm=tmr, rows_in=[sv['x0'], dx, saved[l - 1]['f']],
                params=[row(mix_pre_g, l), row(ffn_post_g, l - 1)], rows_out=[F32, BF16], n_sums=2,
                epilogue=_bwd_rows)
        else:
            dx, gs['mix_pre_g'][l] = _mm_rows(
                "bwd_w_in_first", dz, *W['w_in'], 'nt', tm=tmr, rows_in=[sv['x0'], dx],
                params=[row(mix_pre_g, l)], rows_out=[F32], n_sums=1, epilogue=_bwd_rows_first)
        tok = grads_of(l, {'w_in': G['w_in']}, dx)
    _, dg_mem = _norm_bwd("bwd_mem_norm", mem[0], mem_norm_g.reshape(1, d) + tok, dmemn, None, BF16, tmn)
    small_g = []
    for n in SMALL:
        if n == 'mem_norm_g':
            small_g.append(dg_mem.reshape(P[n].shape))
        elif n == 'pool_w':
            blocks = jnp.stack(gs[n]).reshape(depth, 4, HEAD, 4, HEAD)
            same = jnp.eye(4, dtype=jnp.bool_)[None, :, None, :, None]
            small_g.append(jnp.sum(jnp.where(same, blocks, 0.0), axis=3))
        else:
            small_g.append(jnp.stack([a.reshape(P[n].shape[1:]) for a in gs[n]]))
    return loss_part, dx, small_g
```

```python
import functools

import jax
import jax.numpy as jnp
from jax import lax
from jax.experimental import pallas as pl
from jax.experimental.pallas import tpu as pltpu

F32 = jnp.float32
BF16 = jnp.bfloat16
EPS = 1e-6
CHUNK = 128
HEAD = 64
ROPE_THETA = 10000.0
POOL_WINDOWS = (2, 4, 8, 16)
LR, B1, B2, ADAM_EPS, WD, STEP = 0.001, 0.9, 0.999, 1e-08, 0.01, 10
MESH = pl.DeviceIdType.MESH
VMEM_LIMIT = 56 * 1024 * 1024

NAMES = ['x', 'mem', 'positions', 'mem_norm_g', 'mix_pre_g', 'mix_post_g', 'w_in', 'gm_v_g', 'gm_w_s', 'gm_b_s',
         'pool_w', 'pool_scale', 'attn_sinks', 'w_o', 'x_pre_g', 'x_post_g', 'w_xq', 'w_xkv', 'w_xo', 'ffn_pre_g',
         'ffn_post_g', 'w_gate_up', 'w_down']
WEIGHTS = NAMES[3:]
BIG = ['w_in', 'w_o', 'w_xq', 'w_xkv', 'w_xo', 'w_gate_up', 'w_down']
BIG_AXIS = {'w_in': 2, 'w_o': 1, 'w_xq': 1, 'w_xkv': 2, 'w_xo': 1, 'w_gate_up': 2, 'w_down': 1}
SMALL = [n for n in WEIGHTS if n not in BIG]

NN = (((1,), (0,)), ((), ()))
NT = (((1,), (1,)), ((), ()))
TN = (((0,), (0,)), ((), ()))


def _dot(a, b, dims=NN):
    return lax.dot_general(a, b, dims, preferred_element_type=F32)


def _params(sem):
    return pltpu.CompilerParams(dimension_semantics=sem, vmem_limit_bytes=VMEM_LIMIT)


STREAM_BLOCK_BYTES = 3 * 512 * 1024
PLACE_BLOCK_BYTES = 512 * 1024
PLACE_BUFFERS = 4


def _rows_tile(rows, cols):
    limit = max(16, STREAM_BLOCK_BYTES // (4 * cols))
    return max(t for t in range(16, min(rows, limit) + 1, 16) if rows % t == 0)


def _mm(name, a, a_spec, b, b_spec, dims, grid, nk, out_shape, out_spec, add=None, add_spec=None):
    acc_shape = out_spec.block_shape

    def body(*refs):
        a_ref, b_ref = refs[0], refs[1]
        pos = 2
        add_ref = None
        if add is not None:
            add_ref = refs[pos]
            pos += 1
        o_ref = refs[pos]
        part = _dot(a_ref[...].astype(BF16), b_ref[...].astype(BF16), dims)
        if nk == 1:
            if add_ref is not None:
                part = part + add_ref[...]
            o_ref[...] = part.astype(o_ref.dtype)
        else:
            acc_ref = refs[pos + 1]
            k = pl.program_id(2)

            @pl.when(k == 0)
            def _():
                acc_ref[...] = part if add_ref is None else part + add_ref[...]

            @pl.when(k > 0)
            def _():
                acc_ref[...] += part

            @pl.when(k == nk - 1)
            def _():
                o_ref[...] = acc_ref[...].astype(o_ref.dtype)

    ops, specs = [a, b], [a_spec, b_spec]
    if add is not None:
        ops.append(add)
        specs.append(add_spec)
    return pl.pallas_call(
        body, name=name, grid=grid, in_specs=specs, out_specs=out_spec, out_shape=out_shape,
        scratch_shapes=[pltpu.VMEM(acc_shape, F32)] if nk > 1 else [],
        compiler_params=_params(("parallel", "parallel", "arbitrary")),
    )(*ops)


def _wspec(block, layer, fn):
    return pl.BlockSpec((None,) + block, lambda i, j, k: (layer,) + fn(i, j, k))


def _mm_nn(name, a, w, layer, *, tm, tn, tk, out_dtype):
    m, kk = a.shape
    n = w.shape[2]
    tm = min(tm, m)
    nk = kk // tk
    return _mm(name, a, pl.BlockSpec((tm, tk), lambda i, j, k: (i, k)),
               w, _wspec((tk, tn), layer, lambda i, j, k: (k, j)), NN,
               (m // tm, n // tn, nk), nk, jax.ShapeDtypeStruct((m, n), out_dtype),
               pl.BlockSpec((tm, tn), lambda i, j, k: (i, j)))


def _mm_nt(name, a, w, layer, *, tm, tn, tk, out_dtype, add=None):
    m, kk = a.shape
    n = w.shape[1]
    tm = min(tm, m)
    nk = kk // tk
    ospec = pl.BlockSpec((tm, tn), lambda i, j, k: (i, j))
    return _mm(name, a, pl.BlockSpec((tm, tk), lambda i, j, k: (i, k)),
               w, _wspec((tn, tk), layer, lambda i, j, k: (j, k)), NT,
               (m // tm, n // tn, nk), nk, jax.ShapeDtypeStruct((m, n), out_dtype), ospec,
               add=add, add_spec=ospec if add is not None else None)


def _mm_tn(name, a, b, *, tm, tn, tk):
    kk, m = a.shape
    n = b.shape[1]
    tk = min(tk, kk)
    nk = kk // tk
    return _mm(name, a, pl.BlockSpec((tk, tm), lambda i, j, k: (k, i)),
               b, pl.BlockSpec((tk, tn), lambda i, j, k: (k, j)), TN,
               (m // tm, n // tn, nk), nk, jax.ShapeDtypeStruct((m, n), BF16),
               pl.BlockSpec((tm, tn), lambda i, j, k: (i, j)))


def _mm_rows(name, a, w, layer, mode, *, tm, rows_in=(), params=(), rows_out=(), n_sums=0, epilogue):
    m, kk = a.shape
    n = w.shape[2] if mode == 'nn' else w.shape[1]
    nr, npar, no = len(rows_in), len(params), len(rows_out)

    def body(*refs):
        a_ref, w_ref = refs[0], refs[1]
        rin = refs[2:2 + nr]
        par = refs[2 + nr:2 + nr + npar]
        outs = refs[2 + nr + npar:2 + nr + npar + no]
        sums = refs[2 + nr + npar + no:2 + nr + npar + no + n_sums]
        acc = _dot(a_ref[...], w_ref[...], NN if mode == 'nn' else NT)
        res, sm = epilogue(acc, [r[...] for r in rin], [p[...] for p in par])
        for r, v in zip(outs, res):
            r[...] = v.astype(r.dtype)

        @pl.when(pl.program_id(0) == 0)
        def _():
            for r in sums:
                r[...] = jnp.zeros_like(r)

        for r, v in zip(sums, sm):
            r[...] += v

    wblock = (None, kk, n) if mode == 'nn' else (None, n, kk)
    rowblk = pl.BlockSpec((tm, n), lambda i: (i, 0))
    one = pl.BlockSpec((1, n), lambda i: (0, 0))
    return pl.pallas_call(
        body, name=name, grid=(m // tm,),
        in_specs=[pl.BlockSpec((tm, kk), lambda i: (i, 0)),
                  pl.BlockSpec(wblock, lambda i: (layer, 0, 0), pipeline_mode=pl.Buffered(1))]
                 + [rowblk] * nr + [one] * npar,
        out_specs=[rowblk] * no + [one] * n_sums,
        out_shape=[jax.ShapeDtypeStruct((m, n), dt) for dt in rows_out] +
                  [jax.ShapeDtypeStruct((1, n), F32)] * n_sums,
        compiler_params=_params(("arbitrary",)),
    )(a, w, *rows_in, *params)


def _rstd(x):
    return lax.rsqrt(jnp.mean(x * x, axis=-1, keepdims=True) + EPS)


def _norm_back(xin, g, dy):
    r = _rstd(xin)
    xh = xin * r
    dyg = dy * g
    return r * (dyg - xh * jnp.mean(dyg * xh, axis=-1, keepdims=True)), jnp.sum(dy * xh, axis=0, keepdims=True)


def _plain_rows(acc, rows, pars):
    return [acc], []


def _post_pre_rows(y, rows, pars):
    xn = rows[0] + y * _rstd(y) * pars[0]
    return [y, xn, xn * _rstd(xn) * pars[1]], []


def _make_loss_rows(d):
    def fn(y, rows, pars):
        x, tgt = rows
        err = x + y * _rstd(y) * pars[0] - tgt
        dout = err * (1.0 / d)
        dy, dg = _norm_back(y, pars[0], dout)
        lsum = 0.5 * jnp.sum(jnp.mean(err * err, axis=-1, keepdims=True), axis=0, keepdims=True)
        return [dout, dy], [dg, jnp.broadcast_to(lsum, dg.shape)]
    return fn


def _bwd_rows(dh, rows, pars):
    xin, resid, yprev = rows
    dxa, dg_pre = _norm_back(xin, pars[0], dh)
    dx = resid + dxa
    dyp, dg_post = _norm_back(yprev.astype(F32), pars[1], dx)
    return [dx, dyp], [dg_pre, dg_post]


def _bwd_rows_first(dh, rows, pars):
    xin, resid = rows
    dxa, dg_pre = _norm_back(xin, pars[0], dh)
    return [resid + dxa], [dg_pre]


def _row(d):
    return pl.BlockSpec((1, d), lambda i: (0, 0))


def _prenorm(name, x, g, tm):
    m, d = x.shape

    def body(x_ref, g_ref, o_ref):
        xv = x_ref[...]
        o_ref[...] = (xv * _rstd(xv) * g_ref[...]).astype(BF16)

    blk = pl.BlockSpec((tm, d), lambda i: (i, 0))
    return pl.pallas_call(body, name=name, grid=(m // tm,), in_specs=[blk, _row(d)], out_specs=blk,
                          out_shape=jax.ShapeDtypeStruct((m, d), BF16), compiler_params=_params(("parallel",)))(x, g)


def _norm_bwd(name, xin, g, dy, resid, out_dtype, tm):
    m, d = xin.shape

    def body(*refs):
        if resid is None:
            x_ref, g_ref, dy_ref, dx_ref, dg_ref = refs
        else:
            x_ref, g_ref, dy_ref, r_ref, dx_ref, dg_ref = refs
        xv = x_ref[...]
        r = _rstd(xv)
        xh = xv * r
        dyv = dy_ref[...].astype(F32)
        dyg = dyv * g_ref[...]
        dx = r * (dyg - xh * jnp.mean(dyg * xh, axis=-1, keepdims=True))
        if resid is not None:
            dx = dx + r_ref[...]
        dx_ref[...] = dx.astype(dx_ref.dtype)

        @pl.when(pl.program_id(0) == 0)
        def _():
            dg_ref[...] = jnp.zeros_like(dg_ref)

        dg_ref[...] += jnp.sum(dyv * xh, axis=0, keepdims=True)

    blk = pl.BlockSpec((tm, d), lambda i: (i, 0))
    ops = [xin, g, dy] + ([] if resid is None else [resid])
    specs = [blk, _row(d), blk] + ([] if resid is None else [blk])
    return pl.pallas_call(
        body, name=name, grid=(m // tm,), in_specs=specs, out_specs=[blk, _row(d)],
        out_shape=[jax.ShapeDtypeStruct((m, d), out_dtype), jax.ShapeDtypeStruct((1, d), F32)],
        compiler_params=_params(("arbitrary",)))(*ops)


def _gelu_parts(x):
    c = 0.7978845608028654
    t = jnp.tanh(c * (x + 0.044715 * (x * x * x)))
    return 0.5 * x * (1.0 + t), t


def _gelu_grad(x, t):
    c = 0.7978845608028654
    return 0.5 * (1.0 + t) + 0.5 * x * (1.0 - t * t) * (c * (1.0 + 3.0 * 0.044715 * x * x))


def _rot_half(x):
    ax = x.ndim - 1
    w = x.shape[ax]
    lane = lax.broadcasted_iota(jnp.int32, x.shape, ax)
    return jnp.where((lane & 63) < 32, pltpu.roll(x, w - 32, ax), pltpu.roll(x, 32, ax))


def _group_mean(x, ones_bd):
    hi = x.astype(BF16)
    lo = (x - hi.astype(F32)).astype(BF16)
    return (_dot(hi, ones_bd) + _dot(lo, ones_bd)) * (1.0 / HEAD)


def _gating(gel, gv, ws_ref, bt, ones_bd, mix_s):
    u = gel[:, :256]
    v = gel[:, 256:]
    r = lax.rsqrt(_group_mean(v * v, ones_bd) + EPS)
    xh = v * r
    vn = (xh * gv).astype(BF16)
    row = lax.broadcasted_iota(jnp.int32, (CHUNK, CHUNK), 0)
    col = lax.broadcasted_iota(jnp.int32, (CHUNK, CHUNK), 1)
    causal = col <= row
    wcs = []
    for g in range(4):
        wc = jnp.where(causal, ws_ref[g], 0.0).astype(BF16)
        wcs.append(wc)
        mix_s[:, HEAD * g:HEAD * (g + 1)] = _dot(wc, vn[:, HEAD * g:HEAD * (g + 1)]) + bt[:, g:g + 1]
    return u, r, xh, vn, wcs, causal, mix_s[...]


def _lane_select(lane, vals):
    return jnp.where(lane < 64, vals[0], jnp.where(lane < 128, vals[1], jnp.where(lane < 192, vals[2], vals[3])))


def _pool_fwd(pc, pp, ci):
    ext = jnp.concatenate([pp, pc], axis=0)
    s2 = ext + pltpu.roll(ext, 1, 0)
    s4 = s2 + pltpu.roll(s2, 2, 0)
    s8 = s4 + pltpu.roll(s4, 4, 0)
    s16 = s8 + pltpu.roll(s8, 8, 0)
    t1 = ci * CHUNK + lax.broadcasted_iota(jnp.int32, (CHUNK, 1), 0) + 1
    lane = lax.broadcasted_iota(jnp.int32, (1, 256), 1)
    cnt = _lane_select(lane, [jnp.minimum(t1, w).astype(F32) for w in POOL_WINDOWS])
    ssel = _lane_select(lane, [s[CHUNK:] for s in (s2, s4, s8, s16)])
    return ssel / cnt - pc, cnt, lane


def _attn_prep(zc, zpkv, cc, sc, cp, sp, ci):
    q = zc[:, 768:1280]
    kc = zc[:, 1280:1408]
    vc = zc[:, 1408:1536]
    kp = zpkv[:, :128]
    vp = zpkv[:, 128:]
    qr = q * jnp.concatenate([cc] * 4, axis=1) + _rot_half(q) * jnp.concatenate([sc] * 4, axis=1)
    krc = kc * cc + _rot_half(kc) * sc
    krp = kp * cp + _rot_half(kp) * sp
    kband = jnp.concatenate([krp, krc], axis=0)
    vband = jnp.concatenate([vp, vc], axis=0)
    key = lax.broadcasted_iota(jnp.int32, (2 * CHUNK, 4 * CHUNK), 0)
    t = lax.broadcasted_iota(jnp.int32, (2 * CHUNK, 4 * CHUNK), 1) & (CHUNK - 1)
    valid = ((key < CHUNK) & (key > t) & (ci > 0)) | ((key >= CHUNK) & (key - CHUNK <= t))
    return qr, kband, vband, valid


SCALE = HEAD ** -0.5


def _stack_heads(x, base, hk):
    return jnp.concatenate([x[:, base + HEAD * (4 * hk + i):base + HEAD * (4 * hk + i + 1)] for i in range(4)], axis=0)


def _sink_row(snk, hk):
    lane = lax.broadcasted_iota(jnp.int32, (1, 4 * CHUNK), 1)
    s = [snk[:, 4 * hk + i:4 * hk + i + 1] for i in range(4)]
    return jnp.where(lane < CHUNK, s[0], jnp.where(lane < 2 * CHUNK, s[1], jnp.where(lane < 3 * CHUNK, s[2], s[3])))


def _group_probs(kh, q4, valid, sink4):
    s = jnp.where(valid, _dot(kh, q4, NT), -1e30)
    mx = jnp.maximum(jnp.max(s, axis=0, keepdims=True), sink4)
    e = jnp.exp(s - mx)
    es = jnp.exp(sink4 - mx)
    inv = 1.0 / (jnp.sum(e, axis=0, keepdims=True) + es)
    return e * inv, es * inv


def _mixer_specs(nb, rev):
    def cur(i):
        return nb - 1 - i if rev else i

    def prev(i):
        return jnp.maximum(cur(i) - 1, 0)

    full = lambda shape: pl.BlockSpec(shape, lambda i: (0,) * len(shape))
    specs = [
        pl.BlockSpec((CHUNK, 1536), lambda i: (cur(i), 0)),
        pl.BlockSpec((CHUNK, 256), lambda i: (prev(i), 2)),
        pl.BlockSpec((CHUNK, 256), lambda i: (prev(i), 5)),
        pl.BlockSpec((CHUNK, 128), lambda i: (cur(i), 0)),
        pl.BlockSpec((CHUNK, 128), lambda i: (cur(i), 0)),
        pl.BlockSpec((CHUNK, 128), lambda i: (prev(i), 0)),
        pl.BlockSpec((CHUNK, 128), lambda i: (prev(i), 0)),
        full((1, 256)), full((4, CHUNK, CHUNK)), full((CHUNK, 4)), full((256, 256)), full((1, 256)), full((1, 8)),
        full((256, 256)),
    ]
    return specs, cur


def _ones_bd():
    g = lax.broadcasted_iota(jnp.int32, (256, 256), 0) // HEAD == lax.broadcasted_iota(jnp.int32, (256, 256), 1) // HEAD
    return g.astype(BF16)


def _mixer_fwd(name, z, cosq, sinq, gv, ws, bt, pw, psc, snk):
    s = z.shape[0]
    nb = s // CHUNK
    specs, _ = _mixer_specs(nb, False)

    def body(zc_ref, zpp_ref, zpkv_ref, cq_ref, sq_ref, cp_ref, sp_ref, gv_ref, ws_ref, bt_ref, pw_ref, psc_ref,
             snk_ref, bd_ref, o_ref, mix_s):
        ci = pl.program_id(0)
        zc = zc_ref[...]
        gel, _ = _gelu_parts(zc[:, :512])
        u, _, _, _, _, _, mixed = _gating(gel, gv_ref[...], ws_ref, bt_ref[...], bd_ref[...], mix_s)
        o_ref[:, :256] = (u * mixed).astype(BF16)
        pp = jnp.where(ci > 0, zpp_ref[...], 0.0)
        pooled, _, _ = _pool_fwd(zc[:, 512:768], pp, ci)
        mp = _dot(pooled.astype(BF16), pw_ref[...].astype(BF16))
        o_ref[:, 256:512] = (mp * psc_ref[...]).astype(BF16)
        qr, kband, vband, valid = _attn_prep(zc, zpkv_ref[...], cq_ref[...], sq_ref[...], cp_ref[...], sp_ref[...], ci)
        snkv = snk_ref[...]
        kb = kband.astype(BF16)
        vt = vband.T
        ots = []
        for hk in range(2):
            q4 = (_stack_heads(qr, 0, hk) * SCALE).astype(BF16)
            p, _ = _group_probs(kb[:, HEAD * hk:HEAD * (hk + 1)], q4, valid, _sink_row(snkv, hk))
            ots.append(_dot(vt[HEAD * hk:HEAD * (hk + 1), :].astype(BF16), p.astype(BF16)))
        o = jnp.concatenate(ots, axis=0).T
        for hk in range(2):
            for i in range(4):
                h = 4 * hk + i
                o_ref[:, 512 + HEAD * h:512 + HEAD * (h + 1)] = o[CHUNK * i:CHUNK * (i + 1),
                                                                  HEAD * hk:HEAD * (hk + 1)].astype(BF16)

    return pl.pallas_call(
        body, name=name, grid=(nb,), in_specs=specs, out_specs=pl.BlockSpec((CHUNK, 1024), lambda i: (i, 0)),
        out_shape=jax.ShapeDtypeStruct((s, 1024), BF16), scratch_shapes=[pltpu.VMEM((CHUNK, 256), F32)],
        compiler_params=_params(("parallel",)),
    )(z, z, z, cosq, sinq, cosq, sinq, gv, ws, bt, pw, psc, snk, _ones_bd())


def _mixer_bwd(name, z, dabc, cosq, sinq, gv, ws, bt, pw, psc, snk):
    s = z.shape[0]
    nb = s // CHUNK
    specs, cur = _mixer_specs(nb, True)
    specs = specs + [pl.BlockSpec((CHUNK, 1024), lambda i: (cur(i), 0))]
    full = lambda shape: pl.BlockSpec(shape, lambda i: (0,) * len(shape))
    acc_shapes = [(1, 256), (4, CHUNK, CHUNK), (CHUNK, 4), (256, 256), (1, 256), (1, 8)]

    def body(zc_ref, zpp_ref, zpkv_ref, cq_ref, sq_ref, cp_ref, sp_ref, gv_ref, ws_ref, bt_ref, pw_ref, psc_ref,
             snk_ref, bd_ref, dabc_ref, dz_ref, dgv_ref, dws_ref, dbt_ref, dpw_ref, dpsc_ref, dsnk_ref,
             cpool, ck, cv, dq_s, dkv_s, mix_s, dvn_s):
        step = pl.program_id(0)
        ci = nb - 1 - step

        @pl.when(step == 0)
        def _():
            for r in (dgv_ref, dws_ref, dbt_ref, dpw_ref, dpsc_ref, dsnk_ref, cpool, ck, cv):
                r[...] = jnp.zeros_like(r)

        zc = zc_ref[...]
        dabc = dabc_ref[...]
        zg = zc[:, :512]
        gel, th = _gelu_parts(zg)
        gp = _gelu_grad(zg, th)
        gvv = gv_ref[...]
        bd = bd_ref[...]
        u, r, xh, vn, wcs, causal, mixed = _gating(gel, gvv, ws_ref, bt_ref[...], bd, mix_s)
        da = dabc[:, :256]
        dm = da * u
        dmb = dm.astype(BF16)
        lane4 = lax.broadcasted_iota(jnp.int32, (CHUNK, 4), 1)
        dbt = jnp.zeros((CHUNK, 4), F32)
        for g in range(4):
            lo, hi = HEAD * g, HEAD * (g + 1)
            dws_ref[g] += jnp.where(causal, _dot(dmb[:, lo:hi], vn[:, lo:hi], NT), 0.0)
            dbt = dbt + jnp.where(lane4 == g, jnp.sum(dm[:, lo:hi], axis=-1, keepdims=True), 0.0)
            dvn_s[:, lo:hi] = _dot(wcs[g], dmb[:, lo:hi], TN)
        dbt_ref[...] += dbt
        dvn = dvn_s[...]
        dgv_ref[...] += jnp.sum(dvn * xh, axis=0, keepdims=True)
        dxh = dvn * gvv
        dvg = r * (dxh - xh * _group_mean(dxh * xh, bd))
        dz_ref[:, :256] = (da * mixed * gp[:, :256]).astype(BF16)
        dz_ref[:, 256:512] = (dvg * gp[:, 256:]).astype(BF16)
        pc = zc[:, 512:768]
        pp = jnp.where(ci > 0, zpp_ref[...], 0.0)
        pooled, cnt, lane = _pool_fwd(pc, pp, ci)
        pwb = pw_ref[...].astype(BF16)
        pooled_b = pooled.astype(BF16)
        mp = _dot(pooled_b, pwb)
        db = dabc[:, 256:512]
        dpsc_ref[...] += jnp.sum(db * mp, axis=0, keepdims=True)
        dmpb = (db * psc_ref[...]).astype(BF16)
        dpw_ref[...] += _dot(pooled_b, dmpb, TN)
        dpooled = _dot(dmpb, pwb, NT)
        davg = dpooled / cnt
        zero = jnp.zeros((CHUNK, 256), F32)
        d2, d4, d8, d16 = [jnp.concatenate([zero, jnp.where((lane >= 64 * k) & (lane < 64 * (k + 1)), davg, 0.0)],
                                           axis=0) for k in range(4)]
        g8 = d8 + d16 + pltpu.roll(d16, 2 * CHUNK - 8, 0)
        g4 = d4 + g8 + pltpu.roll(g8, 2 * CHUNK - 4, 0)
        g2 = d2 + g4 + pltpu.roll(g4, 2 * CHUNK - 2, 0)
        ge = g2 + pltpu.roll(g2, 2 * CHUNK - 1, 0)
        dz_ref[:, 512:768] = (ge[CHUNK:] - dpooled + cpool[...]).astype(BF16)
        cpool[...] = ge[:CHUNK]
        cc = cq_ref[...]
        sc = sq_ref[...]
        qr, kband, vband, valid = _attn_prep(zc, zpkv_ref[...], cc, sc, cp_ref[...], sp_ref[...], ci)
        snkv = snk_ref[...]
        lane8 = lax.broadcasted_iota(jnp.int32, (1, 8), 1)
        qlane = lax.broadcasted_iota(jnp.int32, (1, 4 * CHUNK), 1)
        dsnk = jnp.zeros((1, 8), F32)
        kb = kband.astype(BF16)
        vb = vband.astype(BF16)
        kt = kband.T * SCALE
        dqts = []
        for hk in range(2):
            kh = kb[:, HEAD * hk:HEAD * (hk + 1)]
            q4 = (_stack_heads(qr, 0, hk) * SCALE).astype(BF16)
            do4 = _stack_heads(dabc, 512, hk).astype(BF16)
            p, ps = _group_probs(kh, q4, valid, _sink_row(snkv, hk))
            dp = _dot(vb[:, HEAD * hk:HEAD * (hk + 1)], do4, NT)
            dd = jnp.sum(p * dp, axis=0, keepdims=True)
            dsink = -ps * dd
            for i in range(4):
                part = jnp.sum(jnp.where((qlane >= CHUNK * i) & (qlane < CHUNK * (i + 1)), dsink, 0.0),
                               axis=1, keepdims=True)
                dsnk = dsnk + jnp.where(lane8 == 4 * hk + i, part, 0.0)
            dsb = (p * (dp - dd)).astype(BF16)
            dqts.append(_dot(kt[HEAD * hk:HEAD * (hk + 1), :].astype(BF16), dsb))
            dkv_s[:, HEAD * hk:HEAD * (hk + 1)] = _dot(dsb, q4)
            dkv_s[:, 128 + HEAD * hk:128 + HEAD * (hk + 1)] = _dot(p.astype(BF16), do4)
        dq4 = jnp.concatenate(dqts, axis=0).T
        for hk in range(2):
            for i in range(4):
                h = 4 * hk + i
                dq_s[:, HEAD * h:HEAD * (h + 1)] = dq4[CHUNK * i:CHUNK * (i + 1), HEAD * hk:HEAD * (hk + 1)]
        dsnk_ref[...] += dsnk
        dqr = dq_s[...]
        dz_ref[:, 768:1280] = (dqr * jnp.concatenate([cc] * 4, axis=1)
                               + _rot_half(dqr * jnp.concatenate([sc] * 4, axis=1))).astype(BF16)
        dkv = dkv_s[...]
        dkr = dkv[CHUNK:, :128] + ck[...]
        dz_ref[:, 1280:1408] = (dkr * cc + _rot_half(dkr * sc)).astype(BF16)
        dz_ref[:, 1408:1536] = (dkv[CHUNK:, 128:] + cv[...]).astype(BF16)
        ck[...] = dkv[:CHUNK, :128]
        cv[...] = dkv[:CHUNK, 128:]

    return pl.pallas_call(
        body, name=name, grid=(nb,), in_specs=specs,
        out_specs=[pl.BlockSpec((CHUNK, 1536), lambda i: (cur(i), 0))] + [full(a) for a in acc_shapes],
        out_shape=[jax.ShapeDtypeStruct((s, 1536), BF16)] + [jax.ShapeDtypeStruct(a, F32) for a in acc_shapes],
        scratch_shapes=[pltpu.VMEM((CHUNK, 256), F32), pltpu.VMEM((CHUNK, 128), F32), pltpu.VMEM((CHUNK, 128), F32),
                        pltpu.VMEM((CHUNK, 512), F32), pltpu.VMEM((2 * CHUNK, 256), F32),
                        pltpu.VMEM((CHUNK, 256), F32), pltpu.VMEM((CHUNK, 256), F32)],
        compiler_params=_params(("arbitrary",)),
    )(z, z, z, cosq, sinq, cosq, sinq, gv, ws, bt, pw, psc, snk, _ones_bd(), dabc)


def _xattn_probs(qh, kh):
    s = _dot(qh, kh, NT) * (256 ** -0.5)
    e = jnp.exp(s - jnp.max(s, axis=-1, keepdims=True))
    return e * (1.0 / jnp.sum(e, axis=-1, keepdims=True))


def _xattn_fwd(name, q, kv, tq):
    s, d = q.shape
    mlen = kv.shape[0]

    def body(q_ref, kv_ref, o_ref):
        for h in range(4):
            lo, hi = 256 * h, 256 * (h + 1)
            p = _xattn_probs(q_ref[:, lo:hi], kv_ref[:, lo:hi])
            o_ref[:, lo:hi] = _dot(p.astype(BF16), kv_ref[:, d + lo:d + hi]).astype(BF16)

    blk = pl.BlockSpec((tq, d), lambda i: (i, 0))
    return pl.pallas_call(body, name=name, grid=(s // tq,),
                          in_specs=[blk, pl.BlockSpec((mlen, 2 * d), lambda i: (0, 0))], out_specs=blk,
                          out_shape=jax.ShapeDtypeStruct((s, d), BF16), compiler_params=_params(("parallel",)))(q, kv)


def _xattn_bwd(name, q, kv, do, tq):
    s, d = q.shape
    mlen = kv.shape[0]

    def body(q_ref, kv_ref, do_ref, dq_ref, dkv_ref):
        @pl.when(pl.program_id(0) == 0)
        def _():
            dkv_ref[...] = jnp.zeros_like(dkv_ref)

        for h in range(4):
            lo, hi = 256 * h, 256 * (h + 1)
            qh = q_ref[:, lo:hi]
            kh = kv_ref[:, lo:hi]
            vh = kv_ref[:, d + lo:d + hi]
            doh = do_ref[:, lo:hi]
            p = _xattn_probs(qh, kh)
            dp = _dot(doh, vh, NT)
            dsb = (p * (dp - jnp.sum(p * dp, axis=-1, keepdims=True)) * (256 ** -0.5)).astype(BF16)
            dq_ref[:, lo:hi] = _dot(dsb, kh).astype(BF16)
            dkv_ref[:, lo:hi] += _dot(dsb, qh, TN)
            dkv_ref[:, d + lo:d + hi] += _dot(p.astype(BF16), doh, TN)

    blk = pl.BlockSpec((tq, d), lambda i: (i, 0))
    kvb = pl.BlockSpec((mlen, 2 * d), lambda i: (0, 0))
    return pl.pallas_call(
        body, name=name, grid=(s // tq,), in_specs=[blk, kvb, blk], out_specs=[blk, kvb],
        out_shape=[jax.ShapeDtypeStruct((s, d), BF16), jax.ShapeDtypeStruct((mlen, 2 * d), F32)],
        compiler_params=_params(("arbitrary",)))(q, kv, do)


def _sigmoid(x):
    return 0.5 * (1.0 + jnp.tanh(0.5 * x))


def _ffn_up(name, h, wgu, layer, tm, tn):
    s, d = h.shape
    dff = wgu.shape[2] // 2
    nj = dff // tn

    def body(h_ref, wg_ref, wu_ref, g_ref, u_ref, a_ref):
        hv = h_ref[...]
        gate = _dot(hv, wg_ref[...])
        up = _dot(hv, wu_ref[...])
        g_ref[...] = gate.astype(BF16)
        u_ref[...] = up.astype(BF16)
        a_ref[...] = (gate * _sigmoid(gate) * up).astype(BF16)

    ob = pl.BlockSpec((tm, tn), lambda j, i: (i, j))
    sd = jax.ShapeDtypeStruct((s, dff), BF16)
    return pl.pallas_call(
        body, name=name, grid=(nj, s // tm),
        in_specs=[pl.BlockSpec((tm, d), lambda j, i: (i, 0)),
                  pl.BlockSpec((None, d, tn), lambda j, i: (layer, 0, j)),
                  pl.BlockSpec((None, d, tn), lambda j, i: (layer, 0, j + nj))],
        out_specs=[ob, ob, ob], out_shape=[sd, sd, sd], compiler_params=_params(("parallel", "parallel")),
    )(h, wgu, wgu)


def _ffn_act_bwd(name, dfn, wdown, layer, gate, up, tm):
    s, d = dfn.shape
    dff = gate.shape[1]

    tn = 256

    def body(df_ref, wd_ref, g_ref, u_ref, o_ref):
        df = df_ref[...]
        for lo in range(0, dff, tn):
            dact = _dot(df, wd_ref[lo:lo + tn, :], NT).astype(BF16)
            gate = g_ref[:, lo:lo + tn]
            sig = _sigmoid(gate)
            gs = gate * sig
            o_ref[:, lo:lo + tn] = dact * u_ref[:, lo:lo + tn] * (sig + gs * (1.0 - sig))
            o_ref[:, dff + lo:dff + lo + tn] = dact * gs

    gb = pl.BlockSpec((tm, dff), lambda i: (i, 0))
    return pl.pallas_call(
        body, name=name, grid=(s // tm,),
        in_specs=[pl.BlockSpec((tm, d), lambda i: (i, 0)),
                  pl.BlockSpec((None, dff, d), lambda i: (layer, 0, 0)), gb, gb],
        out_specs=pl.BlockSpec((tm, 2 * dff), lambda i: (i, 0)),
        out_shape=jax.ShapeDtypeStruct((s, 2 * dff), BF16), compiler_params=_params(("parallel",)),
    )(dfn, wdown, gate, up)


def _place():
    return lax.axis_index("x"), lax.axis_index("y"), lax.axis_index("c")


def _other_chips(x, y):
    return [(1 - x, y), (x, 1 - y), (1 - x, 1 - y)]


def _region(ref, axis, chip, size):
    start = pl.multiple_of(chip * size, size)
    if axis == 1:
        return ref.at[:, pl.ds(start, size), :]
    return ref.at[:, :, pl.ds(start, size)]


ANY = pl.BlockSpec(memory_space=pl.ANY)


HBM = pl.BlockSpec(memory_space=pltpu.HBM)
SEM = pl.BlockSpec(memory_space=pltpu.SEMAPHORE)
EFFECT = pltpu.SideEffectType.DATAFLOW_SIDE_EFFECTING


def _in_hbm(a):
    return pltpu.with_memory_space_constraint(a, pltpu.HBM)


def _split_start(name, srcs, lands, ncopies, plan):
    ns, nl = len(srcs), len(lands)

    def body(*refs):
        src, land = refs[:ns], refs[ns:ns + nl]
        send, recv = refs[ns + nl], refs[ns + nl + 1]
        token = refs[-1]
        x, y, c = _place()
        for k, (s_ref, d_ref, peer, _) in enumerate(plan(src, land, x, y, c)):
            pltpu.make_async_remote_copy(src_ref=s_ref, dst_ref=d_ref, send_sem=send.at[k], recv_sem=recv.at[k],
                                         device_id=peer, device_id_type=MESH).start()
        token[...] = jnp.zeros_like(token)

    ops = list(srcs) + list(lands)
    out = pl.pallas_call(
        body, name=name,
        out_shape=(pltpu.SemaphoreType.DMA((ncopies,)), pltpu.SemaphoreType.DMA((ncopies,)),
                   *[pltpu.HBM(a.shape, a.dtype) for a in ops], jax.ShapeDtypeStruct((8, 128), F32)),
        in_specs=(HBM,) * (ns + nl),
        out_specs=(SEM, SEM) + (HBM,) * (ns + nl) + (pl.BlockSpec(memory_space=pltpu.VMEM),),
        input_output_aliases={i: 2 + i for i in range(ns + nl)},
        compiler_params=pltpu.CompilerParams(has_side_effects=EFFECT),
    )(*[_in_hbm(a) for a in ops])
    return out[0], out[1], list(out[2:2 + ns]), list(out[2 + ns:2 + ns + nl]), out[-1]


def _split_start_many(name, lands, jobs):
    nl, nj = len(lands), len(jobs)

    def body(*refs):
        land = refs[:nl]
        sems = refs[nl:nl + 2 * nj]
        token = refs[-1]
        x, y, c = _place()
        for j, (idx, _, plan) in enumerate(jobs):
            for k, (s_ref, d_ref, peer, _) in enumerate(plan((), [land[t] for t in idx], x, y, c)):
                pltpu.make_async_remote_copy(src_ref=s_ref, dst_ref=d_ref, send_sem=sems[2 * j].at[k],
                                             recv_sem=sems[2 * j + 1].at[k], device_id=peer,
                                             device_id_type=MESH).start()
        token[...] = jnp.zeros_like(token)

    sem_shapes = tuple(pltpu.SemaphoreType.DMA((n,)) for _, n, _ in jobs for _ in range(2))
    out = pl.pallas_call(
        body, name=name,
        out_shape=sem_shapes + tuple(pltpu.HBM(a.shape, a.dtype) for a in lands)
        + (jax.ShapeDtypeStruct((8, 128), F32),),
        in_specs=(HBM,) * nl,
        out_specs=(SEM,) * (2 * nj) + (HBM,) * nl + (pl.BlockSpec(memory_space=pltpu.VMEM),),
        input_output_aliases={i: 2 * nj + i for i in range(nl)},
        compiler_params=pltpu.CompilerParams(has_side_effects=EFFECT),
    )(*[_in_hbm(a) for a in lands])
    return [(out[2 * j], out[2 * j + 1]) for j in range(nj)], list(out[2 * nj:2 * nj + nl]), out[-1]


def _split_wait(name, send, recv, srcs, lands, after, plan):
    ns, nl = len(srcs), len(lands)

    def body(*refs):
        src, land = refs[:ns], refs[ns:ns + nl]
        send_ref, recv_ref = refs[ns + nl], refs[ns + nl + 1]
        x, y, c = _place()
        for k, (s_ref, _, _, got) in enumerate(plan(src, land, x, y, c)):
            cp = pltpu.make_async_remote_copy(src_ref=s_ref, dst_ref=got, send_sem=send_ref.at[k],
                                              recv_sem=recv_ref.at[k], device_id=(x, y, c), device_id_type=MESH)
            cp.wait_send()
            cp.wait_recv()

    ops = list(srcs) + list(lands)
    out = pl.pallas_call(
        body, name=name, out_shape=tuple(pltpu.HBM(a.shape, a.dtype) for a in ops),
        in_specs=(HBM,) * (ns + nl) + (SEM, SEM, ANY), out_specs=(HBM,) * (ns + nl),
        input_output_aliases={i: i for i in range(ns + nl)},
        compiler_params=pltpu.CompilerParams(has_side_effects=EFFECT),
    )(*ops, send, recv, after)
    return list(out[:ns]), list(out[ns:])


def _split_wait_start(name, send, recv, lands, after, wait_plan, ncopies, start_plan, carried=()):
    nl, nc = len(lands), len(carried)
    lands = list(lands) + list(carried)

    def body(*refs):
        land = refs[:nl]
        send_in, recv_in = refs[nl + nc], refs[nl + nc + 1]
        send_out, recv_out = refs[nl + nc + 3], refs[nl + nc + 4]
        x, y, c = _place()
        for k, (s_ref, _, _, got) in enumerate(wait_plan((), land, x, y, c)):
            cp = pltpu.make_async_remote_copy(src_ref=s_ref, dst_ref=got, send_sem=send_in.at[k],
                                              recv_sem=recv_in.at[k], device_id=(x, y, c), device_id_type=MESH)
            cp.wait_send()
            cp.wait_recv()
        for k, (s_ref, d_ref, peer, _) in enumerate(start_plan((), land, x, y, c)):
            pltpu.make_async_remote_copy(src_ref=s_ref, dst_ref=d_ref, send_sem=send_out.at[k],
                                         recv_sem=recv_out.at[k], device_id=peer, device_id_type=MESH).start()

    out = pl.pallas_call(
        body, name=name,
        out_shape=(pltpu.SemaphoreType.DMA((ncopies,)), pltpu.SemaphoreType.DMA((ncopies,)),
                   *[pltpu.HBM(a.shape, a.dtype) for a in lands]),
        in_specs=(HBM,) * (nl + nc) + (SEM, SEM, ANY), out_specs=(SEM, SEM) + (HBM,) * (nl + nc),
        input_output_aliases={i: 2 + i for i in range(nl + nc)},
        compiler_params=pltpu.CompilerParams(has_side_effects=EFFECT),
    )(*lands, send, recv, after)
    return out[0], out[1], list(out[2:2 + nl]), list(out[2 + nl:])


def _half(ref, axis, chip, size, layer, h):
    reg = _region(ref, axis, chip, size).at[pl.ds(layer, 1)]
    rows = reg.shape[1] // 2
    return reg.at[:, pl.ds(pl.multiple_of(h * rows, rows), rows), :]


def _gather_plan(axes, sizes, layer):
    def plan(src, land, x, y, c):
        me = 2 * x + y
        out = []
        for t in range(len(land)):
            mine = _half(land[t], axes[t], me, sizes[t], layer, c)
            for px, py in _other_chips(x, y):
                out.append((mine, mine, (px, py, c), _half(land[t], axes[t], 2 * px + py, sizes[t], layer, c)))
        return out
    return plan


def _forward_plan(axes, sizes, layer):
    def plan(src, land, x, y, c):
        out = []
        for t in range(len(land)):
            for px, py in _other_chips(x, y):
                got = _half(land[t], axes[t], 2 * px + py, sizes[t], layer, c)
                out.append((got, got, (x, y, 1 - c), _half(land[t], axes[t], 2 * px + py, sizes[t], layer, 1 - c)))
        return out
    return plan


def _place_own(name, w, axis, chip):
    nl, r, cs = w.shape
    tr = max(t for t in range(16, r + 1, 16) if r % t == 0 and 4 * cs * t <= PLACE_BLOCK_BYTES)
    nb = r // tr
    steps = nl * nb
    full = (nl, 4 * r, cs) if axis == 1 else (nl, r, 4 * cs)

    def body(m_ref, w_ref, o_ref, inb, outb, isem, osem):
        me = m_ref[0]

        def read(s, k):
            return pltpu.make_async_copy(w_ref.at[s // nb, pl.ds((s % nb) * tr, tr), :], inb.at[k], isem.at[k])

        def write(s, k):
            if axis == 1:
                dst = o_ref.at[s // nb, pl.ds(me * r + (s % nb) * tr, tr), :]
            else:
                dst = o_ref.at[s // nb, pl.ds((s % nb) * tr, tr), pl.ds(pl.multiple_of(me * cs, cs), cs)]
            return pltpu.make_async_copy(outb.at[k], dst, osem.at[k])

        for s in range(min(PLACE_BUFFERS, steps)):
            read(s, s).start()

        def step(s, carry):
            k = s % PLACE_BUFFERS
            read(s, k).wait()

            @pl.when(s >= PLACE_BUFFERS)
            def _():
                write(s - PLACE_BUFFERS, k).wait()

            outb[k] = inb[k].astype(BF16)
            write(s, k).start()

            @pl.when(s + PLACE_BUFFERS < steps)
            def _():
                read(s + PLACE_BUFFERS, k).start()

            return carry

        lax.fori_loop(0, steps, step, 0)
        for s in range(max(0, steps - PLACE_BUFFERS), steps):
            write(s, s % PLACE_BUFFERS).wait()

    return pl.pallas_call(
        body, name=name,
        in_specs=[pl.BlockSpec(memory_space=pltpu.SMEM), ANY], out_specs=ANY,
        out_shape=jax.ShapeDtypeStruct(full, BF16),
        scratch_shapes=[pltpu.VMEM((PLACE_BUFFERS, tr, cs), F32), pltpu.VMEM((PLACE_BUFFERS, tr, cs), BF16),
                        pltpu.SemaphoreType.DMA((PLACE_BUFFERS,)), pltpu.SemaphoreType.DMA((PLACE_BUFFERS,))],
        compiler_params=pltpu.CompilerParams(vmem_limit_bytes=VMEM_LIMIT),
    )(chip, w)


def _scatter_plan(axes, sizes):
    def plan(src, land, x, y, c):
        out = []
        for t in range(len(src)):
            for k, (px, py) in enumerate(_other_chips(x, y)):
                out.append((_region(src[t], axes[t], 2 * px + py, sizes[t]).at[0], land[t].at[k], (px, py, c),
                            land[t].at[k]))
        return out
    return plan


def _pair_plan(src, land, x, y, c):
    return [(src[t], land[t], (x, y, 1 - c), land[t]) for t in range(len(src))]


UNIT_STEPS = 16


def _unit_rows(r):
    return min(t for t in range(16, r + 1, 16) if r % t == 0 and r // t <= UNIT_STEPS)


def _guarded(i, nb, steps, work):
    if nb == steps:
        work()
    else:
        pl.when(i < nb)(work)


def _unit_chip_sum(name, gs, slots, axes, chip):
    n = len(gs)
    dims = [s.shape[1:] for s in slots]
    trs = [_unit_rows(r) for r, _ in dims]
    nbs = [r // tr for (r, _), tr in zip(dims, trs)]
    steps = max(nbs)

    def body(m_ref, *refs):
        i = pl.program_id(0)
        for t in range(n):
            def work(t=t):
                acc = refs[t][...].astype(F32)
                for k in range(3):
                    acc = acc + refs[n + t][k].astype(F32)
                refs[2 * n + t][...] = acc.astype(BF16)
            _guarded(i, nbs[t], steps, work)

    gspecs, sspecs, ospecs = [], [], []
    for (r, cs), tr, nb, axis in zip(dims, trs, nbs, axes):
        if axis == 1:
            gspecs.append(pl.BlockSpec((tr, cs), lambda i, m, nb=nb: (m[0] * nb + jnp.minimum(i, nb - 1), 0)))
        else:
            gspecs.append(pl.BlockSpec((tr, cs), lambda i, m, nb=nb: (jnp.minimum(i, nb - 1), m[0])))
        sspecs.append(pl.BlockSpec((3, tr, cs), lambda i, m, nb=nb: (0, jnp.minimum(i, nb - 1), 0)))
        ospecs.append(pl.BlockSpec((tr, cs), lambda i, m, nb=nb: (jnp.minimum(i, nb - 1), 0)))
    return pl.pallas_call(
        body, name=name,
        grid_spec=pltpu.PrefetchScalarGridSpec(num_scalar_prefetch=1, grid=(steps,), in_specs=gspecs + sspecs,
                                               out_specs=ospecs),
        out_shape=[jax.ShapeDtypeStruct(d, BF16) for d in dims], compiler_params=_params(("arbitrary",)),
    )(chip, *gs, *slots)


def _unit_adamw(name, mine, theirs, ws, ms, vs, layer, bufs):
    n = len(mine)
    dims = [a.shape for a in mine]
    trs = [_unit_rows(r) for r, _ in dims]
    nbs = [r // tr for (r, _), tr in zip(dims, trs)]
    steps = max(nbs)
    c1 = 1.0 - B1 ** STEP
    c2 = 1.0 - B2 ** STEP

    def body(*refs):
        i = pl.program_id(0)
        outs = refs[9 * n:]
        for t in range(n):
            def work(t=t):
                a_ref, b_ref, w_ref, m_ref, v_ref = refs[5 * t:5 * t + 5]
                g_ref, d_ref, mo_ref, vo_ref = outs[4 * t:4 * t + 4]
                gv = a_ref[...].astype(F32) + b_ref[...].astype(F32)
                mn = B1 * m_ref[...] + (1.0 - B1) * gv
                vn = B2 * v_ref[...] + (1.0 - B2) * (gv * gv)
                g_ref[...] = gv
                mo_ref[...] = mn
                vo_ref[...] = vn
                d_ref[...] = -LR * ((mn / c1) / (jnp.sqrt(vn / c2) + ADAM_EPS) + WD * w_ref[...])
            _guarded(i, nbs[t], steps, work)

    in_specs, out_specs, ops = [], [], []
    for t, ((r, cs), tr, nb) in enumerate(zip(dims, trs, nbs)):
        blk = pl.BlockSpec((tr, cs), lambda i, nb=nb: (jnp.minimum(i, nb - 1), 0))
        lay = pl.BlockSpec((None, tr, cs), lambda i, nb=nb: (layer, jnp.minimum(i, nb - 1), 0))
        in_specs += [blk, blk, lay, lay, lay]
        out_specs += [lay] * 4
        ops += [mine[t], theirs[t], ws[t], ms[t], vs[t]]
    flat = [b for bs in bufs for b in bs]
    out = pl.pallas_call(
        body, name=name, grid=(steps,), in_specs=in_specs + [ANY] * (4 * n), out_specs=out_specs,
        out_shape=[jax.ShapeDtypeStruct(b.shape, b.dtype) for b in flat],
        input_output_aliases={5 * n + k: k for k in range(4 * n)}, compiler_params=_params(("arbitrary",)),
    )(*ops, *flat)
    return [list(out[4 * t:4 * t + 4]) for t in range(n)]


def _small_pair_sum(p):
    rows = p.shape[0]
    half = rows // 2

    def body(p_ref, o_ref, sib, send, recv):
        x, y, c = _place()
        mine = pl.ds(pl.multiple_of(c * half, half), half)
        theirs = pl.ds(pl.multiple_of((1 - c) * half, half), half)
        pair = pltpu.make_async_remote_copy(src_ref=p_ref.at[theirs], dst_ref=sib, send_sem=send, recv_sem=recv,
                                            device_id=(x, y, 1 - c), device_id_type=MESH)
        pair.start()
        pair.wait()
        o_ref[...] = (p_ref[mine] + sib[...]).astype(BF16)

    vm = pl.BlockSpec(memory_space=pltpu.VMEM)
    return pl.pallas_call(
        body, name="small_pair_sum", in_specs=[vm], out_specs=vm, out_shape=jax.ShapeDtypeStruct((half, 128), BF16),
        scratch_shapes=[pltpu.VMEM((half, 128), F32), pltpu.SemaphoreType.DMA, pltpu.SemaphoreType.DMA],
        compiler_params=pltpu.CompilerParams(vmem_limit_bytes=VMEM_LIMIT),
    )(p)


def _small_plan(src, land, x, y, c):
    return [(src[0], land[0].at[k], (px, py, c), land[0].at[k]) for k, (px, py) in enumerate(_other_chips(x, y))]


def _small_total(own, slots):
    half = own.shape[0]

    def body(own_ref, slots_ref, o_ref, sums, send, recv):
        x, y, c = _place()
        me = 2 * x + y
        mine = pl.ds(pl.multiple_of(c * half, half), half)
        theirs = pl.ds(pl.multiple_of((1 - c) * half, half), half)
        sums[me] = own_ref[...]
        for k, (px, py) in enumerate(_other_chips(x, y)):
            sums[2 * px + py] = slots_ref[k]
        acc = sums[0].astype(F32)
        for k in range(1, 4):
            acc = acc + sums[k].astype(F32)
        o_ref[mine] = acc
        back = pltpu.make_async_remote_copy(src_ref=o_ref.at[mine], dst_ref=o_ref.at[mine], send_sem=send,
                                            recv_sem=recv, device_id=(x, y, 1 - c), device_id_type=MESH)
        back.start()
        pltpu.make_async_remote_copy(src_ref=o_ref.at[theirs], dst_ref=o_ref.at[theirs], send_sem=send,
                                     recv_sem=recv, device_id=(x, y, c), device_id_type=MESH).wait_recv()
        back.wait_send()

    vm = pl.BlockSpec(memory_space=pltpu.VMEM)
    return pl.pallas_call(
        body, name="small_total", in_specs=[vm, vm], out_specs=vm,
        out_shape=jax.ShapeDtypeStruct((2 * half, 128), F32),
        scratch_shapes=[pltpu.VMEM((4, half, 128), BF16), pltpu.SemaphoreType.DMA, pltpu.SemaphoreType.DMA],
        compiler_params=pltpu.CompilerParams(vmem_limit_bytes=VMEM_LIMIT),
    )(own, slots)


def _small_adamw(gs, ws, ms, vs):
    n = len(gs)
    c1 = 1.0 - B1 ** STEP
    c2 = 1.0 - B2 ** STEP

    def body(*refs):
        for t in range(n):
            gv, wv = refs[t][...], refs[n + t][...]
            d_ref, mo_ref, vo_ref = refs[4 * n + 3 * t:4 * n + 3 * t + 3]
            mn = B1 * refs[2 * n + t][...] + (1.0 - B1) * gv
            vn = B2 * refs[3 * n + t][...] + (1.0 - B2) * (gv * gv)
            mo_ref[...] = mn
            vo_ref[...] = vn
            d_ref[...] = -LR * ((mn / c1) / (jnp.sqrt(vn / c2) + ADAM_EPS) + WD * wv)

    vm = pl.BlockSpec(memory_space=pltpu.VMEM)
    out = pl.pallas_call(
        body, name="adamw_small", in_specs=[vm] * (4 * n), out_specs=[vm] * (3 * n),
        out_shape=[jax.ShapeDtypeStruct(w.shape, F32) for w in ws for _ in range(3)],
        compiler_params=pltpu.CompilerParams(vmem_limit_bytes=VMEM_LIMIT),
    )(*gs, *ws, *ms, *vs)
    return [tuple(out[3 * t:3 * t + 3]) for t in range(n)]


def _pack(parts):
    flat = []
    for p in parts:
        v = p.reshape(-1).astype(F32)
        flat.append(jnp.pad(v, (0, (-v.shape[0]) % 128)))
    v = jnp.concatenate(flat)
    v = jnp.pad(v, (0, (-v.shape[0]) % (512 * 128)))
    return v.reshape(-1, 128)


def _unpack(buf, like):
    out, r0 = [], 0
    for p in like:
        nelem = 1
        for s in p.shape:
            nelem *= s
        rows = -(-nelem // 128)
        blk = buf[r0:r0 + rows]
        if nelem % 128:
            blk = blk.reshape(-1)[:nelem]
        out.append(blk.reshape(p.shape))
        r0 += rows
    return out


def kernel(x, mem, positions, mem_norm_g, mix_pre_g, mix_post_g, w_in, gm_v_g, gm_w_s, gm_b_s, pool_w, pool_scale, attn_sinks, w_o, x_pre_g, x_post_g, w_xq, w_xkv, w_xo, ffn_pre_g, ffn_post_g, w_gate_up, w_down, loss_target, m_mem_norm_g, m_mix_pre_g, m_mix_post_g, m_w_in, m_gm_v_g, m_gm_w_s, m_gm_b_s, m_pool_w, m_pool_scale, m_attn_sinks, m_w_o, m_x_pre_g, m_x_post_g, m_w_xq, m_w_xkv, m_w_xo, m_ffn_pre_g, m_ffn_post_g, m_w_gate_up, m_w_down, v_mem_norm_g, v_mix_pre_g, v_mix_post_g, v_w_in, v_gm_v_g, v_gm_w_s, v_gm_b_s, v_pool_w, v_pool_scale, v_attn_sinks, v_w_o, v_x_pre_g, v_x_post_g, v_w_xq, v_w_xkv, v_w_xo, v_ffn_pre_g, v_ffn_post_g, v_w_gate_up, v_w_down):
    args = (x, mem, positions, mem_norm_g, mix_pre_g, mix_post_g, w_in, gm_v_g, gm_w_s, gm_b_s, pool_w, pool_scale, attn_sinks, w_o, x_pre_g, x_post_g, w_xq, w_xkv, w_xo, ffn_pre_g, ffn_post_g, w_gate_up, w_down)
    moms_m = (m_mem_norm_g, m_mix_pre_g, m_mix_post_g, m_w_in, m_gm_v_g, m_gm_w_s, m_gm_b_s, m_pool_w, m_pool_scale, m_attn_sinks, m_w_o, m_x_pre_g, m_x_post_g, m_w_xq, m_w_xkv, m_w_xo, m_ffn_pre_g, m_ffn_post_g, m_w_gate_up, m_w_down)
    moms_v = (v_mem_norm_g, v_mix_pre_g, v_mix_post_g, v_w_in, v_gm_v_g, v_gm_w_s, v_gm_b_s, v_pool_w, v_pool_scale, v_attn_sinks, v_w_o, v_x_pre_g, v_x_post_g, v_w_xq, v_w_xkv, v_w_xo, v_ffn_pre_g, v_ffn_post_g, v_w_gate_up, v_w_down)
    P = dict(zip(NAMES, args))
    P['loss_target'] = loss_target
    M = dict(zip(WEIGHTS, moms_m))
    V = dict(zip(WEIGHTS, moms_v))
    depth = w_in.shape[0]
    nbig = len(BIG)
    axes = [BIG_AXIS[n] for n in BIG]
    sizes = [P[n].shape[a] for n, a in zip(BIG, axes)]
    chip = (2 * lax.axis_index("x") + lax.axis_index("y")).astype(jnp.int32).reshape(1)

    groups = [['w_in'], ['w_o', 'w_xq', 'w_xkv', 'w_xo'], ['w_gate_up', 'w_down']]
    units = [(l, g) for l in range(depth) for g in groups]
    unit_of = {(l, n): i for i, (l, names) in enumerate(units) for n in names}
    ax = lambda names: [BIG_AXIS[n] for n in names]
    sz = lambda names: [P[n].shape[BIG_AXIS[n]] for n in names]

    full = {n: _place_own("place_" + n, P[n], BIG_AXIS[n], chip) for n in BIG}
    gathers, land, tok = _split_start_many(
        "gather_start", [full[n] for n in BIG],
        [([BIG.index(n) for n in names], 3 * len(names), _gather_plan(ax(names), sz(names), l)) for l, names in units])
    full.update(zip(BIG, land))
    P['first_dep'] = tok[:1, :1]
    forwards, gathered = {}, set()

    def forward_unit(i, after, carried=()):
        ul, unames = units[i]
        send, recv = gathers[i]
        send, recv, land, thru = _split_wait_start(
            "gather_pass%d" % i, send, recv, [full[n] for n in unames], after,
            _gather_plan(ax(unames), sz(unames), ul), 3 * len(unames), _forward_plan(ax(unames), sz(unames), ul),
            carried=[full[n] for n in carried])
        full.update(zip(unames, land))
        full.update(zip(carried, thru))
        forwards[i] = (send, recv)

    def weights_of(l, names, after):
        i = unit_of[(l, names[0])]
        if i not in gathered:
            _, unames = units[i]
            if i not in forwards:
                forward_unit(i, after)
            send, recv = forwards.pop(i)
            _, land = _split_wait("gather_wait%d" % i, send, recv, [], [full[n] for n in unames], after,
                                  _forward_plan(ax(unames), sz(unames), l))
            full.update(zip(unames, land))
            gathered.add(i)
            if len(groups) <= i + 1 < len(units):
                forward_unit(i + 1, after, carried=[n for n in unames if n not in units[i + 1][1]])
        return {n: (full[n], l) for n in names}

    outs = {n: [lax.empty(P[n].shape, F32) for _ in range(4)] for n in BIG}
    gunits = [(l, BIG) for l in range(depth - 1, 0, -1)] + [
        (0, g) for g in (['w_gate_up', 'w_down'], ['w_xq', 'w_xkv', 'w_xo'], ['w_o'], ['w_in'])]
    collected, scatters, pairs = {}, {}, {}

    def finish_scatter(i, after):
        _, names = gunits[i]
        send, recv, g_l, slots = scatters.pop(i)
        g_l, slots = _split_wait("scatter_wait%d" % i, send, recv, g_l, slots, after,
                                 _scatter_plan(ax(names), sz(names)))
        mine = _unit_chip_sum("chip_sum", [g.reshape(g.shape[1:]) for g in g_l], slots, ax(names), chip)
        send, recv, mine, theirs, tok = _split_start("pair_start%d" % i, mine, [lax.empty(a.shape, BF16) for a in mine],
                                                     len(names), _pair_plan)
        pairs[i] = (send, recv, mine, theirs)
        return tok[:1, :1]

    def finish_pair(i, after):
        l, names = gunits[i]
        send, recv, mine, theirs = pairs.pop(i)
        mine, theirs = _split_wait("pair_wait%d" % i, send, recv, mine, theirs, after, _pair_plan)
        new = _unit_adamw("adamw", mine, theirs, [P[n] for n in names], [M[n] for n in names], [V[n] for n in names],
                          l, [outs[n] for n in names])
        outs.update(zip(names, new))

    calls = {'n': 0}
    lag = 4

    def grads_of(l, g_part, after):
        collected.update({(l, n): g for n, g in g_part.items()})
        calls['n'] += 1
        now = calls['n']
        tok = jnp.zeros((1, 1), F32)
        for i, (ul, names) in enumerate(gunits):
            if ul != l or ('started', i) in collected or any((l, n) not in collected for n in names):
                continue
            collected[('started', i)] = now
            srcs = [collected[(l, n)].reshape((1,) + collected[(l, n)].shape) for n in names]
            send, recv, srcs, slots, t = _split_start("scatter_start%d" % i, srcs,
                                                      [lax.empty((3,) + P[n].shape[1:], BF16) for n in names],
                                                      3 * len(names), _scatter_plan(ax(names), sz(names)))
            scatters[i] = (send, recv, srcs, slots)
            tok = tok + t[:1, :1]
        for i in sorted(pairs):
            if collected[('summed', i)] + lag <= now:
                finish_pair(i, after)
        for i in sorted(scatters):
            if collected[('started', i)] + lag <= now:
                tok = tok + finish_scatter(i, after)
                collected[('summed', i)] = now
        return tok

    loss_part, dx, small_g = _fwd_bwd(P, weights_of, grads_of)
    loss = lax.psum(loss_part[0, 0], ("x", "y", "c"))
    grad_x = dx.reshape(x.shape)

    small_like = [P[n] for n in SMALL]
    half_sum = _small_pair_sum(_pack(small_g))
    send, recv, (half_sum,), (slots,), _ = _split_start(
        "small_start", [half_sum], [lax.empty((3,) + half_sum.shape, BF16)], 3, _small_plan)

    for i in sorted(pairs):
        finish_pair(i, half_sum)
    for i in sorted(scatters):
        finish_scatter(i, half_sum)
    for i in sorted(pairs):
        finish_pair(i, half_sum)
    grads, deltas, new_m, new_v = {}, {}, {}, {}
    for n in BIG:
        grads[n], deltas[n], new_m[n], new_v[n] = outs[n]

    (half_sum,), (slots,) = _split_wait("small_wait", send, recv, [half_sum], [slots], grads[BIG[0]], _small_plan)
    two_d = lambda a: a.reshape(-1, a.shape[-1])
    gs = _unpack(_small_total(half_sum, slots), small_like)
    upd = _small_adamw([two_d(g) for g in gs], [two_d(P[n]) for n in SMALL], [two_d(M[n]) for n in SMALL],
                       [two_d(V[n]) for n in SMALL])
    for n, g, (dlt, mn, vn) in zip(SMALL, gs, upd):
        shape = P[n].shape
        grads[n], deltas[n], new_m[n], new_v[n] = g, dlt.reshape(shape), mn.reshape(shape), vn.reshape(shape)

    return (loss, grad_x, *[grads[n] for n in WEIGHTS], *[deltas[n] for n in WEIGHTS],
            *[new_m[n] for n in WEIGHTS], *[new_v[n] for n in WEIGHTS])


def _fwd_bwd(P, weights_of, grads_of):
    (x, mem, positions, mem_norm_g, mix_pre_g, mix_post_g, w_in, gm_v_g, gm_w_s, gm_b_s, pool_w, pool_scale, attn_sinks,
     w_o, x_pre_g, x_post_g, w_xq, w_xkv, w_xo, ffn_pre_g, ffn_post_g, w_gate_up, w_down) = [P[n] for n in NAMES]
    x0 = x[0]
    s, d = x0.shape
    depth = w_in.shape[0]
    tgt = P['loss_target'][0]
    tmn = 256
    tmr = min(512, s)
    tmp = min(1024, s)
    tkw = min(2048, s)

    half = HEAD // 2
    inv = ROPE_THETA ** (-jnp.arange(half, dtype=F32) / half)
    ang = positions[0].astype(F32)[:, None] * inv
    cos, sin = jnp.cos(ang), jnp.sin(ang)
    cosq = jnp.tile(jnp.concatenate([cos, cos], axis=-1), (1, 2))
    sinq = jnp.tile(jnp.concatenate([-sin, sin], axis=-1), (1, 2))

    row = lambda a, l: a[l].reshape(1, -1)
    memn = _prenorm("mem_norm", mem[0], mem_norm_g.reshape(1, d), tmn)
    pw_all = jnp.where(_ones_bd() > 0, jnp.tile(pool_w.reshape(depth, 4 * HEAD, HEAD), (1, 1, 4)), 0.0)
    pw_bd = [pw_all[l] for l in range(depth)]

    saved = []
    xc = x0
    h = _prenorm("pre_norm0", x0, row(mix_pre_g, 0) + P['first_dep'], tmn)
    for l in range(depth):
        W = weights_of(l, ['w_in'], xc)
        sv = {'x0': xc, 'h1': h}
        z, = _mm_rows("fwd_w_in", h, *W['w_in'], 'nn', tm=tmp, rows_out=[F32], epilogue=_plain_rows)
        abc = _mixer_fwd("mixer_fwd", z, cosq, sinq, row(gm_v_g, l), gm_w_s[l], gm_b_s[l].T, pw_bd[l],
                         row(pool_scale, l), row(attn_sinks, l))
        W.update(weights_of(l, ['w_o'], z))
        mix, xc, h = _mm_rows("fwd_w_o", abc, *W['w_o'], 'nn', tm=tmr, rows_in=[xc],
                              params=[row(mix_post_g, l), row(x_pre_g, l)], rows_out=[BF16, F32, BF16],
                              epilogue=_post_pre_rows)
        sv.update(z=z, abc=abc, mix=mix, x1=xc, h2=h)
        W.update(weights_of(l, ['w_xq', 'w_xkv', 'w_xo'], xc))
        q, = _mm_rows("fwd_w_xq", h, *W['w_xq'], 'nn', tm=tmp, rows_out=[BF16], epilogue=_plain_rows)
        kv = _mm_nn("fwd_w_xkv", memn, *W['w_xkv'], tm=256, tn=512, tk=d, out_dtype=BF16)
        o = _xattn_fwd("xattn_fwd", q, kv, 512)
        xo, xc, h = _mm_rows("fwd_w_xo", o, *W['w_xo'], 'nn', tm=tmr, rows_in=[xc],
                             params=[row(x_post_g, l), row(ffn_pre_g, l)], rows_out=[BF16, F32, BF16],
                             epilogue=_post_pre_rows)
        sv.update(q=q, kv=kv, o=o, xo=xo, x2=xc, h3=h)
        W.update(weights_of(l, ['w_gate_up', 'w_down'], xc))
        dff = W['w_down'][0].shape[1]
        gate, up, act = _ffn_up("ffn_up", h, *W['w_gate_up'], 512, dff // 2)
        sv.update(gate=gate, up=up, act=act)
        if l + 1 < depth:
            f, xc, h = _mm_rows("fwd_w_down", act, *W['w_down'], 'nn', tm=tmr, rows_in=[xc],
                                params=[row(ffn_post_g, l), row(mix_pre_g, l + 1)], rows_out=[BF16, F32, BF16],
                                epilogue=_post_pre_rows)
            sv.update(f=f)
        saved.append(sv)
    gs = {n: [None] * depth for n in SMALL if n != 'mem_norm_g'}
    dx, dfn, gs['ffn_post_g'][depth - 1], loss_part = _mm_rows(
        "fwd_w_down_loss", saved[-1]['act'], *W['w_down'], 'nn', tm=tmr, rows_in=[xc, tgt],
        params=[row(ffn_post_g, depth - 1)], rows_out=[F32, BF16], n_sums=2, epilogue=_make_loss_rows(d))

    dmemn = None
    tok = jnp.zeros((1, 1), F32)
    for l in reversed(range(depth)):
        sv, W, G = saved[l], weights_of(l, BIG, dx), {}
        G['w_down'] = _mm_tn("dw_down", sv['act'], dfn, tm=dff // 2, tn=d, tk=tkw)
        dgu = _ffn_act_bwd("ffn_act_bwd", dfn, *W['w_down'], sv['gate'], sv['up'], 256)
        G['w_gate_up'] = _mm_tn("dw_gate_up", sv['h3'], dgu, tm=d, tn=dff // 2, tk=tkw)
        dx, dxo, gs['ffn_pre_g'][l], gs['x_post_g'][l] = _mm_rows(
            "bwd_w_gate_up", dgu, *W['w_gate_up'], 'nt', tm=tmr, rows_in=[sv['x2'], dx, sv['xo']],
            params=[row(ffn_pre_g, l) + tok, row(x_post_g, l)], rows_out=[F32, BF16], n_sums=2, epilogue=_bwd_rows)
        tok = grads_of(l, {n: G[n] for n in ('w_gate_up', 'w_down')}, dx)
        G['w_xo'] = _mm_tn("dw_xo", sv['o'], dxo, tm=d, tn=d, tk=tkw)
        do, = _mm_rows("bwd_w_xo", dxo, *W['w_xo'], 'nt', tm=tmp, params=[jnp.zeros((1, d), F32) + tok],
                       rows_out=[BF16], epilogue=_plain_rows)
        dq, dkv = _xattn_bwd("xattn_bwd", sv['q'], sv['kv'], do, 512)
        dkv = dkv.astype(BF16)
        G['w_xkv'] = _mm_tn("dw_xkv", memn, dkv, tm=d, tn=d, tk=mem.shape[1])
        dmemn = _mm_nt("bwd_w_xkv", dkv, *W['w_xkv'], tm=mem.shape[1], tn=512, tk=2 * d, out_dtype=F32, add=dmemn)
        G['w_xq'] = _mm_tn("dw_xq", sv['h2'], dq, tm=d, tn=d, tk=tkw)
        dx, dmix, gs['x_pre_g'][l], gs['mix_post_g'][l] = _mm_rows(
            "bwd_w_xq", dq, *W['w_xq'], 'nt', tm=tmr, rows_in=[sv['x1'], dx, sv['mix']],
            params=[row(x_pre_g, l), row(mix_post_g, l)], rows_out=[F32, BF16], n_sums=2, epilogue=_bwd_rows)
        tok = grads_of(l, {n: G[n] for n in ('w_xq', 'w_xkv', 'w_xo')}, dx)
        G['w_o'] = _mm_tn("dw_o", sv['abc'], dmix, tm=d, tn=d, tk=tkw)
        dabc, = _mm_rows("bwd_w_o", dmix, *W['w_o'], 'nt', tm=tmp, params=[jnp.zeros((1, d), F32) + tok],
                         rows_out=[F32], epilogue=_plain_rows)
        tok = grads_of(l, {'w_o': G['w_o']}, dabc)
        dz, dgv, dws, dbt, dpw, dpsc, dsnk = _mixer_bwd(
            "mixer_bwd", sv['z'], dabc, cosq, sinq, row(gm_v_g, l) + tok, gm_w_s[l], gm_b_s[l].T, pw_bd[l],
            row(pool_scale, l), row(attn_sinks, l))
        gs['gm_v_g'][l] = dgv
        gs['gm_w_s'][l] = dws
        gs['gm_b_s'][l] = dbt.T
        gs['pool_w'][l] = dpw
        gs['pool_scale'][l] = dpsc
        gs['attn_sinks'][l] = dsnk
        G['w_in'] = _mm_tn("dw_in", sv['h1'], dz, tm=d, tn=dz.shape[1], tk=tkw)
        if l > 0:
            dx, dfn, gs['mix_pre_g'][l], gs['ffn_post_g'][l - 1] = _mm_rows(
                "bwd_w_in", dz, *W['w_in'], 'nt', tm=tmr, rows_in=[sv['x0'], dx, saved[l - 1]['f']],
                params=[row(mix_pre_g, l), row(ffn_post_g, l - 1)], rows_out=[F32, BF16], n_sums=2,
                epilogue=_bwd_rows)
        else:
            dx, gs['mix_pre_g'][l] = _mm_rows(
                "bwd_w_in_first", dz, *W['w_in'], 'nt', tm=tmr, rows_in=[sv['x0'], dx],
                params=[row(mix_pre_g, l)], rows_out=[F32], n_sums=1, epilogue=_bwd_rows_first)
        tok = grads_of(l, {'w_in': G['w_in']}, dx)
    _, dg_mem = _norm_bwd("bwd_mem_norm", mem[0], mem_norm_g.reshape(1, d) + tok, dmemn, None, BF16, tmn)
    small_g = []
    for n in SMALL:
        if n == 'mem_norm_g':
            small_g.append(dg_mem.reshape(P[n].shape))
        elif n == 'pool_w':
            blocks = jnp.stack(gs[n]).reshape(depth, 4, HEAD, 4, HEAD)
            same = jnp.eye(4, dtype=jnp.bool_)[None, :, None, :, None]
            small_g.append(jnp.sum(jnp.where(same, blocks, 0.0), axis=3))
        else:
            small_g.append(jnp.stack([a.reshape(P[n].shape[1:]) for a in gs[n]]))
    return loss_part, dx, small_g
```
